```python
import math
import jax, jax.numpy as jnp
from jax import lax
import numpy as np


D_MODEL = 1024
BATCH = 8
SEQ = 2048
DEPTH = 2

GRID_W = 64
CTX_LEN = 256

D_MIX = D_MODEL
A_WIDTH = D_MODEL // 4
A_GROUPS = 4
A_GD = A_WIDTH // A_GROUPS
CHUNK = 128
B_WIDTH = D_MODEL // 4
CONV_W = 3
C_WIDTH = D_MODEL // 2
ATT_HEADS = 4
V_DIM = C_WIDTH // ATT_HEADS
QK_DIM = V_DIM // 2
ROPE_BASE = 10000.0
Q_BLOCK = 128

OFF_AU = 0
OFF_AV = OFF_AU + A_WIDTH
OFF_BB = OFF_AV + A_WIDTH
OFF_BC = OFF_BB + B_WIDTH
OFF_BX = OFF_BC + B_WIDTH
OFF_Q = OFF_BX + B_WIDTH
OFF_K = OFF_Q + 2 * ATT_HEADS * QK_DIM
OFF_V = OFF_K + 2 * ATT_HEADS * QK_DIM
D_IN = OFF_V + C_WIDTH

N_EXPERTS = 64
TOP_K = 8
D_EXPERT = D_MODEL // 4
D_SHARED = D_MODEL // 4
ROUTED_SCALE = 2.5

EPS = 1e-6

kernel_name = "hybrid_dit_gmlp_conv_diffattn_moe"


def rmsnorm(x, g):
    xf = x.astype(jnp.float32)
    y = xf * lax.rsqrt(jnp.mean(xf * xf, axis=-1, keepdims=True) + EPS)
    return y.astype(x.dtype) * g


def axial_rope_tables(rows):
    row = jnp.repeat(jnp.arange(rows), GRID_W).astype(jnp.float32)
    col = jnp.tile(jnp.arange(GRID_W), rows).astype(jnp.float32)
    n_freq = QK_DIM // 4
    inv = ROPE_BASE ** (-jnp.arange(n_freq, dtype=jnp.float32) / n_freq)
    ang_r = row[:, None] * inv
    ang_c = col[:, None] * inv
    return (jnp.cos(ang_r), jnp.sin(ang_r), jnp.cos(ang_c), jnp.sin(ang_c))


def rope_half(x, cos, sin):
    x1, x2 = jnp.split(x, 2, axis=-1)
    cos = cos.astype(x.dtype)
    sin = sin.astype(x.dtype)
    return jnp.concatenate([x1 * cos - x2 * sin, x2 * cos + x1 * sin], axis=-1)


def axial_rope(x, tabs):
    cr, sr, cc, sc = tabs
    xr, xc = jnp.split(x, 2, axis=-1)
    return jnp.concatenate([rope_half(xr, cr, sr), rope_half(xc, cc, sc)], axis=-1)


def chunk_gmlp(u, v, g_v, w_s, b_s):
    bn, L, _ = v.shape
    v = rmsnorm(v, g_v).reshape(bn, L // CHUNK, CHUNK, A_GROUPS, A_GD)
    v = jnp.einsum('gpq,bnqgd->bnpgd', w_s, v) + b_s.T[:, :, None]
    return u * v.reshape(bn, L, A_WIDTH)


def short_conv(bg, cg, xin, w_conv):
    z = cg * xin
    y = lax.conv_general_dilated(z, w_conv[:, None, :], window_strides=(1,), padding=((1, 1),),
                                 dimension_numbers=('NWC', 'WIO', 'NWC'),
                                 feature_group_count=B_WIDTH)
    return bg * y


def qk_heads(t, g):
    bn, L, _ = t.shape
    t = t.reshape(bn, L, 2, ATT_HEADS, QK_DIM).transpose(2, 0, 3, 1, 4)
    return rmsnorm(t, g)


def v_heads(t):
    bn, L, _ = t.shape
    return t.reshape(bn, L, ATT_HEADS, V_DIM).transpose(0, 2, 1, 3)


def diff_attend(q1, q2, k1, k2, v, lam):
    scale = QK_DIM ** -0.5
    s1 = jnp.einsum('bhqd,bhkd->bhqk', q1, k1, preferred_element_type=jnp.float32) * scale
    s2 = jnp.einsum('bhqd,bhkd->bhqk', q2, k2, preferred_element_type=jnp.float32) * scale
    w = jax.nn.softmax(s1, axis=-1) - lam.astype(jnp.float32) * jax.nn.softmax(s2, axis=-1)
    return jnp.einsum('bhqk,bhkd->bhqd', w.astype(v.dtype), v)


def token_mixers(h, hc, w_in, w_out, g_v, w_s, b_s, w_conv, g_q, g_k,
                 lam_q1, lam_k1, lam_q2, lam_k2, g_sub, lam_init, tabs, last):
    bn, L, _ = h.shape
    lam = jnp.exp(jnp.sum(lam_q1 * lam_k1)) - jnp.exp(jnp.sum(lam_q2 * lam_k2)) + lam_init

    def local_mixers(p):
        u, v = jnp.split(jax.nn.gelu(p[..., OFF_AU:OFF_BB], approximate=False), 2, axis=-1)
        ya = chunk_gmlp(u, v, g_v, w_s, b_s)
        yb = short_conv(p[..., OFF_BB:OFF_BC], p[..., OFF_BC:OFF_BX], p[..., OFF_BX:OFF_Q], w_conv)
        return ya, yb

    def head_out(y):
        return (rmsnorm(y, g_sub) * (1.0 - lam_init)).reshape(y.shape[0], y.shape[1], C_WIDTH)

    kv_c = hc @ w_in[:, OFF_K:]
    k_c = qk_heads(kv_c[..., :OFF_V - OFF_K], g_k)
    v_c = v_heads(kv_c[..., OFF_V - OFF_K:])

    p = h @ w_in
    ya, yb = local_mixers(p)
    q = axial_rope(qk_heads(p[..., OFF_Q:OFF_K], g_q), tabs)
    k_l = axial_rope(qk_heads(p[..., OFF_K:OFF_V], g_k), tabs)
    v_l = v_heads(p[..., OFF_V:D_IN])
    k_all = jnp.concatenate([k_c, k_l], axis=3)
    v_all = jnp.concatenate([v_c, v_l], axis=2)
    nb = L // Q_BLOCK
    qb = q.reshape(2, bn, ATT_HEADS, nb, Q_BLOCK, QK_DIM).transpose(3, 0, 1, 2, 4, 5)
    yq = lax.map(lambda qq: diff_attend(qq[0], qq[1], k_all[0], k_all[1], v_all, lam), qb)
    yq = yq.transpose(1, 0, 3, 2, 4).reshape(bn, L, ATT_HEADS, V_DIM)
    y = jnp.concatenate([ya, yb, head_out(yq)], axis=-1) @ w_out
    if last:
        return y, None

    pc = hc @ w_in[:, :OFF_K]
    ya_c, yb_c = local_mixers(pc)
    q_c = qk_heads(pc[..., OFF_Q:OFF_K], g_q)
    yq_c = diff_attend(q_c[0], q_c[1], k_c[0], k_c[1], v_c, lam).transpose(0, 2, 1, 3)
    yc = jnp.concatenate([ya_c, yb_c, head_out(yq_c)], axis=-1) @ w_out
    return y, yc


def moe(t, w_router, b_router, w_gate, w_up, w_down, ws_gate, ws_up, ws_down):
    scores = jax.nn.sigmoid((t @ w_router).astype(jnp.float32))
    _, idx = lax.top_k(scores + b_router.astype(jnp.float32), TOP_K)
    sel = jnp.take_along_axis(scores, idx, axis=-1)
    sel = sel / jnp.sum(sel, axis=-1, keepdims=True) * ROUTED_SCALE
    gates = jnp.einsum('nk,nke->ne', sel, jax.nn.one_hot(idx, N_EXPERTS, dtype=jnp.float32)).astype(t.dtype)
    shared = (jax.nn.silu(t @ ws_gate) * (t @ ws_up)) @ ws_down

    def expert_step(acc, e):
        wg, wu, wd, ge = e
        hid = jax.nn.silu(t @ wg) * (t @ wu)
        return acc + (ge[:, None] * hid) @ wd, None

    out, _ = lax.scan(expert_step, shared, (w_gate, w_up, w_down, gates.T))
    return out


def setup_inputs(seed: int = 0) -> dict:
    key = jax.random.key(seed)
    ks = jax.random.split(key, 32)
    D = D_MODEL

    def nrm(k, shape, s):
        return jax.random.normal(k, shape, jnp.float32) * s

    return {
        "x": nrm(ks[0], (BATCH, SEQ, D), 1.0),
        "c": nrm(ks[1], (BATCH, D), 1.0),
        "ctx": nrm(ks[2], (BATCH, CTX_LEN, D), 1.0),
        "c_ctx": nrm(ks[3], (D,), 1.0),
        "w_ada": nrm(ks[4], (DEPTH, D, 6 * D), 0.5 * D ** -0.5),
        "b_ada": nrm(ks[5], (DEPTH, 6 * D), 0.02),
        "g_norm1": 1.0 + nrm(ks[6], (DEPTH, D), 0.02),
        "g_norm2": 1.0 + nrm(ks[7], (DEPTH, D), 0.02),
        "w_in": nrm(ks[8], (DEPTH, D, D_IN), D ** -0.5),
        "w_out": nrm(ks[9], (DEPTH, D_MIX, D), D_MIX ** -0.5),
        "g_v": 1.0 + nrm(ks[10], (DEPTH, A_WIDTH), 0.02),
        "w_s": nrm(ks[11], (DEPTH, A_GROUPS, CHUNK, CHUNK), CHUNK ** -0.5),
        "b_s": 1.0 + nrm(ks[12], (DEPTH, A_GROUPS, CHUNK), 0.1),
        "w_conv": nrm(ks[13], (DEPTH, CONV_W, B_WIDTH), CONV_W ** -0.5),
        "g_q": 1.0 + nrm(ks[14], (DEPTH, QK_DIM), 0.02),
        "g_k": 1.0 + nrm(ks[15], (DEPTH, QK_DIM), 0.02),
        "lam_q1": nrm(ks[16], (DEPTH, QK_DIM), 0.1),
        "lam_k1": nrm(ks[17], (DEPTH, QK_DIM), 0.1),
        "lam_q2": nrm(ks[18], (DEPTH, QK_DIM), 0.1),
        "lam_k2": nrm(ks[19], (DEPTH, QK_DIM), 0.1),
        "g_sub": 1.0 + nrm(ks[20], (DEPTH, V_DIM), 0.02),
        "w_router": nrm(ks[21], (DEPTH, D, N_EXPERTS), D ** -0.5),
        "b_router": nrm(ks[22], (DEPTH, N_EXPERTS), 0.01),
        "w_gate": nrm(ks[23], (DEPTH, N_EXPERTS, D, D_EXPERT), D ** -0.5),
        "w_up": nrm(ks[24], (DEPTH, N_EXPERTS, D, D_EXPERT), D ** -0.5),
        "w_down": nrm(ks[25], (DEPTH, N_EXPERTS, D_EXPERT, D), D_EXPERT ** -0.5),
        "ws_gate": nrm(ks[26], (DEPTH, D, D_SHARED), D ** -0.5),
        "ws_up": nrm(ks[27], (DEPTH, D, D_SHARED), D ** -0.5),
        "ws_down": nrm(ks[28], (DEPTH, D_SHARED, D), D_SHARED ** -0.5),
    }


def reference(x, c, ctx, c_ctx, w_ada, b_ada, g_norm1, g_norm2, w_in, w_out, g_v, w_s, b_s,
              w_conv, g_q, g_k, lam_q1, lam_k1, lam_q2, lam_k2, g_sub, w_router, b_router,
              w_gate, w_up, w_down, ws_gate, ws_up, ws_down):
    bn, L, D = x.shape
    rows = L // GRID_W
    tabs = axial_rope_tables(rows)
    xc = ctx
    for l in range(DEPTH):
        last = l == DEPTH - 1
        lam_init = 0.8 - 0.6 * math.exp(-0.3 * l)
        mod = (jax.nn.silu(c) @ w_ada[l] + b_ada[l])[:, None, :]
        mod_c = jax.nn.silu(c_ctx) @ w_ada[l] + b_ada[l]
        sh1, sc1, g1, sh2, sc2, g2 = jnp.split(mod, 6, axis=-1)
        csh1, csc1, cg1, csh2, csc2, cg2 = jnp.split(mod_c, 6, axis=-1)

        h = rmsnorm(x, g_norm1[l]) * (1.0 + sc1) + sh1
        hc = rmsnorm(xc, g_norm1[l]) * (1.0 + csc1) + csh1
        y, yc = token_mixers(h, hc, w_in[l], w_out[l], g_v[l], w_s[l], b_s[l], w_conv[l],
                             g_q[l], g_k[l], lam_q1[l], lam_k1[l], lam_q2[l], lam_k2[l],
                             g_sub[l], lam_init, tabs, last)
        x = x + g1 * y
        h2 = rmsnorm(x, g_norm2[l]) * (1.0 + sc2) + sh2
        if last:
            f = moe(h2.reshape(-1, D), w_router[l], b_router[l], w_gate[l], w_up[l], w_down[l],
                    ws_gate[l], ws_up[l], ws_down[l])
            x = x + g2 * f.reshape(bn, L, D)
        else:
            xc = xc + cg1 * yc
            h2c = rmsnorm(xc, g_norm2[l]) * (1.0 + csc2) + csh2
            t = jnp.concatenate([h2.reshape(-1, D), h2c.reshape(-1, D)], axis=0)
            f = moe(t, w_router[l], b_router[l], w_gate[l], w_up[l], w_down[l],
                    ws_gate[l], ws_up[l], ws_down[l])
            x = x + g2 * f[:bn * L].reshape(bn, L, D)
            xc = xc + cg2 * f[bn * L:].reshape(bn, -1, D)
    return x
```

```python
import functools
import math

import numpy as np
import jax
import jax.numpy as jnp
from jax import lax
from jax.experimental import pallas as pl
from jax.experimental.pallas import tpu as pltpu

F32 = jnp.float32
BF16 = jnp.bfloat16

D = 1024
B = 8
L = 2048
DEPTH = 2
GRID_W = 64
CTX = 256
A_WIDTH = 256
A_GROUPS = 4
A_GD = 64
CHUNK = 128
B_WIDTH = 256
C_WIDTH = 512
HEADS = 4
V_DIM = 128
QK_DIM = 64
ROPE_BASE = 10000.0
OFF_BB = 512
OFF_BC = 768
OFF_BX = 1024
OFF_Q = 1280
OFF_K = 1792
OFF_V = 2304
D_IN = 2816
N_EXPERTS = 64
TOP_K = 8
D_EXPERT = 256
D_SHARED = 256
ROUTED_SCALE = 2.5
EPS = 1e-6

NL = B * L
NC = B * CTX
NR = NL + NC
MOD_ROWS = 16
LOG2E = 1.4426950408889634

TM_IN = 512
TM_MIX = 256
TQ = 256
TM_OUT = 256
TM_X = 512
TM_F = 256

_DN_T = (((1,), (1,)), ((), ()))


def _cparams(sem, vmem_mb=None):
    kw = dict(dimension_semantics=sem)
    if vmem_mb is not None:
        kw["vmem_limit_bytes"] = vmem_mb * 1024 * 1024
    return pltpu.CompilerParams(**kw)


def _mod_row(i, tm):
    return jnp.where(i < NL // tm, i // (L // tm), B)


def _mod_spec(tm, chunk):
    return pl.BlockSpec((None, 1, D), lambda i: (_mod_row(i, tm), 0, chunk))


def _ada_kernel(c_ref, w_ref, b_ref, o_ref):
    c = c_ref[...]
    cs = c * jax.nn.sigmoid(c)
    o_ref[...] = jnp.dot(cs, w_ref[...], preferred_element_type=F32,
                         precision=lax.Precision.HIGHEST) + b_ref[...]


def _ada(cc, w_ada, b_ada):
    nb = 6
    return pl.pallas_call(
        _ada_kernel,
        out_shape=jax.ShapeDtypeStruct((DEPTH, MOD_ROWS, 6 * D), F32),
        grid=(DEPTH, nb),
        in_specs=[pl.BlockSpec((MOD_ROWS, D), lambda l, j: (0, 0)),
                  pl.BlockSpec((None, D, D), lambda l, j: (l, 0, j)),
                  pl.BlockSpec((None, 1, D), lambda l, j: (l, 0, j))],
        out_specs=pl.BlockSpec((None, MOD_ROWS, D), lambda l, j: (l, 0, j)),
        compiler_params=_cparams(("arbitrary", "arbitrary"), 40),
        name="ada_mod",
    )(cc, w_ada, b_ada.reshape(DEPTH, 1, 6 * D))


def _rms_mod(x, g, sc, sh):
    ms = jnp.mean(x * x, axis=-1, keepdims=True)
    return x * lax.rsqrt(ms + EPS) * g * (1.0 + sc) + sh


def _in_proj_kernel(x_ref, g_ref, sh_ref, sc_ref, w_ref, o_ref):
    h = _rms_mod(x_ref[...], g_ref[...], sc_ref[...], sh_ref[...])
    o_ref[...] = jnp.dot(h.astype(BF16), w_ref[...], preferred_element_type=F32).astype(o_ref.dtype)


def _in_proj(xa, g, mod3, w_bf):
    tm = TM_IN
    return pl.pallas_call(
        _in_proj_kernel,
        out_shape=jax.ShapeDtypeStruct((NR, D_IN), BF16),
        grid=(NR // tm,),
        in_specs=[pl.BlockSpec((tm, D), lambda i: (i, 0)),
                  pl.BlockSpec((1, D), lambda i: (0, 0)),
                  _mod_spec(tm, 0), _mod_spec(tm, 1),
                  pl.BlockSpec((D, D_IN), lambda i: (0, 0))],
        out_specs=pl.BlockSpec((tm, D_IN), lambda i: (i, 0)),
        compiler_params=_cparams(("parallel",), 48),
        name="in_proj",
    )(xa, g.reshape(1, D), mod3, mod3, w_bf)


def _group_rms(t, g, bd):
    sq = t * t
    hi = sq.astype(BF16)
    lo = (sq - hi.astype(F32)).astype(BF16)
    ms = (jnp.dot(hi, bd, preferred_element_type=F32) + jnp.dot(lo, bd, preferred_element_type=F32))
    return t * lax.rsqrt(ms + EPS) * g


def _rope(t, cos, sin):
    w = t.shape[1]
    lane = lax.broadcasted_iota(jnp.int32, t.shape, 1)
    first = (lane % 32) < 16
    partner = jnp.where(first, pltpu.roll(t, w - 16, 1), pltpu.roll(t, 16, 1))
    cos4 = jnp.concatenate([cos] * (w // 128), axis=1)
    sin4 = jnp.concatenate([sin] * (w // 128), axis=1)
    return t * cos4 + partner * sin4


def _mixers_kernel(p_ref, pc_ref, px_ref, nc_ref, nx_ref, cos_ref, sin_ref, gv_ref, ws_ref, bias_ref,
                   wconv_ref, gq_ref, gk_ref, bd_ref, yab_ref, qa_ref, qb_ref, kk_ref):
    tm = TM_MIX
    i = pl.program_id(0)
    tiles_per_seq = L // tm
    is_lat = i < NL // tm
    is_start = jnp.logical_or(jnp.logical_not(is_lat), i % tiles_per_seq == 0)
    is_end = jnp.logical_or(jnp.logical_not(is_lat), i % tiles_per_seq == tiles_per_seq - 1)

    uv = p_ref[:, 0:2 * A_WIDTH].astype(F32)
    uv = 0.5 * uv * (1.0 + lax.erf(uv * (2.0 ** -0.5)))
    u = uv[:, :A_WIDTH]
    v = uv[:, A_WIDTH:]
    ms = jnp.mean(v * v, axis=-1, keepdims=True)
    vb = (v * lax.rsqrt(ms + EPS) * gv_ref[...]).astype(BF16)
    lane = lax.broadcasted_iota(jnp.int32, (CHUNK, 128), 1)
    mixes = []
    for c in range(tm // CHUNK):
        vc = vb[c * CHUNK:(c + 1) * CHUNK]
        halves = []
        for j in range(2):
            vj = vc[:, j * 128:(j + 1) * 128]
            m0 = jnp.dot(ws_ref[2 * j], vj, preferred_element_type=F32)
            m1 = jnp.dot(ws_ref[2 * j + 1], vj, preferred_element_type=F32)
            halves.append(jnp.where(lane < A_GD, m0, m1))
        mixes.append(jnp.concatenate(halves, axis=1) + bias_ref[...])
    ya = u * jnp.concatenate(mixes, axis=0)

    bg = p_ref[:, OFF_BB:OFF_BC].astype(F32)
    z = p_ref[:, OFF_BC:OFF_BX].astype(F32) * p_ref[:, OFF_BX:OFF_Q].astype(F32)
    zp = (pc_ref[...].astype(F32) * px_ref[...].astype(F32))[15:16]
    zn = (nc_ref[...].astype(F32) * nx_ref[...].astype(F32))[0:1]
    zp = jnp.where(is_start, 0.0, zp)
    zn = jnp.where(is_end, 0.0, zn)
    row = lax.broadcasted_iota(jnp.int32, z.shape, 0)
    z_prev = jnp.where(row == 0, zp, pltpu.roll(z, 1, 0))
    z_next = jnp.where(row == tm - 1, zn, pltpu.roll(z, tm - 1, 0))
    yb = bg * (z_prev * wconv_ref[0:1] + z * wconv_ref[1:2] + z_next * wconv_ref[2:3])
    yab_ref[...] = jnp.concatenate([ya, yb], axis=1).astype(BF16)

    cos = cos_ref[...]
    sin = sin_ref[...]
    bd = bd_ref[...]
    q = _rope(_group_rms(p_ref[:, OFF_Q:OFF_K].astype(F32), gq_ref[...], bd), cos, sin)
    q = q * (QK_DIM ** -0.5 * LOG2E)
    lane5 = lax.broadcasted_iota(jnp.int32, q.shape, 1) % 128
    qa_ref[...] = jnp.where(lane5 < QK_DIM, q, 0.0).astype(BF16)
    qb_ref[...] = jnp.where(lane5 >= QK_DIM, q, 0.0).astype(BF16)
    k = _rope(_group_rms(p_ref[:, OFF_K:OFF_V].astype(F32), gk_ref[...], bd), cos, sin)
    kk_ref[...] = k.astype(BF16)


def _mixers(p, cos_t, sin_t, gv, ws_bf, bias_t, wconv, gq, gk, bd):
    tm = TM_MIX
    nt = NR // tm
    hb = tm // 16
    last_hb = NR // 16 - 1
    pos_blocks = L // tm

    def tab_map(i):
        return (jnp.where(i < NL // tm, i % pos_blocks, pos_blocks), 0)

    prev_map_c = lambda i: (jnp.maximum(i * hb - 1, 0), OFF_BC // 256)
    prev_map_x = lambda i: (jnp.maximum(i * hb - 1, 0), OFF_BX // 256)
    next_map_c = lambda i: (jnp.minimum((i + 1) * hb, last_hb), OFF_BC // 256)
    next_map_x = lambda i: (jnp.minimum((i + 1) * hb, last_hb), OFF_BX // 256)
    const2 = lambda i: (0, 0)
    outs = pl.pallas_call(
        _mixers_kernel,
        out_shape=[jax.ShapeDtypeStruct((NR, 2 * A_WIDTH), BF16),
                   jax.ShapeDtypeStruct((NR, 512), BF16),
                   jax.ShapeDtypeStruct((NR, 512), BF16),
                   jax.ShapeDtypeStruct((NR, 512), BF16)],
        grid=(nt,),
        in_specs=[pl.BlockSpec((tm, D_IN), lambda i: (i, 0)),
                  pl.BlockSpec((16, 256), prev_map_c), pl.BlockSpec((16, 256), prev_map_x),
                  pl.BlockSpec((16, 256), next_map_c), pl.BlockSpec((16, 256), next_map_x),
                  pl.BlockSpec((tm, 128), tab_map), pl.BlockSpec((tm, 128), tab_map),
                  pl.BlockSpec((1, A_WIDTH), const2),
                  pl.BlockSpec((A_GROUPS, CHUNK, CHUNK), lambda i: (0, 0, 0)),
                  pl.BlockSpec((CHUNK, A_WIDTH), const2),
                  pl.BlockSpec((8, B_WIDTH), const2),
                  pl.BlockSpec((1, 512), const2), pl.BlockSpec((1, 512), const2),
                  pl.BlockSpec((512, 512), const2)],
        out_specs=[pl.BlockSpec((tm, 512), lambda i: (i, 0))] * 4,
        compiler_params=_cparams(("parallel",), 48),
        name="mixers",
    )(p, p, p, p, p, cos_t, sin_t, gv, ws_bf, bias_t, wconv, gq, gk, bd)
    return outs


def _attn_kernel(lam_ref, qa_ref, qb_ref, *rest, n_seg, coef):
    kv_refs = rest[:2 * n_seg]
    gsub_ref, o_ref, k_scr, v_scr = rest[2 * n_seg:]

    @pl.when(pl.program_id(2) == 0)
    def _():
        off = 0
        for s in range(n_seg):
            n = kv_refs[s].shape[0]
            k_scr[off:off + n, :] = kv_refs[s][...]
            v_scr[off:off + n, :] = kv_refs[n_seg + s][...]
            off += n

    k = k_scr[...]
    s1 = lax.dot_general(qa_ref[...], k, _DN_T, preferred_element_type=F32)
    s2 = lax.dot_general(qb_ref[...], k, _DN_T, preferred_element_type=F32)
    p1 = jnp.exp2(s1 - jnp.max(s1, axis=-1, keepdims=True))
    p2 = jnp.exp2(s2 - jnp.max(s2, axis=-1, keepdims=True))
    a1 = 1.0 / jnp.sum(p1, axis=-1, keepdims=True)
    a2 = lam_ref[0] / jnp.sum(p2, axis=-1, keepdims=True)
    w = (p1 * a1 - p2 * a2).astype(BF16)
    o = jnp.dot(w, v_scr[...], preferred_element_type=F32)
    ms = jnp.mean(o * o, axis=-1, keepdims=True)
    o_ref[...] = (o * lax.rsqrt(ms + EPS) * gsub_ref[...] * coef).astype(o_ref.dtype)


def _attention(lam, qa, qb, kk, p, gsub, coef, *, ctx_queries):
    vcol = OFF_V // 128
    if ctx_queries:
        nq, lk, n_seg = 1, CTX, 1
        q_map = lambda b, h, qi: (NL // TQ + b, h)
        kv_specs = [pl.BlockSpec((CTX, 128), lambda b, h, qi: (NL // CTX + b, h)),
                    pl.BlockSpec((CTX, 128), lambda b, h, qi: (NL // CTX + b, vcol + h))]
        kv_args = [kk, p]
        rows = NC
        o_map = lambda b, h, qi: (b, h)
    else:
        nq, lk, n_seg = L // TQ, CTX + L, 2
        q_map = lambda b, h, qi: (b * (L // TQ) + qi, h)
        kv_specs = [pl.BlockSpec((CTX, 128), lambda b, h, qi: (NL // CTX + b, h)),
                    pl.BlockSpec((L, 128), lambda b, h, qi: (b, h)),
                    pl.BlockSpec((CTX, 128), lambda b, h, qi: (NL // CTX + b, vcol + h)),
                    pl.BlockSpec((L, 128), lambda b, h, qi: (b, vcol + h))]
        kv_args = [kk, kk, p, p]
        rows = NL
        o_map = lambda b, h, qi: (b * (L // TQ) + qi, h)
    return pl.pallas_call(
        functools.partial(_attn_kernel, n_seg=n_seg, coef=coef),
        out_shape=jax.ShapeDtypeStruct((rows, C_WIDTH), BF16),
        grid=(B, HEADS, nq),
        in_specs=[pl.BlockSpec(memory_space=pltpu.SMEM),
                  pl.BlockSpec((TQ, 128), q_map), pl.BlockSpec((TQ, 128), q_map)]
                 + kv_specs + [pl.BlockSpec((1, V_DIM), lambda b, h, qi: (0, 0))],
        out_specs=pl.BlockSpec((TQ, 128), o_map),
        scratch_shapes=[pltpu.VMEM((lk, 128), BF16), pltpu.VMEM((lk, 128), BF16)],
        compiler_params=_cparams(("parallel", "parallel", "arbitrary"), 48),
        name="attn_ctx" if ctx_queries else "attn_lat",
    )(lam, qa, qb, *kv_args, gsub)


def _out_router_kernel(x_ref, yab_ref, yc_ref, wo_ref, g1_ref, g2n_ref, sh2_ref, sc2_ref, wrh_ref, wrl_ref, br_ref,
                       xo_ref, h2_ref, idx_ref, gate_ref):
    y = (jnp.dot(yab_ref[...], wo_ref[0:2 * A_WIDTH, :], preferred_element_type=F32)
         + jnp.dot(yc_ref[...], wo_ref[2 * A_WIDTH:, :], preferred_element_type=F32))
    x = x_ref[...] + g1_ref[...] * y
    xo_ref[...] = x
    h2 = _rms_mod(x, g2n_ref[...], sc2_ref[...], sh2_ref[...])
    hi = h2.astype(BF16)
    h2_ref[...] = hi
    lo = (h2 - hi.astype(F32)).astype(BF16)
    wh = wrh_ref[...]
    z = (lax.dot_general(wh, hi, _DN_T, preferred_element_type=F32)
         + lax.dot_general(wh, lo, _DN_T, preferred_element_type=F32)
         + lax.dot_general(wrl_ref[...], hi, _DN_T, preferred_element_type=F32))
    scores = jax.nn.sigmoid(z)
    work = scores + br_ref[...]
    eio = lax.broadcasted_iota(jnp.int32, work.shape, 0)
    idxs, sels = [], []
    for _ in range(TOP_K):
        m = jnp.max(work, axis=0, keepdims=True)
        idx = jnp.min(jnp.where(work == m, eio, N_EXPERTS), axis=0, keepdims=True)
        hit = eio == idx
        sels.append(jnp.sum(jnp.where(hit, scores, 0.0), axis=0, keepdims=True))
        idxs.append(idx)
        work = jnp.where(hit, -jnp.inf, work)
    sel = jnp.concatenate(sels, axis=0)
    idx_ref[...] = jnp.concatenate(idxs, axis=0)
    gate_ref[...] = sel / jnp.sum(sel, axis=0, keepdims=True) * ROUTED_SCALE


def _out_router(xa, yab, yc, wo_bf, g2n, mod3, wr_hi, wr_lo, br, n_rows):
    tm = TM_OUT
    const2 = lambda i: (0, 0)
    row = lambda i: (i, 0)
    return pl.pallas_call(
        _out_router_kernel,
        out_shape=[jax.ShapeDtypeStruct((n_rows, D), F32),
                   jax.ShapeDtypeStruct((n_rows, D), BF16),
                   jax.ShapeDtypeStruct((TOP_K, n_rows), jnp.int32),
                   jax.ShapeDtypeStruct((TOP_K, n_rows), F32)],
        grid=(n_rows // tm,),
        in_specs=[pl.BlockSpec((tm, D), row), pl.BlockSpec((tm, 512), row), pl.BlockSpec((tm, 512), row),
                  pl.BlockSpec((D, D), const2),
                  _mod_spec(tm, 2),
                  pl.BlockSpec((1, D), const2), _mod_spec(tm, 3), _mod_spec(tm, 4),
                  pl.BlockSpec((N_EXPERTS, D), const2), pl.BlockSpec((N_EXPERTS, D), const2),
                  pl.BlockSpec((N_EXPERTS, 1), const2)],
        out_specs=[pl.BlockSpec((tm, D), row), pl.BlockSpec((tm, D), row),
                   pl.BlockSpec((TOP_K, tm), lambda i: (0, i)), pl.BlockSpec((TOP_K, tm), lambda i: (0, i))],
        compiler_params=_cparams(("parallel",), 48),
        name="out_router",
    )(xa, yab, yc, wo_bf, mod3, g2n.reshape(1, D), mod3, mod3, wr_hi, wr_lo, br.reshape(N_EXPERTS, 1))


def _experts_kernel(te_ref, nu_ref, x_ref, wg_ref, wu_ref, wd_ref, y_ref):
    i = pl.program_id(0)

    @pl.when(i < nu_ref[0])
    def _():
        x = x_ref[...]
        a = jnp.dot(x, wg_ref[...], preferred_element_type=F32)
        b = jnp.dot(x, wu_ref[...], preferred_element_type=F32)
        hid = (a * jax.nn.sigmoid(a) * b).astype(BF16)
        y_ref[...] = jnp.dot(hid, wd_ref[...], preferred_element_type=F32).astype(y_ref.dtype)

    @pl.when(i >= nu_ref[0])
    def _():
        y_ref[...] = jnp.zeros_like(y_ref)


def _experts(tile_expert, n_used, x_sorted, wg_bf, wu_bf, wd_bf):
    tm = TM_X
    r_pad = x_sorted.shape[0]
    nt = r_pad // tm
    xrow = lambda i, te, nu: (jnp.minimum(i, nu[0] - 1), 0)
    grid_spec = pltpu.PrefetchScalarGridSpec(
        num_scalar_prefetch=2,
        grid=(nt,),
        in_specs=[pl.BlockSpec((tm, D), xrow),
                  pl.BlockSpec((None, D, D_EXPERT), lambda i, te, nu: (te[i], 0, 0)),
                  pl.BlockSpec((None, D, D_EXPERT), lambda i, te, nu: (te[i], 0, 0)),
                  pl.BlockSpec((None, D_EXPERT, D), lambda i, te, nu: (te[i], 0, 0))],
        out_specs=pl.BlockSpec((tm, D), lambda i, te, nu: (i, 0)),
    )
    return pl.pallas_call(
        _experts_kernel,
        out_shape=jax.ShapeDtypeStruct((r_pad, D), BF16),
        grid_spec=grid_spec,
        compiler_params=_cparams(("arbitrary",), 48),
        name="experts",
    )(tile_expert, n_used, x_sorted, wg_bf, wu_bf, wd_bf)


def _shared_kernel(x_ref, h2_ref, moe_ref, wg_ref, wu_ref, wd_ref, g2_ref, o_ref):
    h = h2_ref[...]
    a = jnp.dot(h, wg_ref[...], preferred_element_type=F32)
    b = jnp.dot(h, wu_ref[...], preferred_element_type=F32)
    hid = (a * jax.nn.sigmoid(a) * b).astype(BF16)
    f = jnp.dot(hid, wd_ref[...], preferred_element_type=F32) + moe_ref[...]
    o_ref[...] = x_ref[...] + g2_ref[...] * f


def _shared_residual(xa, h2, moe_tok, wsg_bf, wsu_bf, wsd_bf, mod3, n_rows):
    tm = TM_F
    row = lambda i: (i, 0)
    const2 = lambda i: (0, 0)
    return pl.pallas_call(
        _shared_kernel,
        out_shape=jax.ShapeDtypeStruct((n_rows, D), F32),
        grid=(n_rows // tm,),
        in_specs=[pl.BlockSpec((tm, D), row), pl.BlockSpec((tm, D), row), pl.BlockSpec((tm, D), row),
                  pl.BlockSpec((D, D_SHARED), const2), pl.BlockSpec((D, D_SHARED), const2),
                  pl.BlockSpec((D_SHARED, D), const2), _mod_spec(tm, 5)],
        out_specs=pl.BlockSpec((tm, D), row),
        compiler_params=_cparams(("parallel",), 48),
        name="shared_residual",
    )(xa, h2, moe_tok, wsg_bf, wsu_bf, wsd_bf, mod3)


def _route_positions(idx_t, n_rows):
    tm = TM_X
    idx = idx_t.T
    mask = jnp.any(idx[:, :, None] == jnp.arange(N_EXPERTS, dtype=jnp.int32), axis=1).astype(jnp.int32)
    csum = jnp.cumsum(mask, axis=0)
    counts = csum[-1]
    padded = ((counts + tm - 1) // tm) * tm
    ends = jnp.cumsum(padded)
    offs = ends - padded
    pos = jnp.take_along_axis(offs[None, :] + csum - mask, idx, axis=1)
    r_pad = n_rows * TOP_K + N_EXPERTS * tm
    nt = r_pad // tm
    tile_expert = jnp.searchsorted(ends // tm, jnp.arange(nt, dtype=jnp.int32), side="right")
    tile_expert = jnp.minimum(tile_expert, N_EXPERTS - 1).astype(jnp.int32)
    n_used = (ends[-1] // tm).astype(jnp.int32).reshape(1)
    return pos, tile_expert, n_used, r_pad


def _q_perm():
    perm = np.empty(512, np.int32)
    for h in range(HEADS):
        for half in range(2):
            for d in range(QK_DIM):
                perm[h * 128 + half * QK_DIM + d] = half * HEADS * QK_DIM + h * QK_DIM + d
    return perm


def _rope_tables():
    t = jnp.arange(L)
    row = (t // GRID_W).astype(F32)
    col = (t % GRID_W).astype(F32)
    n_freq = QK_DIM // 4
    inv = ROPE_BASE ** (-jnp.arange(n_freq, dtype=F32) / n_freq)
    ar = row[:, None] * inv
    ac = col[:, None] * inv
    cos64 = jnp.concatenate([jnp.cos(ar), jnp.cos(ar), jnp.cos(ac), jnp.cos(ac)], axis=1)
    sin64 = jnp.concatenate([-jnp.sin(ar), jnp.sin(ar), -jnp.sin(ac), jnp.sin(ac)], axis=1)
    cos_t = jnp.concatenate([jnp.tile(cos64, (1, 2)), jnp.ones((TM_MIX, 128), F32)], axis=0)
    sin_t = jnp.concatenate([jnp.tile(sin64, (1, 2)), jnp.zeros((TM_MIX, 128), F32)], axis=0)
    return cos_t, sin_t


def _split_bf16(w):
    hi = w.astype(BF16)
    return hi, (w - hi.astype(F32)).astype(BF16)


def kernel(x, c, ctx, c_ctx, w_ada, b_ada, g_norm1, g_norm2, w_in, w_out, g_v, w_s, b_s, w_conv, g_q, g_k,
           lam_q1, lam_k1, lam_q2, lam_k2, g_sub, w_router, b_router, w_gate, w_up, w_down,
           ws_gate, ws_up, ws_down):
    xa = jnp.concatenate([x.reshape(NL, D), ctx.reshape(NC, D)], axis=0)
    cc = jnp.concatenate([c, c_ctx[None, :], jnp.zeros((MOD_ROWS - B - 1, D), F32)], axis=0)
    mod = _ada(cc, w_ada, b_ada)
    cos_t, sin_t = _rope_tables()
    perm = _q_perm()
    col_perm = np.concatenate([np.arange(OFF_Q), OFF_Q + perm, OFF_K + perm, np.arange(OFF_V, D_IN)])
    bd = jnp.asarray(np.kron(np.eye(8, dtype=np.float32), np.full((64, 64), 1.0 / 64, np.float32)), BF16)

    for l in range(DEPTH):
        last = l == DEPTH - 1
        lam_init = 0.8 - 0.6 * math.exp(-0.3 * l)
        lam = (jnp.exp(jnp.sum(lam_q1[l] * lam_k1[l])) - jnp.exp(jnp.sum(lam_q2[l] * lam_k2[l]))
               + lam_init).reshape(1).astype(F32)
        mod3 = mod[l].reshape(MOD_ROWS, 1, 6 * D)
        w_in_bf = w_in[l][:, col_perm].astype(BF16)
        p = _in_proj(xa, g_norm1[l], mod3, w_in_bf)
        bias_t = jnp.repeat(b_s[l].T, A_GD, axis=1)
        wconv = jnp.concatenate([w_conv[l], jnp.zeros((5, B_WIDTH), F32)], axis=0)
        yab, qa, qb, kk = _mixers(p, cos_t, sin_t, g_v[l].reshape(1, A_WIDTH), w_s[l].astype(BF16), bias_t, wconv,
                                  jnp.tile(g_q[l], 8).reshape(1, 512), jnp.tile(g_k[l], 8).reshape(1, 512), bd)
        gsub = g_sub[l].reshape(1, V_DIM)
        coef = 1.0 - lam_init
        yc = _attention(lam, qa, qb, kk, p, gsub, coef, ctx_queries=False)
        if last:
            n_rows = NL
        else:
            n_rows = NR
            yc = jnp.concatenate([yc, _attention(lam, qa, qb, kk, p, gsub, coef, ctx_queries=True)], axis=0)
        wr_hi, wr_lo = _split_bf16(w_router[l].T)
        xa, h2, idx_t, gate_t = _out_router(xa, yab, yc, w_out[l].astype(BF16), g_norm2[l], mod3,
                                            wr_hi, wr_lo, b_router[l], n_rows)
        pos, tile_expert, n_used, r_pad = _route_positions(idx_t, n_rows)
        flat = pos.reshape(-1)
        src_tok = jnp.zeros((r_pad,), jnp.int32).at[flat].set(
            jnp.arange(n_rows * TOP_K, dtype=jnp.int32) // TOP_K)
        x_sorted = jnp.take(h2, src_tok, axis=0)
        y_sorted = _experts(tile_expert, n_used, x_sorted, w_gate[l].astype(BF16), w_up[l].astype(BF16),
                            w_down[l].astype(BF16))
        yg = jnp.take(y_sorted, flat, axis=0).reshape(n_rows, TOP_K, D).astype(F32)
        moe_tok = jnp.einsum("nk,nkd->nd", gate_t.T, yg)
        xa = _shared_residual(xa, h2, moe_tok, ws_gate[l].astype(BF16), ws_up[l].astype(BF16),
                              ws_down[l].astype(BF16), mod3, n_rows)
    return xa.reshape(B, L, D)
```

```python
import functools
import math

import numpy as np
import jax
import jax.numpy as jnp
from jax import lax
from jax.experimental import pallas as pl
from jax.experimental.pallas import tpu as pltpu

F32 = jnp.float32
BF16 = jnp.bfloat16

D = 1024
B = 8
L = 2048
DEPTH = 2
GRID_W = 64
CTX = 256
A_WIDTH = 256
A_GROUPS = 4
A_GD = 64
CHUNK = 128
B_WIDTH = 256
C_WIDTH = 512
HEADS = 4
V_DIM = 128
QK_DIM = 64
ROPE_BASE = 10000.0
OFF_BB = 512
OFF_BC = 768
OFF_BX = 1024
OFF_Q = 1280
OFF_K = 1792
OFF_V = 2304
D_IN = 2816
N_EXPERTS = 64
TOP_K = 8
D_EXPERT = 256
D_SHARED = 256
ROUTED_SCALE = 2.5
EPS = 1e-6

NL = B * L
NC = B * CTX
NR = NL + NC
MOD_ROWS = 16
LOG2E = 1.4426950408889634

TM_IN = 512
TM_MIX = 256
TQ = 256
TM_OUT = 256
TM_X = 512
TM_F = 256

_DN_T = (((1,), (1,)), ((), ()))


def _cparams(sem, vmem_mb=None):
    kw = dict(dimension_semantics=sem)
    if vmem_mb is not None:
        kw["vmem_limit_bytes"] = vmem_mb * 1024 * 1024
    return pltpu.CompilerParams(**kw)


def _mod_row(i, tm):
    return jnp.where(i < NL // tm, i // (L // tm), B)


def _mod_spec(tm, chunk):
    return pl.BlockSpec((None, 1, D), lambda i: (_mod_row(i, tm), 0, chunk))


def _ada_kernel(c_ref, w_ref, b_ref, o_ref):
    c = c_ref[...]
    cs = c * jax.nn.sigmoid(c)
    o_ref[...] = jnp.dot(cs, w_ref[...], preferred_element_type=F32,
                         precision=lax.Precision.HIGHEST) + b_ref[...]


def _ada(cc, w_ada, b_ada):
    nb = 6
    return pl.pallas_call(
        _ada_kernel,
        out_shape=jax.ShapeDtypeStruct((DEPTH, MOD_ROWS, 6 * D), F32),
        grid=(DEPTH, nb),
        in_specs=[pl.BlockSpec((MOD_ROWS, D), lambda l, j: (0, 0)),
                  pl.BlockSpec((None, D, D), lambda l, j: (l, 0, j)),
                  pl.BlockSpec((None, 1, D), lambda l, j: (l, 0, j))],
        out_specs=pl.BlockSpec((None, MOD_ROWS, D), lambda l, j: (l, 0, j)),
        compiler_params=_cparams(("arbitrary", "arbitrary"), 40),
        name="ada_mod",
    )(cc, w_ada, b_ada.reshape(DEPTH, 1, 6 * D))


def _rms_mod(x, g, sc, sh):
    ms = jnp.mean(x * x, axis=-1, keepdims=True)
    return x * lax.rsqrt(ms + EPS) * g * (1.0 + sc) + sh


def _in_proj_kernel(x_ref, g_ref, sh_ref, sc_ref, w_ref, o_ref):
    h = _rms_mod(x_ref[...], g_ref[...], sc_ref[...], sh_ref[...])
    o_ref[...] = jnp.dot(h.astype(BF16), w_ref[...], preferred_element_type=F32).astype(o_ref.dtype)


def _in_proj(xa, g, mod3, w_bf):
    tm = TM_IN
    return pl.pallas_call(
        _in_proj_kernel,
        out_shape=jax.ShapeDtypeStruct((NR, D_IN), BF16),
        grid=(NR // tm,),
        in_specs=[pl.BlockSpec((tm, D), lambda i: (i, 0)),
                  pl.BlockSpec((1, D), lambda i: (0, 0)),
                  _mod_spec(tm, 0), _mod_spec(tm, 1),
                  pl.BlockSpec((D, D_IN), lambda i: (0, 0))],
        out_specs=pl.BlockSpec((tm, D_IN), lambda i: (i, 0)),
        compiler_params=_cparams(("parallel",), 48),
        name="in_proj",
    )(xa, g.reshape(1, D), mod3, mod3, w_bf)


def _group_rms(t, g, bd):
    sq = t * t
    hi = sq.astype(BF16)
    lo = (sq - hi.astype(F32)).astype(BF16)
    ms = (jnp.dot(hi, bd, preferred_element_type=F32) + jnp.dot(lo, bd, preferred_element_type=F32))
    return t * lax.rsqrt(ms + EPS) * g


def _rope(t, cos, sin):
    w = t.shape[1]
    lane = lax.broadcasted_iota(jnp.int32, t.shape, 1)
    first = (lane % 32) < 16
    partner = jnp.where(first, pltpu.roll(t, w - 16, 1), pltpu.roll(t, 16, 1))
    cos4 = jnp.concatenate([cos] * (w // 128), axis=1)
    sin4 = jnp.concatenate([sin] * (w // 128), axis=1)
    return t * cos4 + partner * sin4


def _mixers_kernel(p_ref, pc_ref, px_ref, nc_ref, nx_ref, cos_ref, sin_ref, gv_ref, ws_ref, bias_ref,
                   wconv_ref, gq_ref, gk_ref, bd_ref, yab_ref, qa_ref, qb_ref, kk_ref):
    tm = TM_MIX
    i = pl.program_id(0)
    tiles_per_seq = L // tm
    is_lat = i < NL // tm
    is_start = jnp.logical_or(jnp.logical_not(is_lat), i % tiles_per_seq == 0)
    is_end = jnp.logical_or(jnp.logical_not(is_lat), i % tiles_per_seq == tiles_per_seq - 1)

    uv = p_ref[:, 0:2 * A_WIDTH].astype(F32)
    uv = 0.5 * uv * (1.0 + lax.erf(uv * (2.0 ** -0.5)))
    u = uv[:, :A_WIDTH]
    v = uv[:, A_WIDTH:]
    ms = jnp.mean(v * v, axis=-1, keepdims=True)
    vb = (v * lax.rsqrt(ms + EPS) * gv_ref[...]).astype(BF16)
    lane = lax.broadcasted_iota(jnp.int32, (CHUNK, 128), 1)
    mixes = []
    for c in range(tm // CHUNK):
        vc = vb[c * CHUNK:(c + 1) * CHUNK]
        halves = []
        for j in range(2):
            vj = vc[:, j * 128:(j + 1) * 128]
            m0 = jnp.dot(ws_ref[2 * j], vj, preferred_element_type=F32)
            m1 = jnp.dot(ws_ref[2 * j + 1], vj, preferred_element_type=F32)
            halves.append(jnp.where(lane < A_GD, m0, m1))
        mixes.append(jnp.concatenate(halves, axis=1) + bias_ref[...])
    ya = u * jnp.concatenate(mixes, axis=0)

    bg = p_ref[:, OFF_BB:OFF_BC].astype(F32)
    z = p_ref[:, OFF_BC:OFF_BX].astype(F32) * p_ref[:, OFF_BX:OFF_Q].astype(F32)
    zp = (pc_ref[...].astype(F32) * px_ref[...].astype(F32))[15:16]
    zn = (nc_ref[...].astype(F32) * nx_ref[...].astype(F32))[0:1]
    zp = jnp.where(is_start, 0.0, zp)
    zn = jnp.where(is_end, 0.0, zn)
    row = lax.broadcasted_iota(jnp.int32, z.shape, 0)
    z_prev = jnp.where(row == 0, zp, pltpu.roll(z, 1, 0))
    z_next = jnp.where(row == tm - 1, zn, pltpu.roll(z, tm - 1, 0))
    yb = bg * (z_prev * wconv_ref[0:1] + z * wconv_ref[1:2] + z_next * wconv_ref[2:3])
    yab_ref[...] = jnp.concatenate([ya, yb], axis=1).astype(BF16)

    cos = cos_ref[...]
    sin = sin_ref[...]
    bd = bd_ref[...]
    q = _rope(_group_rms(p_ref[:, OFF_Q:OFF_K].astype(F32), gq_ref[...], bd), cos, sin)
    q = q * (QK_DIM ** -0.5 * LOG2E)
    lane5 = lax.broadcasted_iota(jnp.int32, q.shape, 1) % 128
    qa_ref[...] = jnp.where(lane5 < QK_DIM, q, 0.0).astype(BF16)
    qb_ref[...] = jnp.where(lane5 >= QK_DIM, q, 0.0).astype(BF16)
    k = _rope(_group_rms(p_ref[:, OFF_K:OFF_V].astype(F32), gk_ref[...], bd), cos, sin)
    kk_ref[...] = k.astype(BF16)


def _mixers(p, cos_t, sin_t, gv, ws_bf, bias_t, wconv, gq, gk, bd):
    tm = TM_MIX
    nt = NR // tm
    hb = tm // 16
    last_hb = NR // 16 - 1
    pos_blocks = L // tm

    def tab_map(i):
        return (jnp.where(i < NL // tm, i % pos_blocks, pos_blocks), 0)

    prev_map_c = lambda i: (jnp.maximum(i * hb - 1, 0), OFF_BC // 256)
    prev_map_x = lambda i: (jnp.maximum(i * hb - 1, 0), OFF_BX // 256)
    next_map_c = lambda i: (jnp.minimum((i + 1) * hb, last_hb), OFF_BC // 256)
    next_map_x = lambda i: (jnp.minimum((i + 1) * hb, last_hb), OFF_BX // 256)
    const2 = lambda i: (0, 0)
    outs = pl.pallas_call(
        _mixers_kernel,
        out_shape=[jax.ShapeDtypeStruct((NR, 2 * A_WIDTH), BF16),
                   jax.ShapeDtypeStruct((NR, 512), BF16),
                   jax.ShapeDtypeStruct((NR, 512), BF16),
                   jax.ShapeDtypeStruct((NR, 512), BF16)],
        grid=(nt,),
        in_specs=[pl.BlockSpec((tm, D_IN), lambda i: (i, 0)),
                  pl.BlockSpec((16, 256), prev_map_c), pl.BlockSpec((16, 256), prev_map_x),
                  pl.BlockSpec((16, 256), next_map_c), pl.BlockSpec((16, 256), next_map_x),
                  pl.BlockSpec((tm, 128), tab_map), pl.BlockSpec((tm, 128), tab_map),
                  pl.BlockSpec((1, A_WIDTH), const2),
                  pl.BlockSpec((A_GROUPS, CHUNK, CHUNK), lambda i: (0, 0, 0)),
                  pl.BlockSpec((CHUNK, A_WIDTH), const2),
                  pl.BlockSpec((8, B_WIDTH), const2),
                  pl.BlockSpec((1, 512), const2), pl.BlockSpec((1, 512), const2),
                  pl.BlockSpec((512, 512), const2)],
        out_specs=[pl.BlockSpec((tm, 512), lambda i: (i, 0))] * 4,
        compiler_params=_cparams(("parallel",), 48),
        name="mixers",
    )(p, p, p, p, p, cos_t, sin_t, gv, ws_bf, bias_t, wconv, gq, gk, bd)
    return outs


def _attn_kernel(lam_ref, qa_ref, qb_ref, *rest, n_seg, coef):
    kv_refs = rest[:2 * n_seg]
    gsub_ref, o_ref, k_scr, v_scr = rest[2 * n_seg:]

    @pl.when(pl.program_id(2) == 0)
    def _():
        off = 0
        for s in range(n_seg):
            n = kv_refs[s].shape[0]
            k_scr[off:off + n, :] = kv_refs[s][...]
            v_scr[off:off + n, :] = kv_refs[n_seg + s][...]
            off += n

    k = k_scr[...]
    s1 = lax.dot_general(qa_ref[...], k, _DN_T, preferred_element_type=F32)
    s2 = lax.dot_general(qb_ref[...], k, _DN_T, preferred_element_type=F32)
    p1 = jnp.exp2(s1 - jnp.max(s1, axis=-1, keepdims=True))
    p2 = jnp.exp2(s2 - jnp.max(s2, axis=-1, keepdims=True))
    a1 = 1.0 / jnp.sum(p1, axis=-1, keepdims=True)
    a2 = lam_ref[0] / jnp.sum(p2, axis=-1, keepdims=True)
    w = (p1 * a1 - p2 * a2).astype(BF16)
    o = jnp.dot(w, v_scr[...], preferred_element_type=F32)
    ms = jnp.mean(o * o, axis=-1, keepdims=True)
    o_ref[...] = (o * lax.rsqrt(ms + EPS) * gsub_ref[...] * coef).astype(o_ref.dtype)


def _attention(lam, qa, qb, kk, p, gsub, coef, *, ctx_queries):
    vcol = OFF_V // 128
    if ctx_queries:
        nq, lk, n_seg = 1, CTX, 1
        q_map = lambda b, h, qi: (NL // TQ + b, h)
        kv_specs = [pl.BlockSpec((CTX, 128), lambda b, h, qi: (NL // CTX + b, h)),
                    pl.BlockSpec((CTX, 128), lambda b, h, qi: (NL // CTX + b, vcol + h))]
        kv_args = [kk, p]
        rows = NC
        o_map = lambda b, h, qi: (b, h)
    else:
        nq, lk, n_seg = L // TQ, CTX + L, 2
        q_map = lambda b, h, qi: (b * (L // TQ) + qi, h)
        kv_specs = [pl.BlockSpec((CTX, 128), lambda b, h, qi: (NL // CTX + b, h)),
                    pl.BlockSpec((L, 128), lambda b, h, qi: (b, h)),
                    pl.BlockSpec((CTX, 128), lambda b, h, qi: (NL // CTX + b, vcol + h)),
                    pl.BlockSpec((L, 128), lambda b, h, qi: (b, vcol + h))]
        kv_args = [kk, kk, p, p]
        rows = NL
        o_map = lambda b, h, qi: (b * (L // TQ) + qi, h)
    return pl.pallas_call(
        functools.partial(_attn_kernel, n_seg=n_seg, coef=coef),
        out_shape=jax.ShapeDtypeStruct((rows, C_WIDTH), BF16),
        grid=(B, HEADS, nq),
        in_specs=[pl.BlockSpec(memory_space=pltpu.SMEM),
                  pl.BlockSpec((TQ, 128), q_map), pl.BlockSpec((TQ, 128), q_map)]
                 + kv_specs + [pl.BlockSpec((1, V_DIM), lambda b, h, qi: (0, 0))],
        out_specs=pl.BlockSpec((TQ, 128), o_map),
        scratch_shapes=[pltpu.VMEM((lk, 128), BF16), pltpu.VMEM((lk, 128), BF16)],
        compiler_params=_cparams(("parallel", "parallel", "arbitrary"), 48),
        name="attn_ctx" if ctx_queries else "attn_lat",
    )(lam, qa, qb, *kv_args, gsub)


def _out_router_kernel(x_ref, yab_ref, yc_ref, wo_ref, g1_ref, g2n_ref, sh2_ref, sc2_ref, wrh_ref, wrl_ref, br_ref,
                       xo_ref, h2_ref, idx_ref, gate_ref):
    y = (jnp.dot(yab_ref[...], wo_ref[0:2 * A_WIDTH, :], preferred_element_type=F32)
         + jnp.dot(yc_ref[...], wo_ref[2 * A_WIDTH:, :], preferred_element_type=F32))
    x = x_ref[...] + g1_ref[...] * y
    xo_ref[...] = x
    h2 = _rms_mod(x, g2n_ref[...], sc2_ref[...], sh2_ref[...])
    hi = h2.astype(BF16)
    h2_ref[...] = hi
    lo = (h2 - hi.astype(F32)).astype(BF16)
    wh = wrh_ref[...]
    z = (lax.dot_general(wh, hi, _DN_T, preferred_element_type=F32)
         + lax.dot_general(wh, lo, _DN_T, preferred_element_type=F32)
         + lax.dot_general(wrl_ref[...], hi, _DN_T, preferred_element_type=F32))
    scores = jax.nn.sigmoid(z)
    work = scores + br_ref[...]
    eio = lax.broadcasted_iota(jnp.int32, work.shape, 0)
    idxs, sels = [], []
    for _ in range(TOP_K):
        m = jnp.max(work, axis=0, keepdims=True)
        idx = jnp.min(jnp.where(work == m, eio, N_EXPERTS), axis=0, keepdims=True)
        hit = eio == idx
        sels.append(jnp.sum(jnp.where(hit, scores, 0.0), axis=0, keepdims=True))
        idxs.append(idx)
        work = jnp.where(hit, -jnp.inf, work)
    sel = jnp.concatenate(sels, axis=0)
    idx_ref[...] = jnp.concatenate(idxs, axis=0)
    gate_ref[...] = sel / jnp.sum(sel, axis=0, keepdims=True) * ROUTED_SCALE


def _out_router(xa, yab, yc, wo_bf, g2n, mod3, wr_hi, wr_lo, br, n_rows):
    tm = TM_OUT
    const2 = lambda i: (0, 0)
    row = lambda i: (i, 0)
    return pl.pallas_call(
        _out_router_kernel,
        out_shape=[jax.ShapeDtypeStruct((n_rows, D), F32),
                   jax.ShapeDtypeStruct((n_rows, D), BF16),
                   jax.ShapeDtypeStruct((TOP_K, n_rows), jnp.int32),
                   jax.ShapeDtypeStruct((TOP_K, n_rows), F32)],
        grid=(n_rows // tm,),
        in_specs=[pl.BlockSpec((tm, D), row), pl.BlockSpec((tm, 512), row), pl.BlockSpec((tm, 512), row),
                  pl.BlockSpec((D, D), const2),
                  _mod_spec(tm, 2),
                  pl.BlockSpec((1, D), const2), _mod_spec(tm, 3), _mod_spec(tm, 4),
                  pl.BlockSpec((N_EXPERTS, D), const2), pl.BlockSpec((N_EXPERTS, D), const2),
                  pl.BlockSpec((N_EXPERTS, 1), const2)],
        out_specs=[pl.BlockSpec((tm, D), row), pl.BlockSpec((tm, D), row),
                   pl.BlockSpec((TOP_K, tm), lambda i: (0, i)), pl.BlockSpec((TOP_K, tm), lambda i: (0, i))],
        compiler_params=_cparams(("parallel",), 48),
        name="out_router",
    )(xa, yab, yc, wo_bf, mod3, g2n.reshape(1, D), mod3, mod3, wr_hi, wr_lo, br.reshape(N_EXPERTS, 1))


def _experts_kernel(te_ref, nu_ref, x_ref, wg_ref, wu_ref, wd_ref, y_ref):
    i = pl.program_id(0)

    @pl.when(i < nu_ref[0])
    def _():
        x = x_ref[...]
        a = jnp.dot(x, wg_ref[...], preferred_element_type=F32)
        b = jnp.dot(x, wu_ref[...], preferred_element_type=F32)
        hid = (a * jax.nn.sigmoid(a) * b).astype(BF16)
        y_ref[...] = jnp.dot(hid, wd_ref[...], preferred_element_type=F32).astype(y_ref.dtype)

    @pl.when(i >= nu_ref[0])
    def _():
        y_ref[...] = jnp.zeros_like(y_ref)


def _experts(tile_expert, n_used, x_sorted, wg_bf, wu_bf, wd_bf):
    tm = TM_X
    r_pad = x_sorted.shape[0]
    nt = r_pad // tm
    xrow = lambda i, te, nu: (jnp.minimum(i, nu[0] - 1), 0)
    grid_spec = pltpu.PrefetchScalarGridSpec(
        num_scalar_prefetch=2,
        grid=(nt,),
        in_specs=[pl.BlockSpec((tm, D), xrow),
                  pl.BlockSpec((None, D, D_EXPERT), lambda i, te, nu: (te[i], 0, 0)),
                  pl.BlockSpec((None, D, D_EXPERT), lambda i, te, nu: (te[i], 0, 0)),
                  pl.BlockSpec((None, D_EXPERT, D), lambda i, te, nu: (te[i], 0, 0))],
        out_specs=pl.BlockSpec((tm, D), lambda i, te, nu: (i, 0)),
    )
    return pl.pallas_call(
        _experts_kernel,
        out_shape=jax.ShapeDtypeStruct((r_pad, D), BF16),
        grid_spec=grid_spec,
        compiler_params=_cparams(("arbitrary",), 48),
        name="experts",
    )(tile_expert, n_used, x_sorted, wg_bf, wu_bf, wd_bf)


def _shared_kernel(x_ref, h2_ref, yg_ref, gate_ref, wg_ref, wu_ref, wd_ref, g2_ref, o_ref):
    h = h2_ref[...]
    a = jnp.dot(h, wg_ref[...], preferred_element_type=F32)
    b = jnp.dot(h, wu_ref[...], preferred_element_type=F32)
    hid = (a * jax.nn.sigmoid(a) * b).astype(BF16)
    f = jnp.dot(hid, wd_ref[...], preferred_element_type=F32)
    gate = gate_ref[...]
    for k in range(TOP_K):
        f = f + gate[:, k:k + 1] * yg_ref[:, k * D:(k + 1) * D].astype(F32)
    o_ref[...] = x_ref[...] + g2_ref[...] * f


def _shared_residual(xa, h2, yg, gates, wsg_bf, wsu_bf, wsd_bf, mod3, n_rows):
    tm = TM_F
    row = lambda i: (i, 0)
    const2 = lambda i: (0, 0)
    return pl.pallas_call(
        _shared_kernel,
        out_shape=jax.ShapeDtypeStruct((n_rows, D), F32),
        grid=(n_rows // tm,),
        in_specs=[pl.BlockSpec((tm, D), row), pl.BlockSpec((tm, D), row), pl.BlockSpec((tm, TOP_K * D), row),
                  pl.BlockSpec((tm, TOP_K), row),
                  pl.BlockSpec((D, D_SHARED), const2), pl.BlockSpec((D, D_SHARED), const2),
                  pl.BlockSpec((D_SHARED, D), const2), _mod_spec(tm, 5)],
        out_specs=pl.BlockSpec((tm, D), row),
        compiler_params=_cparams(("parallel",), 48),
        name="shared_residual",
    )(xa, h2, yg, gates, wsg_bf, wsu_bf, wsd_bf, mod3)


def _route_positions(idx_t, n_rows):
    tm = TM_X
    idx = idx_t.T
    mask = jnp.any(idx[:, :, None] == jnp.arange(N_EXPERTS, dtype=jnp.int32), axis=1).astype(jnp.int32)
    csum = jnp.cumsum(mask, axis=0)
    counts = csum[-1]
    padded = ((counts + tm - 1) // tm) * tm
    ends = jnp.cumsum(padded)
    offs = ends - padded
    pos = jnp.take_along_axis(offs[None, :] + csum - mask, idx, axis=1)
    r_pad = n_rows * TOP_K + N_EXPERTS * tm
    nt = r_pad // tm
    tile_ids = jnp.arange(nt, dtype=jnp.int32)
    tile_expert = jnp.sum((ends // tm)[None, :] <= tile_ids[:, None], axis=1)
    tile_expert = jnp.minimum(tile_expert, N_EXPERTS - 1).astype(jnp.int32)
    n_used = (ends[-1] // tm).astype(jnp.int32).reshape(1)
    return pos, tile_expert, n_used, r_pad


def _q_perm():
    perm = np.empty(512, np.int32)
    for h in range(HEADS):
        for half in range(2):
            for d in range(QK_DIM):
                perm[h * 128 + half * QK_DIM + d] = half * HEADS * QK_DIM + h * QK_DIM + d
    return perm


def _rope_tables():
    t = jnp.arange(L)
    row = (t // GRID_W).astype(F32)
    col = (t % GRID_W).astype(F32)
    n_freq = QK_DIM // 4
    inv = ROPE_BASE ** (-jnp.arange(n_freq, dtype=F32) / n_freq)
    ar = row[:, None] * inv
    ac = col[:, None] * inv
    cos64 = jnp.concatenate([jnp.cos(ar), jnp.cos(ar), jnp.cos(ac), jnp.cos(ac)], axis=1)
    sin64 = jnp.concatenate([-jnp.sin(ar), jnp.sin(ar), -jnp.sin(ac), jnp.sin(ac)], axis=1)
    cos_t = jnp.concatenate([jnp.tile(cos64, (1, 2)), jnp.ones((TM_MIX, 128), F32)], axis=0)
    sin_t = jnp.concatenate([jnp.tile(sin64, (1, 2)), jnp.zeros((TM_MIX, 128), F32)], axis=0)
    return cos_t, sin_t


def _split_bf16(w):
    hi = w.astype(BF16)
    return hi, (w - hi.astype(F32)).astype(BF16)


def kernel(x, c, ctx, c_ctx, w_ada, b_ada, g_norm1, g_norm2, w_in, w_out, g_v, w_s, b_s, w_conv, g_q, g_k,
           lam_q1, lam_k1, lam_q2, lam_k2, g_sub, w_router, b_router, w_gate, w_up, w_down,
           ws_gate, ws_up, ws_down):
    xa = jnp.concatenate([x.reshape(NL, D), ctx.reshape(NC, D)], axis=0)
    cc = jnp.concatenate([c, c_ctx[None, :], jnp.zeros((MOD_ROWS - B - 1, D), F32)], axis=0)
    mod = _ada(cc, w_ada, b_ada)
    cos_t, sin_t = _rope_tables()
    perm = _q_perm()
    col_perm = np.concatenate([np.arange(OFF_Q), OFF_Q + perm, OFF_K + perm, np.arange(OFF_V, D_IN)])
    bd = jnp.asarray(np.kron(np.eye(8, dtype=np.float32), np.full((64, 64), 1.0 / 64, np.float32)), BF16)

    for l in range(DEPTH):
        last = l == DEPTH - 1
        lam_init = 0.8 - 0.6 * math.exp(-0.3 * l)
        lam = (jnp.exp(jnp.sum(lam_q1[l] * lam_k1[l])) - jnp.exp(jnp.sum(lam_q2[l] * lam_k2[l]))
               + lam_init).reshape(1).astype(F32)
        mod3 = mod[l].reshape(MOD_ROWS, 1, 6 * D)
        w_in_bf = w_in[l][:, col_perm].astype(BF16)
        p = _in_proj(xa, g_norm1[l], mod3, w_in_bf)
        bias_t = jnp.repeat(b_s[l].T, A_GD, axis=1)
        wconv = jnp.concatenate([w_conv[l], jnp.zeros((5, B_WIDTH), F32)], axis=0)
        yab, qa, qb, kk = _mixers(p, cos_t, sin_t, g_v[l].reshape(1, A_WIDTH), w_s[l].astype(BF16), bias_t, wconv,
                                  jnp.tile(g_q[l], 8).reshape(1, 512), jnp.tile(g_k[l], 8).reshape(1, 512), bd)
        gsub = g_sub[l].reshape(1, V_DIM)
        coef = 1.0 - lam_init
        yc = _attention(lam, qa, qb, kk, p, gsub, coef, ctx_queries=False)
        if last:
            n_rows = NL
        else:
            n_rows = NR
            yc = jnp.concatenate([yc, _attention(lam, qa, qb, kk, p, gsub, coef, ctx_queries=True)], axis=0)
        wr_hi, wr_lo = _split_bf16(w_router[l].T)
        xa, h2, idx_t, gate_t = _out_router(xa, yab, yc, w_out[l].astype(BF16), g_norm2[l], mod3,
                                            wr_hi, wr_lo, b_router[l], n_rows)
        pos, tile_expert, n_used, r_pad = _route_positions(idx_t, n_rows)
        flat = pos.reshape(-1)
        src_tok = jnp.zeros((r_pad,), jnp.int32).at[flat].set(
            jnp.arange(n_rows * TOP_K, dtype=jnp.int32) // TOP_K)
        x_sorted = jnp.take(h2, src_tok, axis=0)
        y_sorted = _experts(tile_expert, n_used, x_sorted, w_gate[l].astype(BF16), w_up[l].astype(BF16),
                            w_down[l].astype(BF16))
        yg = jnp.take(y_sorted, flat, axis=0).reshape(n_rows, TOP_K * D)
        xa = _shared_residual(xa, h2, yg, gate_t.T, ws_gate[l].astype(BF16), ws_up[l].astype(BF16),
                              ws_down[l].astype(BF16), mod3, n_rows)
    return xa.reshape(B, L, D)
```

```python
import functools
import math

import numpy as np
import jax
import jax.numpy as jnp
from jax import lax
from jax.experimental import pallas as pl
from jax.experimental.pallas import tpu as pltpu
from jax.experimental.pallas import tpu_sc as plsc

F32 = jnp.float32
BF16 = jnp.bfloat16

D = 1024
B = 8
L = 2048
DEPTH = 2
GRID_W = 64
CTX = 256
A_WIDTH = 256
A_GROUPS = 4
A_GD = 64
CHUNK = 128
B_WIDTH = 256
C_WIDTH = 512
HEADS = 4
V_DIM = 128
QK_DIM = 64
ROPE_BASE = 10000.0
OFF_BB = 512
OFF_BC = 768
OFF_BX = 1024
OFF_Q = 1280
OFF_K = 1792
OFF_V = 2304
D_IN = 2816
N_EXPERTS = 64
TOP_K = 8
D_EXPERT = 256
D_SHARED = 256
ROUTED_SCALE = 2.5
EPS = 1e-6

NL = B * L
NC = B * CTX
NR = NL + NC
MOD_ROWS = 16
LOG2E = 1.4426950408889634

TM_IN = 512
TM_MIX = 256
TQ = 256
TM_OUT = 256
TM_X = 512
TM_F = 256

_DN_T = (((1,), (1,)), ((), ()))


def _cparams(sem, vmem_mb=None):
    kw = dict(dimension_semantics=sem)
    if vmem_mb is not None:
        kw["vmem_limit_bytes"] = vmem_mb * 1024 * 1024
    return pltpu.CompilerParams(**kw)


def _mod_row(i, tm):
    return jnp.where(i < NL // tm, i // (L // tm), B)


def _mod_spec(tm, chunk):
    return pl.BlockSpec((None, 1, D), lambda i: (_mod_row(i, tm), 0, chunk))


def _ada_kernel(c_ref, w_ref, b_ref, o_ref):
    c = c_ref[...]
    cs = c * jax.nn.sigmoid(c)
    o_ref[...] = jnp.dot(cs, w_ref[...], preferred_element_type=F32,
                         precision=lax.Precision.HIGHEST) + b_ref[...]


def _ada(cc, w_ada, b_ada):
    nb = 6
    return pl.pallas_call(
        _ada_kernel,
        out_shape=jax.ShapeDtypeStruct((DEPTH, MOD_ROWS, 6 * D), F32),
        grid=(DEPTH, nb),
        in_specs=[pl.BlockSpec((MOD_ROWS, D), lambda l, j: (0, 0)),
                  pl.BlockSpec((None, D, D), lambda l, j: (l, 0, j)),
                  pl.BlockSpec((None, 1, D), lambda l, j: (l, 0, j))],
        out_specs=pl.BlockSpec((None, MOD_ROWS, D), lambda l, j: (l, 0, j)),
        compiler_params=_cparams(("arbitrary", "arbitrary"), 40),
        name="ada_mod",
    )(cc, w_ada, b_ada.reshape(DEPTH, 1, 6 * D))


def _rms_mod(x, g, sc, sh):
    ms = jnp.mean(x * x, axis=-1, keepdims=True)
    return x * lax.rsqrt(ms + EPS) * g * (1.0 + sc) + sh


def _in_proj_kernel(x_ref, g_ref, sh_ref, sc_ref, w_ref, o_ref):
    h = _rms_mod(x_ref[...], g_ref[...], sc_ref[...], sh_ref[...])
    o_ref[...] = jnp.dot(h.astype(BF16), w_ref[...], preferred_element_type=F32).astype(o_ref.dtype)


def _in_proj(xa, g, mod3, w_bf):
    tm = TM_IN
    return pl.pallas_call(
        _in_proj_kernel,
        out_shape=jax.ShapeDtypeStruct((NR, D_IN), BF16),
        grid=(NR // tm,),
        in_specs=[pl.BlockSpec((tm, D), lambda i: (i, 0)),
                  pl.BlockSpec((1, D), lambda i: (0, 0)),
                  _mod_spec(tm, 0), _mod_spec(tm, 1),
                  pl.BlockSpec((D, D_IN), lambda i: (0, 0))],
        out_specs=pl.BlockSpec((tm, D_IN), lambda i: (i, 0)),
        compiler_params=_cparams(("parallel",), 48),
        name="in_proj",
    )(xa, g.reshape(1, D), mod3, mod3, w_bf)


def _group_rms(t, g, bd):
    sq = t * t
    hi = sq.astype(BF16)
    lo = (sq - hi.astype(F32)).astype(BF16)
    ms = (jnp.dot(hi, bd, preferred_element_type=F32) + jnp.dot(lo, bd, preferred_element_type=F32))
    return t * lax.rsqrt(ms + EPS) * g


def _rope(t, cos, sin):
    w = t.shape[1]
    lane = lax.broadcasted_iota(jnp.int32, t.shape, 1)
    first = (lane % 32) < 16
    partner = jnp.where(first, pltpu.roll(t, w - 16, 1), pltpu.roll(t, 16, 1))
    cos4 = jnp.concatenate([cos] * (w // 128), axis=1)
    sin4 = jnp.concatenate([sin] * (w // 128), axis=1)
    return t * cos4 + partner * sin4


def _mixers_kernel(p_ref, pc_ref, px_ref, nc_ref, nx_ref, cos_ref, sin_ref, gv_ref, ws_ref, bias_ref,
                   wconv_ref, gq_ref, gk_ref, bd_ref, yab_ref, qa_ref, qb_ref, kk_ref):
    tm = TM_MIX
    i = pl.program_id(0)
    tiles_per_seq = L // tm
    is_lat = i < NL // tm
    is_start = jnp.logical_or(jnp.logical_not(is_lat), i % tiles_per_seq == 0)
    is_end = jnp.logical_or(jnp.logical_not(is_lat), i % tiles_per_seq == tiles_per_seq - 1)

    uv = p_ref[:, 0:2 * A_WIDTH].astype(F32)
    uv = 0.5 * uv * (1.0 + lax.erf(uv * (2.0 ** -0.5)))
    u = uv[:, :A_WIDTH]
    v = uv[:, A_WIDTH:]
    ms = jnp.mean(v * v, axis=-1, keepdims=True)
    vb = (v * lax.rsqrt(ms + EPS) * gv_ref[...]).astype(BF16)
    lane = lax.broadcasted_iota(jnp.int32, (CHUNK, 128), 1)
    mixes = []
    for c in range(tm // CHUNK):
        vc = vb[c * CHUNK:(c + 1) * CHUNK]
        halves = []
        for j in range(2):
            vj = vc[:, j * 128:(j + 1) * 128]
            m0 = jnp.dot(ws_ref[2 * j], vj, preferred_element_type=F32)
            m1 = jnp.dot(ws_ref[2 * j + 1], vj, preferred_element_type=F32)
            halves.append(jnp.where(lane < A_GD, m0, m1))
        mixes.append(jnp.concatenate(halves, axis=1) + bias_ref[...])
    ya = u * jnp.concatenate(mixes, axis=0)

    bg = p_ref[:, OFF_BB:OFF_BC].astype(F32)
    z = p_ref[:, OFF_BC:OFF_BX].astype(F32) * p_ref[:, OFF_BX:OFF_Q].astype(F32)
    zp = (pc_ref[...].astype(F32) * px_ref[...].astype(F32))[15:16]
    zn = (nc_ref[...].astype(F32) * nx_ref[...].astype(F32))[0:1]
    zp = jnp.where(is_start, 0.0, zp)
    zn = jnp.where(is_end, 0.0, zn)
    row = lax.broadcasted_iota(jnp.int32, z.shape, 0)
    z_prev = jnp.where(row == 0, zp, pltpu.roll(z, 1, 0))
    z_next = jnp.where(row == tm - 1, zn, pltpu.roll(z, tm - 1, 0))
    yb = bg * (z_prev * wconv_ref[0:1] + z * wconv_ref[1:2] + z_next * wconv_ref[2:3])
    yab_ref[...] = jnp.concatenate([ya, yb], axis=1).astype(BF16)

    cos = cos_ref[...]
    sin = sin_ref[...]
    bd = bd_ref[...]
    q = _rope(_group_rms(p_ref[:, OFF_Q:OFF_K].astype(F32), gq_ref[...], bd), cos, sin)
    q = q * (QK_DIM ** -0.5 * LOG2E)
    lane5 = lax.broadcasted_iota(jnp.int32, q.shape, 1) % 128
    qa_ref[...] = jnp.where(lane5 < QK_DIM, q, 0.0).astype(BF16)
    qb_ref[...] = jnp.where(lane5 >= QK_DIM, q, 0.0).astype(BF16)
    k = _rope(_group_rms(p_ref[:, OFF_K:OFF_V].astype(F32), gk_ref[...], bd), cos, sin)
    kk_ref[...] = k.astype(BF16)


def _mixers(p, cos_t, sin_t, gv, ws_bf, bias_t, wconv, gq, gk, bd):
    tm = TM_MIX
    nt = NR // tm
    hb = tm // 16
    last_hb = NR // 16 - 1
    pos_blocks = L // tm

    def tab_map(i):
        return (jnp.where(i < NL // tm, i % pos_blocks, pos_blocks), 0)

    prev_map_c = lambda i: (jnp.maximum(i * hb - 1, 0), OFF_BC // 256)
    prev_map_x = lambda i: (jnp.maximum(i * hb - 1, 0), OFF_BX // 256)
    next_map_c = lambda i: (jnp.minimum((i + 1) * hb, last_hb), OFF_BC // 256)
    next_map_x = lambda i: (jnp.minimum((i + 1) * hb, last_hb), OFF_BX // 256)
    const2 = lambda i: (0, 0)
    outs = pl.pallas_call(
        _mixers_kernel,
        out_shape=[jax.ShapeDtypeStruct((NR, 2 * A_WIDTH), BF16),
                   jax.ShapeDtypeStruct((NR, 512), BF16),
                   jax.ShapeDtypeStruct((NR, 512), BF16),
                   jax.ShapeDtypeStruct((NR, 512), BF16)],
        grid=(nt,),
        in_specs=[pl.BlockSpec((tm, D_IN), lambda i: (i, 0)),
                  pl.BlockSpec((16, 256), prev_map_c), pl.BlockSpec((16, 256), prev_map_x),
                  pl.BlockSpec((16, 256), next_map_c), pl.BlockSpec((16, 256), next_map_x),
                  pl.BlockSpec((tm, 128), tab_map), pl.BlockSpec((tm, 128), tab_map),
                  pl.BlockSpec((1, A_WIDTH), const2),
                  pl.BlockSpec((A_GROUPS, CHUNK, CHUNK), lambda i: (0, 0, 0)),
                  pl.BlockSpec((CHUNK, A_WIDTH), const2),
                  pl.BlockSpec((8, B_WIDTH), const2),
                  pl.BlockSpec((1, 512), const2), pl.BlockSpec((1, 512), const2),
                  pl.BlockSpec((512, 512), const2)],
        out_specs=[pl.BlockSpec((tm, 512), lambda i: (i, 0))] * 4,
        compiler_params=_cparams(("parallel",), 48),
        name="mixers",
    )(p, p, p, p, p, cos_t, sin_t, gv, ws_bf, bias_t, wconv, gq, gk, bd)
    return outs


def _attn_kernel(lam_ref, qa_ref, qb_ref, *rest, n_seg, coef):
    kv_refs = rest[:2 * n_seg]
    gsub_ref, o_ref, k_scr, v_scr = rest[2 * n_seg:]

    @pl.when(pl.program_id(2) == 0)
    def _():
        off = 0
        for s in range(n_seg):
            n = kv_refs[s].shape[0]
            k_scr[off:off + n, :] = kv_refs[s][...]
            v_scr[off:off + n, :] = kv_refs[n_seg + s][...]
            off += n

    k = k_scr[...]
    s1 = lax.dot_general(qa_ref[...], k, _DN_T, preferred_element_type=F32)
    s2 = lax.dot_general(qb_ref[...], k, _DN_T, preferred_element_type=F32)
    p1 = jnp.exp2(s1 - jnp.max(s1, axis=-1, keepdims=True))
    p2 = jnp.exp2(s2 - jnp.max(s2, axis=-1, keepdims=True))
    a1 = 1.0 / jnp.sum(p1, axis=-1, keepdims=True)
    a2 = lam_ref[0] / jnp.sum(p2, axis=-1, keepdims=True)
    w = (p1 * a1 - p2 * a2).astype(BF16)
    o = jnp.dot(w, v_scr[...], preferred_element_type=F32)
    ms = jnp.mean(o * o, axis=-1, keepdims=True)
    o_ref[...] = (o * lax.rsqrt(ms + EPS) * gsub_ref[...] * coef).astype(o_ref.dtype)


def _attention(lam, qa, qb, kk, p, gsub, coef, *, ctx_queries):
    vcol = OFF_V // 128
    if ctx_queries:
        nq, lk, n_seg = 1, CTX, 1
        q_map = lambda b, h, qi: (NL // TQ + b, h)
        kv_specs = [pl.BlockSpec((CTX, 128), lambda b, h, qi: (NL // CTX + b, h)),
                    pl.BlockSpec((CTX, 128), lambda b, h, qi: (NL // CTX + b, vcol + h))]
        kv_args = [kk, p]
        rows = NC
        o_map = lambda b, h, qi: (b, h)
    else:
        nq, lk, n_seg = L // TQ, CTX + L, 2
        q_map = lambda b, h, qi: (b * (L // TQ) + qi, h)
        kv_specs = [pl.BlockSpec((CTX, 128), lambda b, h, qi: (NL // CTX + b, h)),
                    pl.BlockSpec((L, 128), lambda b, h, qi: (b, h)),
                    pl.BlockSpec((CTX, 128), lambda b, h, qi: (NL // CTX + b, vcol + h)),
                    pl.BlockSpec((L, 128), lambda b, h, qi: (b, vcol + h))]
        kv_args = [kk, kk, p, p]
        rows = NL
        o_map = lambda b, h, qi: (b * (L // TQ) + qi, h)
    return pl.pallas_call(
        functools.partial(_attn_kernel, n_seg=n_seg, coef=coef),
        out_shape=jax.ShapeDtypeStruct((rows, C_WIDTH), BF16),
        grid=(B, HEADS, nq),
        in_specs=[pl.BlockSpec(memory_space=pltpu.SMEM),
                  pl.BlockSpec((TQ, 128), q_map), pl.BlockSpec((TQ, 128), q_map)]
                 + kv_specs + [pl.BlockSpec((1, V_DIM), lambda b, h, qi: (0, 0))],
        out_specs=pl.BlockSpec((TQ, 128), o_map),
        scratch_shapes=[pltpu.VMEM((lk, 128), BF16), pltpu.VMEM((lk, 128), BF16)],
        compiler_params=_cparams(("parallel", "parallel", "arbitrary"), 48),
        name="attn_ctx" if ctx_queries else "attn_lat",
    )(lam, qa, qb, *kv_args, gsub)


def _pack_rows(t, out_ref):
    half = D // 2
    lo = lax.bitcast_convert_type(t[:, :half].astype(BF16).astype(F32), jnp.uint32) >> 16
    hi = lax.bitcast_convert_type(t[:, half:].astype(BF16).astype(F32), jnp.uint32) & jnp.uint32(0xFFFF0000)
    w = lo | hi
    rows = t.shape[0]
    for j in range(4):
        out_ref[pl.ds(j, rows, stride=4), :] = w[:, j * 128:(j + 1) * 128]


def _unpack_piece(w):
    lo = lax.bitcast_convert_type(w << 16, F32)
    hi = lax.bitcast_convert_type(w & jnp.uint32(0xFFFF0000), F32)
    return lo, hi


def _unpack_rows(ref, rows, lead=None):
    los, his = [], []
    for j in range(4):
        w = ref[pl.ds(j, rows, stride=4), :] if lead is None else ref[lead, pl.ds(j, rows, stride=4), :]
        lo, hi = _unpack_piece(w)
        los.append(lo)
        his.append(hi)
    return jnp.concatenate(los, axis=1), jnp.concatenate(his, axis=1)


def _out_router_kernel(x_ref, yab_ref, yc_ref, wo_ref, g1_ref, g2n_ref, sh2_ref, sc2_ref, wrh_ref, wrl_ref, br_ref,
                       xo_ref, h2_ref, h2p_ref, idx_ref, gate_ref, rank_ref, cnt_ref, run_ref):
    tm = TM_OUT

    @pl.when(pl.program_id(0) == 0)
    def _():
        run_ref[...] = jnp.zeros_like(run_ref)

    y = (jnp.dot(yab_ref[...], wo_ref[0:2 * A_WIDTH, :], preferred_element_type=F32)
         + jnp.dot(yc_ref[...], wo_ref[2 * A_WIDTH:, :], preferred_element_type=F32))
    x = x_ref[...] + g1_ref[...] * y
    xo_ref[...] = x
    h2 = _rms_mod(x, g2n_ref[...], sc2_ref[...], sh2_ref[...])
    hi = h2.astype(BF16)
    h2_ref[...] = hi
    _pack_rows(h2, h2p_ref)
    lo = (h2 - hi.astype(F32)).astype(BF16)
    wh = wrh_ref[...]
    z = (lax.dot_general(wh, hi, _DN_T, preferred_element_type=F32)
         + lax.dot_general(wh, lo, _DN_T, preferred_element_type=F32)
         + lax.dot_general(wrl_ref[...], hi, _DN_T, preferred_element_type=F32))
    scores = jax.nn.sigmoid(z)
    work = scores + br_ref[...]
    eio = lax.broadcasted_iota(jnp.int32, work.shape, 0)
    idxs, sels, hits = [], [], []
    for _ in range(TOP_K):
        m = jnp.max(work, axis=0, keepdims=True)
        idx = jnp.min(jnp.where(work == m, eio, N_EXPERTS), axis=0, keepdims=True)
        hit = eio == idx
        sels.append(jnp.sum(jnp.where(hit, scores, 0.0), axis=0, keepdims=True))
        idxs.append(idx)
        hits.append(hit)
        work = jnp.where(hit, -jnp.inf, work)
    sel = jnp.concatenate(sels, axis=0)
    idx8 = jnp.concatenate(idxs, axis=0)
    gate8 = sel / jnp.sum(sel, axis=0, keepdims=True) * ROUTED_SCALE

    chosen = functools.reduce(jnp.logical_or, hits)
    before = (lax.broadcasted_iota(jnp.int32, (tm, tm), 0) < lax.broadcasted_iota(jnp.int32, (tm, tm), 1))
    prefix = jnp.dot(jnp.where(chosen, 1.0, 0.0).astype(BF16), jnp.where(before, 1.0, 0.0).astype(BF16),
                     preferred_element_type=F32)
    rank_dense = prefix + run_ref[:, 0:1]
    rank8 = jnp.concatenate([jnp.sum(jnp.where(h, rank_dense, 0.0), axis=0, keepdims=True) for h in hits],
                            axis=0).astype(jnp.int32)
    run = run_ref[...] + jnp.sum(jnp.where(chosen, 1.0, 0.0), axis=1, keepdims=True)
    run_ref[...] = run
    cnt_ref[...] = run
    for c in range(tm // 128):
        idx_ref[c] = idx8[:, c * 128:(c + 1) * 128]
        gate_ref[c] = gate8[:, c * 128:(c + 1) * 128]
        rank_ref[c] = rank8[:, c * 128:(c + 1) * 128]


def _out_router(xa, yab, yc, wo_bf, g2n, mod3, wr_hi, wr_lo, br, n_rows):
    tm = TM_OUT
    const2 = lambda i: (0, 0)
    row = lambda i: (i, 0)
    chunk3 = pl.BlockSpec((tm // 128, TOP_K, 128), lambda i: (i, 0, 0))
    nch = n_rows // 128
    return pl.pallas_call(
        _out_router_kernel,
        out_shape=[jax.ShapeDtypeStruct((n_rows, D), F32),
                   jax.ShapeDtypeStruct((n_rows, D), BF16),
                   jax.ShapeDtypeStruct((n_rows * 4, 128), jnp.uint32),
                   jax.ShapeDtypeStruct((nch, TOP_K, 128), jnp.int32),
                   jax.ShapeDtypeStruct((nch, TOP_K, 128), F32),
                   jax.ShapeDtypeStruct((nch, TOP_K, 128), jnp.int32),
                   jax.ShapeDtypeStruct((N_EXPERTS, 128), F32)],
        grid=(n_rows // tm,),
        in_specs=[pl.BlockSpec((tm, D), row), pl.BlockSpec((tm, 512), row), pl.BlockSpec((tm, 512), row),
                  pl.BlockSpec((D, D), const2),
                  _mod_spec(tm, 2),
                  pl.BlockSpec((1, D), const2), _mod_spec(tm, 3), _mod_spec(tm, 4),
                  pl.BlockSpec((N_EXPERTS, D), const2), pl.BlockSpec((N_EXPERTS, D), const2),
                  pl.BlockSpec((N_EXPERTS, 1), const2)],
        out_specs=[pl.BlockSpec((tm, D), row), pl.BlockSpec((tm, D), row), pl.BlockSpec((tm * 4, 128), row),
                   chunk3, chunk3, chunk3, pl.BlockSpec((N_EXPERTS, 128), const2)],
        scratch_shapes=[pltpu.VMEM((N_EXPERTS, 128), F32)],
        compiler_params=_cparams(("arbitrary",), 48),
        name="out_router",
    )(xa, yab, yc, wo_bf, mod3, g2n.reshape(1, D), mod3, mod3, wr_hi, wr_lo, br.reshape(N_EXPERTS, 1))


def _experts_kernel(te_ref, nu_ref, x_ref, wg_ref, wu_ref, wd_ref, y_ref):
    i = pl.program_id(0)

    @pl.when(i < nu_ref[0])
    def _():
        half = D // 2
        x_lo, x_hi = _unpack_rows(x_ref, TM_X)
        x_lo = x_lo.astype(BF16)
        x_hi = x_hi.astype(BF16)
        a = (jnp.dot(x_lo, wg_ref[0:half, :], preferred_element_type=F32)
             + jnp.dot(x_hi, wg_ref[half:, :], preferred_element_type=F32))
        b = (jnp.dot(x_lo, wu_ref[0:half, :], preferred_element_type=F32)
             + jnp.dot(x_hi, wu_ref[half:, :], preferred_element_type=F32))
        hid = (a * jax.nn.sigmoid(a) * b).astype(BF16)
        _pack_rows(jnp.dot(hid, wd_ref[...], preferred_element_type=F32), y_ref)

    @pl.when(i >= nu_ref[0])
    def _():
        y_ref[...] = jnp.zeros_like(y_ref)


def _experts(tile_expert, n_used, x_sorted, wg_bf, wu_bf, wd_bf):
    tm = TM_X
    r_pad = x_sorted.shape[0] // 4
    nt = r_pad // tm
    xrow = lambda i, te, nu: (jnp.minimum(i, nu[0] - 1), 0)
    grid_spec = pltpu.PrefetchScalarGridSpec(
        num_scalar_prefetch=2,
        grid=(nt,),
        in_specs=[pl.BlockSpec((tm * 4, 128), xrow),
                  pl.BlockSpec((None, D, D_EXPERT), lambda i, te, nu: (te[i], 0, 0)),
                  pl.BlockSpec((None, D, D_EXPERT), lambda i, te, nu: (te[i], 0, 0)),
                  pl.BlockSpec((None, D_EXPERT, D), lambda i, te, nu: (te[i], 0, 0))],
        out_specs=pl.BlockSpec((tm * 4, 128), lambda i, te, nu: (i, 0)),
    )
    return pl.pallas_call(
        _experts_kernel,
        out_shape=jax.ShapeDtypeStruct((r_pad * 4, 128), jnp.uint32),
        grid_spec=grid_spec,
        compiler_params=_cparams(("arbitrary",), 48),
        name="experts",
    )(tile_expert, n_used, x_sorted, wg_bf, wu_bf, wd_bf)


SC_CORES = 2
SC_SUBCORES = 16
SC_WORKERS = SC_CORES * SC_SUBCORES
SC_CHUNK = 128


def _sc_mesh():
    return plsc.VectorSubcoreMesh(core_axis_name="c", subcore_axis_name="s")


def _sc_params():
    return pltpu.CompilerParams(use_tc_tiling_on_sc=True)


def _sc_dispatch(h2p, pos3, r_pad):
    nch = pos3.shape[0]
    steps = -(-nch // SC_WORKERS)

    def body(h_hbm, pos_hbm, out_hbm, idx_v, rows_v, sem):
        wid = lax.axis_index("s") * SC_CORES + lax.axis_index("c")

        @pl.loop(0, steps)
        def _(s):
            ch = wid + s * SC_WORKERS

            @pl.when(ch < nch)
            def _():
                pltpu.sync_copy(pos_hbm.at[ch], idx_v)
                pltpu.sync_copy(h_hbm.at[pl.ds(ch * SC_CHUNK, SC_CHUNK)], rows_v)
                copies = [pltpu.async_copy(rows_v, out_hbm.at[idx_v.at[k]], sem) for k in range(TOP_K)]
                for cp in copies:
                    cp.wait()

    return pl.kernel(
        body,
        out_type=jax.ShapeDtypeStruct((r_pad, 4, 128), jnp.uint32),
        mesh=_sc_mesh(),
        scratch_types=[pltpu.VMEM((TOP_K, SC_CHUNK), jnp.int32),
                       pltpu.VMEM((SC_CHUNK, 4, 128), jnp.uint32),
                       pltpu.SemaphoreType.DMA],
        compiler_params=_sc_params(),
        name="sc_dispatch",
    )(h2p, pos3)


def _sc_collect(y_sorted, pos3):
    nch = pos3.shape[0]
    steps = -(-nch // SC_WORKERS)

    def body(y_hbm, pos_hbm, out_hbm, idx_v, rows_v, sem):
        wid = lax.axis_index("s") * SC_CORES + lax.axis_index("c")

        @pl.loop(0, steps)
        def _(s):
            ch = wid + s * SC_WORKERS

            @pl.when(ch < nch)
            def _():
                pltpu.sync_copy(pos_hbm.at[ch], idx_v)
                for k in range(TOP_K):
                    pltpu.async_copy(y_hbm.at[idx_v.at[k]], rows_v, sem).wait()
                    pltpu.sync_copy(rows_v, out_hbm.at[k, pl.ds(ch * SC_CHUNK, SC_CHUNK)])

    return pl.kernel(
        body,
        out_type=jax.ShapeDtypeStruct((TOP_K, nch * SC_CHUNK, 4, 128), jnp.uint32),
        mesh=_sc_mesh(),
        scratch_types=[pltpu.VMEM((TOP_K, SC_CHUNK), jnp.int32),
                       pltpu.VMEM((SC_CHUNK, 4, 128), jnp.uint32),
                       pltpu.SemaphoreType.DMA],
        compiler_params=_sc_params(),
        name="sc_collect",
    )(y_sorted, pos3)


def _shared_kernel(x_ref, h2_ref, yg_ref, gate_ref, wg_ref, wu_ref, wd_ref, g2_ref, o_ref):
    h = h2_ref[...]
    a = jnp.dot(h, wg_ref[...], preferred_element_type=F32)
    b = jnp.dot(h, wu_ref[...], preferred_element_type=F32)
    hid = (a * jax.nn.sigmoid(a) * b).astype(BF16)
    f = jnp.dot(hid, wd_ref[...], preferred_element_type=F32)
    gate = gate_ref[...]
    f_lo = f[:, :D // 2]
    f_hi = f[:, D // 2:]
    for k in range(TOP_K):
        y_lo, y_hi = _unpack_rows(yg_ref, TM_F, lead=k)
        f_lo = f_lo + gate[:, k:k + 1] * y_lo
        f_hi = f_hi + gate[:, k:k + 1] * y_hi
    o_ref[...] = x_ref[...] + g2_ref[...] * jnp.concatenate([f_lo, f_hi], axis=1)


def _shared_residual(xa, h2, yg, gates, wsg_bf, wsu_bf, wsd_bf, mod3, n_rows):
    tm = TM_F
    row = lambda i: (i, 0)
    const2 = lambda i: (0, 0)
    return pl.pallas_call(
        _shared_kernel,
        out_shape=jax.ShapeDtypeStruct((n_rows, D), F32),
        grid=(n_rows // tm,),
        in_specs=[pl.BlockSpec((tm, D), row), pl.BlockSpec((tm, D), row),
                  pl.BlockSpec((TOP_K, tm * 4, 128), lambda i: (0, i, 0)),
                  pl.BlockSpec((tm, TOP_K), row),
                  pl.BlockSpec((D, D_SHARED), const2), pl.BlockSpec((D, D_SHARED), const2),
                  pl.BlockSpec((D_SHARED, D), const2), _mod_spec(tm, 5)],
        out_specs=pl.BlockSpec((tm, D), row),
        compiler_params=_cparams(("parallel",), 48),
        name="shared_residual",
    )(xa, h2, yg, gates, wsg_bf, wsu_bf, wsd_bf, mod3)


def _route_positions(idx3, rank3, counts, n_rows):
    tm = TM_X
    counts = counts.astype(jnp.int32)
    padded = ((counts + tm - 1) // tm) * tm
    ends = jnp.cumsum(padded)
    offs = ends - padded
    pos3 = rank3 + jnp.take(offs, idx3)
    r_pad = n_rows * TOP_K + N_EXPERTS * tm
    nt = r_pad // tm
    tile_ids = jnp.arange(nt, dtype=jnp.int32)
    tile_expert = jnp.sum((ends // tm)[None, :] <= tile_ids[:, None], axis=1)
    tile_expert = jnp.minimum(tile_expert, N_EXPERTS - 1).astype(jnp.int32)
    n_used = (ends[-1] // tm).astype(jnp.int32).reshape(1)
    return pos3, tile_expert, n_used, r_pad


def _q_perm():
    perm = np.empty(512, np.int32)
    for h in range(HEADS):
        for half in range(2):
            for d in range(QK_DIM):
                perm[h * 128 + half * QK_DIM + d] = half * HEADS * QK_DIM + h * QK_DIM + d
    return perm


def _rope_tables():
    t = jnp.arange(L)
    row = (t // GRID_W).astype(F32)
    col = (t % GRID_W).astype(F32)
    n_freq = QK_DIM // 4
    inv = ROPE_BASE ** (-jnp.arange(n_freq, dtype=F32) / n_freq)
    ar = row[:, None] * inv
    ac = col[:, None] * inv
    cos64 = jnp.concatenate([jnp.cos(ar), jnp.cos(ar), jnp.cos(ac), jnp.cos(ac)], axis=1)
    sin64 = jnp.concatenate([-jnp.sin(ar), jnp.sin(ar), -jnp.sin(ac), jnp.sin(ac)], axis=1)
    cos_t = jnp.concatenate([jnp.tile(cos64, (1, 2)), jnp.ones((TM_MIX, 128), F32)], axis=0)
    sin_t = jnp.concatenate([jnp.tile(sin64, (1, 2)), jnp.zeros((TM_MIX, 128), F32)], axis=0)
    return cos_t, sin_t


def _split_bf16(w):
    hi = w.astype(BF16)
    return hi, (w - hi.astype(F32)).astype(BF16)


def kernel(x, c, ctx, c_ctx, w_ada, b_ada, g_norm1, g_norm2, w_in, w_out, g_v, w_s, b_s, w_conv, g_q, g_k,
           lam_q1, lam_k1, lam_q2, lam_k2, g_sub, w_router, b_router, w_gate, w_up, w_down,
           ws_gate, ws_up, ws_down):
    xa = jnp.concatenate([x.reshape(NL, D), ctx.reshape(NC, D)], axis=0)
    cc = jnp.concatenate([c, c_ctx[None, :], jnp.zeros((MOD_ROWS - B - 1, D), F32)], axis=0)
    mod = _ada(cc, w_ada, b_ada)
    cos_t, sin_t = _rope_tables()
    perm = _q_perm()
    col_perm = np.concatenate([np.arange(OFF_Q), OFF_Q + perm, OFF_K + perm, np.arange(OFF_V, D_IN)])
    bd = jnp.asarray(np.kron(np.eye(8, dtype=np.float32), np.full((64, 64), 1.0 / 64, np.float32)), BF16)

    for l in range(DEPTH):
        last = l == DEPTH - 1
        lam_init = 0.8 - 0.6 * math.exp(-0.3 * l)
        lam = (jnp.exp(jnp.sum(lam_q1[l] * lam_k1[l])) - jnp.exp(jnp.sum(lam_q2[l] * lam_k2[l]))
               + lam_init).reshape(1).astype(F32)
        mod3 = mod[l].reshape(MOD_ROWS, 1, 6 * D)
        w_in_bf = w_in[l][:, col_perm].astype(BF16)
        p = _in_proj(xa, g_norm1[l], mod3, w_in_bf)
        bias_t = jnp.repeat(b_s[l].T, A_GD, axis=1)
        wconv = jnp.concatenate([w_conv[l], jnp.zeros((5, B_WIDTH), F32)], axis=0)
        yab, qa, qb, kk = _mixers(p, cos_t, sin_t, g_v[l].reshape(1, A_WIDTH), w_s[l].astype(BF16), bias_t, wconv,
                                  jnp.tile(g_q[l], 8).reshape(1, 512), jnp.tile(g_k[l], 8).reshape(1, 512), bd)
        gsub = g_sub[l].reshape(1, V_DIM)
        coef = 1.0 - lam_init
        yc = _attention(lam, qa, qb, kk, p, gsub, coef, ctx_queries=False)
        if last:
            n_rows = NL
        else:
            n_rows = NR
            yc = jnp.concatenate([yc, _attention(lam, qa, qb, kk, p, gsub, coef, ctx_queries=True)], axis=0)
        wr_hi, wr_lo = _split_bf16(w_router[l].T)
        xa, h2, h2p, idx3, gate3, rank3, counts = _out_router(
            xa, yab, yc, w_out[l].astype(BF16), g_norm2[l], mod3, wr_hi, wr_lo, b_router[l], n_rows)
        pos3, tile_expert, n_used, r_pad = _route_positions(idx3, rank3, counts[:, 0], n_rows)
        x_sorted = _sc_dispatch(h2p.reshape(n_rows, 4, 128), pos3, r_pad)
        y_sorted = _experts(tile_expert, n_used, x_sorted.reshape(r_pad * 4, 128), w_gate[l].astype(BF16),
                            w_up[l].astype(BF16), w_down[l].astype(BF16))
        yg = _sc_collect(y_sorted.reshape(r_pad, 4, 128), pos3)
        gates = gate3.transpose(0, 2, 1).reshape(n_rows, TOP_K)
        xa = _shared_residual(xa, h2, yg.reshape(TOP_K, n_rows * 4, 128), gates, ws_gate[l].astype(BF16),
                              ws_up[l].astype(BF16), ws_down[l].astype(BF16), mod3, n_rows)
    return xa.reshape(B, L, D)
```

```python
import functools
import math

import numpy as np
import jax
import jax.numpy as jnp
from jax import lax
from jax.experimental import pallas as pl
from jax.experimental.pallas import tpu as pltpu
from jax.experimental.pallas import tpu_sc as plsc

F32 = jnp.float32
BF16 = jnp.bfloat16

D = 1024
B = 8
L = 2048
DEPTH = 2
GRID_W = 64
CTX = 256
A_WIDTH = 256
A_GROUPS = 4
A_GD = 64
CHUNK = 128
B_WIDTH = 256
C_WIDTH = 512
HEADS = 4
V_DIM = 128
QK_DIM = 64
ROPE_BASE = 10000.0
OFF_BB = 512
OFF_BC = 768
OFF_BX = 1024
OFF_Q = 1280
OFF_K = 1792
OFF_V = 2304
D_IN = 2816
N_EXPERTS = 64
TOP_K = 8
D_EXPERT = 256
D_SHARED = 256
ROUTED_SCALE = 2.5
EPS = 1e-6

NL = B * L
NC = B * CTX
NR = NL + NC
MOD_ROWS = 16
LOG2E = 1.4426950408889634

TM_IN = 512
TM_MIX = 256
TQ = 256
TM_OUT = 256
TM_X = 512
TM_F = 256

_DN_T = (((1,), (1,)), ((), ()))


def _cparams(sem, vmem_mb=None):
    kw = dict(dimension_semantics=sem)
    if vmem_mb is not None:
        kw["vmem_limit_bytes"] = vmem_mb * 1024 * 1024
    return pltpu.CompilerParams(**kw)


def _mod_row(i, tm):
    return jnp.where(i < NL // tm, i // (L // tm), B)


def _mod_spec(tm, chunk):
    return pl.BlockSpec((None, 1, D), lambda i: (_mod_row(i, tm), 0, chunk))


def _ada_kernel(c_ref, w_ref, b_ref, o_ref):
    c = c_ref[...]
    cs = c * jax.nn.sigmoid(c)
    o_ref[...] = jnp.dot(cs, w_ref[...], preferred_element_type=F32,
                         precision=lax.Precision.HIGHEST) + b_ref[...]


def _ada(cc, w_ada, b_ada):
    nb = 6
    return pl.pallas_call(
        _ada_kernel,
        out_shape=jax.ShapeDtypeStruct((DEPTH, MOD_ROWS, 6 * D), F32),
        grid=(DEPTH, nb),
        in_specs=[pl.BlockSpec((MOD_ROWS, D), lambda l, j: (0, 0)),
                  pl.BlockSpec((None, D, D), lambda l, j: (l, 0, j)),
                  pl.BlockSpec((None, 1, D), lambda l, j: (l, 0, j))],
        out_specs=pl.BlockSpec((None, MOD_ROWS, D), lambda l, j: (l, 0, j)),
        compiler_params=_cparams(("arbitrary", "arbitrary"), 40),
        name="ada_mod",
    )(cc, w_ada, b_ada.reshape(DEPTH, 1, 6 * D))


def _rms_mod(x, g, sc, sh):
    ms = jnp.mean(x * x, axis=-1, keepdims=True)
    return x * lax.rsqrt(ms + EPS) * g * (1.0 + sc) + sh


def _in_proj_kernel(x_ref, g_ref, sh_ref, sc_ref, w_ref, o_ref):
    h = _rms_mod(x_ref[...], g_ref[...], sc_ref[...], sh_ref[...])
    o_ref[...] = jnp.dot(h.astype(BF16), w_ref[...], preferred_element_type=F32).astype(o_ref.dtype)


def _in_proj(xa, g, mod3, w_bf):
    tm = TM_IN
    return pl.pallas_call(
        _in_proj_kernel,
        out_shape=jax.ShapeDtypeStruct((NR, D_IN), BF16),
        grid=(NR // tm,),
        in_specs=[pl.BlockSpec((tm, D), lambda i: (i, 0)),
                  pl.BlockSpec((1, D), lambda i: (0, 0)),
                  _mod_spec(tm, 0), _mod_spec(tm, 1),
                  pl.BlockSpec((D, D_IN), lambda i: (0, 0))],
        out_specs=pl.BlockSpec((tm, D_IN), lambda i: (i, 0)),
        compiler_params=_cparams(("parallel",), 48),
        name="in_proj",
    )(xa, g.reshape(1, D), mod3, mod3, w_bf)


def _group_rms(t, g, bd):
    sq = t * t
    hi = sq.astype(BF16)
    lo = (sq - hi.astype(F32)).astype(BF16)
    ms = (jnp.dot(hi, bd, preferred_element_type=F32) + jnp.dot(lo, bd, preferred_element_type=F32))
    return t * lax.rsqrt(ms + EPS) * g


def _rope(t, cos, sin):
    w = t.shape[1]
    lane = lax.broadcasted_iota(jnp.int32, t.shape, 1)
    first = (lane % 32) < 16
    partner = jnp.where(first, pltpu.roll(t, w - 16, 1), pltpu.roll(t, 16, 1))
    cos4 = jnp.concatenate([cos] * (w // 128), axis=1)
    sin4 = jnp.concatenate([sin] * (w // 128), axis=1)
    return t * cos4 + partner * sin4


def _mixers_kernel(p_ref, pc_ref, px_ref, nc_ref, nx_ref, cos_ref, sin_ref, gv_ref, ws_ref, bias_ref,
                   wconv_ref, gq_ref, gk_ref, bd_ref, yab_ref, qa_ref, qb_ref, kk_ref):
    tm = TM_MIX
    i = pl.program_id(0)
    tiles_per_seq = L // tm
    is_lat = i < NL // tm
    is_start = jnp.logical_or(jnp.logical_not(is_lat), i % tiles_per_seq == 0)
    is_end = jnp.logical_or(jnp.logical_not(is_lat), i % tiles_per_seq == tiles_per_seq - 1)

    uv = p_ref[:, 0:2 * A_WIDTH].astype(F32)
    uv = 0.5 * uv * (1.0 + lax.erf(uv * (2.0 ** -0.5)))
    u = uv[:, :A_WIDTH]
    v = uv[:, A_WIDTH:]
    ms = jnp.mean(v * v, axis=-1, keepdims=True)
    vb = (v * lax.rsqrt(ms + EPS) * gv_ref[...]).astype(BF16)
    lane = lax.broadcasted_iota(jnp.int32, (CHUNK, 128), 1)
    mixes = []
    for c in range(tm // CHUNK):
        vc = vb[c * CHUNK:(c + 1) * CHUNK]
        halves = []
        for j in range(2):
            vj = vc[:, j * 128:(j + 1) * 128]
            m0 = jnp.dot(ws_ref[2 * j], vj, preferred_element_type=F32)
            m1 = jnp.dot(ws_ref[2 * j + 1], vj, preferred_element_type=F32)
            halves.append(jnp.where(lane < A_GD, m0, m1))
        mixes.append(jnp.concatenate(halves, axis=1) + bias_ref[...])
    ya = u * jnp.concatenate(mixes, axis=0)

    bg = p_ref[:, OFF_BB:OFF_BC].astype(F32)
    z = p_ref[:, OFF_BC:OFF_BX].astype(F32) * p_ref[:, OFF_BX:OFF_Q].astype(F32)
    zp = (pc_ref[...].astype(F32) * px_ref[...].astype(F32))[15:16]
    zn = (nc_ref[...].astype(F32) * nx_ref[...].astype(F32))[0:1]
    zp = jnp.where(is_start, 0.0, zp)
    zn = jnp.where(is_end, 0.0, zn)
    row = lax.broadcasted_iota(jnp.int32, z.shape, 0)
    z_prev = jnp.where(row == 0, zp, pltpu.roll(z, 1, 0))
    z_next = jnp.where(row == tm - 1, zn, pltpu.roll(z, tm - 1, 0))
    yb = bg * (z_prev * wconv_ref[0:1] + z * wconv_ref[1:2] + z_next * wconv_ref[2:3])
    yab_ref[...] = jnp.concatenate([ya, yb], axis=1).astype(BF16)

    cos = cos_ref[...]
    sin = sin_ref[...]
    bd = bd_ref[...]
    q = _rope(_group_rms(p_ref[:, OFF_Q:OFF_K].astype(F32), gq_ref[...], bd), cos, sin)
    q = q * (QK_DIM ** -0.5 * LOG2E)
    lane5 = lax.broadcasted_iota(jnp.int32, q.shape, 1) % 128
    qa_ref[...] = jnp.where(lane5 < QK_DIM, q, 0.0).astype(BF16)
    qb_ref[...] = jnp.where(lane5 >= QK_DIM, q, 0.0).astype(BF16)
    k = _rope(_group_rms(p_ref[:, OFF_K:OFF_V].astype(F32), gk_ref[...], bd), cos, sin)
    kk_ref[...] = k.astype(BF16)


def _mixers(p, cos_t, sin_t, gv, ws_bf, bias_t, wconv, gq, gk, bd):
    tm = TM_MIX
    nt = NR // tm
    hb = tm // 16
    last_hb = NR // 16 - 1
    pos_blocks = L // tm

    def tab_map(i):
        return (jnp.where(i < NL // tm, i % pos_blocks, pos_blocks), 0)

    prev_map_c = lambda i: (jnp.maximum(i * hb - 1, 0), OFF_BC // 256)
    prev_map_x = lambda i: (jnp.maximum(i * hb - 1, 0), OFF_BX // 256)
    next_map_c = lambda i: (jnp.minimum((i + 1) * hb, last_hb), OFF_BC // 256)
    next_map_x = lambda i: (jnp.minimum((i + 1) * hb, last_hb), OFF_BX // 256)
    const2 = lambda i: (0, 0)
    outs = pl.pallas_call(
        _mixers_kernel,
        out_shape=[jax.ShapeDtypeStruct((NR, 2 * A_WIDTH), BF16),
                   jax.ShapeDtypeStruct((NR, 512), BF16),
                   jax.ShapeDtypeStruct((NR, 512), BF16),
                   jax.ShapeDtypeStruct((NR, 512), BF16)],
        grid=(nt,),
        in_specs=[pl.BlockSpec((tm, D_IN), lambda i: (i, 0)),
                  pl.BlockSpec((16, 256), prev_map_c), pl.BlockSpec((16, 256), prev_map_x),
                  pl.BlockSpec((16, 256), next_map_c), pl.BlockSpec((16, 256), next_map_x),
                  pl.BlockSpec((tm, 128), tab_map), pl.BlockSpec((tm, 128), tab_map),
                  pl.BlockSpec((1, A_WIDTH), const2),
                  pl.BlockSpec((A_GROUPS, CHUNK, CHUNK), lambda i: (0, 0, 0)),
                  pl.BlockSpec((CHUNK, A_WIDTH), const2),
                  pl.BlockSpec((8, B_WIDTH), const2),
                  pl.BlockSpec((1, 512), const2), pl.BlockSpec((1, 512), const2),
                  pl.BlockSpec((512, 512), const2)],
        out_specs=[pl.BlockSpec((tm, 512), lambda i: (i, 0))] * 4,
        compiler_params=_cparams(("parallel",), 48),
        name="mixers",
    )(p, p, p, p, p, cos_t, sin_t, gv, ws_bf, bias_t, wconv, gq, gk, bd)
    return outs


def _attn_kernel(lam_ref, qa_ref, qb_ref, *rest, n_seg, coef):
    kv_refs = rest[:2 * n_seg]
    gsub_ref, o_ref, k_scr, v_scr = rest[2 * n_seg:]

    @pl.when(pl.program_id(2) == 0)
    def _():
        off = 0
        for s in range(n_seg):
            n = kv_refs[s].shape[0]
            k_scr[off:off + n, :] = kv_refs[s][...]
            v_scr[off:off + n, :] = kv_refs[n_seg + s][...]
            off += n

    k = k_scr[...]
    s1 = lax.dot_general(qa_ref[...], k, _DN_T, preferred_element_type=F32)
    s2 = lax.dot_general(qb_ref[...], k, _DN_T, preferred_element_type=F32)
    p1 = jnp.exp2(s1 - jnp.max(s1, axis=-1, keepdims=True))
    p2 = jnp.exp2(s2 - jnp.max(s2, axis=-1, keepdims=True))
    a1 = 1.0 / jnp.sum(p1, axis=-1, keepdims=True)
    a2 = lam_ref[0] / jnp.sum(p2, axis=-1, keepdims=True)
    w = (p1 * a1 - p2 * a2).astype(BF16)
    o = jnp.dot(w, v_scr[...], preferred_element_type=F32)
    ms = jnp.mean(o * o, axis=-1, keepdims=True)
    o_ref[...] = (o * lax.rsqrt(ms + EPS) * gsub_ref[...] * coef).astype(o_ref.dtype)


def _attention(lam, qa, qb, kk, p, gsub, coef, *, ctx_queries):
    vcol = OFF_V // 128
    if ctx_queries:
        nq, lk, n_seg = 1, CTX, 1
        q_map = lambda b, h, qi: (NL // TQ + b, h)
        kv_specs = [pl.BlockSpec((CTX, 128), lambda b, h, qi: (NL // CTX + b, h)),
                    pl.BlockSpec((CTX, 128), lambda b, h, qi: (NL // CTX + b, vcol + h))]
        kv_args = [kk, p]
        rows = NC
        o_map = lambda b, h, qi: (b, h)
    else:
        nq, lk, n_seg = L // TQ, CTX + L, 2
        q_map = lambda b, h, qi: (b * (L // TQ) + qi, h)
        kv_specs = [pl.BlockSpec((CTX, 128), lambda b, h, qi: (NL // CTX + b, h)),
                    pl.BlockSpec((L, 128), lambda b, h, qi: (b, h)),
                    pl.BlockSpec((CTX, 128), lambda b, h, qi: (NL // CTX + b, vcol + h)),
                    pl.BlockSpec((L, 128), lambda b, h, qi: (b, vcol + h))]
        kv_args = [kk, kk, p, p]
        rows = NL
        o_map = lambda b, h, qi: (b * (L // TQ) + qi, h)
    return pl.pallas_call(
        functools.partial(_attn_kernel, n_seg=n_seg, coef=coef),
        out_shape=jax.ShapeDtypeStruct((rows, C_WIDTH), BF16),
        grid=(B, HEADS, nq),
        in_specs=[pl.BlockSpec(memory_space=pltpu.SMEM),
                  pl.BlockSpec((TQ, 128), q_map), pl.BlockSpec((TQ, 128), q_map)]
                 + kv_specs + [pl.BlockSpec((1, V_DIM), lambda b, h, qi: (0, 0))],
        out_specs=pl.BlockSpec((TQ, 128), o_map),
        scratch_shapes=[pltpu.VMEM((lk, 128), BF16), pltpu.VMEM((lk, 128), BF16)],
        compiler_params=_cparams(("parallel", "parallel", "arbitrary"), 48),
        name="attn_ctx" if ctx_queries else "attn_lat",
    )(lam, qa, qb, *kv_args, gsub)


def _pack_rows(t, out_ref):
    half = D // 2
    lo = lax.bitcast_convert_type(t[:, :half].astype(BF16).astype(F32), jnp.uint32) >> 16
    hi = lax.bitcast_convert_type(t[:, half:].astype(BF16).astype(F32), jnp.uint32) & jnp.uint32(0xFFFF0000)
    w = lo | hi
    rows = t.shape[0]
    for j in range(4):
        out_ref[pl.ds(j, rows, stride=4), :] = w[:, j * 128:(j + 1) * 128]


def _unpack_piece(w):
    lo = lax.bitcast_convert_type(w << 16, F32)
    hi = lax.bitcast_convert_type(w & jnp.uint32(0xFFFF0000), F32)
    return lo, hi


def _unpack_rows(ref, rows, lead=None):
    los, his = [], []
    for j in range(4):
        w = ref[pl.ds(j, rows, stride=4), :] if lead is None else ref[lead, pl.ds(j, rows, stride=4), :]
        lo, hi = _unpack_piece(w)
        los.append(lo)
        his.append(hi)
    return jnp.concatenate(los, axis=1), jnp.concatenate(his, axis=1)


def _out_router_kernel(x_ref, yab_ref, yc_ref, wo_ref, g1_ref, g2n_ref, sh2_ref, sc2_ref, wrh_ref, wrl_ref, br_ref,
                       xo_ref, h2_ref, h2p_ref, idx_ref, gate_ref, rank_ref, cnt_ref, run_ref):
    tm = TM_OUT

    @pl.when(pl.program_id(0) == 0)
    def _():
        run_ref[...] = jnp.zeros_like(run_ref)

    y = (jnp.dot(yab_ref[...], wo_ref[0:2 * A_WIDTH, :], preferred_element_type=F32)
         + jnp.dot(yc_ref[...], wo_ref[2 * A_WIDTH:, :], preferred_element_type=F32))
    x = x_ref[...] + g1_ref[...] * y
    xo_ref[...] = x
    h2 = _rms_mod(x, g2n_ref[...], sc2_ref[...], sh2_ref[...])
    hi = h2.astype(BF16)
    h2_ref[...] = hi
    _pack_rows(h2, h2p_ref)
    lo = (h2 - hi.astype(F32)).astype(BF16)
    wh = wrh_ref[...]
    z = (lax.dot_general(wh, hi, _DN_T, preferred_element_type=F32)
         + lax.dot_general(wh, lo, _DN_T, preferred_element_type=F32)
         + lax.dot_general(wrl_ref[...], hi, _DN_T, preferred_element_type=F32))
    scores = jax.nn.sigmoid(z)
    work = scores + br_ref[...]
    eio = lax.broadcasted_iota(jnp.int32, work.shape, 0)
    idxs, sels, hits = [], [], []
    for _ in range(TOP_K):
        m = jnp.max(work, axis=0, keepdims=True)
        idx = jnp.min(jnp.where(work == m, eio, N_EXPERTS), axis=0, keepdims=True)
        hit = eio == idx
        sels.append(jnp.sum(jnp.where(hit, scores, 0.0), axis=0, keepdims=True))
        idxs.append(idx)
        hits.append(hit)
        work = jnp.where(hit, -jnp.inf, work)
    sel = jnp.concatenate(sels, axis=0)
    idx8 = jnp.concatenate(idxs, axis=0)
    gate8 = sel / jnp.sum(sel, axis=0, keepdims=True) * ROUTED_SCALE

    chosen = functools.reduce(jnp.logical_or, hits)
    before = (lax.broadcasted_iota(jnp.int32, (tm, tm), 0) < lax.broadcasted_iota(jnp.int32, (tm, tm), 1))
    prefix = jnp.dot(jnp.where(chosen, 1.0, 0.0).astype(BF16), jnp.where(before, 1.0, 0.0).astype(BF16),
                     preferred_element_type=F32)
    rank_dense = prefix + run_ref[:, 0:1]
    rank8 = jnp.concatenate([jnp.sum(jnp.where(h, rank_dense, 0.0), axis=0, keepdims=True) for h in hits],
                            axis=0).astype(jnp.int32)
    run = run_ref[...] + jnp.sum(jnp.where(chosen, 1.0, 0.0), axis=1, keepdims=True)
    run_ref[...] = run
    cnt_ref[...] = run
    for c in range(tm // 128):
        idx_ref[c] = idx8[:, c * 128:(c + 1) * 128]
        gate_ref[c] = gate8[:, c * 128:(c + 1) * 128]
        rank_ref[c] = rank8[:, c * 128:(c + 1) * 128]


def _out_router(xa, yab, yc, wo_bf, g2n, mod3, wr_hi, wr_lo, br, n_rows):
    tm = TM_OUT
    const2 = lambda i: (0, 0)
    row = lambda i: (i, 0)
    chunk3 = pl.BlockSpec((tm // 128, TOP_K, 128), lambda i: (i, 0, 0))
    nch = n_rows // 128
    return pl.pallas_call(
        _out_router_kernel,
        out_shape=[jax.ShapeDtypeStruct((n_rows, D), F32),
                   jax.ShapeDtypeStruct((n_rows, D), BF16),
                   jax.ShapeDtypeStruct((n_rows * 4, 128), jnp.uint32),
                   jax.ShapeDtypeStruct((nch, TOP_K, 128), jnp.int32),
                   jax.ShapeDtypeStruct((nch, TOP_K, 128), F32),
                   jax.ShapeDtypeStruct((nch, TOP_K, 128), jnp.int32),
                   jax.ShapeDtypeStruct((N_EXPERTS, 128), F32)],
        grid=(n_rows // tm,),
        in_specs=[pl.BlockSpec((tm, D), row), pl.BlockSpec((tm, 512), row), pl.BlockSpec((tm, 512), row),
                  pl.BlockSpec((D, D), const2),
                  _mod_spec(tm, 2),
                  pl.BlockSpec((1, D), const2), _mod_spec(tm, 3), _mod_spec(tm, 4),
                  pl.BlockSpec((N_EXPERTS, D), const2), pl.BlockSpec((N_EXPERTS, D), const2),
                  pl.BlockSpec((N_EXPERTS, 1), const2)],
        out_specs=[pl.BlockSpec((tm, D), row), pl.BlockSpec((tm, D), row), pl.BlockSpec((tm * 4, 128), row),
                   chunk3, chunk3, chunk3, pl.BlockSpec((N_EXPERTS, 128), const2)],
        scratch_shapes=[pltpu.VMEM((N_EXPERTS, 128), F32)],
        compiler_params=_cparams(("arbitrary",), 48),
        name="out_router",
    )(xa, yab, yc, wo_bf, mod3, g2n.reshape(1, D), mod3, mod3, wr_hi, wr_lo, br.reshape(N_EXPERTS, 1))


def _experts_kernel(te_ref, nu_ref, x_ref, wg_ref, wu_ref, wd_ref, y_ref, wgu_s, wd_s):
    i = pl.program_id(0)
    used = i < nu_ref[0]
    new_expert = jnp.logical_or(i == 0, te_ref[i] != te_ref[jnp.maximum(i - 1, 0)])

    @pl.when(jnp.logical_and(used, new_expert))
    def _():
        wgu_s[:, 0:D_EXPERT] = wg_ref[...].astype(BF16)
        wgu_s[:, D_EXPERT:] = wu_ref[...].astype(BF16)
        wd_s[...] = wd_ref[...].astype(BF16)

    @pl.when(used)
    def _():
        half = D // 2
        x_lo, x_hi = _unpack_rows(x_ref, TM_X)
        ab = (jnp.dot(x_lo.astype(BF16), wgu_s[0:half, :], preferred_element_type=F32)
              + jnp.dot(x_hi.astype(BF16), wgu_s[half:, :], preferred_element_type=F32))
        a = ab[:, :D_EXPERT]
        hid = (a * jax.nn.sigmoid(a) * ab[:, D_EXPERT:]).astype(BF16)
        _pack_rows(jnp.dot(hid, wd_s[...], preferred_element_type=F32), y_ref)


def _experts(layer, tile_expert, n_used, x_sorted, w_gate, w_up, w_down):
    tm = TM_X
    r_pad = x_sorted.shape[0] // 4
    nt = r_pad // tm
    row = lambda i, te, nu: (jnp.minimum(i, nu[0] - 1), 0)
    wmap = lambda i, te, nu: (layer, te[i], 0, 0)
    grid_spec = pltpu.PrefetchScalarGridSpec(
        num_scalar_prefetch=2,
        grid=(nt,),
        in_specs=[pl.BlockSpec((tm * 4, 128), row),
                  pl.BlockSpec((None, None, D, D_EXPERT), wmap),
                  pl.BlockSpec((None, None, D, D_EXPERT), wmap),
                  pl.BlockSpec((None, None, D_EXPERT, D), wmap)],
        out_specs=pl.BlockSpec((tm * 4, 128), row),
        scratch_shapes=[pltpu.VMEM((D, 2 * D_EXPERT), BF16), pltpu.VMEM((D_EXPERT, D), BF16)],
    )
    return pl.pallas_call(
        _experts_kernel,
        out_shape=jax.ShapeDtypeStruct((r_pad * 4, 128), jnp.uint32),
        grid_spec=grid_spec,
        compiler_params=_cparams(("arbitrary",), 48),
        name="experts",
    )(tile_expert, n_used, x_sorted, w_gate, w_up, w_down)


SC_CORES = 2
SC_SUBCORES = 16
SC_WORKERS = SC_CORES * SC_SUBCORES
SC_CHUNK = 128


def _sc_mesh():
    return plsc.VectorSubcoreMesh(core_axis_name="c", subcore_axis_name="s")


def _sc_params():
    return pltpu.CompilerParams(use_tc_tiling_on_sc=True)


def _sc_dispatch(h2p, pos3, r_pad):
    nch = pos3.shape[0]
    steps = -(-nch // SC_WORKERS)

    def body(h_hbm, pos_hbm, out_hbm, idx_v, rows_v, sem):
        wid = lax.axis_index("s") * SC_CORES + lax.axis_index("c")

        @pl.loop(0, steps)
        def _(s):
            ch = wid + s * SC_WORKERS

            @pl.when(ch < nch)
            def _():
                pltpu.sync_copy(pos_hbm.at[ch], idx_v)
                pltpu.sync_copy(h_hbm.at[pl.ds(ch * SC_CHUNK, SC_CHUNK)], rows_v)
                copies = [pltpu.async_copy(rows_v, out_hbm.at[idx_v.at[k]], sem) for k in range(TOP_K)]
                for cp in copies:
                    cp.wait()

    return pl.kernel(
        body,
        out_type=jax.ShapeDtypeStruct((r_pad, 4, 128), jnp.uint32),
        mesh=_sc_mesh(),
        scratch_types=[pltpu.VMEM((TOP_K, SC_CHUNK), jnp.int32),
                       pltpu.VMEM((SC_CHUNK, 4, 128), jnp.uint32),
                       pltpu.SemaphoreType.DMA],
        compiler_params=_sc_params(),
        name="sc_dispatch",
    )(h2p, pos3)


def _sc_collect(y_sorted, pos3):
    nch = pos3.shape[0]
    steps = -(-nch // SC_WORKERS)

    def body(y_hbm, pos_hbm, out_hbm, idx_v, rows_v, sem):
        wid = lax.axis_index("s") * SC_CORES + lax.axis_index("c")

        @pl.loop(0, steps)
        def _(s):
            ch = wid + s * SC_WORKERS

            @pl.when(ch < nch)
            def _():
                pltpu.sync_copy(pos_hbm.at[ch], idx_v)
                for k in range(TOP_K):
                    pltpu.async_copy(y_hbm.at[idx_v.at[k]], rows_v, sem).wait()
                    pltpu.sync_copy(rows_v, out_hbm.at[k, pl.ds(ch * SC_CHUNK, SC_CHUNK)])

    return pl.kernel(
        body,
        out_type=jax.ShapeDtypeStruct((TOP_K, nch * SC_CHUNK, 4, 128), jnp.uint32),
        mesh=_sc_mesh(),
        scratch_types=[pltpu.VMEM((TOP_K, SC_CHUNK), jnp.int32),
                       pltpu.VMEM((SC_CHUNK, 4, 128), jnp.uint32),
                       pltpu.SemaphoreType.DMA],
        compiler_params=_sc_params(),
        name="sc_collect",
    )(y_sorted, pos3)


def _shared_kernel(x_ref, h2_ref, yg_ref, gate_ref, wg_ref, wu_ref, wd_ref, g2_ref, o_ref):
    h = h2_ref[...]
    a = jnp.dot(h, wg_ref[...], preferred_element_type=F32)
    b = jnp.dot(h, wu_ref[...], preferred_element_type=F32)
    hid = (a * jax.nn.sigmoid(a) * b).astype(BF16)
    f = jnp.dot(hid, wd_ref[...], preferred_element_type=F32)
    gate = gate_ref[...]
    f_lo = f[:, :D // 2]
    f_hi = f[:, D // 2:]
    for k in range(TOP_K):
        y_lo, y_hi = _unpack_rows(yg_ref, TM_F, lead=k)
        f_lo = f_lo + gate[:, k:k + 1] * y_lo
        f_hi = f_hi + gate[:, k:k + 1] * y_hi
    o_ref[...] = x_ref[...] + g2_ref[...] * jnp.concatenate([f_lo, f_hi], axis=1)


def _shared_residual(xa, h2, yg, gates, wsg_bf, wsu_bf, wsd_bf, mod3, n_rows):
    tm = TM_F
    row = lambda i: (i, 0)
    const2 = lambda i: (0, 0)
    return pl.pallas_call(
        _shared_kernel,
        out_shape=jax.ShapeDtypeStruct((n_rows, D), F32),
        grid=(n_rows // tm,),
        in_specs=[pl.BlockSpec((tm, D), row), pl.BlockSpec((tm, D), row),
                  pl.BlockSpec((TOP_K, tm * 4, 128), lambda i: (0, i, 0)),
                  pl.BlockSpec((tm, TOP_K), row),
                  pl.BlockSpec((D, D_SHARED), const2), pl.BlockSpec((D, D_SHARED), const2),
                  pl.BlockSpec((D_SHARED, D), const2), _mod_spec(tm, 5)],
        out_specs=pl.BlockSpec((tm, D), row),
        compiler_params=_cparams(("parallel",), 48),
        name="shared_residual",
    )(xa, h2, yg, gates, wsg_bf, wsu_bf, wsd_bf, mod3)


def _positions_kernel(offs_ref, idx_ref, rank_ref, pos_ref):
    idx = idx_ref[...]
    base = jnp.zeros_like(idx)
    for e in range(N_EXPERTS):
        base = jnp.where(idx == e, offs_ref[e], base)
    pos_ref[...] = rank_ref[...] + base


def _positions(offs, idx3, rank3):
    nch = idx3.shape[0]
    cb = 16
    spec = pl.BlockSpec((cb, TOP_K, 128), lambda i, offs: (i, 0, 0))
    return pl.pallas_call(
        _positions_kernel,
        out_shape=jax.ShapeDtypeStruct((nch, TOP_K, 128), jnp.int32),
        grid_spec=pltpu.PrefetchScalarGridSpec(num_scalar_prefetch=1, grid=(nch // cb,),
                                               in_specs=[spec, spec], out_specs=spec),
        compiler_params=_cparams(("parallel",)),
        name="positions",
    )(offs, idx3, rank3)


def _route_positions(idx3, rank3, counts, n_rows):
    tm = TM_X
    counts = counts.astype(jnp.int32)
    padded = ((counts + tm - 1) // tm) * tm
    ends = jnp.cumsum(padded)
    offs = ends - padded
    pos3 = _positions(offs.astype(jnp.int32), idx3, rank3)
    r_pad = n_rows * TOP_K + N_EXPERTS * tm
    nt = r_pad // tm
    tile_ids = jnp.arange(nt, dtype=jnp.int32)
    tile_expert = jnp.sum((ends // tm)[None, :] <= tile_ids[:, None], axis=1)
    tile_expert = jnp.minimum(tile_expert, N_EXPERTS - 1).astype(jnp.int32)
    n_used = (ends[-1] // tm).astype(jnp.int32).reshape(1)
    return pos3, tile_expert, n_used, r_pad


def _q_perm():
    perm = np.empty(512, np.int32)
    for h in range(HEADS):
        for half in range(2):
            for d in range(QK_DIM):
                perm[h * 128 + half * QK_DIM + d] = half * HEADS * QK_DIM + h * QK_DIM + d
    return perm


def _rope_tables():
    t = jnp.arange(L)
    row = (t // GRID_W).astype(F32)
    col = (t % GRID_W).astype(F32)
    n_freq = QK_DIM // 4
    inv = ROPE_BASE ** (-jnp.arange(n_freq, dtype=F32) / n_freq)
    ar = row[:, None] * inv
    ac = col[:, None] * inv
    cos64 = jnp.concatenate([jnp.cos(ar), jnp.cos(ar), jnp.cos(ac), jnp.cos(ac)], axis=1)
    sin64 = jnp.concatenate([-jnp.sin(ar), jnp.sin(ar), -jnp.sin(ac), jnp.sin(ac)], axis=1)
    cos_t = jnp.concatenate([jnp.tile(cos64, (1, 2)), jnp.ones((TM_MIX, 128), F32)], axis=0)
    sin_t = jnp.concatenate([jnp.tile(sin64, (1, 2)), jnp.zeros((TM_MIX, 128), F32)], axis=0)
    return cos_t, sin_t


def _split_bf16(w):
    hi = w.astype(BF16)
    return hi, (w - hi.astype(F32)).astype(BF16)


def kernel(x, c, ctx, c_ctx, w_ada, b_ada, g_norm1, g_norm2, w_in, w_out, g_v, w_s, b_s, w_conv, g_q, g_k,
           lam_q1, lam_k1, lam_q2, lam_k2, g_sub, w_router, b_router, w_gate, w_up, w_down,
           ws_gate, ws_up, ws_down):
    xa = jnp.concatenate([x.reshape(NL, D), ctx.reshape(NC, D)], axis=0)
    cc = jnp.concatenate([c, c_ctx[None, :], jnp.zeros((MOD_ROWS - B - 1, D), F32)], axis=0)
    mod = _ada(cc, w_ada, b_ada)
    cos_t, sin_t = _rope_tables()
    perm = _q_perm()
    col_perm = np.concatenate([np.arange(OFF_Q), OFF_Q + perm, OFF_K + perm, np.arange(OFF_V, D_IN)])
    bd = jnp.asarray(np.kron(np.eye(8, dtype=np.float32), np.full((64, 64), 1.0 / 64, np.float32)), BF16)

    for l in range(DEPTH):
        last = l == DEPTH - 1
        lam_init = 0.8 - 0.6 * math.exp(-0.3 * l)
        lam = (jnp.exp(jnp.sum(lam_q1[l] * lam_k1[l])) - jnp.exp(jnp.sum(lam_q2[l] * lam_k2[l]))
               + lam_init).reshape(1).astype(F32)
        mod3 = mod[l].reshape(MOD_ROWS, 1, 6 * D)
        w_in_bf = w_in[l][:, col_perm].astype(BF16)
        p = _in_proj(xa, g_norm1[l], mod3, w_in_bf)
        bias_t = jnp.repeat(b_s[l].T, A_GD, axis=1)
        wconv = jnp.concatenate([w_conv[l], jnp.zeros((5, B_WIDTH), F32)], axis=0)
        yab, qa, qb, kk = _mixers(p, cos_t, sin_t, g_v[l].reshape(1, A_WIDTH), w_s[l].astype(BF16), bias_t, wconv,
                                  jnp.tile(g_q[l], 8).reshape(1, 512), jnp.tile(g_k[l], 8).reshape(1, 512), bd)
        gsub = g_sub[l].reshape(1, V_DIM)
        coef = 1.0 - lam_init
        yc = _attention(lam, qa, qb, kk, p, gsub, coef, ctx_queries=False)
        if last:
            n_rows = NL
        else:
            n_rows = NR
            yc = jnp.concatenate([yc, _attention(lam, qa, qb, kk, p, gsub, coef, ctx_queries=True)], axis=0)
        wr_hi, wr_lo = _split_bf16(w_router[l].T)
        xa, h2, h2p, idx3, gate3, rank3, counts = _out_router(
            xa, yab, yc, w_out[l].astype(BF16), g_norm2[l], mod3, wr_hi, wr_lo, b_router[l], n_rows)
        pos3, tile_expert, n_used, r_pad = _route_positions(idx3, rank3, counts[:, 0], n_rows)
        x_sorted = _sc_dispatch(h2p.reshape(n_rows, 4, 128), pos3, r_pad)
        y_sorted = _experts(l, tile_expert, n_used, x_sorted.reshape(r_pad * 4, 128), w_gate, w_up, w_down)
        yg = _sc_collect(y_sorted.reshape(r_pad, 4, 128), pos3)
        gates = gate3.transpose(0, 2, 1).reshape(n_rows, TOP_K)
        xa = _shared_residual(xa, h2, yg.reshape(TOP_K, n_rows * 4, 128), gates, ws_gate[l].astype(BF16),
                              ws_up[l].astype(BF16), ws_down[l].astype(BF16), mod3, n_rows)
    return xa.reshape(B, L, D)
```

```python
import functools
import math

import numpy as np
import jax
import jax.numpy as jnp
from jax import lax
from jax.experimental import pallas as pl
from jax.experimental.pallas import tpu as pltpu
from jax.experimental.pallas import tpu_sc as plsc

F32 = jnp.float32
BF16 = jnp.bfloat16

D = 1024
B = 8
L = 2048
DEPTH = 2
GRID_W = 64
CTX = 256
A_WIDTH = 256
A_GROUPS = 4
A_GD = 64
CHUNK = 128
B_WIDTH = 256
C_WIDTH = 512
HEADS = 4
V_DIM = 128
QK_DIM = 64
ROPE_BASE = 10000.0
OFF_BB = 512
OFF_BC = 768
OFF_BX = 1024
OFF_Q = 1280
OFF_K = 1792
OFF_V = 2304
D_IN = 2816
N_EXPERTS = 64
TOP_K = 8
D_EXPERT = 256
D_SHARED = 256
ROUTED_SCALE = 2.5
EPS = 1e-6

NL = B * L
NC = B * CTX
NR = NL + NC
MOD_ROWS = 16
LOG2E = 1.4426950408889634

TM_IN = 512
TM_MIX = 256
TQ = 1024
ATT_CHAIN = 256
TM_OUT = 256
TM_X = 512
TM_F = 256

_DN_T = (((1,), (1,)), ((), ()))


def _cparams(sem, vmem_mb=None):
    kw = dict(dimension_semantics=sem)
    if vmem_mb is not None:
        kw["vmem_limit_bytes"] = vmem_mb * 1024 * 1024
    return pltpu.CompilerParams(**kw)


def _mod_row(i, tm):
    return jnp.where(i < NL // tm, i // (L // tm), B)


def _mod_spec(tm, chunk):
    return pl.BlockSpec((None, 1, D), lambda i: (_mod_row(i, tm), 0, chunk))


def _ada_kernel(c_ref, w_ref, b_ref, o_ref):
    c = c_ref[...]
    cs = c * jax.nn.sigmoid(c)
    o_ref[...] = jnp.dot(cs, w_ref[...], preferred_element_type=F32,
                         precision=lax.Precision.HIGHEST) + b_ref[...]


def _ada(cc, w_ada, b_ada):
    nb = 6
    return pl.pallas_call(
        _ada_kernel,
        out_shape=jax.ShapeDtypeStruct((DEPTH, MOD_ROWS, 6 * D), F32),
        grid=(DEPTH, nb),
        in_specs=[pl.BlockSpec((MOD_ROWS, D), lambda l, j: (0, 0)),
                  pl.BlockSpec((None, D, D), lambda l, j: (l, 0, j)),
                  pl.BlockSpec((None, 1, D), lambda l, j: (l, 0, j))],
        out_specs=pl.BlockSpec((None, MOD_ROWS, D), lambda l, j: (l, 0, j)),
        compiler_params=_cparams(("arbitrary", "arbitrary"), 40),
        name="ada_mod",
    )(cc, w_ada, b_ada.reshape(DEPTH, 1, 6 * D))


def _rms_mod(x, g, sc, sh):
    ms = jnp.mean(x * x, axis=-1, keepdims=True)
    return x * lax.rsqrt(ms + EPS) * g * (1.0 + sc) + sh


def _in_proj_kernel(x_ref, g_ref, sh_ref, sc_ref, w_ref, o_ref):
    h = _rms_mod(x_ref[...], g_ref[...], sc_ref[...], sh_ref[...])
    o_ref[...] = jnp.dot(h.astype(BF16), w_ref[...], preferred_element_type=F32).astype(o_ref.dtype)


def _in_proj(xa, g, mod3, w_bf):
    tm = TM_IN
    return pl.pallas_call(
        _in_proj_kernel,
        out_shape=jax.ShapeDtypeStruct((NR, D_IN), BF16),
        grid=(NR // tm,),
        in_specs=[pl.BlockSpec((tm, D), lambda i: (i, 0)),
                  pl.BlockSpec((1, D), lambda i: (0, 0)),
                  _mod_spec(tm, 0), _mod_spec(tm, 1),
                  pl.BlockSpec((D, D_IN), lambda i: (0, 0))],
        out_specs=pl.BlockSpec((tm, D_IN), lambda i: (i, 0)),
        compiler_params=_cparams(("parallel",), 48),
        name="in_proj",
    )(xa, g.reshape(1, D), mod3, mod3, w_bf)


def _group_rms(t, g, bd):
    sq = t * t
    hi = sq.astype(BF16)
    lo = (sq - hi.astype(F32)).astype(BF16)
    ms = (jnp.dot(hi, bd, preferred_element_type=F32) + jnp.dot(lo, bd, preferred_element_type=F32))
    return t * lax.rsqrt(ms + EPS) * g


def _rope(t, cos, sin):
    w = t.shape[1]
    lane = lax.broadcasted_iota(jnp.int32, t.shape, 1)
    first = (lane % 32) < 16
    partner = jnp.where(first, pltpu.roll(t, w - 16, 1), pltpu.roll(t, 16, 1))
    cos4 = jnp.concatenate([cos] * (w // 128), axis=1)
    sin4 = jnp.concatenate([sin] * (w // 128), axis=1)
    return t * cos4 + partner * sin4


def _mixers_kernel(p_ref, pc_ref, px_ref, nc_ref, nx_ref, cos_ref, sin_ref, gv_ref, ws_ref, bias_ref,
                   wconv_ref, gq_ref, gk_ref, bd_ref, yab_ref, qa_ref, qb_ref, kk_ref):
    tm = TM_MIX
    i = pl.program_id(0)
    tiles_per_seq = L // tm
    is_lat = i < NL // tm
    is_start = jnp.logical_or(jnp.logical_not(is_lat), i % tiles_per_seq == 0)
    is_end = jnp.logical_or(jnp.logical_not(is_lat), i % tiles_per_seq == tiles_per_seq - 1)

    uv = p_ref[:, 0:2 * A_WIDTH].astype(F32)
    uv = 0.5 * uv * (1.0 + lax.erf(uv * (2.0 ** -0.5)))
    u = uv[:, :A_WIDTH]
    v = uv[:, A_WIDTH:]
    ms = jnp.mean(v * v, axis=-1, keepdims=True)
    vb = (v * lax.rsqrt(ms + EPS) * gv_ref[...]).astype(BF16)
    lane = lax.broadcasted_iota(jnp.int32, (CHUNK, 128), 1)
    mixes = []
    for c in range(tm // CHUNK):
        vc = vb[c * CHUNK:(c + 1) * CHUNK]
        halves = []
        for j in range(2):
            vj = vc[:, j * 128:(j + 1) * 128]
            m0 = jnp.dot(ws_ref[2 * j], vj, preferred_element_type=F32)
            m1 = jnp.dot(ws_ref[2 * j + 1], vj, preferred_element_type=F32)
            halves.append(jnp.where(lane < A_GD, m0, m1))
        mixes.append(jnp.concatenate(halves, axis=1) + bias_ref[...])
    ya = u * jnp.concatenate(mixes, axis=0)

    bg = p_ref[:, OFF_BB:OFF_BC].astype(F32)
    z = p_ref[:, OFF_BC:OFF_BX].astype(F32) * p_ref[:, OFF_BX:OFF_Q].astype(F32)
    zp = (pc_ref[...].astype(F32) * px_ref[...].astype(F32))[15:16]
    zn = (nc_ref[...].astype(F32) * nx_ref[...].astype(F32))[0:1]
    zp = jnp.where(is_start, 0.0, zp)
    zn = jnp.where(is_end, 0.0, zn)
    row = lax.broadcasted_iota(jnp.int32, z.shape, 0)
    z_prev = jnp.where(row == 0, zp, pltpu.roll(z, 1, 0))
    z_next = jnp.where(row == tm - 1, zn, pltpu.roll(z, tm - 1, 0))
    yb = bg * (z_prev * wconv_ref[0:1] + z * wconv_ref[1:2] + z_next * wconv_ref[2:3])
    yab_ref[...] = jnp.concatenate([ya, yb], axis=1).astype(BF16)

    cos = cos_ref[...]
    sin = sin_ref[...]
    bd = bd_ref[...]
    q = _rope(_group_rms(p_ref[:, OFF_Q:OFF_K].astype(F32), gq_ref[...], bd), cos, sin)
    q = q * (QK_DIM ** -0.5 * LOG2E)
    lane5 = lax.broadcasted_iota(jnp.int32, q.shape, 1) % 128
    qa_ref[...] = jnp.where(lane5 < QK_DIM, q, 0.0).astype(BF16)
    qb_ref[...] = jnp.where(lane5 >= QK_DIM, q, 0.0).astype(BF16)
    k = _rope(_group_rms(p_ref[:, OFF_K:OFF_V].astype(F32), gk_ref[...], bd), cos, sin)
    kk_ref[...] = k.astype(BF16)


def _mixers(p, cos_t, sin_t, gv, ws_bf, bias_t, wconv, gq, gk, bd):
    tm = TM_MIX
    nt = NR // tm
    hb = tm // 16
    last_hb = NR // 16 - 1
    pos_blocks = L // tm

    def tab_map(i):
        return (jnp.where(i < NL // tm, i % pos_blocks, pos_blocks), 0)

    prev_map_c = lambda i: (jnp.maximum(i * hb - 1, 0), OFF_BC // 256)
    prev_map_x = lambda i: (jnp.maximum(i * hb - 1, 0), OFF_BX // 256)
    next_map_c = lambda i: (jnp.minimum((i + 1) * hb, last_hb), OFF_BC // 256)
    next_map_x = lambda i: (jnp.minimum((i + 1) * hb, last_hb), OFF_BX // 256)
    const2 = lambda i: (0, 0)
    outs = pl.pallas_call(
        _mixers_kernel,
        out_shape=[jax.ShapeDtypeStruct((NR, 2 * A_WIDTH), BF16),
                   jax.ShapeDtypeStruct((NR, 512), BF16),
                   jax.ShapeDtypeStruct((NR, 512), BF16),
                   jax.ShapeDtypeStruct((NR, 512), BF16)],
        grid=(nt,),
        in_specs=[pl.BlockSpec((tm, D_IN), lambda i: (i, 0)),
                  pl.BlockSpec((16, 256), prev_map_c), pl.BlockSpec((16, 256), prev_map_x),
                  pl.BlockSpec((16, 256), next_map_c), pl.BlockSpec((16, 256), next_map_x),
                  pl.BlockSpec((tm, 128), tab_map), pl.BlockSpec((tm, 128), tab_map),
                  pl.BlockSpec((1, A_WIDTH), const2),
                  pl.BlockSpec((A_GROUPS, CHUNK, CHUNK), lambda i: (0, 0, 0)),
                  pl.BlockSpec((CHUNK, A_WIDTH), const2),
                  pl.BlockSpec((8, B_WIDTH), const2),
                  pl.BlockSpec((1, 512), const2), pl.BlockSpec((1, 512), const2),
                  pl.BlockSpec((512, 512), const2)],
        out_specs=[pl.BlockSpec((tm, 512), lambda i: (i, 0))] * 4,
        compiler_params=_cparams(("parallel",), 48),
        name="mixers",
    )(p, p, p, p, p, cos_t, sin_t, gv, ws_bf, bias_t, wconv, gq, gk, bd)
    return outs


def _attn_kernel(lam_ref, qa_ref, qb_ref, *rest, n_seg, coef, tq):
    kv_refs = rest[:2 * n_seg]
    gsub_ref, o_ref, k_scr, vt_scr = rest[2 * n_seg:]

    @pl.when(pl.program_id(2) == 0)
    def _():
        off = 0
        for s in range(n_seg):
            n = kv_refs[s].shape[0]
            k_scr[off:off + n, :] = kv_refs[s][...]
            vt_scr[0:V_DIM, off:off + n] = kv_refs[n_seg + s][...].astype(F32).T.astype(BF16)
            off += n
        ones_row = lax.broadcasted_iota(jnp.int32, (16, off), 0) == 0
        vt_scr[V_DIM:, :] = jnp.where(ones_row, 1.0, 0.0).astype(BF16)

    k = k_scr[...]
    vt = vt_scr[...]
    lam = lam_ref[0]
    qc = ATT_CHAIN
    sts = []
    for c in range(tq // qc):
        rows = slice(c * qc, (c + 1) * qc)
        qs = jnp.concatenate([qa_ref[rows, :], qb_ref[rows, :]], axis=0)
        sts.append(lax.dot_general(k, qs, _DN_T, preferred_element_type=F32))
    for c in range(tq // qc):
        st = sts[c]
        pt = jnp.exp2(st - jnp.max(st, axis=0, keepdims=True)).astype(BF16)
        ot = jnp.dot(vt, pt, preferred_element_type=F32)
        inv = 1.0 / ot[V_DIM:V_DIM + 1, :]
        dt = ot[0:V_DIM, :qc] * inv[:, :qc] - ot[0:V_DIM, qc:] * (lam * inv[:, qc:])
        o = dt.T
        ms = jnp.mean(o * o, axis=-1, keepdims=True)
        o_ref[c * qc:(c + 1) * qc, :] = (o * lax.rsqrt(ms + EPS) * gsub_ref[...] * coef).astype(o_ref.dtype)


def _attention(lam, qa, qb, kk, p, gsub, coef, *, ctx_queries):
    vcol = OFF_V // 128
    if ctx_queries:
        tq = CTX
        nq, lk, n_seg = 1, CTX, 1
        q_map = lambda b, h, qi: (NL // tq + b, h)
        kv_specs = [pl.BlockSpec((CTX, 128), lambda b, h, qi: (NL // CTX + b, h)),
                    pl.BlockSpec((CTX, 128), lambda b, h, qi: (NL // CTX + b, vcol + h))]
        kv_args = [kk, p]
        rows = NC
        o_map = lambda b, h, qi: (b, h)
    else:
        tq = TQ
        nq, lk, n_seg = L // tq, CTX + L, 2
        q_map = lambda b, h, qi: (b * (L // tq) + qi, h)
        kv_specs = [pl.BlockSpec((CTX, 128), lambda b, h, qi: (NL // CTX + b, h)),
                    pl.BlockSpec((L, 128), lambda b, h, qi: (b, h)),
                    pl.BlockSpec((CTX, 128), lambda b, h, qi: (NL // CTX + b, vcol + h)),
                    pl.BlockSpec((L, 128), lambda b, h, qi: (b, vcol + h))]
        kv_args = [kk, kk, p, p]
        rows = NL
        o_map = lambda b, h, qi: (b * (L // tq) + qi, h)
    return pl.pallas_call(
        functools.partial(_attn_kernel, n_seg=n_seg, coef=coef, tq=tq),
        out_shape=jax.ShapeDtypeStruct((rows, C_WIDTH), BF16),
        grid=(B, HEADS, nq),
        in_specs=[pl.BlockSpec(memory_space=pltpu.SMEM),
                  pl.BlockSpec((tq, 128), q_map), pl.BlockSpec((tq, 128), q_map)]
                 + kv_specs + [pl.BlockSpec((1, V_DIM), lambda b, h, qi: (0, 0))],
        out_specs=pl.BlockSpec((tq, 128), o_map),
        scratch_shapes=[pltpu.VMEM((lk, 128), BF16), pltpu.VMEM((V_DIM + 16, lk), BF16)],
        compiler_params=_cparams(("parallel", "parallel", "arbitrary"), 56),
        name="attn_ctx" if ctx_queries else "attn_lat",
    )(lam, qa, qb, *kv_args, gsub)


def _pack_rows(t, out_ref):
    half = D // 2
    lo = lax.bitcast_convert_type(t[:, :half].astype(BF16).astype(F32), jnp.uint32) >> 16
    hi = lax.bitcast_convert_type(t[:, half:].astype(BF16).astype(F32), jnp.uint32) & jnp.uint32(0xFFFF0000)
    w = lo | hi
    rows = t.shape[0]
    for j in range(4):
        out_ref[pl.ds(j, rows, stride=4), :] = w[:, j * 128:(j + 1) * 128]


def _unpack_piece(w):
    lo = lax.bitcast_convert_type(w << 16, F32)
    hi = lax.bitcast_convert_type(w & jnp.uint32(0xFFFF0000), F32)
    return lo, hi


def _unpack_rows(ref, rows, lead=None):
    los, his = [], []
    for j in range(4):
        w = ref[pl.ds(j, rows, stride=4), :] if lead is None else ref[lead, pl.ds(j, rows, stride=4), :]
        lo, hi = _unpack_piece(w)
        los.append(lo)
        his.append(hi)
    return jnp.concatenate(los, axis=1), jnp.concatenate(his, axis=1)


def _out_router_kernel(x_ref, yab_ref, yc_ref, wo_ref, g1_ref, g2n_ref, sh2_ref, sc2_ref, wrh_ref, wrl_ref, br_ref,
                       xo_ref, h2_ref, h2p_ref, idx_ref, gate_ref, rank_ref, cnt_ref, run_ref):
    tm = TM_OUT

    @pl.when(pl.program_id(0) == 0)
    def _():
        run_ref[...] = jnp.zeros_like(run_ref)

    y = (jnp.dot(yab_ref[...], wo_ref[0:2 * A_WIDTH, :], preferred_element_type=F32)
         + jnp.dot(yc_ref[...], wo_ref[2 * A_WIDTH:, :], preferred_element_type=F32))
    x = x_ref[...] + g1_ref[...] * y
    xo_ref[...] = x
    h2 = _rms_mod(x, g2n_ref[...], sc2_ref[...], sh2_ref[...])
    hi = h2.astype(BF16)
    h2_ref[...] = hi
    _pack_rows(h2, h2p_ref)
    lo = (h2 - hi.astype(F32)).astype(BF16)
    wh = wrh_ref[...]
    z = (lax.dot_general(wh, hi, _DN_T, preferred_element_type=F32)
         + lax.dot_general(wh, lo, _DN_T, preferred_element_type=F32)
         + lax.dot_general(wrl_ref[...], hi, _DN_T, preferred_element_type=F32))
    scores = jax.nn.sigmoid(z)
    work = scores + br_ref[...]
    eio = lax.broadcasted_iota(jnp.int32, work.shape, 0)
    idxs, sels, hits = [], [], []
    for _ in range(TOP_K):
        m = jnp.max(work, axis=0, keepdims=True)
        idx = jnp.min(jnp.where(work == m, eio, N_EXPERTS), axis=0, keepdims=True)
        hit = eio == idx
        sels.append(jnp.sum(jnp.where(hit, scores, 0.0), axis=0, keepdims=True))
        idxs.append(idx)
        hits.append(hit)
        work = jnp.where(hit, -jnp.inf, work)
    sel = jnp.concatenate(sels, axis=0)
    idx8 = jnp.concatenate(idxs, axis=0)
    gate8 = sel / jnp.sum(sel, axis=0, keepdims=True) * ROUTED_SCALE

    chosen = functools.reduce(jnp.logical_or, hits)
    before = (lax.broadcasted_iota(jnp.int32, (tm, tm), 0) < lax.broadcasted_iota(jnp.int32, (tm, tm), 1))
    prefix = jnp.dot(jnp.where(chosen, 1.0, 0.0).astype(BF16), jnp.where(before, 1.0, 0.0).astype(BF16),
                     preferred_element_type=F32)
    rank_dense = prefix + run_ref[:, 0:1]
    rank8 = jnp.concatenate([jnp.sum(jnp.where(h, rank_dense, 0.0), axis=0, keepdims=True) for h in hits],
                            axis=0).astype(jnp.int32)
    run = run_ref[...] + jnp.sum(jnp.where(chosen, 1.0, 0.0), axis=1, keepdims=True)
    run_ref[...] = run
    cnt_ref[...] = run
    for c in range(tm // 128):
        idx_ref[c] = idx8[:, c * 128:(c + 1) * 128]
        gate_ref[c] = gate8[:, c * 128:(c + 1) * 128]
        rank_ref[c] = rank8[:, c * 128:(c + 1) * 128]


def _out_router(xa, yab, yc, wo_bf, g2n, mod3, wr_hi, wr_lo, br, n_rows):
    tm = TM_OUT
    const2 = lambda i: (0, 0)
    row = lambda i: (i, 0)
    chunk3 = pl.BlockSpec((tm // 128, TOP_K, 128), lambda i: (i, 0, 0))
    nch = n_rows // 128
    return pl.pallas_call(
        _out_router_kernel,
        out_shape=[jax.ShapeDtypeStruct((n_rows, D), F32),
                   jax.ShapeDtypeStruct((n_rows, D), BF16),
                   jax.ShapeDtypeStruct((n_rows * 4, 128), jnp.uint32),
                   jax.ShapeDtypeStruct((nch, TOP_K, 128), jnp.int32),
                   jax.ShapeDtypeStruct((nch, TOP_K, 128), F32),
                   jax.ShapeDtypeStruct((nch, TOP_K, 128), jnp.int32),
                   jax.ShapeDtypeStruct((N_EXPERTS, 128), F32)],
        grid=(n_rows // tm,),
        in_specs=[pl.BlockSpec((tm, D), row), pl.BlockSpec((tm, 512), row), pl.BlockSpec((tm, 512), row),
                  pl.BlockSpec((D, D), const2),
                  _mod_spec(tm, 2),
                  pl.BlockSpec((1, D), const2), _mod_spec(tm, 3), _mod_spec(tm, 4),
                  pl.BlockSpec((N_EXPERTS, D), const2), pl.BlockSpec((N_EXPERTS, D), const2),
                  pl.BlockSpec((N_EXPERTS, 1), const2)],
        out_specs=[pl.BlockSpec((tm, D), row), pl.BlockSpec((tm, D), row), pl.BlockSpec((tm * 4, 128), row),
                   chunk3, chunk3, chunk3, pl.BlockSpec((N_EXPERTS, 128), const2)],
        scratch_shapes=[pltpu.VMEM((N_EXPERTS, 128), F32)],
        compiler_params=_cparams(("arbitrary",), 48),
        name="out_router",
    )(xa, yab, yc, wo_bf, mod3, g2n.reshape(1, D), mod3, mod3, wr_hi, wr_lo, br.reshape(N_EXPERTS, 1))


def _experts_kernel(te_ref, nu_ref, x_ref, wg_ref, wu_ref, wd_ref, y_ref, wgu_s, wd_s):
    i = pl.program_id(0)
    used = i < nu_ref[0]
    new_expert = jnp.logical_or(i == 0, te_ref[i] != te_ref[jnp.maximum(i - 1, 0)])

    @pl.when(jnp.logical_and(used, new_expert))
    def _():
        wgu_s[:, 0:D_EXPERT] = wg_ref[...].astype(BF16)
        wgu_s[:, D_EXPERT:] = wu_ref[...].astype(BF16)
        wd_s[...] = wd_ref[...].astype(BF16)

    @pl.when(used)
    def _():
        half = D // 2
        x_lo, x_hi = _unpack_rows(x_ref, TM_X)
        ab = (jnp.dot(x_lo.astype(BF16), wgu_s[0:half, :], preferred_element_type=F32)
              + jnp.dot(x_hi.astype(BF16), wgu_s[half:, :], preferred_element_type=F32))
        a = ab[:, :D_EXPERT]
        hid = (a * jax.nn.sigmoid(a) * ab[:, D_EXPERT:]).astype(BF16)
        _pack_rows(jnp.dot(hid, wd_s[...], preferred_element_type=F32), y_ref)


def _experts(layer, tile_expert, n_used, x_sorted, w_gate, w_up, w_down):
    tm = TM_X
    r_pad = x_sorted.shape[0] // 4
    nt = r_pad // tm
    row = lambda i, te, nu: (jnp.minimum(i, nu[0] - 1), 0)
    wmap = lambda i, te, nu: (layer, te[i], 0, 0)
    grid_spec = pltpu.PrefetchScalarGridSpec(
        num_scalar_prefetch=2,
        grid=(nt,),
        in_specs=[pl.BlockSpec((tm * 4, 128), row),
                  pl.BlockSpec((None, None, D, D_EXPERT), wmap),
                  pl.BlockSpec((None, None, D, D_EXPERT), wmap),
                  pl.BlockSpec((None, None, D_EXPERT, D), wmap)],
        out_specs=pl.BlockSpec((tm * 4, 128), row),
        scratch_shapes=[pltpu.VMEM((D, 2 * D_EXPERT), BF16), pltpu.VMEM((D_EXPERT, D), BF16)],
    )
    return pl.pallas_call(
        _experts_kernel,
        out_shape=jax.ShapeDtypeStruct((r_pad * 4, 128), jnp.uint32),
        grid_spec=grid_spec,
        compiler_params=_cparams(("arbitrary",), 48),
        name="experts",
    )(tile_expert, n_used, x_sorted, w_gate, w_up, w_down)


SC_CORES = 2
SC_SUBCORES = 16
SC_WORKERS = SC_CORES * SC_SUBCORES
SC_CHUNK = 128


def _sc_mesh():
    return plsc.VectorSubcoreMesh(core_axis_name="c", subcore_axis_name="s")


def _sc_params():
    return pltpu.CompilerParams(use_tc_tiling_on_sc=True)


def _sc_dispatch(h2p, pos3, r_pad):
    nch = pos3.shape[0]
    steps = -(-nch // SC_WORKERS)

    def body(h_hbm, pos_hbm, out_hbm, idx_v, rows_v, sem):
        wid = lax.axis_index("s") * SC_CORES + lax.axis_index("c")

        @pl.loop(0, steps)
        def _(s):
            ch = wid + s * SC_WORKERS

            @pl.when(ch < nch)
            def _():
                pltpu.sync_copy(pos_hbm.at[ch], idx_v)
                pltpu.sync_copy(h_hbm.at[pl.ds(ch * SC_CHUNK, SC_CHUNK)], rows_v)
                copies = [pltpu.async_copy(rows_v, out_hbm.at[idx_v.at[k]], sem) for k in range(TOP_K)]
                for cp in copies:
                    cp.wait()

    return pl.kernel(
        body,
        out_type=jax.ShapeDtypeStruct((r_pad, 4, 128), jnp.uint32),
        mesh=_sc_mesh(),
        scratch_types=[pltpu.VMEM((TOP_K, SC_CHUNK), jnp.int32),
                       pltpu.VMEM((SC_CHUNK, 4, 128), jnp.uint32),
                       pltpu.SemaphoreType.DMA],
        compiler_params=_sc_params(),
        name="sc_dispatch",
    )(h2p, pos3)


def _sc_collect(y_sorted, pos3):
    nch = pos3.shape[0]
    steps = -(-nch // SC_WORKERS)

    def body(y_hbm, pos_hbm, out_hbm, idx_v, rows_v, sem):
        wid = lax.axis_index("s") * SC_CORES + lax.axis_index("c")

        @pl.loop(0, steps)
        def _(s):
            ch = wid + s * SC_WORKERS

            @pl.when(ch < nch)
            def _():
                pltpu.sync_copy(pos_hbm.at[ch], idx_v)
                for k in range(TOP_K):
                    pltpu.async_copy(y_hbm.at[idx_v.at[k]], rows_v, sem).wait()
                    pltpu.sync_copy(rows_v, out_hbm.at[k, pl.ds(ch * SC_CHUNK, SC_CHUNK)])

    return pl.kernel(
        body,
        out_type=jax.ShapeDtypeStruct((TOP_K, nch * SC_CHUNK, 4, 128), jnp.uint32),
        mesh=_sc_mesh(),
        scratch_types=[pltpu.VMEM((TOP_K, SC_CHUNK), jnp.int32),
                       pltpu.VMEM((SC_CHUNK, 4, 128), jnp.uint32),
                       pltpu.SemaphoreType.DMA],
        compiler_params=_sc_params(),
        name="sc_collect",
    )(y_sorted, pos3)


def _shared_kernel(x_ref, h2_ref, yg_ref, gate_ref, wg_ref, wu_ref, wd_ref, g2_ref, o_ref):
    h = h2_ref[...]
    a = jnp.dot(h, wg_ref[...], preferred_element_type=F32)
    b = jnp.dot(h, wu_ref[...], preferred_element_type=F32)
    hid = (a * jax.nn.sigmoid(a) * b).astype(BF16)
    f = jnp.dot(hid, wd_ref[...], preferred_element_type=F32)
    gate = gate_ref[...]
    f_lo = f[:, :D // 2]
    f_hi = f[:, D // 2:]
    for k in range(TOP_K):
        y_lo, y_hi = _unpack_rows(yg_ref, TM_F, lead=k)
        f_lo = f_lo + gate[:, k:k + 1] * y_lo
        f_hi = f_hi + gate[:, k:k + 1] * y_hi
    o_ref[...] = x_ref[...] + g2_ref[...] * jnp.concatenate([f_lo, f_hi], axis=1)


def _shared_residual(xa, h2, yg, gates, wsg_bf, wsu_bf, wsd_bf, mod3, n_rows):
    tm = TM_F
    row = lambda i: (i, 0)
    const2 = lambda i: (0, 0)
    return pl.pallas_call(
        _shared_kernel,
        out_shape=jax.ShapeDtypeStruct((n_rows, D), F32),
        grid=(n_rows // tm,),
        in_specs=[pl.BlockSpec((tm, D), row), pl.BlockSpec((tm, D), row),
                  pl.BlockSpec((TOP_K, tm * 4, 128), lambda i: (0, i, 0)),
                  pl.BlockSpec((tm, TOP_K), row),
                  pl.BlockSpec((D, D_SHARED), const2), pl.BlockSpec((D, D_SHARED), const2),
                  pl.BlockSpec((D_SHARED, D), const2), _mod_spec(tm, 5)],
        out_specs=pl.BlockSpec((tm, D), row),
        compiler_params=_cparams(("parallel",), 48),
        name="shared_residual",
    )(xa, h2, yg, gates, wsg_bf, wsu_bf, wsd_bf, mod3)


def _positions_kernel(offs_ref, idx_ref, rank_ref, pos_ref):
    idx = idx_ref[...]
    base = jnp.zeros_like(idx)
    for e in range(N_EXPERTS):
        base = jnp.where(idx == e, offs_ref[e], base)
    pos_ref[...] = rank_ref[...] + base


def _positions(offs, idx3, rank3):
    nch = idx3.shape[0]
    cb = 16
    spec = pl.BlockSpec((cb, TOP_K, 128), lambda i, offs: (i, 0, 0))
    return pl.pallas_call(
        _positions_kernel,
        out_shape=jax.ShapeDtypeStruct((nch, TOP_K, 128), jnp.int32),
        grid_spec=pltpu.PrefetchScalarGridSpec(num_scalar_prefetch=1, grid=(nch // cb,),
                                               in_specs=[spec, spec], out_specs=spec),
        compiler_params=_cparams(("parallel",)),
        name="positions",
    )(offs, idx3, rank3)


def _route_positions(idx3, rank3, counts, n_rows):
    tm = TM_X
    counts = counts.astype(jnp.int32)
    padded = ((counts + tm - 1) // tm) * tm
    ends = jnp.cumsum(padded)
    offs = ends - padded
    pos3 = _positions(offs.astype(jnp.int32), idx3, rank3)
    r_pad = n_rows * TOP_K + N_EXPERTS * tm
    nt = r_pad // tm
    tile_ids = jnp.arange(nt, dtype=jnp.int32)
    tile_expert = jnp.sum((ends // tm)[None, :] <= tile_ids[:, None], axis=1)
    tile_expert = jnp.minimum(tile_expert, N_EXPERTS - 1).astype(jnp.int32)
    n_used = (ends[-1] // tm).astype(jnp.int32).reshape(1)
    return pos3, tile_expert, n_used, r_pad


def _q_perm():
    perm = np.empty(512, np.int32)
    for h in range(HEADS):
        for half in range(2):
            for d in range(QK_DIM):
                perm[h * 128 + half * QK_DIM + d] = half * HEADS * QK_DIM + h * QK_DIM + d
    return perm


def _rope_tables():
    t = jnp.arange(L)
    row = (t // GRID_W).astype(F32)
    col = (t % GRID_W).astype(F32)
    n_freq = QK_DIM // 4
    inv = ROPE_BASE ** (-jnp.arange(n_freq, dtype=F32) / n_freq)
    ar = row[:, None] * inv
    ac = col[:, None] * inv
    cos64 = jnp.concatenate([jnp.cos(ar), jnp.cos(ar), jnp.cos(ac), jnp.cos(ac)], axis=1)
    sin64 = jnp.concatenate([-jnp.sin(ar), jnp.sin(ar), -jnp.sin(ac), jnp.sin(ac)], axis=1)
    cos_t = jnp.concatenate([jnp.tile(cos64, (1, 2)), jnp.ones((TM_MIX, 128), F32)], axis=0)
    sin_t = jnp.concatenate([jnp.tile(sin64, (1, 2)), jnp.zeros((TM_MIX, 128), F32)], axis=0)
    return cos_t, sin_t


def _split_bf16(w):
    hi = w.astype(BF16)
    return hi, (w - hi.astype(F32)).astype(BF16)


def kernel(x, c, ctx, c_ctx, w_ada, b_ada, g_norm1, g_norm2, w_in, w_out, g_v, w_s, b_s, w_conv, g_q, g_k,
           lam_q1, lam_k1, lam_q2, lam_k2, g_sub, w_router, b_router, w_gate, w_up, w_down,
           ws_gate, ws_up, ws_down):
    xa = jnp.concatenate([x.reshape(NL, D), ctx.reshape(NC, D)], axis=0)
    cc = jnp.concatenate([c, c_ctx[None, :], jnp.zeros((MOD_ROWS - B - 1, D), F32)], axis=0)
    mod = _ada(cc, w_ada, b_ada)
    cos_t, sin_t = _rope_tables()
    perm = _q_perm()
    col_perm = np.concatenate([np.arange(OFF_Q), OFF_Q + perm, OFF_K + perm, np.arange(OFF_V, D_IN)])
    bd = jnp.asarray(np.kron(np.eye(8, dtype=np.float32), np.full((64, 64), 1.0 / 64, np.float32)), BF16)

    for l in range(DEPTH):
        last = l == DEPTH - 1
        lam_init = 0.8 - 0.6 * math.exp(-0.3 * l)
        lam = (jnp.exp(jnp.sum(lam_q1[l] * lam_k1[l])) - jnp.exp(jnp.sum(lam_q2[l] * lam_k2[l]))
               + lam_init).reshape(1).astype(F32)
        mod3 = mod[l].reshape(MOD_ROWS, 1, 6 * D)
        w_in_bf = w_in[l][:, col_perm].astype(BF16)
        p = _in_proj(xa, g_norm1[l], mod3, w_in_bf)
        bias_t = jnp.repeat(b_s[l].T, A_GD, axis=1)
        wconv = jnp.concatenate([w_conv[l], jnp.zeros((5, B_WIDTH), F32)], axis=0)
        yab, qa, qb, kk = _mixers(p, cos_t, sin_t, g_v[l].reshape(1, A_WIDTH), w_s[l].astype(BF16), bias_t, wconv,
                                  jnp.tile(g_q[l], 8).reshape(1, 512), jnp.tile(g_k[l], 8).reshape(1, 512), bd)
        gsub = g_sub[l].reshape(1, V_DIM)
        coef = 1.0 - lam_init
        yc = _attention(lam, qa, qb, kk, p, gsub, coef, ctx_queries=False)
        if last:
            n_rows = NL
        else:
            n_rows = NR
            yc = jnp.concatenate([yc, _attention(lam, qa, qb, kk, p, gsub, coef, ctx_queries=True)], axis=0)
        wr_hi, wr_lo = _split_bf16(w_router[l].T)
        xa, h2, h2p, idx3, gate3, rank3, counts = _out_router(
            xa, yab, yc, w_out[l].astype(BF16), g_norm2[l], mod3, wr_hi, wr_lo, b_router[l], n_rows)
        pos3, tile_expert, n_used, r_pad = _route_positions(idx3, rank3, counts[:, 0], n_rows)
        x_sorted = _sc_dispatch(h2p.reshape(n_rows, 4, 128), pos3, r_pad)
        y_sorted = _experts(l, tile_expert, n_used, x_sorted.reshape(r_pad * 4, 128), w_gate, w_up, w_down)
        yg = _sc_collect(y_sorted.reshape(r_pad, 4, 128), pos3)
        gates = gate3.transpose(0, 2, 1).reshape(n_rows, TOP_K)
        xa = _shared_residual(xa, h2, yg.reshape(TOP_K, n_rows * 4, 128), gates, ws_gate[l].astype(BF16),
                              ws_up[l].astype(BF16), ws_down[l].astype(BF16), mod3, n_rows)
    return xa.reshape(B, L, D)
```

```python
import functools
import math

import numpy as np
import jax
import jax.numpy as jnp
from jax import lax
from jax.experimental import pallas as pl
from jax.experimental.pallas import tpu as pltpu
from jax.experimental.pallas import tpu_sc as plsc

F32 = jnp.float32
BF16 = jnp.bfloat16

D = 1024
B = 8
L = 2048
DEPTH = 2
GRID_W = 64
CTX = 256
A_WIDTH = 256
A_GROUPS = 4
A_GD = 64
CHUNK = 128
B_WIDTH = 256
C_WIDTH = 512
HEADS = 4
V_DIM = 128
QK_DIM = 64
ROPE_BASE = 10000.0
OFF_BB = 512
OFF_BC = 768
OFF_BX = 1024
OFF_Q = 1280
OFF_K = 1792
OFF_V = 2304
D_IN = 2816
N_EXPERTS = 64
TOP_K = 8
D_EXPERT = 256
D_SHARED = 256
ROUTED_SCALE = 2.5
EPS = 1e-6

NL = B * L
NC = B * CTX
NR = NL + NC
MOD_ROWS = 16
LOG2E = 1.4426950408889634

TM_IN = 512
TM_MIX = 256
TQ = 1024
ATT_CHAIN = 256
TM_OUT = 512
TM_X = 512
X_CHAINS = 2
TM_F = 256

_DN_T = (((1,), (1,)), ((), ()))


def _cparams(sem, vmem_mb=None):
    kw = dict(dimension_semantics=sem)
    if vmem_mb is not None:
        kw["vmem_limit_bytes"] = vmem_mb * 1024 * 1024
    return pltpu.CompilerParams(**kw)


def _mod_row(i, tm):
    return jnp.where(i < NL // tm, i // (L // tm), B)


def _mod_spec(tm, chunk):
    return pl.BlockSpec((None, 1, D), lambda i: (_mod_row(i, tm), 0, chunk))


def _ada_kernel(c_ref, w_ref, b_ref, o_ref):
    c = c_ref[...]
    cs = c * jax.nn.sigmoid(c)
    o_ref[...] = jnp.dot(cs, w_ref[...], preferred_element_type=F32,
                         precision=lax.Precision.HIGHEST) + b_ref[...]


def _ada(cc, w_ada, b_ada):
    nb = 6
    return pl.pallas_call(
        _ada_kernel,
        out_shape=jax.ShapeDtypeStruct((DEPTH, MOD_ROWS, 6 * D), F32),
        grid=(DEPTH, nb),
        in_specs=[pl.BlockSpec((MOD_ROWS, D), lambda l, j: (0, 0)),
                  pl.BlockSpec((None, D, D), lambda l, j: (l, 0, j)),
                  pl.BlockSpec((None, 1, D), lambda l, j: (l, 0, j))],
        out_specs=pl.BlockSpec((None, MOD_ROWS, D), lambda l, j: (l, 0, j)),
        compiler_params=_cparams(("arbitrary", "arbitrary"), 40),
        name="ada_mod",
    )(cc, w_ada, b_ada.reshape(DEPTH, 1, 6 * D))


def _rms_mod(x, g, sc, sh):
    ms = jnp.mean(x * x, axis=-1, keepdims=True)
    return x * lax.rsqrt(ms + EPS) * g * (1.0 + sc) + sh


def _two_source_specs(tm, n_first, width=D):
    return [pl.BlockSpec((tm, width), lambda i: (jnp.minimum(i, n_first - 1), 0)),
            pl.BlockSpec((tm, width), lambda i: (jnp.maximum(i - n_first, 0), 0))]


def _two_source_rows(a_ref, b_ref, n_first):
    return jnp.where(pl.program_id(0) < n_first, a_ref[...], b_ref[...])


def _in_proj_kernel(xa_ref, xb_ref, g_ref, sh_ref, sc_ref, w_ref, o_ref, *, n_first):
    h = _rms_mod(_two_source_rows(xa_ref, xb_ref, n_first), g_ref[...], sc_ref[...], sh_ref[...])
    o_ref[...] = jnp.dot(h.astype(BF16), w_ref[...], preferred_element_type=F32).astype(o_ref.dtype)


def _in_proj(x_first, x_second, g, mod3, w_bf):
    tm = TM_IN
    n_first = x_first.shape[0] // tm
    return pl.pallas_call(
        functools.partial(_in_proj_kernel, n_first=n_first),
        out_shape=jax.ShapeDtypeStruct((NR, D_IN), BF16),
        grid=(NR // tm,),
        in_specs=_two_source_specs(tm, n_first)
                 + [pl.BlockSpec((1, D), lambda i: (0, 0)),
                    _mod_spec(tm, 0), _mod_spec(tm, 1),
                    pl.BlockSpec((D, D_IN), lambda i: (0, 0))],
        out_specs=pl.BlockSpec((tm, D_IN), lambda i: (i, 0)),
        compiler_params=_cparams(("parallel",), 48),
        name="in_proj",
    )(x_first, x_second, g.reshape(1, D), mod3, mod3, w_bf)


def _group_rms(t, g, bd):
    sq = t * t
    hi = sq.astype(BF16)
    lo = (sq - hi.astype(F32)).astype(BF16)
    ms = (jnp.dot(hi, bd, preferred_element_type=F32) + jnp.dot(lo, bd, preferred_element_type=F32))
    return t * lax.rsqrt(ms + EPS) * g


def _rope(t, cos, sin):
    w = t.shape[1]
    lane = lax.broadcasted_iota(jnp.int32, t.shape, 1)
    first = (lane % 32) < 16
    partner = jnp.where(first, pltpu.roll(t, w - 16, 1), pltpu.roll(t, 16, 1))
    cos4 = jnp.concatenate([cos] * (w // 128), axis=1)
    sin4 = jnp.concatenate([sin] * (w // 128), axis=1)
    return t * cos4 + partner * sin4


def _mixers_kernel(p_ref, pc_ref, px_ref, nc_ref, nx_ref, cos_ref, sin_ref, gv_ref, ws_ref, bias_ref,
                   wconv_ref, gq_ref, gk_ref, bd_ref, yab_ref, qa_ref, qb_ref, kk_ref):
    tm = TM_MIX
    i = pl.program_id(0)
    tiles_per_seq = L // tm
    is_lat = i < NL // tm
    is_start = jnp.logical_or(jnp.logical_not(is_lat), i % tiles_per_seq == 0)
    is_end = jnp.logical_or(jnp.logical_not(is_lat), i % tiles_per_seq == tiles_per_seq - 1)

    uv = p_ref[:, 0:2 * A_WIDTH].astype(F32)
    uv = 0.5 * uv * (1.0 + lax.erf(uv * (2.0 ** -0.5)))
    u = uv[:, :A_WIDTH]
    v = uv[:, A_WIDTH:]
    ms = jnp.mean(v * v, axis=-1, keepdims=True)
    vb = (v * lax.rsqrt(ms + EPS) * gv_ref[...]).astype(BF16)
    lane = lax.broadcasted_iota(jnp.int32, (CHUNK, 128), 1)
    mixes = []
    for c in range(tm // CHUNK):
        vc = vb[c * CHUNK:(c + 1) * CHUNK]
        halves = []
        for j in range(2):
            vj = vc[:, j * 128:(j + 1) * 128]
            m0 = jnp.dot(ws_ref[2 * j], vj, preferred_element_type=F32)
            m1 = jnp.dot(ws_ref[2 * j + 1], vj, preferred_element_type=F32)
            halves.append(jnp.where(lane < A_GD, m0, m1))
        mixes.append(jnp.concatenate(halves, axis=1) + bias_ref[...])
    ya = u * jnp.concatenate(mixes, axis=0)

    bg = p_ref[:, OFF_BB:OFF_BC].astype(F32)
    z = p_ref[:, OFF_BC:OFF_BX].astype(F32) * p_ref[:, OFF_BX:OFF_Q].astype(F32)
    zp = (pc_ref[...].astype(F32) * px_ref[...].astype(F32))[15:16]
    zn = (nc_ref[...].astype(F32) * nx_ref[...].astype(F32))[0:1]
    zp = jnp.where(is_start, 0.0, zp)
    zn = jnp.where(is_end, 0.0, zn)
    row = lax.broadcasted_iota(jnp.int32, z.shape, 0)
    z_prev = jnp.where(row == 0, zp, pltpu.roll(z, 1, 0))
    z_next = jnp.where(row == tm - 1, zn, pltpu.roll(z, tm - 1, 0))
    yb = bg * (z_prev * wconv_ref[0:1] + z * wconv_ref[1:2] + z_next * wconv_ref[2:3])
    yab_ref[...] = jnp.concatenate([ya, yb], axis=1).astype(BF16)

    cos = cos_ref[...]
    sin = sin_ref[...]
    bd = bd_ref[...]
    q = _rope(_group_rms(p_ref[:, OFF_Q:OFF_K].astype(F32), gq_ref[...], bd), cos, sin)
    q = q * (QK_DIM ** -0.5 * LOG2E)
    lane5 = lax.broadcasted_iota(jnp.int32, q.shape, 1) % 128
    qa_ref[...] = jnp.where(lane5 < QK_DIM, q, 0.0).astype(BF16)
    qb_ref[...] = jnp.where(lane5 >= QK_DIM, q, 0.0).astype(BF16)
    k = _rope(_group_rms(p_ref[:, OFF_K:OFF_V].astype(F32), gk_ref[...], bd), cos, sin)
    kk_ref[...] = k.astype(BF16)


def _mixers(p, cos_t, sin_t, gv, ws_bf, bias_t, wconv, gq, gk, bd):
    tm = TM_MIX
    nt = NR // tm
    hb = tm // 16
    last_hb = NR // 16 - 1
    pos_blocks = L // tm

    def tab_map(i):
        return (jnp.where(i < NL // tm, i % pos_blocks, pos_blocks), 0)

    prev_map_c = lambda i: (jnp.maximum(i * hb - 1, 0), OFF_BC // 256)
    prev_map_x = lambda i: (jnp.maximum(i * hb - 1, 0), OFF_BX // 256)
    next_map_c = lambda i: (jnp.minimum((i + 1) * hb, last_hb), OFF_BC // 256)
    next_map_x = lambda i: (jnp.minimum((i + 1) * hb, last_hb), OFF_BX // 256)
    const2 = lambda i: (0, 0)
    outs = pl.pallas_call(
        _mixers_kernel,
        out_shape=[jax.ShapeDtypeStruct((NR, 2 * A_WIDTH), BF16),
                   jax.ShapeDtypeStruct((NR, 512), BF16),
                   jax.ShapeDtypeStruct((NR, 512), BF16),
                   jax.ShapeDtypeStruct((NR, 512), BF16)],
        grid=(nt,),
        in_specs=[pl.BlockSpec((tm, D_IN), lambda i: (i, 0)),
                  pl.BlockSpec((16, 256), prev_map_c), pl.BlockSpec((16, 256), prev_map_x),
                  pl.BlockSpec((16, 256), next_map_c), pl.BlockSpec((16, 256), next_map_x),
                  pl.BlockSpec((tm, 128), tab_map), pl.BlockSpec((tm, 128), tab_map),
                  pl.BlockSpec((1, A_WIDTH), const2),
                  pl.BlockSpec((A_GROUPS, CHUNK, CHUNK), lambda i: (0, 0, 0)),
                  pl.BlockSpec((CHUNK, A_WIDTH), const2),
                  pl.BlockSpec((8, B_WIDTH), const2),
                  pl.BlockSpec((1, 512), const2), pl.BlockSpec((1, 512), const2),
                  pl.BlockSpec((512, 512), const2)],
        out_specs=[pl.BlockSpec((tm, 512), lambda i: (i, 0))] * 4,
        compiler_params=_cparams(("parallel",), 48),
        name="mixers",
    )(p, p, p, p, p, cos_t, sin_t, gv, ws_bf, bias_t, wconv, gq, gk, bd)
    return outs


def _attn_kernel(lam_ref, qa_ref, qb_ref, *rest, n_seg, coef, tq):
    kv_refs = rest[:2 * n_seg]
    gsub_ref, o_ref, k_scr, vt_scr = rest[2 * n_seg:]

    @pl.when(pl.program_id(2) == 0)
    def _():
        off = 0
        for s in range(n_seg):
            n = kv_refs[s].shape[0]
            k_scr[off:off + n, :] = kv_refs[s][...]
            vt_scr[0:V_DIM, off:off + n] = kv_refs[n_seg + s][...].astype(F32).T.astype(BF16)
            off += n
        ones_row = lax.broadcasted_iota(jnp.int32, (16, off), 0) == 0
        vt_scr[V_DIM:, :] = jnp.where(ones_row, 1.0, 0.0).astype(BF16)

    k = k_scr[...]
    vt = vt_scr[...]
    lam = lam_ref[0]
    qc = ATT_CHAIN
    sts = []
    for c in range(tq // qc):
        rows = slice(c * qc, (c + 1) * qc)
        qs = jnp.concatenate([qa_ref[rows, :], qb_ref[rows, :]], axis=0)
        sts.append(lax.dot_general(k, qs, _DN_T, preferred_element_type=F32))
    for c in range(tq // qc):
        st = sts[c]
        pt = jnp.exp2(st - jnp.max(st, axis=0, keepdims=True)).astype(BF16)
        ot = jnp.dot(vt, pt, preferred_element_type=F32)
        inv = 1.0 / ot[V_DIM:V_DIM + 1, :]
        dt = ot[0:V_DIM, :qc] * inv[:, :qc] - ot[0:V_DIM, qc:] * (lam * inv[:, qc:])
        o = dt.T
        ms = jnp.mean(o * o, axis=-1, keepdims=True)
        o_ref[c * qc:(c + 1) * qc, :] = (o * lax.rsqrt(ms + EPS) * gsub_ref[...] * coef).astype(o_ref.dtype)


def _attention(lam, qa, qb, kk, p, gsub, coef, *, ctx_queries):
    vcol = OFF_V // 128
    if ctx_queries:
        tq = CTX
        nq, lk, n_seg = 1, CTX, 1
        q_map = lambda b, h, qi: (NL // tq + b, h)
        kv_specs = [pl.BlockSpec((CTX, 128), lambda b, h, qi: (NL // CTX + b, h)),
                    pl.BlockSpec((CTX, 128), lambda b, h, qi: (NL // CTX + b, vcol + h))]
        kv_args = [kk, p]
        rows = NC
        o_map = lambda b, h, qi: (b, h)
    else:
        tq = TQ
        nq, lk, n_seg = L // tq, CTX + L, 2
        q_map = lambda b, h, qi: (b * (L // tq) + qi, h)
        kv_specs = [pl.BlockSpec((CTX, 128), lambda b, h, qi: (NL // CTX + b, h)),
                    pl.BlockSpec((L, 128), lambda b, h, qi: (b, h)),
                    pl.BlockSpec((CTX, 128), lambda b, h, qi: (NL // CTX + b, vcol + h)),
                    pl.BlockSpec((L, 128), lambda b, h, qi: (b, vcol + h))]
        kv_args = [kk, kk, p, p]
        rows = NL
        o_map = lambda b, h, qi: (b * (L // tq) + qi, h)
    return pl.pallas_call(
        functools.partial(_attn_kernel, n_seg=n_seg, coef=coef, tq=tq),
        out_shape=jax.ShapeDtypeStruct((rows, C_WIDTH), BF16),
        grid=(B, HEADS, nq),
        in_specs=[pl.BlockSpec(memory_space=pltpu.SMEM),
                  pl.BlockSpec((tq, 128), q_map), pl.BlockSpec((tq, 128), q_map)]
                 + kv_specs + [pl.BlockSpec((1, V_DIM), lambda b, h, qi: (0, 0))],
        out_specs=pl.BlockSpec((tq, 128), o_map),
        scratch_shapes=[pltpu.VMEM((lk, 128), BF16), pltpu.VMEM((V_DIM + 16, lk), BF16)],
        compiler_params=_cparams(("parallel", "parallel", "arbitrary"), 56),
        name="attn_ctx" if ctx_queries else "attn_lat",
    )(lam, qa, qb, *kv_args, gsub)


def _pack_rows(t, out_ref, row0=0):
    half = D // 2
    w = pltpu.pack_elementwise([t[:, :half], t[:, half:]], packed_dtype=BF16)
    w = lax.bitcast_convert_type(w, jnp.uint32)
    rows = t.shape[0]
    for j in range(4):
        out_ref[pl.ds(4 * row0 + j, rows, stride=4), :] = w[:, j * 128:(j + 1) * 128]


def _unpack_rows(ref, rows, lead=None, row0=0):
    los, his = [], []
    for j in range(4):
        sl = pl.ds(4 * row0 + j, rows, stride=4)
        w = ref[sl, :] if lead is None else ref[lead, sl, :]
        los.append(pltpu.unpack_elementwise(w, index=0, packed_dtype=BF16, unpacked_dtype=F32))
        his.append(pltpu.unpack_elementwise(w, index=1, packed_dtype=BF16, unpacked_dtype=F32))
    return jnp.concatenate(los, axis=1), jnp.concatenate(his, axis=1)


def _out_router_kernel(xa_ref, xb_ref, yab_ref, yca_ref, ycb_ref, wo_ref, g1_ref, g2n_ref, sh2_ref, sc2_ref, wrh_ref,
                       wrl_ref, br_ref, xo_ref, h2_ref, h2p_ref, idx_ref, gate_ref, rank_ref, cnt_ref, run_ref, *,
                       n_first, n_first_c):
    tm = TM_OUT

    @pl.when(pl.program_id(0) == 0)
    def _():
        run_ref[...] = jnp.zeros_like(run_ref)

    yc = _two_source_rows(yca_ref, ycb_ref, n_first_c)
    y = jnp.dot(jnp.concatenate([yab_ref[...], yc], axis=1), wo_ref[...], preferred_element_type=F32)
    x = _two_source_rows(xa_ref, xb_ref, n_first) + g1_ref[...] * y
    xo_ref[...] = x
    h2 = _rms_mod(x, g2n_ref[...], sc2_ref[...], sh2_ref[...])
    hi = h2.astype(BF16)
    h2_ref[...] = hi
    _pack_rows(h2, h2p_ref)
    lo = (h2 - hi.astype(F32)).astype(BF16)
    wh = wrh_ref[...]
    z = (lax.dot_general(wh, hi, _DN_T, preferred_element_type=F32)
         + lax.dot_general(wh, lo, _DN_T, preferred_element_type=F32)
         + lax.dot_general(wrl_ref[...], hi, _DN_T, preferred_element_type=F32))
    scores = jax.nn.sigmoid(z)
    work = scores + br_ref[...]
    eio = lax.broadcasted_iota(jnp.int32, work.shape, 0)
    idxs, sels, hits = [], [], []
    for _ in range(TOP_K):
        m = jnp.max(work, axis=0, keepdims=True)
        idx = jnp.min(jnp.where(work == m, eio, N_EXPERTS), axis=0, keepdims=True)
        hit = eio == idx
        sels.append(jnp.sum(jnp.where(hit, scores, 0.0), axis=0, keepdims=True))
        idxs.append(idx)
        hits.append(hit)
        work = jnp.where(hit, -jnp.inf, work)
    sel = jnp.concatenate(sels, axis=0)
    idx8 = jnp.concatenate(idxs, axis=0)
    gate8 = sel / jnp.sum(sel, axis=0, keepdims=True) * ROUTED_SCALE

    chosen = functools.reduce(jnp.logical_or, hits)
    before = (lax.broadcasted_iota(jnp.int32, (tm, tm), 0) < lax.broadcasted_iota(jnp.int32, (tm, tm), 1))
    prefix = jnp.dot(jnp.where(chosen, 1.0, 0.0).astype(BF16), jnp.where(before, 1.0, 0.0).astype(BF16),
                     preferred_element_type=F32)
    rank_dense = prefix + run_ref[:, 0:1]
    rank8 = jnp.concatenate([jnp.sum(jnp.where(h, rank_dense, 0.0), axis=0, keepdims=True) for h in hits],
                            axis=0).astype(jnp.int32)
    run = run_ref[...] + jnp.sum(jnp.where(chosen, 1.0, 0.0), axis=1, keepdims=True)
    run_ref[...] = run
    cnt_ref[...] = run
    for c in range(tm // 128):
        idx_ref[c] = idx8[:, c * 128:(c + 1) * 128]
        gate_ref[c] = gate8[:, c * 128:(c + 1) * 128]
        rank_ref[c] = rank8[:, c * 128:(c + 1) * 128]


def _out_router(x_first, x_second, yab, yc_first, yc_second, wo_bf, g2n, mod3, wr_hi, wr_lo, br, n_rows):
    tm = TM_OUT
    n_first = x_first.shape[0] // tm
    n_first_c = yc_first.shape[0] // tm
    const2 = lambda i: (0, 0)
    row = lambda i: (i, 0)
    chunk3 = pl.BlockSpec((tm // 128, TOP_K, 128), lambda i: (i, 0, 0))
    nch = n_rows // 128
    return pl.pallas_call(
        functools.partial(_out_router_kernel, n_first=n_first, n_first_c=n_first_c),
        out_shape=[jax.ShapeDtypeStruct((n_rows, D), F32),
                   jax.ShapeDtypeStruct((n_rows, D), BF16),
                   jax.ShapeDtypeStruct((n_rows * 4, 128), jnp.uint32),
                   jax.ShapeDtypeStruct((nch, TOP_K, 128), jnp.int32),
                   jax.ShapeDtypeStruct((nch, TOP_K, 128), F32),
                   jax.ShapeDtypeStruct((nch, TOP_K, 128), jnp.int32),
                   jax.ShapeDtypeStruct((N_EXPERTS, 128), F32)],
        grid=(n_rows // tm,),
        in_specs=_two_source_specs(tm, n_first)
                 + [pl.BlockSpec((tm, 512), row)] + _two_source_specs(tm, n_first_c, C_WIDTH)
                 + [pl.BlockSpec((D, D), const2),
                  _mod_spec(tm, 2),
                  pl.BlockSpec((1, D), const2), _mod_spec(tm, 3), _mod_spec(tm, 4),
                  pl.BlockSpec((N_EXPERTS, D), const2), pl.BlockSpec((N_EXPERTS, D), const2),
                  pl.BlockSpec((N_EXPERTS, 1), const2)],
        out_specs=[pl.BlockSpec((tm, D), row), pl.BlockSpec((tm, D), row), pl.BlockSpec((tm * 4, 128), row),
                   chunk3, chunk3, chunk3, pl.BlockSpec((N_EXPERTS, 128), const2)],
        scratch_shapes=[pltpu.VMEM((N_EXPERTS, 128), F32)],
        compiler_params=_cparams(("arbitrary",), 48),
        name="out_router",
    )(x_first, x_second, yab, yc_first, yc_second, wo_bf, mod3, g2n.reshape(1, D), mod3, mod3, wr_hi, wr_lo,
      br.reshape(N_EXPERTS, 1))


def _experts_kernel(te_ref, nu_ref, x_ref, wg_ref, wu_ref, wd_ref, y_ref, wgu_s, wd_s):
    i = pl.program_id(0)
    used = i < nu_ref[0]
    new_expert = jnp.logical_or(i == 0, te_ref[i] != te_ref[jnp.maximum(i - 1, 0)])

    @pl.when(jnp.logical_and(used, new_expert))
    def _():
        wgu_s[:, 0:D_EXPERT] = wg_ref[...].astype(BF16)
        wgu_s[:, D_EXPERT:] = wu_ref[...].astype(BF16)
        wd_s[...] = wd_ref[...].astype(BF16)

    @pl.when(used)
    def _():
        rc = TM_X // X_CHAINS
        wgu = wgu_s[...]
        wd = wd_s[...]
        abs_ = []
        for c in range(X_CHAINS):
            x_lo, x_hi = _unpack_rows(x_ref, rc, row0=c * rc)
            x = jnp.concatenate([x_lo.astype(BF16), x_hi.astype(BF16)], axis=1)
            abs_.append(jnp.dot(x, wgu, preferred_element_type=F32))
        for c in range(X_CHAINS):
            a = abs_[c][:, :D_EXPERT]
            hid = (a * jax.nn.sigmoid(a) * abs_[c][:, D_EXPERT:]).astype(BF16)
            _pack_rows(jnp.dot(hid, wd, preferred_element_type=F32), y_ref, row0=c * rc)


def _experts(layer, tile_expert, n_used, x_sorted, w_gate, w_up, w_down):
    tm = TM_X
    r_pad = x_sorted.shape[0] // 4
    nt = r_pad // tm
    row = lambda i, te, nu: (jnp.minimum(i, nu[0] - 1), 0)
    wmap = lambda i, te, nu: (layer, te[i], 0, 0)
    grid_spec = pltpu.PrefetchScalarGridSpec(
        num_scalar_prefetch=2,
        grid=(nt,),
        in_specs=[pl.BlockSpec((tm * 4, 128), row),
                  pl.BlockSpec((None, None, D, D_EXPERT), wmap),
                  pl.BlockSpec((None, None, D, D_EXPERT), wmap),
                  pl.BlockSpec((None, None, D_EXPERT, D), wmap)],
        out_specs=pl.BlockSpec((tm * 4, 128), row),
        scratch_shapes=[pltpu.VMEM((D, 2 * D_EXPERT), BF16), pltpu.VMEM((D_EXPERT, D), BF16)],
    )
    return pl.pallas_call(
        _experts_kernel,
        out_shape=jax.ShapeDtypeStruct((r_pad * 4, 128), jnp.uint32),
        grid_spec=grid_spec,
        compiler_params=_cparams(("arbitrary",), 48),
        name="experts",
    )(tile_expert, n_used, x_sorted, w_gate, w_up, w_down)


SC_CORES = 2
SC_SUBCORES = 16
SC_WORKERS = SC_CORES * SC_SUBCORES
SC_CHUNK = 128


def _sc_mesh():
    return plsc.VectorSubcoreMesh(core_axis_name="c", subcore_axis_name="s")


def _sc_params():
    return pltpu.CompilerParams(use_tc_tiling_on_sc=True)


def _sc_dispatch(h2p, pos3, r_pad):
    nch = pos3.shape[0]
    steps = -(-nch // SC_WORKERS)

    def body(h_hbm, pos_hbm, out_hbm, idx_v, rows_v, sem):
        wid = lax.axis_index("s") * SC_CORES + lax.axis_index("c")

        @pl.loop(0, steps)
        def _(s):
            ch = wid + s * SC_WORKERS

            @pl.when(ch < nch)
            def _():
                pltpu.sync_copy(pos_hbm.at[ch], idx_v)
                pltpu.sync_copy(h_hbm.at[pl.ds(ch * SC_CHUNK, SC_CHUNK)], rows_v)
                copies = [pltpu.async_copy(rows_v, out_hbm.at[idx_v.at[k]], sem) for k in range(TOP_K)]
                for cp in copies:
                    cp.wait()

    return pl.kernel(
        body,
        out_type=jax.ShapeDtypeStruct((r_pad, 4, 128), jnp.uint32),
        mesh=_sc_mesh(),
        scratch_types=[pltpu.VMEM((TOP_K, SC_CHUNK), jnp.int32),
                       pltpu.VMEM((SC_CHUNK, 4, 128), jnp.uint32),
                       pltpu.SemaphoreType.DMA],
        compiler_params=_sc_params(),
        name="sc_dispatch",
    )(h2p, pos3)


def _sc_collect(y_sorted, pos3):
    nch = pos3.shape[0]
    steps = -(-nch // SC_WORKERS)

    def body(y_hbm, pos_hbm, out_hbm, idx_v, rows_v, sem):
        wid = lax.axis_index("s") * SC_CORES + lax.axis_index("c")

        @pl.loop(0, steps)
        def _(s):
            ch = wid + s * SC_WORKERS

            @pl.when(ch < nch)
            def _():
                pltpu.sync_copy(pos_hbm.at[ch], idx_v)
                for k in range(TOP_K):
                    pltpu.async_copy(y_hbm.at[idx_v.at[k]], rows_v, sem).wait()
                    pltpu.sync_copy(rows_v, out_hbm.at[k, pl.ds(ch * SC_CHUNK, SC_CHUNK)])

    return pl.kernel(
        body,
        out_type=jax.ShapeDtypeStruct((TOP_K, nch * SC_CHUNK, 4, 128), jnp.uint32),
        mesh=_sc_mesh(),
        scratch_types=[pltpu.VMEM((TOP_K, SC_CHUNK), jnp.int32),
                       pltpu.VMEM((SC_CHUNK, 4, 128), jnp.uint32),
                       pltpu.SemaphoreType.DMA],
        compiler_params=_sc_params(),
        name="sc_collect",
    )(y_sorted, pos3)


def _shared_kernel(x_ref, h2_ref, yg_ref, gate_ref, wg_ref, wu_ref, wd_ref, g2_ref, o_ref):
    h = h2_ref[...]
    a = jnp.dot(h, wg_ref[...], preferred_element_type=F32)
    b = jnp.dot(h, wu_ref[...], preferred_element_type=F32)
    hid = (a * jax.nn.sigmoid(a) * b).astype(BF16)
    f = jnp.dot(hid, wd_ref[...], preferred_element_type=F32)
    gate = gate_ref[...]
    f_lo = f[:, :D // 2]
    f_hi = f[:, D // 2:]
    for k in range(TOP_K):
        y_lo, y_hi = _unpack_rows(yg_ref, TM_F, lead=k)
        f_lo = f_lo + gate[:, k:k + 1] * y_lo
        f_hi = f_hi + gate[:, k:k + 1] * y_hi
    o_ref[...] = x_ref[...] + g2_ref[...] * jnp.concatenate([f_lo, f_hi], axis=1)


def _shared_residual(xa, h2, yg, gates, wsg_bf, wsu_bf, wsd_bf, mod3, n_rows):
    tm = TM_F
    row = lambda i: (i, 0)
    const2 = lambda i: (0, 0)
    return pl.pallas_call(
        _shared_kernel,
        out_shape=jax.ShapeDtypeStruct((n_rows, D), F32),
        grid=(n_rows // tm,),
        in_specs=[pl.BlockSpec((tm, D), row), pl.BlockSpec((tm, D), row),
                  pl.BlockSpec((TOP_K, tm * 4, 128), lambda i: (0, i, 0)),
                  pl.BlockSpec((tm, TOP_K), row),
                  pl.BlockSpec((D, D_SHARED), const2), pl.BlockSpec((D, D_SHARED), const2),
                  pl.BlockSpec((D_SHARED, D), const2), _mod_spec(tm, 5)],
        out_specs=pl.BlockSpec((tm, D), row),
        compiler_params=_cparams(("parallel",), 48),
        name="shared_residual",
    )(xa, h2, yg, gates, wsg_bf, wsu_bf, wsd_bf, mod3)


def _positions_kernel(offs_ref, idx_ref, rank_ref, pos_ref):
    idx = idx_ref[...]
    base = jnp.zeros_like(idx)
    for e in range(N_EXPERTS):
        base = jnp.where(idx == e, offs_ref[e], base)
    pos_ref[...] = rank_ref[...] + base


def _positions(offs, idx3, rank3):
    nch = idx3.shape[0]
    cb = 16
    spec = pl.BlockSpec((cb, TOP_K, 128), lambda i, offs: (i, 0, 0))
    return pl.pallas_call(
        _positions_kernel,
        out_shape=jax.ShapeDtypeStruct((nch, TOP_K, 128), jnp.int32),
        grid_spec=pltpu.PrefetchScalarGridSpec(num_scalar_prefetch=1, grid=(nch // cb,),
                                               in_specs=[spec, spec], out_specs=spec),
        compiler_params=_cparams(("parallel",)),
        name="positions",
    )(offs, idx3, rank3)


def _route_positions(idx3, rank3, counts, n_rows):
    tm = TM_X
    counts = counts.astype(jnp.int32)
    padded = ((counts + tm - 1) // tm) * tm
    ends = jnp.cumsum(padded)
    offs = ends - padded
    pos3 = _positions(offs.astype(jnp.int32), idx3, rank3)
    r_pad = n_rows * TOP_K + N_EXPERTS * tm
    nt = r_pad // tm
    tile_ids = jnp.arange(nt, dtype=jnp.int32)
    tile_expert = jnp.sum((ends // tm)[None, :] <= tile_ids[:, None], axis=1)
    tile_expert = jnp.minimum(tile_expert, N_EXPERTS - 1).astype(jnp.int32)
    n_used = (ends[-1] // tm).astype(jnp.int32).reshape(1)
    return pos3, tile_expert, n_used, r_pad


def _q_perm():
    perm = np.empty(512, np.int32)
    for h in range(HEADS):
        for half in range(2):
            for d in range(QK_DIM):
                perm[h * 128 + half * QK_DIM + d] = half * HEADS * QK_DIM + h * QK_DIM + d
    return perm


def _rope_tables():
    t = jnp.arange(L)
    row = (t // GRID_W).astype(F32)
    col = (t % GRID_W).astype(F32)
    n_freq = QK_DIM // 4
    inv = ROPE_BASE ** (-jnp.arange(n_freq, dtype=F32) / n_freq)
    ar = row[:, None] * inv
    ac = col[:, None] * inv
    cos64 = jnp.concatenate([jnp.cos(ar), jnp.cos(ar), jnp.cos(ac), jnp.cos(ac)], axis=1)
    sin64 = jnp.concatenate([-jnp.sin(ar), jnp.sin(ar), -jnp.sin(ac), jnp.sin(ac)], axis=1)
    cos_t = jnp.concatenate([jnp.tile(cos64, (1, 2)), jnp.ones((TM_MIX, 128), F32)], axis=0)
    sin_t = jnp.concatenate([jnp.tile(sin64, (1, 2)), jnp.zeros((TM_MIX, 128), F32)], axis=0)
    return cos_t, sin_t


def _split_bf16(w):
    hi = w.astype(BF16)
    return hi, (w - hi.astype(F32)).astype(BF16)


def kernel(x, c, ctx, c_ctx, w_ada, b_ada, g_norm1, g_norm2, w_in, w_out, g_v, w_s, b_s, w_conv, g_q, g_k,
           lam_q1, lam_k1, lam_q2, lam_k2, g_sub, w_router, b_router, w_gate, w_up, w_down,
           ws_gate, ws_up, ws_down):
    src = (x.reshape(NL, D), ctx.reshape(NC, D))
    cc = jnp.concatenate([c, c_ctx[None, :], jnp.zeros((MOD_ROWS - B - 1, D), F32)], axis=0)
    mod = _ada(cc, w_ada, b_ada)
    cos_t, sin_t = _rope_tables()
    perm = _q_perm()
    col_perm = np.concatenate([np.arange(OFF_Q), OFF_Q + perm, OFF_K + perm, np.arange(OFF_V, D_IN)])
    bd = jnp.asarray(np.kron(np.eye(8, dtype=np.float32), np.full((64, 64), 1.0 / 64, np.float32)), BF16)

    for l in range(DEPTH):
        last = l == DEPTH - 1
        lam_init = 0.8 - 0.6 * math.exp(-0.3 * l)
        lam = (jnp.exp(jnp.sum(lam_q1[l] * lam_k1[l])) - jnp.exp(jnp.sum(lam_q2[l] * lam_k2[l]))
               + lam_init).reshape(1).astype(F32)
        mod3 = mod[l].reshape(MOD_ROWS, 1, 6 * D)
        w_in_bf = w_in[l][:, col_perm].astype(BF16)
        p = _in_proj(src[0], src[1], g_norm1[l], mod3, w_in_bf)
        bias_t = jnp.repeat(b_s[l].T, A_GD, axis=1)
        wconv = jnp.concatenate([w_conv[l], jnp.zeros((5, B_WIDTH), F32)], axis=0)
        yab, qa, qb, kk = _mixers(p, cos_t, sin_t, g_v[l].reshape(1, A_WIDTH), w_s[l].astype(BF16), bias_t, wconv,
                                  jnp.tile(g_q[l], 8).reshape(1, 512), jnp.tile(g_k[l], 8).reshape(1, 512), bd)
        gsub = g_sub[l].reshape(1, V_DIM)
        coef = 1.0 - lam_init
        n_rows = NL if last else NR
        yc = _attention(lam, qa, qb, kk, p, gsub, coef, ctx_queries=False)
        yc_ctx = yc if last else _attention(lam, qa, qb, kk, p, gsub, coef, ctx_queries=True)
        wr_hi, wr_lo = _split_bf16(w_router[l].T)
        xa, h2, h2p, idx3, gate3, rank3, counts = _out_router(
            src[0], src[1], yab, yc, yc_ctx, w_out[l].astype(BF16), g_norm2[l], mod3, wr_hi, wr_lo, b_router[l],
            n_rows)
        pos3, tile_expert, n_used, r_pad = _route_positions(idx3, rank3, counts[:, 0], n_rows)
        x_sorted = _sc_dispatch(h2p.reshape(n_rows, 4, 128), pos3, r_pad)
        y_sorted = _experts(l, tile_expert, n_used, x_sorted.reshape(r_pad * 4, 128), w_gate, w_up, w_down)
        yg = _sc_collect(y_sorted.reshape(r_pad, 4, 128), pos3)
        gates = gate3.transpose(0, 2, 1).reshape(n_rows, TOP_K)
        xa = _shared_residual(xa, h2, yg.reshape(TOP_K, n_rows * 4, 128), gates, ws_gate[l].astype(BF16),
                              ws_up[l].astype(BF16), ws_down[l].astype(BF16), mod3, n_rows)
        src = (xa, xa)
    return xa.reshape(B, L, D)
```

```python
import functools
import math

import numpy as np
import jax
import jax.numpy as jnp
from jax import lax
from jax.experimental import pallas as pl
from jax.experimental.pallas import tpu as pltpu
from jax.experimental.pallas import tpu_sc as plsc

F32 = jnp.float32
BF16 = jnp.bfloat16

D = 1024
B = 8
L = 2048
DEPTH = 2
GRID_W = 64
CTX = 256
A_WIDTH = 256
A_GROUPS = 4
A_GD = 64
CHUNK = 128
B_WIDTH = 256
C_WIDTH = 512
HEADS = 4
V_DIM = 128
QK_DIM = 64
ROPE_BASE = 10000.0
OFF_BB = 512
OFF_BC = 768
OFF_BX = 1024
OFF_Q = 1280
OFF_K = 1792
OFF_V = 2304
D_IN = 2816
N_EXPERTS = 64
TOP_K = 8
D_EXPERT = 256
D_SHARED = 256
ROUTED_SCALE = 2.5
EPS = 1e-6

NL = B * L
NC = B * CTX
NR = NL + NC
MOD_ROWS = 16
LOG2E = 1.4426950408889634

TM_IN = 512
TM_MIX = 256
TQ = 1024
ATT_CHAIN = 256
TM_OUT = 512
TM_X = 512
X_CHAINS = 2
MOE_GROUPS = 2
TM_F = 256

_DN_T = (((1,), (1,)), ((), ()))


def _cparams(sem, vmem_mb=None):
    kw = dict(dimension_semantics=sem)
    if vmem_mb is not None:
        kw["vmem_limit_bytes"] = vmem_mb * 1024 * 1024
    return pltpu.CompilerParams(**kw)


def _mod_row(i, tm):
    return jnp.where(i < NL // tm, i // (L // tm), B)


def _mod_spec(tm, chunk, t0=0):
    return pl.BlockSpec((None, 1, D), lambda i: (_mod_row(i + t0, tm), 0, chunk))


def _ada_kernel(c_ref, w_ref, b_ref, o_ref):
    c = c_ref[...]
    cs = c * jax.nn.sigmoid(c)
    o_ref[...] = jnp.dot(cs, w_ref[...], preferred_element_type=F32,
                         precision=lax.Precision.HIGHEST) + b_ref[...]


def _ada(cc, w_ada, b_ada):
    nb = 6
    return pl.pallas_call(
        _ada_kernel,
        out_shape=jax.ShapeDtypeStruct((DEPTH, MOD_ROWS, 6 * D), F32),
        grid=(DEPTH, nb),
        in_specs=[pl.BlockSpec((MOD_ROWS, D), lambda l, j: (0, 0)),
                  pl.BlockSpec((None, D, D), lambda l, j: (l, 0, j)),
                  pl.BlockSpec((None, 1, D), lambda l, j: (l, 0, j))],
        out_specs=pl.BlockSpec((None, MOD_ROWS, D), lambda l, j: (l, 0, j)),
        compiler_params=_cparams(("arbitrary", "arbitrary"), 40),
        name="ada_mod",
    )(cc, w_ada, b_ada.reshape(DEPTH, 1, 6 * D))


def _rms_mod(x, g, sc, sh):
    ms = jnp.mean(x * x, axis=-1, keepdims=True)
    return x * lax.rsqrt(ms + EPS) * g * (1.0 + sc) + sh


def _two_source_specs(tm, n_first, width=D, t0=0):
    return [pl.BlockSpec((tm, width), lambda i: (jnp.minimum(i + t0, n_first - 1), 0)),
            pl.BlockSpec((tm, width), lambda i: (jnp.maximum(i + t0 - n_first, 0), 0))]


def _two_source_rows(a_ref, b_ref, n_first, t0=0):
    return jnp.where(pl.program_id(0) + t0 < n_first, a_ref[...], b_ref[...])


def _in_proj_kernel(xa_ref, xb_ref, g_ref, sh_ref, sc_ref, w_ref, o_ref, *, n_first):
    h = _rms_mod(_two_source_rows(xa_ref, xb_ref, n_first), g_ref[...], sc_ref[...], sh_ref[...])
    o_ref[...] = jnp.dot(h.astype(BF16), w_ref[...], preferred_element_type=F32).astype(o_ref.dtype)


def _in_proj(x_first, x_second, g, mod3, w_bf):
    tm = TM_IN
    n_first = x_first.shape[0] // tm
    return pl.pallas_call(
        functools.partial(_in_proj_kernel, n_first=n_first),
        out_shape=jax.ShapeDtypeStruct((NR, D_IN), BF16),
        grid=(NR // tm,),
        in_specs=_two_source_specs(tm, n_first)
                 + [pl.BlockSpec((1, D), lambda i: (0, 0)),
                    _mod_spec(tm, 0), _mod_spec(tm, 1),
                    pl.BlockSpec((D, D_IN), lambda i: (0, 0))],
        out_specs=pl.BlockSpec((tm, D_IN), lambda i: (i, 0)),
        compiler_params=_cparams(("parallel",), 48),
        name="in_proj",
    )(x_first, x_second, g.reshape(1, D), mod3, mod3, w_bf)


def _group_rms(t, g, bd):
    sq = t * t
    hi = sq.astype(BF16)
    lo = (sq - hi.astype(F32)).astype(BF16)
    ms = (jnp.dot(hi, bd, preferred_element_type=F32) + jnp.dot(lo, bd, preferred_element_type=F32))
    return t * lax.rsqrt(ms + EPS) * g


def _rope(t, cos, sin):
    w = t.shape[1]
    lane = lax.broadcasted_iota(jnp.int32, t.shape, 1)
    first = (lane % 32) < 16
    partner = jnp.where(first, pltpu.roll(t, w - 16, 1), pltpu.roll(t, 16, 1))
    cos4 = jnp.concatenate([cos] * (w // 128), axis=1)
    sin4 = jnp.concatenate([sin] * (w // 128), axis=1)
    return t * cos4 + partner * sin4


def _mixers_kernel(p_ref, pc_ref, px_ref, nc_ref, nx_ref, cos_ref, sin_ref, gv_ref, ws_ref, bias_ref,
                   wconv_ref, gq_ref, gk_ref, bd_ref, yab_ref, qa_ref, qb_ref, kk_ref):
    tm = TM_MIX
    i = pl.program_id(0)
    tiles_per_seq = L // tm
    is_lat = i < NL // tm
    is_start = jnp.logical_or(jnp.logical_not(is_lat), i % tiles_per_seq == 0)
    is_end = jnp.logical_or(jnp.logical_not(is_lat), i % tiles_per_seq == tiles_per_seq - 1)

    uv = p_ref[:, 0:2 * A_WIDTH].astype(F32)
    uv = 0.5 * uv * (1.0 + lax.erf(uv * (2.0 ** -0.5)))
    u = uv[:, :A_WIDTH]
    v = uv[:, A_WIDTH:]
    ms = jnp.mean(v * v, axis=-1, keepdims=True)
    vb = (v * lax.rsqrt(ms + EPS) * gv_ref[...]).astype(BF16)
    lane = lax.broadcasted_iota(jnp.int32, (CHUNK, 128), 1)
    mixes = []
    for c in range(tm // CHUNK):
        vc = vb[c * CHUNK:(c + 1) * CHUNK]
        halves = []
        for j in range(2):
            vj = vc[:, j * 128:(j + 1) * 128]
            m0 = jnp.dot(ws_ref[2 * j], vj, preferred_element_type=F32)
            m1 = jnp.dot(ws_ref[2 * j + 1], vj, preferred_element_type=F32)
            halves.append(jnp.where(lane < A_GD, m0, m1))
        mixes.append(jnp.concatenate(halves, axis=1) + bias_ref[...])
    ya = u * jnp.concatenate(mixes, axis=0)

    bg = p_ref[:, OFF_BB:OFF_BC].astype(F32)
    z = p_ref[:, OFF_BC:OFF_BX].astype(F32) * p_ref[:, OFF_BX:OFF_Q].astype(F32)
    zp = (pc_ref[...].astype(F32) * px_ref[...].astype(F32))[15:16]
    zn = (nc_ref[...].astype(F32) * nx_ref[...].astype(F32))[0:1]
    zp = jnp.where(is_start, 0.0, zp)
    zn = jnp.where(is_end, 0.0, zn)
    row = lax.broadcasted_iota(jnp.int32, z.shape, 0)
    z_prev = jnp.where(row == 0, zp, pltpu.roll(z, 1, 0))
    z_next = jnp.where(row == tm - 1, zn, pltpu.roll(z, tm - 1, 0))
    yb = bg * (z_prev * wconv_ref[0:1] + z * wconv_ref[1:2] + z_next * wconv_ref[2:3])
    yab_ref[...] = jnp.concatenate([ya, yb], axis=1).astype(BF16)

    cos = cos_ref[...]
    sin = sin_ref[...]
    bd = bd_ref[...]
    q = _rope(_group_rms(p_ref[:, OFF_Q:OFF_K].astype(F32), gq_ref[...], bd), cos, sin)
    q = q * (QK_DIM ** -0.5 * LOG2E)
    lane5 = lax.broadcasted_iota(jnp.int32, q.shape, 1) % 128
    qa_ref[...] = jnp.where(lane5 < QK_DIM, q, 0.0).astype(BF16)
    qb_ref[...] = jnp.where(lane5 >= QK_DIM, q, 0.0).astype(BF16)
    k = _rope(_group_rms(p_ref[:, OFF_K:OFF_V].astype(F32), gk_ref[...], bd), cos, sin)
    kk_ref[...] = k.astype(BF16)


def _mixers(p, cos_t, sin_t, gv, ws_bf, bias_t, wconv, gq, gk, bd):
    tm = TM_MIX
    nt = NR // tm
    hb = tm // 16
    last_hb = NR // 16 - 1
    pos_blocks = L // tm

    def tab_map(i):
        return (jnp.where(i < NL // tm, i % pos_blocks, pos_blocks), 0)

    prev_map_c = lambda i: (jnp.maximum(i * hb - 1, 0), OFF_BC // 256)
    prev_map_x = lambda i: (jnp.maximum(i * hb - 1, 0), OFF_BX // 256)
    next_map_c = lambda i: (jnp.minimum((i + 1) * hb, last_hb), OFF_BC // 256)
    next_map_x = lambda i: (jnp.minimum((i + 1) * hb, last_hb), OFF_BX // 256)
    const2 = lambda i: (0, 0)
    outs = pl.pallas_call(
        _mixers_kernel,
        out_shape=[jax.ShapeDtypeStruct((NR, 2 * A_WIDTH), BF16),
                   jax.ShapeDtypeStruct((NR, 512), BF16),
                   jax.ShapeDtypeStruct((NR, 512), BF16),
                   jax.ShapeDtypeStruct((NR, 512), BF16)],
        grid=(nt,),
        in_specs=[pl.BlockSpec((tm, D_IN), lambda i: (i, 0)),
                  pl.BlockSpec((16, 256), prev_map_c), pl.BlockSpec((16, 256), prev_map_x),
                  pl.BlockSpec((16, 256), next_map_c), pl.BlockSpec((16, 256), next_map_x),
                  pl.BlockSpec((tm, 128), tab_map), pl.BlockSpec((tm, 128), tab_map),
                  pl.BlockSpec((1, A_WIDTH), const2),
                  pl.BlockSpec((A_GROUPS, CHUNK, CHUNK), lambda i: (0, 0, 0)),
                  pl.BlockSpec((CHUNK, A_WIDTH), const2),
                  pl.BlockSpec((8, B_WIDTH), const2),
                  pl.BlockSpec((1, 512), const2), pl.BlockSpec((1, 512), const2),
                  pl.BlockSpec((512, 512), const2)],
        out_specs=[pl.BlockSpec((tm, 512), lambda i: (i, 0))] * 4,
        compiler_params=_cparams(("parallel",), 48),
        name="mixers",
    )(p, p, p, p, p, cos_t, sin_t, gv, ws_bf, bias_t, wconv, gq, gk, bd)
    return outs


def _attn_kernel(lam_ref, qa_ref, qb_ref, *rest, n_seg, coef, tq):
    kv_refs = rest[:2 * n_seg]
    gsub_ref, o_ref, k_scr, vt_scr = rest[2 * n_seg:]

    @pl.when(pl.program_id(2) == 0)
    def _():
        off = 0
        for s in range(n_seg):
            n = kv_refs[s].shape[0]
            k_scr[off:off + n, :] = kv_refs[s][...]
            vt_scr[0:V_DIM, off:off + n] = kv_refs[n_seg + s][...].astype(F32).T.astype(BF16)
            off += n
        ones_row = lax.broadcasted_iota(jnp.int32, (16, off), 0) == 0
        vt_scr[V_DIM:, :] = jnp.where(ones_row, 1.0, 0.0).astype(BF16)

    k = k_scr[...]
    vt = vt_scr[...]
    lam = lam_ref[0]
    qc = ATT_CHAIN
    sts = []
    for c in range(tq // qc):
        rows = slice(c * qc, (c + 1) * qc)
        qs = jnp.concatenate([qa_ref[rows, :], qb_ref[rows, :]], axis=0)
        sts.append(lax.dot_general(k, qs, _DN_T, preferred_element_type=F32))
    for c in range(tq // qc):
        st = sts[c]
        pt = jnp.exp2(st - jnp.max(st, axis=0, keepdims=True)).astype(BF16)
        ot = jnp.dot(vt, pt, preferred_element_type=F32)
        inv = 1.0 / ot[V_DIM:V_DIM + 1, :]
        dt = ot[0:V_DIM, :qc] * inv[:, :qc] - ot[0:V_DIM, qc:] * (lam * inv[:, qc:])
        o = dt.T
        ms = jnp.mean(o * o, axis=-1, keepdims=True)
        o_ref[c * qc:(c + 1) * qc, :] = (o * lax.rsqrt(ms + EPS) * gsub_ref[...] * coef).astype(o_ref.dtype)


def _attention(lam, qa, qb, kk, p, gsub, coef, *, ctx_queries):
    vcol = OFF_V // 128
    if ctx_queries:
        tq = CTX
        nq, lk, n_seg = 1, CTX, 1
        q_map = lambda b, h, qi: (NL // tq + b, h)
        kv_specs = [pl.BlockSpec((CTX, 128), lambda b, h, qi: (NL // CTX + b, h)),
                    pl.BlockSpec((CTX, 128), lambda b, h, qi: (NL // CTX + b, vcol + h))]
        kv_args = [kk, p]
        rows = NC
        o_map = lambda b, h, qi: (b, h)
    else:
        tq = TQ
        nq, lk, n_seg = L // tq, CTX + L, 2
        q_map = lambda b, h, qi: (b * (L // tq) + qi, h)
        kv_specs = [pl.BlockSpec((CTX, 128), lambda b, h, qi: (NL // CTX + b, h)),
                    pl.BlockSpec((L, 128), lambda b, h, qi: (b, h)),
                    pl.BlockSpec((CTX, 128), lambda b, h, qi: (NL // CTX + b, vcol + h)),
                    pl.BlockSpec((L, 128), lambda b, h, qi: (b, vcol + h))]
        kv_args = [kk, kk, p, p]
        rows = NL
        o_map = lambda b, h, qi: (b * (L // tq) + qi, h)
    return pl.pallas_call(
        functools.partial(_attn_kernel, n_seg=n_seg, coef=coef, tq=tq),
        out_shape=jax.ShapeDtypeStruct((rows, C_WIDTH), BF16),
        grid=(B, HEADS, nq),
        in_specs=[pl.BlockSpec(memory_space=pltpu.SMEM),
                  pl.BlockSpec((tq, 128), q_map), pl.BlockSpec((tq, 128), q_map)]
                 + kv_specs + [pl.BlockSpec((1, V_DIM), lambda b, h, qi: (0, 0))],
        out_specs=pl.BlockSpec((tq, 128), o_map),
        scratch_shapes=[pltpu.VMEM((lk, 128), BF16), pltpu.VMEM((V_DIM + 16, lk), BF16)],
        compiler_params=_cparams(("parallel", "parallel", "arbitrary"), 56),
        name="attn_ctx" if ctx_queries else "attn_lat",
    )(lam, qa, qb, *kv_args, gsub)


def _pack_rows(t, out_ref, row0=0):
    half = D // 2
    w = pltpu.pack_elementwise([t[:, :half], t[:, half:]], packed_dtype=BF16)
    w = lax.bitcast_convert_type(w, jnp.uint32)
    rows = t.shape[0]
    for j in range(4):
        out_ref[pl.ds(4 * row0 + j, rows, stride=4), :] = w[:, j * 128:(j + 1) * 128]


def _unpack_rows(ref, rows, lead=None, row0=0):
    los, his = [], []
    for j in range(4):
        sl = pl.ds(4 * row0 + j, rows, stride=4)
        w = ref[sl, :] if lead is None else ref[lead, sl, :]
        los.append(pltpu.unpack_elementwise(w, index=0, packed_dtype=BF16, unpacked_dtype=F32))
        his.append(pltpu.unpack_elementwise(w, index=1, packed_dtype=BF16, unpacked_dtype=F32))
    return jnp.concatenate(los, axis=1), jnp.concatenate(his, axis=1)


def _out_router_kernel(xa_ref, xb_ref, yab_ref, yca_ref, ycb_ref, wo_ref, g1_ref, g2n_ref, sh2_ref, sc2_ref, wrh_ref,
                       wrl_ref, br_ref, xo_ref, h2_ref, h2p_ref, idx_ref, gate_ref, rank_ref, cnt_ref, run_ref, *,
                       n_first, n_first_c, t0):
    tm = TM_OUT

    @pl.when(pl.program_id(0) == 0)
    def _():
        run_ref[...] = jnp.zeros_like(run_ref)

    yc = _two_source_rows(yca_ref, ycb_ref, n_first_c, t0)
    y = jnp.dot(jnp.concatenate([yab_ref[...], yc], axis=1), wo_ref[...], preferred_element_type=F32)
    x = _two_source_rows(xa_ref, xb_ref, n_first, t0) + g1_ref[...] * y
    xo_ref[...] = x
    h2 = _rms_mod(x, g2n_ref[...], sc2_ref[...], sh2_ref[...])
    hi = h2.astype(BF16)
    h2_ref[...] = hi
    _pack_rows(h2, h2p_ref)
    lo = (h2 - hi.astype(F32)).astype(BF16)
    wh = wrh_ref[...]
    z = (lax.dot_general(wh, hi, _DN_T, preferred_element_type=F32)
         + lax.dot_general(wh, lo, _DN_T, preferred_element_type=F32)
         + lax.dot_general(wrl_ref[...], hi, _DN_T, preferred_element_type=F32))
    scores = jax.nn.sigmoid(z)
    work = scores + br_ref[...]
    eio = lax.broadcasted_iota(jnp.int32, work.shape, 0)
    idxs, sels, hits = [], [], []
    for _ in range(TOP_K):
        m = jnp.max(work, axis=0, keepdims=True)
        idx = jnp.min(jnp.where(work == m, eio, N_EXPERTS), axis=0, keepdims=True)
        hit = eio == idx
        sels.append(jnp.sum(jnp.where(hit, scores, 0.0), axis=0, keepdims=True))
        idxs.append(idx)
        hits.append(hit)
        work = jnp.where(hit, -jnp.inf, work)
    sel = jnp.concatenate(sels, axis=0)
    idx8 = jnp.concatenate(idxs, axis=0)
    gate8 = sel / jnp.sum(sel, axis=0, keepdims=True) * ROUTED_SCALE

    chosen = functools.reduce(jnp.logical_or, hits)
    before = (lax.broadcasted_iota(jnp.int32, (tm, tm), 0) < lax.broadcasted_iota(jnp.int32, (tm, tm), 1))
    prefix = jnp.dot(jnp.where(chosen, 1.0, 0.0).astype(BF16), jnp.where(before, 1.0, 0.0).astype(BF16),
                     preferred_element_type=F32)
    rank_dense = prefix + run_ref[:, 0:1]
    rank8 = jnp.concatenate([jnp.sum(jnp.where(h, rank_dense, 0.0), axis=0, keepdims=True) for h in hits],
                            axis=0).astype(jnp.int32)
    run = run_ref[...] + jnp.sum(jnp.where(chosen, 1.0, 0.0), axis=1, keepdims=True)
    run_ref[...] = run
    cnt_ref[...] = run
    for c in range(tm // 128):
        idx_ref[c] = idx8[:, c * 128:(c + 1) * 128]
        gate_ref[c] = gate8[:, c * 128:(c + 1) * 128]
        rank_ref[c] = rank8[:, c * 128:(c + 1) * 128]


def _out_router(x_first, x_second, yab, yc_first, yc_second, wo_bf, g2n, mod3, wr_hi, wr_lo, br, row0, n_rows):
    tm = TM_OUT
    t0 = row0 // tm
    n_first = x_first.shape[0] // tm
    n_first_c = yc_first.shape[0] // tm
    const2 = lambda i: (0, 0)
    row = lambda i: (i, 0)
    chunk3 = pl.BlockSpec((tm // 128, TOP_K, 128), lambda i: (i, 0, 0))
    nch = n_rows // 128
    return pl.pallas_call(
        functools.partial(_out_router_kernel, n_first=n_first, n_first_c=n_first_c, t0=t0),
        out_shape=[jax.ShapeDtypeStruct((n_rows, D), F32),
                   jax.ShapeDtypeStruct((n_rows, D), BF16),
                   jax.ShapeDtypeStruct((n_rows * 4, 128), jnp.uint32),
                   jax.ShapeDtypeStruct((nch, TOP_K, 128), jnp.int32),
                   jax.ShapeDtypeStruct((nch, TOP_K, 128), F32),
                   jax.ShapeDtypeStruct((nch, TOP_K, 128), jnp.int32),
                   jax.ShapeDtypeStruct((N_EXPERTS, 128), F32)],
        grid=(n_rows // tm,),
        in_specs=_two_source_specs(tm, n_first, D, t0)
                 + [pl.BlockSpec((tm, 512), lambda i: (i + t0, 0))]
                 + _two_source_specs(tm, n_first_c, C_WIDTH, t0)
                 + [pl.BlockSpec((D, D), const2),
                  _mod_spec(tm, 2, t0),
                  pl.BlockSpec((1, D), const2), _mod_spec(tm, 3, t0), _mod_spec(tm, 4, t0),
                  pl.BlockSpec((N_EXPERTS, D), const2), pl.BlockSpec((N_EXPERTS, D), const2),
                  pl.BlockSpec((N_EXPERTS, 1), const2)],
        out_specs=[pl.BlockSpec((tm, D), row), pl.BlockSpec((tm, D), row), pl.BlockSpec((tm * 4, 128), row),
                   chunk3, chunk3, chunk3, pl.BlockSpec((N_EXPERTS, 128), const2)],
        scratch_shapes=[pltpu.VMEM((N_EXPERTS, 128), F32)],
        compiler_params=_cparams(("arbitrary",), 48),
        name="out_router",
    )(x_first, x_second, yab, yc_first, yc_second, wo_bf, mod3, g2n.reshape(1, D), mod3, mod3, wr_hi, wr_lo,
      br.reshape(N_EXPERTS, 1))


def _experts_kernel(te_ref, nu_ref, x_ref, wg_ref, wu_ref, wd_ref, y_ref, wgu_s, wd_s):
    i = pl.program_id(0)
    used = i < nu_ref[0]
    new_expert = jnp.logical_or(i == 0, te_ref[i] != te_ref[jnp.maximum(i - 1, 0)])

    @pl.when(jnp.logical_and(used, new_expert))
    def _():
        wgu_s[:, 0:D_EXPERT] = wg_ref[...].astype(BF16)
        wgu_s[:, D_EXPERT:] = wu_ref[...].astype(BF16)
        wd_s[...] = wd_ref[...].astype(BF16)

    @pl.when(used)
    def _():
        rc = TM_X // X_CHAINS
        wgu = wgu_s[...]
        wd = wd_s[...]
        abs_ = []
        for c in range(X_CHAINS):
            x_lo, x_hi = _unpack_rows(x_ref, rc, row0=c * rc)
            x = jnp.concatenate([x_lo.astype(BF16), x_hi.astype(BF16)], axis=1)
            abs_.append(jnp.dot(x, wgu, preferred_element_type=F32))
        for c in range(X_CHAINS):
            a = abs_[c][:, :D_EXPERT]
            hid = (a * jax.nn.sigmoid(a) * abs_[c][:, D_EXPERT:]).astype(BF16)
            _pack_rows(jnp.dot(hid, wd, preferred_element_type=F32), y_ref, row0=c * rc)


def _experts(layer, tile_expert, n_used, x_sorted, w_gate, w_up, w_down):
    tm = TM_X
    r_pad = x_sorted.shape[0] // 4
    nt = r_pad // tm
    row = lambda i, te, nu: (jnp.minimum(i, nu[0] - 1), 0)
    wmap = lambda i, te, nu: (layer, te[i], 0, 0)
    grid_spec = pltpu.PrefetchScalarGridSpec(
        num_scalar_prefetch=2,
        grid=(nt,),
        in_specs=[pl.BlockSpec((tm * 4, 128), row),
                  pl.BlockSpec((None, None, D, D_EXPERT), wmap),
                  pl.BlockSpec((None, None, D, D_EXPERT), wmap),
                  pl.BlockSpec((None, None, D_EXPERT, D), wmap)],
        out_specs=pl.BlockSpec((tm * 4, 128), row),
        scratch_shapes=[pltpu.VMEM((D, 2 * D_EXPERT), BF16), pltpu.VMEM((D_EXPERT, D), BF16)],
    )
    return pl.pallas_call(
        _experts_kernel,
        out_shape=jax.ShapeDtypeStruct((r_pad * 4, 128), jnp.uint32),
        grid_spec=grid_spec,
        compiler_params=_cparams(("arbitrary",), 48),
        name="experts",
    )(tile_expert, n_used, x_sorted, w_gate, w_up, w_down)


SC_CORES = 2
SC_SUBCORES = 16
SC_WORKERS = SC_CORES * SC_SUBCORES
SC_CHUNK = 128


def _sc_mesh():
    return plsc.VectorSubcoreMesh(core_axis_name="c", subcore_axis_name="s")


def _sc_params():
    return pltpu.CompilerParams(use_tc_tiling_on_sc=True)


def _sc_dispatch(h2p, pos3, r_pad):
    nch = pos3.shape[0]
    steps = -(-nch // SC_WORKERS)

    def body(h_hbm, pos_hbm, out_hbm, idx_v, rows_v, sem):
        wid = lax.axis_index("s") * SC_CORES + lax.axis_index("c")

        @pl.loop(0, steps)
        def _(s):
            ch = wid + s * SC_WORKERS

            @pl.when(ch < nch)
            def _():
                pltpu.sync_copy(pos_hbm.at[ch], idx_v)
                pltpu.sync_copy(h_hbm.at[pl.ds(ch * SC_CHUNK, SC_CHUNK)], rows_v)
                copies = [pltpu.async_copy(rows_v, out_hbm.at[idx_v.at[k]], sem) for k in range(TOP_K)]
                for cp in copies:
                    cp.wait()

    return pl.kernel(
        body,
        out_type=jax.ShapeDtypeStruct((r_pad, 4, 128), jnp.uint32),
        mesh=_sc_mesh(),
        scratch_types=[pltpu.VMEM((TOP_K, SC_CHUNK), jnp.int32),
                       pltpu.VMEM((SC_CHUNK, 4, 128), jnp.uint32),
                       pltpu.SemaphoreType.DMA],
        compiler_params=_sc_params(),
        name="sc_dispatch",
    )(h2p, pos3)


def _sc_collect(y_sorted, pos3):
    nch = pos3.shape[0]
    steps = -(-nch // SC_WORKERS)

    def body(y_hbm, pos_hbm, out_hbm, idx_v, rows_v, sem):
        wid = lax.axis_index("s") * SC_CORES + lax.axis_index("c")

        @pl.loop(0, steps)
        def _(s):
            ch = wid + s * SC_WORKERS

            @pl.when(ch < nch)
            def _():
                pltpu.sync_copy(pos_hbm.at[ch], idx_v)
                for k in range(TOP_K):
                    pltpu.async_copy(y_hbm.at[idx_v.at[k]], rows_v, sem).wait()
                    pltpu.sync_copy(rows_v, out_hbm.at[k, pl.ds(ch * SC_CHUNK, SC_CHUNK)])

    return pl.kernel(
        body,
        out_type=jax.ShapeDtypeStruct((TOP_K, nch * SC_CHUNK, 4, 128), jnp.uint32),
        mesh=_sc_mesh(),
        scratch_types=[pltpu.VMEM((TOP_K, SC_CHUNK), jnp.int32),
                       pltpu.VMEM((SC_CHUNK, 4, 128), jnp.uint32),
                       pltpu.SemaphoreType.DMA],
        compiler_params=_sc_params(),
        name="sc_collect",
    )(y_sorted, pos3)


def _shared_kernel(x_ref, h2_ref, yg_ref, gate_ref, wg_ref, wu_ref, wd_ref, g2_ref, o_ref):
    h = h2_ref[...]
    a = jnp.dot(h, wg_ref[...], preferred_element_type=F32)
    b = jnp.dot(h, wu_ref[...], preferred_element_type=F32)
    hid = (a * jax.nn.sigmoid(a) * b).astype(BF16)
    f = jnp.dot(hid, wd_ref[...], preferred_element_type=F32)
    gate = gate_ref[...]
    f_lo = f[:, :D // 2]
    f_hi = f[:, D // 2:]
    for k in range(TOP_K):
        y_lo, y_hi = _unpack_rows(yg_ref, TM_F, lead=k)
        f_lo = f_lo + gate[:, k:k + 1] * y_lo
        f_hi = f_hi + gate[:, k:k + 1] * y_hi
    o_ref[...] = x_ref[...] + g2_ref[...] * jnp.concatenate([f_lo, f_hi], axis=1)


def _shared_residual(xa, h2, yg, gates, wsg_bf, wsu_bf, wsd_bf, mod3, row0, n_rows):
    tm = TM_F
    t0 = row0 // tm
    row = lambda i: (i, 0)
    const2 = lambda i: (0, 0)
    return pl.pallas_call(
        _shared_kernel,
        out_shape=jax.ShapeDtypeStruct((n_rows, D), F32),
        grid=(n_rows // tm,),
        in_specs=[pl.BlockSpec((tm, D), row), pl.BlockSpec((tm, D), row),
                  pl.BlockSpec((TOP_K, tm * 4, 128), lambda i: (0, i, 0)),
                  pl.BlockSpec((tm, TOP_K), row),
                  pl.BlockSpec((D, D_SHARED), const2), pl.BlockSpec((D, D_SHARED), const2),
                  pl.BlockSpec((D_SHARED, D), const2), _mod_spec(tm, 5, t0)],
        out_specs=pl.BlockSpec((tm, D), row),
        compiler_params=_cparams(("parallel",), 48),
        name="shared_residual",
    )(xa, h2, yg, gates, wsg_bf, wsu_bf, wsd_bf, mod3)


def _positions_kernel(offs_ref, idx_ref, rank_ref, pos_ref):
    idx = idx_ref[...]
    base = jnp.zeros_like(idx)
    for e in range(N_EXPERTS):
        base = jnp.where(idx == e, offs_ref[e], base)
    pos_ref[...] = rank_ref[...] + base


def _positions(offs, idx3, rank3):
    nch = idx3.shape[0]
    cb = 8
    spec = pl.BlockSpec((cb, TOP_K, 128), lambda i, offs: (i, 0, 0))
    return pl.pallas_call(
        _positions_kernel,
        out_shape=jax.ShapeDtypeStruct((nch, TOP_K, 128), jnp.int32),
        grid_spec=pltpu.PrefetchScalarGridSpec(num_scalar_prefetch=1, grid=(nch // cb,),
                                               in_specs=[spec, spec], out_specs=spec),
        compiler_params=_cparams(("parallel",)),
        name="positions",
    )(offs, idx3, rank3)


def _route_positions(idx3, rank3, counts, n_rows):
    tm = TM_X
    counts = counts.astype(jnp.int32)
    padded = ((counts + tm - 1) // tm) * tm
    ends = jnp.cumsum(padded)
    offs = ends - padded
    pos3 = _positions(offs.astype(jnp.int32), idx3, rank3)
    r_pad = n_rows * TOP_K + N_EXPERTS * tm
    nt = r_pad // tm
    tile_ids = jnp.arange(nt, dtype=jnp.int32)
    tile_expert = jnp.sum((ends // tm)[None, :] <= tile_ids[:, None], axis=1)
    tile_expert = jnp.minimum(tile_expert, N_EXPERTS - 1).astype(jnp.int32)
    n_used = (ends[-1] // tm).astype(jnp.int32).reshape(1)
    return pos3, tile_expert, n_used, r_pad


def _q_perm():
    perm = np.empty(512, np.int32)
    for h in range(HEADS):
        for half in range(2):
            for d in range(QK_DIM):
                perm[h * 128 + half * QK_DIM + d] = half * HEADS * QK_DIM + h * QK_DIM + d
    return perm


def _rope_tables():
    t = jnp.arange(L)
    row = (t // GRID_W).astype(F32)
    col = (t % GRID_W).astype(F32)
    n_freq = QK_DIM // 4
    inv = ROPE_BASE ** (-jnp.arange(n_freq, dtype=F32) / n_freq)
    ar = row[:, None] * inv
    ac = col[:, None] * inv
    cos64 = jnp.concatenate([jnp.cos(ar), jnp.cos(ar), jnp.cos(ac), jnp.cos(ac)], axis=1)
    sin64 = jnp.concatenate([-jnp.sin(ar), jnp.sin(ar), -jnp.sin(ac), jnp.sin(ac)], axis=1)
    cos_t = jnp.concatenate([jnp.tile(cos64, (1, 2)), jnp.ones((TM_MIX, 128), F32)], axis=0)
    sin_t = jnp.concatenate([jnp.tile(sin64, (1, 2)), jnp.zeros((TM_MIX, 128), F32)], axis=0)
    return cos_t, sin_t


def _split_bf16(w):
    hi = w.astype(BF16)
    return hi, (w - hi.astype(F32)).astype(BF16)


def kernel(x, c, ctx, c_ctx, w_ada, b_ada, g_norm1, g_norm2, w_in, w_out, g_v, w_s, b_s, w_conv, g_q, g_k,
           lam_q1, lam_k1, lam_q2, lam_k2, g_sub, w_router, b_router, w_gate, w_up, w_down,
           ws_gate, ws_up, ws_down):
    src = (x.reshape(NL, D), ctx.reshape(NC, D))
    cc = jnp.concatenate([c, c_ctx[None, :], jnp.zeros((MOD_ROWS - B - 1, D), F32)], axis=0)
    mod = _ada(cc, w_ada, b_ada)
    cos_t, sin_t = _rope_tables()
    perm = _q_perm()
    col_perm = np.concatenate([np.arange(OFF_Q), OFF_Q + perm, OFF_K + perm, np.arange(OFF_V, D_IN)])
    bd = jnp.asarray(np.kron(np.eye(8, dtype=np.float32), np.full((64, 64), 1.0 / 64, np.float32)), BF16)

    for l in range(DEPTH):
        last = l == DEPTH - 1
        lam_init = 0.8 - 0.6 * math.exp(-0.3 * l)
        lam = (jnp.exp(jnp.sum(lam_q1[l] * lam_k1[l])) - jnp.exp(jnp.sum(lam_q2[l] * lam_k2[l]))
               + lam_init).reshape(1).astype(F32)
        mod3 = mod[l].reshape(MOD_ROWS, 1, 6 * D)
        w_in_bf = w_in[l][:, col_perm].astype(BF16)
        p = _in_proj(src[0], src[1], g_norm1[l], mod3, w_in_bf)
        bias_t = jnp.repeat(b_s[l].T, A_GD, axis=1)
        wconv = jnp.concatenate([w_conv[l], jnp.zeros((5, B_WIDTH), F32)], axis=0)
        yab, qa, qb, kk = _mixers(p, cos_t, sin_t, g_v[l].reshape(1, A_WIDTH), w_s[l].astype(BF16), bias_t, wconv,
                                  jnp.tile(g_q[l], 8).reshape(1, 512), jnp.tile(g_k[l], 8).reshape(1, 512), bd)
        gsub = g_sub[l].reshape(1, V_DIM)
        coef = 1.0 - lam_init
        n_rows = NL if last else NR
        yc = _attention(lam, qa, qb, kk, p, gsub, coef, ctx_queries=False)
        yc_ctx = yc if last else _attention(lam, qa, qb, kk, p, gsub, coef, ctx_queries=True)
        wr_hi, wr_lo = _split_bf16(w_router[l].T)
        wo_bf = w_out[l].astype(BF16)
        ws_bf = (ws_gate[l].astype(BF16), ws_up[l].astype(BF16), ws_down[l].astype(BF16))
        nh = n_rows // MOE_GROUPS
        outs = []
        for h in range(MOE_GROUPS):
            xh, h2, h2p, idx3, gate3, rank3, counts = _out_router(
                src[0], src[1], yab, yc, yc_ctx, wo_bf, g_norm2[l], mod3, wr_hi, wr_lo, b_router[l], h * nh, nh)
            pos3, tile_expert, n_used, r_pad = _route_positions(idx3, rank3, counts[:, 0], nh)
            x_sorted = _sc_dispatch(h2p.reshape(nh, 4, 128), pos3, r_pad)
            y_sorted = _experts(l, tile_expert, n_used, x_sorted.reshape(r_pad * 4, 128), w_gate, w_up, w_down)
            yg = _sc_collect(y_sorted.reshape(r_pad, 4, 128), pos3)
            gates = gate3.transpose(0, 2, 1).reshape(nh, TOP_K)
            outs.append(_shared_residual(xh, h2, yg.reshape(TOP_K, nh * 4, 128), gates, *ws_bf, mod3, h * nh, nh))
        src = (outs[0], outs[1])
    return jnp.concatenate(outs, axis=0).reshape(B, L, D)
```

```python
import functools
import math

import numpy as np
import jax
import jax.numpy as jnp
from jax import lax
from jax.experimental import pallas as pl
from jax.experimental.pallas import tpu as pltpu
from jax.experimental.pallas import tpu_sc as plsc

F32 = jnp.float32
BF16 = jnp.bfloat16

D = 1024
B = 8
L = 2048
DEPTH = 2
GRID_W = 64
CTX = 256
A_WIDTH = 256
A_GROUPS = 4
A_GD = 64
CHUNK = 128
B_WIDTH = 256
C_WIDTH = 512
HEADS = 4
V_DIM = 128
QK_DIM = 64
ROPE_BASE = 10000.0
OFF_BB = 512
OFF_BC = 768
OFF_BX = 1024
OFF_Q = 1280
OFF_K = 1792
OFF_V = 2304
D_IN = 2816
N_EXPERTS = 64
TOP_K = 8
D_EXPERT = 256
D_SHARED = 256
ROUTED_SCALE = 2.5
EPS = 1e-6

NL = B * L
NC = B * CTX
NR = NL + NC
MOD_ROWS = 16
LOG2E = 1.4426950408889634

TM_IN = 512
TM_MIX = 256
TQ = 1024
ATT_CHAIN = 256
TM_OUT = 512
TM_X = 1024
X_CHAINS = 2
MOE_GROUPS = 1
TM_F = 256

_DN_T = (((1,), (1,)), ((), ()))


def _cparams(sem, vmem_mb=None):
    kw = dict(dimension_semantics=sem)
    if vmem_mb is not None:
        kw["vmem_limit_bytes"] = vmem_mb * 1024 * 1024
    return pltpu.CompilerParams(**kw)


def _mod_row(i, tm):
    return jnp.where(i < NL // tm, i // (L // tm), B)


def _mod_spec(tm, chunk, t0=0):
    return pl.BlockSpec((None, 1, D), lambda i: (_mod_row(i + t0, tm), 0, chunk))


def _ada_kernel(c_ref, w_ref, b_ref, o_ref):
    c = c_ref[...]
    cs = c * jax.nn.sigmoid(c)
    o_ref[...] = jnp.dot(cs, w_ref[...], preferred_element_type=F32,
                         precision=lax.Precision.HIGHEST) + b_ref[...]


def _ada(cc, w_ada, b_ada):
    nb = 6
    return pl.pallas_call(
        _ada_kernel,
        out_shape=jax.ShapeDtypeStruct((DEPTH, MOD_ROWS, 6 * D), F32),
        grid=(DEPTH, nb),
        in_specs=[pl.BlockSpec((MOD_ROWS, D), lambda l, j: (0, 0)),
                  pl.BlockSpec((None, D, D), lambda l, j: (l, 0, j)),
                  pl.BlockSpec((None, 1, D), lambda l, j: (l, 0, j))],
        out_specs=pl.BlockSpec((None, MOD_ROWS, D), lambda l, j: (l, 0, j)),
        compiler_params=_cparams(("arbitrary", "arbitrary"), 40),
        name="ada_mod",
    )(cc, w_ada, b_ada.reshape(DEPTH, 1, 6 * D))


def _rms_mod(x, g, sc, sh):
    ms = jnp.mean(x * x, axis=-1, keepdims=True)
    return x * lax.rsqrt(ms + EPS) * g * (1.0 + sc) + sh


def _two_source_specs(tm, n_first, width=D, t0=0):
    return [pl.BlockSpec((tm, width), lambda i: (jnp.minimum(i + t0, n_first - 1), 0)),
            pl.BlockSpec((tm, width), lambda i: (jnp.maximum(i + t0 - n_first, 0), 0))]


def _two_source_rows(a_ref, b_ref, n_first, t0=0):
    return jnp.where(pl.program_id(0) + t0 < n_first, a_ref[...], b_ref[...])


def _in_proj_kernel(xa_ref, xb_ref, g_ref, sh_ref, sc_ref, w_ref, o_ref, *, n_first):
    h = _rms_mod(_two_source_rows(xa_ref, xb_ref, n_first), g_ref[...], sc_ref[...], sh_ref[...])
    o_ref[...] = jnp.dot(h.astype(BF16), w_ref[...], preferred_element_type=F32).astype(o_ref.dtype)


def _in_proj(x_first, x_second, g, mod3, w_bf):
    tm = TM_IN
    n_first = x_first.shape[0] // tm
    return pl.pallas_call(
        functools.partial(_in_proj_kernel, n_first=n_first),
        out_shape=jax.ShapeDtypeStruct((NR, D_IN), BF16),
        grid=(NR // tm,),
        in_specs=_two_source_specs(tm, n_first)
                 + [pl.BlockSpec((1, D), lambda i: (0, 0)),
                    _mod_spec(tm, 0), _mod_spec(tm, 1),
                    pl.BlockSpec((D, D_IN), lambda i: (0, 0))],
        out_specs=pl.BlockSpec((tm, D_IN), lambda i: (i, 0)),
        compiler_params=_cparams(("parallel",), 48),
        name="in_proj",
    )(x_first, x_second, g.reshape(1, D), mod3, mod3, w_bf)


def _group_rms(t, g, bd):
    sq = t * t
    hi = sq.astype(BF16)
    lo = (sq - hi.astype(F32)).astype(BF16)
    ms = (jnp.dot(hi, bd, preferred_element_type=F32) + jnp.dot(lo, bd, preferred_element_type=F32))
    return t * lax.rsqrt(ms + EPS) * g


def _rope(t, cos, sin):
    w = t.shape[1]
    lane = lax.broadcasted_iota(jnp.int32, t.shape, 1)
    first = (lane % 32) < 16
    partner = jnp.where(first, pltpu.roll(t, w - 16, 1), pltpu.roll(t, 16, 1))
    cos4 = jnp.concatenate([cos] * (w // 128), axis=1)
    sin4 = jnp.concatenate([sin] * (w // 128), axis=1)
    return t * cos4 + partner * sin4


def _mixers_kernel(p_ref, pc_ref, px_ref, nc_ref, nx_ref, cos_ref, sin_ref, gv_ref, ws_ref, bias_ref,
                   wconv_ref, gq_ref, gk_ref, bd_ref, yab_ref, qa_ref, qb_ref, kk_ref):
    tm = TM_MIX
    i = pl.program_id(0)
    tiles_per_seq = L // tm
    is_lat = i < NL // tm
    is_start = jnp.logical_or(jnp.logical_not(is_lat), i % tiles_per_seq == 0)
    is_end = jnp.logical_or(jnp.logical_not(is_lat), i % tiles_per_seq == tiles_per_seq - 1)

    uv = p_ref[:, 0:2 * A_WIDTH].astype(F32)
    uv = 0.5 * uv * (1.0 + lax.erf(uv * (2.0 ** -0.5)))
    u = uv[:, :A_WIDTH]
    v = uv[:, A_WIDTH:]
    ms = jnp.mean(v * v, axis=-1, keepdims=True)
    vb = (v * lax.rsqrt(ms + EPS) * gv_ref[...]).astype(BF16)
    lane = lax.broadcasted_iota(jnp.int32, (CHUNK, 128), 1)
    mixes = []
    for c in range(tm // CHUNK):
        vc = vb[c * CHUNK:(c + 1) * CHUNK]
        halves = []
        for j in range(2):
            vj = vc[:, j * 128:(j + 1) * 128]
            m0 = jnp.dot(ws_ref[2 * j], vj, preferred_element_type=F32)
            m1 = jnp.dot(ws_ref[2 * j + 1], vj, preferred_element_type=F32)
            halves.append(jnp.where(lane < A_GD, m0, m1))
        mixes.append(jnp.concatenate(halves, axis=1) + bias_ref[...])
    ya = u * jnp.concatenate(mixes, axis=0)

    bg = p_ref[:, OFF_BB:OFF_BC].astype(F32)
    z = p_ref[:, OFF_BC:OFF_BX].astype(F32) * p_ref[:, OFF_BX:OFF_Q].astype(F32)
    zp = (pc_ref[...].astype(F32) * px_ref[...].astype(F32))[15:16]
    zn = (nc_ref[...].astype(F32) * nx_ref[...].astype(F32))[0:1]
    zp = jnp.where(is_start, 0.0, zp)
    zn = jnp.where(is_end, 0.0, zn)
    row = lax.broadcasted_iota(jnp.int32, z.shape, 0)
    z_prev = jnp.where(row == 0, zp, pltpu.roll(z, 1, 0))
    z_next = jnp.where(row == tm - 1, zn, pltpu.roll(z, tm - 1, 0))
    yb = bg * (z_prev * wconv_ref[0:1] + z * wconv_ref[1:2] + z_next * wconv_ref[2:3])
    yab_ref[...] = jnp.concatenate([ya, yb], axis=1).astype(BF16)

    cos = cos_ref[...]
    sin = sin_ref[...]
    bd = bd_ref[...]
    q = _rope(_group_rms(p_ref[:, OFF_Q:OFF_K].astype(F32), gq_ref[...], bd), cos, sin)
    q = q * (QK_DIM ** -0.5 * LOG2E)
    lane5 = lax.broadcasted_iota(jnp.int32, q.shape, 1) % 128
    qa_ref[...] = jnp.where(lane5 < QK_DIM, q, 0.0).astype(BF16)
    qb_ref[...] = jnp.where(lane5 >= QK_DIM, q, 0.0).astype(BF16)
    k = _rope(_group_rms(p_ref[:, OFF_K:OFF_V].astype(F32), gk_ref[...], bd), cos, sin)
    kk_ref[...] = k.astype(BF16)


def _mixers(p, cos_t, sin_t, gv, ws_bf, bias_t, wconv, gq, gk, bd):
    tm = TM_MIX
    nt = NR // tm
    hb = tm // 16
    last_hb = NR // 16 - 1
    pos_blocks = L // tm

    def tab_map(i):
        return (jnp.where(i < NL // tm, i % pos_blocks, pos_blocks), 0)

    prev_map_c = lambda i: (jnp.maximum(i * hb - 1, 0), OFF_BC // 256)
    prev_map_x = lambda i: (jnp.maximum(i * hb - 1, 0), OFF_BX // 256)
    next_map_c = lambda i: (jnp.minimum((i + 1) * hb, last_hb), OFF_BC // 256)
    next_map_x = lambda i: (jnp.minimum((i + 1) * hb, last_hb), OFF_BX // 256)
    const2 = lambda i: (0, 0)
    outs = pl.pallas_call(
        _mixers_kernel,
        out_shape=[jax.ShapeDtypeStruct((NR, 2 * A_WIDTH), BF16),
                   jax.ShapeDtypeStruct((NR, 512), BF16),
                   jax.ShapeDtypeStruct((NR, 512), BF16),
                   jax.ShapeDtypeStruct((NR, 512), BF16)],
        grid=(nt,),
        in_specs=[pl.BlockSpec((tm, D_IN), lambda i: (i, 0)),
                  pl.BlockSpec((16, 256), prev_map_c), pl.BlockSpec((16, 256), prev_map_x),
                  pl.BlockSpec((16, 256), next_map_c), pl.BlockSpec((16, 256), next_map_x),
                  pl.BlockSpec((tm, 128), tab_map), pl.BlockSpec((tm, 128), tab_map),
                  pl.BlockSpec((1, A_WIDTH), const2),
                  pl.BlockSpec((A_GROUPS, CHUNK, CHUNK), lambda i: (0, 0, 0)),
                  pl.BlockSpec((CHUNK, A_WIDTH), const2),
                  pl.BlockSpec((8, B_WIDTH), const2),
                  pl.BlockSpec((1, 512), const2), pl.BlockSpec((1, 512), const2),
                  pl.BlockSpec((512, 512), const2)],
        out_specs=[pl.BlockSpec((tm, 512), lambda i: (i, 0))] * 4,
        compiler_params=_cparams(("parallel",), 48),
        name="mixers",
    )(p, p, p, p, p, cos_t, sin_t, gv, ws_bf, bias_t, wconv, gq, gk, bd)
    return outs


def _attn_kernel(lam_ref, qa_ref, qb_ref, *rest, n_seg, coef, tq):
    kv_refs = rest[:2 * n_seg]
    gsub_ref, o_ref, k_scr, vt_scr = rest[2 * n_seg:]

    @pl.when(pl.program_id(2) == 0)
    def _():
        off = 0
        for s in range(n_seg):
            n = kv_refs[s].shape[0]
            k_scr[off:off + n, :] = kv_refs[s][...]
            vt_scr[0:V_DIM, off:off + n] = kv_refs[n_seg + s][...].astype(F32).T.astype(BF16)
            off += n
        ones_row = lax.broadcasted_iota(jnp.int32, (16, off), 0) == 0
        vt_scr[V_DIM:, :] = jnp.where(ones_row, 1.0, 0.0).astype(BF16)

    k = k_scr[...]
    vt = vt_scr[...]
    lam = lam_ref[0]
    qc = ATT_CHAIN
    sts = []
    for c in range(tq // qc):
        rows = slice(c * qc, (c + 1) * qc)
        qs = jnp.concatenate([qa_ref[rows, :], qb_ref[rows, :]], axis=0)
        sts.append(lax.dot_general(k, qs, _DN_T, preferred_element_type=F32))
    for c in range(tq // qc):
        st = sts[c]
        pt = jnp.exp2(st - jnp.max(st, axis=0, keepdims=True)).astype(BF16)
        ot = jnp.dot(vt, pt, preferred_element_type=F32)
        inv = 1.0 / ot[V_DIM:V_DIM + 1, :]
        dt = ot[0:V_DIM, :qc] * inv[:, :qc] - ot[0:V_DIM, qc:] * (lam * inv[:, qc:])
        o = dt.T
        ms = jnp.mean(o * o, axis=-1, keepdims=True)
        o_ref[c * qc:(c + 1) * qc, :] = (o * lax.rsqrt(ms + EPS) * gsub_ref[...] * coef).astype(o_ref.dtype)


def _attention(lam, qa, qb, kk, p, gsub, coef, *, ctx_queries):
    vcol = OFF_V // 128
    if ctx_queries:
        tq = CTX
        nq, lk, n_seg = 1, CTX, 1
        q_map = lambda b, h, qi: (NL // tq + b, h)
        kv_specs = [pl.BlockSpec((CTX, 128), lambda b, h, qi: (NL // CTX + b, h)),
                    pl.BlockSpec((CTX, 128), lambda b, h, qi: (NL // CTX + b, vcol + h))]
        kv_args = [kk, p]
        rows = NC
        o_map = lambda b, h, qi: (b, h)
    else:
        tq = TQ
        nq, lk, n_seg = L // tq, CTX + L, 2
        q_map = lambda b, h, qi: (b * (L // tq) + qi, h)
        kv_specs = [pl.BlockSpec((CTX, 128), lambda b, h, qi: (NL // CTX + b, h)),
                    pl.BlockSpec((L, 128), lambda b, h, qi: (b, h)),
                    pl.BlockSpec((CTX, 128), lambda b, h, qi: (NL // CTX + b, vcol + h)),
                    pl.BlockSpec((L, 128), lambda b, h, qi: (b, vcol + h))]
        kv_args = [kk, kk, p, p]
        rows = NL
        o_map = lambda b, h, qi: (b * (L // tq) + qi, h)
    return pl.pallas_call(
        functools.partial(_attn_kernel, n_seg=n_seg, coef=coef, tq=tq),
        out_shape=jax.ShapeDtypeStruct((rows, C_WIDTH), BF16),
        grid=(B, HEADS, nq),
        in_specs=[pl.BlockSpec(memory_space=pltpu.SMEM),
                  pl.BlockSpec((tq, 128), q_map), pl.BlockSpec((tq, 128), q_map)]
                 + kv_specs + [pl.BlockSpec((1, V_DIM), lambda b, h, qi: (0, 0))],
        out_specs=pl.BlockSpec((tq, 128), o_map),
        scratch_shapes=[pltpu.VMEM((lk, 128), BF16), pltpu.VMEM((V_DIM + 16, lk), BF16)],
        compiler_params=_cparams(("parallel", "parallel", "arbitrary"), 56),
        name="attn_ctx" if ctx_queries else "attn_lat",
    )(lam, qa, qb, *kv_args, gsub)


def _pack_rows(t, out_ref, row0=0):
    half = D // 2
    w = pltpu.pack_elementwise([t[:, :half], t[:, half:]], packed_dtype=BF16)
    w = lax.bitcast_convert_type(w, jnp.uint32)
    rows = t.shape[0]
    for j in range(4):
        out_ref[pl.ds(4 * row0 + j, rows, stride=4), :] = w[:, j * 128:(j + 1) * 128]


def _unpack_rows(ref, rows, lead=None, row0=0):
    los, his = [], []
    for j in range(4):
        sl = pl.ds(4 * row0 + j, rows, stride=4)
        w = ref[sl, :] if lead is None else ref[lead, sl, :]
        los.append(pltpu.unpack_elementwise(w, index=0, packed_dtype=BF16, unpacked_dtype=F32))
        his.append(pltpu.unpack_elementwise(w, index=1, packed_dtype=BF16, unpacked_dtype=F32))
    return jnp.concatenate(los, axis=1), jnp.concatenate(his, axis=1)


def _out_router_kernel(xa_ref, xb_ref, yab_ref, yca_ref, ycb_ref, wo_ref, g1_ref, g2n_ref, sh2_ref, sc2_ref, wrh_ref,
                       wrl_ref, br_ref, xo_ref, h2p_ref, idx_ref, gate_ref, rank_ref, cnt_ref, run_ref, *,
                       n_first, n_first_c, t0):
    tm = TM_OUT

    @pl.when(pl.program_id(0) == 0)
    def _():
        run_ref[...] = jnp.zeros_like(run_ref)

    yc = _two_source_rows(yca_ref, ycb_ref, n_first_c, t0)
    y = jnp.dot(jnp.concatenate([yab_ref[...], yc], axis=1), wo_ref[...], preferred_element_type=F32)
    x = _two_source_rows(xa_ref, xb_ref, n_first, t0) + g1_ref[...] * y
    xo_ref[...] = x
    h2 = _rms_mod(x, g2n_ref[...], sc2_ref[...], sh2_ref[...])
    hi = h2.astype(BF16)
    _pack_rows(h2, h2p_ref)
    lo = (h2 - hi.astype(F32)).astype(BF16)
    wh = wrh_ref[...]
    z = (lax.dot_general(wh, hi, _DN_T, preferred_element_type=F32)
         + lax.dot_general(wh, lo, _DN_T, preferred_element_type=F32)
         + lax.dot_general(wrl_ref[...], hi, _DN_T, preferred_element_type=F32))
    scores = jax.nn.sigmoid(z)
    work = scores + br_ref[...]
    eio = lax.broadcasted_iota(jnp.int32, work.shape, 0)
    idxs, sels, hits = [], [], []
    for _ in range(TOP_K):
        m = jnp.max(work, axis=0, keepdims=True)
        idx = jnp.min(jnp.where(work == m, eio, N_EXPERTS), axis=0, keepdims=True)
        hit = eio == idx
        sels.append(jnp.sum(jnp.where(hit, scores, 0.0), axis=0, keepdims=True))
        idxs.append(idx)
        hits.append(hit)
        work = jnp.where(hit, -jnp.inf, work)
    sel = jnp.concatenate(sels, axis=0)
    idx8 = jnp.concatenate(idxs, axis=0)
    gate8 = sel / jnp.sum(sel, axis=0, keepdims=True) * ROUTED_SCALE

    chosen = functools.reduce(jnp.logical_or, hits)
    before = (lax.broadcasted_iota(jnp.int32, (tm, tm), 0) < lax.broadcasted_iota(jnp.int32, (tm, tm), 1))
    prefix = jnp.dot(jnp.where(chosen, 1.0, 0.0).astype(BF16), jnp.where(before, 1.0, 0.0).astype(BF16),
                     preferred_element_type=F32)
    rank_dense = prefix + run_ref[:, 0:1]
    rank8 = jnp.concatenate([jnp.sum(jnp.where(h, rank_dense, 0.0), axis=0, keepdims=True) for h in hits],
                            axis=0).astype(jnp.int32)
    run = run_ref[...] + jnp.sum(jnp.where(chosen, 1.0, 0.0), axis=1, keepdims=True)
    run_ref[...] = run
    cnt_ref[...] = run
    for c in range(tm // 128):
        idx_ref[c] = idx8[:, c * 128:(c + 1) * 128]
        gate_ref[c] = gate8[:, c * 128:(c + 1) * 128]
        rank_ref[c] = rank8[:, c * 128:(c + 1) * 128]


def _out_router(x_first, x_second, yab, yc_first, yc_second, wo_bf, g2n, mod3, wr_hi, wr_lo, br, row0, n_rows):
    tm = TM_OUT
    t0 = row0 // tm
    n_first = x_first.shape[0] // tm
    n_first_c = yc_first.shape[0] // tm
    const2 = lambda i: (0, 0)
    row = lambda i: (i, 0)
    chunk3 = pl.BlockSpec((tm // 128, TOP_K, 128), lambda i: (i, 0, 0))
    nch = n_rows // 128
    return pl.pallas_call(
        functools.partial(_out_router_kernel, n_first=n_first, n_first_c=n_first_c, t0=t0),
        out_shape=[jax.ShapeDtypeStruct((n_rows, D), F32),
                   jax.ShapeDtypeStruct((n_rows * 4, 128), jnp.uint32),
                   jax.ShapeDtypeStruct((nch, TOP_K, 128), jnp.int32),
                   jax.ShapeDtypeStruct((nch, TOP_K, 128), F32),
                   jax.ShapeDtypeStruct((nch, TOP_K, 128), jnp.int32),
                   jax.ShapeDtypeStruct((N_EXPERTS, 128), F32)],
        grid=(n_rows // tm,),
        in_specs=_two_source_specs(tm, n_first, D, t0)
                 + [pl.BlockSpec((tm, 512), lambda i: (i + t0, 0))]
                 + _two_source_specs(tm, n_first_c, C_WIDTH, t0)
                 + [pl.BlockSpec((D, D), const2),
                  _mod_spec(tm, 2, t0),
                  pl.BlockSpec((1, D), const2), _mod_spec(tm, 3, t0), _mod_spec(tm, 4, t0),
                  pl.BlockSpec((N_EXPERTS, D), const2), pl.BlockSpec((N_EXPERTS, D), const2),
                  pl.BlockSpec((N_EXPERTS, 1), const2)],
        out_specs=[pl.BlockSpec((tm, D), row), pl.BlockSpec((tm * 4, 128), row),
                   chunk3, chunk3, chunk3, pl.BlockSpec((N_EXPERTS, 128), const2)],
        scratch_shapes=[pltpu.VMEM((N_EXPERTS, 128), F32)],
        compiler_params=_cparams(("arbitrary",), 48),
        name="out_router",
    )(x_first, x_second, yab, yc_first, yc_second, wo_bf, mod3, g2n.reshape(1, D), mod3, mod3, wr_hi, wr_lo,
      br.reshape(N_EXPERTS, 1))


def _experts_kernel(te_ref, nu_ref, x_ref, wg_ref, wu_ref, wd_ref, y_ref, wgu_s, wd_s):
    i = pl.program_id(0)
    used = i < nu_ref[0]
    new_expert = jnp.logical_or(i == 0, te_ref[i] != te_ref[jnp.maximum(i - 1, 0)])

    @pl.when(jnp.logical_and(used, new_expert))
    def _():
        wgu_s[:, 0:D_EXPERT] = wg_ref[...].astype(BF16)
        wgu_s[:, D_EXPERT:] = wu_ref[...].astype(BF16)
        wd_s[...] = wd_ref[...].astype(BF16)

    @pl.when(used)
    def _():
        rc = TM_X // X_CHAINS
        wgu = wgu_s[...]
        wd = wd_s[...]
        abs_ = []
        for c in range(X_CHAINS):
            x_lo, x_hi = _unpack_rows(x_ref, rc, row0=c * rc)
            x = jnp.concatenate([x_lo.astype(BF16), x_hi.astype(BF16)], axis=1)
            abs_.append(jnp.dot(x, wgu, preferred_element_type=F32))
        for c in range(X_CHAINS):
            a = abs_[c][:, :D_EXPERT]
            hid = (a * jax.nn.sigmoid(a) * abs_[c][:, D_EXPERT:]).astype(BF16)
            _pack_rows(jnp.dot(hid, wd, preferred_element_type=F32), y_ref, row0=c * rc)


def _experts(layer, tile_expert, n_used, x_sorted, w_gate, w_up, w_down):
    tm = TM_X
    r_pad = x_sorted.shape[0] // 4
    nt = r_pad // tm
    row = lambda i, te, nu: (jnp.minimum(i, nu[0] - 1), 0)
    wmap = lambda i, te, nu: (layer, te[i], 0, 0)
    grid_spec = pltpu.PrefetchScalarGridSpec(
        num_scalar_prefetch=2,
        grid=(nt,),
        in_specs=[pl.BlockSpec((tm * 4, 128), row),
                  pl.BlockSpec((None, None, D, D_EXPERT), wmap),
                  pl.BlockSpec((None, None, D, D_EXPERT), wmap),
                  pl.BlockSpec((None, None, D_EXPERT, D), wmap)],
        out_specs=pl.BlockSpec((tm * 4, 128), row),
        scratch_shapes=[pltpu.VMEM((D, 2 * D_EXPERT), BF16), pltpu.VMEM((D_EXPERT, D), BF16)],
    )
    return pl.pallas_call(
        _experts_kernel,
        out_shape=jax.ShapeDtypeStruct((r_pad * 4, 128), jnp.uint32),
        grid_spec=grid_spec,
        compiler_params=_cparams(("arbitrary",), 48),
        name="experts",
    )(tile_expert, n_used, x_sorted, w_gate, w_up, w_down)


SC_CORES = 2
SC_SUBCORES = 16
SC_WORKERS = SC_CORES * SC_SUBCORES
SC_CHUNK = 128


def _sc_mesh():
    return plsc.VectorSubcoreMesh(core_axis_name="c", subcore_axis_name="s")


def _sc_params():
    return pltpu.CompilerParams(use_tc_tiling_on_sc=True)


def _sc_dispatch(h2p, pos3, r_pad):
    nch = pos3.shape[0]
    steps = -(-nch // SC_WORKERS)

    def body(h_hbm, pos_hbm, out_hbm, idx_v, rows_v, sem):
        wid = lax.axis_index("s") * SC_CORES + lax.axis_index("c")

        @pl.loop(0, steps)
        def _(s):
            ch = wid + s * SC_WORKERS

            @pl.when(ch < nch)
            def _():
                pltpu.sync_copy(pos_hbm.at[ch], idx_v)
                pltpu.sync_copy(h_hbm.at[pl.ds(ch * SC_CHUNK, SC_CHUNK)], rows_v)
                copies = [pltpu.async_copy(rows_v, out_hbm.at[idx_v.at[k]], sem) for k in range(TOP_K)]
                for cp in copies:
                    cp.wait()

    return pl.kernel(
        body,
        out_type=jax.ShapeDtypeStruct((r_pad, 4, 128), jnp.uint32),
        mesh=_sc_mesh(),
        scratch_types=[pltpu.VMEM((TOP_K, SC_CHUNK), jnp.int32),
                       pltpu.VMEM((SC_CHUNK, 4, 128), jnp.uint32),
                       pltpu.SemaphoreType.DMA],
        compiler_params=_sc_params(),
        name="sc_dispatch",
    )(h2p, pos3)


def _sc_collect(y_sorted, pos3):
    nch = pos3.shape[0]
    steps = -(-nch // SC_WORKERS)

    half = SC_CHUNK // 2
    units = [(k, hh) for k in range(TOP_K) for hh in range(2)]

    def body(y_hbm, pos_hbm, out_hbm, idx_v, rows_a, rows_b, sem_a, sem_b):
        wid = lax.axis_index("s") * SC_CORES + lax.axis_index("c")
        bufs = (rows_a, rows_b)
        sems = (sem_a, sem_b)

        def gather(u):
            k, hh = units[u]
            return pltpu.async_copy(y_hbm.at[idx_v.at[k, pl.ds(hh * half, half)]], bufs[u % 2], sems[u % 2])

        @pl.loop(0, steps)
        def _(s):
            ch = wid + s * SC_WORKERS

            @pl.when(ch < nch)
            def _():
                pltpu.sync_copy(pos_hbm.at[ch], idx_v)
                pending = gather(0)
                for u, (k, hh) in enumerate(units):
                    nxt = gather(u + 1) if u + 1 < len(units) else None
                    pending.wait()
                    pltpu.sync_copy(bufs[u % 2], out_hbm.at[k, pl.ds(ch * SC_CHUNK + hh * half, half)])
                    pending = nxt

    return pl.kernel(
        body,
        out_type=jax.ShapeDtypeStruct((TOP_K, nch * SC_CHUNK, 4, 128), jnp.uint32),
        mesh=_sc_mesh(),
        scratch_types=[pltpu.VMEM((TOP_K, SC_CHUNK), jnp.int32),
                       pltpu.VMEM((half, 4, 128), jnp.uint32),
                       pltpu.VMEM((half, 4, 128), jnp.uint32),
                       pltpu.SemaphoreType.DMA, pltpu.SemaphoreType.DMA],
        compiler_params=_sc_params(),
        name="sc_collect",
    )(y_sorted, pos3)


def _shared_kernel(x_ref, h2p_ref, yg_ref, gate_ref, wg_ref, wu_ref, wd_ref, g2_ref, o_ref):
    h_lo, h_hi = _unpack_rows(h2p_ref, TM_F)
    h = jnp.concatenate([h_lo.astype(BF16), h_hi.astype(BF16)], axis=1)
    a = jnp.dot(h, wg_ref[...], preferred_element_type=F32)
    b = jnp.dot(h, wu_ref[...], preferred_element_type=F32)
    hid = (a * jax.nn.sigmoid(a) * b).astype(BF16)
    f = jnp.dot(hid, wd_ref[...], preferred_element_type=F32)
    gate = gate_ref[...]
    f_lo = f[:, :D // 2]
    f_hi = f[:, D // 2:]
    for k in range(TOP_K):
        y_lo, y_hi = _unpack_rows(yg_ref, TM_F, lead=k)
        f_lo = f_lo + gate[:, k:k + 1] * y_lo
        f_hi = f_hi + gate[:, k:k + 1] * y_hi
    o_ref[...] = x_ref[...] + g2_ref[...] * jnp.concatenate([f_lo, f_hi], axis=1)


def _shared_residual(xa, h2p, yg, gates, wsg_bf, wsu_bf, wsd_bf, mod3, row0, n_rows):
    tm = TM_F
    t0 = row0 // tm
    row = lambda i: (i, 0)
    const2 = lambda i: (0, 0)
    return pl.pallas_call(
        _shared_kernel,
        out_shape=jax.ShapeDtypeStruct((n_rows, D), F32),
        grid=(n_rows // tm,),
        in_specs=[pl.BlockSpec((tm, D), row), pl.BlockSpec((tm * 4, 128), row),
                  pl.BlockSpec((TOP_K, tm * 4, 128), lambda i: (0, i, 0)),
                  pl.BlockSpec((tm, TOP_K), row),
                  pl.BlockSpec((D, D_SHARED), const2), pl.BlockSpec((D, D_SHARED), const2),
                  pl.BlockSpec((D_SHARED, D), const2), _mod_spec(tm, 5, t0)],
        out_specs=pl.BlockSpec((tm, D), row),
        compiler_params=_cparams(("parallel",), 48),
        name="shared_residual",
    )(xa, h2p, yg, gates, wsg_bf, wsu_bf, wsd_bf, mod3)


def _positions_kernel(offs_ref, idx_ref, rank_ref, pos_ref):
    idx = idx_ref[...]
    base = jnp.zeros_like(idx)
    for e in range(N_EXPERTS):
        base = jnp.where(idx == e, offs_ref[e], base)
    pos_ref[...] = rank_ref[...] + base


def _positions(offs, idx3, rank3):
    nch = idx3.shape[0]
    cb = 8
    spec = pl.BlockSpec((cb, TOP_K, 128), lambda i, offs: (i, 0, 0))
    return pl.pallas_call(
        _positions_kernel,
        out_shape=jax.ShapeDtypeStruct((nch, TOP_K, 128), jnp.int32),
        grid_spec=pltpu.PrefetchScalarGridSpec(num_scalar_prefetch=1, grid=(nch // cb,),
                                               in_specs=[spec, spec], out_specs=spec),
        compiler_params=_cparams(("parallel",)),
        name="positions",
    )(offs, idx3, rank3)


def _route_positions(idx3, rank3, counts, n_rows):
    tm = TM_X
    counts = counts.astype(jnp.int32)
    padded = ((counts + tm - 1) // tm) * tm
    ends = jnp.cumsum(padded)
    offs = ends - padded
    pos3 = _positions(offs.astype(jnp.int32), idx3, rank3)
    r_pad = n_rows * TOP_K + N_EXPERTS * tm
    nt = r_pad // tm
    tile_ids = jnp.arange(nt, dtype=jnp.int32)
    tile_expert = jnp.sum((ends // tm)[None, :] <= tile_ids[:, None], axis=1)
    tile_expert = jnp.minimum(tile_expert, N_EXPERTS - 1).astype(jnp.int32)
    n_used = (ends[-1] // tm).astype(jnp.int32).reshape(1)
    return pos3, tile_expert, n_used, r_pad


def _q_perm():
    perm = np.empty(512, np.int32)
    for h in range(HEADS):
        for half in range(2):
            for d in range(QK_DIM):
                perm[h * 128 + half * QK_DIM + d] = half * HEADS * QK_DIM + h * QK_DIM + d
    return perm


def _rope_tables():
    t = jnp.arange(L)
    row = (t // GRID_W).astype(F32)
    col = (t % GRID_W).astype(F32)
    n_freq = QK_DIM // 4
    inv = ROPE_BASE ** (-jnp.arange(n_freq, dtype=F32) / n_freq)
    ar = row[:, None] * inv
    ac = col[:, None] * inv
    cos64 = jnp.concatenate([jnp.cos(ar), jnp.cos(ar), jnp.cos(ac), jnp.cos(ac)], axis=1)
    sin64 = jnp.concatenate([-jnp.sin(ar), jnp.sin(ar), -jnp.sin(ac), jnp.sin(ac)], axis=1)
    cos_t = jnp.concatenate([jnp.tile(cos64, (1, 2)), jnp.ones((TM_MIX, 128), F32)], axis=0)
    sin_t = jnp.concatenate([jnp.tile(sin64, (1, 2)), jnp.zeros((TM_MIX, 128), F32)], axis=0)
    return cos_t, sin_t


def _split_bf16(w):
    hi = w.astype(BF16)
    return hi, (w - hi.astype(F32)).astype(BF16)


def kernel(x, c, ctx, c_ctx, w_ada, b_ada, g_norm1, g_norm2, w_in, w_out, g_v, w_s, b_s, w_conv, g_q, g_k,
           lam_q1, lam_k1, lam_q2, lam_k2, g_sub, w_router, b_router, w_gate, w_up, w_down,
           ws_gate, ws_up, ws_down):
    src = (x.reshape(NL, D), ctx.reshape(NC, D))
    cc = jnp.concatenate([c, c_ctx[None, :], jnp.zeros((MOD_ROWS - B - 1, D), F32)], axis=0)
    mod = _ada(cc, w_ada, b_ada)
    cos_t, sin_t = _rope_tables()
    perm = _q_perm()
    col_perm = np.concatenate([np.arange(OFF_Q), OFF_Q + perm, OFF_K + perm, np.arange(OFF_V, D_IN)])
    bd = jnp.asarray(np.kron(np.eye(8, dtype=np.float32), np.full((64, 64), 1.0 / 64, np.float32)), BF16)

    for l in range(DEPTH):
        last = l == DEPTH - 1
        lam_init = 0.8 - 0.6 * math.exp(-0.3 * l)
        lam = (jnp.exp(jnp.sum(lam_q1[l] * lam_k1[l])) - jnp.exp(jnp.sum(lam_q2[l] * lam_k2[l]))
               + lam_init).reshape(1).astype(F32)
        mod3 = mod[l].reshape(MOD_ROWS, 1, 6 * D)
        w_in_bf = w_in[l][:, col_perm].astype(BF16)
        p = _in_proj(src[0], src[1], g_norm1[l], mod3, w_in_bf)
        bias_t = jnp.repeat(b_s[l].T, A_GD, axis=1)
        wconv = jnp.concatenate([w_conv[l], jnp.zeros((5, B_WIDTH), F32)], axis=0)
        yab, qa, qb, kk = _mixers(p, cos_t, sin_t, g_v[l].reshape(1, A_WIDTH), w_s[l].astype(BF16), bias_t, wconv,
                                  jnp.tile(g_q[l], 8).reshape(1, 512), jnp.tile(g_k[l], 8).reshape(1, 512), bd)
        gsub = g_sub[l].reshape(1, V_DIM)
        coef = 1.0 - lam_init
        n_rows = NL if last else NR
        yc = _attention(lam, qa, qb, kk, p, gsub, coef, ctx_queries=False)
        yc_ctx = yc if last else _attention(lam, qa, qb, kk, p, gsub, coef, ctx_queries=True)
        wr_hi, wr_lo = _split_bf16(w_router[l].T)
        wo_bf = w_out[l].astype(BF16)
        ws_bf = (ws_gate[l].astype(BF16), ws_up[l].astype(BF16), ws_down[l].astype(BF16))
        nh = n_rows // MOE_GROUPS
        outs = []
        for h in range(MOE_GROUPS):
            xh, h2p, idx3, gate3, rank3, counts = _out_router(
                src[0], src[1], yab, yc, yc_ctx, wo_bf, g_norm2[l], mod3, wr_hi, wr_lo, b_router[l], h * nh, nh)
            pos3, tile_expert, n_used, r_pad = _route_positions(idx3, rank3, counts[:, 0], nh)
            x_sorted = _sc_dispatch(h2p.reshape(nh, 4, 128), pos3, r_pad)
            y_sorted = _experts(l, tile_expert, n_used, x_sorted.reshape(r_pad * 4, 128), w_gate, w_up, w_down)
            yg = _sc_collect(y_sorted.reshape(r_pad, 4, 128), pos3)
            gates = gate3.transpose(0, 2, 1).reshape(nh, TOP_K)
            outs.append(_shared_residual(xh, h2p, yg.reshape(TOP_K, nh * 4, 128), gates, *ws_bf, mod3, h * nh, nh))
        src = (outs[0], outs[-1])
    return jnp.concatenate(outs, axis=0).reshape(B, L, D)
```

```python
import functools
import math

import numpy as np
import jax
import jax.numpy as jnp
from jax import lax
from jax.experimental import pallas as pl
from jax.experimental.pallas import tpu as pltpu
from jax.experimental.pallas import tpu_sc as plsc

F32 = jnp.float32
BF16 = jnp.bfloat16

D = 1024
B = 8
L = 2048
DEPTH = 2
GRID_W = 64
CTX = 256
A_WIDTH = 256
A_GROUPS = 4
A_GD = 64
CHUNK = 128
B_WIDTH = 256
C_WIDTH = 512
HEADS = 4
V_DIM = 128
QK_DIM = 64
ROPE_BASE = 10000.0
OFF_BB = 512
OFF_BC = 768
OFF_BX = 1024
OFF_Q = 1280
OFF_K = 1792
OFF_V = 2304
D_IN = 2816
N_EXPERTS = 64
TOP_K = 8
D_EXPERT = 256
D_SHARED = 256
ROUTED_SCALE = 2.5
EPS = 1e-6

NL = B * L
NC = B * CTX
NR = NL + NC
MOD_ROWS = 16
LOG2E = 1.4426950408889634

TM_IN = 512
TM_MIX = 256
TQ = 1024
ATT_CHAIN = 256
ATT_BOUND_MARGIN = 1.02
ATT_MAX_SHIFT_RANGE = 100.0
TM_OUT = 512
TM_X = 1024
X_CHAINS = 2
MOE_GROUPS = 1
TM_F = 256

_DN_T = (((1,), (1,)), ((), ()))


def _cparams(sem, vmem_mb=None):
    kw = dict(dimension_semantics=sem)
    if vmem_mb is not None:
        kw["vmem_limit_bytes"] = vmem_mb * 1024 * 1024
    return pltpu.CompilerParams(**kw)


def _mod_row(i, tm):
    return jnp.where(i < NL // tm, i // (L // tm), B)


def _mod_spec(tm, chunk, t0=0):
    return pl.BlockSpec((None, 1, D), lambda i: (_mod_row(i + t0, tm), 0, chunk))


def _ada_kernel(c_ref, w_ref, b_ref, o_ref):
    c = c_ref[...]
    cs = c * jax.nn.sigmoid(c)
    o_ref[...] = jnp.dot(cs, w_ref[...], preferred_element_type=F32,
                         precision=lax.Precision.HIGHEST) + b_ref[...]


def _ada(cc, w_ada, b_ada):
    nb = 6
    return pl.pallas_call(
        _ada_kernel,
        out_shape=jax.ShapeDtypeStruct((DEPTH, MOD_ROWS, 6 * D), F32),
        grid=(DEPTH, nb),
        in_specs=[pl.BlockSpec((MOD_ROWS, D), lambda l, j: (0, 0)),
                  pl.BlockSpec((None, D, D), lambda l, j: (l, 0, j)),
                  pl.BlockSpec((None, 1, D), lambda l, j: (l, 0, j))],
        out_specs=pl.BlockSpec((None, MOD_ROWS, D), lambda l, j: (l, 0, j)),
        compiler_params=_cparams(("arbitrary", "arbitrary"), 40),
        name="ada_mod",
    )(cc, w_ada, b_ada.reshape(DEPTH, 1, 6 * D))


def _rms_mod(x, g, sc, sh):
    ms = jnp.mean(x * x, axis=-1, keepdims=True)
    return x * lax.rsqrt(ms + EPS) * g * (1.0 + sc) + sh


def _two_source_specs(tm, n_first, width=D, t0=0):
    return [pl.BlockSpec((tm, width), lambda i: (jnp.minimum(i + t0, n_first - 1), 0)),
            pl.BlockSpec((tm, width), lambda i: (jnp.maximum(i + t0 - n_first, 0), 0))]


def _two_source_rows(a_ref, b_ref, n_first, t0=0):
    return jnp.where(pl.program_id(0) + t0 < n_first, a_ref[...], b_ref[...])


def _in_proj_kernel(xa_ref, xb_ref, g_ref, sh_ref, sc_ref, w_ref, o_ref, *, n_first):
    h = _rms_mod(_two_source_rows(xa_ref, xb_ref, n_first), g_ref[...], sc_ref[...], sh_ref[...])
    o_ref[...] = jnp.dot(h.astype(BF16), w_ref[...], preferred_element_type=F32).astype(o_ref.dtype)


def _in_proj(x_first, x_second, g, mod3, w_bf):
    tm = TM_IN
    n_first = x_first.shape[0] // tm
    return pl.pallas_call(
        functools.partial(_in_proj_kernel, n_first=n_first),
        out_shape=jax.ShapeDtypeStruct((NR, D_IN), BF16),
        grid=(NR // tm,),
        in_specs=_two_source_specs(tm, n_first)
                 + [pl.BlockSpec((1, D), lambda i: (0, 0)),
                    _mod_spec(tm, 0), _mod_spec(tm, 1),
                    pl.BlockSpec((D, D_IN), lambda i: (0, 0))],
        out_specs=pl.BlockSpec((tm, D_IN), lambda i: (i, 0)),
        compiler_params=_cparams(("parallel",), 48),
        name="in_proj",
    )(x_first, x_second, g.reshape(1, D), mod3, mod3, w_bf)


def _group_rms(t, g, bd):
    sq = t * t
    hi = sq.astype(BF16)
    lo = (sq - hi.astype(F32)).astype(BF16)
    ms = (jnp.dot(hi, bd, preferred_element_type=F32) + jnp.dot(lo, bd, preferred_element_type=F32))
    return t * lax.rsqrt(ms + EPS) * g


def _rope(t, cos, sin):
    w = t.shape[1]
    lane = lax.broadcasted_iota(jnp.int32, t.shape, 1)
    first = (lane % 32) < 16
    partner = jnp.where(first, pltpu.roll(t, w - 16, 1), pltpu.roll(t, 16, 1))
    cos4 = jnp.concatenate([cos] * (w // 128), axis=1)
    sin4 = jnp.concatenate([sin] * (w // 128), axis=1)
    return t * cos4 + partner * sin4


def _mixers_kernel(p_ref, pc_ref, px_ref, nc_ref, nx_ref, cos_ref, sin_ref, gv_ref, ws_ref, bias_ref,
                   wconv_ref, gq_ref, gk_ref, bd_ref, yab_ref, qa_ref, qb_ref, kk_ref):
    tm = TM_MIX
    i = pl.program_id(0)
    tiles_per_seq = L // tm
    is_lat = i < NL // tm
    is_start = jnp.logical_or(jnp.logical_not(is_lat), i % tiles_per_seq == 0)
    is_end = jnp.logical_or(jnp.logical_not(is_lat), i % tiles_per_seq == tiles_per_seq - 1)

    uv = p_ref[:, 0:2 * A_WIDTH].astype(F32)
    uv = 0.5 * uv * (1.0 + lax.erf(uv * (2.0 ** -0.5)))
    u = uv[:, :A_WIDTH]
    v = uv[:, A_WIDTH:]
    ms = jnp.mean(v * v, axis=-1, keepdims=True)
    vb = (v * lax.rsqrt(ms + EPS) * gv_ref[...]).astype(BF16)
    lane = lax.broadcasted_iota(jnp.int32, (CHUNK, 128), 1)
    mixes = []
    for c in range(tm // CHUNK):
        vc = vb[c * CHUNK:(c + 1) * CHUNK]
        halves = []
        for j in range(2):
            vj = vc[:, j * 128:(j + 1) * 128]
            m0 = jnp.dot(ws_ref[2 * j], vj, preferred_element_type=F32)
            m1 = jnp.dot(ws_ref[2 * j + 1], vj, preferred_element_type=F32)
            halves.append(jnp.where(lane < A_GD, m0, m1))
        mixes.append(jnp.concatenate(halves, axis=1) + bias_ref[...])
    ya = u * jnp.concatenate(mixes, axis=0)

    bg = p_ref[:, OFF_BB:OFF_BC].astype(F32)
    z = p_ref[:, OFF_BC:OFF_BX].astype(F32) * p_ref[:, OFF_BX:OFF_Q].astype(F32)
    zp = (pc_ref[...].astype(F32) * px_ref[...].astype(F32))[15:16]
    zn = (nc_ref[...].astype(F32) * nx_ref[...].astype(F32))[0:1]
    zp = jnp.where(is_start, 0.0, zp)
    zn = jnp.where(is_end, 0.0, zn)
    row = lax.broadcasted_iota(jnp.int32, z.shape, 0)
    z_prev = jnp.where(row == 0, zp, pltpu.roll(z, 1, 0))
    z_next = jnp.where(row == tm - 1, zn, pltpu.roll(z, tm - 1, 0))
    yb = bg * (z_prev * wconv_ref[0:1] + z * wconv_ref[1:2] + z_next * wconv_ref[2:3])
    yab_ref[...] = jnp.concatenate([ya, yb], axis=1).astype(BF16)

    cos = cos_ref[...]
    sin = sin_ref[...]
    bd = bd_ref[...]
    q = _rope(_group_rms(p_ref[:, OFF_Q:OFF_K].astype(F32), gq_ref[...], bd), cos, sin)
    q = q * (QK_DIM ** -0.5 * LOG2E)
    lane5 = lax.broadcasted_iota(jnp.int32, q.shape, 1) % 128
    qa_ref[...] = jnp.where(lane5 < QK_DIM, q, 0.0).astype(BF16)
    qb_ref[...] = jnp.where(lane5 >= QK_DIM, q, 0.0).astype(BF16)
    k = _rope(_group_rms(p_ref[:, OFF_K:OFF_V].astype(F32), gk_ref[...], bd), cos, sin)
    kk_ref[...] = k.astype(BF16)


def _mixers(p, cos_t, sin_t, gv, ws_bf, bias_t, wconv, gq, gk, bd):
    tm = TM_MIX
    nt = NR // tm
    hb = tm // 16
    last_hb = NR // 16 - 1
    pos_blocks = L // tm

    def tab_map(i):
        return (jnp.where(i < NL // tm, i % pos_blocks, pos_blocks), 0)

    prev_map_c = lambda i: (jnp.maximum(i * hb - 1, 0), OFF_BC // 256)
    prev_map_x = lambda i: (jnp.maximum(i * hb - 1, 0), OFF_BX // 256)
    next_map_c = lambda i: (jnp.minimum((i + 1) * hb, last_hb), OFF_BC // 256)
    next_map_x = lambda i: (jnp.minimum((i + 1) * hb, last_hb), OFF_BX // 256)
    const2 = lambda i: (0, 0)
    outs = pl.pallas_call(
        _mixers_kernel,
        out_shape=[jax.ShapeDtypeStruct((NR, 2 * A_WIDTH), BF16),
                   jax.ShapeDtypeStruct((NR, 512), BF16),
                   jax.ShapeDtypeStruct((NR, 512), BF16),
                   jax.ShapeDtypeStruct((NR, 512), BF16)],
        grid=(nt,),
        in_specs=[pl.BlockSpec((tm, D_IN), lambda i: (i, 0)),
                  pl.BlockSpec((16, 256), prev_map_c), pl.BlockSpec((16, 256), prev_map_x),
                  pl.BlockSpec((16, 256), next_map_c), pl.BlockSpec((16, 256), next_map_x),
                  pl.BlockSpec((tm, 128), tab_map), pl.BlockSpec((tm, 128), tab_map),
                  pl.BlockSpec((1, A_WIDTH), const2),
                  pl.BlockSpec((A_GROUPS, CHUNK, CHUNK), lambda i: (0, 0, 0)),
                  pl.BlockSpec((CHUNK, A_WIDTH), const2),
                  pl.BlockSpec((8, B_WIDTH), const2),
                  pl.BlockSpec((1, 512), const2), pl.BlockSpec((1, 512), const2),
                  pl.BlockSpec((512, 512), const2)],
        out_specs=[pl.BlockSpec((tm, 512), lambda i: (i, 0))] * 4,
        compiler_params=_cparams(("parallel",), 48),
        name="mixers",
    )(p, p, p, p, p, cos_t, sin_t, gv, ws_bf, bias_t, wconv, gq, gk, bd)
    return outs


def _attn_kernel(lam_ref, qa_ref, qb_ref, *rest, n_seg, coef, tq):
    kv_refs = rest[:2 * n_seg]
    gsub_ref, o_ref, k_scr, vt_scr = rest[2 * n_seg:]

    @pl.when(pl.program_id(2) == 0)
    def _():
        off = 0
        for s in range(n_seg):
            n = kv_refs[s].shape[0]
            k_scr[off:off + n, :] = kv_refs[s][...]
            vt_scr[0:V_DIM, off:off + n] = kv_refs[n_seg + s][...].astype(F32).T.astype(BF16)
            off += n
        ones_row = lax.broadcasted_iota(jnp.int32, (16, off), 0) == 0
        vt_scr[V_DIM:, :] = jnp.where(ones_row, 1.0, 0.0).astype(BF16)

    lam = lam_ref[0]
    shift = lam_ref[1]
    qc = ATT_CHAIN

    def scores(c):
        rows = slice(c * qc, (c + 1) * qc)
        qs = jnp.concatenate([qa_ref[rows, :], qb_ref[rows, :]], axis=0)
        return lax.dot_general(k_scr[...], qs, _DN_T, preferred_element_type=F32)

    def finish(c, pt):
        ot = jnp.dot(vt_scr[...], pt, preferred_element_type=F32)
        inv = 1.0 / ot[V_DIM:V_DIM + 1, :]
        dt = ot[0:V_DIM, :qc] * inv[:, :qc] - ot[0:V_DIM, qc:] * (lam * inv[:, qc:])
        o = dt.T
        ms = jnp.mean(o * o, axis=-1, keepdims=True)
        o_ref[c * qc:(c + 1) * qc, :] = (o * lax.rsqrt(ms + EPS) * gsub_ref[...] * coef).astype(o_ref.dtype)

    @pl.when(lam_ref[2] > 0.5)
    def _():
        for c in range(tq // qc):
            finish(c, jnp.exp2(scores(c) - shift).astype(BF16))

    @pl.when(lam_ref[2] <= 0.5)
    def _():
        sts = [scores(c) for c in range(tq // qc)]
        for c in range(tq // qc):
            st = sts[c]
            finish(c, jnp.exp2(st - jnp.max(st, axis=0, keepdims=True)).astype(BF16))


def _attention(lam, qa, qb, kk, p, gsub, coef, *, ctx_queries):
    vcol = OFF_V // 128
    if ctx_queries:
        tq = CTX
        nq, lk, n_seg = 1, CTX, 1
        q_map = lambda b, h, qi: (NL // tq + b, h)
        kv_specs = [pl.BlockSpec((CTX, 128), lambda b, h, qi: (NL // CTX + b, h)),
                    pl.BlockSpec((CTX, 128), lambda b, h, qi: (NL // CTX + b, vcol + h))]
        kv_args = [kk, p]
        rows = NC
        o_map = lambda b, h, qi: (b, h)
    else:
        tq = TQ
        nq, lk, n_seg = L // tq, CTX + L, 2
        q_map = lambda b, h, qi: (b * (L // tq) + qi, h)
        kv_specs = [pl.BlockSpec((CTX, 128), lambda b, h, qi: (NL // CTX + b, h)),
                    pl.BlockSpec((L, 128), lambda b, h, qi: (b, h)),
                    pl.BlockSpec((CTX, 128), lambda b, h, qi: (NL // CTX + b, vcol + h)),
                    pl.BlockSpec((L, 128), lambda b, h, qi: (b, vcol + h))]
        kv_args = [kk, kk, p, p]
        rows = NL
        o_map = lambda b, h, qi: (b * (L // tq) + qi, h)
    return pl.pallas_call(
        functools.partial(_attn_kernel, n_seg=n_seg, coef=coef, tq=tq),
        out_shape=jax.ShapeDtypeStruct((rows, C_WIDTH), BF16),
        grid=(B, HEADS, nq),
        in_specs=[pl.BlockSpec(memory_space=pltpu.SMEM),
                  pl.BlockSpec((tq, 128), q_map), pl.BlockSpec((tq, 128), q_map)]
                 + kv_specs + [pl.BlockSpec((1, V_DIM), lambda b, h, qi: (0, 0))],
        out_specs=pl.BlockSpec((tq, 128), o_map),
        scratch_shapes=[pltpu.VMEM((lk, 128), BF16), pltpu.VMEM((V_DIM + 16, lk), BF16)],
        compiler_params=_cparams(("parallel", "parallel", "arbitrary"), 56),
        name="attn_ctx" if ctx_queries else "attn_lat",
    )(lam, qa, qb, *kv_args, gsub)


def _pack_rows(t, out_ref, row0=0):
    half = D // 2
    w = pltpu.pack_elementwise([t[:, :half], t[:, half:]], packed_dtype=BF16)
    w = lax.bitcast_convert_type(w, jnp.uint32)
    rows = t.shape[0]
    for j in range(4):
        out_ref[pl.ds(4 * row0 + j, rows, stride=4), :] = w[:, j * 128:(j + 1) * 128]


def _unpack_rows(ref, rows, lead=None, row0=0):
    los, his = [], []
    for j in range(4):
        sl = pl.ds(4 * row0 + j, rows, stride=4)
        w = ref[sl, :] if lead is None else ref[lead, sl, :]
        los.append(pltpu.unpack_elementwise(w, index=0, packed_dtype=BF16, unpacked_dtype=F32))
        his.append(pltpu.unpack_elementwise(w, index=1, packed_dtype=BF16, unpacked_dtype=F32))
    return jnp.concatenate(los, axis=1), jnp.concatenate(his, axis=1)


def _out_router_kernel(xa_ref, xb_ref, yab_ref, yca_ref, ycb_ref, wo_ref, g1_ref, g2n_ref, sh2_ref, sc2_ref, wrh_ref,
                       wrl_ref, br_ref, xo_ref, h2p_ref, idx_ref, gate_ref, rank_ref, cnt_ref, run_ref, *,
                       n_first, n_first_c, t0):
    tm = TM_OUT

    @pl.when(pl.program_id(0) == 0)
    def _():
        run_ref[...] = jnp.zeros_like(run_ref)

    yc = _two_source_rows(yca_ref, ycb_ref, n_first_c, t0)
    y = jnp.dot(jnp.concatenate([yab_ref[...], yc], axis=1), wo_ref[...], preferred_element_type=F32)
    x = _two_source_rows(xa_ref, xb_ref, n_first, t0) + g1_ref[...] * y
    xo_ref[...] = x
    h2 = _rms_mod(x, g2n_ref[...], sc2_ref[...], sh2_ref[...])
    hi = h2.astype(BF16)
    _pack_rows(h2, h2p_ref)
    lo = (h2 - hi.astype(F32)).astype(BF16)
    wh = wrh_ref[...]
    z = (lax.dot_general(wh, hi, _DN_T, preferred_element_type=F32)
         + lax.dot_general(wh, lo, _DN_T, preferred_element_type=F32)
         + lax.dot_general(wrl_ref[...], hi, _DN_T, preferred_element_type=F32))
    scores = jax.nn.sigmoid(z)
    work = scores + br_ref[...]
    eio = lax.broadcasted_iota(jnp.int32, work.shape, 0)
    idxs, sels, hits = [], [], []
    for _ in range(TOP_K):
        m = jnp.max(work, axis=0, keepdims=True)
        idx = jnp.min(jnp.where(work == m, eio, N_EXPERTS), axis=0, keepdims=True)
        hit = eio == idx
        sels.append(jnp.sum(jnp.where(hit, scores, 0.0), axis=0, keepdims=True))
        idxs.append(idx)
        hits.append(hit)
        work = jnp.where(hit, -jnp.inf, work)
    sel = jnp.concatenate(sels, axis=0)
    idx8 = jnp.concatenate(idxs, axis=0)
    gate8 = sel / jnp.sum(sel, axis=0, keepdims=True) * ROUTED_SCALE

    chosen = functools.reduce(jnp.logical_or, hits)
    before = (lax.broadcasted_iota(jnp.int32, (tm, tm), 0) < lax.broadcasted_iota(jnp.int32, (tm, tm), 1))
    prefix = jnp.dot(jnp.where(chosen, 1.0, 0.0).astype(BF16), jnp.where(before, 1.0, 0.0).astype(BF16),
                     preferred_element_type=F32)
    rank_dense = prefix + run_ref[:, 0:1]
    rank8 = jnp.concatenate([jnp.sum(jnp.where(h, rank_dense, 0.0), axis=0, keepdims=True) for h in hits],
                            axis=0).astype(jnp.int32)
    run = run_ref[...] + jnp.sum(jnp.where(chosen, 1.0, 0.0), axis=1, keepdims=True)
    run_ref[...] = run
    cnt_ref[...] = run
    for c in range(tm // 128):
        idx_ref[c] = idx8[:, c * 128:(c + 1) * 128]
        gate_ref[c] = gate8[:, c * 128:(c + 1) * 128]
        rank_ref[c] = rank8[:, c * 128:(c + 1) * 128]


def _out_router(x_first, x_second, yab, yc_first, yc_second, wo_bf, g2n, mod3, wr_hi, wr_lo, br, row0, n_rows):
    tm = TM_OUT
    t0 = row0 // tm
    n_first = x_first.shape[0] // tm
    n_first_c = yc_first.shape[0] // tm
    const2 = lambda i: (0, 0)
    row = lambda i: (i, 0)
    chunk3 = pl.BlockSpec((tm // 128, TOP_K, 128), lambda i: (i, 0, 0))
    nch = n_rows // 128
    return pl.pallas_call(
        functools.partial(_out_router_kernel, n_first=n_first, n_first_c=n_first_c, t0=t0),
        out_shape=[jax.ShapeDtypeStruct((n_rows, D), F32),
                   jax.ShapeDtypeStruct((n_rows * 4, 128), jnp.uint32),
                   jax.ShapeDtypeStruct((nch, TOP_K, 128), jnp.int32),
                   jax.ShapeDtypeStruct((nch, TOP_K, 128), F32),
                   jax.ShapeDtypeStruct((nch, TOP_K, 128), jnp.int32),
                   jax.ShapeDtypeStruct((N_EXPERTS, 128), F32)],
        grid=(n_rows // tm,),
        in_specs=_two_source_specs(tm, n_first, D, t0)
                 + [pl.BlockSpec((tm, 512), lambda i: (i + t0, 0))]
                 + _two_source_specs(tm, n_first_c, C_WIDTH, t0)
                 + [pl.BlockSpec((D, D), const2),
                  _mod_spec(tm, 2, t0),
                  pl.BlockSpec((1, D), const2), _mod_spec(tm, 3, t0), _mod_spec(tm, 4, t0),
                  pl.BlockSpec((N_EXPERTS, D), const2), pl.BlockSpec((N_EXPERTS, D), const2),
                  pl.BlockSpec((N_EXPERTS, 1), const2)],
        out_specs=[pl.BlockSpec((tm, D), row), pl.BlockSpec((tm * 4, 128), row),
                   chunk3, chunk3, chunk3, pl.BlockSpec((N_EXPERTS, 128), const2)],
        scratch_shapes=[pltpu.VMEM((N_EXPERTS, 128), F32)],
        compiler_params=_cparams(("arbitrary",), 48),
        name="out_router",
    )(x_first, x_second, yab, yc_first, yc_second, wo_bf, mod3, g2n.reshape(1, D), mod3, mod3, wr_hi, wr_lo,
      br.reshape(N_EXPERTS, 1))


def _experts_kernel(te_ref, nu_ref, x_ref, wg_ref, wu_ref, wd_ref, y_ref, wgu_s, wd_s):
    i = pl.program_id(0)
    used = i < nu_ref[0]
    new_expert = jnp.logical_or(i == 0, te_ref[i] != te_ref[jnp.maximum(i - 1, 0)])

    @pl.when(jnp.logical_and(used, new_expert))
    def _():
        wgu_s[:, 0:D_EXPERT] = wg_ref[...].astype(BF16)
        wgu_s[:, D_EXPERT:] = wu_ref[...].astype(BF16)
        wd_s[...] = wd_ref[...].astype(BF16)

    @pl.when(used)
    def _():
        rc = TM_X // X_CHAINS
        wgu = wgu_s[...]
        wd = wd_s[...]
        abs_ = []
        for c in range(X_CHAINS):
            x_lo, x_hi = _unpack_rows(x_ref, rc, row0=c * rc)
            x = jnp.concatenate([x_lo.astype(BF16), x_hi.astype(BF16)], axis=1)
            abs_.append(jnp.dot(x, wgu, preferred_element_type=F32))
        for c in range(X_CHAINS):
            a = abs_[c][:, :D_EXPERT]
            hid = (a * jax.nn.sigmoid(a) * abs_[c][:, D_EXPERT:]).astype(BF16)
            _pack_rows(jnp.dot(hid, wd, preferred_element_type=F32), y_ref, row0=c * rc)


def _experts(layer, tile_expert, n_used, x_sorted, w_gate, w_up, w_down):
    tm = TM_X
    r_pad = x_sorted.shape[0] // 4
    nt = r_pad // tm
    row = lambda i, te, nu: (jnp.minimum(i, nu[0] - 1), 0)
    wmap = lambda i, te, nu: (layer, te[i], 0, 0)
    grid_spec = pltpu.PrefetchScalarGridSpec(
        num_scalar_prefetch=2,
        grid=(nt,),
        in_specs=[pl.BlockSpec((tm * 4, 128), row),
                  pl.BlockSpec((None, None, D, D_EXPERT), wmap),
                  pl.BlockSpec((None, None, D, D_EXPERT), wmap),
                  pl.BlockSpec((None, None, D_EXPERT, D), wmap)],
        out_specs=pl.BlockSpec((tm * 4, 128), row),
        scratch_shapes=[pltpu.VMEM((D, 2 * D_EXPERT), BF16), pltpu.VMEM((D_EXPERT, D), BF16)],
    )
    return pl.pallas_call(
        _experts_kernel,
        out_shape=jax.ShapeDtypeStruct((r_pad * 4, 128), jnp.uint32),
        grid_spec=grid_spec,
        compiler_params=_cparams(("arbitrary",), 48),
        name="experts",
    )(tile_expert, n_used, x_sorted, w_gate, w_up, w_down)


SC_CORES = 2
SC_SUBCORES = 16
SC_WORKERS = SC_CORES * SC_SUBCORES
SC_CHUNK = 128


def _sc_mesh():
    return plsc.VectorSubcoreMesh(core_axis_name="c", subcore_axis_name="s")


def _sc_params():
    return pltpu.CompilerParams(use_tc_tiling_on_sc=True)


def _sc_dispatch(h2p, pos3, r_pad):
    nch = pos3.shape[0]
    steps = -(-nch // SC_WORKERS)

    def body(h_hbm, pos_hbm, out_hbm, idx_v, rows_v, sem):
        wid = lax.axis_index("s") * SC_CORES + lax.axis_index("c")

        @pl.loop(0, steps)
        def _(s):
            ch = wid + s * SC_WORKERS

            @pl.when(ch < nch)
            def _():
                pltpu.sync_copy(pos_hbm.at[ch], idx_v)
                pltpu.sync_copy(h_hbm.at[pl.ds(ch * SC_CHUNK, SC_CHUNK)], rows_v)
                copies = [pltpu.async_copy(rows_v, out_hbm.at[idx_v.at[k]], sem) for k in range(TOP_K)]
                for cp in copies:
                    cp.wait()

    return pl.kernel(
        body,
        out_type=jax.ShapeDtypeStruct((r_pad, 4, 128), jnp.uint32),
        mesh=_sc_mesh(),
        scratch_types=[pltpu.VMEM((TOP_K, SC_CHUNK), jnp.int32),
                       pltpu.VMEM((SC_CHUNK, 4, 128), jnp.uint32),
                       pltpu.SemaphoreType.DMA],
        compiler_params=_sc_params(),
        name="sc_dispatch",
    )(h2p, pos3)


def _sc_collect(y_sorted, pos3):
    nch = pos3.shape[0]
    steps = -(-nch // SC_WORKERS)

    half = SC_CHUNK // 2
    units = [(k, hh) for k in range(TOP_K) for hh in range(2)]

    def body(y_hbm, pos_hbm, out_hbm, idx_v, rows_a, rows_b, sem_a, sem_b):
        wid = lax.axis_index("s") * SC_CORES + lax.axis_index("c")
        bufs = (rows_a, rows_b)
        sems = (sem_a, sem_b)

        def gather(u):
            k, hh = units[u]
            return pltpu.async_copy(y_hbm.at[idx_v.at[k, pl.ds(hh * half, half)]], bufs[u % 2], sems[u % 2])

        @pl.loop(0, steps)
        def _(s):
            ch = wid + s * SC_WORKERS

            @pl.when(ch < nch)
            def _():
                pltpu.sync_copy(pos_hbm.at[ch], idx_v)
                pending = gather(0)
                for u, (k, hh) in enumerate(units):
                    nxt = gather(u + 1) if u + 1 < len(units) else None
                    pending.wait()
                    pltpu.sync_copy(bufs[u % 2], out_hbm.at[k, pl.ds(ch * SC_CHUNK + hh * half, half)])
                    pending = nxt

    return pl.kernel(
        body,
        out_type=jax.ShapeDtypeStruct((TOP_K, nch * SC_CHUNK, 4, 128), jnp.uint32),
        mesh=_sc_mesh(),
        scratch_types=[pltpu.VMEM((TOP_K, SC_CHUNK), jnp.int32),
                       pltpu.VMEM((half, 4, 128), jnp.uint32),
                       pltpu.VMEM((half, 4, 128), jnp.uint32),
                       pltpu.SemaphoreType.DMA, pltpu.SemaphoreType.DMA],
        compiler_params=_sc_params(),
        name="sc_collect",
    )(y_sorted, pos3)


def _shared_kernel(x_ref, h2p_ref, yg_ref, gate_ref, wg_ref, wu_ref, wd_ref, g2_ref, o_ref):
    h_lo, h_hi = _unpack_rows(h2p_ref, TM_F)
    h = jnp.concatenate([h_lo.astype(BF16), h_hi.astype(BF16)], axis=1)
    a = jnp.dot(h, wg_ref[...], preferred_element_type=F32)
    b = jnp.dot(h, wu_ref[...], preferred_element_type=F32)
    hid = (a * jax.nn.sigmoid(a) * b).astype(BF16)
    f = jnp.dot(hid, wd_ref[...], preferred_element_type=F32)
    gate = gate_ref[...]
    f_lo = f[:, :D // 2]
    f_hi = f[:, D // 2:]
    for k in range(TOP_K):
        y_lo, y_hi = _unpack_rows(yg_ref, TM_F, lead=k)
        f_lo = f_lo + gate[:, k:k + 1] * y_lo
        f_hi = f_hi + gate[:, k:k + 1] * y_hi
    o_ref[...] = x_ref[...] + g2_ref[...] * jnp.concatenate([f_lo, f_hi], axis=1)


def _shared_residual(xa, h2p, yg, gates, wsg_bf, wsu_bf, wsd_bf, mod3, row0, n_rows):
    tm = TM_F
    t0 = row0 // tm
    row = lambda i: (i, 0)
    const2 = lambda i: (0, 0)
    return pl.pallas_call(
        _shared_kernel,
        out_shape=jax.ShapeDtypeStruct((n_rows, D), F32),
        grid=(n_rows // tm,),
        in_specs=[pl.BlockSpec((tm, D), row), pl.BlockSpec((tm * 4, 128), row),
                  pl.BlockSpec((TOP_K, tm * 4, 128), lambda i: (0, i, 0)),
                  pl.BlockSpec((tm, TOP_K), row),
                  pl.BlockSpec((D, D_SHARED), const2), pl.BlockSpec((D, D_SHARED), const2),
                  pl.BlockSpec((D_SHARED, D), const2), _mod_spec(tm, 5, t0)],
        out_specs=pl.BlockSpec((tm, D), row),
        compiler_params=_cparams(("parallel",), 48),
        name="shared_residual",
    )(xa, h2p, yg, gates, wsg_bf, wsu_bf, wsd_bf, mod3)


def _positions_kernel(offs_ref, idx_ref, rank_ref, pos_ref):
    idx = idx_ref[...]
    base = jnp.zeros_like(idx)
    for e in range(N_EXPERTS):
        base = jnp.where(idx == e, offs_ref[e], base)
    pos_ref[...] = rank_ref[...] + base


def _positions(offs, idx3, rank3):
    nch = idx3.shape[0]
    cb = 8
    spec = pl.BlockSpec((cb, TOP_K, 128), lambda i, offs: (i, 0, 0))
    return pl.pallas_call(
        _positions_kernel,
        out_shape=jax.ShapeDtypeStruct((nch, TOP_K, 128), jnp.int32),
        grid_spec=pltpu.PrefetchScalarGridSpec(num_scalar_prefetch=1, grid=(nch // cb,),
                                               in_specs=[spec, spec], out_specs=spec),
        compiler_params=_cparams(("parallel",)),
        name="positions",
    )(offs, idx3, rank3)


def _route_positions(idx3, rank3, counts, n_rows):
    tm = TM_X
    counts = counts.astype(jnp.int32)
    padded = ((counts + tm - 1) // tm) * tm
    ends = jnp.cumsum(padded)
    offs = ends - padded
    pos3 = _positions(offs.astype(jnp.int32), idx3, rank3)
    r_pad = n_rows * TOP_K + N_EXPERTS * tm
    nt = r_pad // tm
    tile_ids = jnp.arange(nt, dtype=jnp.int32)
    tile_expert = jnp.sum((ends // tm)[None, :] <= tile_ids[:, None], axis=1)
    tile_expert = jnp.minimum(tile_expert, N_EXPERTS - 1).astype(jnp.int32)
    n_used = (ends[-1] // tm).astype(jnp.int32).reshape(1)
    return pos3, tile_expert, n_used, r_pad


def _q_perm():
    perm = np.empty(512, np.int32)
    for h in range(HEADS):
        for half in range(2):
            for d in range(QK_DIM):
                perm[h * 128 + half * QK_DIM + d] = half * HEADS * QK_DIM + h * QK_DIM + d
    return perm


def _rope_tables():
    t = jnp.arange(L)
    row = (t // GRID_W).astype(F32)
    col = (t % GRID_W).astype(F32)
    n_freq = QK_DIM // 4
    inv = ROPE_BASE ** (-jnp.arange(n_freq, dtype=F32) / n_freq)
    ar = row[:, None] * inv
    ac = col[:, None] * inv
    cos64 = jnp.concatenate([jnp.cos(ar), jnp.cos(ar), jnp.cos(ac), jnp.cos(ac)], axis=1)
    sin64 = jnp.concatenate([-jnp.sin(ar), jnp.sin(ar), -jnp.sin(ac), jnp.sin(ac)], axis=1)
    cos_t = jnp.concatenate([jnp.tile(cos64, (1, 2)), jnp.ones((TM_MIX, 128), F32)], axis=0)
    sin_t = jnp.concatenate([jnp.tile(sin64, (1, 2)), jnp.zeros((TM_MIX, 128), F32)], axis=0)
    return cos_t, sin_t


def _split_bf16(w):
    hi = w.astype(BF16)
    return hi, (w - hi.astype(F32)).astype(BF16)


def kernel(x, c, ctx, c_ctx, w_ada, b_ada, g_norm1, g_norm2, w_in, w_out, g_v, w_s, b_s, w_conv, g_q, g_k,
           lam_q1, lam_k1, lam_q2, lam_k2, g_sub, w_router, b_router, w_gate, w_up, w_down,
           ws_gate, ws_up, ws_down):
    src = (x.reshape(NL, D), ctx.reshape(NC, D))
    cc = jnp.concatenate([c, c_ctx[None, :], jnp.zeros((MOD_ROWS - B - 1, D), F32)], axis=0)
    mod = _ada(cc, w_ada, b_ada)
    cos_t, sin_t = _rope_tables()
    perm = _q_perm()
    col_perm = np.concatenate([np.arange(OFF_Q), OFF_Q + perm, OFF_K + perm, np.arange(OFF_V, D_IN)])
    bd = jnp.asarray(np.kron(np.eye(8, dtype=np.float32), np.full((64, 64), 1.0 / 64, np.float32)), BF16)

    for l in range(DEPTH):
        last = l == DEPTH - 1
        lam_init = 0.8 - 0.6 * math.exp(-0.3 * l)
        lam = (jnp.exp(jnp.sum(lam_q1[l] * lam_k1[l])) - jnp.exp(jnp.sum(lam_q2[l] * lam_k2[l])) + lam_init)
        bound = (QK_DIM * jnp.max(jnp.abs(g_q[l])) * jnp.max(jnp.abs(g_k[l]))
                 * (QK_DIM ** -0.5 * LOG2E) * ATT_BOUND_MARGIN)
        use_bound = (2.0 * bound < ATT_MAX_SHIFT_RANGE).astype(F32)
        lam = jnp.stack([lam, bound, use_bound]).astype(F32)
        mod3 = mod[l].reshape(MOD_ROWS, 1, 6 * D)
        w_in_bf = w_in[l][:, col_perm].astype(BF16)
        p = _in_proj(src[0], src[1], g_norm1[l], mod3, w_in_bf)
        bias_t = jnp.repeat(b_s[l].T, A_GD, axis=1)
        wconv = jnp.concatenate([w_conv[l], jnp.zeros((5, B_WIDTH), F32)], axis=0)
        yab, qa, qb, kk = _mixers(p, cos_t, sin_t, g_v[l].reshape(1, A_WIDTH), w_s[l].astype(BF16), bias_t, wconv,
                                  jnp.tile(g_q[l], 8).reshape(1, 512), jnp.tile(g_k[l], 8).reshape(1, 512), bd)
        gsub = g_sub[l].reshape(1, V_DIM)
        coef = 1.0 - lam_init
        n_rows = NL if last else NR
        yc = _attention(lam, qa, qb, kk, p, gsub, coef, ctx_queries=False)
        yc_ctx = yc if last else _attention(lam, qa, qb, kk, p, gsub, coef, ctx_queries=True)
        wr_hi, wr_lo = _split_bf16(w_router[l].T)
        wo_bf = w_out[l].astype(BF16)
        ws_bf = (ws_gate[l].astype(BF16), ws_up[l].astype(BF16), ws_down[l].astype(BF16))
        nh = n_rows // MOE_GROUPS
        outs = []
        for h in range(MOE_GROUPS):
            xh, h2p, idx3, gate3, rank3, counts = _out_router(
                src[0], src[1], yab, yc, yc_ctx, wo_bf, g_norm2[l], mod3, wr_hi, wr_lo, b_router[l], h * nh, nh)
            pos3, tile_expert, n_used, r_pad = _route_positions(idx3, rank3, counts[:, 0], nh)
            x_sorted = _sc_dispatch(h2p.reshape(nh, 4, 128), pos3, r_pad)
            y_sorted = _experts(l, tile_expert, n_used, x_sorted.reshape(r_pad * 4, 128), w_gate, w_up, w_down)
            yg = _sc_collect(y_sorted.reshape(r_pad, 4, 128), pos3)
            gates = gate3.transpose(0, 2, 1).reshape(nh, TOP_K)
            outs.append(_shared_residual(xh, h2p, yg.reshape(TOP_K, nh * 4, 128), gates, *ws_bf, mod3, h * nh, nh))
        src = (outs[0], outs[-1])
    return jnp.concatenate(outs, axis=0).reshape(B, L, D)
```

```python
import functools
import math

import numpy as np
import jax
import jax.numpy as jnp
from jax import lax
from jax.experimental import pallas as pl
from jax.experimental.pallas import tpu as pltpu
from jax.experimental.pallas import tpu_sc as plsc

F32 = jnp.float32
BF16 = jnp.bfloat16

D = 1024
B = 8
L = 2048
DEPTH = 2
GRID_W = 64
CTX = 256
A_WIDTH = 256
A_GROUPS = 4
A_GD = 64
CHUNK = 128
B_WIDTH = 256
C_WIDTH = 512
HEADS = 4
V_DIM = 128
QK_DIM = 64
ROPE_BASE = 10000.0
OFF_BB = 512
OFF_BC = 768
OFF_BX = 1024
OFF_Q = 1280
OFF_K = 1792
OFF_V = 2304
D_IN = 2816
N_EXPERTS = 64
TOP_K = 8
D_EXPERT = 256
D_SHARED = 256
ROUTED_SCALE = 2.5
EPS = 1e-6

NL = B * L
NC = B * CTX
NR = NL + NC
MOD_ROWS = 16
LOG2E = 1.4426950408889634

TM_IN = 512
TQ = 1024
ATT_CHAIN = 256
ATT_BOUND_MARGIN = 1.02
ATT_MAX_SHIFT_RANGE = 100.0
TM_OUT = 512
TM_X = 1024
X_CHAINS = 2
MOE_GROUPS = 1
TM_F = 256

_DN_T = (((1,), (1,)), ((), ()))


def _cparams(sem, vmem_mb=None):
    kw = dict(dimension_semantics=sem)
    if vmem_mb is not None:
        kw["vmem_limit_bytes"] = vmem_mb * 1024 * 1024
    return pltpu.CompilerParams(**kw)


def _mod_row(i, tm):
    return jnp.where(i < NL // tm, i // (L // tm), B)


def _mod_spec(tm, chunk, t0=0):
    return pl.BlockSpec((None, 1, D), lambda i: (_mod_row(i + t0, tm), 0, chunk))


def _ada_kernel(c_ref, w_ref, b_ref, o_ref):
    c = c_ref[...]
    cs = c * jax.nn.sigmoid(c)
    o_ref[...] = jnp.dot(cs, w_ref[...], preferred_element_type=F32,
                         precision=lax.Precision.HIGHEST) + b_ref[...]


def _ada(cc, w_ada, b_ada):
    nb = 6
    return pl.pallas_call(
        _ada_kernel,
        out_shape=jax.ShapeDtypeStruct((DEPTH, MOD_ROWS, 6 * D), F32),
        grid=(DEPTH, nb),
        in_specs=[pl.BlockSpec((MOD_ROWS, D), lambda l, j: (0, 0)),
                  pl.BlockSpec((None, D, D), lambda l, j: (l, 0, j)),
                  pl.BlockSpec((None, 1, D), lambda l, j: (l, 0, j))],
        out_specs=pl.BlockSpec((None, MOD_ROWS, D), lambda l, j: (l, 0, j)),
        compiler_params=_cparams(("arbitrary", "arbitrary"), 40),
        name="ada_mod",
    )(cc, w_ada, b_ada.reshape(DEPTH, 1, 6 * D))


def _rms_mod(x, g, sc, sh):
    ms = jnp.mean(x * x, axis=-1, keepdims=True)
    return x * lax.rsqrt(ms + EPS) * g * (1.0 + sc) + sh


def _two_source_specs(tm, n_first, width=D, t0=0):
    return [pl.BlockSpec((tm, width), lambda i: (jnp.minimum(i + t0, n_first - 1), 0)),
            pl.BlockSpec((tm, width), lambda i: (jnp.maximum(i + t0 - n_first, 0), 0))]


def _two_source_rows(a_ref, b_ref, n_first, t0=0):
    return jnp.where(pl.program_id(0) + t0 < n_first, a_ref[...], b_ref[...])


def _group_rms(t, g, bd):
    sq = t * t
    hi = sq.astype(BF16)
    lo = (sq - hi.astype(F32)).astype(BF16)
    ms = (jnp.dot(hi, bd, preferred_element_type=F32) + jnp.dot(lo, bd, preferred_element_type=F32))
    return t * lax.rsqrt(ms + EPS) * g


def _rope(t, cos, sin):
    w = t.shape[1]
    lane = lax.broadcasted_iota(jnp.int32, t.shape, 1)
    first = (lane % 32) < 16
    partner = jnp.where(first, pltpu.roll(t, w - 16, 1), pltpu.roll(t, 16, 1))
    cos4 = jnp.concatenate([cos] * (w // 128), axis=1)
    sin4 = jnp.concatenate([sin] * (w // 128), axis=1)
    return t * cos4 + partner * sin4


def _in_mix_kernel(xa_ref, xb_ref, pa_ref, pb_ref, na_ref, nb_ref, g_ref, sh_ref, sc_ref, w_ref,
                   cos_ref, sin_ref, gv_ref, ws_ref, bias_ref, wconv_ref, gq_ref, gk_ref, bd_ref,
                   yab_ref, qa_ref, qb_ref, kk_ref, v_ref, *, n_first):
    tm = TM_IN
    i = pl.program_id(0)
    tiles_per_seq = L // tm
    is_lat = i < NL // tm
    is_start = jnp.logical_or(jnp.logical_not(is_lat), i % tiles_per_seq == 0)
    is_end = jnp.logical_or(jnp.logical_not(is_lat), i % tiles_per_seq == tiles_per_seq - 1)
    first = i < n_first
    g, sc, sh = g_ref[...], sc_ref[...], sh_ref[...]

    h = _rms_mod(jnp.where(first, xa_ref[...], xb_ref[...]), g, sc, sh)
    p = jnp.dot(h.astype(BF16), w_ref[...], preferred_element_type=F32)
    v_ref[...] = p[:, OFF_V:].astype(BF16)
    halo = jnp.concatenate([jnp.where(first, pa_ref[...], pb_ref[...]),
                            jnp.where(first, na_ref[...], nb_ref[...])], axis=0)
    ph = jnp.dot(_rms_mod(halo, g, sc, sh).astype(BF16), w_ref[:, OFF_BC:OFF_Q], preferred_element_type=F32)
    zh = ph[:, :B_WIDTH] * ph[:, B_WIDTH:]
    zp = jnp.where(is_start, 0.0, zh[15:16])
    zn = jnp.where(is_end, 0.0, zh[16:17])

    uv = p[:, 0:2 * A_WIDTH]
    uv = 0.5 * uv * (1.0 + lax.erf(uv * (2.0 ** -0.5)))
    u = uv[:, :A_WIDTH]
    v = uv[:, A_WIDTH:]
    ms = jnp.mean(v * v, axis=-1, keepdims=True)
    vb = (v * lax.rsqrt(ms + EPS) * gv_ref[...]).astype(BF16)
    lane = lax.broadcasted_iota(jnp.int32, (CHUNK, 128), 1)
    mixes = []
    for c in range(tm // CHUNK):
        vc = vb[c * CHUNK:(c + 1) * CHUNK]
        halves = []
        for j in range(2):
            vj = vc[:, j * 128:(j + 1) * 128]
            m0 = jnp.dot(ws_ref[2 * j], vj, preferred_element_type=F32)
            m1 = jnp.dot(ws_ref[2 * j + 1], vj, preferred_element_type=F32)
            halves.append(jnp.where(lane < A_GD, m0, m1))
        mixes.append(jnp.concatenate(halves, axis=1) + bias_ref[...])
    ya = u * jnp.concatenate(mixes, axis=0)

    bg = p[:, OFF_BB:OFF_BC]
    z = p[:, OFF_BC:OFF_BX] * p[:, OFF_BX:OFF_Q]
    row = lax.broadcasted_iota(jnp.int32, z.shape, 0)
    inner = jnp.logical_not(is_lat)
    z_prev = jnp.where(row == 0, zp, pltpu.roll(z, 1, 0))
    z_prev = jnp.where(jnp.logical_and(inner, row % CTX == 0), 0.0, z_prev)
    z_next = jnp.where(row == tm - 1, zn, pltpu.roll(z, tm - 1, 0))
    z_next = jnp.where(jnp.logical_and(inner, row % CTX == CTX - 1), 0.0, z_next)
    yb = bg * (z_prev * wconv_ref[0:1] + z * wconv_ref[1:2] + z_next * wconv_ref[2:3])
    yab_ref[...] = jnp.concatenate([ya, yb], axis=1).astype(BF16)

    cos = cos_ref[...]
    sin = sin_ref[...]
    bd = bd_ref[...]
    q = _rope(_group_rms(p[:, OFF_Q:OFF_K], gq_ref[...], bd), cos, sin)
    q = q * (QK_DIM ** -0.5 * LOG2E)
    lane5 = lax.broadcasted_iota(jnp.int32, q.shape, 1) % 128
    qa_ref[...] = jnp.where(lane5 < QK_DIM, q, 0.0).astype(BF16)
    qb_ref[...] = jnp.where(lane5 >= QK_DIM, q, 0.0).astype(BF16)
    k = _rope(_group_rms(p[:, OFF_K:OFF_V], gk_ref[...], bd), cos, sin)
    kk_ref[...] = k.astype(BF16)


def _in_mix(x_first, x_second, g, mod3, w_bf, cos_t, sin_t, gv, ws_bf, bias_t, wconv, gq, gk, bd):
    tm = TM_IN
    n_first = x_first.shape[0] // tm
    hb = tm // 16
    nhb_first = x_first.shape[0] // 16
    nhb_second = x_second.shape[0] // 16
    pos_blocks = L // tm

    def tab_map(i):
        return (jnp.where(i < NL // tm, i % pos_blocks, pos_blocks), 0)

    def halo_specs(shift):
        blk = lambda i: (i * tm + shift) // 16
        return [pl.BlockSpec((16, D), lambda i: (jnp.clip(blk(i), 0, nhb_first - 1), 0)),
                pl.BlockSpec((16, D), lambda i: (jnp.clip(blk(i) - nhb_first, 0, nhb_second - 1), 0))]

    const2 = lambda i: (0, 0)
    row512 = pl.BlockSpec((tm, 512), lambda i: (i, 0))
    return pl.pallas_call(
        functools.partial(_in_mix_kernel, n_first=n_first),
        out_shape=[jax.ShapeDtypeStruct((NR, 512), BF16)] * 5,
        grid=(NR // tm,),
        in_specs=_two_source_specs(tm, n_first) + halo_specs(-1) + halo_specs(tm)
                 + [pl.BlockSpec((1, D), const2), _mod_spec(tm, 0), _mod_spec(tm, 1),
                    pl.BlockSpec((D, D_IN), const2),
                    pl.BlockSpec((tm, 128), tab_map), pl.BlockSpec((tm, 128), tab_map),
                    pl.BlockSpec((1, A_WIDTH), const2),
                    pl.BlockSpec((A_GROUPS, CHUNK, CHUNK), lambda i: (0, 0, 0)),
                    pl.BlockSpec((CHUNK, A_WIDTH), const2),
                    pl.BlockSpec((8, B_WIDTH), const2),
                    pl.BlockSpec((1, 512), const2), pl.BlockSpec((1, 512), const2),
                    pl.BlockSpec((512, 512), const2)],
        out_specs=[row512] * 5,
        compiler_params=_cparams(("parallel",), 56),
        name="in_mix",
    )(x_first, x_second, x_first, x_second, x_first, x_second, g.reshape(1, D), mod3, mod3, w_bf,
      cos_t, sin_t, gv, ws_bf, bias_t, wconv, gq, gk, bd)


def _attn_kernel(lam_ref, qa_ref, qb_ref, *rest, n_seg, coef, tq):
    kv_refs = rest[:2 * n_seg]
    gsub_ref, o_ref, k_scr, vt_scr = rest[2 * n_seg:]

    @pl.when(pl.program_id(2) == 0)
    def _():
        off = 0
        for s in range(n_seg):
            n = kv_refs[s].shape[0]
            k_scr[off:off + n, :] = kv_refs[s][...]
            vt_scr[0:V_DIM, off:off + n] = kv_refs[n_seg + s][...].astype(F32).T.astype(BF16)
            off += n
        ones_row = lax.broadcasted_iota(jnp.int32, (16, off), 0) == 0
        vt_scr[V_DIM:, :] = jnp.where(ones_row, 1.0, 0.0).astype(BF16)

    lam = lam_ref[0]
    shift = lam_ref[1]
    qc = ATT_CHAIN

    def scores(c):
        rows = slice(c * qc, (c + 1) * qc)
        qs = jnp.concatenate([qa_ref[rows, :], qb_ref[rows, :]], axis=0)
        return lax.dot_general(k_scr[...], qs, _DN_T, preferred_element_type=F32)

    def finish(c, pt):
        ot = jnp.dot(vt_scr[...], pt, preferred_element_type=F32)
        inv = 1.0 / ot[V_DIM:V_DIM + 1, :]
        dt = ot[0:V_DIM, :qc] * inv[:, :qc] - ot[0:V_DIM, qc:] * (lam * inv[:, qc:])
        o = dt.T
        ms = jnp.mean(o * o, axis=-1, keepdims=True)
        o_ref[c * qc:(c + 1) * qc, :] = (o * lax.rsqrt(ms + EPS) * gsub_ref[...] * coef).astype(o_ref.dtype)

    @pl.when(lam_ref[2] > 0.5)
    def _():
        for c in range(tq // qc):
            finish(c, jnp.exp2(scores(c) - shift).astype(BF16))

    @pl.when(lam_ref[2] <= 0.5)
    def _():
        sts = [scores(c) for c in range(tq // qc)]
        for c in range(tq // qc):
            st = sts[c]
            finish(c, jnp.exp2(st - jnp.max(st, axis=0, keepdims=True)).astype(BF16))


def _attention(lam, qa, qb, kk, v, gsub, coef, *, ctx_queries):
    if ctx_queries:
        tq = CTX
        nq, lk, n_seg = 1, CTX, 1
        q_map = lambda b, h, qi: (NL // tq + b, h)
        kv_specs = [pl.BlockSpec((CTX, 128), lambda b, h, qi: (NL // CTX + b, h)),
                    pl.BlockSpec((CTX, 128), lambda b, h, qi: (NL // CTX + b, h))]
        kv_args = [kk, v]
        rows = NC
        o_map = lambda b, h, qi: (b, h)
    else:
        tq = TQ
        nq, lk, n_seg = L // tq, CTX + L, 2
        q_map = lambda b, h, qi: (b * (L // tq) + qi, h)
        kv_specs = [pl.BlockSpec((CTX, 128), lambda b, h, qi: (NL // CTX + b, h)),
                    pl.BlockSpec((L, 128), lambda b, h, qi: (b, h)),
                    pl.BlockSpec((CTX, 128), lambda b, h, qi: (NL // CTX + b, h)),
                    pl.BlockSpec((L, 128), lambda b, h, qi: (b, h))]
        kv_args = [kk, kk, v, v]
        rows = NL
        o_map = lambda b, h, qi: (b * (L // tq) + qi, h)
    return pl.pallas_call(
        functools.partial(_attn_kernel, n_seg=n_seg, coef=coef, tq=tq),
        out_shape=jax.ShapeDtypeStruct((rows, C_WIDTH), BF16),
        grid=(B, HEADS, nq),
        in_specs=[pl.BlockSpec(memory_space=pltpu.SMEM),
                  pl.BlockSpec((tq, 128), q_map), pl.BlockSpec((tq, 128), q_map)]
                 + kv_specs + [pl.BlockSpec((1, V_DIM), lambda b, h, qi: (0, 0))],
        out_specs=pl.BlockSpec((tq, 128), o_map),
        scratch_shapes=[pltpu.VMEM((lk, 128), BF16), pltpu.VMEM((V_DIM + 16, lk), BF16)],
        compiler_params=_cparams(("parallel", "parallel", "arbitrary"), 56),
        name="attn_ctx" if ctx_queries else "attn_lat",
    )(lam, qa, qb, *kv_args, gsub)


def _pack_rows(t, out_ref, row0=0):
    half = D // 2
    w = pltpu.pack_elementwise([t[:, :half], t[:, half:]], packed_dtype=BF16)
    w = lax.bitcast_convert_type(w, jnp.uint32)
    rows = t.shape[0]
    for j in range(4):
        out_ref[pl.ds(4 * row0 + j, rows, stride=4), :] = w[:, j * 128:(j + 1) * 128]


def _unpack_rows(ref, rows, lead=None, row0=0):
    los, his = [], []
    for j in range(4):
        sl = pl.ds(4 * row0 + j, rows, stride=4)
        w = ref[sl, :] if lead is None else ref[lead, sl, :]
        los.append(pltpu.unpack_elementwise(w, index=0, packed_dtype=BF16, unpacked_dtype=F32))
        his.append(pltpu.unpack_elementwise(w, index=1, packed_dtype=BF16, unpacked_dtype=F32))
    return jnp.concatenate(los, axis=1), jnp.concatenate(his, axis=1)


def _out_router_kernel(xa_ref, xb_ref, yab_ref, yca_ref, ycb_ref, wo_ref, g1_ref, g2n_ref, sh2_ref, sc2_ref, wrh_ref,
                       wrl_ref, br_ref, xo_ref, h2p_ref, idx_ref, gate_ref, rank_ref, cnt_ref, run_ref, *,
                       n_first, n_first_c, t0):
    tm = TM_OUT

    @pl.when(pl.program_id(0) == 0)
    def _():
        run_ref[...] = jnp.zeros_like(run_ref)

    yc = _two_source_rows(yca_ref, ycb_ref, n_first_c, t0)
    y = jnp.dot(jnp.concatenate([yab_ref[...], yc], axis=1), wo_ref[...], preferred_element_type=F32)
    x = _two_source_rows(xa_ref, xb_ref, n_first, t0) + g1_ref[...] * y
    xo_ref[...] = x
    h2 = _rms_mod(x, g2n_ref[...], sc2_ref[...], sh2_ref[...])
    hi = h2.astype(BF16)
    _pack_rows(h2, h2p_ref)
    lo = (h2 - hi.astype(F32)).astype(BF16)
    wh = wrh_ref[...]
    z = (lax.dot_general(wh, hi, _DN_T, preferred_element_type=F32)
         + lax.dot_general(wh, lo, _DN_T, preferred_element_type=F32)
         + lax.dot_general(wrl_ref[...], hi, _DN_T, preferred_element_type=F32))
    scores = jax.nn.sigmoid(z)
    work = scores + br_ref[...]
    eio = lax.broadcasted_iota(jnp.int32, work.shape, 0)
    idxs, sels, hits = [], [], []
    for _ in range(TOP_K):
        m = jnp.max(work, axis=0, keepdims=True)
        idx = jnp.min(jnp.where(work == m, eio, N_EXPERTS), axis=0, keepdims=True)
        hit = eio == idx
        sels.append(jnp.sum(jnp.where(hit, scores, 0.0), axis=0, keepdims=True))
        idxs.append(idx)
        hits.append(hit)
        work = jnp.where(hit, -jnp.inf, work)
    sel = jnp.concatenate(sels, axis=0)
    idx8 = jnp.concatenate(idxs, axis=0)
    gate8 = sel / jnp.sum(sel, axis=0, keepdims=True) * ROUTED_SCALE

    chosen = functools.reduce(jnp.logical_or, hits)
    before = (lax.broadcasted_iota(jnp.int32, (tm, tm), 0) < lax.broadcasted_iota(jnp.int32, (tm, tm), 1))
    prefix = jnp.dot(jnp.where(chosen, 1.0, 0.0).astype(BF16), jnp.where(before, 1.0, 0.0).astype(BF16),
                     preferred_element_type=F32)
    rank_dense = prefix + run_ref[:, 0:1]
    rank8 = jnp.concatenate([jnp.sum(jnp.where(h, rank_dense, 0.0), axis=0, keepdims=True) for h in hits],
                            axis=0).astype(jnp.int32)
    run = run_ref[...] + jnp.sum(jnp.where(chosen, 1.0, 0.0), axis=1, keepdims=True)
    run_ref[...] = run
    cnt_ref[...] = run
    for c in range(tm // 128):
        idx_ref[c] = idx8[:, c * 128:(c + 1) * 128]
        gate_ref[c] = gate8[:, c * 128:(c + 1) * 128]
        rank_ref[c] = rank8[:, c * 128:(c + 1) * 128]


def _out_router(x_first, x_second, yab, yc_first, yc_second, wo_bf, g2n, mod3, wr_hi, wr_lo, br, row0, n_rows):
    tm = TM_OUT
    t0 = row0 // tm
    n_first = x_first.shape[0] // tm
    n_first_c = yc_first.shape[0] // tm
    const2 = lambda i: (0, 0)
    row = lambda i: (i, 0)
    chunk3 = pl.BlockSpec((tm // 128, TOP_K, 128), lambda i: (i, 0, 0))
    nch = n_rows // 128
    return pl.pallas_call(
        functools.partial(_out_router_kernel, n_first=n_first, n_first_c=n_first_c, t0=t0),
        out_shape=[jax.ShapeDtypeStruct((n_rows, D), F32),
                   jax.ShapeDtypeStruct((n_rows * 4, 128), jnp.uint32),
                   jax.ShapeDtypeStruct((nch, TOP_K, 128), jnp.int32),
                   jax.ShapeDtypeStruct((nch, TOP_K, 128), F32),
                   jax.ShapeDtypeStruct((nch, TOP_K, 128), jnp.int32),
                   jax.ShapeDtypeStruct((N_EXPERTS, 128), F32)],
        grid=(n_rows // tm,),
        in_specs=_two_source_specs(tm, n_first, D, t0)
                 + [pl.BlockSpec((tm, 512), lambda i: (i + t0, 0))]
                 + _two_source_specs(tm, n_first_c, C_WIDTH, t0)
                 + [pl.BlockSpec((D, D), const2),
                  _mod_spec(tm, 2, t0),
                  pl.BlockSpec((1, D), const2), _mod_spec(tm, 3, t0), _mod_spec(tm, 4, t0),
                  pl.BlockSpec((N_EXPERTS, D), const2), pl.BlockSpec((N_EXPERTS, D), const2),
                  pl.BlockSpec((N_EXPERTS, 1), const2)],
        out_specs=[pl.BlockSpec((tm, D), row), pl.BlockSpec((tm * 4, 128), row),
                   chunk3, chunk3, chunk3, pl.BlockSpec((N_EXPERTS, 128), const2)],
        scratch_shapes=[pltpu.VMEM((N_EXPERTS, 128), F32)],
        compiler_params=_cparams(("arbitrary",), 48),
        name="out_router",
    )(x_first, x_second, yab, yc_first, yc_second, wo_bf, mod3, g2n.reshape(1, D), mod3, mod3, wr_hi, wr_lo,
      br.reshape(N_EXPERTS, 1))


def _experts_kernel(te_ref, nu_ref, x_ref, wg_ref, wu_ref, wd_ref, y_ref, wgu_s, wd_s):
    i = pl.program_id(0)
    used = i < nu_ref[0]
    new_expert = jnp.logical_or(i == 0, te_ref[i] != te_ref[jnp.maximum(i - 1, 0)])

    @pl.when(jnp.logical_and(used, new_expert))
    def _():
        wgu_s[:, 0:D_EXPERT] = wg_ref[...].astype(BF16)
        wgu_s[:, D_EXPERT:] = wu_ref[...].astype(BF16)
        wd_s[...] = wd_ref[...].astype(BF16)

    @pl.when(used)
    def _():
        rc = TM_X // X_CHAINS
        wgu = wgu_s[...]
        wd = wd_s[...]
        abs_ = []
        for c in range(X_CHAINS):
            x_lo, x_hi = _unpack_rows(x_ref, rc, row0=c * rc)
            x = jnp.concatenate([x_lo.astype(BF16), x_hi.astype(BF16)], axis=1)
            abs_.append(jnp.dot(x, wgu, preferred_element_type=F32))
        for c in range(X_CHAINS):
            a = abs_[c][:, :D_EXPERT]
            hid = (a * jax.nn.sigmoid(a) * abs_[c][:, D_EXPERT:]).astype(BF16)
            _pack_rows(jnp.dot(hid, wd, preferred_element_type=F32), y_ref, row0=c * rc)


def _experts(layer, tile_expert, n_used, x_sorted, w_gate, w_up, w_down):
    tm = TM_X
    r_pad = x_sorted.shape[0] // 4
    nt = r_pad // tm
    row = lambda i, te, nu: (jnp.minimum(i, nu[0] - 1), 0)
    wmap = lambda i, te, nu: (layer, te[i], 0, 0)
    grid_spec = pltpu.PrefetchScalarGridSpec(
        num_scalar_prefetch=2,
        grid=(nt,),
        in_specs=[pl.BlockSpec((tm * 4, 128), row),
                  pl.BlockSpec((None, None, D, D_EXPERT), wmap),
                  pl.BlockSpec((None, None, D, D_EXPERT), wmap),
                  pl.BlockSpec((None, None, D_EXPERT, D), wmap)],
        out_specs=pl.BlockSpec((tm * 4, 128), row),
        scratch_shapes=[pltpu.VMEM((D, 2 * D_EXPERT), BF16), pltpu.VMEM((D_EXPERT, D), BF16)],
    )
    return pl.pallas_call(
        _experts_kernel,
        out_shape=jax.ShapeDtypeStruct((r_pad * 4, 128), jnp.uint32),
        grid_spec=grid_spec,
        compiler_params=_cparams(("arbitrary",), 48),
        name="experts",
    )(tile_expert, n_used, x_sorted, w_gate, w_up, w_down)


SC_CORES = 2
SC_SUBCORES = 16
SC_WORKERS = SC_CORES * SC_SUBCORES
SC_CHUNK = 128


def _sc_mesh():
    return plsc.VectorSubcoreMesh(core_axis_name="c", subcore_axis_name="s")


def _sc_params():
    return pltpu.CompilerParams(use_tc_tiling_on_sc=True)


def _sc_dispatch(h2p, pos3, r_pad):
    nch = pos3.shape[0]
    steps = -(-nch // SC_WORKERS)

    def body(h_hbm, pos_hbm, out_hbm, idx_v, rows_v, sem):
        wid = lax.axis_index("s") * SC_CORES + lax.axis_index("c")

        @pl.loop(0, steps)
        def _(s):
            ch = wid + s * SC_WORKERS

            @pl.when(ch < nch)
            def _():
                pltpu.sync_copy(pos_hbm.at[ch], idx_v)
                pltpu.sync_copy(h_hbm.at[pl.ds(ch * SC_CHUNK, SC_CHUNK)], rows_v)
                copies = [pltpu.async_copy(rows_v, out_hbm.at[idx_v.at[k]], sem) for k in range(TOP_K)]
                for cp in copies:
                    cp.wait()

    return pl.kernel(
        body,
        out_type=jax.ShapeDtypeStruct((r_pad, 4, 128), jnp.uint32),
        mesh=_sc_mesh(),
        scratch_types=[pltpu.VMEM((TOP_K, SC_CHUNK), jnp.int32),
                       pltpu.VMEM((SC_CHUNK, 4, 128), jnp.uint32),
                       pltpu.SemaphoreType.DMA],
        compiler_params=_sc_params(),
        name="sc_dispatch",
    )(h2p, pos3)


def _sc_collect(y_sorted, pos3):
    nch = pos3.shape[0]
    steps = -(-nch // SC_WORKERS)

    half = SC_CHUNK // 2
    units = [(k, hh) for k in range(TOP_K) for hh in range(2)]

    def body(y_hbm, pos_hbm, out_hbm, idx_v, rows_a, rows_b, sem_a, sem_b):
        wid = lax.axis_index("s") * SC_CORES + lax.axis_index("c")
        bufs = (rows_a, rows_b)
        sems = (sem_a, sem_b)

        def gather(u):
            k, hh = units[u]
            return pltpu.async_copy(y_hbm.at[idx_v.at[k, pl.ds(hh * half, half)]], bufs[u % 2], sems[u % 2])

        @pl.loop(0, steps)
        def _(s):
            ch = wid + s * SC_WORKERS

            @pl.when(ch < nch)
            def _():
                pltpu.sync_copy(pos_hbm.at[ch], idx_v)
                pending = gather(0)
                for u, (k, hh) in enumerate(units):
                    nxt = gather(u + 1) if u + 1 < len(units) else None
                    pending.wait()
                    pltpu.sync_copy(bufs[u % 2], out_hbm.at[k, pl.ds(ch * SC_CHUNK + hh * half, half)])
                    pending = nxt

    return pl.kernel(
        body,
        out_type=jax.ShapeDtypeStruct((TOP_K, nch * SC_CHUNK, 4, 128), jnp.uint32),
        mesh=_sc_mesh(),
        scratch_types=[pltpu.VMEM((TOP_K, SC_CHUNK), jnp.int32),
                       pltpu.VMEM((half, 4, 128), jnp.uint32),
                       pltpu.VMEM((half, 4, 128), jnp.uint32),
                       pltpu.SemaphoreType.DMA, pltpu.SemaphoreType.DMA],
        compiler_params=_sc_params(),
        name="sc_collect",
    )(y_sorted, pos3)


def _shared_kernel(x_ref, h2p_ref, yg_ref, gate_ref, wg_ref, wu_ref, wd_ref, g2_ref, o_ref):
    h_lo, h_hi = _unpack_rows(h2p_ref, TM_F)
    h = jnp.concatenate([h_lo.astype(BF16), h_hi.astype(BF16)], axis=1)
    a = jnp.dot(h, wg_ref[...], preferred_element_type=F32)
    b = jnp.dot(h, wu_ref[...], preferred_element_type=F32)
    hid = (a * jax.nn.sigmoid(a) * b).astype(BF16)
    f = jnp.dot(hid, wd_ref[...], preferred_element_type=F32)
    gate = gate_ref[...]
    f_lo = f[:, :D // 2]
    f_hi = f[:, D // 2:]
    for k in range(TOP_K):
        y_lo, y_hi = _unpack_rows(yg_ref, TM_F, lead=k)
        f_lo = f_lo + gate[:, k:k + 1] * y_lo
        f_hi = f_hi + gate[:, k:k + 1] * y_hi
    o_ref[...] = x_ref[...] + g2_ref[...] * jnp.concatenate([f_lo, f_hi], axis=1)


def _shared_residual(xa, h2p, yg, gates, wsg_bf, wsu_bf, wsd_bf, mod3, row0, n_rows):
    tm = TM_F
    t0 = row0 // tm
    row = lambda i: (i, 0)
    const2 = lambda i: (0, 0)
    return pl.pallas_call(
        _shared_kernel,
        out_shape=jax.ShapeDtypeStruct((n_rows, D), F32),
        grid=(n_rows // tm,),
        in_specs=[pl.BlockSpec((tm, D), row), pl.BlockSpec((tm * 4, 128), row),
                  pl.BlockSpec((TOP_K, tm * 4, 128), lambda i: (0, i, 0)),
                  pl.BlockSpec((tm, TOP_K), row),
                  pl.BlockSpec((D, D_SHARED), const2), pl.BlockSpec((D, D_SHARED), const2),
                  pl.BlockSpec((D_SHARED, D), const2), _mod_spec(tm, 5, t0)],
        out_specs=pl.BlockSpec((tm, D), row),
        compiler_params=_cparams(("parallel",), 48),
        name="shared_residual",
    )(xa, h2p, yg, gates, wsg_bf, wsu_bf, wsd_bf, mod3)


def _positions_kernel(offs_ref, idx_ref, rank_ref, pos_ref):
    idx = idx_ref[...]
    base = jnp.zeros_like(idx)
    for e in range(N_EXPERTS):
        base = jnp.where(idx == e, offs_ref[e], base)
    pos_ref[...] = rank_ref[...] + base


def _positions(offs, idx3, rank3):
    nch = idx3.shape[0]
    cb = 8
    spec = pl.BlockSpec((cb, TOP_K, 128), lambda i, offs: (i, 0, 0))
    return pl.pallas_call(
        _positions_kernel,
        out_shape=jax.ShapeDtypeStruct((nch, TOP_K, 128), jnp.int32),
        grid_spec=pltpu.PrefetchScalarGridSpec(num_scalar_prefetch=1, grid=(nch // cb,),
                                               in_specs=[spec, spec], out_specs=spec),
        compiler_params=_cparams(("parallel",)),
        name="positions",
    )(offs, idx3, rank3)


def _route_positions(idx3, rank3, counts, n_rows):
    tm = TM_X
    counts = counts.astype(jnp.int32)
    padded = ((counts + tm - 1) // tm) * tm
    ends = jnp.cumsum(padded)
    offs = ends - padded
    pos3 = _positions(offs.astype(jnp.int32), idx3, rank3)
    r_pad = n_rows * TOP_K + N_EXPERTS * tm
    nt = r_pad // tm
    tile_ids = jnp.arange(nt, dtype=jnp.int32)
    tile_expert = jnp.sum((ends // tm)[None, :] <= tile_ids[:, None], axis=1)
    tile_expert = jnp.minimum(tile_expert, N_EXPERTS - 1).astype(jnp.int32)
    n_used = (ends[-1] // tm).astype(jnp.int32).reshape(1)
    return pos3, tile_expert, n_used, r_pad


def _in_weights(w):
    def regroup(cols):
        return cols.reshape(D, 2, HEADS, QK_DIM).transpose(0, 2, 1, 3).reshape(D, 2 * HEADS * QK_DIM)
    return jnp.concatenate([w[:, :OFF_Q], regroup(w[:, OFF_Q:OFF_K]), regroup(w[:, OFF_K:OFF_V]), w[:, OFF_V:]],
                           axis=1).astype(BF16)


def _rope_tables():
    t = jnp.arange(L)
    row = (t // GRID_W).astype(F32)
    col = (t % GRID_W).astype(F32)
    n_freq = QK_DIM // 4
    inv = ROPE_BASE ** (-jnp.arange(n_freq, dtype=F32) / n_freq)
    ar = row[:, None] * inv
    ac = col[:, None] * inv
    cos64 = jnp.concatenate([jnp.cos(ar), jnp.cos(ar), jnp.cos(ac), jnp.cos(ac)], axis=1)
    sin64 = jnp.concatenate([-jnp.sin(ar), jnp.sin(ar), -jnp.sin(ac), jnp.sin(ac)], axis=1)
    cos_t = jnp.concatenate([jnp.tile(cos64, (1, 2)), jnp.ones((TM_IN, 128), F32)], axis=0)
    sin_t = jnp.concatenate([jnp.tile(sin64, (1, 2)), jnp.zeros((TM_IN, 128), F32)], axis=0)
    return cos_t, sin_t


def _split_bf16(w):
    hi = w.astype(BF16)
    return hi, (w - hi.astype(F32)).astype(BF16)


def kernel(x, c, ctx, c_ctx, w_ada, b_ada, g_norm1, g_norm2, w_in, w_out, g_v, w_s, b_s, w_conv, g_q, g_k,
           lam_q1, lam_k1, lam_q2, lam_k2, g_sub, w_router, b_router, w_gate, w_up, w_down,
           ws_gate, ws_up, ws_down):
    src = (x.reshape(NL, D), ctx.reshape(NC, D))
    cc = jnp.concatenate([c, c_ctx[None, :], jnp.zeros((MOD_ROWS - B - 1, D), F32)], axis=0)
    mod = _ada(cc, w_ada, b_ada)
    cos_t, sin_t = _rope_tables()
    bd = jnp.asarray(np.kron(np.eye(8, dtype=np.float32), np.full((64, 64), 1.0 / 64, np.float32)), BF16)

    for l in range(DEPTH):
        last = l == DEPTH - 1
        lam_init = 0.8 - 0.6 * math.exp(-0.3 * l)
        lam = (jnp.exp(jnp.sum(lam_q1[l] * lam_k1[l])) - jnp.exp(jnp.sum(lam_q2[l] * lam_k2[l])) + lam_init)
        bound = (QK_DIM * jnp.max(jnp.abs(g_q[l])) * jnp.max(jnp.abs(g_k[l]))
                 * (QK_DIM ** -0.5 * LOG2E) * ATT_BOUND_MARGIN)
        use_bound = (2.0 * bound < ATT_MAX_SHIFT_RANGE).astype(F32)
        lam = jnp.stack([lam, bound, use_bound]).astype(F32)
        mod3 = mod[l].reshape(MOD_ROWS, 1, 6 * D)
        w_in_bf = _in_weights(w_in[l])
        bias_t = jnp.repeat(b_s[l].T, A_GD, axis=1)
        wconv = jnp.concatenate([w_conv[l], jnp.zeros((5, B_WIDTH), F32)], axis=0)
        yab, qa, qb, kk, v = _in_mix(src[0], src[1], g_norm1[l], mod3, w_in_bf, cos_t, sin_t,
                                     g_v[l].reshape(1, A_WIDTH), w_s[l].astype(BF16), bias_t, wconv,
                                     jnp.tile(g_q[l], 8).reshape(1, 512), jnp.tile(g_k[l], 8).reshape(1, 512), bd)
        gsub = g_sub[l].reshape(1, V_DIM)
        coef = 1.0 - lam_init
        n_rows = NL if last else NR
        yc = _attention(lam, qa, qb, kk, v, gsub, coef, ctx_queries=False)
        yc_ctx = yc if last else _attention(lam, qa, qb, kk, v, gsub, coef, ctx_queries=True)
        wr_hi, wr_lo = _split_bf16(w_router[l].T)
        wo_bf = w_out[l].astype(BF16)
        ws_bf = (ws_gate[l].astype(BF16), ws_up[l].astype(BF16), ws_down[l].astype(BF16))
        nh = n_rows // MOE_GROUPS
        outs = []
        for h in range(MOE_GROUPS):
            xh, h2p, idx3, gate3, rank3, counts = _out_router(
                src[0], src[1], yab, yc, yc_ctx, wo_bf, g_norm2[l], mod3, wr_hi, wr_lo, b_router[l], h * nh, nh)
            pos3, tile_expert, n_used, r_pad = _route_positions(idx3, rank3, counts[:, 0], nh)
            x_sorted = _sc_dispatch(h2p.reshape(nh, 4, 128), pos3, r_pad)
            y_sorted = _experts(l, tile_expert, n_used, x_sorted.reshape(r_pad * 4, 128), w_gate, w_up, w_down)
            yg = _sc_collect(y_sorted.reshape(r_pad, 4, 128), pos3)
            gates = gate3.transpose(0, 2, 1).reshape(nh, TOP_K)
            outs.append(_shared_residual(xh, h2p, yg.reshape(TOP_K, nh * 4, 128), gates, *ws_bf, mod3, h * nh, nh))
        src = (outs[0], outs[-1])
    return jnp.concatenate(outs, axis=0).reshape(B, L, D)
```

```python
import functools
import math

import numpy as np
import jax
import jax.numpy as jnp
from jax import lax
from jax.experimental import pallas as pl
from jax.experimental.pallas import tpu as pltpu
from jax.experimental.pallas import tpu_sc as plsc

F32 = jnp.float32
BF16 = jnp.bfloat16

D = 1024
B = 8
L = 2048
DEPTH = 2
GRID_W = 64
CTX = 256
A_WIDTH = 256
A_GROUPS = 4
A_GD = 64
CHUNK = 128
B_WIDTH = 256
C_WIDTH = 512
HEADS = 4
V_DIM = 128
QK_DIM = 64
ROPE_BASE = 10000.0
OFF_BB = 512
OFF_BC = 768
OFF_BX = 1024
OFF_Q = 1280
OFF_K = 1792
OFF_V = 2304
D_IN = 2816
N_EXPERTS = 64
TOP_K = 8
D_EXPERT = 256
D_SHARED = 256
ROUTED_SCALE = 2.5
EPS = 1e-6

NL = B * L
NC = B * CTX
NR = NL + NC
MOD_ROWS = 16
LOG2E = 1.4426950408889634

TM_IN = 512
TQ = 1024
ATT_CHAIN = 256
ATT_BOUND_MARGIN = 1.02
ATT_MAX_SHIFT_RANGE = 100.0
TM_OUT = 512
TM_X = 1024
X_CHAINS = 2
X_RING = 3
MOE_GROUPS = 1
TM_F = 256

_DN_T = (((1,), (1,)), ((), ()))


def _cparams(sem, vmem_mb=None):
    kw = dict(dimension_semantics=sem)
    if vmem_mb is not None:
        kw["vmem_limit_bytes"] = vmem_mb * 1024 * 1024
    return pltpu.CompilerParams(**kw)


def _mod_row(i, tm):
    return jnp.where(i < NL // tm, i // (L // tm), B)


def _mod_spec(tm, chunk, t0=0):
    return pl.BlockSpec((None, 1, D), lambda i: (_mod_row(i + t0, tm), 0, chunk))


def _ada_kernel(c_ref, w_ref, b_ref, o_ref):
    c = c_ref[...]
    cs = c * jax.nn.sigmoid(c)
    o_ref[...] = jnp.dot(cs, w_ref[...], preferred_element_type=F32,
                         precision=lax.Precision.HIGHEST) + b_ref[...]


def _ada(cc, w_ada, b_ada):
    nb = 6
    return pl.pallas_call(
        _ada_kernel,
        out_shape=jax.ShapeDtypeStruct((DEPTH, MOD_ROWS, 6 * D), F32),
        grid=(DEPTH, nb),
        in_specs=[pl.BlockSpec((MOD_ROWS, D), lambda l, j: (0, 0)),
                  pl.BlockSpec((None, D, D), lambda l, j: (l, 0, j)),
                  pl.BlockSpec((None, 1, D), lambda l, j: (l, 0, j))],
        out_specs=pl.BlockSpec((None, MOD_ROWS, D), lambda l, j: (l, 0, j)),
        compiler_params=_cparams(("arbitrary", "arbitrary"), 40),
        name="ada_mod",
    )(cc, w_ada, b_ada.reshape(DEPTH, 1, 6 * D))


def _rms_mod(x, g, sc, sh):
    ms = jnp.mean(x * x, axis=-1, keepdims=True)
    return x * lax.rsqrt(ms + EPS) * g * (1.0 + sc) + sh


def _two_source_specs(tm, n_first, width=D, t0=0):
    return [pl.BlockSpec((tm, width), lambda i: (jnp.minimum(i + t0, n_first - 1), 0)),
            pl.BlockSpec((tm, width), lambda i: (jnp.maximum(i + t0 - n_first, 0), 0))]


def _two_source_rows(a_ref, b_ref, n_first, t0=0):
    return jnp.where(pl.program_id(0) + t0 < n_first, a_ref[...], b_ref[...])


def _group_rms(t, g, bd):
    sq = t * t
    hi = sq.astype(BF16)
    lo = (sq - hi.astype(F32)).astype(BF16)
    ms = (jnp.dot(hi, bd, preferred_element_type=F32) + jnp.dot(lo, bd, preferred_element_type=F32))
    return t * lax.rsqrt(ms + EPS) * g


def _rope(t, cos, sin):
    w = t.shape[1]
    lane = lax.broadcasted_iota(jnp.int32, t.shape, 1)
    first = (lane % 32) < 16
    partner = jnp.where(first, pltpu.roll(t, w - 16, 1), pltpu.roll(t, 16, 1))
    cos4 = jnp.concatenate([cos] * (w // 128), axis=1)
    sin4 = jnp.concatenate([sin] * (w // 128), axis=1)
    return t * cos4 + partner * sin4


def _in_mix_kernel(xa_ref, xb_ref, pa_ref, pb_ref, na_ref, nb_ref, g_ref, sh_ref, sc_ref, w_ref,
                   cos_ref, sin_ref, gv_ref, ws_ref, bias_ref, wconv_ref, gq_ref, gk_ref, bd_ref,
                   yab_ref, qa_ref, qb_ref, kk_ref, v_ref, *, n_first):
    tm = TM_IN
    i = pl.program_id(0)
    tiles_per_seq = L // tm
    is_lat = i < NL // tm
    is_start = jnp.logical_or(jnp.logical_not(is_lat), i % tiles_per_seq == 0)
    is_end = jnp.logical_or(jnp.logical_not(is_lat), i % tiles_per_seq == tiles_per_seq - 1)
    first = i < n_first
    g, sc, sh = g_ref[...], sc_ref[...], sh_ref[...]

    h = _rms_mod(jnp.where(first, xa_ref[...], xb_ref[...]), g, sc, sh)
    p = jnp.dot(h.astype(BF16), w_ref[...], preferred_element_type=F32)
    v_ref[...] = p[:, OFF_V:].astype(BF16)
    halo = jnp.concatenate([jnp.where(first, pa_ref[...], pb_ref[...]),
                            jnp.where(first, na_ref[...], nb_ref[...])], axis=0)
    ph = jnp.dot(_rms_mod(halo, g, sc, sh).astype(BF16), w_ref[:, OFF_BC:OFF_Q], preferred_element_type=F32)
    zh = ph[:, :B_WIDTH] * ph[:, B_WIDTH:]
    zp = jnp.where(is_start, 0.0, zh[15:16])
    zn = jnp.where(is_end, 0.0, zh[16:17])

    uv = p[:, 0:2 * A_WIDTH]
    uv = 0.5 * uv * (1.0 + lax.erf(uv * (2.0 ** -0.5)))
    u = uv[:, :A_WIDTH]
    v = uv[:, A_WIDTH:]
    ms = jnp.mean(v * v, axis=-1, keepdims=True)
    vb = (v * lax.rsqrt(ms + EPS) * gv_ref[...]).astype(BF16)
    lane = lax.broadcasted_iota(jnp.int32, (CHUNK, 128), 1)
    mixes = []
    for c in range(tm // CHUNK):
        vc = vb[c * CHUNK:(c + 1) * CHUNK]
        halves = []
        for j in range(2):
            vj = vc[:, j * 128:(j + 1) * 128]
            m0 = jnp.dot(ws_ref[2 * j], vj, preferred_element_type=F32)
            m1 = jnp.dot(ws_ref[2 * j + 1], vj, preferred_element_type=F32)
            halves.append(jnp.where(lane < A_GD, m0, m1))
        mixes.append(jnp.concatenate(halves, axis=1) + bias_ref[...])
    ya = u * jnp.concatenate(mixes, axis=0)

    bg = p[:, OFF_BB:OFF_BC]
    z = p[:, OFF_BC:OFF_BX] * p[:, OFF_BX:OFF_Q]
    row = lax.broadcasted_iota(jnp.int32, z.shape, 0)
    inner = jnp.logical_not(is_lat)
    z_prev = jnp.where(row == 0, zp, pltpu.roll(z, 1, 0))
    z_prev = jnp.where(jnp.logical_and(inner, row % CTX == 0), 0.0, z_prev)
    z_next = jnp.where(row == tm - 1, zn, pltpu.roll(z, tm - 1, 0))
    z_next = jnp.where(jnp.logical_and(inner, row % CTX == CTX - 1), 0.0, z_next)
    yb = bg * (z_prev * wconv_ref[0:1] + z * wconv_ref[1:2] + z_next * wconv_ref[2:3])
    yab_ref[...] = jnp.concatenate([ya, yb], axis=1).astype(BF16)

    cos = cos_ref[...]
    sin = sin_ref[...]
    bd = bd_ref[...]
    q = _rope(_group_rms(p[:, OFF_Q:OFF_K], gq_ref[...], bd), cos, sin)
    q = q * (QK_DIM ** -0.5 * LOG2E)
    lane5 = lax.broadcasted_iota(jnp.int32, q.shape, 1) % 128
    qa_ref[...] = jnp.where(lane5 < QK_DIM, q, 0.0).astype(BF16)
    qb_ref[...] = jnp.where(lane5 >= QK_DIM, q, 0.0).astype(BF16)
    k = _rope(_group_rms(p[:, OFF_K:OFF_V], gk_ref[...], bd), cos, sin)
    kk_ref[...] = k.astype(BF16)


def _in_mix(x_first, x_second, g, mod3, w_bf, cos_t, sin_t, gv, ws_bf, bias_t, wconv, gq, gk, bd):
    tm = TM_IN
    n_first = x_first.shape[0] // tm
    hb = tm // 16
    nhb_first = x_first.shape[0] // 16
    nhb_second = x_second.shape[0] // 16
    pos_blocks = L // tm

    def tab_map(i):
        return (jnp.where(i < NL // tm, i % pos_blocks, pos_blocks), 0)

    def halo_specs(shift):
        blk = lambda i: (i * tm + shift) // 16
        return [pl.BlockSpec((16, D), lambda i: (jnp.clip(blk(i), 0, nhb_first - 1), 0)),
                pl.BlockSpec((16, D), lambda i: (jnp.clip(blk(i) - nhb_first, 0, nhb_second - 1), 0))]

    const2 = lambda i: (0, 0)
    row512 = pl.BlockSpec((tm, 512), lambda i: (i, 0))
    return pl.pallas_call(
        functools.partial(_in_mix_kernel, n_first=n_first),
        out_shape=[jax.ShapeDtypeStruct((NR, 512), BF16)] * 5,
        grid=(NR // tm,),
        in_specs=_two_source_specs(tm, n_first) + halo_specs(-1) + halo_specs(tm)
                 + [pl.BlockSpec((1, D), const2), _mod_spec(tm, 0), _mod_spec(tm, 1),
                    pl.BlockSpec((D, D_IN), const2),
                    pl.BlockSpec((tm, 128), tab_map), pl.BlockSpec((tm, 128), tab_map),
                    pl.BlockSpec((1, A_WIDTH), const2),
                    pl.BlockSpec((A_GROUPS, CHUNK, CHUNK), lambda i: (0, 0, 0)),
                    pl.BlockSpec((CHUNK, A_WIDTH), const2),
                    pl.BlockSpec((8, B_WIDTH), const2),
                    pl.BlockSpec((1, 512), const2), pl.BlockSpec((1, 512), const2),
                    pl.BlockSpec((512, 512), const2)],
        out_specs=[row512] * 5,
        compiler_params=_cparams(("parallel",), 56),
        name="in_mix",
    )(x_first, x_second, x_first, x_second, x_first, x_second, g.reshape(1, D), mod3, mod3, w_bf,
      cos_t, sin_t, gv, ws_bf, bias_t, wconv, gq, gk, bd)


def _attn_kernel(lam_ref, qa_ref, qb_ref, *rest, n_seg, coef, tq):
    kv_refs = rest[:2 * n_seg]
    gsub_ref, o_ref, k_scr, vt_scr = rest[2 * n_seg:]

    @pl.when(pl.program_id(2) == 0)
    def _():
        off = 0
        for s in range(n_seg):
            n = kv_refs[s].shape[0]
            k_scr[off:off + n, :] = kv_refs[s][...]
            vt_scr[0:V_DIM, off:off + n] = kv_refs[n_seg + s][...].astype(F32).T.astype(BF16)
            off += n
        ones_row = lax.broadcasted_iota(jnp.int32, (16, off), 0) == 0
        vt_scr[V_DIM:, :] = jnp.where(ones_row, 1.0, 0.0).astype(BF16)

    lam = lam_ref[0]
    shift = lam_ref[1]
    qc = ATT_CHAIN

    def scores(c):
        rows = slice(c * qc, (c + 1) * qc)
        qs = jnp.concatenate([qa_ref[rows, :], qb_ref[rows, :]], axis=0)
        return lax.dot_general(k_scr[...], qs, _DN_T, preferred_element_type=F32)

    def finish(c, pt):
        ot = jnp.dot(vt_scr[...], pt, preferred_element_type=F32)
        inv = 1.0 / ot[V_DIM:V_DIM + 1, :]
        dt = ot[0:V_DIM, :qc] * inv[:, :qc] - ot[0:V_DIM, qc:] * (lam * inv[:, qc:])
        o = dt.T
        ms = jnp.mean(o * o, axis=-1, keepdims=True)
        o_ref[c * qc:(c + 1) * qc, :] = (o * lax.rsqrt(ms + EPS) * gsub_ref[...] * coef).astype(o_ref.dtype)

    @pl.when(lam_ref[2] > 0.5)
    def _():
        for c in range(tq // qc):
            finish(c, jnp.exp2(scores(c) - shift).astype(BF16))

    @pl.when(lam_ref[2] <= 0.5)
    def _():
        sts = [scores(c) for c in range(tq // qc)]
        for c in range(tq // qc):
            st = sts[c]
            finish(c, jnp.exp2(st - jnp.max(st, axis=0, keepdims=True)).astype(BF16))


def _attention(lam, qa, qb, kk, v, gsub, coef, *, ctx_queries):
    if ctx_queries:
        tq = CTX
        nq, lk, n_seg = 1, CTX, 1
        q_map = lambda b, h, qi: (NL // tq + b, h)
        kv_specs = [pl.BlockSpec((CTX, 128), lambda b, h, qi: (NL // CTX + b, h)),
                    pl.BlockSpec((CTX, 128), lambda b, h, qi: (NL // CTX + b, h))]
        kv_args = [kk, v]
        rows = NC
        o_map = lambda b, h, qi: (b, h)
    else:
        tq = TQ
        nq, lk, n_seg = L // tq, CTX + L, 2
        q_map = lambda b, h, qi: (b * (L // tq) + qi, h)
        kv_specs = [pl.BlockSpec((CTX, 128), lambda b, h, qi: (NL // CTX + b, h)),
                    pl.BlockSpec((L, 128), lambda b, h, qi: (b, h)),
                    pl.BlockSpec((CTX, 128), lambda b, h, qi: (NL // CTX + b, h)),
                    pl.BlockSpec((L, 128), lambda b, h, qi: (b, h))]
        kv_args = [kk, kk, v, v]
        rows = NL
        o_map = lambda b, h, qi: (b * (L // tq) + qi, h)
    return pl.pallas_call(
        functools.partial(_attn_kernel, n_seg=n_seg, coef=coef, tq=tq),
        out_shape=jax.ShapeDtypeStruct((rows, C_WIDTH), BF16),
        grid=(B, HEADS, nq),
        in_specs=[pl.BlockSpec(memory_space=pltpu.SMEM),
                  pl.BlockSpec((tq, 128), q_map), pl.BlockSpec((tq, 128), q_map)]
                 + kv_specs + [pl.BlockSpec((1, V_DIM), lambda b, h, qi: (0, 0))],
        out_specs=pl.BlockSpec((tq, 128), o_map),
        scratch_shapes=[pltpu.VMEM((lk, 128), BF16), pltpu.VMEM((V_DIM + 16, lk), BF16)],
        compiler_params=_cparams(("parallel", "parallel", "arbitrary"), 56),
        name="attn_ctx" if ctx_queries else "attn_lat",
    )(lam, qa, qb, *kv_args, gsub)


def _pack_rows(t, out_ref, row0=0):
    half = D // 2
    w = pltpu.pack_elementwise([t[:, :half], t[:, half:]], packed_dtype=BF16)
    w = lax.bitcast_convert_type(w, jnp.uint32)
    rows = t.shape[0]
    for j in range(4):
        out_ref[pl.ds(4 * row0 + j, rows, stride=4), :] = w[:, j * 128:(j + 1) * 128]


def _unpack_rows(ref, rows, lead=None, row0=0):
    los, his = [], []
    for j in range(4):
        sl = pl.ds(4 * row0 + j, rows, stride=4)
        w = ref[sl, :] if lead is None else ref[lead, sl, :]
        los.append(pltpu.unpack_elementwise(w, index=0, packed_dtype=BF16, unpacked_dtype=F32))
        his.append(pltpu.unpack_elementwise(w, index=1, packed_dtype=BF16, unpacked_dtype=F32))
    return jnp.concatenate(los, axis=1), jnp.concatenate(his, axis=1)


def _out_router_kernel(xa_ref, xb_ref, yab_ref, yca_ref, ycb_ref, wo_ref, g1_ref, g2n_ref, sh2_ref, sc2_ref, wrh_ref,
                       wrl_ref, br_ref, xo_ref, h2p_ref, idx_ref, gate_ref, rank_ref, cnt_ref, run_ref, *,
                       n_first, n_first_c, t0):
    tm = TM_OUT

    @pl.when(pl.program_id(0) == 0)
    def _():
        run_ref[...] = jnp.zeros_like(run_ref)

    yc = _two_source_rows(yca_ref, ycb_ref, n_first_c, t0)
    y = jnp.dot(jnp.concatenate([yab_ref[...], yc], axis=1), wo_ref[...], preferred_element_type=F32)
    x = _two_source_rows(xa_ref, xb_ref, n_first, t0) + g1_ref[...] * y
    xo_ref[...] = x
    h2 = _rms_mod(x, g2n_ref[...], sc2_ref[...], sh2_ref[...])
    hi = h2.astype(BF16)
    _pack_rows(h2, h2p_ref)
    lo = (h2 - hi.astype(F32)).astype(BF16)
    wh = wrh_ref[...]
    z = (lax.dot_general(wh, hi, _DN_T, preferred_element_type=F32)
         + lax.dot_general(wh, lo, _DN_T, preferred_element_type=F32)
         + lax.dot_general(wrl_ref[...], hi, _DN_T, preferred_element_type=F32))
    scores = jax.nn.sigmoid(z)
    work = scores + br_ref[...]
    eio = lax.broadcasted_iota(jnp.int32, work.shape, 0)
    idxs, sels, hits = [], [], []
    for _ in range(TOP_K):
        m = jnp.max(work, axis=0, keepdims=True)
        idx = jnp.min(jnp.where(work == m, eio, N_EXPERTS), axis=0, keepdims=True)
        hit = eio == idx
        sels.append(jnp.sum(jnp.where(hit, scores, 0.0), axis=0, keepdims=True))
        idxs.append(idx)
        hits.append(hit)
        work = jnp.where(hit, -jnp.inf, work)
    sel = jnp.concatenate(sels, axis=0)
    idx8 = jnp.concatenate(idxs, axis=0)
    gate8 = sel / jnp.sum(sel, axis=0, keepdims=True) * ROUTED_SCALE

    chosen = functools.reduce(jnp.logical_or, hits)
    before = (lax.broadcasted_iota(jnp.int32, (tm, tm), 0) < lax.broadcasted_iota(jnp.int32, (tm, tm), 1))
    prefix = jnp.dot(jnp.where(chosen, 1.0, 0.0).astype(BF16), jnp.where(before, 1.0, 0.0).astype(BF16),
                     preferred_element_type=F32)
    rank_dense = prefix + run_ref[:, 0:1]
    rank8 = jnp.concatenate([jnp.sum(jnp.where(h, rank_dense, 0.0), axis=0, keepdims=True) for h in hits],
                            axis=0).astype(jnp.int32)
    run = run_ref[...] + jnp.sum(jnp.where(chosen, 1.0, 0.0), axis=1, keepdims=True)
    run_ref[...] = run
    cnt_ref[...] = run
    for c in range(tm // 128):
        idx_ref[c] = idx8[:, c * 128:(c + 1) * 128]
        gate_ref[c] = gate8[:, c * 128:(c + 1) * 128]
        rank_ref[c] = rank8[:, c * 128:(c + 1) * 128]


def _out_router(x_first, x_second, yab, yc_first, yc_second, wo_bf, g2n, mod3, wr_hi, wr_lo, br, row0, n_rows):
    tm = TM_OUT
    t0 = row0 // tm
    n_first = x_first.shape[0] // tm
    n_first_c = yc_first.shape[0] // tm
    const2 = lambda i: (0, 0)
    row = lambda i: (i, 0)
    chunk3 = pl.BlockSpec((tm // 128, TOP_K, 128), lambda i: (i, 0, 0))
    nch = n_rows // 128
    return pl.pallas_call(
        functools.partial(_out_router_kernel, n_first=n_first, n_first_c=n_first_c, t0=t0),
        out_shape=[jax.ShapeDtypeStruct((n_rows, D), F32),
                   jax.ShapeDtypeStruct((n_rows * 4, 128), jnp.uint32),
                   jax.ShapeDtypeStruct((nch, TOP_K, 128), jnp.int32),
                   jax.ShapeDtypeStruct((nch, TOP_K, 128), F32),
                   jax.ShapeDtypeStruct((nch, TOP_K, 128), jnp.int32),
                   jax.ShapeDtypeStruct((N_EXPERTS, 128), F32)],
        grid=(n_rows // tm,),
        in_specs=_two_source_specs(tm, n_first, D, t0)
                 + [pl.BlockSpec((tm, 512), lambda i: (i + t0, 0))]
                 + _two_source_specs(tm, n_first_c, C_WIDTH, t0)
                 + [pl.BlockSpec((D, D), const2),
                  _mod_spec(tm, 2, t0),
                  pl.BlockSpec((1, D), const2), _mod_spec(tm, 3, t0), _mod_spec(tm, 4, t0),
                  pl.BlockSpec((N_EXPERTS, D), const2), pl.BlockSpec((N_EXPERTS, D), const2),
                  pl.BlockSpec((N_EXPERTS, 1), const2)],
        out_specs=[pl.BlockSpec((tm, D), row), pl.BlockSpec((tm * 4, 128), row),
                   chunk3, chunk3, chunk3, pl.BlockSpec((N_EXPERTS, 128), const2)],
        scratch_shapes=[pltpu.VMEM((N_EXPERTS, 128), F32)],
        compiler_params=_cparams(("arbitrary",), 48),
        name="out_router",
    )(x_first, x_second, yab, yc_first, yc_second, wo_bf, mod3, g2n.reshape(1, D), mod3, mod3, wr_hi, wr_lo,
      br.reshape(N_EXPERTS, 1))


def _experts_kernel(te_ref, nu_ref, x_hbm, wg_ref, wu_ref, wd_ref, y_ref, wgu_s, wd_s, x_ring, x_sems):
    i = pl.program_id(0)
    n_used = nu_ref[0]
    used = i < n_used
    new_expert = jnp.logical_or(i == 0, te_ref[i] != te_ref[jnp.maximum(i - 1, 0)])

    ahead = X_RING - 1

    def x_copy(t):
        slot = t % X_RING
        return pltpu.make_async_copy(x_hbm.at[pl.ds(t * (4 * TM_X), 4 * TM_X)], x_ring.at[slot], x_sems.at[slot])

    @pl.when(i == 0)
    def _():
        for t in range(ahead):
            @pl.when(t < n_used)
            def _():
                x_copy(t).start()

    @pl.when(i + ahead < n_used)
    def _():
        x_copy(i + ahead).start()

    @pl.when(jnp.logical_and(used, new_expert))
    def _():
        wgu_s[:, 0:D_EXPERT] = wg_ref[...].astype(BF16)
        wgu_s[:, D_EXPERT:] = wu_ref[...].astype(BF16)
        wd_s[...] = wd_ref[...].astype(BF16)

    @pl.when(used)
    def _():
        x_copy(i).wait()
        x_ref = x_ring.at[i % X_RING]
        rc = TM_X // X_CHAINS
        wgu = wgu_s[...]
        wd = wd_s[...]
        abs_ = []
        for c in range(X_CHAINS):
            x_lo, x_hi = _unpack_rows(x_ref, rc, row0=c * rc)
            x = jnp.concatenate([x_lo.astype(BF16), x_hi.astype(BF16)], axis=1)
            abs_.append(jnp.dot(x, wgu, preferred_element_type=F32))
        for c in range(X_CHAINS):
            a = abs_[c][:, :D_EXPERT]
            hid = (a * jax.nn.sigmoid(a) * abs_[c][:, D_EXPERT:]).astype(BF16)
            _pack_rows(jnp.dot(hid, wd, preferred_element_type=F32), y_ref, row0=c * rc)


def _experts(layer, tile_expert, n_used, x_sorted, w_gate, w_up, w_down):
    tm = TM_X
    r_pad = x_sorted.shape[0] // 4
    nt = r_pad // tm
    row = lambda i, te, nu: (jnp.minimum(i, nu[0] - 1), 0)
    wmap = lambda i, te, nu: (layer, te[i], 0, 0)
    grid_spec = pltpu.PrefetchScalarGridSpec(
        num_scalar_prefetch=2,
        grid=(nt,),
        in_specs=[pl.BlockSpec(memory_space=pl.ANY),
                  pl.BlockSpec((None, None, D, D_EXPERT), wmap),
                  pl.BlockSpec((None, None, D, D_EXPERT), wmap),
                  pl.BlockSpec((None, None, D_EXPERT, D), wmap)],
        out_specs=pl.BlockSpec((tm * 4, 128), row),
        scratch_shapes=[pltpu.VMEM((D, 2 * D_EXPERT), BF16), pltpu.VMEM((D_EXPERT, D), BF16),
                        pltpu.VMEM((X_RING, tm * 4, 128), jnp.uint32), pltpu.SemaphoreType.DMA((X_RING,))],
    )
    return pl.pallas_call(
        _experts_kernel,
        out_shape=jax.ShapeDtypeStruct((r_pad * 4, 128), jnp.uint32),
        grid_spec=grid_spec,
        compiler_params=_cparams(("arbitrary",), 48),
        name="experts",
    )(tile_expert, n_used, x_sorted, w_gate, w_up, w_down)


SC_CORES = 2
SC_SUBCORES = 16
SC_WORKERS = SC_CORES * SC_SUBCORES
SC_CHUNK = 128


def _sc_mesh():
    return plsc.VectorSubcoreMesh(core_axis_name="c", subcore_axis_name="s")


def _sc_params():
    return pltpu.CompilerParams(use_tc_tiling_on_sc=True)


def _sc_dispatch(h2p, pos3, r_pad):
    nch = pos3.shape[0]
    steps = -(-nch // SC_WORKERS)

    def body(h_hbm, pos_hbm, out_hbm, idx_v, rows_v, sem):
        wid = lax.axis_index("s") * SC_CORES + lax.axis_index("c")

        @pl.loop(0, steps)
        def _(s):
            ch = wid + s * SC_WORKERS

            @pl.when(ch < nch)
            def _():
                pltpu.sync_copy(pos_hbm.at[ch], idx_v)
                pltpu.sync_copy(h_hbm.at[pl.ds(ch * SC_CHUNK, SC_CHUNK)], rows_v)
                copies = [pltpu.async_copy(rows_v, out_hbm.at[idx_v.at[k]], sem) for k in range(TOP_K)]
                for cp in copies:
                    cp.wait()

    return pl.kernel(
        body,
        out_type=jax.ShapeDtypeStruct((r_pad, 4, 128), jnp.uint32),
        mesh=_sc_mesh(),
        scratch_types=[pltpu.VMEM((TOP_K, SC_CHUNK), jnp.int32),
                       pltpu.VMEM((SC_CHUNK, 4, 128), jnp.uint32),
                       pltpu.SemaphoreType.DMA],
        compiler_params=_sc_params(),
        name="sc_dispatch",
    )(h2p, pos3)


def _sc_collect(y_sorted, pos3):
    nch = pos3.shape[0]
    steps = -(-nch // SC_WORKERS)

    half = SC_CHUNK // 2
    units = [(k, hh) for k in range(TOP_K) for hh in range(2)]

    def body(y_hbm, pos_hbm, out_hbm, idx_v, rows_a, rows_b, sem_a, sem_b):
        wid = lax.axis_index("s") * SC_CORES + lax.axis_index("c")
        bufs = (rows_a, rows_b)
        sems = (sem_a, sem_b)

        def gather(u):
            k, hh = units[u]
            return pltpu.async_copy(y_hbm.at[idx_v.at[k, pl.ds(hh * half, half)]], bufs[u % 2], sems[u % 2])

        @pl.loop(0, steps)
        def _(s):
            ch = wid + s * SC_WORKERS

            @pl.when(ch < nch)
            def _():
                pltpu.sync_copy(pos_hbm.at[ch], idx_v)
                pending = gather(0)
                for u, (k, hh) in enumerate(units):
                    nxt = gather(u + 1) if u + 1 < len(units) else None
                    pending.wait()
                    pltpu.sync_copy(bufs[u % 2], out_hbm.at[k, pl.ds(ch * SC_CHUNK + hh * half, half)])
                    pending = nxt

    return pl.kernel(
        body,
        out_type=jax.ShapeDtypeStruct((TOP_K, nch * SC_CHUNK, 4, 128), jnp.uint32),
        mesh=_sc_mesh(),
        scratch_types=[pltpu.VMEM((TOP_K, SC_CHUNK), jnp.int32),
                       pltpu.VMEM((half, 4, 128), jnp.uint32),
                       pltpu.VMEM((half, 4, 128), jnp.uint32),
                       pltpu.SemaphoreType.DMA, pltpu.SemaphoreType.DMA],
        compiler_params=_sc_params(),
        name="sc_collect",
    )(y_sorted, pos3)


def _shared_kernel(x_ref, h2p_ref, yg_ref, gate_ref, wg_ref, wu_ref, wd_ref, g2_ref, o_ref):
    h_lo, h_hi = _unpack_rows(h2p_ref, TM_F)
    h = jnp.concatenate([h_lo.astype(BF16), h_hi.astype(BF16)], axis=1)
    a = jnp.dot(h, wg_ref[...], preferred_element_type=F32)
    b = jnp.dot(h, wu_ref[...], preferred_element_type=F32)
    hid = (a * jax.nn.sigmoid(a) * b).astype(BF16)
    f = jnp.dot(hid, wd_ref[...], preferred_element_type=F32)
    gate = gate_ref[...]
    f_lo = f[:, :D // 2]
    f_hi = f[:, D // 2:]
    for k in range(TOP_K):
        y_lo, y_hi = _unpack_rows(yg_ref, TM_F, lead=k)
        f_lo = f_lo + gate[:, k:k + 1] * y_lo
        f_hi = f_hi + gate[:, k:k + 1] * y_hi
    o_ref[...] = x_ref[...] + g2_ref[...] * jnp.concatenate([f_lo, f_hi], axis=1)


def _shared_residual(xa, h2p, yg, gates, wsg_bf, wsu_bf, wsd_bf, mod3, row0, n_rows):
    tm = TM_F
    t0 = row0 // tm
    row = lambda i: (i, 0)
    const2 = lambda i: (0, 0)
    return pl.pallas_call(
        _shared_kernel,
        out_shape=jax.ShapeDtypeStruct((n_rows, D), F32),
        grid=(n_rows // tm,),
        in_specs=[pl.BlockSpec((tm, D), row), pl.BlockSpec((tm * 4, 128), row),
                  pl.BlockSpec((TOP_K, tm * 4, 128), lambda i: (0, i, 0)),
                  pl.BlockSpec((tm, TOP_K), row),
                  pl.BlockSpec((D, D_SHARED), const2), pl.BlockSpec((D, D_SHARED), const2),
                  pl.BlockSpec((D_SHARED, D), const2), _mod_spec(tm, 5, t0)],
        out_specs=pl.BlockSpec((tm, D), row),
        compiler_params=_cparams(("parallel",), 48),
        name="shared_residual",
    )(xa, h2p, yg, gates, wsg_bf, wsu_bf, wsd_bf, mod3)


def _positions_kernel(offs_ref, idx_ref, rank_ref, pos_ref):
    idx = idx_ref[...]
    base = jnp.zeros_like(idx)
    for e in range(N_EXPERTS):
        base = jnp.where(idx == e, offs_ref[e], base)
    pos_ref[...] = rank_ref[...] + base


def _positions(offs, idx3, rank3):
    nch = idx3.shape[0]
    cb = 8
    spec = pl.BlockSpec((cb, TOP_K, 128), lambda i, offs: (i, 0, 0))
    return pl.pallas_call(
        _positions_kernel,
        out_shape=jax.ShapeDtypeStruct((nch, TOP_K, 128), jnp.int32),
        grid_spec=pltpu.PrefetchScalarGridSpec(num_scalar_prefetch=1, grid=(nch // cb,),
                                               in_specs=[spec, spec], out_specs=spec),
        compiler_params=_cparams(("parallel",)),
        name="positions",
    )(offs, idx3, rank3)


def _route_positions(idx3, rank3, counts, n_rows):
    tm = TM_X
    counts = counts.astype(jnp.int32)
    padded = ((counts + tm - 1) // tm) * tm
    ends = jnp.cumsum(padded)
    offs = ends - padded
    pos3 = _positions(offs.astype(jnp.int32), idx3, rank3)
    r_pad = n_rows * TOP_K + N_EXPERTS * tm
    nt = r_pad // tm
    tile_ids = jnp.arange(nt, dtype=jnp.int32)
    tile_expert = jnp.sum((ends // tm)[None, :] <= tile_ids[:, None], axis=1)
    tile_expert = jnp.minimum(tile_expert, N_EXPERTS - 1).astype(jnp.int32)
    n_used = (ends[-1] // tm).astype(jnp.int32).reshape(1)
    return pos3, tile_expert, n_used, r_pad


def _in_weights(w):
    def regroup(cols):
        return cols.reshape(D, 2, HEADS, QK_DIM).transpose(0, 2, 1, 3).reshape(D, 2 * HEADS * QK_DIM)
    return jnp.concatenate([w[:, :OFF_Q], regroup(w[:, OFF_Q:OFF_K]), regroup(w[:, OFF_K:OFF_V]), w[:, OFF_V:]],
                           axis=1).astype(BF16)


def _rope_tables():
    t = jnp.arange(L)
    row = (t // GRID_W).astype(F32)
    col = (t % GRID_W).astype(F32)
    n_freq = QK_DIM // 4
    inv = ROPE_BASE ** (-jnp.arange(n_freq, dtype=F32) / n_freq)
    ar = row[:, None] * inv
    ac = col[:, None] * inv
    cos64 = jnp.concatenate([jnp.cos(ar), jnp.cos(ar), jnp.cos(ac), jnp.cos(ac)], axis=1)
    sin64 = jnp.concatenate([-jnp.sin(ar), jnp.sin(ar), -jnp.sin(ac), jnp.sin(ac)], axis=1)
    cos_t = jnp.concatenate([jnp.tile(cos64, (1, 2)), jnp.ones((TM_IN, 128), F32)], axis=0)
    sin_t = jnp.concatenate([jnp.tile(sin64, (1, 2)), jnp.zeros((TM_IN, 128), F32)], axis=0)
    return cos_t, sin_t


def _split_bf16(w):
    hi = w.astype(BF16)
    return hi, (w - hi.astype(F32)).astype(BF16)


def kernel(x, c, ctx, c_ctx, w_ada, b_ada, g_norm1, g_norm2, w_in, w_out, g_v, w_s, b_s, w_conv, g_q, g_k,
           lam_q1, lam_k1, lam_q2, lam_k2, g_sub, w_router, b_router, w_gate, w_up, w_down,
           ws_gate, ws_up, ws_down):
    src = (x.reshape(NL, D), ctx.reshape(NC, D))
    cc = jnp.concatenate([c, c_ctx[None, :], jnp.zeros((MOD_ROWS - B - 1, D), F32)], axis=0)
    mod = _ada(cc, w_ada, b_ada)
    cos_t, sin_t = _rope_tables()
    bd = jnp.asarray(np.kron(np.eye(8, dtype=np.float32), np.full((64, 64), 1.0 / 64, np.float32)), BF16)

    for l in range(DEPTH):
        last = l == DEPTH - 1
        lam_init = 0.8 - 0.6 * math.exp(-0.3 * l)
        lam = (jnp.exp(jnp.sum(lam_q1[l] * lam_k1[l])) - jnp.exp(jnp.sum(lam_q2[l] * lam_k2[l])) + lam_init)
        bound = (QK_DIM * jnp.max(jnp.abs(g_q[l])) * jnp.max(jnp.abs(g_k[l]))
                 * (QK_DIM ** -0.5 * LOG2E) * ATT_BOUND_MARGIN)
        use_bound = (2.0 * bound < ATT_MAX_SHIFT_RANGE).astype(F32)
        lam = jnp.stack([lam, bound, use_bound]).astype(F32)
        mod3 = mod[l].reshape(MOD_ROWS, 1, 6 * D)
        w_in_bf = _in_weights(w_in[l])
        bias_t = jnp.repeat(b_s[l].T, A_GD, axis=1)
        wconv = jnp.concatenate([w_conv[l], jnp.zeros((5, B_WIDTH), F32)], axis=0)
        yab, qa, qb, kk, v = _in_mix(src[0], src[1], g_norm1[l], mod3, w_in_bf, cos_t, sin_t,
                                     g_v[l].reshape(1, A_WIDTH), w_s[l].astype(BF16), bias_t, wconv,
                                     jnp.tile(g_q[l], 8).reshape(1, 512), jnp.tile(g_k[l], 8).reshape(1, 512), bd)
        gsub = g_sub[l].reshape(1, V_DIM)
        coef = 1.0 - lam_init
        n_rows = NL if last else NR
        yc = _attention(lam, qa, qb, kk, v, gsub, coef, ctx_queries=False)
        yc_ctx = yc if last else _attention(lam, qa, qb, kk, v, gsub, coef, ctx_queries=True)
        wr_hi, wr_lo = _split_bf16(w_router[l].T)
        wo_bf = w_out[l].astype(BF16)
        ws_bf = (ws_gate[l].astype(BF16), ws_up[l].astype(BF16), ws_down[l].astype(BF16))
        nh = n_rows // MOE_GROUPS
        outs = []
        for h in range(MOE_GROUPS):
            xh, h2p, idx3, gate3, rank3, counts = _out_router(
                src[0], src[1], yab, yc, yc_ctx, wo_bf, g_norm2[l], mod3, wr_hi, wr_lo, b_router[l], h * nh, nh)
            pos3, tile_expert, n_used, r_pad = _route_positions(idx3, rank3, counts[:, 0], nh)
            x_sorted = _sc_dispatch(h2p.reshape(nh, 4, 128), pos3, r_pad)
            y_sorted = _experts(l, tile_expert, n_used, x_sorted.reshape(r_pad * 4, 128), w_gate, w_up, w_down)
            yg = _sc_collect(y_sorted.reshape(r_pad, 4, 128), pos3)
            gates = gate3.transpose(0, 2, 1).reshape(nh, TOP_K)
            outs.append(_shared_residual(xh, h2p, yg.reshape(TOP_K, nh * 4, 128), gates, *ws_bf, mod3, h * nh, nh))
        src = (outs[0], outs[-1])
    return jnp.concatenate(outs, axis=0).reshape(B, L, D)
```

```python
import functools
import math

import numpy as np
import jax
import jax.numpy as jnp
from jax import lax
from jax.experimental import pallas as pl
from jax.experimental.pallas import tpu as pltpu
from jax.experimental.pallas import tpu_sc as plsc

F32 = jnp.float32
BF16 = jnp.bfloat16

D = 1024
B = 8
L = 2048
DEPTH = 2
GRID_W = 64
CTX = 256
A_WIDTH = 256
A_GROUPS = 4
A_GD = 64
CHUNK = 128
B_WIDTH = 256
C_WIDTH = 512
HEADS = 4
V_DIM = 128
QK_DIM = 64
ROPE_BASE = 10000.0
OFF_BB = 512
OFF_BC = 768
OFF_BX = 1024
OFF_Q = 1280
OFF_K = 1792
OFF_V = 2304
D_IN = 2816
N_EXPERTS = 64
TOP_K = 8
D_EXPERT = 256
D_SHARED = 256
ROUTED_SCALE = 2.5
EPS = 1e-6

NL = B * L
NC = B * CTX
NR = NL + NC
MOD_ROWS = 16
LOG2E = 1.4426950408889634

TM_IN = 512
TQ = 1024
ATT_CHAIN = 256
ATT_BOUND_MARGIN = 1.02
ATT_MAX_SHIFT_RANGE = 100.0
TM_OUT = 512
TM_X = 1024
X_CHAINS = 2
X_RING = 3
TM_F = 512

_DN_T = (((1,), (1,)), ((), ()))


def _cparams(sem, vmem_mb=None):
    kw = dict(dimension_semantics=sem)
    if vmem_mb is not None:
        kw["vmem_limit_bytes"] = vmem_mb * 1024 * 1024
    return pltpu.CompilerParams(**kw)


def _mod_row(i, tm):
    return jnp.where(i < NL // tm, i // (L // tm), B)


def _mod_spec(tm, chunk):
    return pl.BlockSpec((None, 1, D), lambda i: (_mod_row(i, tm), 0, chunk))


def _ada_kernel(c_ref, w_ref, b_ref, o_ref):
    c = c_ref[...]
    cs = c * jax.nn.sigmoid(c)
    o_ref[...] = jnp.dot(cs, w_ref[...], preferred_element_type=F32,
                         precision=lax.Precision.HIGHEST) + b_ref[...]


def _ada(cc, w_ada, b_ada):
    nb = 6
    return pl.pallas_call(
        _ada_kernel,
        out_shape=jax.ShapeDtypeStruct((DEPTH, MOD_ROWS, 6 * D), F32),
        grid=(DEPTH, nb),
        in_specs=[pl.BlockSpec((MOD_ROWS, D), lambda l, j: (0, 0)),
                  pl.BlockSpec((None, D, D), lambda l, j: (l, 0, j)),
                  pl.BlockSpec((None, 1, D), lambda l, j: (l, 0, j))],
        out_specs=pl.BlockSpec((None, MOD_ROWS, D), lambda l, j: (l, 0, j)),
        compiler_params=_cparams(("arbitrary", "arbitrary"), 40),
        name="ada_mod",
    )(cc, w_ada, b_ada.reshape(DEPTH, 1, 6 * D))


def _rms_mod(x, g, sc, sh):
    ms = jnp.mean(x * x, axis=-1, keepdims=True)
    return x * lax.rsqrt(ms + EPS) * g * (1.0 + sc) + sh


def _two_source_specs(tm, n_first, width=D):
    return [pl.BlockSpec((tm, width), lambda i: (jnp.minimum(i, n_first - 1), 0)),
            pl.BlockSpec((tm, width), lambda i: (jnp.maximum(i - n_first, 0), 0))]


def _two_source_rows(a_ref, b_ref, n_first):
    return jnp.where(pl.program_id(0) < n_first, a_ref[...], b_ref[...])


def _group_rms(t, g, bd):
    ms = jnp.dot((t * t).astype(BF16), bd, preferred_element_type=F32)
    return t * lax.rsqrt(ms + EPS) * g


def _rope(t, cos, sin):
    w = t.shape[1]
    lane = lax.broadcasted_iota(jnp.int32, t.shape, 1)
    first = (lane % 32) < 16
    partner = jnp.where(first, pltpu.roll(t, w - 16, 1), pltpu.roll(t, 16, 1))
    cos4 = jnp.concatenate([cos] * (w // 128), axis=1)
    sin4 = jnp.concatenate([sin] * (w // 128), axis=1)
    return t * cos4 + partner * sin4


def _in_mix_kernel(xa_ref, xb_ref, pa_ref, pb_ref, na_ref, nb_ref, g_ref, sh_ref, sc_ref, w_ref,
                   cos_ref, sin_ref, gv_ref, ws_ref, bias_ref, wconv_ref, gq_ref, gk_ref, bd_ref,
                   yab_ref, qa_ref, qb_ref, kk_ref, v_ref, *, n_first):
    tm = TM_IN
    i = pl.program_id(0)
    tiles_per_seq = L // tm
    is_lat = i < NL // tm
    is_start = jnp.logical_or(jnp.logical_not(is_lat), i % tiles_per_seq == 0)
    is_end = jnp.logical_or(jnp.logical_not(is_lat), i % tiles_per_seq == tiles_per_seq - 1)
    first = i < n_first
    g, sc, sh = g_ref[...], sc_ref[...], sh_ref[...]

    h = _rms_mod(jnp.where(first, xa_ref[...], xb_ref[...]), g, sc, sh)
    p = jnp.dot(h.astype(BF16), w_ref[...], preferred_element_type=F32)
    v_ref[...] = p[:, OFF_V:].astype(BF16)
    halo = jnp.concatenate([jnp.where(first, pa_ref[...], pb_ref[...]),
                            jnp.where(first, na_ref[...], nb_ref[...])], axis=0)
    ph = jnp.dot(_rms_mod(halo, g, sc, sh).astype(BF16), w_ref[:, OFF_BC:OFF_Q], preferred_element_type=F32)
    zh = ph[:, :B_WIDTH] * ph[:, B_WIDTH:]
    zp = jnp.where(is_start, 0.0, zh[15:16])
    zn = jnp.where(is_end, 0.0, zh[16:17])

    uv = p[:, 0:2 * A_WIDTH]
    uv = 0.5 * uv * (1.0 + lax.erf(uv * (2.0 ** -0.5)))
    u = uv[:, :A_WIDTH]
    v = uv[:, A_WIDTH:]
    ms = jnp.mean(v * v, axis=-1, keepdims=True)
    vb = (v * lax.rsqrt(ms + EPS) * gv_ref[...]).astype(BF16)
    lane = lax.broadcasted_iota(jnp.int32, (CHUNK, 128), 1)
    mixes = []
    for c in range(tm // CHUNK):
        vc = vb[c * CHUNK:(c + 1) * CHUNK]
        halves = []
        for j in range(2):
            vj = vc[:, j * 128:(j + 1) * 128]
            m0 = jnp.dot(ws_ref[2 * j], vj, preferred_element_type=F32)
            m1 = jnp.dot(ws_ref[2 * j + 1], vj, preferred_element_type=F32)
            halves.append(jnp.where(lane < A_GD, m0, m1))
        mixes.append(jnp.concatenate(halves, axis=1) + bias_ref[...])
    ya = u * jnp.concatenate(mixes, axis=0)

    bg = p[:, OFF_BB:OFF_BC]
    z = p[:, OFF_BC:OFF_BX] * p[:, OFF_BX:OFF_Q]
    row = lax.broadcasted_iota(jnp.int32, z.shape, 0)
    inner = jnp.logical_not(is_lat)
    z_prev = jnp.where(row == 0, zp, pltpu.roll(z, 1, 0))
    z_prev = jnp.where(jnp.logical_and(inner, row % CTX == 0), 0.0, z_prev)
    z_next = jnp.where(row == tm - 1, zn, pltpu.roll(z, tm - 1, 0))
    z_next = jnp.where(jnp.logical_and(inner, row % CTX == CTX - 1), 0.0, z_next)
    yb = bg * (z_prev * wconv_ref[0:1] + z * wconv_ref[1:2] + z_next * wconv_ref[2:3])
    yab_ref[...] = jnp.concatenate([ya, yb], axis=1).astype(BF16)

    cos = cos_ref[...]
    sin = sin_ref[...]
    bd = bd_ref[...]
    q = _rope(_group_rms(p[:, OFF_Q:OFF_K], gq_ref[...], bd), cos, sin)
    q = q * (QK_DIM ** -0.5 * LOG2E)
    lane5 = lax.broadcasted_iota(jnp.int32, q.shape, 1) % 128
    qa_ref[...] = jnp.where(lane5 < QK_DIM, q, 0.0).astype(BF16)
    qb_ref[...] = jnp.where(lane5 >= QK_DIM, q, 0.0).astype(BF16)
    k = _rope(_group_rms(p[:, OFF_K:OFF_V], gk_ref[...], bd), cos, sin)
    kk_ref[...] = k.astype(BF16)


def _in_mix(x_first, x_second, g, mod3, w_bf, cos_t, sin_t, gv, ws_bf, bias_t, wconv, gq, gk, bd):
    tm = TM_IN
    n_first = x_first.shape[0] // tm
    hb = tm // 16
    nhb_first = x_first.shape[0] // 16
    nhb_second = x_second.shape[0] // 16
    pos_blocks = L // tm

    def tab_map(i):
        return (jnp.where(i < NL // tm, i % pos_blocks, pos_blocks), 0)

    def halo_specs(shift):
        blk = lambda i: (i * tm + shift) // 16
        return [pl.BlockSpec((16, D), lambda i: (jnp.clip(blk(i), 0, nhb_first - 1), 0)),
                pl.BlockSpec((16, D), lambda i: (jnp.clip(blk(i) - nhb_first, 0, nhb_second - 1), 0))]

    const2 = lambda i: (0, 0)
    row512 = pl.BlockSpec((tm, 512), lambda i: (i, 0))
    return pl.pallas_call(
        functools.partial(_in_mix_kernel, n_first=n_first),
        out_shape=[jax.ShapeDtypeStruct((NR, 512), BF16)] * 5,
        grid=(NR // tm,),
        in_specs=_two_source_specs(tm, n_first) + halo_specs(-1) + halo_specs(tm)
                 + [pl.BlockSpec((1, D), const2), _mod_spec(tm, 0), _mod_spec(tm, 1),
                    pl.BlockSpec((D, D_IN), const2),
                    pl.BlockSpec((tm, 128), tab_map), pl.BlockSpec((tm, 128), tab_map),
                    pl.BlockSpec((1, A_WIDTH), const2),
                    pl.BlockSpec((A_GROUPS, CHUNK, CHUNK), lambda i: (0, 0, 0)),
                    pl.BlockSpec((CHUNK, A_WIDTH), const2),
                    pl.BlockSpec((8, B_WIDTH), const2),
                    pl.BlockSpec((1, 512), const2), pl.BlockSpec((1, 512), const2),
                    pl.BlockSpec((512, 512), const2)],
        out_specs=[row512] * 5,
        compiler_params=_cparams(("parallel",), 56),
        name="in_mix",
    )(x_first, x_second, x_first, x_second, x_first, x_second, g.reshape(1, D), mod3, mod3, w_bf,
      cos_t, sin_t, gv, ws_bf, bias_t, wconv, gq, gk, bd)


def _attn_kernel(lam_ref, qa_ref, qb_ref, *rest, n_seg, coef, tq):
    kv_refs = rest[:2 * n_seg]
    gsub_ref, o_ref, k_scr, vt_scr = rest[2 * n_seg:]

    @pl.when(pl.program_id(2) == 0)
    def _():
        off = 0
        for s in range(n_seg):
            n = kv_refs[s].shape[0]
            k_scr[off:off + n, :] = kv_refs[s][...]
            vt_scr[0:V_DIM, off:off + n] = kv_refs[n_seg + s][...].astype(F32).T.astype(BF16)
            off += n
        ones_row = lax.broadcasted_iota(jnp.int32, (16, off), 0) == 0
        vt_scr[V_DIM:, :] = jnp.where(ones_row, 1.0, 0.0).astype(BF16)

    lam = lam_ref[0]
    shift = lam_ref[1]
    qc = ATT_CHAIN

    def scores(c):
        rows = slice(c * qc, (c + 1) * qc)
        qs = jnp.concatenate([qa_ref[rows, :], qb_ref[rows, :]], axis=0)
        return lax.dot_general(k_scr[...], qs, _DN_T, preferred_element_type=F32)

    def finish(c, pt):
        ot = jnp.dot(vt_scr[...], pt, preferred_element_type=F32)
        inv = 1.0 / ot[V_DIM:V_DIM + 1, :]
        dt = ot[0:V_DIM, :qc] * inv[:, :qc] - ot[0:V_DIM, qc:] * (lam * inv[:, qc:])
        o = dt.T
        ms = jnp.mean(o * o, axis=-1, keepdims=True)
        o_ref[c * qc:(c + 1) * qc, :] = (o * lax.rsqrt(ms + EPS) * gsub_ref[...] * coef).astype(o_ref.dtype)

    @pl.when(lam_ref[2] > 0.5)
    def _():
        for c in range(tq // qc):
            finish(c, jnp.exp2(scores(c) - shift).astype(BF16))

    @pl.when(lam_ref[2] <= 0.5)
    def _():
        sts = [scores(c) for c in range(tq // qc)]
        for c in range(tq // qc):
            st = sts[c]
            finish(c, jnp.exp2(st - jnp.max(st, axis=0, keepdims=True)).astype(BF16))


def _attention(lam, qa, qb, kk, v, gsub, coef, *, ctx_queries):
    if ctx_queries:
        tq = CTX
        nq, lk, n_seg = 1, CTX, 1
        q_map = lambda b, h, qi: (NL // tq + b, h)
        kv_specs = [pl.BlockSpec((CTX, 128), lambda b, h, qi: (NL // CTX + b, h)),
                    pl.BlockSpec((CTX, 128), lambda b, h, qi: (NL // CTX + b, h))]
        kv_args = [kk, v]
        rows = NC
        o_map = lambda b, h, qi: (b, h)
    else:
        tq = TQ
        nq, lk, n_seg = L // tq, CTX + L, 2
        q_map = lambda b, h, qi: (b * (L // tq) + qi, h)
        kv_specs = [pl.BlockSpec((CTX, 128), lambda b, h, qi: (NL // CTX + b, h)),
                    pl.BlockSpec((L, 128), lambda b, h, qi: (b, h)),
                    pl.BlockSpec((CTX, 128), lambda b, h, qi: (NL // CTX + b, h)),
                    pl.BlockSpec((L, 128), lambda b, h, qi: (b, h))]
        kv_args = [kk, kk, v, v]
        rows = NL
        o_map = lambda b, h, qi: (b * (L // tq) + qi, h)
    return pl.pallas_call(
        functools.partial(_attn_kernel, n_seg=n_seg, coef=coef, tq=tq),
        out_shape=jax.ShapeDtypeStruct((rows, C_WIDTH), BF16),
        grid=(B, HEADS, nq),
        in_specs=[pl.BlockSpec(memory_space=pltpu.SMEM),
                  pl.BlockSpec((tq, 128), q_map), pl.BlockSpec((tq, 128), q_map)]
                 + kv_specs + [pl.BlockSpec((1, V_DIM), lambda b, h, qi: (0, 0))],
        out_specs=pl.BlockSpec((tq, 128), o_map),
        scratch_shapes=[pltpu.VMEM((lk, 128), BF16), pltpu.VMEM((V_DIM + 16, lk), BF16)],
        compiler_params=_cparams(("parallel", "parallel", "arbitrary"), 56),
        name="attn_ctx" if ctx_queries else "attn_lat",
    )(lam, qa, qb, *kv_args, gsub)


def _pack_rows(t, out_ref, row0=0):
    half = D // 2
    w = pltpu.pack_elementwise([t[:, :half], t[:, half:]], packed_dtype=BF16)
    w = lax.bitcast_convert_type(w, jnp.uint32)
    rows = t.shape[0]
    for j in range(4):
        out_ref[pl.ds(4 * row0 + j, rows, stride=4), :] = w[:, j * 128:(j + 1) * 128]


def _unpack_rows(ref, rows, lead=None, row0=0):
    los, his = [], []
    for j in range(4):
        sl = pl.ds(4 * row0 + j, rows, stride=4)
        w = ref[sl, :] if lead is None else ref[lead, sl, :]
        los.append(pltpu.unpack_elementwise(w, index=0, packed_dtype=BF16, unpacked_dtype=F32))
        his.append(pltpu.unpack_elementwise(w, index=1, packed_dtype=BF16, unpacked_dtype=F32))
    return jnp.concatenate(los, axis=1), jnp.concatenate(his, axis=1)


def _out_router_kernel(xa_ref, xb_ref, yab_ref, yca_ref, ycb_ref, wo_ref, g1_ref, g2n_ref, sh2_ref, sc2_ref, wrh_ref,
                       wrl_ref, br_ref, xo_ref, h2p_ref, idx_ref, gate_ref, rank_ref, cnt_ref, run_ref, *,
                       n_first, n_first_c):
    tm = TM_OUT

    @pl.when(pl.program_id(0) == 0)
    def _():
        run_ref[...] = jnp.zeros_like(run_ref)

    yc = _two_source_rows(yca_ref, ycb_ref, n_first_c)
    y = jnp.dot(jnp.concatenate([yab_ref[...], yc], axis=1), wo_ref[...], preferred_element_type=F32)
    x = _two_source_rows(xa_ref, xb_ref, n_first) + g1_ref[...] * y
    xo_ref[...] = x
    h2 = _rms_mod(x, g2n_ref[...], sc2_ref[...], sh2_ref[...])
    hi = h2.astype(BF16)
    _pack_rows(h2, h2p_ref)
    lo = (h2 - hi.astype(F32)).astype(BF16)
    wh = wrh_ref[...]
    z = (lax.dot_general(wh, hi, _DN_T, preferred_element_type=F32)
         + lax.dot_general(wh, lo, _DN_T, preferred_element_type=F32)
         + lax.dot_general(wrl_ref[...], hi, _DN_T, preferred_element_type=F32))
    scores = jax.nn.sigmoid(z)
    work = scores + br_ref[...]
    eio = lax.broadcasted_iota(jnp.int32, work.shape, 0)
    idxs, sels, hits = [], [], []
    for _ in range(TOP_K):
        m = jnp.max(work, axis=0, keepdims=True)
        idx = jnp.min(jnp.where(work == m, eio, N_EXPERTS), axis=0, keepdims=True)
        hit = eio == idx
        sels.append(jnp.sum(jnp.where(hit, scores, 0.0), axis=0, keepdims=True))
        idxs.append(idx)
        hits.append(hit)
        work = jnp.where(hit, -jnp.inf, work)
    sel = jnp.concatenate(sels, axis=0)
    idx8 = jnp.concatenate(idxs, axis=0)
    gate8 = sel / jnp.sum(sel, axis=0, keepdims=True) * ROUTED_SCALE

    chosen = functools.reduce(jnp.logical_or, hits)
    before = (lax.broadcasted_iota(jnp.int32, (tm, tm), 0) < lax.broadcasted_iota(jnp.int32, (tm, tm), 1))
    prefix = jnp.dot(jnp.where(chosen, 1.0, 0.0).astype(BF16), jnp.where(before, 1.0, 0.0).astype(BF16),
                     preferred_element_type=F32)
    rank_dense = prefix + run_ref[:, 0:1]
    rank8 = jnp.concatenate([jnp.sum(jnp.where(h, rank_dense, 0.0), axis=0, keepdims=True) for h in hits],
                            axis=0).astype(jnp.int32)
    run = run_ref[...] + jnp.sum(jnp.where(chosen, 1.0, 0.0), axis=1, keepdims=True)
    run_ref[...] = run
    cnt_ref[...] = run
    for c in range(tm // 128):
        idx_ref[c] = idx8[:, c * 128:(c + 1) * 128]
        gate_ref[c] = gate8[:, c * 128:(c + 1) * 128]
        rank_ref[c] = rank8[:, c * 128:(c + 1) * 128]


def _out_router(x_first, x_second, yab, yc_first, yc_second, wo_bf, g2n, mod3, wr_hi, wr_lo, br, n_rows):
    tm = TM_OUT
    n_first = x_first.shape[0] // tm
    n_first_c = yc_first.shape[0] // tm
    const2 = lambda i: (0, 0)
    row = lambda i: (i, 0)
    chunk3 = pl.BlockSpec((tm // 128, TOP_K, 128), lambda i: (i, 0, 0))
    nch = n_rows // 128
    return pl.pallas_call(
        functools.partial(_out_router_kernel, n_first=n_first, n_first_c=n_first_c),
        out_shape=[jax.ShapeDtypeStruct((n_rows, D), F32),
                   jax.ShapeDtypeStruct((n_rows * 4, 128), jnp.uint32),
                   jax.ShapeDtypeStruct((nch, TOP_K, 128), jnp.int32),
                   jax.ShapeDtypeStruct((nch, TOP_K, 128), F32),
                   jax.ShapeDtypeStruct((nch, TOP_K, 128), jnp.int32),
                   jax.ShapeDtypeStruct((N_EXPERTS, 128), F32)],
        grid=(n_rows // tm,),
        in_specs=_two_source_specs(tm, n_first)
                 + [pl.BlockSpec((tm, 512), row)]
                 + _two_source_specs(tm, n_first_c, C_WIDTH)
                 + [pl.BlockSpec((D, D), const2),
                  _mod_spec(tm, 2),
                  pl.BlockSpec((1, D), const2), _mod_spec(tm, 3), _mod_spec(tm, 4),
                  pl.BlockSpec((N_EXPERTS, D), const2), pl.BlockSpec((N_EXPERTS, D), const2),
                  pl.BlockSpec((N_EXPERTS, 1), const2)],
        out_specs=[pl.BlockSpec((tm, D), row), pl.BlockSpec((tm * 4, 128), row),
                   chunk3, chunk3, chunk3, pl.BlockSpec((N_EXPERTS, 128), const2)],
        scratch_shapes=[pltpu.VMEM((N_EXPERTS, 128), F32)],
        compiler_params=_cparams(("arbitrary",), 48),
        name="out_router",
    )(x_first, x_second, yab, yc_first, yc_second, wo_bf, mod3, g2n.reshape(1, D), mod3, mod3, wr_hi, wr_lo,
      br.reshape(N_EXPERTS, 1))


def _experts_kernel(te_ref, nu_ref, x_hbm, wg_ref, wu_ref, wd_ref, y_ref, wgu_s, wd_s, x_ring, x_sems):
    i = pl.program_id(0)
    n_used = nu_ref[0]
    used = i < n_used
    new_expert = jnp.logical_or(i == 0, te_ref[i] != te_ref[jnp.maximum(i - 1, 0)])

    ahead = X_RING - 1

    def x_copy(t):
        slot = t % X_RING
        return pltpu.make_async_copy(x_hbm.at[pl.ds(t * (4 * TM_X), 4 * TM_X)], x_ring.at[slot], x_sems.at[slot])

    @pl.when(i == 0)
    def _():
        for t in range(ahead):
            @pl.when(t < n_used)
            def _():
                x_copy(t).start()

    @pl.when(i + ahead < n_used)
    def _():
        x_copy(i + ahead).start()

    @pl.when(jnp.logical_and(used, new_expert))
    def _():
        wgu_s[:, 0:D_EXPERT] = wg_ref[...].astype(BF16)
        wgu_s[:, D_EXPERT:] = wu_ref[...].astype(BF16)
        wd_s[...] = wd_ref[...].astype(BF16)

    @pl.when(used)
    def _():
        x_copy(i).wait()
        x_ref = x_ring.at[i % X_RING]
        rc = TM_X // X_CHAINS
        wgu = wgu_s[...]
        wd = wd_s[...]
        abs_ = []
        for c in range(X_CHAINS):
            x_lo, x_hi = _unpack_rows(x_ref, rc, row0=c * rc)
            x = jnp.concatenate([x_lo.astype(BF16), x_hi.astype(BF16)], axis=1)
            abs_.append(jnp.dot(x, wgu, preferred_element_type=F32))
        for c in range(X_CHAINS):
            a = abs_[c][:, :D_EXPERT]
            hid = (a * jax.nn.sigmoid(a) * abs_[c][:, D_EXPERT:]).astype(BF16)
            _pack_rows(jnp.dot(hid, wd, preferred_element_type=F32), y_ref, row0=c * rc)


def _experts(layer, tile_expert, n_used, x_sorted, w_gate, w_up, w_down):
    tm = TM_X
    r_pad = x_sorted.shape[0] // 4
    nt = r_pad // tm
    row = lambda i, te, nu: (jnp.minimum(i, nu[0] - 1), 0)
    wmap = lambda i, te, nu: (layer, te[i], 0, 0)
    grid_spec = pltpu.PrefetchScalarGridSpec(
        num_scalar_prefetch=2,
        grid=(nt,),
        in_specs=[pl.BlockSpec(memory_space=pl.ANY),
                  pl.BlockSpec((None, None, D, D_EXPERT), wmap),
                  pl.BlockSpec((None, None, D, D_EXPERT), wmap),
                  pl.BlockSpec((None, None, D_EXPERT, D), wmap)],
        out_specs=pl.BlockSpec((tm * 4, 128), row),
        scratch_shapes=[pltpu.VMEM((D, 2 * D_EXPERT), BF16), pltpu.VMEM((D_EXPERT, D), BF16),
                        pltpu.VMEM((X_RING, tm * 4, 128), jnp.uint32), pltpu.SemaphoreType.DMA((X_RING,))],
    )
    return pl.pallas_call(
        _experts_kernel,
        out_shape=jax.ShapeDtypeStruct((r_pad * 4, 128), jnp.uint32),
        grid_spec=grid_spec,
        compiler_params=_cparams(("arbitrary",), 48),
        name="experts",
    )(tile_expert, n_used, x_sorted, w_gate, w_up, w_down)


SC_CORES = 2
SC_SUBCORES = 16
SC_WORKERS = SC_CORES * SC_SUBCORES
SC_CHUNK = 128


def _sc_mesh():
    return plsc.VectorSubcoreMesh(core_axis_name="c", subcore_axis_name="s")


def _sc_params():
    return pltpu.CompilerParams(use_tc_tiling_on_sc=True)


def _sc_dispatch(h2p, pos3, r_pad):
    nch = pos3.shape[0]
    n_items = 2 * nch
    steps = -(-n_items // SC_WORKERS)
    kh = TOP_K // 2

    def body(h_hbm, pos_hbm, out_hbm, idx_v, rows_v, sem):
        wid = lax.axis_index("s") * SC_CORES + lax.axis_index("c")

        @pl.loop(0, steps)
        def _(s):
            item = wid + s * SC_WORKERS

            @pl.when(item < n_items)
            def _():
                ch = item // 2
                k0 = (item % 2) * kh
                pltpu.sync_copy(pos_hbm.at[ch], idx_v)
                pltpu.sync_copy(h_hbm.at[pl.ds(ch * SC_CHUNK, SC_CHUNK)], rows_v)
                copies = [pltpu.async_copy(rows_v, out_hbm.at[idx_v.at[k0 + j]], sem) for j in range(kh)]
                for cp in copies:
                    cp.wait()

    return pl.kernel(
        body,
        out_type=jax.ShapeDtypeStruct((r_pad, 4, 128), jnp.uint32),
        mesh=_sc_mesh(),
        scratch_types=[pltpu.VMEM((TOP_K, SC_CHUNK), jnp.int32),
                       pltpu.VMEM((SC_CHUNK, 4, 128), jnp.uint32),
                       pltpu.SemaphoreType.DMA],
        compiler_params=_sc_params(),
        name="sc_dispatch",
    )(h2p, pos3)


def _sc_collect(y_sorted, pos3):
    nch = pos3.shape[0]
    half = SC_CHUNK // 2
    n_items = 2 * nch
    steps = -(-n_items // SC_WORKERS)

    def body(y_hbm, pos_hbm, out_hbm, idx_v, rows_a, rows_b, sem_a, sem_b):
        wid = lax.axis_index("s") * SC_CORES + lax.axis_index("c")
        bufs = (rows_a, rows_b)
        sems = (sem_a, sem_b)

        @pl.loop(0, steps)
        def _(s):
            item = wid + s * SC_WORKERS

            @pl.when(item < n_items)
            def _():
                ch = item // 2
                t0 = pl.multiple_of((item % 2) * half, half)

                def gather(k):
                    return pltpu.async_copy(y_hbm.at[idx_v.at[k, pl.ds(t0, half)]], bufs[k % 2], sems[k % 2])

                pltpu.sync_copy(pos_hbm.at[ch], idx_v)
                pending = gather(0)
                for k in range(TOP_K):
                    nxt = gather(k + 1) if k + 1 < TOP_K else None
                    pending.wait()
                    pltpu.sync_copy(bufs[k % 2], out_hbm.at[k, pl.ds(ch * SC_CHUNK + t0, half)])
                    pending = nxt

    return pl.kernel(
        body,
        out_type=jax.ShapeDtypeStruct((TOP_K, nch * SC_CHUNK, 4, 128), jnp.uint32),
        mesh=_sc_mesh(),
        scratch_types=[pltpu.VMEM((TOP_K, SC_CHUNK), jnp.int32),
                       pltpu.VMEM((half, 4, 128), jnp.uint32),
                       pltpu.VMEM((half, 4, 128), jnp.uint32),
                       pltpu.SemaphoreType.DMA, pltpu.SemaphoreType.DMA],
        compiler_params=_sc_params(),
        name="sc_collect",
    )(y_sorted, pos3)


def _shared_kernel(x_ref, h2p_ref, yg_ref, gate_ref, wg_ref, wu_ref, wd_ref, g2_ref, o_ref):
    h_lo, h_hi = _unpack_rows(h2p_ref, TM_F)
    h = jnp.concatenate([h_lo.astype(BF16), h_hi.astype(BF16)], axis=1)
    a = jnp.dot(h, wg_ref[...], preferred_element_type=F32)
    b = jnp.dot(h, wu_ref[...], preferred_element_type=F32)
    hid = (a * jax.nn.sigmoid(a) * b).astype(BF16)
    f = jnp.dot(hid, wd_ref[...], preferred_element_type=F32)
    gate = gate_ref[...]
    f_lo = f[:, :D // 2]
    f_hi = f[:, D // 2:]
    for k in range(TOP_K):
        y_lo, y_hi = _unpack_rows(yg_ref, TM_F, lead=k)
        f_lo = f_lo + gate[:, k:k + 1] * y_lo
        f_hi = f_hi + gate[:, k:k + 1] * y_hi
    o_ref[...] = x_ref[...] + g2_ref[...] * jnp.concatenate([f_lo, f_hi], axis=1)


def _shared_residual(xa, h2p, yg, gates, wsg_bf, wsu_bf, wsd_bf, mod3, n_rows):
    tm = TM_F
    row = lambda i: (i, 0)
    const2 = lambda i: (0, 0)
    return pl.pallas_call(
        _shared_kernel,
        out_shape=jax.ShapeDtypeStruct((n_rows, D), F32),
        grid=(n_rows // tm,),
        in_specs=[pl.BlockSpec((tm, D), row), pl.BlockSpec((tm * 4, 128), row),
                  pl.BlockSpec((TOP_K, tm * 4, 128), lambda i: (0, i, 0)),
                  pl.BlockSpec((tm, TOP_K), row),
                  pl.BlockSpec((D, D_SHARED), const2), pl.BlockSpec((D, D_SHARED), const2),
                  pl.BlockSpec((D_SHARED, D), const2), _mod_spec(tm, 5)],
        out_specs=pl.BlockSpec((tm, D), row),
        compiler_params=_cparams(("parallel",), 48),
        name="shared_residual",
    )(xa, h2p, yg, gates, wsg_bf, wsu_bf, wsd_bf, mod3)


def _positions_kernel(offs_ref, idx_ref, rank_ref, pos_ref):
    idx = idx_ref[...]
    base = jnp.zeros_like(idx)
    for e in range(N_EXPERTS):
        base = jnp.where(idx == e, offs_ref[e], base)
    pos_ref[...] = rank_ref[...] + base


def _positions(offs, idx3, rank3):
    nch = idx3.shape[0]
    cb = 8
    spec = pl.BlockSpec((cb, TOP_K, 128), lambda i, offs: (i, 0, 0))
    return pl.pallas_call(
        _positions_kernel,
        out_shape=jax.ShapeDtypeStruct((nch, TOP_K, 128), jnp.int32),
        grid_spec=pltpu.PrefetchScalarGridSpec(num_scalar_prefetch=1, grid=(nch // cb,),
                                               in_specs=[spec, spec], out_specs=spec),
        compiler_params=_cparams(("parallel",)),
        name="positions",
    )(offs, idx3, rank3)


def _route_positions(idx3, rank3, counts, n_rows):
    tm = TM_X
    counts = counts.astype(jnp.int32)
    padded = ((counts + tm - 1) // tm) * tm
    ends = jnp.cumsum(padded)
    offs = ends - padded
    pos3 = _positions(offs.astype(jnp.int32), idx3, rank3)
    r_pad = n_rows * TOP_K + N_EXPERTS * tm
    nt = r_pad // tm
    tile_ids = jnp.arange(nt, dtype=jnp.int32)
    tile_expert = jnp.sum((ends // tm)[None, :] <= tile_ids[:, None], axis=1)
    tile_expert = jnp.minimum(tile_expert, N_EXPERTS - 1).astype(jnp.int32)
    n_used = (ends[-1] // tm).astype(jnp.int32).reshape(1)
    return pos3, tile_expert, n_used, r_pad


def _in_weights(w):
    def regroup(cols):
        return cols.reshape(D, 2, HEADS, QK_DIM).transpose(0, 2, 1, 3).reshape(D, 2 * HEADS * QK_DIM)
    return jnp.concatenate([w[:, :OFF_Q], regroup(w[:, OFF_Q:OFF_K]), regroup(w[:, OFF_K:OFF_V]), w[:, OFF_V:]],
                           axis=1).astype(BF16)


def _rope_tables():
    t = jnp.arange(L)
    row = (t // GRID_W).astype(F32)
    col = (t % GRID_W).astype(F32)
    n_freq = QK_DIM // 4
    inv = ROPE_BASE ** (-jnp.arange(n_freq, dtype=F32) / n_freq)
    ar = row[:, None] * inv
    ac = col[:, None] * inv
    cos64 = jnp.concatenate([jnp.cos(ar), jnp.cos(ar), jnp.cos(ac), jnp.cos(ac)], axis=1)
    sin64 = jnp.concatenate([-jnp.sin(ar), jnp.sin(ar), -jnp.sin(ac), jnp.sin(ac)], axis=1)
    cos_t = jnp.concatenate([jnp.tile(cos64, (1, 2)), jnp.ones((TM_IN, 128), F32)], axis=0)
    sin_t = jnp.concatenate([jnp.tile(sin64, (1, 2)), jnp.zeros((TM_IN, 128), F32)], axis=0)
    return cos_t, sin_t


def _split_bf16(w):
    hi = w.astype(BF16)
    return hi, (w - hi.astype(F32)).astype(BF16)


def kernel(x, c, ctx, c_ctx, w_ada, b_ada, g_norm1, g_norm2, w_in, w_out, g_v, w_s, b_s, w_conv, g_q, g_k,
           lam_q1, lam_k1, lam_q2, lam_k2, g_sub, w_router, b_router, w_gate, w_up, w_down,
           ws_gate, ws_up, ws_down):
    src = (x.reshape(NL, D), ctx.reshape(NC, D))
    cc = jnp.concatenate([c, c_ctx[None, :], jnp.zeros((MOD_ROWS - B - 1, D), F32)], axis=0)
    mod = _ada(cc, w_ada, b_ada)
    cos_t, sin_t = _rope_tables()
    bd = jnp.asarray(np.kron(np.eye(8, dtype=np.float32), np.full((64, 64), 1.0 / 64, np.float32)), BF16)

    for l in range(DEPTH):
        last = l == DEPTH - 1
        lam_init = 0.8 - 0.6 * math.exp(-0.3 * l)
        lam = (jnp.exp(jnp.sum(lam_q1[l] * lam_k1[l])) - jnp.exp(jnp.sum(lam_q2[l] * lam_k2[l])) + lam_init)
        bound = (QK_DIM * jnp.max(jnp.abs(g_q[l])) * jnp.max(jnp.abs(g_k[l]))
                 * (QK_DIM ** -0.5 * LOG2E) * ATT_BOUND_MARGIN)
        use_bound = (2.0 * bound < ATT_MAX_SHIFT_RANGE).astype(F32)
        lam = jnp.stack([lam, bound, use_bound]).astype(F32)
        mod3 = mod[l].reshape(MOD_ROWS, 1, 6 * D)
        w_in_bf = _in_weights(w_in[l])
        bias_t = jnp.repeat(b_s[l].T, A_GD, axis=1)
        wconv = jnp.concatenate([w_conv[l], jnp.zeros((5, B_WIDTH), F32)], axis=0)
        yab, qa, qb, kk, v = _in_mix(src[0], src[1], g_norm1[l], mod3, w_in_bf, cos_t, sin_t,
                                     g_v[l].reshape(1, A_WIDTH), w_s[l].astype(BF16), bias_t, wconv,
                                     jnp.tile(g_q[l], 8).reshape(1, 512), jnp.tile(g_k[l], 8).reshape(1, 512), bd)
        gsub = g_sub[l].reshape(1, V_DIM)
        coef = 1.0 - lam_init
        n_rows = NL if last else NR
        yc = _attention(lam, qa, qb, kk, v, gsub, coef, ctx_queries=False)
        yc_ctx = yc if last else _attention(lam, qa, qb, kk, v, gsub, coef, ctx_queries=True)
        wr_hi, wr_lo = _split_bf16(w_router[l].T)
        wo_bf = w_out[l].astype(BF16)
        ws_bf = (ws_gate[l].astype(BF16), ws_up[l].astype(BF16), ws_down[l].astype(BF16))
        xa, h2p, idx3, gate3, rank3, counts = _out_router(
            src[0], src[1], yab, yc, yc_ctx, wo_bf, g_norm2[l], mod3, wr_hi, wr_lo, b_router[l], n_rows)
        pos3, tile_expert, n_used, r_pad = _route_positions(idx3, rank3, counts[:, 0], n_rows)
        x_sorted = _sc_dispatch(h2p.reshape(n_rows, 4, 128), pos3, r_pad)
        y_sorted = _experts(l, tile_expert, n_used, x_sorted.reshape(r_pad * 4, 128), w_gate, w_up, w_down)
        yg = _sc_collect(y_sorted.reshape(r_pad, 4, 128), pos3)
        gates = gate3.transpose(0, 2, 1).reshape(n_rows, TOP_K)
        xa = _shared_residual(xa, h2p, yg.reshape(TOP_K, n_rows * 4, 128), gates, *ws_bf, mod3, n_rows)
        src = (xa, xa)
    return xa.reshape(B, L, D)
```

```python
import functools
import math

import numpy as np
import jax
import jax.numpy as jnp
from jax import lax
from jax.experimental import pallas as pl
from jax.experimental.pallas import tpu as pltpu
from jax.experimental.pallas import tpu_sc as plsc

F32 = jnp.float32
BF16 = jnp.bfloat16

D = 1024
B = 8
L = 2048
DEPTH = 2
GRID_W = 64
CTX = 256
A_WIDTH = 256
A_GROUPS = 4
A_GD = 64
CHUNK = 128
B_WIDTH = 256
C_WIDTH = 512
HEADS = 4
V_DIM = 128
QK_DIM = 64
ROPE_BASE = 10000.0
OFF_BB = 512
OFF_BC = 768
OFF_BX = 1024
OFF_Q = 1280
OFF_K = 1792
OFF_V = 2304
D_IN = 2816
N_EXPERTS = 64
TOP_K = 8
D_EXPERT = 256
D_SHARED = 256
ROUTED_SCALE = 2.5
EPS = 1e-6

NL = B * L
NC = B * CTX
NR = NL + NC
MOD_ROWS = 16
LOG2E = 1.4426950408889634

TM_IN = 512
TQ = 1024
ATT_CHAIN = 256
ATT_BOUND_MARGIN = 1.02
ATT_MAX_SHIFT_RANGE = 100.0
TM_OUT = 512
TM_X = 512
X_CHAINS = 2
X_RING = 3
TM_F = 512

_DN_T = (((1,), (1,)), ((), ()))


def _cparams(sem, vmem_mb=None):
    kw = dict(dimension_semantics=sem)
    if vmem_mb is not None:
        kw["vmem_limit_bytes"] = vmem_mb * 1024 * 1024
    return pltpu.CompilerParams(**kw)


def _mod_row(i, tm):
    return jnp.where(i < NL // tm, i // (L // tm), B)


def _mod_spec(tm, chunk):
    return pl.BlockSpec((None, 1, D), lambda i: (_mod_row(i, tm), 0, chunk))


def _ada_kernel(c_ref, w_ref, b_ref, o_ref):
    c = c_ref[...]
    cs = c * jax.nn.sigmoid(c)
    o_ref[...] = jnp.dot(cs, w_ref[...], preferred_element_type=F32,
                         precision=lax.Precision.HIGHEST) + b_ref[...]


def _ada(cc, w_ada, b_ada):
    nb = 6
    return pl.pallas_call(
        _ada_kernel,
        out_shape=jax.ShapeDtypeStruct((DEPTH, MOD_ROWS, 6 * D), F32),
        grid=(DEPTH, nb),
        in_specs=[pl.BlockSpec((MOD_ROWS, D), lambda l, j: (0, 0)),
                  pl.BlockSpec((None, D, D), lambda l, j: (l, 0, j)),
                  pl.BlockSpec((None, 1, D), lambda l, j: (l, 0, j))],
        out_specs=pl.BlockSpec((None, MOD_ROWS, D), lambda l, j: (l, 0, j)),
        compiler_params=_cparams(("arbitrary", "arbitrary"), 40),
        name="ada_mod",
    )(cc, w_ada, b_ada.reshape(DEPTH, 1, 6 * D))


def _rms_mod(x, g, sc, sh):
    ms = jnp.mean(x * x, axis=-1, keepdims=True)
    return x * lax.rsqrt(ms + EPS) * g * (1.0 + sc) + sh


def _two_source_specs(tm, n_first, width=D):
    return [pl.BlockSpec((tm, width), lambda i: (jnp.minimum(i, n_first - 1), 0)),
            pl.BlockSpec((tm, width), lambda i: (jnp.maximum(i - n_first, 0), 0))]


def _two_source_rows(a_ref, b_ref, n_first):
    return jnp.where(pl.program_id(0) < n_first, a_ref[...], b_ref[...])


def _group_rms(t, g, bd):
    ms = jnp.dot((t * t).astype(BF16), bd, preferred_element_type=F32)
    return t * lax.rsqrt(ms + EPS) * g


def _rope(t, cos, sin):
    w = t.shape[1]
    lane = lax.broadcasted_iota(jnp.int32, t.shape, 1)
    first = (lane % 32) < 16
    partner = jnp.where(first, pltpu.roll(t, w - 16, 1), pltpu.roll(t, 16, 1))
    cos4 = jnp.concatenate([cos] * (w // 128), axis=1)
    sin4 = jnp.concatenate([sin] * (w // 128), axis=1)
    return t * cos4 + partner * sin4


def _in_mix_kernel(xa_ref, xb_ref, pa_ref, pb_ref, na_ref, nb_ref, g_ref, sh_ref, sc_ref, w_ref,
                   cos_ref, sin_ref, gv_ref, ws_ref, bias_ref, wconv_ref, gq_ref, gk_ref, bd_ref,
                   yab_ref, qa_ref, qb_ref, kk_ref, v_ref, *, n_first):
    tm = TM_IN
    i = pl.program_id(0)
    tiles_per_seq = L // tm
    is_lat = i < NL // tm
    is_start = jnp.logical_or(jnp.logical_not(is_lat), i % tiles_per_seq == 0)
    is_end = jnp.logical_or(jnp.logical_not(is_lat), i % tiles_per_seq == tiles_per_seq - 1)
    first = i < n_first
    g, sc, sh = g_ref[...], sc_ref[...], sh_ref[...]

    h = _rms_mod(jnp.where(first, xa_ref[...], xb_ref[...]), g, sc, sh)
    p = jnp.dot(h.astype(BF16), w_ref[...], preferred_element_type=F32)
    v_ref[...] = p[:, OFF_V:].astype(BF16)
    halo = jnp.concatenate([jnp.where(first, pa_ref[...], pb_ref[...]),
                            jnp.where(first, na_ref[...], nb_ref[...])], axis=0)
    ph = jnp.dot(_rms_mod(halo, g, sc, sh).astype(BF16), w_ref[:, OFF_BC:OFF_Q], preferred_element_type=F32)
    zh = ph[:, :B_WIDTH] * ph[:, B_WIDTH:]
    zp = jnp.where(is_start, 0.0, zh[15:16])
    zn = jnp.where(is_end, 0.0, zh[16:17])

    uv = p[:, 0:2 * A_WIDTH]
    uv = 0.5 * uv * (1.0 + lax.erf(uv * (2.0 ** -0.5)))
    u = uv[:, :A_WIDTH]
    v = uv[:, A_WIDTH:]
    ms = jnp.mean(v * v, axis=-1, keepdims=True)
    vb = (v * lax.rsqrt(ms + EPS) * gv_ref[...]).astype(BF16)
    lane = lax.broadcasted_iota(jnp.int32, (CHUNK, 128), 1)
    mixes = []
    for c in range(tm // CHUNK):
        vc = vb[c * CHUNK:(c + 1) * CHUNK]
        halves = []
        for j in range(2):
            vj = vc[:, j * 128:(j + 1) * 128]
            m0 = jnp.dot(ws_ref[2 * j], vj, preferred_element_type=F32)
            m1 = jnp.dot(ws_ref[2 * j + 1], vj, preferred_element_type=F32)
            halves.append(jnp.where(lane < A_GD, m0, m1))
        mixes.append(jnp.concatenate(halves, axis=1) + bias_ref[...])
    ya = u * jnp.concatenate(mixes, axis=0)

    bg = p[:, OFF_BB:OFF_BC]
    z = p[:, OFF_BC:OFF_BX] * p[:, OFF_BX:OFF_Q]
    row = lax.broadcasted_iota(jnp.int32, z.shape, 0)
    inner = jnp.logical_not(is_lat)
    z_prev = jnp.where(row == 0, zp, pltpu.roll(z, 1, 0))
    z_prev = jnp.where(jnp.logical_and(inner, row % CTX == 0), 0.0, z_prev)
    z_next = jnp.where(row == tm - 1, zn, pltpu.roll(z, tm - 1, 0))
    z_next = jnp.where(jnp.logical_and(inner, row % CTX == CTX - 1), 0.0, z_next)
    yb = bg * (z_prev * wconv_ref[0:1] + z * wconv_ref[1:2] + z_next * wconv_ref[2:3])
    yab_ref[...] = jnp.concatenate([ya, yb], axis=1).astype(BF16)

    cos = cos_ref[...]
    sin = sin_ref[...]
    bd = bd_ref[...]
    q = _rope(_group_rms(p[:, OFF_Q:OFF_K], gq_ref[...], bd), cos, sin)
    q = q * (QK_DIM ** -0.5 * LOG2E)
    lane5 = lax.broadcasted_iota(jnp.int32, q.shape, 1) % 128
    qa_ref[...] = jnp.where(lane5 < QK_DIM, q, 0.0).astype(BF16)
    qb_ref[...] = jnp.where(lane5 >= QK_DIM, q, 0.0).astype(BF16)
    k = _rope(_group_rms(p[:, OFF_K:OFF_V], gk_ref[...], bd), cos, sin)
    kk_ref[...] = k.astype(BF16)


def _in_mix(x_first, x_second, g, mod3, w_bf, cos_t, sin_t, gv, ws_bf, bias_t, wconv, gq, gk, bd):
    tm = TM_IN
    n_first = x_first.shape[0] // tm
    hb = tm // 16
    nhb_first = x_first.shape[0] // 16
    nhb_second = x_second.shape[0] // 16
    pos_blocks = L // tm

    def tab_map(i):
        return (jnp.where(i < NL // tm, i % pos_blocks, pos_blocks), 0)

    def halo_specs(shift):
        blk = lambda i: (i * tm + shift) // 16
        return [pl.BlockSpec((16, D), lambda i: (jnp.clip(blk(i), 0, nhb_first - 1), 0)),
                pl.BlockSpec((16, D), lambda i: (jnp.clip(blk(i) - nhb_first, 0, nhb_second - 1), 0))]

    const2 = lambda i: (0, 0)
    row512 = pl.BlockSpec((tm, 512), lambda i: (i, 0))
    return pl.pallas_call(
        functools.partial(_in_mix_kernel, n_first=n_first),
        out_shape=[jax.ShapeDtypeStruct((NR, 512), BF16)] * 5,
        grid=(NR // tm,),
        in_specs=_two_source_specs(tm, n_first) + halo_specs(-1) + halo_specs(tm)
                 + [pl.BlockSpec((1, D), const2), _mod_spec(tm, 0), _mod_spec(tm, 1),
                    pl.BlockSpec((D, D_IN), const2),
                    pl.BlockSpec((tm, 128), tab_map), pl.BlockSpec((tm, 128), tab_map),
                    pl.BlockSpec((1, A_WIDTH), const2),
                    pl.BlockSpec((A_GROUPS, CHUNK, CHUNK), lambda i: (0, 0, 0)),
                    pl.BlockSpec((CHUNK, A_WIDTH), const2),
                    pl.BlockSpec((8, B_WIDTH), const2),
                    pl.BlockSpec((1, 512), const2), pl.BlockSpec((1, 512), const2),
                    pl.BlockSpec((512, 512), const2)],
        out_specs=[row512] * 5,
        compiler_params=_cparams(("parallel",), 56),
        name="in_mix",
    )(x_first, x_second, x_first, x_second, x_first, x_second, g.reshape(1, D), mod3, mod3, w_bf,
      cos_t, sin_t, gv, ws_bf, bias_t, wconv, gq, gk, bd)


def _attn_kernel(lam_ref, qa_ref, qb_ref, *rest, n_seg, coef, tq):
    kv_refs = rest[:2 * n_seg]
    gsub_ref, o_ref, k_scr, vt_scr = rest[2 * n_seg:]

    @pl.when(pl.program_id(2) == 0)
    def _():
        off = 0
        for s in range(n_seg):
            n = kv_refs[s].shape[0]
            k_scr[off:off + n, :] = kv_refs[s][...]
            vt_scr[0:V_DIM, off:off + n] = kv_refs[n_seg + s][...].astype(F32).T.astype(BF16)
            off += n
        ones_row = lax.broadcasted_iota(jnp.int32, (16, off), 0) == 0
        vt_scr[V_DIM:, :] = jnp.where(ones_row, 1.0, 0.0).astype(BF16)

    lam = lam_ref[0]
    shift = lam_ref[1]
    qc = ATT_CHAIN

    def scores(c):
        rows = slice(c * qc, (c + 1) * qc)
        qs = jnp.concatenate([qa_ref[rows, :], qb_ref[rows, :]], axis=0)
        return lax.dot_general(k_scr[...], qs, _DN_T, preferred_element_type=F32)

    def finish(c, pt):
        ot = jnp.dot(vt_scr[...], pt, preferred_element_type=F32)
        inv = 1.0 / ot[V_DIM:V_DIM + 1, :]
        dt = ot[0:V_DIM, :qc] * inv[:, :qc] - ot[0:V_DIM, qc:] * (lam * inv[:, qc:])
        o = dt.T
        ms = jnp.mean(o * o, axis=-1, keepdims=True)
        o_ref[c * qc:(c + 1) * qc, :] = (o * lax.rsqrt(ms + EPS) * gsub_ref[...] * coef).astype(o_ref.dtype)

    @pl.when(lam_ref[2] > 0.5)
    def _():
        for c in range(tq // qc):
            finish(c, jnp.exp2(scores(c) - shift).astype(BF16))

    @pl.when(lam_ref[2] <= 0.5)
    def _():
        sts = [scores(c) for c in range(tq // qc)]
        for c in range(tq // qc):
            st = sts[c]
            finish(c, jnp.exp2(st - jnp.max(st, axis=0, keepdims=True)).astype(BF16))


def _attention(lam, qa, qb, kk, v, gsub, coef, *, ctx_queries):
    if ctx_queries:
        tq = CTX
        nq, lk, n_seg = 1, CTX, 1
        q_map = lambda b, h, qi: (NL // tq + b, h)
        kv_specs = [pl.BlockSpec((CTX, 128), lambda b, h, qi: (NL // CTX + b, h)),
                    pl.BlockSpec((CTX, 128), lambda b, h, qi: (NL // CTX + b, h))]
        kv_args = [kk, v]
        rows = NC
        o_map = lambda b, h, qi: (b, h)
    else:
        tq = TQ
        nq, lk, n_seg = L // tq, CTX + L, 2
        q_map = lambda b, h, qi: (b * (L // tq) + qi, h)
        kv_specs = [pl.BlockSpec((CTX, 128), lambda b, h, qi: (NL // CTX + b, h)),
                    pl.BlockSpec((L, 128), lambda b, h, qi: (b, h)),
                    pl.BlockSpec((CTX, 128), lambda b, h, qi: (NL // CTX + b, h)),
                    pl.BlockSpec((L, 128), lambda b, h, qi: (b, h))]
        kv_args = [kk, kk, v, v]
        rows = NL
        o_map = lambda b, h, qi: (b * (L // tq) + qi, h)
    return pl.pallas_call(
        functools.partial(_attn_kernel, n_seg=n_seg, coef=coef, tq=tq),
        out_shape=jax.ShapeDtypeStruct((rows, C_WIDTH), BF16),
        grid=(B, HEADS, nq),
        in_specs=[pl.BlockSpec(memory_space=pltpu.SMEM),
                  pl.BlockSpec((tq, 128), q_map), pl.BlockSpec((tq, 128), q_map)]
                 + kv_specs + [pl.BlockSpec((1, V_DIM), lambda b, h, qi: (0, 0))],
        out_specs=pl.BlockSpec((tq, 128), o_map),
        scratch_shapes=[pltpu.VMEM((lk, 128), BF16), pltpu.VMEM((V_DIM + 16, lk), BF16)],
        compiler_params=_cparams(("parallel", "parallel", "arbitrary"), 56),
        name="attn_ctx" if ctx_queries else "attn_lat",
    )(lam, qa, qb, *kv_args, gsub)


def _pack_rows(t, out_ref, row0=0):
    half = D // 2
    w = pltpu.pack_elementwise([t[:, :half], t[:, half:]], packed_dtype=BF16)
    w = lax.bitcast_convert_type(w, jnp.uint32)
    rows = t.shape[0]
    for j in range(4):
        out_ref[pl.ds(4 * row0 + j, rows, stride=4), :] = w[:, j * 128:(j + 1) * 128]


def _unpack_rows(ref, rows, lead=None, row0=0):
    los, his = [], []
    for j in range(4):
        sl = pl.ds(4 * row0 + j, rows, stride=4)
        w = ref[sl, :] if lead is None else ref[lead, sl, :]
        los.append(pltpu.unpack_elementwise(w, index=0, packed_dtype=BF16, unpacked_dtype=F32))
        his.append(pltpu.unpack_elementwise(w, index=1, packed_dtype=BF16, unpacked_dtype=F32))
    return jnp.concatenate(los, axis=1), jnp.concatenate(his, axis=1)


def _out_router_kernel(xa_ref, xb_ref, yab_ref, yca_ref, ycb_ref, wo_ref, g1_ref, g2n_ref, sh2_ref, sc2_ref, wrh_ref,
                       wrl_ref, br_ref, xo_ref, h2p_ref, idx_ref, gate_ref, rank_ref, cnt_ref, run_ref, *,
                       n_first, n_first_c):
    tm = TM_OUT

    @pl.when(pl.program_id(0) == 0)
    def _():
        run_ref[...] = jnp.zeros_like(run_ref)

    yc = _two_source_rows(yca_ref, ycb_ref, n_first_c)
    y = jnp.dot(jnp.concatenate([yab_ref[...], yc], axis=1), wo_ref[...], preferred_element_type=F32)
    x = _two_source_rows(xa_ref, xb_ref, n_first) + g1_ref[...] * y
    xo_ref[...] = x
    h2 = _rms_mod(x, g2n_ref[...], sc2_ref[...], sh2_ref[...])
    hi = h2.astype(BF16)
    _pack_rows(h2, h2p_ref)
    lo = (h2 - hi.astype(F32)).astype(BF16)
    wh = wrh_ref[...]
    z = (lax.dot_general(wh, hi, _DN_T, preferred_element_type=F32)
         + lax.dot_general(wh, lo, _DN_T, preferred_element_type=F32)
         + lax.dot_general(wrl_ref[...], hi, _DN_T, preferred_element_type=F32))
    scores = jax.nn.sigmoid(z)
    work = scores + br_ref[...]
    eio = lax.broadcasted_iota(jnp.int32, work.shape, 0)
    idxs, sels, hits = [], [], []
    for _ in range(TOP_K):
        m = jnp.max(work, axis=0, keepdims=True)
        idx = jnp.min(jnp.where(work == m, eio, N_EXPERTS), axis=0, keepdims=True)
        hit = eio == idx
        sels.append(jnp.sum(jnp.where(hit, scores, 0.0), axis=0, keepdims=True))
        idxs.append(idx)
        hits.append(hit)
        work = jnp.where(hit, -jnp.inf, work)
    sel = jnp.concatenate(sels, axis=0)
    idx8 = jnp.concatenate(idxs, axis=0)
    gate8 = sel / jnp.sum(sel, axis=0, keepdims=True) * ROUTED_SCALE

    chosen = functools.reduce(jnp.logical_or, hits)
    before = (lax.broadcasted_iota(jnp.int32, (tm, tm), 0) < lax.broadcasted_iota(jnp.int32, (tm, tm), 1))
    prefix = jnp.dot(jnp.where(chosen, 1.0, 0.0).astype(BF16), jnp.where(before, 1.0, 0.0).astype(BF16),
                     preferred_element_type=F32)
    rank_dense = prefix + run_ref[:, 0:1]
    rank8 = jnp.concatenate([jnp.sum(jnp.where(h, rank_dense, 0.0), axis=0, keepdims=True) for h in hits],
                            axis=0).astype(jnp.int32)
    run = run_ref[...] + jnp.sum(jnp.where(chosen, 1.0, 0.0), axis=1, keepdims=True)
    run_ref[...] = run
    cnt_ref[...] = run
    for c in range(tm // 128):
        idx_ref[c] = idx8[:, c * 128:(c + 1) * 128]
        gate_ref[c] = gate8[:, c * 128:(c + 1) * 128]
        rank_ref[c] = rank8[:, c * 128:(c + 1) * 128]


def _out_router(x_first, x_second, yab, yc_first, yc_second, wo_bf, g2n, mod3, wr_hi, wr_lo, br, n_rows):
    tm = TM_OUT
    n_first = x_first.shape[0] // tm
    n_first_c = yc_first.shape[0] // tm
    const2 = lambda i: (0, 0)
    row = lambda i: (i, 0)
    chunk3 = pl.BlockSpec((tm // 128, TOP_K, 128), lambda i: (i, 0, 0))
    nch = n_rows // 128
    return pl.pallas_call(
        functools.partial(_out_router_kernel, n_first=n_first, n_first_c=n_first_c),
        out_shape=[jax.ShapeDtypeStruct((n_rows, D), F32),
                   jax.ShapeDtypeStruct((n_rows * 4, 128), jnp.uint32),
                   jax.ShapeDtypeStruct((nch, TOP_K, 128), jnp.int32),
                   jax.ShapeDtypeStruct((nch, TOP_K, 128), F32),
                   jax.ShapeDtypeStruct((nch, TOP_K, 128), jnp.int32),
                   jax.ShapeDtypeStruct((N_EXPERTS, 128), F32)],
        grid=(n_rows // tm,),
        in_specs=_two_source_specs(tm, n_first)
                 + [pl.BlockSpec((tm, 512), row)]
                 + _two_source_specs(tm, n_first_c, C_WIDTH)
                 + [pl.BlockSpec((D, D), const2),
                  _mod_spec(tm, 2),
                  pl.BlockSpec((1, D), const2), _mod_spec(tm, 3), _mod_spec(tm, 4),
                  pl.BlockSpec((N_EXPERTS, D), const2), pl.BlockSpec((N_EXPERTS, D), const2),
                  pl.BlockSpec((N_EXPERTS, 1), const2)],
        out_specs=[pl.BlockSpec((tm, D), row), pl.BlockSpec((tm * 4, 128), row),
                   chunk3, chunk3, chunk3, pl.BlockSpec((N_EXPERTS, 128), const2)],
        scratch_shapes=[pltpu.VMEM((N_EXPERTS, 128), F32)],
        compiler_params=_cparams(("arbitrary",), 48),
        name="out_router",
    )(x_first, x_second, yab, yc_first, yc_second, wo_bf, mod3, g2n.reshape(1, D), mod3, mod3, wr_hi, wr_lo,
      br.reshape(N_EXPERTS, 1))


def _experts_kernel(te_ref, nu_ref, x_hbm, wg_ref, wu_ref, wd_ref, y_ref, wgu_s, wd_s, x_ring, x_sems):
    i = pl.program_id(0)
    n_used = nu_ref[0]
    used = i < n_used
    new_expert = jnp.logical_or(i == 0, te_ref[i] != te_ref[jnp.maximum(i - 1, 0)])

    ahead = X_RING - 1

    def x_copy(t):
        slot = t % X_RING
        return pltpu.make_async_copy(x_hbm.at[pl.ds(t * (4 * TM_X), 4 * TM_X)], x_ring.at[slot], x_sems.at[slot])

    @pl.when(i == 0)
    def _():
        for t in range(ahead):
            @pl.when(t < n_used)
            def _():
                x_copy(t).start()

    @pl.when(i + ahead < n_used)
    def _():
        x_copy(i + ahead).start()

    @pl.when(jnp.logical_and(used, new_expert))
    def _():
        wgu_s[:, 0:D_EXPERT] = wg_ref[...].astype(BF16)
        wgu_s[:, D_EXPERT:] = wu_ref[...].astype(BF16)
        wd_s[...] = wd_ref[...].astype(BF16)

    @pl.when(used)
    def _():
        x_copy(i).wait()
        x_ref = x_ring.at[i % X_RING]
        rc = TM_X // X_CHAINS
        wgu = wgu_s[...]
        wd = wd_s[...]
        abs_ = []
        for c in range(X_CHAINS):
            x_lo, x_hi = _unpack_rows(x_ref, rc, row0=c * rc)
            x = jnp.concatenate([x_lo.astype(BF16), x_hi.astype(BF16)], axis=1)
            abs_.append(jnp.dot(x, wgu, preferred_element_type=F32))
        for c in range(X_CHAINS):
            a = abs_[c][:, :D_EXPERT]
            hid = (a * jax.nn.sigmoid(a) * abs_[c][:, D_EXPERT:]).astype(BF16)
            _pack_rows(jnp.dot(hid, wd, preferred_element_type=F32), y_ref, row0=c * rc)


def _experts(layer, tile_expert, n_used, x_sorted, w_gate, w_up, w_down):
    tm = TM_X
    r_pad = x_sorted.shape[0] // 4
    nt = r_pad // tm
    row = lambda i, te, nu: (jnp.minimum(i, nu[0] - 1), 0)
    wmap = lambda i, te, nu: (layer, te[i], 0, 0)
    grid_spec = pltpu.PrefetchScalarGridSpec(
        num_scalar_prefetch=2,
        grid=(nt,),
        in_specs=[pl.BlockSpec(memory_space=pl.ANY),
                  pl.BlockSpec((None, None, D, D_EXPERT), wmap),
                  pl.BlockSpec((None, None, D, D_EXPERT), wmap),
                  pl.BlockSpec((None, None, D_EXPERT, D), wmap)],
        out_specs=pl.BlockSpec((tm * 4, 128), row),
        scratch_shapes=[pltpu.VMEM((D, 2 * D_EXPERT), BF16), pltpu.VMEM((D_EXPERT, D), BF16),
                        pltpu.VMEM((X_RING, tm * 4, 128), jnp.uint32), pltpu.SemaphoreType.DMA((X_RING,))],
    )
    return pl.pallas_call(
        _experts_kernel,
        out_shape=jax.ShapeDtypeStruct((r_pad * 4, 128), jnp.uint32),
        grid_spec=grid_spec,
        compiler_params=_cparams(("arbitrary",), 48),
        name="experts",
    )(tile_expert, n_used, x_sorted, w_gate, w_up, w_down)


SC_CORES = 2
SC_SUBCORES = 16
SC_WORKERS = SC_CORES * SC_SUBCORES
SC_CHUNK = 128


def _sc_mesh():
    return plsc.VectorSubcoreMesh(core_axis_name="c", subcore_axis_name="s")


def _sc_params():
    return pltpu.CompilerParams(use_tc_tiling_on_sc=True)


def _sc_dispatch(h2p, pos3, r_pad):
    nch = pos3.shape[0]
    steps = -(-nch // SC_WORKERS)

    def body(h_hbm, pos_hbm, out_hbm, idx_v, rows_v, sem):
        wid = lax.axis_index("s") * SC_CORES + lax.axis_index("c")

        @pl.loop(0, steps)
        def _(s):
            ch = wid + s * SC_WORKERS

            @pl.when(ch < nch)
            def _():
                pltpu.sync_copy(pos_hbm.at[ch], idx_v)
                pltpu.sync_copy(h_hbm.at[pl.ds(ch * SC_CHUNK, SC_CHUNK)], rows_v)
                copies = [pltpu.async_copy(rows_v, out_hbm.at[idx_v.at[k]], sem) for k in range(TOP_K)]
                for cp in copies:
                    cp.wait()

    return pl.kernel(
        body,
        out_type=jax.ShapeDtypeStruct((r_pad, 4, 128), jnp.uint32),
        mesh=_sc_mesh(),
        scratch_types=[pltpu.VMEM((TOP_K, SC_CHUNK), jnp.int32),
                       pltpu.VMEM((SC_CHUNK, 4, 128), jnp.uint32),
                       pltpu.SemaphoreType.DMA],
        compiler_params=_sc_params(),
        name="sc_dispatch",
    )(h2p, pos3)


def _sc_collect(y_sorted, pos3):
    nch = pos3.shape[0]
    steps = -(-nch // SC_WORKERS)
    half = SC_CHUNK // 2
    units = [(k, hh) for k in range(TOP_K) for hh in range(2)]

    def body(y_hbm, pos_hbm, out_hbm, idx_v, rows_a, rows_b, sem_a, sem_b):
        wid = lax.axis_index("s") * SC_CORES + lax.axis_index("c")
        bufs = (rows_a, rows_b)
        sems = (sem_a, sem_b)

        def gather(u):
            k, hh = units[u]
            return pltpu.async_copy(y_hbm.at[idx_v.at[k, pl.ds(hh * half, half)]], bufs[u % 2], sems[u % 2])

        @pl.loop(0, steps)
        def _(s):
            ch = wid + s * SC_WORKERS

            @pl.when(ch < nch)
            def _():
                pltpu.sync_copy(pos_hbm.at[ch], idx_v)
                pending = gather(0)
                for u, (k, hh) in enumerate(units):
                    nxt = gather(u + 1) if u + 1 < len(units) else None
                    pending.wait()
                    pltpu.sync_copy(bufs[u % 2], out_hbm.at[k, pl.ds(ch * SC_CHUNK + hh * half, half)])
                    pending = nxt

    return pl.kernel(
        body,
        out_type=jax.ShapeDtypeStruct((TOP_K, nch * SC_CHUNK, 4, 128), jnp.uint32),
        mesh=_sc_mesh(),
        scratch_types=[pltpu.VMEM((TOP_K, SC_CHUNK), jnp.int32),
                       pltpu.VMEM((half, 4, 128), jnp.uint32),
                       pltpu.VMEM((half, 4, 128), jnp.uint32),
                       pltpu.SemaphoreType.DMA, pltpu.SemaphoreType.DMA],
        compiler_params=_sc_params(),
        name="sc_collect",
    )(y_sorted, pos3)


def _shared_kernel(x_ref, h2p_ref, yg_ref, gate_ref, wg_ref, wu_ref, wd_ref, g2_ref, o_ref):
    h_lo, h_hi = _unpack_rows(h2p_ref, TM_F)
    h = jnp.concatenate([h_lo.astype(BF16), h_hi.astype(BF16)], axis=1)
    a = jnp.dot(h, wg_ref[...], preferred_element_type=F32)
    b = jnp.dot(h, wu_ref[...], preferred_element_type=F32)
    hid = (a * jax.nn.sigmoid(a) * b).astype(BF16)
    f = jnp.dot(hid, wd_ref[...], preferred_element_type=F32)
    gate = gate_ref[...]
    f_lo = f[:, :D // 2]
    f_hi = f[:, D // 2:]
    for k in range(TOP_K):
        y_lo, y_hi = _unpack_rows(yg_ref, TM_F, lead=k)
        f_lo = f_lo + gate[:, k:k + 1] * y_lo
        f_hi = f_hi + gate[:, k:k + 1] * y_hi
    o_ref[...] = x_ref[...] + g2_ref[...] * jnp.concatenate([f_lo, f_hi], axis=1)


def _shared_residual(xa, h2p, yg, gates, wsg_bf, wsu_bf, wsd_bf, mod3, n_rows):
    tm = TM_F
    row = lambda i: (i, 0)
    const2 = lambda i: (0, 0)
    return pl.pallas_call(
        _shared_kernel,
        out_shape=jax.ShapeDtypeStruct((n_rows, D), F32),
        grid=(n_rows // tm,),
        in_specs=[pl.BlockSpec((tm, D), row), pl.BlockSpec((tm * 4, 128), row),
                  pl.BlockSpec((TOP_K, tm * 4, 128), lambda i: (0, i, 0)),
                  pl.BlockSpec((tm, TOP_K), row),
                  pl.BlockSpec((D, D_SHARED), const2), pl.BlockSpec((D, D_SHARED), const2),
                  pl.BlockSpec((D_SHARED, D), const2), _mod_spec(tm, 5)],
        out_specs=pl.BlockSpec((tm, D), row),
        compiler_params=_cparams(("parallel",), 48),
        name="shared_residual",
    )(xa, h2p, yg, gates, wsg_bf, wsu_bf, wsd_bf, mod3)


def _positions_kernel(offs_ref, idx_ref, rank_ref, pos_ref):
    idx = idx_ref[...]
    base = jnp.zeros_like(idx)
    for e in range(N_EXPERTS):
        base = jnp.where(idx == e, offs_ref[e], base)
    pos_ref[...] = rank_ref[...] + base


def _positions(offs, idx3, rank3):
    nch = idx3.shape[0]
    cb = 8
    spec = pl.BlockSpec((cb, TOP_K, 128), lambda i, offs: (i, 0, 0))
    return pl.pallas_call(
        _positions_kernel,
        out_shape=jax.ShapeDtypeStruct((nch, TOP_K, 128), jnp.int32),
        grid_spec=pltpu.PrefetchScalarGridSpec(num_scalar_prefetch=1, grid=(nch // cb,),
                                               in_specs=[spec, spec], out_specs=spec),
        compiler_params=_cparams(("parallel",)),
        name="positions",
    )(offs, idx3, rank3)


def _route_positions(idx3, rank3, counts, n_rows):
    tm = TM_X
    counts = counts.astype(jnp.int32)
    padded = ((counts + tm - 1) // tm) * tm
    ends = jnp.cumsum(padded)
    offs = ends - padded
    pos3 = _positions(offs.astype(jnp.int32), idx3, rank3)
    r_pad = n_rows * TOP_K + N_EXPERTS * tm
    nt = r_pad // tm
    tile_ids = jnp.arange(nt, dtype=jnp.int32)
    tile_expert = jnp.sum((ends // tm)[None, :] <= tile_ids[:, None], axis=1)
    tile_expert = jnp.minimum(tile_expert, N_EXPERTS - 1).astype(jnp.int32)
    n_used = (ends[-1] // tm).astype(jnp.int32).reshape(1)
    return pos3, tile_expert, n_used, r_pad


def _in_weights(w):
    def regroup(cols):
        return cols.reshape(D, 2, HEADS, QK_DIM).transpose(0, 2, 1, 3).reshape(D, 2 * HEADS * QK_DIM)
    return jnp.concatenate([w[:, :OFF_Q], regroup(w[:, OFF_Q:OFF_K]), regroup(w[:, OFF_K:OFF_V]), w[:, OFF_V:]],
                           axis=1).astype(BF16)


def _rope_tables():
    t = jnp.arange(L)
    row = (t // GRID_W).astype(F32)
    col = (t % GRID_W).astype(F32)
    n_freq = QK_DIM // 4
    inv = ROPE_BASE ** (-jnp.arange(n_freq, dtype=F32) / n_freq)
    ar = row[:, None] * inv
    ac = col[:, None] * inv
    cos64 = jnp.concatenate([jnp.cos(ar), jnp.cos(ar), jnp.cos(ac), jnp.cos(ac)], axis=1)
    sin64 = jnp.concatenate([-jnp.sin(ar), jnp.sin(ar), -jnp.sin(ac), jnp.sin(ac)], axis=1)
    cos_t = jnp.concatenate([jnp.tile(cos64, (1, 2)), jnp.ones((TM_IN, 128), F32)], axis=0)
    sin_t = jnp.concatenate([jnp.tile(sin64, (1, 2)), jnp.zeros((TM_IN, 128), F32)], axis=0)
    return cos_t, sin_t


def _split_bf16(w):
    hi = w.astype(BF16)
    return hi, (w - hi.astype(F32)).astype(BF16)


def kernel(x, c, ctx, c_ctx, w_ada, b_ada, g_norm1, g_norm2, w_in, w_out, g_v, w_s, b_s, w_conv, g_q, g_k,
           lam_q1, lam_k1, lam_q2, lam_k2, g_sub, w_router, b_router, w_gate, w_up, w_down,
           ws_gate, ws_up, ws_down):
    src = (x.reshape(NL, D), ctx.reshape(NC, D))
    cc = jnp.concatenate([c, c_ctx[None, :], jnp.zeros((MOD_ROWS - B - 1, D), F32)], axis=0)
    mod = _ada(cc, w_ada, b_ada)
    cos_t, sin_t = _rope_tables()
    bd = jnp.asarray(np.kron(np.eye(8, dtype=np.float32), np.full((64, 64), 1.0 / 64, np.float32)), BF16)

    for l in range(DEPTH):
        last = l == DEPTH - 1
        lam_init = 0.8 - 0.6 * math.exp(-0.3 * l)
        lam = (jnp.exp(jnp.sum(lam_q1[l] * lam_k1[l])) - jnp.exp(jnp.sum(lam_q2[l] * lam_k2[l])) + lam_init)
        bound = (QK_DIM * jnp.max(jnp.abs(g_q[l])) * jnp.max(jnp.abs(g_k[l]))
                 * (QK_DIM ** -0.5 * LOG2E) * ATT_BOUND_MARGIN)
        use_bound = (2.0 * bound < ATT_MAX_SHIFT_RANGE).astype(F32)
        lam = jnp.stack([lam, bound, use_bound]).astype(F32)
        mod3 = mod[l].reshape(MOD_ROWS, 1, 6 * D)
        w_in_bf = _in_weights(w_in[l])
        bias_t = jnp.repeat(b_s[l].T, A_GD, axis=1)
        wconv = jnp.concatenate([w_conv[l], jnp.zeros((5, B_WIDTH), F32)], axis=0)
        yab, qa, qb, kk, v = _in_mix(src[0], src[1], g_norm1[l], mod3, w_in_bf, cos_t, sin_t,
                                     g_v[l].reshape(1, A_WIDTH), w_s[l].astype(BF16), bias_t, wconv,
                                     jnp.tile(g_q[l], 8).reshape(1, 512), jnp.tile(g_k[l], 8).reshape(1, 512), bd)
        gsub = g_sub[l].reshape(1, V_DIM)
        coef = 1.0 - lam_init
        n_rows = NL if last else NR
        yc = _attention(lam, qa, qb, kk, v, gsub, coef, ctx_queries=False)
        yc_ctx = yc if last else _attention(lam, qa, qb, kk, v, gsub, coef, ctx_queries=True)
        wr_hi, wr_lo = _split_bf16(w_router[l].T)
        wo_bf = w_out[l].astype(BF16)
        ws_bf = (ws_gate[l].astype(BF16), ws_up[l].astype(BF16), ws_down[l].astype(BF16))
        xa, h2p, idx3, gate3, rank3, counts = _out_router(
            src[0], src[1], yab, yc, yc_ctx, wo_bf, g_norm2[l], mod3, wr_hi, wr_lo, b_router[l], n_rows)
        pos3, tile_expert, n_used, r_pad = _route_positions(idx3, rank3, counts[:, 0], n_rows)
        x_sorted = _sc_dispatch(h2p.reshape(n_rows, 4, 128), pos3, r_pad)
        y_sorted = _experts(l, tile_expert, n_used, x_sorted.reshape(r_pad * 4, 128), w_gate, w_up, w_down)
        yg = _sc_collect(y_sorted.reshape(r_pad, 4, 128), pos3)
        gates = gate3.transpose(0, 2, 1).reshape(n_rows, TOP_K)
        xa = _shared_residual(xa, h2p, yg.reshape(TOP_K, n_rows * 4, 128), gates, *ws_bf, mod3, n_rows)
        src = (xa, xa)
    return xa.reshape(B, L, D)
```

```python
import functools
import math

import numpy as np
import jax
import jax.numpy as jnp
from jax import lax
from jax.experimental import pallas as pl
from jax.experimental.pallas import tpu as pltpu
from jax.experimental.pallas import tpu_sc as plsc

F32 = jnp.float32
BF16 = jnp.bfloat16

D = 1024
B = 8
L = 2048
DEPTH = 2
GRID_W = 64
CTX = 256
A_WIDTH = 256
A_GROUPS = 4
A_GD = 64
CHUNK = 128
B_WIDTH = 256
C_WIDTH = 512
HEADS = 4
V_DIM = 128
QK_DIM = 64
ROPE_BASE = 10000.0
OFF_BB = 512
OFF_BC = 768
OFF_BX = 1024
OFF_Q = 1280
OFF_K = 1792
OFF_V = 2304
D_IN = 2816
N_EXPERTS = 64
TOP_K = 8
D_EXPERT = 256
D_SHARED = 256
ROUTED_SCALE = 2.5
EPS = 1e-6

NL = B * L
NC = B * CTX
NR = NL + NC
MOD_ROWS = 16
LOG2E = 1.4426950408889634

TM_IN = 512
TQ = 1024
ATT_CHAIN = 256
ATT_BOUND_MARGIN = 1.02
ATT_MAX_SHIFT_RANGE = 100.0
TM_OUT = 512
TM_X = 1024
X_CHAINS = 2
X_RING = 3
TM_F = 512

_DN_T = (((1,), (1,)), ((), ()))


def _cparams(sem, vmem_mb=None):
    kw = dict(dimension_semantics=sem)
    if vmem_mb is not None:
        kw["vmem_limit_bytes"] = vmem_mb * 1024 * 1024
    return pltpu.CompilerParams(**kw)


def _mod_row(i, tm):
    return jnp.where(i < NL // tm, i // (L // tm), B)


def _mod_spec(tm, chunk):
    return pl.BlockSpec((None, 1, D), lambda i: (_mod_row(i, tm), 0, chunk))


def _ada_kernel(c_ref, w_ref, b_ref, o_ref):
    c = c_ref[...]
    cs = c * jax.nn.sigmoid(c)
    o_ref[...] = jnp.dot(cs, w_ref[...], preferred_element_type=F32,
                         precision=lax.Precision.HIGHEST) + b_ref[...]


def _ada(cc, w_ada, b_ada):
    nb = 6
    return pl.pallas_call(
        _ada_kernel,
        out_shape=jax.ShapeDtypeStruct((DEPTH, MOD_ROWS, 6 * D), F32),
        grid=(DEPTH, nb),
        in_specs=[pl.BlockSpec((MOD_ROWS, D), lambda l, j: (0, 0)),
                  pl.BlockSpec((None, D, D), lambda l, j: (l, 0, j)),
                  pl.BlockSpec((None, 1, D), lambda l, j: (l, 0, j))],
        out_specs=pl.BlockSpec((None, MOD_ROWS, D), lambda l, j: (l, 0, j)),
        compiler_params=_cparams(("arbitrary", "arbitrary"), 40),
        name="ada_mod",
    )(cc, w_ada, b_ada.reshape(DEPTH, 1, 6 * D))


def _rms_mod(x, g, sc, sh):
    ms = jnp.mean(x * x, axis=-1, keepdims=True)
    return x * lax.rsqrt(ms + EPS) * (g * (1.0 + sc)) + sh


def _two_source_specs(tm, n_first, width=D):
    return [pl.BlockSpec((tm, width), lambda i: (jnp.minimum(i, n_first - 1), 0)),
            pl.BlockSpec((tm, width), lambda i: (jnp.maximum(i - n_first, 0), 0))]


def _two_source_rows(a_ref, b_ref, n_first):
    return jnp.where(pl.program_id(0) < n_first, a_ref[...], b_ref[...])


def _group_rms(t, g, bd):
    ms = jnp.dot((t * t).astype(BF16), bd, preferred_element_type=F32)
    return t * lax.rsqrt(ms + EPS) * g


def _rope(t, cos, sin):
    w = t.shape[1]
    lane = lax.broadcasted_iota(jnp.int32, t.shape, 1)
    first = (lane % 32) < 16
    partner = jnp.where(first, pltpu.roll(t, w - 16, 1), pltpu.roll(t, 16, 1))
    cos4 = jnp.concatenate([cos] * (w // 128), axis=1)
    sin4 = jnp.concatenate([sin] * (w // 128), axis=1)
    return t * cos4 + partner * sin4


def _in_mix_kernel(xa_ref, xb_ref, pa_ref, pb_ref, na_ref, nb_ref, g_ref, sh_ref, sc_ref, w_ref,
                   cos_ref, sin_ref, gv_ref, ws_ref, bias_ref, wconv_ref, gq_ref, gk_ref, bd_ref,
                   yab_ref, qa_ref, qb_ref, kk_ref, v_ref, *, n_first):
    tm = TM_IN
    i = pl.program_id(0)
    tiles_per_seq = L // tm
    is_lat = i < NL // tm
    is_start = jnp.logical_or(jnp.logical_not(is_lat), i % tiles_per_seq == 0)
    is_end = jnp.logical_or(jnp.logical_not(is_lat), i % tiles_per_seq == tiles_per_seq - 1)
    first = i < n_first
    g, sc, sh = g_ref[...], sc_ref[...], sh_ref[...]

    h = _rms_mod(jnp.where(first, xa_ref[...], xb_ref[...]), g, sc, sh)
    p = jnp.dot(h.astype(BF16), w_ref[...], preferred_element_type=F32)
    v_ref[...] = p[:, OFF_V:].astype(BF16)
    halo = jnp.concatenate([jnp.where(first, pa_ref[...], pb_ref[...]),
                            jnp.where(first, na_ref[...], nb_ref[...])], axis=0)
    ph = jnp.dot(_rms_mod(halo, g, sc, sh).astype(BF16), w_ref[:, OFF_BC:OFF_Q], preferred_element_type=F32)
    zh = ph[:, :B_WIDTH] * ph[:, B_WIDTH:]
    zp = jnp.where(is_start, 0.0, zh[15:16])
    zn = jnp.where(is_end, 0.0, zh[16:17])

    uv = p[:, 0:2 * A_WIDTH]
    uv = 0.5 * uv * (1.0 + lax.erf(uv * (2.0 ** -0.5)))
    u = uv[:, :A_WIDTH]
    v = uv[:, A_WIDTH:]
    ms = jnp.mean(v * v, axis=-1, keepdims=True)
    vb = (v * lax.rsqrt(ms + EPS) * gv_ref[...]).astype(BF16)
    lane = lax.broadcasted_iota(jnp.int32, (CHUNK, 128), 1)
    mixes = []
    for c in range(tm // CHUNK):
        vc = vb[c * CHUNK:(c + 1) * CHUNK]
        halves = []
        for j in range(2):
            vj = vc[:, j * 128:(j + 1) * 128]
            m0 = jnp.dot(ws_ref[2 * j], vj, preferred_element_type=F32)
            m1 = jnp.dot(ws_ref[2 * j + 1], vj, preferred_element_type=F32)
            halves.append(jnp.where(lane < A_GD, m0, m1))
        mixes.append(jnp.concatenate(halves, axis=1) + bias_ref[...])
    ya = u * jnp.concatenate(mixes, axis=0)

    bg = p[:, OFF_BB:OFF_BC]
    z = p[:, OFF_BC:OFF_BX] * p[:, OFF_BX:OFF_Q]
    row = lax.broadcasted_iota(jnp.int32, z.shape, 0)
    inner = jnp.logical_not(is_lat)
    z_prev = jnp.where(row == 0, zp, pltpu.roll(z, 1, 0))
    z_prev = jnp.where(jnp.logical_and(inner, row % CTX == 0), 0.0, z_prev)
    z_next = jnp.where(row == tm - 1, zn, pltpu.roll(z, tm - 1, 0))
    z_next = jnp.where(jnp.logical_and(inner, row % CTX == CTX - 1), 0.0, z_next)
    yb = bg * (z_prev * wconv_ref[0:1] + z * wconv_ref[1:2] + z_next * wconv_ref[2:3])
    yab_ref[...] = jnp.concatenate([ya, yb], axis=1).astype(BF16)

    cos = cos_ref[...]
    sin = sin_ref[...]
    bd = bd_ref[...]
    q = _rope(_group_rms(p[:, OFF_Q:OFF_K], gq_ref[...], bd), cos, sin)
    q = q * (QK_DIM ** -0.5 * LOG2E)
    lane5 = lax.broadcasted_iota(jnp.int32, q.shape, 1) % 128
    qa_ref[...] = jnp.where(lane5 < QK_DIM, q, 0.0).astype(BF16)
    qb_ref[...] = jnp.where(lane5 >= QK_DIM, q, 0.0).astype(BF16)
    k = _rope(_group_rms(p[:, OFF_K:OFF_V], gk_ref[...], bd), cos, sin)
    kk_ref[...] = k.astype(BF16)


def _in_mix(x_first, x_second, g, mod3, w_bf, cos_t, sin_t, gv, ws_bf, bias_t, wconv, gq, gk, bd):
    tm = TM_IN
    n_first = x_first.shape[0] // tm
    hb = tm // 16
    nhb_first = x_first.shape[0] // 16
    nhb_second = x_second.shape[0] // 16
    pos_blocks = L // tm

    def tab_map(i):
        return (jnp.where(i < NL // tm, i % pos_blocks, pos_blocks), 0)

    def halo_specs(shift):
        blk = lambda i: (i * tm + shift) // 16
        return [pl.BlockSpec((16, D), lambda i: (jnp.clip(blk(i), 0, nhb_first - 1), 0)),
                pl.BlockSpec((16, D), lambda i: (jnp.clip(blk(i) - nhb_first, 0, nhb_second - 1), 0))]

    const2 = lambda i: (0, 0)
    row512 = pl.BlockSpec((tm, 512), lambda i: (i, 0))
    return pl.pallas_call(
        functools.partial(_in_mix_kernel, n_first=n_first),
        out_shape=[jax.ShapeDtypeStruct((NR, 512), BF16)] * 5,
        grid=(NR // tm,),
        in_specs=_two_source_specs(tm, n_first) + halo_specs(-1) + halo_specs(tm)
                 + [pl.BlockSpec((1, D), const2), _mod_spec(tm, 0), _mod_spec(tm, 1),
                    pl.BlockSpec((D, D_IN), const2),
                    pl.BlockSpec((tm, 128), tab_map), pl.BlockSpec((tm, 128), tab_map),
                    pl.BlockSpec((1, A_WIDTH), const2),
                    pl.BlockSpec((A_GROUPS, CHUNK, CHUNK), lambda i: (0, 0, 0)),
                    pl.BlockSpec((CHUNK, A_WIDTH), const2),
                    pl.BlockSpec((8, B_WIDTH), const2),
                    pl.BlockSpec((1, 512), const2), pl.BlockSpec((1, 512), const2),
                    pl.BlockSpec((512, 512), const2)],
        out_specs=[row512] * 5,
        compiler_params=_cparams(("parallel",), 56),
        name="in_mix",
    )(x_first, x_second, x_first, x_second, x_first, x_second, g.reshape(1, D), mod3, mod3, w_bf,
      cos_t, sin_t, gv, ws_bf, bias_t, wconv, gq, gk, bd)


def _attn_kernel(lam_ref, qa_ref, qb_ref, *rest, n_seg, coef, tq):
    kv_refs = rest[:2 * n_seg]
    gsub_ref, o_ref, k_scr, vt_scr = rest[2 * n_seg:]

    @pl.when(pl.program_id(2) == 0)
    def _():
        off = 0
        for s in range(n_seg):
            n = kv_refs[s].shape[0]
            k_scr[off:off + n, :] = kv_refs[s][...]
            vt_scr[0:V_DIM, off:off + n] = kv_refs[n_seg + s][...].astype(F32).T.astype(BF16)
            off += n
        ones_row = lax.broadcasted_iota(jnp.int32, (16, off), 0) == 0
        vt_scr[V_DIM:, :] = jnp.where(ones_row, 1.0, 0.0).astype(BF16)

    lam = lam_ref[0]
    shift = lam_ref[1]
    qc = ATT_CHAIN

    def scores(c):
        rows = slice(c * qc, (c + 1) * qc)
        qs = jnp.concatenate([qa_ref[rows, :], qb_ref[rows, :]], axis=0)
        return lax.dot_general(k_scr[...], qs, _DN_T, preferred_element_type=F32)

    def finish(c, pt):
        ot = jnp.dot(vt_scr[...], pt, preferred_element_type=F32)
        inv = 1.0 / ot[V_DIM:V_DIM + 1, :]
        dt = ot[0:V_DIM, :qc] * inv[:, :qc] - ot[0:V_DIM, qc:] * (lam * inv[:, qc:])
        o = dt.T
        ms = jnp.mean(o * o, axis=-1, keepdims=True)
        o_ref[c * qc:(c + 1) * qc, :] = (o * lax.rsqrt(ms + EPS) * gsub_ref[...] * coef).astype(o_ref.dtype)

    @pl.when(lam_ref[2] > 0.5)
    def _():
        for c in range(tq // qc):
            finish(c, jnp.exp2(scores(c) - shift).astype(BF16))

    @pl.when(lam_ref[2] <= 0.5)
    def _():
        sts = [scores(c) for c in range(tq // qc)]
        for c in range(tq // qc):
            st = sts[c]
            finish(c, jnp.exp2(st - jnp.max(st, axis=0, keepdims=True)).astype(BF16))


def _attention(lam, qa, qb, kk, v, gsub, coef, *, ctx_queries):
    if ctx_queries:
        tq = CTX
        nq, lk, n_seg = 1, CTX, 1
        q_map = lambda b, h, qi: (NL // tq + b, h)
        kv_specs = [pl.BlockSpec((CTX, 128), lambda b, h, qi: (NL // CTX + b, h)),
                    pl.BlockSpec((CTX, 128), lambda b, h, qi: (NL // CTX + b, h))]
        kv_args = [kk, v]
        rows = NC
        o_map = lambda b, h, qi: (b, h)
    else:
        tq = TQ
        nq, lk, n_seg = L // tq, CTX + L, 2
        q_map = lambda b, h, qi: (b * (L // tq) + qi, h)
        kv_specs = [pl.BlockSpec((CTX, 128), lambda b, h, qi: (NL // CTX + b, h)),
                    pl.BlockSpec((L, 128), lambda b, h, qi: (b, h)),
                    pl.BlockSpec((CTX, 128), lambda b, h, qi: (NL // CTX + b, h)),
                    pl.BlockSpec((L, 128), lambda b, h, qi: (b, h))]
        kv_args = [kk, kk, v, v]
        rows = NL
        o_map = lambda b, h, qi: (b * (L // tq) + qi, h)
    return pl.pallas_call(
        functools.partial(_attn_kernel, n_seg=n_seg, coef=coef, tq=tq),
        out_shape=jax.ShapeDtypeStruct((rows, C_WIDTH), BF16),
        grid=(B, HEADS, nq),
        in_specs=[pl.BlockSpec(memory_space=pltpu.SMEM),
                  pl.BlockSpec((tq, 128), q_map), pl.BlockSpec((tq, 128), q_map)]
                 + kv_specs + [pl.BlockSpec((1, V_DIM), lambda b, h, qi: (0, 0))],
        out_specs=pl.BlockSpec((tq, 128), o_map),
        scratch_shapes=[pltpu.VMEM((lk, 128), BF16), pltpu.VMEM((V_DIM + 16, lk), BF16)],
        compiler_params=_cparams(("parallel", "parallel", "arbitrary"), 56),
        name="attn_ctx" if ctx_queries else "attn_lat",
    )(lam, qa, qb, *kv_args, gsub)


def _pack_rows(t, out_ref, row0=0):
    half = D // 2
    w = pltpu.pack_elementwise([t[:, :half], t[:, half:]], packed_dtype=BF16)
    w = lax.bitcast_convert_type(w, jnp.uint32)
    rows = t.shape[0]
    for j in range(4):
        out_ref[pl.ds(4 * row0 + j, rows, stride=4), :] = w[:, j * 128:(j + 1) * 128]


def _unpack_rows(ref, rows, lead=None, row0=0):
    los, his = [], []
    for j in range(4):
        sl = pl.ds(4 * row0 + j, rows, stride=4)
        w = ref[sl, :] if lead is None else ref[lead, sl, :]
        los.append(pltpu.unpack_elementwise(w, index=0, packed_dtype=BF16, unpacked_dtype=F32))
        his.append(pltpu.unpack_elementwise(w, index=1, packed_dtype=BF16, unpacked_dtype=F32))
    return jnp.concatenate(los, axis=1), jnp.concatenate(his, axis=1)


def _out_router_kernel(xa_ref, xb_ref, yab_ref, yca_ref, ycb_ref, wo_ref, g1_ref, g2n_ref, sh2_ref, sc2_ref, wrh_ref,
                       wrl_ref, br_ref, xo_ref, h2p_ref, idx_ref, gate_ref, rank_ref, cnt_ref, run_ref, *,
                       n_first, n_first_c):
    tm = TM_OUT

    @pl.when(pl.program_id(0) == 0)
    def _():
        run_ref[...] = jnp.zeros_like(run_ref)

    yc = _two_source_rows(yca_ref, ycb_ref, n_first_c)
    y = jnp.dot(jnp.concatenate([yab_ref[...], yc], axis=1), wo_ref[...], preferred_element_type=F32)
    x = _two_source_rows(xa_ref, xb_ref, n_first) + g1_ref[...] * y
    xo_ref[...] = x
    h2 = _rms_mod(x, g2n_ref[...], sc2_ref[...], sh2_ref[...])
    hi = h2.astype(BF16)
    _pack_rows(h2, h2p_ref)
    lo = (h2 - hi.astype(F32)).astype(BF16)
    wh = wrh_ref[...]
    z = (lax.dot_general(wh, hi, _DN_T, preferred_element_type=F32)
         + lax.dot_general(wh, lo, _DN_T, preferred_element_type=F32)
         + lax.dot_general(wrl_ref[...], hi, _DN_T, preferred_element_type=F32))
    scores = jax.nn.sigmoid(z)
    work = scores + br_ref[...]
    eio = lax.broadcasted_iota(jnp.int32, work.shape, 0)
    idxs, sels, hits = [], [], []
    for _ in range(TOP_K):
        m = jnp.max(work, axis=0, keepdims=True)
        idx = jnp.min(jnp.where(work == m, eio, N_EXPERTS), axis=0, keepdims=True)
        hit = eio == idx
        sels.append(jnp.sum(jnp.where(hit, scores, 0.0), axis=0, keepdims=True))
        idxs.append(idx)
        hits.append(hit)
        work = jnp.where(hit, -jnp.inf, work)
    sel = jnp.concatenate(sels, axis=0)
    idx8 = jnp.concatenate(idxs, axis=0)
    gate8 = sel / jnp.sum(sel, axis=0, keepdims=True) * ROUTED_SCALE

    chosen = functools.reduce(jnp.logical_or, hits)
    before = (lax.broadcasted_iota(jnp.int32, (tm, tm), 0) < lax.broadcasted_iota(jnp.int32, (tm, tm), 1))
    prefix = jnp.dot(jnp.where(chosen, 1.0, 0.0).astype(BF16), jnp.where(before, 1.0, 0.0).astype(BF16),
                     preferred_element_type=F32)
    rank_dense = prefix + run_ref[:, 0:1]
    rank8 = jnp.concatenate([jnp.sum(jnp.where(h, rank_dense, 0.0), axis=0, keepdims=True) for h in hits],
                            axis=0).astype(jnp.int32)
    run = run_ref[...] + jnp.sum(jnp.where(chosen, 1.0, 0.0), axis=1, keepdims=True)
    run_ref[...] = run
    cnt_ref[...] = run
    for c in range(tm // 128):
        idx_ref[c] = idx8[:, c * 128:(c + 1) * 128]
        gate_ref[c] = gate8[:, c * 128:(c + 1) * 128]
        rank_ref[c] = rank8[:, c * 128:(c + 1) * 128]


def _out_router(x_first, x_second, yab, yc_first, yc_second, wo_bf, g2n, mod3, wr_hi, wr_lo, br, n_rows):
    tm = TM_OUT
    n_first = x_first.shape[0] // tm
    n_first_c = yc_first.shape[0] // tm
    const2 = lambda i: (0, 0)
    row = lambda i: (i, 0)
    chunk3 = pl.BlockSpec((tm // 128, TOP_K, 128), lambda i: (i, 0, 0))
    nch = n_rows // 128
    return pl.pallas_call(
        functools.partial(_out_router_kernel, n_first=n_first, n_first_c=n_first_c),
        out_shape=[jax.ShapeDtypeStruct((n_rows, D), F32),
                   jax.ShapeDtypeStruct((n_rows * 4, 128), jnp.uint32),
                   jax.ShapeDtypeStruct((nch, TOP_K, 128), jnp.int32),
                   jax.ShapeDtypeStruct((nch, TOP_K, 128), F32),
                   jax.ShapeDtypeStruct((nch, TOP_K, 128), jnp.int32),
                   jax.ShapeDtypeStruct((N_EXPERTS, 128), F32)],
        grid=(n_rows // tm,),
        in_specs=_two_source_specs(tm, n_first)
                 + [pl.BlockSpec((tm, 512), row)]
                 + _two_source_specs(tm, n_first_c, C_WIDTH)
                 + [pl.BlockSpec((D, D), const2),
                  _mod_spec(tm, 2),
                  pl.BlockSpec((1, D), const2), _mod_spec(tm, 3), _mod_spec(tm, 4),
                  pl.BlockSpec((N_EXPERTS, D), const2), pl.BlockSpec((N_EXPERTS, D), const2),
                  pl.BlockSpec((N_EXPERTS, 1), const2)],
        out_specs=[pl.BlockSpec((tm, D), row), pl.BlockSpec((tm * 4, 128), row),
                   chunk3, chunk3, chunk3, pl.BlockSpec((N_EXPERTS, 128), const2)],
        scratch_shapes=[pltpu.VMEM((N_EXPERTS, 128), F32)],
        compiler_params=_cparams(("arbitrary",), 48),
        name="out_router",
    )(x_first, x_second, yab, yc_first, yc_second, wo_bf, mod3, g2n.reshape(1, D), mod3, mod3, wr_hi, wr_lo,
      br.reshape(N_EXPERTS, 1))


def _experts_kernel(te_ref, nu_ref, x_hbm, wg_ref, wu_ref, wd_ref, y_ref, wgu_s, wd_s, x_ring, x_sems):
    i = pl.program_id(0)
    n_used = nu_ref[0]
    used = i < n_used
    new_expert = jnp.logical_or(i == 0, te_ref[i] != te_ref[jnp.maximum(i - 1, 0)])

    ahead = X_RING - 1

    def x_copy(t):
        slot = t % X_RING
        return pltpu.make_async_copy(x_hbm.at[pl.ds(t * (4 * TM_X), 4 * TM_X)], x_ring.at[slot], x_sems.at[slot])

    @pl.when(i == 0)
    def _():
        for t in range(ahead):
            @pl.when(t < n_used)
            def _():
                x_copy(t).start()

    @pl.when(i + ahead < n_used)
    def _():
        x_copy(i + ahead).start()

    @pl.when(jnp.logical_and(used, new_expert))
    def _():
        wgu_s[:, 0:D_EXPERT] = wg_ref[...].astype(BF16)
        wgu_s[:, D_EXPERT:] = wu_ref[...].astype(BF16)
        wd_s[...] = wd_ref[...].astype(BF16)

    @pl.when(used)
    def _():
        x_copy(i).wait()
        x_ref = x_ring.at[i % X_RING]
        rc = TM_X // X_CHAINS
        wgu = wgu_s[...]
        wd = wd_s[...]
        abs_ = []
        for c in range(X_CHAINS):
            x_lo, x_hi = _unpack_rows(x_ref, rc, row0=c * rc)
            x = jnp.concatenate([x_lo.astype(BF16), x_hi.astype(BF16)], axis=1)
            abs_.append(jnp.dot(x, wgu, preferred_element_type=F32))
        for c in range(X_CHAINS):
            a = abs_[c][:, :D_EXPERT]
            hid = (a * jax.nn.sigmoid(a) * abs_[c][:, D_EXPERT:]).astype(BF16)
            _pack_rows(jnp.dot(hid, wd, preferred_element_type=F32), y_ref, row0=c * rc)


def _experts(layer, tile_expert, n_used, x_sorted, w_gate, w_up, w_down):
    tm = TM_X
    r_pad = x_sorted.shape[0] // 4
    nt = r_pad // tm
    row = lambda i, te, nu: (jnp.minimum(i, nu[0] - 1), 0)
    wmap = lambda i, te, nu: (layer, te[i], 0, 0)
    grid_spec = pltpu.PrefetchScalarGridSpec(
        num_scalar_prefetch=2,
        grid=(nt,),
        in_specs=[pl.BlockSpec(memory_space=pl.ANY),
                  pl.BlockSpec((None, None, D, D_EXPERT), wmap),
                  pl.BlockSpec((None, None, D, D_EXPERT), wmap),
                  pl.BlockSpec((None, None, D_EXPERT, D), wmap)],
        out_specs=pl.BlockSpec((tm * 4, 128), row),
        scratch_shapes=[pltpu.VMEM((D, 2 * D_EXPERT), BF16), pltpu.VMEM((D_EXPERT, D), BF16),
                        pltpu.VMEM((X_RING, tm * 4, 128), jnp.uint32), pltpu.SemaphoreType.DMA((X_RING,))],
    )
    return pl.pallas_call(
        _experts_kernel,
        out_shape=jax.ShapeDtypeStruct((r_pad * 4, 128), jnp.uint32),
        grid_spec=grid_spec,
        compiler_params=_cparams(("arbitrary",), 48),
        name="experts",
    )(tile_expert, n_used, x_sorted, w_gate, w_up, w_down)


SC_CORES = 2
SC_SUBCORES = 16
SC_WORKERS = SC_CORES * SC_SUBCORES
SC_CHUNK = 128


def _sc_mesh():
    return plsc.VectorSubcoreMesh(core_axis_name="c", subcore_axis_name="s")


def _sc_params():
    return pltpu.CompilerParams(use_tc_tiling_on_sc=True)


def _sc_dispatch(h2p, pos3, r_pad):
    nch = pos3.shape[0]
    steps = -(-nch // SC_WORKERS)

    def body(h_hbm, pos_hbm, out_hbm, idx_v, rows_v, sem):
        wid = lax.axis_index("s") * SC_CORES + lax.axis_index("c")

        @pl.loop(0, steps)
        def _(s):
            ch = wid + s * SC_WORKERS

            @pl.when(ch < nch)
            def _():
                pltpu.sync_copy(pos_hbm.at[ch], idx_v)
                pltpu.sync_copy(h_hbm.at[pl.ds(ch * SC_CHUNK, SC_CHUNK)], rows_v)
                copies = [pltpu.async_copy(rows_v, out_hbm.at[idx_v.at[k]], sem) for k in range(TOP_K)]
                for cp in copies:
                    cp.wait()

    return pl.kernel(
        body,
        out_type=jax.ShapeDtypeStruct((r_pad, 4, 128), jnp.uint32),
        mesh=_sc_mesh(),
        scratch_types=[pltpu.VMEM((TOP_K, SC_CHUNK), jnp.int32),
                       pltpu.VMEM((SC_CHUNK, 4, 128), jnp.uint32),
                       pltpu.SemaphoreType.DMA],
        compiler_params=_sc_params(),
        name="sc_dispatch",
    )(h2p, pos3)


def _sc_collect(y_sorted, pos3):
    nch = pos3.shape[0]
    steps = -(-nch // SC_WORKERS)
    half = SC_CHUNK // 2
    units = [(k, hh) for k in range(TOP_K) for hh in range(2)]

    def body(y_hbm, pos_hbm, out_hbm, idx_v, rows_a, rows_b, sem_a, sem_b):
        wid = lax.axis_index("s") * SC_CORES + lax.axis_index("c")
        bufs = (rows_a, rows_b)
        sems = (sem_a, sem_b)

        def gather(u):
            k, hh = units[u]
            return pltpu.async_copy(y_hbm.at[idx_v.at[k, pl.ds(hh * half, half)]], bufs[u % 2], sems[u % 2])

        @pl.loop(0, steps)
        def _(s):
            ch = wid + s * SC_WORKERS

            @pl.when(ch < nch)
            def _():
                pltpu.sync_copy(pos_hbm.at[ch], idx_v)
                pending = gather(0)
                for u, (k, hh) in enumerate(units):
                    nxt = gather(u + 1) if u + 1 < len(units) else None
                    pending.wait()
                    pltpu.sync_copy(bufs[u % 2], out_hbm.at[k, pl.ds(ch * SC_CHUNK + hh * half, half)])
                    pending = nxt

    return pl.kernel(
        body,
        out_type=jax.ShapeDtypeStruct((TOP_K, nch * SC_CHUNK, 4, 128), jnp.uint32),
        mesh=_sc_mesh(),
        scratch_types=[pltpu.VMEM((TOP_K, SC_CHUNK), jnp.int32),
                       pltpu.VMEM((half, 4, 128), jnp.uint32),
                       pltpu.VMEM((half, 4, 128), jnp.uint32),
                       pltpu.SemaphoreType.DMA, pltpu.SemaphoreType.DMA],
        compiler_params=_sc_params(),
        name="sc_collect",
    )(y_sorted, pos3)


def _shared_kernel(x_ref, h2p_ref, yg_ref, gate_ref, wg_ref, wu_ref, wd_ref, g2_ref, o_ref):
    h_lo, h_hi = _unpack_rows(h2p_ref, TM_F)
    h = jnp.concatenate([h_lo.astype(BF16), h_hi.astype(BF16)], axis=1)
    a = jnp.dot(h, wg_ref[...], preferred_element_type=F32)
    b = jnp.dot(h, wu_ref[...], preferred_element_type=F32)
    hid = (a * jax.nn.sigmoid(a) * b).astype(BF16)
    f = jnp.dot(hid, wd_ref[...], preferred_element_type=F32)
    gate = gate_ref[...]
    f_lo = f[:, :D // 2]
    f_hi = f[:, D // 2:]
    for k in range(TOP_K):
        y_lo, y_hi = _unpack_rows(yg_ref, TM_F, lead=k)
        f_lo = f_lo + gate[:, k:k + 1] * y_lo
        f_hi = f_hi + gate[:, k:k + 1] * y_hi
    o_ref[...] = x_ref[...] + g2_ref[...] * jnp.concatenate([f_lo, f_hi], axis=1)


def _shared_residual(xa, h2p, yg, gates, wsg_bf, wsu_bf, wsd_bf, mod3, n_rows):
    tm = TM_F
    row = lambda i: (i, 0)
    const2 = lambda i: (0, 0)
    return pl.pallas_call(
        _shared_kernel,
        out_shape=jax.ShapeDtypeStruct((n_rows, D), F32),
        grid=(n_rows // tm,),
        in_specs=[pl.BlockSpec((tm, D), row), pl.BlockSpec((tm * 4, 128), row),
                  pl.BlockSpec((TOP_K, tm * 4, 128), lambda i: (0, i, 0)),
                  pl.BlockSpec((tm, TOP_K), row),
                  pl.BlockSpec((D, D_SHARED), const2), pl.BlockSpec((D, D_SHARED), const2),
                  pl.BlockSpec((D_SHARED, D), const2), _mod_spec(tm, 5)],
        out_specs=pl.BlockSpec((tm, D), row),
        compiler_params=_cparams(("parallel",), 48),
        name="shared_residual",
    )(xa, h2p, yg, gates, wsg_bf, wsu_bf, wsd_bf, mod3)


def _positions_kernel(offs_ref, idx_ref, rank_ref, pos_ref):
    idx = idx_ref[...]
    base = jnp.zeros_like(idx)
    for e in range(N_EXPERTS):
        base = jnp.where(idx == e, offs_ref[e], base)
    pos_ref[...] = rank_ref[...] + base


def _positions(offs, idx3, rank3):
    nch = idx3.shape[0]
    cb = 8
    spec = pl.BlockSpec((cb, TOP_K, 128), lambda i, offs: (i, 0, 0))
    return pl.pallas_call(
        _positions_kernel,
        out_shape=jax.ShapeDtypeStruct((nch, TOP_K, 128), jnp.int32),
        grid_spec=pltpu.PrefetchScalarGridSpec(num_scalar_prefetch=1, grid=(nch // cb,),
                                               in_specs=[spec, spec], out_specs=spec),
        compiler_params=_cparams(("parallel",)),
        name="positions",
    )(offs, idx3, rank3)


def _route_positions(idx3, rank3, counts, n_rows):
    tm = TM_X
    counts = counts.astype(jnp.int32)
    padded = ((counts + tm - 1) // tm) * tm
    ends = jnp.cumsum(padded)
    offs = ends - padded
    pos3 = _positions(offs.astype(jnp.int32), idx3, rank3)
    r_pad = n_rows * TOP_K + N_EXPERTS * tm
    nt = r_pad // tm
    tile_ids = jnp.arange(nt, dtype=jnp.int32)
    tile_expert = jnp.sum((ends // tm)[None, :] <= tile_ids[:, None], axis=1)
    tile_expert = jnp.minimum(tile_expert, N_EXPERTS - 1).astype(jnp.int32)
    n_used = (ends[-1] // tm).astype(jnp.int32).reshape(1)
    return pos3, tile_expert, n_used, r_pad


def _in_weights(w):
    def regroup(cols):
        return cols.reshape(D, 2, HEADS, QK_DIM).transpose(0, 2, 1, 3).reshape(D, 2 * HEADS * QK_DIM)
    return jnp.concatenate([w[:, :OFF_Q], regroup(w[:, OFF_Q:OFF_K]), regroup(w[:, OFF_K:OFF_V]), w[:, OFF_V:]],
                           axis=1).astype(BF16)


def _rope_tables():
    t = jnp.arange(L)
    row = (t // GRID_W).astype(F32)
    col = (t % GRID_W).astype(F32)
    n_freq = QK_DIM // 4
    inv = ROPE_BASE ** (-jnp.arange(n_freq, dtype=F32) / n_freq)
    ar = row[:, None] * inv
    ac = col[:, None] * inv
    cos64 = jnp.concatenate([jnp.cos(ar), jnp.cos(ar), jnp.cos(ac), jnp.cos(ac)], axis=1)
    sin64 = jnp.concatenate([-jnp.sin(ar), jnp.sin(ar), -jnp.sin(ac), jnp.sin(ac)], axis=1)
    cos_t = jnp.concatenate([jnp.tile(cos64, (1, 2)), jnp.ones((TM_IN, 128), F32)], axis=0)
    sin_t = jnp.concatenate([jnp.tile(sin64, (1, 2)), jnp.zeros((TM_IN, 128), F32)], axis=0)
    return cos_t, sin_t


def _split_bf16(w):
    hi = w.astype(BF16)
    return hi, (w - hi.astype(F32)).astype(BF16)


def kernel(x, c, ctx, c_ctx, w_ada, b_ada, g_norm1, g_norm2, w_in, w_out, g_v, w_s, b_s, w_conv, g_q, g_k,
           lam_q1, lam_k1, lam_q2, lam_k2, g_sub, w_router, b_router, w_gate, w_up, w_down,
           ws_gate, ws_up, ws_down):
    src = (x.reshape(NL, D), ctx.reshape(NC, D))
    cc = jnp.concatenate([c, c_ctx[None, :], jnp.zeros((MOD_ROWS - B - 1, D), F32)], axis=0)
    mod = _ada(cc, w_ada, b_ada)
    cos_t, sin_t = _rope_tables()
    bd = jnp.asarray(np.kron(np.eye(8, dtype=np.float32), np.full((64, 64), 1.0 / 64, np.float32)), BF16)

    for l in range(DEPTH):
        last = l == DEPTH - 1
        lam_init = 0.8 - 0.6 * math.exp(-0.3 * l)
        lam = (jnp.exp(jnp.sum(lam_q1[l] * lam_k1[l])) - jnp.exp(jnp.sum(lam_q2[l] * lam_k2[l])) + lam_init)
        bound = (QK_DIM * jnp.max(jnp.abs(g_q[l])) * jnp.max(jnp.abs(g_k[l]))
                 * (QK_DIM ** -0.5 * LOG2E) * ATT_BOUND_MARGIN)
        use_bound = (2.0 * bound < ATT_MAX_SHIFT_RANGE).astype(F32)
        lam = jnp.stack([lam, bound, use_bound]).astype(F32)
        mod3 = mod[l].reshape(MOD_ROWS, 1, 6 * D)
        w_in_bf = _in_weights(w_in[l])
        bias_t = jnp.repeat(b_s[l].T, A_GD, axis=1)
        wconv = jnp.concatenate([w_conv[l], jnp.zeros((5, B_WIDTH), F32)], axis=0)
        yab, qa, qb, kk, v = _in_mix(src[0], src[1], g_norm1[l], mod3, w_in_bf, cos_t, sin_t,
                                     g_v[l].reshape(1, A_WIDTH), w_s[l].astype(BF16), bias_t, wconv,
                                     jnp.tile(g_q[l], 8).reshape(1, 512), jnp.tile(g_k[l], 8).reshape(1, 512), bd)
        gsub = g_sub[l].reshape(1, V_DIM)
        coef = 1.0 - lam_init
        n_rows = NL if last else NR
        yc = _attention(lam, qa, qb, kk, v, gsub, coef, ctx_queries=False)
        yc_ctx = yc if last else _attention(lam, qa, qb, kk, v, gsub, coef, ctx_queries=True)
        wr_hi, wr_lo = _split_bf16(w_router[l].T)
        wo_bf = w_out[l].astype(BF16)
        ws_bf = (ws_gate[l].astype(BF16), ws_up[l].astype(BF16), ws_down[l].astype(BF16))
        xa, h2p, idx3, gate3, rank3, counts = _out_router(
            src[0], src[1], yab, yc, yc_ctx, wo_bf, g_norm2[l], mod3, wr_hi, wr_lo, b_router[l], n_rows)
        pos3, tile_expert, n_used, r_pad = _route_positions(idx3, rank3, counts[:, 0], n_rows)
        x_sorted = _sc_dispatch(h2p.reshape(n_rows, 4, 128), pos3, r_pad)
        y_sorted = _experts(l, tile_expert, n_used, x_sorted.reshape(r_pad * 4, 128), w_gate, w_up, w_down)
        yg = _sc_collect(y_sorted.reshape(r_pad, 4, 128), pos3)
        gates = gate3.transpose(0, 2, 1).reshape(n_rows, TOP_K)
        xa = _shared_residual(xa, h2p, yg.reshape(TOP_K, n_rows * 4, 128), gates, *ws_bf, mod3, n_rows)
        src = (xa, xa)
    return xa.reshape(B, L, D)
```

```python
import functools
import math

import numpy as np
import jax
import jax.numpy as jnp
from jax import lax
from jax.experimental import pallas as pl
from jax.experimental.pallas import tpu as pltpu
from jax.experimental.pallas import tpu_sc as plsc

F32 = jnp.float32
BF16 = jnp.bfloat16

D = 1024
B = 8
L = 2048
DEPTH = 2
GRID_W = 64
CTX = 256
A_WIDTH = 256
A_GROUPS = 4
A_GD = 64
CHUNK = 128
B_WIDTH = 256
C_WIDTH = 512
HEADS = 4
V_DIM = 128
QK_DIM = 64
ROPE_BASE = 10000.0
OFF_BB = 512
OFF_BC = 768
OFF_BX = 1024
OFF_Q = 1280
OFF_K = 1792
OFF_V = 2304
D_IN = 2816
N_EXPERTS = 64
TOP_K = 8
D_EXPERT = 256
D_SHARED = 256
ROUTED_SCALE = 2.5
EPS = 1e-6

NL = B * L
NC = B * CTX
NR = NL + NC
MOD_ROWS = 16
LOG2E = 1.4426950408889634

TM_IN = 512
TQ = 1024
ATT_CHAIN = 256
ATT_BOUND_MARGIN = 1.02
ATT_MAX_SHIFT_RANGE = 100.0
TM_OUT = 512
TM_X = 1024
X_CHAINS = 2
X_RING = 3
TM_F = 512

_DN_T = (((1,), (1,)), ((), ()))


def _cparams(sem, vmem_mb=None):
    kw = dict(dimension_semantics=sem)
    if vmem_mb is not None:
        kw["vmem_limit_bytes"] = vmem_mb * 1024 * 1024
    return pltpu.CompilerParams(**kw)


def _mod_row(i, tm):
    return jnp.where(i < NL // tm, i // (L // tm), B)


def _mod_spec(tm, chunk):
    return pl.BlockSpec((None, 1, D), lambda i: (_mod_row(i, tm), 0, chunk))


def _ada_kernel(c_ref, w_ref, b_ref, o_ref):
    c = c_ref[...]
    cs = c * jax.nn.sigmoid(c)
    o_ref[...] = jnp.dot(cs, w_ref[...], preferred_element_type=F32,
                         precision=lax.Precision.HIGHEST) + b_ref[...]


def _ada(cc, w_ada, b_ada):
    nb = 6
    return pl.pallas_call(
        _ada_kernel,
        out_shape=jax.ShapeDtypeStruct((DEPTH, MOD_ROWS, 6 * D), F32),
        grid=(DEPTH, nb),
        in_specs=[pl.BlockSpec((MOD_ROWS, D), lambda l, j: (0, 0)),
                  pl.BlockSpec((None, D, D), lambda l, j: (l, 0, j)),
                  pl.BlockSpec((None, 1, D), lambda l, j: (l, 0, j))],
        out_specs=pl.BlockSpec((None, MOD_ROWS, D), lambda l, j: (l, 0, j)),
        compiler_params=_cparams(("arbitrary", "arbitrary"), 40),
        name="ada_mod",
    )(cc, w_ada, b_ada.reshape(DEPTH, 1, 6 * D))


def _rms_mod(x, g, sc, sh):
    ms = jnp.mean(x * x, axis=-1, keepdims=True)
    return x * lax.rsqrt(ms + EPS) * (g * (1.0 + sc)) + sh


def _two_source_specs(tm, n_first, width=D):
    return [pl.BlockSpec((tm, width), lambda i: (jnp.minimum(i, n_first - 1), 0)),
            pl.BlockSpec((tm, width), lambda i: (jnp.maximum(i - n_first, 0), 0))]


def _two_source_rows(a_ref, b_ref, n_first):
    return jnp.where(pl.program_id(0) < n_first, a_ref[...], b_ref[...])


def _group_rms(t, g, bd):
    ms = jnp.dot((t * t).astype(BF16), bd, preferred_element_type=F32)
    return t * lax.rsqrt(ms + EPS) * g


def _rope(t, cos, sin):
    w = t.shape[1]
    lane = lax.broadcasted_iota(jnp.int32, t.shape, 1)
    first = (lane % 32) < 16
    partner = jnp.where(first, pltpu.roll(t, w - 16, 1), pltpu.roll(t, 16, 1))
    cos4 = jnp.concatenate([cos] * (w // 128), axis=1)
    sin4 = jnp.concatenate([sin] * (w // 128), axis=1)
    return t * cos4 + partner * sin4


def _in_mix_kernel(xa_ref, xb_ref, pa_ref, pb_ref, na_ref, nb_ref, g_ref, sh_ref, sc_ref, w_ref,
                   cos_ref, sin_ref, gv_ref, ws_ref, bias_ref, wconv_ref, gq_ref, gk_ref, bd_ref,
                   yab_ref, q_ref, kk_ref, v_ref, *, n_first):
    tm = TM_IN
    i = pl.program_id(0)
    tiles_per_seq = L // tm
    is_lat = i < NL // tm
    is_start = jnp.logical_or(jnp.logical_not(is_lat), i % tiles_per_seq == 0)
    is_end = jnp.logical_or(jnp.logical_not(is_lat), i % tiles_per_seq == tiles_per_seq - 1)
    first = i < n_first
    g, sc, sh = g_ref[...], sc_ref[...], sh_ref[...]

    h = _rms_mod(jnp.where(first, xa_ref[...], xb_ref[...]), g, sc, sh)
    p = jnp.dot(h.astype(BF16), w_ref[...], preferred_element_type=F32)
    v_ref[...] = p[:, OFF_V:].astype(BF16)
    halo = jnp.concatenate([jnp.where(first, pa_ref[...], pb_ref[...]),
                            jnp.where(first, na_ref[...], nb_ref[...])], axis=0)
    ph = jnp.dot(_rms_mod(halo, g, sc, sh).astype(BF16), w_ref[:, OFF_BC:OFF_Q], preferred_element_type=F32)
    zh = ph[:, :B_WIDTH] * ph[:, B_WIDTH:]
    zp = jnp.where(is_start, 0.0, zh[15:16])
    zn = jnp.where(is_end, 0.0, zh[16:17])

    uv = p[:, 0:2 * A_WIDTH]
    uv = 0.5 * uv * (1.0 + lax.erf(uv * (2.0 ** -0.5)))
    u = uv[:, :A_WIDTH]
    v = uv[:, A_WIDTH:]
    ms = jnp.mean(v * v, axis=-1, keepdims=True)
    vb = (v * lax.rsqrt(ms + EPS) * gv_ref[...]).astype(BF16)
    lane = lax.broadcasted_iota(jnp.int32, (CHUNK, 128), 1)
    mixes = []
    for c in range(tm // CHUNK):
        vc = vb[c * CHUNK:(c + 1) * CHUNK]
        halves = []
        for j in range(2):
            vj = vc[:, j * 128:(j + 1) * 128]
            m0 = jnp.dot(ws_ref[2 * j], vj, preferred_element_type=F32)
            m1 = jnp.dot(ws_ref[2 * j + 1], vj, preferred_element_type=F32)
            halves.append(jnp.where(lane < A_GD, m0, m1))
        mixes.append(jnp.concatenate(halves, axis=1) + bias_ref[...])
    ya = u * jnp.concatenate(mixes, axis=0)

    bg = p[:, OFF_BB:OFF_BC]
    z = p[:, OFF_BC:OFF_BX] * p[:, OFF_BX:OFF_Q]
    row = lax.broadcasted_iota(jnp.int32, z.shape, 0)
    inner = jnp.logical_not(is_lat)
    z_prev = jnp.where(row == 0, zp, pltpu.roll(z, 1, 0))
    z_prev = jnp.where(jnp.logical_and(inner, row % CTX == 0), 0.0, z_prev)
    z_next = jnp.where(row == tm - 1, zn, pltpu.roll(z, tm - 1, 0))
    z_next = jnp.where(jnp.logical_and(inner, row % CTX == CTX - 1), 0.0, z_next)
    yb = bg * (z_prev * wconv_ref[0:1] + z * wconv_ref[1:2] + z_next * wconv_ref[2:3])
    yab_ref[...] = jnp.concatenate([ya, yb], axis=1).astype(BF16)

    cos = cos_ref[...]
    sin = sin_ref[...]
    bd = bd_ref[...]
    q = _rope(_group_rms(p[:, OFF_Q:OFF_K], gq_ref[...], bd), cos, sin)
    q = q * (QK_DIM ** -0.5 * LOG2E)
    q_ref[...] = q.astype(BF16)
    k = _rope(_group_rms(p[:, OFF_K:OFF_V], gk_ref[...], bd), cos, sin)
    kk_ref[...] = k.astype(BF16)


def _in_mix(x_first, x_second, g, mod3, w_bf, cos_t, sin_t, gv, ws_bf, bias_t, wconv, gq, gk, bd):
    tm = TM_IN
    n_first = x_first.shape[0] // tm
    hb = tm // 16
    nhb_first = x_first.shape[0] // 16
    nhb_second = x_second.shape[0] // 16
    pos_blocks = L // tm

    def tab_map(i):
        return (jnp.where(i < NL // tm, i % pos_blocks, pos_blocks), 0)

    def halo_specs(shift):
        blk = lambda i: (i * tm + shift) // 16
        return [pl.BlockSpec((16, D), lambda i: (jnp.clip(blk(i), 0, nhb_first - 1), 0)),
                pl.BlockSpec((16, D), lambda i: (jnp.clip(blk(i) - nhb_first, 0, nhb_second - 1), 0))]

    const2 = lambda i: (0, 0)
    row512 = pl.BlockSpec((tm, 512), lambda i: (i, 0))
    return pl.pallas_call(
        functools.partial(_in_mix_kernel, n_first=n_first),
        out_shape=[jax.ShapeDtypeStruct((NR, 512), BF16)] * 4,
        grid=(NR // tm,),
        in_specs=_two_source_specs(tm, n_first) + halo_specs(-1) + halo_specs(tm)
                 + [pl.BlockSpec((1, D), const2), _mod_spec(tm, 0), _mod_spec(tm, 1),
                    pl.BlockSpec((D, D_IN), const2),
                    pl.BlockSpec((tm, 128), tab_map), pl.BlockSpec((tm, 128), tab_map),
                    pl.BlockSpec((1, A_WIDTH), const2),
                    pl.BlockSpec((A_GROUPS, CHUNK, CHUNK), lambda i: (0, 0, 0)),
                    pl.BlockSpec((CHUNK, A_WIDTH), const2),
                    pl.BlockSpec((8, B_WIDTH), const2),
                    pl.BlockSpec((1, 512), const2), pl.BlockSpec((1, 512), const2),
                    pl.BlockSpec((512, 512), const2)],
        out_specs=[row512] * 4,
        compiler_params=_cparams(("parallel",), 56),
        name="in_mix",
    )(x_first, x_second, x_first, x_second, x_first, x_second, g.reshape(1, D), mod3, mod3, w_bf,
      cos_t, sin_t, gv, ws_bf, bias_t, wconv, gq, gk, bd)


def _attn_kernel(lam_ref, q_ref, *rest, n_seg, coef, tq):
    kv_refs = rest[:2 * n_seg]
    gsub_ref, o_ref, k_scr, vt_scr = rest[2 * n_seg:]

    @pl.when(pl.program_id(2) == 0)
    def _():
        off = 0
        for s in range(n_seg):
            n = kv_refs[s].shape[0]
            k_scr[off:off + n, :] = kv_refs[s][...]
            vt_scr[0:V_DIM, off:off + n] = kv_refs[n_seg + s][...].astype(F32).T.astype(BF16)
            off += n
        ones_row = lax.broadcasted_iota(jnp.int32, (16, off), 0) == 0
        vt_scr[V_DIM:, :] = jnp.where(ones_row, 1.0, 0.0).astype(BF16)

    lam = lam_ref[0]
    shift = lam_ref[1]
    qc = ATT_CHAIN

    def scores(c):
        rows = slice(c * qc, (c + 1) * qc)
        q = q_ref[rows, :]
        lane = lax.broadcasted_iota(jnp.int32, q.shape, 1)
        zero = jnp.zeros_like(q)
        qs = jnp.concatenate([jnp.where(lane < QK_DIM, q, zero), jnp.where(lane >= QK_DIM, q, zero)], axis=0)
        return lax.dot_general(k_scr[...], qs, _DN_T, preferred_element_type=F32)

    def finish(c, pt):
        ot = jnp.dot(vt_scr[...], pt, preferred_element_type=F32)
        inv = 1.0 / ot[V_DIM:V_DIM + 1, :]
        dt = ot[0:V_DIM, :qc] * inv[:, :qc] - ot[0:V_DIM, qc:] * (lam * inv[:, qc:])
        o = dt.T
        ms = jnp.mean(o * o, axis=-1, keepdims=True)
        o_ref[c * qc:(c + 1) * qc, :] = (o * lax.rsqrt(ms + EPS) * gsub_ref[...] * coef).astype(o_ref.dtype)

    @pl.when(lam_ref[2] > 0.5)
    def _():
        for c in range(tq // qc):
            finish(c, jnp.exp2(scores(c) - shift).astype(BF16))

    @pl.when(lam_ref[2] <= 0.5)
    def _():
        sts = [scores(c) for c in range(tq // qc)]
        for c in range(tq // qc):
            st = sts[c]
            finish(c, jnp.exp2(st - jnp.max(st, axis=0, keepdims=True)).astype(BF16))


def _attention(lam, q, kk, v, gsub, coef, *, ctx_queries):
    if ctx_queries:
        tq = CTX
        nq, lk, n_seg = 1, CTX, 1
        q_map = lambda b, h, qi: (NL // tq + b, h)
        kv_specs = [pl.BlockSpec((CTX, 128), lambda b, h, qi: (NL // CTX + b, h)),
                    pl.BlockSpec((CTX, 128), lambda b, h, qi: (NL // CTX + b, h))]
        kv_args = [kk, v]
        rows = NC
        o_map = lambda b, h, qi: (b, h)
    else:
        tq = TQ
        nq, lk, n_seg = L // tq, CTX + L, 2
        q_map = lambda b, h, qi: (b * (L // tq) + qi, h)
        kv_specs = [pl.BlockSpec((CTX, 128), lambda b, h, qi: (NL // CTX + b, h)),
                    pl.BlockSpec((L, 128), lambda b, h, qi: (b, h)),
                    pl.BlockSpec((CTX, 128), lambda b, h, qi: (NL // CTX + b, h)),
                    pl.BlockSpec((L, 128), lambda b, h, qi: (b, h))]
        kv_args = [kk, kk, v, v]
        rows = NL
        o_map = lambda b, h, qi: (b * (L // tq) + qi, h)
    return pl.pallas_call(
        functools.partial(_attn_kernel, n_seg=n_seg, coef=coef, tq=tq),
        out_shape=jax.ShapeDtypeStruct((rows, C_WIDTH), BF16),
        grid=(B, HEADS, nq),
        in_specs=[pl.BlockSpec(memory_space=pltpu.SMEM),
                  pl.BlockSpec((tq, 128), q_map)]
                 + kv_specs + [pl.BlockSpec((1, V_DIM), lambda b, h, qi: (0, 0))],
        out_specs=pl.BlockSpec((tq, 128), o_map),
        scratch_shapes=[pltpu.VMEM((lk, 128), BF16), pltpu.VMEM((V_DIM + 16, lk), BF16)],
        compiler_params=_cparams(("parallel", "parallel", "arbitrary"), 56),
        name="attn_ctx" if ctx_queries else "attn_lat",
    )(lam, q, *kv_args, gsub)


def _pack_rows(t, out_ref, row0=0):
    half = D // 2
    w = pltpu.pack_elementwise([t[:, :half], t[:, half:]], packed_dtype=BF16)
    w = lax.bitcast_convert_type(w, jnp.uint32)
    rows = t.shape[0]
    for j in range(4):
        out_ref[pl.ds(4 * row0 + j, rows, stride=4), :] = w[:, j * 128:(j + 1) * 128]


def _unpack_rows(ref, rows, lead=None, row0=0):
    los, his = [], []
    for j in range(4):
        sl = pl.ds(4 * row0 + j, rows, stride=4)
        w = ref[sl, :] if lead is None else ref[lead, sl, :]
        los.append(pltpu.unpack_elementwise(w, index=0, packed_dtype=BF16, unpacked_dtype=F32))
        his.append(pltpu.unpack_elementwise(w, index=1, packed_dtype=BF16, unpacked_dtype=F32))
    return jnp.concatenate(los, axis=1), jnp.concatenate(his, axis=1)


def _out_router_kernel(xa_ref, xb_ref, yab_ref, yca_ref, ycb_ref, wo_ref, g1_ref, g2n_ref, sh2_ref, sc2_ref, wrh_ref,
                       wrl_ref, br_ref, xo_ref, h2p_ref, idx_ref, gate_ref, rank_ref, cnt_ref, run_ref, *,
                       n_first, n_first_c):
    tm = TM_OUT

    @pl.when(pl.program_id(0) == 0)
    def _():
        run_ref[...] = jnp.zeros_like(run_ref)

    yc = _two_source_rows(yca_ref, ycb_ref, n_first_c)
    y = jnp.dot(jnp.concatenate([yab_ref[...], yc], axis=1), wo_ref[...], preferred_element_type=F32)
    x = _two_source_rows(xa_ref, xb_ref, n_first) + g1_ref[...] * y
    xo_ref[...] = x
    h2 = _rms_mod(x, g2n_ref[...], sc2_ref[...], sh2_ref[...])
    hi = h2.astype(BF16)
    _pack_rows(h2, h2p_ref)
    lo = (h2 - hi.astype(F32)).astype(BF16)
    wh = wrh_ref[...]
    z = (lax.dot_general(wh, hi, _DN_T, preferred_element_type=F32)
         + lax.dot_general(wh, lo, _DN_T, preferred_element_type=F32)
         + lax.dot_general(wrl_ref[...], hi, _DN_T, preferred_element_type=F32))
    scores = jax.nn.sigmoid(z)
    work = scores + br_ref[...]
    eio = lax.broadcasted_iota(jnp.int32, work.shape, 0)
    idxs, sels, hits = [], [], []
    for _ in range(TOP_K):
        m = jnp.max(work, axis=0, keepdims=True)
        idx = jnp.min(jnp.where(work == m, eio, N_EXPERTS), axis=0, keepdims=True)
        hit = eio == idx
        sels.append(jnp.sum(jnp.where(hit, scores, 0.0), axis=0, keepdims=True))
        idxs.append(idx)
        hits.append(hit)
        work = jnp.where(hit, -jnp.inf, work)
    sel = jnp.concatenate(sels, axis=0)
    idx8 = jnp.concatenate(idxs, axis=0)
    gate8 = sel / jnp.sum(sel, axis=0, keepdims=True) * ROUTED_SCALE

    chosen = functools.reduce(jnp.logical_or, hits)
    before = (lax.broadcasted_iota(jnp.int32, (tm, tm), 0) < lax.broadcasted_iota(jnp.int32, (tm, tm), 1))
    prefix = jnp.dot(jnp.where(chosen, 1.0, 0.0).astype(BF16), jnp.where(before, 1.0, 0.0).astype(BF16),
                     preferred_element_type=F32)
    rank_dense = prefix + run_ref[:, 0:1]
    rank8 = jnp.concatenate([jnp.sum(jnp.where(h, rank_dense, 0.0), axis=0, keepdims=True) for h in hits],
                            axis=0).astype(jnp.int32)
    run = run_ref[...] + jnp.sum(jnp.where(chosen, 1.0, 0.0), axis=1, keepdims=True)
    run_ref[...] = run
    cnt_ref[...] = run
    for c in range(tm // 128):
        idx_ref[c] = idx8[:, c * 128:(c + 1) * 128]
        gate_ref[c] = gate8[:, c * 128:(c + 1) * 128]
        rank_ref[c] = rank8[:, c * 128:(c + 1) * 128]


def _out_router(x_first, x_second, yab, yc_first, yc_second, wo_bf, g2n, mod3, wr_hi, wr_lo, br, n_rows):
    tm = TM_OUT
    n_first = x_first.shape[0] // tm
    n_first_c = yc_first.shape[0] // tm
    const2 = lambda i: (0, 0)
    row = lambda i: (i, 0)
    chunk3 = pl.BlockSpec((tm // 128, TOP_K, 128), lambda i: (i, 0, 0))
    nch = n_rows // 128
    return pl.pallas_call(
        functools.partial(_out_router_kernel, n_first=n_first, n_first_c=n_first_c),
        out_shape=[jax.ShapeDtypeStruct((n_rows, D), F32),
                   jax.ShapeDtypeStruct((n_rows * 4, 128), jnp.uint32),
                   jax.ShapeDtypeStruct((nch, TOP_K, 128), jnp.int32),
                   jax.ShapeDtypeStruct((nch, TOP_K, 128), F32),
                   jax.ShapeDtypeStruct((nch, TOP_K, 128), jnp.int32),
                   jax.ShapeDtypeStruct((N_EXPERTS, 128), F32)],
        grid=(n_rows // tm,),
        in_specs=_two_source_specs(tm, n_first)
                 + [pl.BlockSpec((tm, 512), row)]
                 + _two_source_specs(tm, n_first_c, C_WIDTH)
                 + [pl.BlockSpec((D, D), const2),
                  _mod_spec(tm, 2),
                  pl.BlockSpec((1, D), const2), _mod_spec(tm, 3), _mod_spec(tm, 4),
                  pl.BlockSpec((N_EXPERTS, D), const2), pl.BlockSpec((N_EXPERTS, D), const2),
                  pl.BlockSpec((N_EXPERTS, 1), const2)],
        out_specs=[pl.BlockSpec((tm, D), row), pl.BlockSpec((tm * 4, 128), row),
                   chunk3, chunk3, chunk3, pl.BlockSpec((N_EXPERTS, 128), const2)],
        scratch_shapes=[pltpu.VMEM((N_EXPERTS, 128), F32)],
        compiler_params=_cparams(("arbitrary",), 48),
        name="out_router",
    )(x_first, x_second, yab, yc_first, yc_second, wo_bf, mod3, g2n.reshape(1, D), mod3, mod3, wr_hi, wr_lo,
      br.reshape(N_EXPERTS, 1))


def _experts_kernel(te_ref, nu_ref, x_hbm, wg_ref, wu_ref, wd_ref, y_ref, wgu_s, wd_s, x_ring, x_sems):
    i = pl.program_id(0)
    n_used = nu_ref[0]
    used = i < n_used
    new_expert = jnp.logical_or(i == 0, te_ref[i] != te_ref[jnp.maximum(i - 1, 0)])

    ahead = X_RING - 1

    def x_copy(t):
        slot = t % X_RING
        return pltpu.make_async_copy(x_hbm.at[pl.ds(t * (4 * TM_X), 4 * TM_X)], x_ring.at[slot], x_sems.at[slot])

    @pl.when(i == 0)
    def _():
        for t in range(ahead):
            @pl.when(t < n_used)
            def _():
                x_copy(t).start()

    @pl.when(i + ahead < n_used)
    def _():
        x_copy(i + ahead).start()

    @pl.when(jnp.logical_and(used, new_expert))
    def _():
        wgu_s[:, 0:D_EXPERT] = wg_ref[...].astype(BF16)
        wgu_s[:, D_EXPERT:] = wu_ref[...].astype(BF16)
        wd_s[...] = wd_ref[...].astype(BF16)

    @pl.when(used)
    def _():
        x_copy(i).wait()
        x_ref = x_ring.at[i % X_RING]
        rc = TM_X // X_CHAINS
        wgu = wgu_s[...]
        wd = wd_s[...]
        abs_ = []
        for c in range(X_CHAINS):
            x_lo, x_hi = _unpack_rows(x_ref, rc, row0=c * rc)
            x = jnp.concatenate([x_lo.astype(BF16), x_hi.astype(BF16)], axis=1)
            abs_.append(jnp.dot(x, wgu, preferred_element_type=F32))
        for c in range(X_CHAINS):
            a = abs_[c][:, :D_EXPERT]
            hid = (a * jax.nn.sigmoid(a) * abs_[c][:, D_EXPERT:]).astype(BF16)
            _pack_rows(jnp.dot(hid, wd, preferred_element_type=F32), y_ref, row0=c * rc)


def _experts(layer, tile_expert, n_used, x_sorted, w_gate, w_up, w_down):
    tm = TM_X
    r_pad = x_sorted.shape[0] // 4
    nt = r_pad // tm
    row = lambda i, te, nu: (jnp.minimum(i, nu[0] - 1), 0)
    wmap = lambda i, te, nu: (layer, te[i], 0, 0)
    grid_spec = pltpu.PrefetchScalarGridSpec(
        num_scalar_prefetch=2,
        grid=(nt,),
        in_specs=[pl.BlockSpec(memory_space=pl.ANY),
                  pl.BlockSpec((None, None, D, D_EXPERT), wmap),
                  pl.BlockSpec((None, None, D, D_EXPERT), wmap),
                  pl.BlockSpec((None, None, D_EXPERT, D), wmap)],
        out_specs=pl.BlockSpec((tm * 4, 128), row),
        scratch_shapes=[pltpu.VMEM((D, 2 * D_EXPERT), BF16), pltpu.VMEM((D_EXPERT, D), BF16),
                        pltpu.VMEM((X_RING, tm * 4, 128), jnp.uint32), pltpu.SemaphoreType.DMA((X_RING,))],
    )
    return pl.pallas_call(
        _experts_kernel,
        out_shape=jax.ShapeDtypeStruct((r_pad * 4, 128), jnp.uint32),
        grid_spec=grid_spec,
        compiler_params=_cparams(("arbitrary",), 48),
        name="experts",
    )(tile_expert, n_used, x_sorted, w_gate, w_up, w_down)


SC_CORES = 2
SC_SUBCORES = 16
SC_WORKERS = SC_CORES * SC_SUBCORES
SC_CHUNK = 128


def _sc_mesh():
    return plsc.VectorSubcoreMesh(core_axis_name="c", subcore_axis_name="s")


def _sc_params():
    return pltpu.CompilerParams(use_tc_tiling_on_sc=True)


def _sc_dispatch(h2p, pos3, r_pad):
    nch = pos3.shape[0]
    steps = -(-nch // SC_WORKERS)

    def body(h_hbm, pos_hbm, out_hbm, idx_v, rows_v, sem):
        wid = lax.axis_index("s") * SC_CORES + lax.axis_index("c")

        @pl.loop(0, steps)
        def _(s):
            ch = wid + s * SC_WORKERS

            @pl.when(ch < nch)
            def _():
                pltpu.sync_copy(pos_hbm.at[ch], idx_v)
                pltpu.sync_copy(h_hbm.at[pl.ds(ch * SC_CHUNK, SC_CHUNK)], rows_v)
                copies = [pltpu.async_copy(rows_v, out_hbm.at[idx_v.at[k]], sem) for k in range(TOP_K)]
                for cp in copies:
                    cp.wait()

    return pl.kernel(
        body,
        out_type=jax.ShapeDtypeStruct((r_pad, 4, 128), jnp.uint32),
        mesh=_sc_mesh(),
        scratch_types=[pltpu.VMEM((TOP_K, SC_CHUNK), jnp.int32),
                       pltpu.VMEM((SC_CHUNK, 4, 128), jnp.uint32),
                       pltpu.SemaphoreType.DMA],
        compiler_params=_sc_params(),
        name="sc_dispatch",
    )(h2p, pos3)


def _sc_collect(y_sorted, pos3):
    nch = pos3.shape[0]
    steps = -(-nch // SC_WORKERS)
    half = SC_CHUNK // 2
    units = [(k, hh) for k in range(TOP_K) for hh in range(2)]

    def body(y_hbm, pos_hbm, out_hbm, idx_v, rows_a, rows_b, sem_a, sem_b):
        wid = lax.axis_index("s") * SC_CORES + lax.axis_index("c")
        bufs = (rows_a, rows_b)
        sems = (sem_a, sem_b)

        def gather(u):
            k, hh = units[u]
            return pltpu.async_copy(y_hbm.at[idx_v.at[k, pl.ds(hh * half, half)]], bufs[u % 2], sems[u % 2])

        @pl.loop(0, steps)
        def _(s):
            ch = wid + s * SC_WORKERS

            @pl.when(ch < nch)
            def _():
                pltpu.sync_copy(pos_hbm.at[ch], idx_v)
                pending = gather(0)
                for u, (k, hh) in enumerate(units):
                    nxt = gather(u + 1) if u + 1 < len(units) else None
                    pending.wait()
                    pltpu.sync_copy(bufs[u % 2], out_hbm.at[k, pl.ds(ch * SC_CHUNK + hh * half, half)])
                    pending = nxt

    return pl.kernel(
        body,
        out_type=jax.ShapeDtypeStruct((TOP_K, nch * SC_CHUNK, 4, 128), jnp.uint32),
        mesh=_sc_mesh(),
        scratch_types=[pltpu.VMEM((TOP_K, SC_CHUNK), jnp.int32),
                       pltpu.VMEM((half, 4, 128), jnp.uint32),
                       pltpu.VMEM((half, 4, 128), jnp.uint32),
                       pltpu.SemaphoreType.DMA, pltpu.SemaphoreType.DMA],
        compiler_params=_sc_params(),
        name="sc_collect",
    )(y_sorted, pos3)


def _shared_kernel(x_ref, h2p_ref, yg_ref, gate_ref, wg_ref, wu_ref, wd_ref, g2_ref, o_ref):
    h_lo, h_hi = _unpack_rows(h2p_ref, TM_F)
    h = jnp.concatenate([h_lo.astype(BF16), h_hi.astype(BF16)], axis=1)
    a = jnp.dot(h, wg_ref[...], preferred_element_type=F32)
    b = jnp.dot(h, wu_ref[...], preferred_element_type=F32)
    hid = (a * jax.nn.sigmoid(a) * b).astype(BF16)
    f = jnp.dot(hid, wd_ref[...], preferred_element_type=F32)
    gate = gate_ref[...]
    f_lo = f[:, :D // 2]
    f_hi = f[:, D // 2:]
    for k in range(TOP_K):
        y_lo, y_hi = _unpack_rows(yg_ref, TM_F, lead=k)
        f_lo = f_lo + gate[:, k:k + 1] * y_lo
        f_hi = f_hi + gate[:, k:k + 1] * y_hi
    o_ref[...] = x_ref[...] + g2_ref[...] * jnp.concatenate([f_lo, f_hi], axis=1)


def _shared_residual(xa, h2p, yg, gates, wsg_bf, wsu_bf, wsd_bf, mod3, n_rows):
    tm = TM_F
    row = lambda i: (i, 0)
    const2 = lambda i: (0, 0)
    return pl.pallas_call(
        _shared_kernel,
        out_shape=jax.ShapeDtypeStruct((n_rows, D), F32),
        grid=(n_rows // tm,),
        in_specs=[pl.BlockSpec((tm, D), row), pl.BlockSpec((tm * 4, 128), row),
                  pl.BlockSpec((TOP_K, tm * 4, 128), lambda i: (0, i, 0)),
                  pl.BlockSpec((tm, TOP_K), row),
                  pl.BlockSpec((D, D_SHARED), const2), pl.BlockSpec((D, D_SHARED), const2),
                  pl.BlockSpec((D_SHARED, D), const2), _mod_spec(tm, 5)],
        out_specs=pl.BlockSpec((tm, D), row),
        compiler_params=_cparams(("parallel",), 48),
        name="shared_residual",
    )(xa, h2p, yg, gates, wsg_bf, wsu_bf, wsd_bf, mod3)


def _positions_kernel(offs_ref, idx_ref, rank_ref, pos_ref):
    idx = idx_ref[...]
    base = jnp.zeros_like(idx)
    for e in range(N_EXPERTS):
        base = jnp.where(idx == e, offs_ref[e], base)
    pos_ref[...] = rank_ref[...] + base


def _positions(offs, idx3, rank3):
    nch = idx3.shape[0]
    cb = nch // 2
    spec = pl.BlockSpec((cb, TOP_K, 128), lambda i, offs: (i, 0, 0))
    return pl.pallas_call(
        _positions_kernel,
        out_shape=jax.ShapeDtypeStruct((nch, TOP_K, 128), jnp.int32),
        grid_spec=pltpu.PrefetchScalarGridSpec(num_scalar_prefetch=1, grid=(nch // cb,),
                                               in_specs=[spec, spec], out_specs=spec),
        compiler_params=_cparams(("parallel",)),
        name="positions",
    )(offs, idx3, rank3)


def _route_positions(idx3, rank3, counts, n_rows):
    tm = TM_X
    counts = counts.astype(jnp.int32)
    padded = ((counts + tm - 1) // tm) * tm
    ends = jnp.cumsum(padded)
    offs = ends - padded
    pos3 = _positions(offs.astype(jnp.int32), idx3, rank3)
    r_pad = n_rows * TOP_K + N_EXPERTS * tm
    nt = r_pad // tm
    tile_ids = jnp.arange(nt, dtype=jnp.int32)
    tile_expert = jnp.sum((ends // tm)[None, :] <= tile_ids[:, None], axis=1)
    tile_expert = jnp.minimum(tile_expert, N_EXPERTS - 1).astype(jnp.int32)
    n_used = (ends[-1] // tm).astype(jnp.int32).reshape(1)
    return pos3, tile_expert, n_used, r_pad


def _in_weights(w):
    def regroup(cols):
        return cols.reshape(D, 2, HEADS, QK_DIM).transpose(0, 2, 1, 3).reshape(D, 2 * HEADS * QK_DIM)
    return jnp.concatenate([w[:, :OFF_Q], regroup(w[:, OFF_Q:OFF_K]), regroup(w[:, OFF_K:OFF_V]), w[:, OFF_V:]],
                           axis=1).astype(BF16)


def _rope_tables():
    t = jnp.arange(L)
    row = (t // GRID_W).astype(F32)
    col = (t % GRID_W).astype(F32)
    n_freq = QK_DIM // 4
    inv = ROPE_BASE ** (-jnp.arange(n_freq, dtype=F32) / n_freq)
    ar = row[:, None] * inv
    ac = col[:, None] * inv
    cos64 = jnp.concatenate([jnp.cos(ar), jnp.cos(ar), jnp.cos(ac), jnp.cos(ac)], axis=1)
    sin64 = jnp.concatenate([-jnp.sin(ar), jnp.sin(ar), -jnp.sin(ac), jnp.sin(ac)], axis=1)
    cos_t = jnp.concatenate([jnp.tile(cos64, (1, 2)), jnp.ones((TM_IN, 128), F32)], axis=0)
    sin_t = jnp.concatenate([jnp.tile(sin64, (1, 2)), jnp.zeros((TM_IN, 128), F32)], axis=0)
    return cos_t, sin_t


def _split_bf16(w):
    hi = w.astype(BF16)
    return hi, (w - hi.astype(F32)).astype(BF16)


def kernel(x, c, ctx, c_ctx, w_ada, b_ada, g_norm1, g_norm2, w_in, w_out, g_v, w_s, b_s, w_conv, g_q, g_k,
           lam_q1, lam_k1, lam_q2, lam_k2, g_sub, w_router, b_router, w_gate, w_up, w_down,
           ws_gate, ws_up, ws_down):
    src = (x.reshape(NL, D), ctx.reshape(NC, D))
    cc = jnp.concatenate([c, c_ctx[None, :], jnp.zeros((MOD_ROWS - B - 1, D), F32)], axis=0)
    mod = _ada(cc, w_ada, b_ada)
    cos_t, sin_t = _rope_tables()
    bd = jnp.asarray(np.kron(np.eye(8, dtype=np.float32), np.full((64, 64), 1.0 / 64, np.float32)), BF16)

    for l in range(DEPTH):
        last = l == DEPTH - 1
        lam_init = 0.8 - 0.6 * math.exp(-0.3 * l)
        lam = (jnp.exp(jnp.sum(lam_q1[l] * lam_k1[l])) - jnp.exp(jnp.sum(lam_q2[l] * lam_k2[l])) + lam_init)
        bound = (QK_DIM * jnp.max(jnp.abs(g_q[l])) * jnp.max(jnp.abs(g_k[l]))
                 * (QK_DIM ** -0.5 * LOG2E) * ATT_BOUND_MARGIN)
        use_bound = (2.0 * bound < ATT_MAX_SHIFT_RANGE).astype(F32)
        lam = jnp.stack([lam, bound, use_bound]).astype(F32)
        mod3 = mod[l].reshape(MOD_ROWS, 1, 6 * D)
        w_in_bf = _in_weights(w_in[l])
        bias_t = jnp.repeat(b_s[l].T, A_GD, axis=1)
        wconv = jnp.concatenate([w_conv[l], jnp.zeros((5, B_WIDTH), F32)], axis=0)
        yab, q, kk, v = _in_mix(src[0], src[1], g_norm1[l], mod3, w_in_bf, cos_t, sin_t,
                                     g_v[l].reshape(1, A_WIDTH), w_s[l].astype(BF16), bias_t, wconv,
                                     jnp.tile(g_q[l], 8).reshape(1, 512), jnp.tile(g_k[l], 8).reshape(1, 512), bd)
        gsub = g_sub[l].reshape(1, V_DIM)
        coef = 1.0 - lam_init
        n_rows = NL if last else NR
        yc = _attention(lam, q, kk, v, gsub, coef, ctx_queries=False)
        yc_ctx = yc if last else _attention(lam, q, kk, v, gsub, coef, ctx_queries=True)
        wr_hi, wr_lo = _split_bf16(w_router[l].T)
        wo_bf = w_out[l].astype(BF16)
        ws_bf = (ws_gate[l].astype(BF16), ws_up[l].astype(BF16), ws_down[l].astype(BF16))
        xa, h2p, idx3, gate3, rank3, counts = _out_router(
            src[0], src[1], yab, yc, yc_ctx, wo_bf, g_norm2[l], mod3, wr_hi, wr_lo, b_router[l], n_rows)
        pos3, tile_expert, n_used, r_pad = _route_positions(idx3, rank3, counts[:, 0], n_rows)
        x_sorted = _sc_dispatch(h2p.reshape(n_rows, 4, 128), pos3, r_pad)
        y_sorted = _experts(l, tile_expert, n_used, x_sorted.reshape(r_pad * 4, 128), w_gate, w_up, w_down)
        yg = _sc_collect(y_sorted.reshape(r_pad, 4, 128), pos3)
        gates = gate3.transpose(0, 2, 1).reshape(n_rows, TOP_K)
        xa = _shared_residual(xa, h2p, yg.reshape(TOP_K, n_rows * 4, 128), gates, *ws_bf, mod3, n_rows)
        src = (xa, xa)
    return xa.reshape(B, L, D)
```

```python
import functools
import math

import numpy as np
import jax
import jax.numpy as jnp
from jax import lax
from jax.experimental import pallas as pl
from jax.experimental.pallas import tpu as pltpu
from jax.experimental.pallas import tpu_sc as plsc

F32 = jnp.float32
BF16 = jnp.bfloat16

D = 1024
B = 8
L = 2048
DEPTH = 2
GRID_W = 64
CTX = 256
A_WIDTH = 256
A_GROUPS = 4
A_GD = 64
CHUNK = 128
B_WIDTH = 256
C_WIDTH = 512
HEADS = 4
V_DIM = 128
QK_DIM = 64
ROPE_BASE = 10000.0
OFF_BB = 512
OFF_BC = 768
OFF_BX = 1024
OFF_Q = 1280
OFF_K = 1792
OFF_V = 2304
D_IN = 2816
N_EXPERTS = 64
TOP_K = 8
D_EXPERT = 256
D_SHARED = 256
ROUTED_SCALE = 2.5
EPS = 1e-6

NL = B * L
NC = B * CTX
NR = NL + NC
MOD_ROWS = 16
LOG2E = 1.4426950408889634

TM_IN = 512
TQ = 1024
ATT_CHAIN = 256
ATT_BOUND_MARGIN = 1.02
ATT_MAX_SHIFT_RANGE = 100.0
TM_OUT = 512
TM_X = 1024
X_CHAINS = 2
X_RING = 3
TM_F = 512

_DN_T = (((1,), (1,)), ((), ()))


def _cparams(sem, vmem_mb=None):
    kw = dict(dimension_semantics=sem)
    if vmem_mb is not None:
        kw["vmem_limit_bytes"] = vmem_mb * 1024 * 1024
    return pltpu.CompilerParams(**kw)


def _mod_row(i, tm):
    return jnp.where(i < NL // tm, i // (L // tm), B)


def _mod_spec(tm, chunk):
    return pl.BlockSpec((None, 1, D), lambda i: (_mod_row(i, tm), 0, chunk))


def _ada_kernel(c_ref, w_ref, b_ref, o_ref):
    c = c_ref[...]
    cs = c * jax.nn.sigmoid(c)
    o_ref[...] = jnp.dot(cs, w_ref[...], preferred_element_type=F32,
                         precision=lax.Precision.HIGHEST) + b_ref[...]


def _ada(cc, w_ada, b_ada):
    nb = 6
    return pl.pallas_call(
        _ada_kernel,
        out_shape=jax.ShapeDtypeStruct((DEPTH, MOD_ROWS, 6 * D), F32),
        grid=(DEPTH, nb),
        in_specs=[pl.BlockSpec((MOD_ROWS, D), lambda l, j: (0, 0)),
                  pl.BlockSpec((None, D, D), lambda l, j: (l, 0, j)),
                  pl.BlockSpec((None, 1, D), lambda l, j: (l, 0, j))],
        out_specs=pl.BlockSpec((None, MOD_ROWS, D), lambda l, j: (l, 0, j)),
        compiler_params=_cparams(("arbitrary", "arbitrary"), 40),
        name="ada_mod",
    )(cc, w_ada, b_ada.reshape(DEPTH, 1, 6 * D))


def _rms_mod(x, g, sc, sh):
    ms = jnp.mean(x * x, axis=-1, keepdims=True)
    return x * lax.rsqrt(ms + EPS) * (g * (1.0 + sc)) + sh


def _two_source_specs(tm, n_first, width=D):
    return [pl.BlockSpec((tm, width), lambda i: (jnp.minimum(i, n_first - 1), 0)),
            pl.BlockSpec((tm, width), lambda i: (jnp.maximum(i - n_first, 0), 0))]


def _two_source_rows(a_ref, b_ref, n_first):
    return jnp.where(pl.program_id(0) < n_first, a_ref[...], b_ref[...])


def _group_rms(t, g, bd):
    ms = jnp.dot((t * t).astype(BF16), bd, preferred_element_type=F32)
    return t * lax.rsqrt(ms + EPS) * g


def _rope(t, cos, sin):
    w = t.shape[1]
    lane = lax.broadcasted_iota(jnp.int32, t.shape, 1)
    first = (lane % 32) < 16
    partner = jnp.where(first, pltpu.roll(t, w - 16, 1), pltpu.roll(t, 16, 1))
    cos4 = jnp.concatenate([cos] * (w // 128), axis=1)
    sin4 = jnp.concatenate([sin] * (w // 128), axis=1)
    return t * cos4 + partner * sin4


def _in_mix_kernel(xa_ref, xb_ref, pa_ref, pb_ref, na_ref, nb_ref, g_ref, sh_ref, sc_ref, w_ref,
                   cos_ref, sin_ref, gv_ref, ws_ref, bias_ref, wconv_ref, gq_ref, gk_ref, bd_ref,
                   yab_ref, q_ref, kk_ref, v_ref, *, n_first):
    tm = TM_IN
    i = pl.program_id(0)
    tiles_per_seq = L // tm
    is_lat = i < NL // tm
    is_start = jnp.logical_or(jnp.logical_not(is_lat), i % tiles_per_seq == 0)
    is_end = jnp.logical_or(jnp.logical_not(is_lat), i % tiles_per_seq == tiles_per_seq - 1)
    first = i < n_first
    g, sc, sh = g_ref[...], sc_ref[...], sh_ref[...]

    h = _rms_mod(jnp.where(first, xa_ref[...], xb_ref[...]), g, sc, sh)
    p = jnp.dot(h.astype(BF16), w_ref[...], preferred_element_type=F32)
    v_ref[...] = p[:, OFF_V:].astype(BF16)
    halo = jnp.concatenate([jnp.where(first, pa_ref[...], pb_ref[...]),
                            jnp.where(first, na_ref[...], nb_ref[...])], axis=0)
    ph = jnp.dot(_rms_mod(halo, g, sc, sh).astype(BF16), w_ref[:, OFF_BC:OFF_Q], preferred_element_type=F32)
    zh = ph[:, :B_WIDTH] * ph[:, B_WIDTH:]
    zp = jnp.where(is_start, 0.0, zh[15:16])
    zn = jnp.where(is_end, 0.0, zh[16:17])

    uv = p[:, 0:2 * A_WIDTH]
    uv = 0.5 * uv * (1.0 + lax.erf(uv * (2.0 ** -0.5)))
    u = uv[:, :A_WIDTH]
    v = uv[:, A_WIDTH:]
    ms = jnp.mean(v * v, axis=-1, keepdims=True)
    vb = (v * lax.rsqrt(ms + EPS) * gv_ref[...]).astype(BF16)
    lane = lax.broadcasted_iota(jnp.int32, (CHUNK, 128), 1)
    mixes = []
    for c in range(tm // CHUNK):
        vc = vb[c * CHUNK:(c + 1) * CHUNK]
        halves = []
        for j in range(2):
            vj = vc[:, j * 128:(j + 1) * 128]
            m0 = jnp.dot(ws_ref[2 * j], vj, preferred_element_type=F32)
            m1 = jnp.dot(ws_ref[2 * j + 1], vj, preferred_element_type=F32)
            halves.append(jnp.where(lane < A_GD, m0, m1))
        mixes.append(jnp.concatenate(halves, axis=1) + bias_ref[...])
    ya = u * jnp.concatenate(mixes, axis=0)

    bg = p[:, OFF_BB:OFF_BC]
    z = p[:, OFF_BC:OFF_BX] * p[:, OFF_BX:OFF_Q]
    row = lax.broadcasted_iota(jnp.int32, z.shape, 0)
    inner = jnp.logical_not(is_lat)
    z_prev = jnp.where(row == 0, zp, pltpu.roll(z, 1, 0))
    z_prev = jnp.where(jnp.logical_and(inner, row % CTX == 0), 0.0, z_prev)
    z_next = jnp.where(row == tm - 1, zn, pltpu.roll(z, tm - 1, 0))
    z_next = jnp.where(jnp.logical_and(inner, row % CTX == CTX - 1), 0.0, z_next)
    yb = bg * (z_prev * wconv_ref[0:1] + z * wconv_ref[1:2] + z_next * wconv_ref[2:3])
    yab_ref[...] = jnp.concatenate([ya, yb], axis=1).astype(BF16)

    cos = cos_ref[...]
    sin = sin_ref[...]
    bd = bd_ref[...]
    q = _rope(_group_rms(p[:, OFF_Q:OFF_K], gq_ref[...], bd), cos, sin)
    q = q * (QK_DIM ** -0.5 * LOG2E)
    q_ref[...] = q.astype(BF16)
    k = _rope(_group_rms(p[:, OFF_K:OFF_V], gk_ref[...], bd), cos, sin)
    kk_ref[...] = k.astype(BF16)


def _in_mix(x_first, x_second, g, mod3, w_bf, cos_t, sin_t, gv, ws_bf, bias_t, wconv, gq, gk, bd):
    tm = TM_IN
    n_first = x_first.shape[0] // tm
    hb = tm // 16
    nhb_first = x_first.shape[0] // 16
    nhb_second = x_second.shape[0] // 16
    pos_blocks = L // tm

    def tab_map(i):
        return (jnp.where(i < NL // tm, i % pos_blocks, pos_blocks), 0)

    def halo_specs(shift):
        blk = lambda i: (i * tm + shift) // 16
        return [pl.BlockSpec((16, D), lambda i: (jnp.clip(blk(i), 0, nhb_first - 1), 0)),
                pl.BlockSpec((16, D), lambda i: (jnp.clip(blk(i) - nhb_first, 0, nhb_second - 1), 0))]

    const2 = lambda i: (0, 0)
    row512 = pl.BlockSpec((tm, 512), lambda i: (i, 0))
    return pl.pallas_call(
        functools.partial(_in_mix_kernel, n_first=n_first),
        out_shape=[jax.ShapeDtypeStruct((NR, 512), BF16)] * 4,
        grid=(NR // tm,),
        in_specs=_two_source_specs(tm, n_first) + halo_specs(-1) + halo_specs(tm)
                 + [pl.BlockSpec((1, D), const2), _mod_spec(tm, 0), _mod_spec(tm, 1),
                    pl.BlockSpec((D, D_IN), const2),
                    pl.BlockSpec((tm, 128), tab_map), pl.BlockSpec((tm, 128), tab_map),
                    pl.BlockSpec((1, A_WIDTH), const2),
                    pl.BlockSpec((A_GROUPS, CHUNK, CHUNK), lambda i: (0, 0, 0)),
                    pl.BlockSpec((CHUNK, A_WIDTH), const2),
                    pl.BlockSpec((8, B_WIDTH), const2),
                    pl.BlockSpec((1, 512), const2), pl.BlockSpec((1, 512), const2),
                    pl.BlockSpec((512, 512), const2)],
        out_specs=[row512] * 4,
        compiler_params=_cparams(("parallel",), 56),
        name="in_mix",
    )(x_first, x_second, x_first, x_second, x_first, x_second, g.reshape(1, D), mod3, mod3, w_bf,
      cos_t, sin_t, gv, ws_bf, bias_t, wconv, gq, gk, bd)


def _attn_kernel(lam_ref, q_ref, *rest, n_seg, coef, tq):
    kv_refs = rest[:2 * n_seg]
    gsub_ref, o_ref, k_scr, vt_scr = rest[2 * n_seg:]

    @pl.when(pl.program_id(2) == 0)
    def _():
        off = 0
        for s in range(n_seg):
            n = kv_refs[s].shape[0]
            k_scr[off:off + n, :] = kv_refs[s][...]
            vt_scr[0:V_DIM, off:off + n] = kv_refs[n_seg + s][...].astype(F32).T.astype(BF16)
            off += n
        ones_row = lax.broadcasted_iota(jnp.int32, (16, off), 0) == 0
        vt_scr[V_DIM:, :] = jnp.where(ones_row, 1.0, 0.0).astype(BF16)

    lam = lam_ref[0]
    shift = lam_ref[1]
    qc = ATT_CHAIN

    def scores(c):
        rows = slice(c * qc, (c + 1) * qc)
        q = q_ref[rows, :]
        lane = lax.broadcasted_iota(jnp.int32, q.shape, 1)
        zero = jnp.zeros_like(q)
        qs = jnp.concatenate([jnp.where(lane < QK_DIM, q, zero), jnp.where(lane >= QK_DIM, q, zero)], axis=0)
        return lax.dot_general(k_scr[...], qs, _DN_T, preferred_element_type=F32)

    def finish(c, pt):
        ot = jnp.dot(vt_scr[...], pt, preferred_element_type=F32)
        inv = 1.0 / ot[V_DIM:V_DIM + 1, :]
        dt = ot[0:V_DIM, :qc] * inv[:, :qc] - ot[0:V_DIM, qc:] * (lam * inv[:, qc:])
        o = dt.T
        ms = jnp.mean(o * o, axis=-1, keepdims=True)
        o_ref[c * qc:(c + 1) * qc, :] = (o * lax.rsqrt(ms + EPS) * gsub_ref[...] * coef).astype(o_ref.dtype)

    @pl.when(lam_ref[2] > 0.5)
    def _():
        for c in range(tq // qc):
            finish(c, jnp.exp2(scores(c) - shift).astype(BF16))

    @pl.when(lam_ref[2] <= 0.5)
    def _():
        sts = [scores(c) for c in range(tq // qc)]
        for c in range(tq // qc):
            st = sts[c]
            finish(c, jnp.exp2(st - jnp.max(st, axis=0, keepdims=True)).astype(BF16))


def _attention(lam, q, kk, v, gsub, coef, *, ctx_queries):
    if ctx_queries:
        tq = CTX
        nq, lk, n_seg = 1, CTX, 1
        q_map = lambda b, h, qi: (NL // tq + b, h)
        kv_specs = [pl.BlockSpec((CTX, 128), lambda b, h, qi: (NL // CTX + b, h)),
                    pl.BlockSpec((CTX, 128), lambda b, h, qi: (NL // CTX + b, h))]
        kv_args = [kk, v]
        rows = NC
        o_map = lambda b, h, qi: (b, h)
    else:
        tq = TQ
        nq, lk, n_seg = L // tq, CTX + L, 2
        q_map = lambda b, h, qi: (b * (L // tq) + qi, h)
        kv_specs = [pl.BlockSpec((CTX, 128), lambda b, h, qi: (NL // CTX + b, h)),
                    pl.BlockSpec((L, 128), lambda b, h, qi: (b, h)),
                    pl.BlockSpec((CTX, 128), lambda b, h, qi: (NL // CTX + b, h)),
                    pl.BlockSpec((L, 128), lambda b, h, qi: (b, h))]
        kv_args = [kk, kk, v, v]
        rows = NL
        o_map = lambda b, h, qi: (b * (L // tq) + qi, h)
    return pl.pallas_call(
        functools.partial(_attn_kernel, n_seg=n_seg, coef=coef, tq=tq),
        out_shape=jax.ShapeDtypeStruct((rows, C_WIDTH), BF16),
        grid=(B, HEADS, nq),
        in_specs=[pl.BlockSpec(memory_space=pltpu.SMEM),
                  pl.BlockSpec((tq, 128), q_map)]
                 + kv_specs + [pl.BlockSpec((1, V_DIM), lambda b, h, qi: (0, 0))],
        out_specs=pl.BlockSpec((tq, 128), o_map),
        scratch_shapes=[pltpu.VMEM((lk, 128), BF16), pltpu.VMEM((V_DIM + 16, lk), BF16)],
        compiler_params=_cparams(("parallel", "parallel", "arbitrary"), 56),
        name="attn_ctx" if ctx_queries else "attn_lat",
    )(lam, q, *kv_args, gsub)


def _pack_rows(t, out_ref, row0=0):
    half = D // 2
    w = pltpu.pack_elementwise([t[:, :half], t[:, half:]], packed_dtype=BF16)
    w = lax.bitcast_convert_type(w, jnp.uint32)
    rows = t.shape[0]
    for j in range(4):
        out_ref[pl.ds(4 * row0 + j, rows, stride=4), :] = w[:, j * 128:(j + 1) * 128]


def _unpack_rows(ref, rows, lead=None, row0=0):
    los, his = [], []
    for j in range(4):
        sl = pl.ds(4 * row0 + j, rows, stride=4)
        w = ref[sl, :] if lead is None else ref[lead, sl, :]
        los.append(pltpu.unpack_elementwise(w, index=0, packed_dtype=BF16, unpacked_dtype=F32))
        his.append(pltpu.unpack_elementwise(w, index=1, packed_dtype=BF16, unpacked_dtype=F32))
    return jnp.concatenate(los, axis=1), jnp.concatenate(his, axis=1)


def _out_router_kernel(xa_ref, xb_ref, yab_ref, yca_ref, ycb_ref, wo_ref, g1_ref, g2n_ref, sh2_ref, sc2_ref, wrh_ref,
                       wrl_ref, br_ref, xo_ref, h2p_ref, idx_ref, gate_ref, rank_ref, cnt_ref, run_ref, *,
                       n_first, n_first_c):
    tm = TM_OUT

    @pl.when(pl.program_id(0) == 0)
    def _():
        run_ref[...] = jnp.zeros_like(run_ref)

    yc = _two_source_rows(yca_ref, ycb_ref, n_first_c)
    y = jnp.dot(jnp.concatenate([yab_ref[...], yc], axis=1), wo_ref[...], preferred_element_type=F32)
    x = _two_source_rows(xa_ref, xb_ref, n_first) + g1_ref[...] * y
    xo_ref[...] = x
    h2 = _rms_mod(x, g2n_ref[...], sc2_ref[...], sh2_ref[...])
    hi = h2.astype(BF16)
    _pack_rows(h2, h2p_ref)
    lo = (h2 - hi.astype(F32)).astype(BF16)
    wh = wrh_ref[...]
    z = (lax.dot_general(wh, hi, _DN_T, preferred_element_type=F32)
         + lax.dot_general(wh, lo, _DN_T, preferred_element_type=F32)
         + lax.dot_general(wrl_ref[...], hi, _DN_T, preferred_element_type=F32))
    scores = jax.nn.sigmoid(z)
    work = scores + br_ref[...]
    eio = lax.broadcasted_iota(jnp.int32, work.shape, 0)
    idxs, sels, hits = [], [], []
    for _ in range(TOP_K):
        m = jnp.max(work, axis=0, keepdims=True)
        idx = jnp.min(jnp.where(work == m, eio, N_EXPERTS), axis=0, keepdims=True)
        hit = eio == idx
        sels.append(jnp.sum(jnp.where(hit, scores, 0.0), axis=0, keepdims=True))
        idxs.append(idx)
        hits.append(hit)
        work = jnp.where(hit, -jnp.inf, work)
    sel = jnp.concatenate(sels, axis=0)
    idx8 = jnp.concatenate(idxs, axis=0)
    gate8 = sel / jnp.sum(sel, axis=0, keepdims=True) * ROUTED_SCALE

    chosen = functools.reduce(jnp.logical_or, hits)
    before = (lax.broadcasted_iota(jnp.int32, (tm, tm), 0) < lax.broadcasted_iota(jnp.int32, (tm, tm), 1))
    prefix = jnp.dot(jnp.where(chosen, 1.0, 0.0).astype(BF16), jnp.where(before, 1.0, 0.0).astype(BF16),
                     preferred_element_type=F32)
    rank_dense = prefix + run_ref[:, 0:1]
    rank8 = jnp.concatenate([jnp.sum(jnp.where(h, rank_dense, 0.0), axis=0, keepdims=True) for h in hits],
                            axis=0).astype(jnp.int32)
    run = run_ref[...] + jnp.sum(jnp.where(chosen, 1.0, 0.0), axis=1, keepdims=True)
    run_ref[...] = run
    cnt_ref[...] = run
    for c in range(tm // 128):
        idx_ref[c] = idx8[:, c * 128:(c + 1) * 128]
        gate_ref[c] = gate8[:, c * 128:(c + 1) * 128]
        rank_ref[c] = rank8[:, c * 128:(c + 1) * 128]


def _out_router(x_first, x_second, yab, yc_first, yc_second, wo_bf, g2n, mod3, wr_hi, wr_lo, br, n_rows):
    tm = TM_OUT
    n_first = x_first.shape[0] // tm
    n_first_c = yc_first.shape[0] // tm
    const2 = lambda i: (0, 0)
    row = lambda i: (i, 0)
    chunk3 = pl.BlockSpec((tm // 128, TOP_K, 128), lambda i: (i, 0, 0))
    nch = n_rows // 128
    return pl.pallas_call(
        functools.partial(_out_router_kernel, n_first=n_first, n_first_c=n_first_c),
        out_shape=[jax.ShapeDtypeStruct((n_rows, D), F32),
                   jax.ShapeDtypeStruct((n_rows * 4, 128), jnp.uint32),
                   jax.ShapeDtypeStruct((nch, TOP_K, 128), jnp.int32),
                   jax.ShapeDtypeStruct((nch, TOP_K, 128), F32),
                   jax.ShapeDtypeStruct((nch, TOP_K, 128), jnp.int32),
                   jax.ShapeDtypeStruct((N_EXPERTS, 128), F32)],
        grid=(n_rows // tm,),
        in_specs=_two_source_specs(tm, n_first)
                 + [pl.BlockSpec((tm, 512), row)]
                 + _two_source_specs(tm, n_first_c, C_WIDTH)
                 + [pl.BlockSpec((D, D), const2),
                  _mod_spec(tm, 2),
                  pl.BlockSpec((1, D), const2), _mod_spec(tm, 3), _mod_spec(tm, 4),
                  pl.BlockSpec((N_EXPERTS, D), const2), pl.BlockSpec((N_EXPERTS, D), const2),
                  pl.BlockSpec((N_EXPERTS, 1), const2)],
        out_specs=[pl.BlockSpec((tm, D), row), pl.BlockSpec((tm * 4, 128), row),
                   chunk3, chunk3, chunk3, pl.BlockSpec((N_EXPERTS, 128), const2)],
        scratch_shapes=[pltpu.VMEM((N_EXPERTS, 128), F32)],
        compiler_params=_cparams(("arbitrary",), 48),
        name="out_router",
    )(x_first, x_second, yab, yc_first, yc_second, wo_bf, mod3, g2n.reshape(1, D), mod3, mod3, wr_hi, wr_lo,
      br.reshape(N_EXPERTS, 1))


def _experts_kernel(te_ref, nx_ref, ws_ref, nu_ref, x_hbm, wg_hbm, wu_hbm, wd_hbm, y_ref,
                    wgu_s, wd_s, x_ring, x_sems, wg_f, wu_f, wd_f, w_sems, *, layer):
    i = pl.program_id(0)
    n_used = nu_ref[0]
    used = i < n_used
    new_expert = jnp.logical_or(i == 0, te_ref[i] != te_ref[jnp.maximum(i - 1, 0)])

    def w_copies(e, slot):
        return [pltpu.make_async_copy(wg_hbm.at[layer, e], wg_f.at[slot], w_sems.at[slot, 0]),
                pltpu.make_async_copy(wu_hbm.at[layer, e], wu_f.at[slot], w_sems.at[slot, 1]),
                pltpu.make_async_copy(wd_hbm.at[layer, e], wd_f.at[slot], w_sems.at[slot, 2])]

    @pl.when(i == 0)
    def _():
        for cp in w_copies(te_ref[0], ws_ref[0]):
            cp.start()

    ahead = X_RING - 1

    def x_copy(t):
        slot = t % X_RING
        return pltpu.make_async_copy(x_hbm.at[pl.ds(t * (4 * TM_X), 4 * TM_X)], x_ring.at[slot], x_sems.at[slot])

    @pl.when(i == 0)
    def _():
        for t in range(ahead):
            @pl.when(t < n_used)
            def _():
                x_copy(t).start()

    @pl.when(i + ahead < n_used)
    def _():
        x_copy(i + ahead).start()

    @pl.when(jnp.logical_and(used, new_expert))
    def _():
        slot = ws_ref[i]
        for cp in w_copies(te_ref[i], slot):
            cp.wait()

        @pl.when(nx_ref[i] >= 0)
        def _():
            for cp in w_copies(nx_ref[i], 1 - slot):
                cp.start()

        wgu_s[:, 0:D_EXPERT] = wg_f[slot].astype(BF16)
        wgu_s[:, D_EXPERT:] = wu_f[slot].astype(BF16)
        wd_s[...] = wd_f[slot].astype(BF16)

    @pl.when(used)
    def _():
        x_copy(i).wait()
        x_ref = x_ring.at[i % X_RING]
        rc = TM_X // X_CHAINS
        wgu = wgu_s[...]
        wd = wd_s[...]
        abs_ = []
        for c in range(X_CHAINS):
            x_lo, x_hi = _unpack_rows(x_ref, rc, row0=c * rc)
            x = jnp.concatenate([x_lo.astype(BF16), x_hi.astype(BF16)], axis=1)
            abs_.append(jnp.dot(x, wgu, preferred_element_type=F32))
        for c in range(X_CHAINS):
            a = abs_[c][:, :D_EXPERT]
            hid = (a * jax.nn.sigmoid(a) * abs_[c][:, D_EXPERT:]).astype(BF16)
            _pack_rows(jnp.dot(hid, wd, preferred_element_type=F32), y_ref, row0=c * rc)


def _experts(layer, sched, x_sorted, w_gate, w_up, w_down):
    tm = TM_X
    r_pad = x_sorted.shape[0] // 4
    nt = r_pad // tm
    row = lambda i, te, nx, ws, nu: (jnp.minimum(i, nu[0] - 1), 0)
    hbm = pl.BlockSpec(memory_space=pl.ANY)
    grid_spec = pltpu.PrefetchScalarGridSpec(
        num_scalar_prefetch=4,
        grid=(nt,),
        in_specs=[hbm, hbm, hbm, hbm],
        out_specs=pl.BlockSpec((tm * 4, 128), row),
        scratch_shapes=[pltpu.VMEM((D, 2 * D_EXPERT), BF16), pltpu.VMEM((D_EXPERT, D), BF16),
                        pltpu.VMEM((X_RING, tm * 4, 128), jnp.uint32), pltpu.SemaphoreType.DMA((X_RING,)),
                        pltpu.VMEM((2, D, D_EXPERT), F32), pltpu.VMEM((2, D, D_EXPERT), F32),
                        pltpu.VMEM((2, D_EXPERT, D), F32), pltpu.SemaphoreType.DMA((2, 3))],
    )
    return pl.pallas_call(
        functools.partial(_experts_kernel, layer=layer),
        out_shape=jax.ShapeDtypeStruct((r_pad * 4, 128), jnp.uint32),
        grid_spec=grid_spec,
        compiler_params=_cparams(("arbitrary",), 48),
        name="experts",
    )(*sched, x_sorted, w_gate, w_up, w_down)


SC_CORES = 2
SC_SUBCORES = 16
SC_WORKERS = SC_CORES * SC_SUBCORES
SC_CHUNK = 128


def _sc_mesh():
    return plsc.VectorSubcoreMesh(core_axis_name="c", subcore_axis_name="s")


def _sc_params():
    return pltpu.CompilerParams(use_tc_tiling_on_sc=True)


def _sc_dispatch(h2p, pos3, r_pad):
    nch = pos3.shape[0]
    steps = -(-nch // SC_WORKERS)

    def body(h_hbm, pos_hbm, out_hbm, idx_v, rows_v, sem):
        wid = lax.axis_index("s") * SC_CORES + lax.axis_index("c")

        @pl.loop(0, steps)
        def _(s):
            ch = wid + s * SC_WORKERS

            @pl.when(ch < nch)
            def _():
                pltpu.sync_copy(pos_hbm.at[ch], idx_v)
                pltpu.sync_copy(h_hbm.at[pl.ds(ch * SC_CHUNK, SC_CHUNK)], rows_v)
                copies = [pltpu.async_copy(rows_v, out_hbm.at[idx_v.at[k]], sem) for k in range(TOP_K)]
                for cp in copies:
                    cp.wait()

    return pl.kernel(
        body,
        out_type=jax.ShapeDtypeStruct((r_pad, 4, 128), jnp.uint32),
        mesh=_sc_mesh(),
        scratch_types=[pltpu.VMEM((TOP_K, SC_CHUNK), jnp.int32),
                       pltpu.VMEM((SC_CHUNK, 4, 128), jnp.uint32),
                       pltpu.SemaphoreType.DMA],
        compiler_params=_sc_params(),
        name="sc_dispatch",
    )(h2p, pos3)


def _sc_collect(y_sorted, pos3):
    nch = pos3.shape[0]
    steps = -(-nch // SC_WORKERS)
    half = SC_CHUNK // 2
    units = [(k, hh) for k in range(TOP_K) for hh in range(2)]

    def body(y_hbm, pos_hbm, out_hbm, idx_v, rows_a, rows_b, sem_a, sem_b):
        wid = lax.axis_index("s") * SC_CORES + lax.axis_index("c")
        bufs = (rows_a, rows_b)
        sems = (sem_a, sem_b)

        def gather(u):
            k, hh = units[u]
            return pltpu.async_copy(y_hbm.at[idx_v.at[k, pl.ds(hh * half, half)]], bufs[u % 2], sems[u % 2])

        @pl.loop(0, steps)
        def _(s):
            ch = wid + s * SC_WORKERS

            @pl.when(ch < nch)
            def _():
                pltpu.sync_copy(pos_hbm.at[ch], idx_v)
                pending = gather(0)
                for u, (k, hh) in enumerate(units):
                    nxt = gather(u + 1) if u + 1 < len(units) else None
                    pending.wait()
                    pltpu.sync_copy(bufs[u % 2], out_hbm.at[k, pl.ds(ch * SC_CHUNK + hh * half, half)])
                    pending = nxt

    return pl.kernel(
        body,
        out_type=jax.ShapeDtypeStruct((TOP_K, nch * SC_CHUNK, 4, 128), jnp.uint32),
        mesh=_sc_mesh(),
        scratch_types=[pltpu.VMEM((TOP_K, SC_CHUNK), jnp.int32),
                       pltpu.VMEM((half, 4, 128), jnp.uint32),
                       pltpu.VMEM((half, 4, 128), jnp.uint32),
                       pltpu.SemaphoreType.DMA, pltpu.SemaphoreType.DMA],
        compiler_params=_sc_params(),
        name="sc_collect",
    )(y_sorted, pos3)


def _shared_kernel(x_ref, h2p_ref, yg_ref, gate_ref, wg_ref, wu_ref, wd_ref, g2_ref, o_ref):
    h_lo, h_hi = _unpack_rows(h2p_ref, TM_F)
    h = jnp.concatenate([h_lo.astype(BF16), h_hi.astype(BF16)], axis=1)
    a = jnp.dot(h, wg_ref[...], preferred_element_type=F32)
    b = jnp.dot(h, wu_ref[...], preferred_element_type=F32)
    hid = (a * jax.nn.sigmoid(a) * b).astype(BF16)
    f = jnp.dot(hid, wd_ref[...], preferred_element_type=F32)
    gate = gate_ref[...]
    f_lo = f[:, :D // 2]
    f_hi = f[:, D // 2:]
    for k in range(TOP_K):
        y_lo, y_hi = _unpack_rows(yg_ref, TM_F, lead=k)
        f_lo = f_lo + gate[:, k:k + 1] * y_lo
        f_hi = f_hi + gate[:, k:k + 1] * y_hi
    o_ref[...] = x_ref[...] + g2_ref[...] * jnp.concatenate([f_lo, f_hi], axis=1)


def _shared_residual(xa, h2p, yg, gates, wsg_bf, wsu_bf, wsd_bf, mod3, n_rows):
    tm = TM_F
    row = lambda i: (i, 0)
    const2 = lambda i: (0, 0)
    return pl.pallas_call(
        _shared_kernel,
        out_shape=jax.ShapeDtypeStruct((n_rows, D), F32),
        grid=(n_rows // tm,),
        in_specs=[pl.BlockSpec((tm, D), row), pl.BlockSpec((tm * 4, 128), row),
                  pl.BlockSpec((TOP_K, tm * 4, 128), lambda i: (0, i, 0)),
                  pl.BlockSpec((tm, TOP_K), row),
                  pl.BlockSpec((D, D_SHARED), const2), pl.BlockSpec((D, D_SHARED), const2),
                  pl.BlockSpec((D_SHARED, D), const2), _mod_spec(tm, 5)],
        out_specs=pl.BlockSpec((tm, D), row),
        compiler_params=_cparams(("parallel",), 48),
        name="shared_residual",
    )(xa, h2p, yg, gates, wsg_bf, wsu_bf, wsd_bf, mod3)


def _positions_kernel(offs_ref, idx_ref, rank_ref, pos_ref):
    idx = idx_ref[...]
    base = jnp.zeros_like(idx)
    for e in range(N_EXPERTS):
        base = jnp.where(idx == e, offs_ref[e], base)
    pos_ref[...] = rank_ref[...] + base


def _positions(offs, idx3, rank3):
    nch = idx3.shape[0]
    cb = nch // 2
    spec = pl.BlockSpec((cb, TOP_K, 128), lambda i, offs: (i, 0, 0))
    return pl.pallas_call(
        _positions_kernel,
        out_shape=jax.ShapeDtypeStruct((nch, TOP_K, 128), jnp.int32),
        grid_spec=pltpu.PrefetchScalarGridSpec(num_scalar_prefetch=1, grid=(nch // cb,),
                                               in_specs=[spec, spec], out_specs=spec),
        compiler_params=_cparams(("parallel",)),
        name="positions",
    )(offs, idx3, rank3)


def _route_positions(idx3, rank3, counts, n_rows):
    tm = TM_X
    counts = counts.astype(jnp.int32)
    padded = ((counts + tm - 1) // tm) * tm
    ends = jnp.cumsum(padded)
    offs = ends - padded
    pos3 = _positions(offs.astype(jnp.int32), idx3, rank3)
    r_pad = n_rows * TOP_K + N_EXPERTS * tm
    nt = r_pad // tm
    tile_ids = jnp.arange(nt, dtype=jnp.int32)
    tile_expert = jnp.sum((ends // tm)[None, :] <= tile_ids[:, None], axis=1)
    tile_expert = jnp.minimum(tile_expert, N_EXPERTS - 1).astype(jnp.int32)
    n_used = (ends[-1] // tm).astype(jnp.int32).reshape(1)
    e_ids = jnp.arange(N_EXPERTS, dtype=jnp.int32)
    nonempty = counts > 0
    later = jnp.where(nonempty[None, :] & (e_ids[None, :] > e_ids[:, None]), e_ids[None, :], N_EXPERTS)
    next_e = jnp.min(later, axis=1)
    next_e = jnp.where(next_e == N_EXPERTS, -1, next_e).astype(jnp.int32)
    slot_e = ((jnp.cumsum(nonempty.astype(jnp.int32)) - nonempty.astype(jnp.int32)) % 2).astype(jnp.int32)
    sched = (tile_expert, next_e[tile_expert], slot_e[tile_expert], n_used)
    return pos3, sched, r_pad


def _in_weights(w):
    def regroup(cols):
        return cols.reshape(D, 2, HEADS, QK_DIM).transpose(0, 2, 1, 3).reshape(D, 2 * HEADS * QK_DIM)
    return jnp.concatenate([w[:, :OFF_Q], regroup(w[:, OFF_Q:OFF_K]), regroup(w[:, OFF_K:OFF_V]), w[:, OFF_V:]],
                           axis=1).astype(BF16)


def _rope_tables():
    t = jnp.arange(L)
    row = (t // GRID_W).astype(F32)
    col = (t % GRID_W).astype(F32)
    n_freq = QK_DIM // 4
    inv = ROPE_BASE ** (-jnp.arange(n_freq, dtype=F32) / n_freq)
    ar = row[:, None] * inv
    ac = col[:, None] * inv
    cos64 = jnp.concatenate([jnp.cos(ar), jnp.cos(ar), jnp.cos(ac), jnp.cos(ac)], axis=1)
    sin64 = jnp.concatenate([-jnp.sin(ar), jnp.sin(ar), -jnp.sin(ac), jnp.sin(ac)], axis=1)
    cos_t = jnp.concatenate([jnp.tile(cos64, (1, 2)), jnp.ones((TM_IN, 128), F32)], axis=0)
    sin_t = jnp.concatenate([jnp.tile(sin64, (1, 2)), jnp.zeros((TM_IN, 128), F32)], axis=0)
    return cos_t, sin_t


def _split_bf16(w):
    hi = w.astype(BF16)
    return hi, (w - hi.astype(F32)).astype(BF16)


def kernel(x, c, ctx, c_ctx, w_ada, b_ada, g_norm1, g_norm2, w_in, w_out, g_v, w_s, b_s, w_conv, g_q, g_k,
           lam_q1, lam_k1, lam_q2, lam_k2, g_sub, w_router, b_router, w_gate, w_up, w_down,
           ws_gate, ws_up, ws_down):
    src = (x.reshape(NL, D), ctx.reshape(NC, D))
    cc = jnp.concatenate([c, c_ctx[None, :], jnp.zeros((MOD_ROWS - B - 1, D), F32)], axis=0)
    mod = _ada(cc, w_ada, b_ada)
    cos_t, sin_t = _rope_tables()
    bd = jnp.asarray(np.kron(np.eye(8, dtype=np.float32), np.full((64, 64), 1.0 / 64, np.float32)), BF16)

    for l in range(DEPTH):
        last = l == DEPTH - 1
        lam_init = 0.8 - 0.6 * math.exp(-0.3 * l)
        lam = (jnp.exp(jnp.sum(lam_q1[l] * lam_k1[l])) - jnp.exp(jnp.sum(lam_q2[l] * lam_k2[l])) + lam_init)
        bound = (QK_DIM * jnp.max(jnp.abs(g_q[l])) * jnp.max(jnp.abs(g_k[l]))
                 * (QK_DIM ** -0.5 * LOG2E) * ATT_BOUND_MARGIN)
        use_bound = (2.0 * bound < ATT_MAX_SHIFT_RANGE).astype(F32)
        lam = jnp.stack([lam, bound, use_bound]).astype(F32)
        mod3 = mod[l].reshape(MOD_ROWS, 1, 6 * D)
        w_in_bf = _in_weights(w_in[l])
        bias_t = jnp.repeat(b_s[l].T, A_GD, axis=1)
        wconv = jnp.concatenate([w_conv[l], jnp.zeros((5, B_WIDTH), F32)], axis=0)
        yab, q, kk, v = _in_mix(src[0], src[1], g_norm1[l], mod3, w_in_bf, cos_t, sin_t,
                                     g_v[l].reshape(1, A_WIDTH), w_s[l].astype(BF16), bias_t, wconv,
                                     jnp.tile(g_q[l], 8).reshape(1, 512), jnp.tile(g_k[l], 8).reshape(1, 512), bd)
        gsub = g_sub[l].reshape(1, V_DIM)
        coef = 1.0 - lam_init
        n_rows = NL if last else NR
        yc = _attention(lam, q, kk, v, gsub, coef, ctx_queries=False)
        yc_ctx = yc if last else _attention(lam, q, kk, v, gsub, coef, ctx_queries=True)
        wr_hi, wr_lo = _split_bf16(w_router[l].T)
        wo_bf = w_out[l].astype(BF16)
        ws_bf = (ws_gate[l].astype(BF16), ws_up[l].astype(BF16), ws_down[l].astype(BF16))
        xa, h2p, idx3, gate3, rank3, counts = _out_router(
            src[0], src[1], yab, yc, yc_ctx, wo_bf, g_norm2[l], mod3, wr_hi, wr_lo, b_router[l], n_rows)
        pos3, sched, r_pad = _route_positions(idx3, rank3, counts[:, 0], n_rows)
        x_sorted = _sc_dispatch(h2p.reshape(n_rows, 4, 128), pos3, r_pad)
        y_sorted = _experts(l, sched, x_sorted.reshape(r_pad * 4, 128), w_gate, w_up, w_down)
        yg = _sc_collect(y_sorted.reshape(r_pad, 4, 128), pos3)
        gates = gate3.transpose(0, 2, 1).reshape(n_rows, TOP_K)
        xa = _shared_residual(xa, h2p, yg.reshape(TOP_K, n_rows * 4, 128), gates, *ws_bf, mod3, n_rows)
        src = (xa, xa)
    return xa.reshape(B, L, D)
```

```python
import functools
import math

import numpy as np
import jax
import jax.numpy as jnp
from jax import lax
from jax.experimental import pallas as pl
from jax.experimental.pallas import tpu as pltpu
from jax.experimental.pallas import tpu_sc as plsc

F32 = jnp.float32
BF16 = jnp.bfloat16

D = 1024
B = 8
L = 2048
DEPTH = 2
GRID_W = 64
CTX = 256
A_WIDTH = 256
A_GROUPS = 4
A_GD = 64
CHUNK = 128
B_WIDTH = 256
C_WIDTH = 512
HEADS = 4
V_DIM = 128
QK_DIM = 64
ROPE_BASE = 10000.0
OFF_BB = 512
OFF_BC = 768
OFF_BX = 1024
OFF_Q = 1280
OFF_K = 1792
OFF_V = 2304
D_IN = 2816
N_EXPERTS = 64
TOP_K = 8
D_EXPERT = 256
D_SHARED = 256
ROUTED_SCALE = 2.5
EPS = 1e-6

NL = B * L
NC = B * CTX
NR = NL + NC
MOD_ROWS = 16
LOG2E = 1.4426950408889634

TM_IN = 512
TQ = 1024
ATT_CHAIN = 256
ATT_BOUND_MARGIN = 1.02
ATT_MAX_SHIFT_RANGE = 100.0
TM_OUT = 512
TM_X = 1024
X_CHAINS = 2
X_TAIL_ROWS = 512
X_TAIL = 256
X_RING = 3
TM_F = 512

_DN_T = (((1,), (1,)), ((), ()))


def _cparams(sem, vmem_mb=None):
    kw = dict(dimension_semantics=sem)
    if vmem_mb is not None:
        kw["vmem_limit_bytes"] = vmem_mb * 1024 * 1024
    return pltpu.CompilerParams(**kw)


def _mod_row(i, tm):
    return jnp.where(i < NL // tm, i // (L // tm), B)


def _mod_spec(tm, chunk):
    return pl.BlockSpec((None, 1, D), lambda i: (_mod_row(i, tm), 0, chunk))


def _ada_kernel(c_ref, w_ref, b_ref, o_ref):
    c = c_ref[...]
    cs = c * jax.nn.sigmoid(c)
    o_ref[...] = jnp.dot(cs, w_ref[...], preferred_element_type=F32,
                         precision=lax.Precision.HIGHEST) + b_ref[...]


def _ada(cc, w_ada, b_ada):
    nb = 6
    return pl.pallas_call(
        _ada_kernel,
        out_shape=jax.ShapeDtypeStruct((DEPTH, MOD_ROWS, 6 * D), F32),
        grid=(DEPTH, nb),
        in_specs=[pl.BlockSpec((MOD_ROWS, D), lambda l, j: (0, 0)),
                  pl.BlockSpec((None, D, D), lambda l, j: (l, 0, j)),
                  pl.BlockSpec((None, 1, D), lambda l, j: (l, 0, j))],
        out_specs=pl.BlockSpec((None, MOD_ROWS, D), lambda l, j: (l, 0, j)),
        compiler_params=_cparams(("arbitrary", "arbitrary"), 40),
        name="ada_mod",
    )(cc, w_ada, b_ada.reshape(DEPTH, 1, 6 * D))


def _rms_mod(x, g, sc, sh):
    ms = jnp.mean(x * x, axis=-1, keepdims=True)
    return x * lax.rsqrt(ms + EPS) * (g * (1.0 + sc)) + sh


def _two_source_specs(tm, n_first, width=D):
    return [pl.BlockSpec((tm, width), lambda i: (jnp.minimum(i, n_first - 1), 0)),
            pl.BlockSpec((tm, width), lambda i: (jnp.maximum(i - n_first, 0), 0))]


def _two_source_rows(a_ref, b_ref, n_first):
    return jnp.where(pl.program_id(0) < n_first, a_ref[...], b_ref[...])


def _group_rms(t, g, bd):
    ms = jnp.dot((t * t).astype(BF16), bd, preferred_element_type=F32)
    return t * lax.rsqrt(ms + EPS) * g


def _rope(t, cos, sin):
    w = t.shape[1]
    lane = lax.broadcasted_iota(jnp.int32, t.shape, 1)
    first = (lane % 32) < 16
    partner = jnp.where(first, pltpu.roll(t, w - 16, 1), pltpu.roll(t, 16, 1))
    cos4 = jnp.concatenate([cos] * (w // 128), axis=1)
    sin4 = jnp.concatenate([sin] * (w // 128), axis=1)
    return t * cos4 + partner * sin4


def _in_mix_kernel(xa_ref, xb_ref, pa_ref, pb_ref, na_ref, nb_ref, g_ref, sh_ref, sc_ref, w_ref,
                   cos_ref, sin_ref, gv_ref, ws_ref, bias_ref, wconv_ref, gq_ref, gk_ref, bd_ref,
                   yab_ref, q_ref, kk_ref, v_ref, *, n_first):
    tm = TM_IN
    i = pl.program_id(0)
    tiles_per_seq = L // tm
    is_lat = i < NL // tm
    is_start = jnp.logical_or(jnp.logical_not(is_lat), i % tiles_per_seq == 0)
    is_end = jnp.logical_or(jnp.logical_not(is_lat), i % tiles_per_seq == tiles_per_seq - 1)
    first = i < n_first
    g, sc, sh = g_ref[...], sc_ref[...], sh_ref[...]

    h = _rms_mod(jnp.where(first, xa_ref[...], xb_ref[...]), g, sc, sh)
    p = jnp.dot(h.astype(BF16), w_ref[...], preferred_element_type=F32)
    v_ref[...] = p[:, OFF_V:].astype(BF16)
    halo = jnp.concatenate([jnp.where(first, pa_ref[...], pb_ref[...]),
                            jnp.where(first, na_ref[...], nb_ref[...])], axis=0)
    ph = jnp.dot(_rms_mod(halo, g, sc, sh).astype(BF16), w_ref[:, OFF_BC:OFF_Q], preferred_element_type=F32)
    zh = ph[:, :B_WIDTH] * ph[:, B_WIDTH:]
    zp = jnp.where(is_start, 0.0, zh[15:16])
    zn = jnp.where(is_end, 0.0, zh[16:17])

    uv = p[:, 0:2 * A_WIDTH]
    uv = 0.5 * uv * (1.0 + lax.erf(uv * (2.0 ** -0.5)))
    u = uv[:, :A_WIDTH]
    v = uv[:, A_WIDTH:]
    ms = jnp.mean(v * v, axis=-1, keepdims=True)
    vb = (v * lax.rsqrt(ms + EPS) * gv_ref[...]).astype(BF16)
    lane = lax.broadcasted_iota(jnp.int32, (CHUNK, 128), 1)
    mixes = []
    for c in range(tm // CHUNK):
        vc = vb[c * CHUNK:(c + 1) * CHUNK]
        halves = []
        for j in range(2):
            vj = vc[:, j * 128:(j + 1) * 128]
            m0 = jnp.dot(ws_ref[2 * j], vj, preferred_element_type=F32)
            m1 = jnp.dot(ws_ref[2 * j + 1], vj, preferred_element_type=F32)
            halves.append(jnp.where(lane < A_GD, m0, m1))
        mixes.append(jnp.concatenate(halves, axis=1) + bias_ref[...])
    ya = u * jnp.concatenate(mixes, axis=0)

    bg = p[:, OFF_BB:OFF_BC]
    z = p[:, OFF_BC:OFF_BX] * p[:, OFF_BX:OFF_Q]
    row = lax.broadcasted_iota(jnp.int32, z.shape, 0)
    inner = jnp.logical_not(is_lat)
    z_prev = jnp.where(row == 0, zp, pltpu.roll(z, 1, 0))
    z_prev = jnp.where(jnp.logical_and(inner, row % CTX == 0), 0.0, z_prev)
    z_next = jnp.where(row == tm - 1, zn, pltpu.roll(z, tm - 1, 0))
    z_next = jnp.where(jnp.logical_and(inner, row % CTX == CTX - 1), 0.0, z_next)
    yb = bg * (z_prev * wconv_ref[0:1] + z * wconv_ref[1:2] + z_next * wconv_ref[2:3])
    yab_ref[...] = jnp.concatenate([ya, yb], axis=1).astype(BF16)

    cos = cos_ref[...]
    sin = sin_ref[...]
    bd = bd_ref[...]
    q = _rope(_group_rms(p[:, OFF_Q:OFF_K], gq_ref[...], bd), cos, sin)
    q = q * (QK_DIM ** -0.5 * LOG2E)
    q_ref[...] = q.astype(BF16)
    k = _rope(_group_rms(p[:, OFF_K:OFF_V], gk_ref[...], bd), cos, sin)
    kk_ref[...] = k.astype(BF16)


def _in_mix(x_first, x_second, g, mod3, w_bf, cos_t, sin_t, gv, ws_bf, bias_t, wconv, gq, gk, bd):
    tm = TM_IN
    n_first = x_first.shape[0] // tm
    hb = tm // 16
    nhb_first = x_first.shape[0] // 16
    nhb_second = x_second.shape[0] // 16
    pos_blocks = L // tm

    def tab_map(i):
        return (jnp.where(i < NL // tm, i % pos_blocks, pos_blocks), 0)

    def halo_specs(shift):
        blk = lambda i: (i * tm + shift) // 16
        return [pl.BlockSpec((16, D), lambda i: (jnp.clip(blk(i), 0, nhb_first - 1), 0)),
                pl.BlockSpec((16, D), lambda i: (jnp.clip(blk(i) - nhb_first, 0, nhb_second - 1), 0))]

    const2 = lambda i: (0, 0)
    row512 = pl.BlockSpec((tm, 512), lambda i: (i, 0))
    return pl.pallas_call(
        functools.partial(_in_mix_kernel, n_first=n_first),
        out_shape=[jax.ShapeDtypeStruct((NR, 512), BF16)] * 4,
        grid=(NR // tm,),
        in_specs=_two_source_specs(tm, n_first) + halo_specs(-1) + halo_specs(tm)
                 + [pl.BlockSpec((1, D), const2), _mod_spec(tm, 0), _mod_spec(tm, 1),
                    pl.BlockSpec((D, D_IN), const2),
                    pl.BlockSpec((tm, 128), tab_map), pl.BlockSpec((tm, 128), tab_map),
                    pl.BlockSpec((1, A_WIDTH), const2),
                    pl.BlockSpec((A_GROUPS, CHUNK, CHUNK), lambda i: (0, 0, 0)),
                    pl.BlockSpec((CHUNK, A_WIDTH), const2),
                    pl.BlockSpec((8, B_WIDTH), const2),
                    pl.BlockSpec((1, 512), const2), pl.BlockSpec((1, 512), const2),
                    pl.BlockSpec((512, 512), const2)],
        out_specs=[row512] * 4,
        compiler_params=_cparams(("parallel",), 56),
        name="in_mix",
    )(x_first, x_second, x_first, x_second, x_first, x_second, g.reshape(1, D), mod3, mod3, w_bf,
      cos_t, sin_t, gv, ws_bf, bias_t, wconv, gq, gk, bd)


def _attn_kernel(lam_ref, q_ref, *rest, n_seg, coef, tq):
    kv_refs = rest[:2 * n_seg]
    gsub_ref, o_ref, k_scr, vt_scr = rest[2 * n_seg:]

    @pl.when(pl.program_id(2) == 0)
    def _():
        off = 0
        for s in range(n_seg):
            n = kv_refs[s].shape[0]
            k_scr[off:off + n, :] = kv_refs[s][...]
            vt_scr[0:V_DIM, off:off + n] = kv_refs[n_seg + s][...].astype(F32).T.astype(BF16)
            off += n
        ones_row = lax.broadcasted_iota(jnp.int32, (16, off), 0) == 0
        vt_scr[V_DIM:, :] = jnp.where(ones_row, 1.0, 0.0).astype(BF16)

    lam = lam_ref[0]
    shift = lam_ref[1]
    qc = ATT_CHAIN

    def scores(c):
        rows = slice(c * qc, (c + 1) * qc)
        q = q_ref[rows, :]
        lane = lax.broadcasted_iota(jnp.int32, q.shape, 1)
        zero = jnp.zeros_like(q)
        qs = jnp.concatenate([jnp.where(lane < QK_DIM, q, zero), jnp.where(lane >= QK_DIM, q, zero)], axis=0)
        return lax.dot_general(k_scr[...], qs, _DN_T, preferred_element_type=F32)

    def finish(c, pt):
        ot = jnp.dot(vt_scr[...], pt, preferred_element_type=F32)
        inv = 1.0 / ot[V_DIM:V_DIM + 1, :]
        dt = ot[0:V_DIM, :qc] * inv[:, :qc] - ot[0:V_DIM, qc:] * (lam * inv[:, qc:])
        o = dt.T
        ms = jnp.mean(o * o, axis=-1, keepdims=True)
        o_ref[c * qc:(c + 1) * qc, :] = (o * lax.rsqrt(ms + EPS) * gsub_ref[...] * coef).astype(o_ref.dtype)

    @pl.when(lam_ref[2] > 0.5)
    def _():
        for c in range(tq // qc):
            finish(c, jnp.exp2(scores(c) - shift).astype(BF16))

    @pl.when(lam_ref[2] <= 0.5)
    def _():
        sts = [scores(c) for c in range(tq // qc)]
        for c in range(tq // qc):
            st = sts[c]
            finish(c, jnp.exp2(st - jnp.max(st, axis=0, keepdims=True)).astype(BF16))


def _attention(lam, q, kk, v, gsub, coef, *, ctx_queries):
    if ctx_queries:
        tq = CTX
        nq, lk, n_seg = 1, CTX, 1
        q_map = lambda b, h, qi: (NL // tq + b, h)
        kv_specs = [pl.BlockSpec((CTX, 128), lambda b, h, qi: (NL // CTX + b, h)),
                    pl.BlockSpec((CTX, 128), lambda b, h, qi: (NL // CTX + b, h))]
        kv_args = [kk, v]
        rows = NC
        o_map = lambda b, h, qi: (b, h)
    else:
        tq = TQ
        nq, lk, n_seg = L // tq, CTX + L, 2
        q_map = lambda b, h, qi: (b * (L // tq) + qi, h)
        kv_specs = [pl.BlockSpec((CTX, 128), lambda b, h, qi: (NL // CTX + b, h)),
                    pl.BlockSpec((L, 128), lambda b, h, qi: (b, h)),
                    pl.BlockSpec((CTX, 128), lambda b, h, qi: (NL // CTX + b, h)),
                    pl.BlockSpec((L, 128), lambda b, h, qi: (b, h))]
        kv_args = [kk, kk, v, v]
        rows = NL
        o_map = lambda b, h, qi: (b * (L // tq) + qi, h)
    return pl.pallas_call(
        functools.partial(_attn_kernel, n_seg=n_seg, coef=coef, tq=tq),
        out_shape=jax.ShapeDtypeStruct((rows, C_WIDTH), BF16),
        grid=(B, HEADS, nq),
        in_specs=[pl.BlockSpec(memory_space=pltpu.SMEM),
                  pl.BlockSpec((tq, 128), q_map)]
                 + kv_specs + [pl.BlockSpec((1, V_DIM), lambda b, h, qi: (0, 0))],
        out_specs=pl.BlockSpec((tq, 128), o_map),
        scratch_shapes=[pltpu.VMEM((lk, 128), BF16), pltpu.VMEM((V_DIM + 16, lk), BF16)],
        compiler_params=_cparams(("parallel", "parallel", "arbitrary"), 56),
        name="attn_ctx" if ctx_queries else "attn_lat",
    )(lam, q, *kv_args, gsub)


def _pack_rows(t, out_ref, row0=0):
    half = D // 2
    w = pltpu.pack_elementwise([t[:, :half], t[:, half:]], packed_dtype=BF16)
    w = lax.bitcast_convert_type(w, jnp.uint32)
    rows = t.shape[0]
    for j in range(4):
        out_ref[pl.ds(4 * row0 + j, rows, stride=4), :] = w[:, j * 128:(j + 1) * 128]


def _unpack_rows(ref, rows, lead=None, row0=0):
    los, his = [], []
    for j in range(4):
        sl = pl.ds(4 * row0 + j, rows, stride=4)
        w = ref[sl, :] if lead is None else ref[lead, sl, :]
        los.append(pltpu.unpack_elementwise(w, index=0, packed_dtype=BF16, unpacked_dtype=F32))
        his.append(pltpu.unpack_elementwise(w, index=1, packed_dtype=BF16, unpacked_dtype=F32))
    return jnp.concatenate(los, axis=1), jnp.concatenate(his, axis=1)


def _out_router_kernel(xa_ref, xb_ref, yab_ref, yca_ref, ycb_ref, wo_ref, g1_ref, g2n_ref, sh2_ref, sc2_ref, wrh_ref,
                       wrl_ref, br_ref, xo_ref, h2p_ref, idx_ref, gate_ref, rank_ref, cnt_ref, run_ref, *,
                       n_first, n_first_c):
    tm = TM_OUT

    @pl.when(pl.program_id(0) == 0)
    def _():
        run_ref[...] = jnp.zeros_like(run_ref)

    yc = _two_source_rows(yca_ref, ycb_ref, n_first_c)
    y = jnp.dot(jnp.concatenate([yab_ref[...], yc], axis=1), wo_ref[...], preferred_element_type=F32)
    x = _two_source_rows(xa_ref, xb_ref, n_first) + g1_ref[...] * y
    xo_ref[...] = x
    h2 = _rms_mod(x, g2n_ref[...], sc2_ref[...], sh2_ref[...])
    hi = h2.astype(BF16)
    _pack_rows(h2, h2p_ref)
    lo = (h2 - hi.astype(F32)).astype(BF16)
    wh = wrh_ref[...]
    z = (lax.dot_general(wh, hi, _DN_T, preferred_element_type=F32)
         + lax.dot_general(wh, lo, _DN_T, preferred_element_type=F32)
         + lax.dot_general(wrl_ref[...], hi, _DN_T, preferred_element_type=F32))
    scores = jax.nn.sigmoid(z)
    work = scores + br_ref[...]
    eio = lax.broadcasted_iota(jnp.int32, work.shape, 0)
    idxs, sels, hits = [], [], []
    for _ in range(TOP_K):
        m = jnp.max(work, axis=0, keepdims=True)
        idx = jnp.min(jnp.where(work == m, eio, N_EXPERTS), axis=0, keepdims=True)
        hit = eio == idx
        sels.append(jnp.sum(jnp.where(hit, scores, 0.0), axis=0, keepdims=True))
        idxs.append(idx)
        hits.append(hit)
        work = jnp.where(hit, -jnp.inf, work)
    sel = jnp.concatenate(sels, axis=0)
    idx8 = jnp.concatenate(idxs, axis=0)
    gate8 = sel / jnp.sum(sel, axis=0, keepdims=True) * ROUTED_SCALE

    chosen = functools.reduce(jnp.logical_or, hits)
    before = (lax.broadcasted_iota(jnp.int32, (tm, tm), 0) < lax.broadcasted_iota(jnp.int32, (tm, tm), 1))
    prefix = jnp.dot(jnp.where(chosen, 1.0, 0.0).astype(BF16), jnp.where(before, 1.0, 0.0).astype(BF16),
                     preferred_element_type=F32)
    rank_dense = prefix + run_ref[:, 0:1]
    rank8 = jnp.concatenate([jnp.sum(jnp.where(h, rank_dense, 0.0), axis=0, keepdims=True) for h in hits],
                            axis=0).astype(jnp.int32)
    run = run_ref[...] + jnp.sum(jnp.where(chosen, 1.0, 0.0), axis=1, keepdims=True)
    run_ref[...] = run
    cnt_ref[...] = run
    for c in range(tm // 128):
        idx_ref[c] = idx8[:, c * 128:(c + 1) * 128]
        gate_ref[c] = gate8[:, c * 128:(c + 1) * 128]
        rank_ref[c] = rank8[:, c * 128:(c + 1) * 128]


def _out_router(x_first, x_second, yab, yc_first, yc_second, wo_bf, g2n, mod3, wr_hi, wr_lo, br, n_rows):
    tm = TM_OUT
    n_first = x_first.shape[0] // tm
    n_first_c = yc_first.shape[0] // tm
    const2 = lambda i: (0, 0)
    row = lambda i: (i, 0)
    chunk3 = pl.BlockSpec((tm // 128, TOP_K, 128), lambda i: (i, 0, 0))
    nch = n_rows // 128
    return pl.pallas_call(
        functools.partial(_out_router_kernel, n_first=n_first, n_first_c=n_first_c),
        out_shape=[jax.ShapeDtypeStruct((n_rows, D), F32),
                   jax.ShapeDtypeStruct((n_rows * 4, 128), jnp.uint32),
                   jax.ShapeDtypeStruct((nch, TOP_K, 128), jnp.int32),
                   jax.ShapeDtypeStruct((nch, TOP_K, 128), F32),
                   jax.ShapeDtypeStruct((nch, TOP_K, 128), jnp.int32),
                   jax.ShapeDtypeStruct((N_EXPERTS, 128), F32)],
        grid=(n_rows // tm,),
        in_specs=_two_source_specs(tm, n_first)
                 + [pl.BlockSpec((tm, 512), row)]
                 + _two_source_specs(tm, n_first_c, C_WIDTH)
                 + [pl.BlockSpec((D, D), const2),
                  _mod_spec(tm, 2),
                  pl.BlockSpec((1, D), const2), _mod_spec(tm, 3), _mod_spec(tm, 4),
                  pl.BlockSpec((N_EXPERTS, D), const2), pl.BlockSpec((N_EXPERTS, D), const2),
                  pl.BlockSpec((N_EXPERTS, 1), const2)],
        out_specs=[pl.BlockSpec((tm, D), row), pl.BlockSpec((tm * 4, 128), row),
                   chunk3, chunk3, chunk3, pl.BlockSpec((N_EXPERTS, 128), const2)],
        scratch_shapes=[pltpu.VMEM((N_EXPERTS, 128), F32)],
        compiler_params=_cparams(("arbitrary",), 48),
        name="out_router",
    )(x_first, x_second, yab, yc_first, yc_second, wo_bf, mod3, g2n.reshape(1, D), mod3, mod3, wr_hi, wr_lo,
      br.reshape(N_EXPERTS, 1))


def _experts_kernel(te_ref, nx_ref, ws_ref, tr_ref, nu_ref, x_hbm, wg_hbm, wu_hbm, wd_hbm, y_ref,
                    wgu_s, wd_s, x_ring, x_sems, wg_f, wu_f, wd_f, w_sems, *, layer):
    i = pl.program_id(0)
    n_used = nu_ref[0]
    used = i < n_used
    new_expert = jnp.logical_or(i == 0, te_ref[i] != te_ref[jnp.maximum(i - 1, 0)])

    def w_copies(e, slot):
        return [pltpu.make_async_copy(wg_hbm.at[layer, e], wg_f.at[slot], w_sems.at[slot, 0]),
                pltpu.make_async_copy(wu_hbm.at[layer, e], wu_f.at[slot], w_sems.at[slot, 1]),
                pltpu.make_async_copy(wd_hbm.at[layer, e], wd_f.at[slot], w_sems.at[slot, 2])]

    @pl.when(i == 0)
    def _():
        for cp in w_copies(te_ref[0], ws_ref[0]):
            cp.start()

    ahead = X_RING - 1

    def x_copy(t):
        slot = t % X_RING
        return pltpu.make_async_copy(x_hbm.at[pl.ds(t * (4 * TM_X), 4 * TM_X)], x_ring.at[slot], x_sems.at[slot])

    @pl.when(i == 0)
    def _():
        for t in range(ahead):
            @pl.when(t < n_used)
            def _():
                x_copy(t).start()

    @pl.when(i + ahead < n_used)
    def _():
        x_copy(i + ahead).start()

    @pl.when(jnp.logical_and(used, new_expert))
    def _():
        slot = ws_ref[i]
        for cp in w_copies(te_ref[i], slot):
            cp.wait()

        @pl.when(nx_ref[i] >= 0)
        def _():
            for cp in w_copies(nx_ref[i], 1 - slot):
                cp.start()

        wgu_s[:, 0:D_EXPERT] = wg_f[slot].astype(BF16)
        wgu_s[:, D_EXPERT:] = wu_f[slot].astype(BF16)
        wd_s[...] = wd_f[slot].astype(BF16)

    def first_dot(x_ref, row0, rows):
        x_lo, x_hi = _unpack_rows(x_ref, rows, row0=row0)
        x = jnp.concatenate([x_lo.astype(BF16), x_hi.astype(BF16)], axis=1)
        return jnp.dot(x, wgu_s[...], preferred_element_type=F32)

    def second_dot(ab, row0):
        a = ab[:, :D_EXPERT]
        hid = (a * jax.nn.sigmoid(a) * ab[:, D_EXPERT:]).astype(BF16)
        _pack_rows(jnp.dot(hid, wd_s[...], preferred_element_type=F32), y_ref, row0=row0)

    @pl.when(used)
    def _():
        x_copy(i).wait()

    valid = tr_ref[i]

    @pl.when(jnp.logical_and(used, valid > X_TAIL_ROWS))
    def _():
        x_ref = x_ring.at[i % X_RING]
        rc = TM_X // X_CHAINS
        abs_ = [first_dot(x_ref, c * rc, rc) for c in range(X_CHAINS)]
        for c in range(X_CHAINS):
            second_dot(abs_[c], c * rc)

    for c in range(X_TAIL_ROWS // X_TAIL):
        @pl.when(jnp.logical_and(used, jnp.logical_and(valid <= X_TAIL_ROWS, c * X_TAIL < valid)))
        def _():
            second_dot(first_dot(x_ring.at[i % X_RING], c * X_TAIL, X_TAIL), c * X_TAIL)


def _experts(layer, sched, x_sorted, w_gate, w_up, w_down):
    tm = TM_X
    r_pad = x_sorted.shape[0] // 4
    nt = r_pad // tm
    row = lambda i, te, nx, ws, tr, nu: (jnp.minimum(i, nu[0] - 1), 0)
    hbm = pl.BlockSpec(memory_space=pl.ANY)
    grid_spec = pltpu.PrefetchScalarGridSpec(
        num_scalar_prefetch=5,
        grid=(nt,),
        in_specs=[hbm, hbm, hbm, hbm],
        out_specs=pl.BlockSpec((tm * 4, 128), row),
        scratch_shapes=[pltpu.VMEM((D, 2 * D_EXPERT), BF16), pltpu.VMEM((D_EXPERT, D), BF16),
                        pltpu.VMEM((X_RING, tm * 4, 128), jnp.uint32), pltpu.SemaphoreType.DMA((X_RING,)),
                        pltpu.VMEM((2, D, D_EXPERT), F32), pltpu.VMEM((2, D, D_EXPERT), F32),
                        pltpu.VMEM((2, D_EXPERT, D), F32), pltpu.SemaphoreType.DMA((2, 3))],
    )
    return pl.pallas_call(
        functools.partial(_experts_kernel, layer=layer),
        out_shape=jax.ShapeDtypeStruct((r_pad * 4, 128), jnp.uint32),
        grid_spec=grid_spec,
        compiler_params=_cparams(("arbitrary",), 48),
        name="experts",
    )(*sched, x_sorted, w_gate, w_up, w_down)


SC_CORES = 2
SC_SUBCORES = 16
SC_WORKERS = SC_CORES * SC_SUBCORES
SC_CHUNK = 128


def _sc_mesh():
    return plsc.VectorSubcoreMesh(core_axis_name="c", subcore_axis_name="s")


def _sc_params():
    return pltpu.CompilerParams(use_tc_tiling_on_sc=True)


def _sc_dispatch(h2p, pos3, r_pad):
    nch = pos3.shape[0]
    steps = -(-nch // SC_WORKERS)

    def body(h_hbm, pos_hbm, out_hbm, idx_v, rows_v, sem):
        wid = lax.axis_index("s") * SC_CORES + lax.axis_index("c")

        @pl.loop(0, steps)
        def _(s):
            ch = wid + s * SC_WORKERS

            @pl.when(ch < nch)
            def _():
                pltpu.sync_copy(pos_hbm.at[ch], idx_v)
                pltpu.sync_copy(h_hbm.at[pl.ds(ch * SC_CHUNK, SC_CHUNK)], rows_v)
                copies = [pltpu.async_copy(rows_v, out_hbm.at[idx_v.at[k]], sem) for k in range(TOP_K)]
                for cp in copies:
                    cp.wait()

    return pl.kernel(
        body,
        out_type=jax.ShapeDtypeStruct((r_pad, 4, 128), jnp.uint32),
        mesh=_sc_mesh(),
        scratch_types=[pltpu.VMEM((TOP_K, SC_CHUNK), jnp.int32),
                       pltpu.VMEM((SC_CHUNK, 4, 128), jnp.uint32),
                       pltpu.SemaphoreType.DMA],
        compiler_params=_sc_params(),
        name="sc_dispatch",
    )(h2p, pos3)


def _sc_collect(y_sorted, pos3):
    nch = pos3.shape[0]
    steps = -(-nch // SC_WORKERS)
    half = SC_CHUNK // 2
    units = [(k, hh) for k in range(TOP_K) for hh in range(2)]

    def body(y_hbm, pos_hbm, out_hbm, idx_v, rows_a, rows_b, sem_a, sem_b):
        wid = lax.axis_index("s") * SC_CORES + lax.axis_index("c")
        bufs = (rows_a, rows_b)
        sems = (sem_a, sem_b)

        def gather(u):
            k, hh = units[u]
            return pltpu.async_copy(y_hbm.at[idx_v.at[k, pl.ds(hh * half, half)]], bufs[u % 2], sems[u % 2])

        @pl.loop(0, steps)
        def _(s):
            ch = wid + s * SC_WORKERS

            @pl.when(ch < nch)
            def _():
                pltpu.sync_copy(pos_hbm.at[ch], idx_v)
                pending = gather(0)
                for u, (k, hh) in enumerate(units):
                    nxt = gather(u + 1) if u + 1 < len(units) else None
                    pending.wait()
                    pltpu.sync_copy(bufs[u % 2], out_hbm.at[k, pl.ds(ch * SC_CHUNK + hh * half, half)])
                    pending = nxt

    return pl.kernel(
        body,
        out_type=jax.ShapeDtypeStruct((TOP_K, nch * SC_CHUNK, 4, 128), jnp.uint32),
        mesh=_sc_mesh(),
        scratch_types=[pltpu.VMEM((TOP_K, SC_CHUNK), jnp.int32),
                       pltpu.VMEM((half, 4, 128), jnp.uint32),
                       pltpu.VMEM((half, 4, 128), jnp.uint32),
                       pltpu.SemaphoreType.DMA, pltpu.SemaphoreType.DMA],
        compiler_params=_sc_params(),
        name="sc_collect",
    )(y_sorted, pos3)


def _shared_kernel(x_ref, h2p_ref, yg_ref, gate_ref, wg_ref, wu_ref, wd_ref, g2_ref, o_ref):
    h_lo, h_hi = _unpack_rows(h2p_ref, TM_F)
    h = jnp.concatenate([h_lo.astype(BF16), h_hi.astype(BF16)], axis=1)
    a = jnp.dot(h, wg_ref[...], preferred_element_type=F32)
    b = jnp.dot(h, wu_ref[...], preferred_element_type=F32)
    hid = (a * jax.nn.sigmoid(a) * b).astype(BF16)
    f = jnp.dot(hid, wd_ref[...], preferred_element_type=F32)
    gate = gate_ref[...]
    f_lo = f[:, :D // 2]
    f_hi = f[:, D // 2:]
    for k in range(TOP_K):
        y_lo, y_hi = _unpack_rows(yg_ref, TM_F, lead=k)
        f_lo = f_lo + gate[:, k:k + 1] * y_lo
        f_hi = f_hi + gate[:, k:k + 1] * y_hi
    o_ref[...] = x_ref[...] + g2_ref[...] * jnp.concatenate([f_lo, f_hi], axis=1)


def _shared_residual(xa, h2p, yg, gates, wsg_bf, wsu_bf, wsd_bf, mod3, n_rows):
    tm = TM_F
    row = lambda i: (i, 0)
    const2 = lambda i: (0, 0)
    return pl.pallas_call(
        _shared_kernel,
        out_shape=jax.ShapeDtypeStruct((n_rows, D), F32),
        grid=(n_rows // tm,),
        in_specs=[pl.BlockSpec((tm, D), row), pl.BlockSpec((tm * 4, 128), row),
                  pl.BlockSpec((TOP_K, tm * 4, 128), lambda i: (0, i, 0)),
                  pl.BlockSpec((tm, TOP_K), row),
                  pl.BlockSpec((D, D_SHARED), const2), pl.BlockSpec((D, D_SHARED), const2),
                  pl.BlockSpec((D_SHARED, D), const2), _mod_spec(tm, 5)],
        out_specs=pl.BlockSpec((tm, D), row),
        compiler_params=_cparams(("parallel",), 48),
        name="shared_residual",
    )(xa, h2p, yg, gates, wsg_bf, wsu_bf, wsd_bf, mod3)


def _positions_kernel(offs_ref, idx_ref, rank_ref, pos_ref):
    idx = idx_ref[...]
    base = jnp.zeros_like(idx)
    for e in range(N_EXPERTS):
        base = jnp.where(idx == e, offs_ref[e], base)
    pos_ref[...] = rank_ref[...] + base


def _positions(offs, idx3, rank3):
    nch = idx3.shape[0]
    cb = nch // 2
    spec = pl.BlockSpec((cb, TOP_K, 128), lambda i, offs: (i, 0, 0))
    return pl.pallas_call(
        _positions_kernel,
        out_shape=jax.ShapeDtypeStruct((nch, TOP_K, 128), jnp.int32),
        grid_spec=pltpu.PrefetchScalarGridSpec(num_scalar_prefetch=1, grid=(nch // cb,),
                                               in_specs=[spec, spec], out_specs=spec),
        compiler_params=_cparams(("parallel",)),
        name="positions",
    )(offs, idx3, rank3)


def _route_positions(idx3, rank3, counts, n_rows):
    tm = TM_X
    counts = counts.astype(jnp.int32)
    padded = ((counts + tm - 1) // tm) * tm
    ends = jnp.cumsum(padded)
    offs = ends - padded
    pos3 = _positions(offs.astype(jnp.int32), idx3, rank3)
    r_pad = n_rows * TOP_K + N_EXPERTS * tm
    nt = r_pad // tm
    tile_ids = jnp.arange(nt, dtype=jnp.int32)
    tile_expert = jnp.sum((ends // tm)[None, :] <= tile_ids[:, None], axis=1)
    tile_expert = jnp.minimum(tile_expert, N_EXPERTS - 1).astype(jnp.int32)
    n_used = (ends[-1] // tm).astype(jnp.int32).reshape(1)
    e_ids = jnp.arange(N_EXPERTS, dtype=jnp.int32)
    nonempty = counts > 0
    later = jnp.where(nonempty[None, :] & (e_ids[None, :] > e_ids[:, None]), e_ids[None, :], N_EXPERTS)
    next_e = jnp.min(later, axis=1)
    next_e = jnp.where(next_e == N_EXPERTS, -1, next_e).astype(jnp.int32)
    slot_e = ((jnp.cumsum(nonempty.astype(jnp.int32)) - nonempty.astype(jnp.int32)) % 2).astype(jnp.int32)
    tile_rows = jnp.clip(counts[tile_expert] - (tile_ids - (offs // tm)[tile_expert]) * tm, 0, tm)
    tile_rows = jnp.where(tile_ids < n_used[0], tile_rows, 0).astype(jnp.int32)
    sched = (tile_expert, next_e[tile_expert], slot_e[tile_expert], tile_rows, n_used)
    return pos3, sched, r_pad


def _in_weights(w):
    def regroup(cols):
        return cols.reshape(D, 2, HEADS, QK_DIM).transpose(0, 2, 1, 3).reshape(D, 2 * HEADS * QK_DIM)
    return jnp.concatenate([w[:, :OFF_Q], regroup(w[:, OFF_Q:OFF_K]), regroup(w[:, OFF_K:OFF_V]), w[:, OFF_V:]],
                           axis=1).astype(BF16)


def _rope_tables():
    t = jnp.arange(L)
    row = (t // GRID_W).astype(F32)
    col = (t % GRID_W).astype(F32)
    n_freq = QK_DIM // 4
    inv = ROPE_BASE ** (-jnp.arange(n_freq, dtype=F32) / n_freq)
    ar = row[:, None] * inv
    ac = col[:, None] * inv
    cos64 = jnp.concatenate([jnp.cos(ar), jnp.cos(ar), jnp.cos(ac), jnp.cos(ac)], axis=1)
    sin64 = jnp.concatenate([-jnp.sin(ar), jnp.sin(ar), -jnp.sin(ac), jnp.sin(ac)], axis=1)
    cos_t = jnp.concatenate([jnp.tile(cos64, (1, 2)), jnp.ones((TM_IN, 128), F32)], axis=0)
    sin_t = jnp.concatenate([jnp.tile(sin64, (1, 2)), jnp.zeros((TM_IN, 128), F32)], axis=0)
    return cos_t, sin_t


def _split_bf16(w):
    hi = w.astype(BF16)
    return hi, (w - hi.astype(F32)).astype(BF16)


def kernel(x, c, ctx, c_ctx, w_ada, b_ada, g_norm1, g_norm2, w_in, w_out, g_v, w_s, b_s, w_conv, g_q, g_k,
           lam_q1, lam_k1, lam_q2, lam_k2, g_sub, w_router, b_router, w_gate, w_up, w_down,
           ws_gate, ws_up, ws_down):
    src = (x.reshape(NL, D), ctx.reshape(NC, D))
    cc = jnp.concatenate([c, c_ctx[None, :], jnp.zeros((MOD_ROWS - B - 1, D), F32)], axis=0)
    mod = _ada(cc, w_ada, b_ada)
    cos_t, sin_t = _rope_tables()
    bd = jnp.asarray(np.kron(np.eye(8, dtype=np.float32), np.full((64, 64), 1.0 / 64, np.float32)), BF16)

    for l in range(DEPTH):
        last = l == DEPTH - 1
        lam_init = 0.8 - 0.6 * math.exp(-0.3 * l)
        lam = (jnp.exp(jnp.sum(lam_q1[l] * lam_k1[l])) - jnp.exp(jnp.sum(lam_q2[l] * lam_k2[l])) + lam_init)
        bound = (QK_DIM * jnp.max(jnp.abs(g_q[l])) * jnp.max(jnp.abs(g_k[l]))
                 * (QK_DIM ** -0.5 * LOG2E) * ATT_BOUND_MARGIN)
        use_bound = (2.0 * bound < ATT_MAX_SHIFT_RANGE).astype(F32)
        lam = jnp.stack([lam, bound, use_bound]).astype(F32)
        mod3 = mod[l].reshape(MOD_ROWS, 1, 6 * D)
        w_in_bf = _in_weights(w_in[l])
        bias_t = jnp.repeat(b_s[l].T, A_GD, axis=1)
        wconv = jnp.concatenate([w_conv[l], jnp.zeros((5, B_WIDTH), F32)], axis=0)
        yab, q, kk, v = _in_mix(src[0], src[1], g_norm1[l], mod3, w_in_bf, cos_t, sin_t,
                                     g_v[l].reshape(1, A_WIDTH), w_s[l].astype(BF16), bias_t, wconv,
                                     jnp.tile(g_q[l], 8).reshape(1, 512), jnp.tile(g_k[l], 8).reshape(1, 512), bd)
        gsub = g_sub[l].reshape(1, V_DIM)
        coef = 1.0 - lam_init
        n_rows = NL if last else NR
        yc = _attention(lam, q, kk, v, gsub, coef, ctx_queries=False)
        yc_ctx = yc if last else _attention(lam, q, kk, v, gsub, coef, ctx_queries=True)
        wr_hi, wr_lo = _split_bf16(w_router[l].T)
        wo_bf = w_out[l].astype(BF16)
        ws_bf = (ws_gate[l].astype(BF16), ws_up[l].astype(BF16), ws_down[l].astype(BF16))
        xa, h2p, idx3, gate3, rank3, counts = _out_router(
            src[0], src[1], yab, yc, yc_ctx, wo_bf, g_norm2[l], mod3, wr_hi, wr_lo, b_router[l], n_rows)
        pos3, sched, r_pad = _route_positions(idx3, rank3, counts[:, 0], n_rows)
        x_sorted = _sc_dispatch(h2p.reshape(n_rows, 4, 128), pos3, r_pad)
        y_sorted = _experts(l, sched, x_sorted.reshape(r_pad * 4, 128), w_gate, w_up, w_down)
        yg = _sc_collect(y_sorted.reshape(r_pad, 4, 128), pos3)
        gates = gate3.transpose(0, 2, 1).reshape(n_rows, TOP_K)
        xa = _shared_residual(xa, h2p, yg.reshape(TOP_K, n_rows * 4, 128), gates, *ws_bf, mod3, n_rows)
        src = (xa, xa)
    return xa.reshape(B, L, D)
```

```python
import functools
import math

import numpy as np
import jax
import jax.numpy as jnp
from jax import lax
from jax.experimental import pallas as pl
from jax.experimental.pallas import tpu as pltpu
from jax.experimental.pallas import tpu_sc as plsc

F32 = jnp.float32
BF16 = jnp.bfloat16

D = 1024
B = 8
L = 2048
DEPTH = 2
GRID_W = 64
CTX = 256
A_WIDTH = 256
A_GROUPS = 4
A_GD = 64
CHUNK = 128
B_WIDTH = 256
C_WIDTH = 512
HEADS = 4
V_DIM = 128
QK_DIM = 64
ROPE_BASE = 10000.0
OFF_BB = 512
OFF_BC = 768
OFF_BX = 1024
OFF_Q = 1280
OFF_K = 1792
OFF_V = 2304
D_IN = 2816
N_EXPERTS = 64
TOP_K = 8
D_EXPERT = 256
D_SHARED = 256
ROUTED_SCALE = 2.5
EPS = 1e-6

NL = B * L
NC = B * CTX
NR = NL + NC
MOD_ROWS = 16
LOG2E = 1.4426950408889634

TM_IN = 512
RMS_BLOCK = 256
TQ = 1024
ATT_CHAIN = 256
ATT_BOUND_MARGIN = 1.02
ATT_MAX_SHIFT_RANGE = 100.0
TM_OUT = 512
TM_X = 1024
X_CHAINS = 2
X_TAIL_ROWS = 512
X_TAIL = 256
X_RING = 3
TM_F = 512

_DN_T = (((1,), (1,)), ((), ()))


def _cparams(sem, vmem_mb=None):
    kw = dict(dimension_semantics=sem)
    if vmem_mb is not None:
        kw["vmem_limit_bytes"] = vmem_mb * 1024 * 1024
    return pltpu.CompilerParams(**kw)


def _mod_row(i, tm):
    return jnp.where(i < NL // tm, i // (L // tm), B)


def _mod_spec(tm, chunk):
    return pl.BlockSpec((None, 1, D), lambda i: (_mod_row(i, tm), 0, chunk))


def _ada_kernel(c_ref, w_ref, b_ref, o_ref):
    c = c_ref[...]
    cs = c * jax.nn.sigmoid(c)
    o_ref[...] = jnp.dot(cs, w_ref[...], preferred_element_type=F32,
                         precision=lax.Precision.HIGHEST) + b_ref[...]


def _ada(cc, w_ada, b_ada):
    nb = 6
    return pl.pallas_call(
        _ada_kernel,
        out_shape=jax.ShapeDtypeStruct((DEPTH, MOD_ROWS, 6 * D), F32),
        grid=(DEPTH, nb),
        in_specs=[pl.BlockSpec((MOD_ROWS, D), lambda l, j: (0, 0)),
                  pl.BlockSpec((None, D, D), lambda l, j: (l, 0, j)),
                  pl.BlockSpec((None, 1, D), lambda l, j: (l, 0, j))],
        out_specs=pl.BlockSpec((None, MOD_ROWS, D), lambda l, j: (l, 0, j)),
        compiler_params=_cparams(("arbitrary", "arbitrary"), 40),
        name="ada_mod",
    )(cc, w_ada, b_ada.reshape(DEPTH, 1, 6 * D))


def _rms_mod(x, g, sc, sh):
    ms = jnp.mean(x * x, axis=-1, keepdims=True)
    return x * lax.rsqrt(ms + EPS) * (g * (1.0 + sc)) + sh


def _two_source_specs(tm, n_first, width=D):
    return [pl.BlockSpec((tm, width), lambda i: (jnp.minimum(i, n_first - 1), 0)),
            pl.BlockSpec((tm, width), lambda i: (jnp.maximum(i - n_first, 0), 0))]


def _two_source_rows(a_ref, b_ref, n_first):
    return jnp.where(pl.program_id(0) < n_first, a_ref[...], b_ref[...])


def _group_rms(t, g, bd):
    sq = (t * t).astype(BF16)
    ms = jnp.concatenate([jnp.dot(sq[:, c:c + RMS_BLOCK], bd, preferred_element_type=F32)
                          for c in range(0, t.shape[1], RMS_BLOCK)], axis=1)
    return t * lax.rsqrt(ms + EPS) * g


def _rope(t, cos, sin):
    w = t.shape[1]
    lane = lax.broadcasted_iota(jnp.int32, t.shape, 1)
    first = (lane % 32) < 16
    partner = jnp.where(first, pltpu.roll(t, w - 16, 1), pltpu.roll(t, 16, 1))
    cos4 = jnp.concatenate([cos] * (w // 128), axis=1)
    sin4 = jnp.concatenate([sin] * (w // 128), axis=1)
    return t * cos4 + partner * sin4


def _in_mix_kernel(xa_ref, xb_ref, pa_ref, pb_ref, na_ref, nb_ref, g_ref, sh_ref, sc_ref, w_ref,
                   cos_ref, sin_ref, gv_ref, ws_ref, bias_ref, wconv_ref, gq_ref, gk_ref, bd_ref,
                   yab_ref, q_ref, kk_ref, v_ref, *, n_first):
    tm = TM_IN
    i = pl.program_id(0)
    tiles_per_seq = L // tm
    is_lat = i < NL // tm
    is_start = jnp.logical_or(jnp.logical_not(is_lat), i % tiles_per_seq == 0)
    is_end = jnp.logical_or(jnp.logical_not(is_lat), i % tiles_per_seq == tiles_per_seq - 1)
    first = i < n_first
    g, sc, sh = g_ref[...], sc_ref[...], sh_ref[...]

    h = _rms_mod(jnp.where(first, xa_ref[...], xb_ref[...]), g, sc, sh)
    p = jnp.dot(h.astype(BF16), w_ref[...], preferred_element_type=F32)
    v_ref[...] = p[:, OFF_V:].astype(BF16)
    halo = jnp.concatenate([jnp.where(first, pa_ref[...], pb_ref[...]),
                            jnp.where(first, na_ref[...], nb_ref[...])], axis=0)
    ph = jnp.dot(_rms_mod(halo, g, sc, sh).astype(BF16), w_ref[:, OFF_BC:OFF_Q], preferred_element_type=F32)
    zh = ph[:, :B_WIDTH] * ph[:, B_WIDTH:]
    zp = jnp.where(is_start, 0.0, zh[15:16])
    zn = jnp.where(is_end, 0.0, zh[16:17])

    uv = p[:, 0:2 * A_WIDTH]
    uv = 0.5 * uv * (1.0 + lax.erf(uv * (2.0 ** -0.5)))
    u = uv[:, :A_WIDTH]
    v = uv[:, A_WIDTH:]
    ms = jnp.mean(v * v, axis=-1, keepdims=True)
    vb = (v * lax.rsqrt(ms + EPS) * gv_ref[...]).astype(BF16)
    lane = lax.broadcasted_iota(jnp.int32, (CHUNK, 128), 1)
    mixes = []
    for c in range(tm // CHUNK):
        vc = vb[c * CHUNK:(c + 1) * CHUNK]
        halves = []
        for j in range(2):
            vj = vc[:, j * 128:(j + 1) * 128]
            m0 = jnp.dot(ws_ref[2 * j], vj, preferred_element_type=F32)
            m1 = jnp.dot(ws_ref[2 * j + 1], vj, preferred_element_type=F32)
            halves.append(jnp.where(lane < A_GD, m0, m1))
        mixes.append(jnp.concatenate(halves, axis=1) + bias_ref[...])
    ya = u * jnp.concatenate(mixes, axis=0)

    bg = p[:, OFF_BB:OFF_BC]
    z = p[:, OFF_BC:OFF_BX] * p[:, OFF_BX:OFF_Q]
    row = lax.broadcasted_iota(jnp.int32, z.shape, 0)
    inner = jnp.logical_not(is_lat)
    z_prev = jnp.where(row == 0, zp, pltpu.roll(z, 1, 0))
    z_prev = jnp.where(jnp.logical_and(inner, row % CTX == 0), 0.0, z_prev)
    z_next = jnp.where(row == tm - 1, zn, pltpu.roll(z, tm - 1, 0))
    z_next = jnp.where(jnp.logical_and(inner, row % CTX == CTX - 1), 0.0, z_next)
    yb = bg * (z_prev * wconv_ref[0:1] + z * wconv_ref[1:2] + z_next * wconv_ref[2:3])
    yab_ref[...] = jnp.concatenate([ya, yb], axis=1).astype(BF16)

    cos = cos_ref[...]
    sin = sin_ref[...]
    bd = bd_ref[...]
    q = _rope(_group_rms(p[:, OFF_Q:OFF_K], gq_ref[...], bd), cos, sin)
    q = q * (QK_DIM ** -0.5 * LOG2E)
    q_ref[...] = q.astype(BF16)
    k = _rope(_group_rms(p[:, OFF_K:OFF_V], gk_ref[...], bd), cos, sin)
    kk_ref[...] = k.astype(BF16)


def _in_mix(x_first, x_second, g, mod3, w_bf, cos_t, sin_t, gv, ws_bf, bias_t, wconv, gq, gk, bd):
    tm = TM_IN
    n_first = x_first.shape[0] // tm
    hb = tm // 16
    nhb_first = x_first.shape[0] // 16
    nhb_second = x_second.shape[0] // 16
    pos_blocks = L // tm

    def tab_map(i):
        return (jnp.where(i < NL // tm, i % pos_blocks, pos_blocks), 0)

    def halo_specs(shift):
        blk = lambda i: (i * tm + shift) // 16
        return [pl.BlockSpec((16, D), lambda i: (jnp.clip(blk(i), 0, nhb_first - 1), 0)),
                pl.BlockSpec((16, D), lambda i: (jnp.clip(blk(i) - nhb_first, 0, nhb_second - 1), 0))]

    const2 = lambda i: (0, 0)
    row512 = pl.BlockSpec((tm, 512), lambda i: (i, 0))
    return pl.pallas_call(
        functools.partial(_in_mix_kernel, n_first=n_first),
        out_shape=[jax.ShapeDtypeStruct((NR, 512), BF16)] * 4,
        grid=(NR // tm,),
        in_specs=_two_source_specs(tm, n_first) + halo_specs(-1) + halo_specs(tm)
                 + [pl.BlockSpec((1, D), const2), _mod_spec(tm, 0), _mod_spec(tm, 1),
                    pl.BlockSpec((D, D_IN), const2),
                    pl.BlockSpec((tm, 128), tab_map), pl.BlockSpec((tm, 128), tab_map),
                    pl.BlockSpec((1, A_WIDTH), const2),
                    pl.BlockSpec((A_GROUPS, CHUNK, CHUNK), lambda i: (0, 0, 0)),
                    pl.BlockSpec((CHUNK, A_WIDTH), const2),
                    pl.BlockSpec((8, B_WIDTH), const2),
                    pl.BlockSpec((1, 512), const2), pl.BlockSpec((1, 512), const2),
                    pl.BlockSpec((RMS_BLOCK, RMS_BLOCK), const2)],
        out_specs=[row512] * 4,
        compiler_params=_cparams(("parallel",), 56),
        name="in_mix",
    )(x_first, x_second, x_first, x_second, x_first, x_second, g.reshape(1, D), mod3, mod3, w_bf,
      cos_t, sin_t, gv, ws_bf, bias_t, wconv, gq, gk, bd)


def _attn_kernel(lam_ref, q_ref, *rest, n_seg, coef, tq):
    kv_refs = rest[:2 * n_seg]
    gsub_ref, o_ref, k_scr, vt_scr = rest[2 * n_seg:]

    @pl.when(pl.program_id(2) == 0)
    def _():
        off = 0
        for s in range(n_seg):
            n = kv_refs[s].shape[0]
            k_scr[off:off + n, :] = kv_refs[s][...]
            vt_scr[0:V_DIM, off:off + n] = kv_refs[n_seg + s][...].astype(F32).T.astype(BF16)
            off += n
        ones_row = lax.broadcasted_iota(jnp.int32, (16, off), 0) == 0
        vt_scr[V_DIM:, :] = jnp.where(ones_row, 1.0, 0.0).astype(BF16)

    lam = lam_ref[0]
    shift = lam_ref[1]
    qc = ATT_CHAIN

    def scores(c):
        rows = slice(c * qc, (c + 1) * qc)
        q = q_ref[rows, :]
        lane = lax.broadcasted_iota(jnp.int32, q.shape, 1)
        zero = jnp.zeros_like(q)
        qs = jnp.concatenate([jnp.where(lane < QK_DIM, q, zero), jnp.where(lane >= QK_DIM, q, zero)], axis=0)
        return lax.dot_general(k_scr[...], qs, _DN_T, preferred_element_type=F32)

    def finish(c, pt):
        ot = jnp.dot(vt_scr[...], pt, preferred_element_type=F32)
        inv = 1.0 / ot[V_DIM:V_DIM + 1, :]
        dt = ot[0:V_DIM, :qc] * inv[:, :qc] - ot[0:V_DIM, qc:] * (lam * inv[:, qc:])
        o = dt.T
        ms = jnp.mean(o * o, axis=-1, keepdims=True)
        o_ref[c * qc:(c + 1) * qc, :] = (o * lax.rsqrt(ms + EPS) * gsub_ref[...] * coef).astype(o_ref.dtype)

    @pl.when(lam_ref[2] > 0.5)
    def _():
        for c in range(tq // qc):
            finish(c, jnp.exp2(scores(c) - shift).astype(BF16))

    @pl.when(lam_ref[2] <= 0.5)
    def _():
        sts = [scores(c) for c in range(tq // qc)]
        for c in range(tq // qc):
            st = sts[c]
            finish(c, jnp.exp2(st - jnp.max(st, axis=0, keepdims=True)).astype(BF16))


def _attention(lam, q, kk, v, gsub, coef, *, ctx_queries):
    if ctx_queries:
        tq = CTX
        nq, lk, n_seg = 1, CTX, 1
        q_map = lambda b, h, qi: (NL // tq + b, h)
        kv_specs = [pl.BlockSpec((CTX, 128), lambda b, h, qi: (NL // CTX + b, h)),
                    pl.BlockSpec((CTX, 128), lambda b, h, qi: (NL // CTX + b, h))]
        kv_args = [kk, v]
        rows = NC
        o_map = lambda b, h, qi: (b, h)
    else:
        tq = TQ
        nq, lk, n_seg = L // tq, CTX + L, 2
        q_map = lambda b, h, qi: (b * (L // tq) + qi, h)
        kv_specs = [pl.BlockSpec((CTX, 128), lambda b, h, qi: (NL // CTX + b, h)),
                    pl.BlockSpec((L, 128), lambda b, h, qi: (b, h)),
                    pl.BlockSpec((CTX, 128), lambda b, h, qi: (NL // CTX + b, h)),
                    pl.BlockSpec((L, 128), lambda b, h, qi: (b, h))]
        kv_args = [kk, kk, v, v]
        rows = NL
        o_map = lambda b, h, qi: (b * (L // tq) + qi, h)
    return pl.pallas_call(
        functools.partial(_attn_kernel, n_seg=n_seg, coef=coef, tq=tq),
        out_shape=jax.ShapeDtypeStruct((rows, C_WIDTH), BF16),
        grid=(B, HEADS, nq),
        in_specs=[pl.BlockSpec(memory_space=pltpu.SMEM),
                  pl.BlockSpec((tq, 128), q_map)]
                 + kv_specs + [pl.BlockSpec((1, V_DIM), lambda b, h, qi: (0, 0))],
        out_specs=pl.BlockSpec((tq, 128), o_map),
        scratch_shapes=[pltpu.VMEM((lk, 128), BF16), pltpu.VMEM((V_DIM + 16, lk), BF16)],
        compiler_params=_cparams(("parallel", "parallel", "arbitrary"), 56),
        name="attn_ctx" if ctx_queries else "attn_lat",
    )(lam, q, *kv_args, gsub)


def _pack_rows(t, out_ref, row0=0):
    half = D // 2
    w = pltpu.pack_elementwise([t[:, :half], t[:, half:]], packed_dtype=BF16)
    w = lax.bitcast_convert_type(w, jnp.uint32)
    rows = t.shape[0]
    for j in range(4):
        out_ref[pl.ds(4 * row0 + j, rows, stride=4), :] = w[:, j * 128:(j + 1) * 128]


def _unpack_rows(ref, rows, lead=None, row0=0):
    los, his = [], []
    for j in range(4):
        sl = pl.ds(4 * row0 + j, rows, stride=4)
        w = ref[sl, :] if lead is None else ref[lead, sl, :]
        los.append(pltpu.unpack_elementwise(w, index=0, packed_dtype=BF16, unpacked_dtype=F32))
        his.append(pltpu.unpack_elementwise(w, index=1, packed_dtype=BF16, unpacked_dtype=F32))
    return jnp.concatenate(los, axis=1), jnp.concatenate(his, axis=1)


def _out_router_kernel(xa_ref, xb_ref, yab_ref, yca_ref, ycb_ref, wo_ref, g1_ref, g2n_ref, sh2_ref, sc2_ref, wrh_ref,
                       wrl_ref, br_ref, xo_ref, h2p_ref, idx_ref, gate_ref, rank_ref, cnt_ref, run_ref, *,
                       n_first, n_first_c):
    tm = TM_OUT

    @pl.when(pl.program_id(0) == 0)
    def _():
        run_ref[...] = jnp.zeros_like(run_ref)

    yc = _two_source_rows(yca_ref, ycb_ref, n_first_c)
    y = jnp.dot(jnp.concatenate([yab_ref[...], yc], axis=1), wo_ref[...], preferred_element_type=F32)
    x = _two_source_rows(xa_ref, xb_ref, n_first) + g1_ref[...] * y
    xo_ref[...] = x
    h2 = _rms_mod(x, g2n_ref[...], sc2_ref[...], sh2_ref[...])
    hi = h2.astype(BF16)
    _pack_rows(h2, h2p_ref)
    lo = (h2 - hi.astype(F32)).astype(BF16)
    wh = wrh_ref[...]
    z = (lax.dot_general(wh, hi, _DN_T, preferred_element_type=F32)
         + lax.dot_general(wh, lo, _DN_T, preferred_element_type=F32)
         + lax.dot_general(wrl_ref[...], hi, _DN_T, preferred_element_type=F32))
    scores = jax.nn.sigmoid(z)
    work = scores + br_ref[...]
    eio = lax.broadcasted_iota(jnp.int32, work.shape, 0)
    idxs, sels, hits = [], [], []
    for _ in range(TOP_K):
        m = jnp.max(work, axis=0, keepdims=True)
        idx = jnp.min(jnp.where(work == m, eio, N_EXPERTS), axis=0, keepdims=True)
        hit = eio == idx
        sels.append(jnp.sum(jnp.where(hit, scores, 0.0), axis=0, keepdims=True))
        idxs.append(idx)
        hits.append(hit)
        work = jnp.where(hit, -jnp.inf, work)
    sel = jnp.concatenate(sels, axis=0)
    idx8 = jnp.concatenate(idxs, axis=0)
    gate8 = sel / jnp.sum(sel, axis=0, keepdims=True) * ROUTED_SCALE

    chosen = functools.reduce(jnp.logical_or, hits)
    before = (lax.broadcasted_iota(jnp.int32, (tm, tm), 0) < lax.broadcasted_iota(jnp.int32, (tm, tm), 1))
    prefix = jnp.dot(jnp.where(chosen, 1.0, 0.0).astype(BF16), jnp.where(before, 1.0, 0.0).astype(BF16),
                     preferred_element_type=F32)
    rank_dense = prefix + run_ref[:, 0:1]
    rank8 = jnp.concatenate([jnp.sum(jnp.where(h, rank_dense, 0.0), axis=0, keepdims=True) for h in hits],
                            axis=0).astype(jnp.int32)
    run = run_ref[...] + jnp.sum(jnp.where(chosen, 1.0, 0.0), axis=1, keepdims=True)
    run_ref[...] = run
    cnt_ref[...] = run
    for c in range(tm // 128):
        idx_ref[c] = idx8[:, c * 128:(c + 1) * 128]
        gate_ref[c] = gate8[:, c * 128:(c + 1) * 128]
        rank_ref[c] = rank8[:, c * 128:(c + 1) * 128]


def _out_router(x_first, x_second, yab, yc_first, yc_second, wo_bf, g2n, mod3, wr_hi, wr_lo, br, n_rows):
    tm = TM_OUT
    n_first = x_first.shape[0] // tm
    n_first_c = yc_first.shape[0] // tm
    const2 = lambda i: (0, 0)
    row = lambda i: (i, 0)
    chunk3 = pl.BlockSpec((tm // 128, TOP_K, 128), lambda i: (i, 0, 0))
    nch = n_rows // 128
    return pl.pallas_call(
        functools.partial(_out_router_kernel, n_first=n_first, n_first_c=n_first_c),
        out_shape=[jax.ShapeDtypeStruct((n_rows, D), F32),
                   jax.ShapeDtypeStruct((n_rows * 4, 128), jnp.uint32),
                   jax.ShapeDtypeStruct((nch, TOP_K, 128), jnp.int32),
                   jax.ShapeDtypeStruct((nch, TOP_K, 128), F32),
                   jax.ShapeDtypeStruct((nch, TOP_K, 128), jnp.int32),
                   jax.ShapeDtypeStruct((N_EXPERTS, 128), F32)],
        grid=(n_rows // tm,),
        in_specs=_two_source_specs(tm, n_first)
                 + [pl.BlockSpec((tm, 512), row)]
                 + _two_source_specs(tm, n_first_c, C_WIDTH)
                 + [pl.BlockSpec((D, D), const2),
                  _mod_spec(tm, 2),
                  pl.BlockSpec((1, D), const2), _mod_spec(tm, 3), _mod_spec(tm, 4),
                  pl.BlockSpec((N_EXPERTS, D), const2), pl.BlockSpec((N_EXPERTS, D), const2),
                  pl.BlockSpec((N_EXPERTS, 1), const2)],
        out_specs=[pl.BlockSpec((tm, D), row), pl.BlockSpec((tm * 4, 128), row),
                   chunk3, chunk3, chunk3, pl.BlockSpec((N_EXPERTS, 128), const2)],
        scratch_shapes=[pltpu.VMEM((N_EXPERTS, 128), F32)],
        compiler_params=_cparams(("arbitrary",), 48),
        name="out_router",
    )(x_first, x_second, yab, yc_first, yc_second, wo_bf, mod3, g2n.reshape(1, D), mod3, mod3, wr_hi, wr_lo,
      br.reshape(N_EXPERTS, 1))


def _experts_kernel(te_ref, nx_ref, ws_ref, tr_ref, nu_ref, x_hbm, wg_hbm, wu_hbm, wd_hbm, y_ref,
                    wgu_s, wd_s, x_ring, x_sems, wg_f, wu_f, wd_f, w_sems, *, layer):
    i = pl.program_id(0)
    n_used = nu_ref[0]
    used = i < n_used
    new_expert = jnp.logical_or(i == 0, te_ref[i] != te_ref[jnp.maximum(i - 1, 0)])

    def w_copies(e, slot):
        return [pltpu.make_async_copy(wg_hbm.at[layer, e], wg_f.at[slot], w_sems.at[slot, 0]),
                pltpu.make_async_copy(wu_hbm.at[layer, e], wu_f.at[slot], w_sems.at[slot, 1]),
                pltpu.make_async_copy(wd_hbm.at[layer, e], wd_f.at[slot], w_sems.at[slot, 2])]

    @pl.when(i == 0)
    def _():
        for cp in w_copies(te_ref[0], ws_ref[0]):
            cp.start()

    ahead = X_RING - 1

    def x_copy(t):
        slot = t % X_RING
        return pltpu.make_async_copy(x_hbm.at[pl.ds(t * (4 * TM_X), 4 * TM_X)], x_ring.at[slot], x_sems.at[slot])

    @pl.when(i == 0)
    def _():
        for t in range(ahead):
            @pl.when(t < n_used)
            def _():
                x_copy(t).start()

    @pl.when(i + ahead < n_used)
    def _():
        x_copy(i + ahead).start()

    @pl.when(jnp.logical_and(used, new_expert))
    def _():
        slot = ws_ref[i]
        for cp in w_copies(te_ref[i], slot):
            cp.wait()

        @pl.when(nx_ref[i] >= 0)
        def _():
            for cp in w_copies(nx_ref[i], 1 - slot):
                cp.start()

        wgu_s[:, 0:D_EXPERT] = wg_f[slot].astype(BF16)
        wgu_s[:, D_EXPERT:] = wu_f[slot].astype(BF16)
        wd_s[...] = wd_f[slot].astype(BF16)

    def first_dot(x_ref, row0, rows):
        x_lo, x_hi = _unpack_rows(x_ref, rows, row0=row0)
        x = jnp.concatenate([x_lo.astype(BF16), x_hi.astype(BF16)], axis=1)
        return jnp.dot(x, wgu_s[...], preferred_element_type=F32)

    def second_dot(ab, row0):
        a = ab[:, :D_EXPERT]
        hid = (a * jax.nn.sigmoid(a) * ab[:, D_EXPERT:]).astype(BF16)
        _pack_rows(jnp.dot(hid, wd_s[...], preferred_element_type=F32), y_ref, row0=row0)

    @pl.when(used)
    def _():
        x_copy(i).wait()

    valid = tr_ref[i]

    @pl.when(jnp.logical_and(used, valid > X_TAIL_ROWS))
    def _():
        x_ref = x_ring.at[i % X_RING]
        rc = TM_X // X_CHAINS
        abs_ = [first_dot(x_ref, c * rc, rc) for c in range(X_CHAINS)]
        for c in range(X_CHAINS):
            second_dot(abs_[c], c * rc)

    for c in range(X_TAIL_ROWS // X_TAIL):
        @pl.when(jnp.logical_and(used, jnp.logical_and(valid <= X_TAIL_ROWS, c * X_TAIL < valid)))
        def _():
            second_dot(first_dot(x_ring.at[i % X_RING], c * X_TAIL, X_TAIL), c * X_TAIL)


def _experts(layer, sched, x_sorted, w_gate, w_up, w_down):
    tm = TM_X
    r_pad = x_sorted.shape[0] // 4
    nt = r_pad // tm
    row = lambda i, te, nx, ws, tr, nu: (jnp.minimum(i, nu[0] - 1), 0)
    hbm = pl.BlockSpec(memory_space=pl.ANY)
    grid_spec = pltpu.PrefetchScalarGridSpec(
        num_scalar_prefetch=5,
        grid=(nt,),
        in_specs=[hbm, hbm, hbm, hbm],
        out_specs=pl.BlockSpec((tm * 4, 128), row),
        scratch_shapes=[pltpu.VMEM((D, 2 * D_EXPERT), BF16), pltpu.VMEM((D_EXPERT, D), BF16),
                        pltpu.VMEM((X_RING, tm * 4, 128), jnp.uint32), pltpu.SemaphoreType.DMA((X_RING,)),
                        pltpu.VMEM((2, D, D_EXPERT), F32), pltpu.VMEM((2, D, D_EXPERT), F32),
                        pltpu.VMEM((2, D_EXPERT, D), F32), pltpu.SemaphoreType.DMA((2, 3))],
    )
    return pl.pallas_call(
        functools.partial(_experts_kernel, layer=layer),
        out_shape=jax.ShapeDtypeStruct((r_pad * 4, 128), jnp.uint32),
        grid_spec=grid_spec,
        compiler_params=_cparams(("arbitrary",), 48),
        name="experts",
    )(*sched, x_sorted, w_gate, w_up, w_down)


SC_CORES = 2
SC_SUBCORES = 16
SC_WORKERS = SC_CORES * SC_SUBCORES
SC_CHUNK = 128


def _sc_mesh():
    return plsc.VectorSubcoreMesh(core_axis_name="c", subcore_axis_name="s")


def _sc_params():
    return pltpu.CompilerParams(use_tc_tiling_on_sc=True)


def _sc_dispatch(h2p, pos3, r_pad):
    nch = pos3.shape[0]
    steps = -(-nch // SC_WORKERS)

    def body(h_hbm, pos_hbm, out_hbm, idx_v, rows_v, sem):
        wid = lax.axis_index("s") * SC_CORES + lax.axis_index("c")

        @pl.loop(0, steps)
        def _(s):
            ch = wid + s * SC_WORKERS

            @pl.when(ch < nch)
            def _():
                pltpu.sync_copy(pos_hbm.at[ch], idx_v)
                pltpu.sync_copy(h_hbm.at[pl.ds(ch * SC_CHUNK, SC_CHUNK)], rows_v)
                copies = [pltpu.async_copy(rows_v, out_hbm.at[idx_v.at[k]], sem) for k in range(TOP_K)]
                for cp in copies:
                    cp.wait()

    return pl.kernel(
        body,
        out_type=jax.ShapeDtypeStruct((r_pad, 4, 128), jnp.uint32),
        mesh=_sc_mesh(),
        scratch_types=[pltpu.VMEM((TOP_K, SC_CHUNK), jnp.int32),
                       pltpu.VMEM((SC_CHUNK, 4, 128), jnp.uint32),
                       pltpu.SemaphoreType.DMA],
        compiler_params=_sc_params(),
        name="sc_dispatch",
    )(h2p, pos3)


def _sc_collect(y_sorted, pos3):
    nch = pos3.shape[0]
    steps = -(-nch // SC_WORKERS)
    half = SC_CHUNK // 2
    units = [(k, hh) for k in range(TOP_K) for hh in range(2)]

    def body(y_hbm, pos_hbm, out_hbm, idx_v, rows_a, rows_b, sem_a, sem_b):
        wid = lax.axis_index("s") * SC_CORES + lax.axis_index("c")
        bufs = (rows_a, rows_b)
        sems = (sem_a, sem_b)

        def gather(u):
            k, hh = units[u]
            return pltpu.async_copy(y_hbm.at[idx_v.at[k, pl.ds(hh * half, half)]], bufs[u % 2], sems[u % 2])

        @pl.loop(0, steps)
        def _(s):
            ch = wid + s * SC_WORKERS

            @pl.when(ch < nch)
            def _():
                pltpu.sync_copy(pos_hbm.at[ch], idx_v)
                pending = gather(0)
                for u, (k, hh) in enumerate(units):
                    nxt = gather(u + 1) if u + 1 < len(units) else None
                    pending.wait()
                    pltpu.sync_copy(bufs[u % 2], out_hbm.at[k, pl.ds(ch * SC_CHUNK + hh * half, half)])
                    pending = nxt

    return pl.kernel(
        body,
        out_type=jax.ShapeDtypeStruct((TOP_K, nch * SC_CHUNK, 4, 128), jnp.uint32),
        mesh=_sc_mesh(),
        scratch_types=[pltpu.VMEM((TOP_K, SC_CHUNK), jnp.int32),
                       pltpu.VMEM((half, 4, 128), jnp.uint32),
                       pltpu.VMEM((half, 4, 128), jnp.uint32),
                       pltpu.SemaphoreType.DMA, pltpu.SemaphoreType.DMA],
        compiler_params=_sc_params(),
        name="sc_collect",
    )(y_sorted, pos3)


def _shared_kernel(x_ref, h2p_ref, yg_ref, gate_ref, wg_ref, wu_ref, wd_ref, g2_ref, o_ref):
    h_lo, h_hi = _unpack_rows(h2p_ref, TM_F)
    h = jnp.concatenate([h_lo.astype(BF16), h_hi.astype(BF16)], axis=1)
    a = jnp.dot(h, wg_ref[...], preferred_element_type=F32)
    b = jnp.dot(h, wu_ref[...], preferred_element_type=F32)
    hid = (a * jax.nn.sigmoid(a) * b).astype(BF16)
    f = jnp.dot(hid, wd_ref[...], preferred_element_type=F32)
    gate = gate_ref[...]
    f_lo = f[:, :D // 2]
    f_hi = f[:, D // 2:]
    for k in range(TOP_K):
        y_lo, y_hi = _unpack_rows(yg_ref, TM_F, lead=k)
        f_lo = f_lo + gate[:, k:k + 1] * y_lo
        f_hi = f_hi + gate[:, k:k + 1] * y_hi
    o_ref[...] = x_ref[...] + g2_ref[...] * jnp.concatenate([f_lo, f_hi], axis=1)


def _shared_residual(xa, h2p, yg, gates, wsg_bf, wsu_bf, wsd_bf, mod3, n_rows):
    tm = TM_F
    row = lambda i: (i, 0)
    const2 = lambda i: (0, 0)
    return pl.pallas_call(
        _shared_kernel,
        out_shape=jax.ShapeDtypeStruct((n_rows, D), F32),
        grid=(n_rows // tm,),
        in_specs=[pl.BlockSpec((tm, D), row), pl.BlockSpec((tm * 4, 128), row),
                  pl.BlockSpec((TOP_K, tm * 4, 128), lambda i: (0, i, 0)),
                  pl.BlockSpec((tm, TOP_K), row),
                  pl.BlockSpec((D, D_SHARED), const2), pl.BlockSpec((D, D_SHARED), const2),
                  pl.BlockSpec((D_SHARED, D), const2), _mod_spec(tm, 5)],
        out_specs=pl.BlockSpec((tm, D), row),
        compiler_params=_cparams(("parallel",), 48),
        name="shared_residual",
    )(xa, h2p, yg, gates, wsg_bf, wsu_bf, wsd_bf, mod3)


def _positions_kernel(offs_ref, idx_ref, rank_ref, pos_ref):
    idx = idx_ref[...]
    base = jnp.zeros_like(idx)
    for e in range(N_EXPERTS):
        base = jnp.where(idx == e, offs_ref[e], base)
    pos_ref[...] = rank_ref[...] + base


def _positions(offs, idx3, rank3):
    nch = idx3.shape[0]
    cb = nch // 2
    spec = pl.BlockSpec((cb, TOP_K, 128), lambda i, offs: (i, 0, 0))
    return pl.pallas_call(
        _positions_kernel,
        out_shape=jax.ShapeDtypeStruct((nch, TOP_K, 128), jnp.int32),
        grid_spec=pltpu.PrefetchScalarGridSpec(num_scalar_prefetch=1, grid=(nch // cb,),
                                               in_specs=[spec, spec], out_specs=spec),
        compiler_params=_cparams(("parallel",)),
        name="positions",
    )(offs, idx3, rank3)


def _route_positions(idx3, rank3, counts, n_rows):
    tm = TM_X
    counts = counts.astype(jnp.int32)
    padded = ((counts + tm - 1) // tm) * tm
    ends = jnp.cumsum(padded)
    offs = ends - padded
    pos3 = _positions(offs.astype(jnp.int32), idx3, rank3)
    r_pad = n_rows * TOP_K + N_EXPERTS * tm
    nt = r_pad // tm
    tile_ids = jnp.arange(nt, dtype=jnp.int32)
    tile_expert = jnp.sum((ends // tm)[None, :] <= tile_ids[:, None], axis=1)
    tile_expert = jnp.minimum(tile_expert, N_EXPERTS - 1).astype(jnp.int32)
    n_used = (ends[-1] // tm).astype(jnp.int32).reshape(1)
    e_ids = jnp.arange(N_EXPERTS, dtype=jnp.int32)
    nonempty = counts > 0
    later = jnp.where(nonempty[None, :] & (e_ids[None, :] > e_ids[:, None]), e_ids[None, :], N_EXPERTS)
    next_e = jnp.min(later, axis=1)
    next_e = jnp.where(next_e == N_EXPERTS, -1, next_e).astype(jnp.int32)
    slot_e = ((jnp.cumsum(nonempty.astype(jnp.int32)) - nonempty.astype(jnp.int32)) % 2).astype(jnp.int32)
    tile_rows = jnp.clip(counts[tile_expert] - (tile_ids - (offs // tm)[tile_expert]) * tm, 0, tm)
    tile_rows = jnp.where(tile_ids < n_used[0], tile_rows, 0).astype(jnp.int32)
    sched = (tile_expert, next_e[tile_expert], slot_e[tile_expert], tile_rows, n_used)
    return pos3, sched, r_pad


def _in_weights(w):
    def regroup(cols):
        return cols.reshape(D, 2, HEADS, QK_DIM).transpose(0, 2, 1, 3).reshape(D, 2 * HEADS * QK_DIM)
    return jnp.concatenate([w[:, :OFF_Q], regroup(w[:, OFF_Q:OFF_K]), regroup(w[:, OFF_K:OFF_V]), w[:, OFF_V:]],
                           axis=1).astype(BF16)


def _rope_tables():
    t = jnp.arange(L)
    row = (t // GRID_W).astype(F32)
    col = (t % GRID_W).astype(F32)
    n_freq = QK_DIM // 4
    inv = ROPE_BASE ** (-jnp.arange(n_freq, dtype=F32) / n_freq)
    ar = row[:, None] * inv
    ac = col[:, None] * inv
    cos64 = jnp.concatenate([jnp.cos(ar), jnp.cos(ar), jnp.cos(ac), jnp.cos(ac)], axis=1)
    sin64 = jnp.concatenate([-jnp.sin(ar), jnp.sin(ar), -jnp.sin(ac), jnp.sin(ac)], axis=1)
    cos_t = jnp.concatenate([jnp.tile(cos64, (1, 2)), jnp.ones((TM_IN, 128), F32)], axis=0)
    sin_t = jnp.concatenate([jnp.tile(sin64, (1, 2)), jnp.zeros((TM_IN, 128), F32)], axis=0)
    return cos_t, sin_t


def _split_bf16(w):
    hi = w.astype(BF16)
    return hi, (w - hi.astype(F32)).astype(BF16)


def kernel(x, c, ctx, c_ctx, w_ada, b_ada, g_norm1, g_norm2, w_in, w_out, g_v, w_s, b_s, w_conv, g_q, g_k,
           lam_q1, lam_k1, lam_q2, lam_k2, g_sub, w_router, b_router, w_gate, w_up, w_down,
           ws_gate, ws_up, ws_down):
    src = (x.reshape(NL, D), ctx.reshape(NC, D))
    cc = jnp.concatenate([c, c_ctx[None, :], jnp.zeros((MOD_ROWS - B - 1, D), F32)], axis=0)
    mod = _ada(cc, w_ada, b_ada)
    cos_t, sin_t = _rope_tables()
    bd = jnp.asarray(np.kron(np.eye(RMS_BLOCK // QK_DIM, dtype=np.float32),
                             np.full((QK_DIM, QK_DIM), 1.0 / QK_DIM, np.float32)), BF16)

    for l in range(DEPTH):
        last = l == DEPTH - 1
        lam_init = 0.8 - 0.6 * math.exp(-0.3 * l)
        lam = (jnp.exp(jnp.sum(lam_q1[l] * lam_k1[l])) - jnp.exp(jnp.sum(lam_q2[l] * lam_k2[l])) + lam_init)
        bound = (QK_DIM * jnp.max(jnp.abs(g_q[l])) * jnp.max(jnp.abs(g_k[l]))
                 * (QK_DIM ** -0.5 * LOG2E) * ATT_BOUND_MARGIN)
        use_bound = (2.0 * bound < ATT_MAX_SHIFT_RANGE).astype(F32)
        lam = jnp.stack([lam, bound, use_bound]).astype(F32)
        mod3 = mod[l].reshape(MOD_ROWS, 1, 6 * D)
        w_in_bf = _in_weights(w_in[l])
        bias_t = jnp.repeat(b_s[l].T, A_GD, axis=1)
        wconv = jnp.concatenate([w_conv[l], jnp.zeros((5, B_WIDTH), F32)], axis=0)
        yab, q, kk, v = _in_mix(src[0], src[1], g_norm1[l], mod3, w_in_bf, cos_t, sin_t,
                                     g_v[l].reshape(1, A_WIDTH), w_s[l].astype(BF16), bias_t, wconv,
                                     jnp.tile(g_q[l], 8).reshape(1, 512), jnp.tile(g_k[l], 8).reshape(1, 512), bd)
        gsub = g_sub[l].reshape(1, V_DIM)
        coef = 1.0 - lam_init
        n_rows = NL if last else NR
        yc = _attention(lam, q, kk, v, gsub, coef, ctx_queries=False)
        yc_ctx = yc if last else _attention(lam, q, kk, v, gsub, coef, ctx_queries=True)
        wr_hi, wr_lo = _split_bf16(w_router[l].T)
        wo_bf = w_out[l].astype(BF16)
        ws_bf = (ws_gate[l].astype(BF16), ws_up[l].astype(BF16), ws_down[l].astype(BF16))
        xa, h2p, idx3, gate3, rank3, counts = _out_router(
            src[0], src[1], yab, yc, yc_ctx, wo_bf, g_norm2[l], mod3, wr_hi, wr_lo, b_router[l], n_rows)
        pos3, sched, r_pad = _route_positions(idx3, rank3, counts[:, 0], n_rows)
        x_sorted = _sc_dispatch(h2p.reshape(n_rows, 4, 128), pos3, r_pad)
        y_sorted = _experts(l, sched, x_sorted.reshape(r_pad * 4, 128), w_gate, w_up, w_down)
        yg = _sc_collect(y_sorted.reshape(r_pad, 4, 128), pos3)
        gates = gate3.transpose(0, 2, 1).reshape(n_rows, TOP_K)
        xa = _shared_residual(xa, h2p, yg.reshape(TOP_K, n_rows * 4, 128), gates, *ws_bf, mod3, n_rows)
        src = (xa, xa)
    return xa.reshape(B, L, D)
```

```python
import functools
import math

import numpy as np
import jax
import jax.numpy as jnp
from jax import lax
from jax.experimental import pallas as pl
from jax.experimental.pallas import tpu as pltpu
from jax.experimental.pallas import tpu_sc as plsc

F32 = jnp.float32
BF16 = jnp.bfloat16

D = 1024
B = 8
L = 2048
DEPTH = 2
GRID_W = 64
CTX = 256
A_WIDTH = 256
A_GROUPS = 4
A_GD = 64
CHUNK = 128
B_WIDTH = 256
C_WIDTH = 512
HEADS = 4
V_DIM = 128
QK_DIM = 64
ROPE_BASE = 10000.0
OFF_BB = 512
OFF_BC = 768
OFF_BX = 1024
OFF_Q = 1280
OFF_K = 1792
OFF_V = 2304
D_IN = 2816
N_EXPERTS = 64
TOP_K = 8
D_EXPERT = 256
D_SHARED = 256
ROUTED_SCALE = 2.5
EPS = 1e-6

NL = B * L
NC = B * CTX
NR = NL + NC
MOD_ROWS = 16
LOG2E = 1.4426950408889634

TM_IN = 512
RMS_BLOCK = 256
TQ = 1024
ATT_CHAIN = 256
ATT_BOUND_MARGIN = 1.02
ATT_MAX_SHIFT_RANGE = 100.0
TM_OUT = 512
TM_X = 1024
X_CHAINS = 2
X_TAIL_ROWS = 512
X_TAIL = 256
X_RING = 3
TM_F = 512

_DN_T = (((1,), (1,)), ((), ()))


def _cparams(sem, vmem_mb=None):
    kw = dict(dimension_semantics=sem)
    if vmem_mb is not None:
        kw["vmem_limit_bytes"] = vmem_mb * 1024 * 1024
    return pltpu.CompilerParams(**kw)


def _mod_row(i, tm):
    return jnp.where(i < NL // tm, i // (L // tm), B)


def _mod_spec(tm, chunk):
    return pl.BlockSpec((None, 1, D), lambda i: (_mod_row(i, tm), 0, chunk))


def _ada_kernel(c_ref, w_ref, b_ref, o_ref):
    c = c_ref[...]
    cs = c * jax.nn.sigmoid(c)
    o_ref[...] = jnp.dot(cs, w_ref[...], preferred_element_type=F32,
                         precision=lax.Precision.HIGHEST) + b_ref[...]


def _ada(cc, w_ada, b_ada):
    nb = 6
    return pl.pallas_call(
        _ada_kernel,
        out_shape=jax.ShapeDtypeStruct((DEPTH, MOD_ROWS, 6 * D), F32),
        grid=(DEPTH, nb),
        in_specs=[pl.BlockSpec((MOD_ROWS, D), lambda l, j: (0, 0)),
                  pl.BlockSpec((None, D, D), lambda l, j: (l, 0, j)),
                  pl.BlockSpec((None, 1, D), lambda l, j: (l, 0, j))],
        out_specs=pl.BlockSpec((None, MOD_ROWS, D), lambda l, j: (l, 0, j)),
        compiler_params=_cparams(("arbitrary", "arbitrary"), 40),
        name="ada_mod",
    )(cc, w_ada, b_ada.reshape(DEPTH, 1, 6 * D))


def _rms_mod(x, g, sc, sh):
    ms = jnp.mean(x * x, axis=-1, keepdims=True)
    return x * lax.rsqrt(ms + EPS) * (g * (1.0 + sc)) + sh


def _two_source_specs(tm, n_first, width=D):
    return [pl.BlockSpec((tm, width), lambda i: (jnp.minimum(i, n_first - 1), 0)),
            pl.BlockSpec((tm, width), lambda i: (jnp.maximum(i - n_first, 0), 0))]


def _two_source_rows(a_ref, b_ref, n_first):
    return jnp.where(pl.program_id(0) < n_first, a_ref[...], b_ref[...])


def _group_rms(t, g, bd):
    sq = (t * t).astype(BF16)
    ms = jnp.concatenate([jnp.dot(sq[:, c:c + RMS_BLOCK], bd, preferred_element_type=F32)
                          for c in range(0, t.shape[1], RMS_BLOCK)], axis=1)
    return t * lax.rsqrt(ms + EPS) * g


def _rope(t, cos, sin):
    w = t.shape[1]
    lane = lax.broadcasted_iota(jnp.int32, t.shape, 1)
    first = (lane % 32) < 16
    partner = jnp.where(first, pltpu.roll(t, w - 16, 1), pltpu.roll(t, 16, 1))
    cos4 = jnp.concatenate([cos] * (w // 128), axis=1)
    sin4 = jnp.concatenate([sin] * (w // 128), axis=1)
    return t * cos4 + partner * sin4


def _in_mix_kernel(xa_ref, xb_ref, pa_ref, pb_ref, na_ref, nb_ref, g_ref, sh_ref, sc_ref, w_ref,
                   cos_ref, sin_ref, gv_ref, ws_ref, bias_ref, wconv_ref, gq_ref, gk_ref, bd_ref,
                   yab_ref, q_ref, kk_ref, v_ref, *, n_first):
    tm = TM_IN
    i = pl.program_id(0)
    tiles_per_seq = L // tm
    is_lat = i < NL // tm
    is_start = jnp.logical_or(jnp.logical_not(is_lat), i % tiles_per_seq == 0)
    is_end = jnp.logical_or(jnp.logical_not(is_lat), i % tiles_per_seq == tiles_per_seq - 1)
    first = i < n_first
    g, sc, sh = g_ref[...], sc_ref[...], sh_ref[...]

    h = _rms_mod(jnp.where(first, xa_ref[...], xb_ref[...]), g, sc, sh)
    p = jnp.dot(h.astype(BF16), w_ref[...], preferred_element_type=F32)
    v_ref[...] = p[:, OFF_V:].astype(BF16)
    halo = jnp.concatenate([jnp.where(first, pa_ref[...], pb_ref[...]),
                            jnp.where(first, na_ref[...], nb_ref[...])], axis=0)
    ph = jnp.dot(_rms_mod(halo, g, sc, sh).astype(BF16), w_ref[:, OFF_BC:OFF_Q], preferred_element_type=F32)
    zh = ph[:, :B_WIDTH] * ph[:, B_WIDTH:]
    zp = jnp.where(is_start, 0.0, zh[15:16])
    zn = jnp.where(is_end, 0.0, zh[16:17])

    uv = p[:, 0:2 * A_WIDTH]
    uv = 0.5 * uv * (1.0 + lax.erf(uv * (2.0 ** -0.5)))
    u = uv[:, :A_WIDTH]
    v = uv[:, A_WIDTH:]
    ms = jnp.mean(v * v, axis=-1, keepdims=True)
    vb = (v * lax.rsqrt(ms + EPS) * gv_ref[...]).astype(BF16)
    lane = lax.broadcasted_iota(jnp.int32, (CHUNK, 128), 1)
    mixes = []
    for c in range(tm // CHUNK):
        vc = vb[c * CHUNK:(c + 1) * CHUNK]
        halves = []
        for j in range(2):
            vj = vc[:, j * 128:(j + 1) * 128]
            m0 = jnp.dot(ws_ref[2 * j], vj, preferred_element_type=F32)
            m1 = jnp.dot(ws_ref[2 * j + 1], vj, preferred_element_type=F32)
            halves.append(jnp.where(lane < A_GD, m0, m1))
        mixes.append(jnp.concatenate(halves, axis=1) + bias_ref[...])
    ya = u * jnp.concatenate(mixes, axis=0)

    bg = p[:, OFF_BB:OFF_BC]
    z = p[:, OFF_BC:OFF_BX] * p[:, OFF_BX:OFF_Q]
    row = lax.broadcasted_iota(jnp.int32, z.shape, 0)
    inner = jnp.logical_not(is_lat)
    z_prev = jnp.where(row == 0, zp, pltpu.roll(z, 1, 0))
    z_prev = jnp.where(jnp.logical_and(inner, row % CTX == 0), 0.0, z_prev)
    z_next = jnp.where(row == tm - 1, zn, pltpu.roll(z, tm - 1, 0))
    z_next = jnp.where(jnp.logical_and(inner, row % CTX == CTX - 1), 0.0, z_next)
    yb = bg * (z_prev * wconv_ref[0:1] + z * wconv_ref[1:2] + z_next * wconv_ref[2:3])
    yab_ref[...] = jnp.concatenate([ya, yb], axis=1).astype(BF16)

    cos = cos_ref[...]
    sin = sin_ref[...]
    bd = bd_ref[...]
    q = _rope(_group_rms(p[:, OFF_Q:OFF_K], gq_ref[...], bd), cos, sin)
    q = q * (QK_DIM ** -0.5 * LOG2E)
    q_ref[...] = q.astype(BF16)
    k = _rope(_group_rms(p[:, OFF_K:OFF_V], gk_ref[...], bd), cos, sin)
    kk_ref[...] = k.astype(BF16)


def _in_mix(x_first, x_second, g, mod3, w_bf, cos_t, sin_t, gv, ws_bf, bias_t, wconv, gq, gk, bd):
    tm = TM_IN
    n_first = x_first.shape[0] // tm
    hb = tm // 16
    nhb_first = x_first.shape[0] // 16
    nhb_second = x_second.shape[0] // 16
    pos_blocks = L // tm

    def tab_map(i):
        return (jnp.where(i < NL // tm, i % pos_blocks, pos_blocks), 0)

    def halo_specs(shift):
        blk = lambda i: (i * tm + shift) // 16
        return [pl.BlockSpec((16, D), lambda i: (jnp.clip(blk(i), 0, nhb_first - 1), 0)),
                pl.BlockSpec((16, D), lambda i: (jnp.clip(blk(i) - nhb_first, 0, nhb_second - 1), 0))]

    const2 = lambda i: (0, 0)
    row512 = pl.BlockSpec((tm, 512), lambda i: (i, 0))
    return pl.pallas_call(
        functools.partial(_in_mix_kernel, n_first=n_first),
        out_shape=[jax.ShapeDtypeStruct((NR, 512), BF16)] * 4,
        grid=(NR // tm,),
        in_specs=_two_source_specs(tm, n_first) + halo_specs(-1) + halo_specs(tm)
                 + [pl.BlockSpec((1, D), const2), _mod_spec(tm, 0), _mod_spec(tm, 1),
                    pl.BlockSpec((D, D_IN), const2),
                    pl.BlockSpec((tm, 128), tab_map), pl.BlockSpec((tm, 128), tab_map),
                    pl.BlockSpec((1, A_WIDTH), const2),
                    pl.BlockSpec((A_GROUPS, CHUNK, CHUNK), lambda i: (0, 0, 0)),
                    pl.BlockSpec((CHUNK, A_WIDTH), const2),
                    pl.BlockSpec((8, B_WIDTH), const2),
                    pl.BlockSpec((1, 512), const2), pl.BlockSpec((1, 512), const2),
                    pl.BlockSpec((RMS_BLOCK, RMS_BLOCK), const2)],
        out_specs=[row512] * 4,
        compiler_params=_cparams(("parallel",), 56),
        name="in_mix",
    )(x_first, x_second, x_first, x_second, x_first, x_second, g.reshape(1, D), mod3, mod3, w_bf,
      cos_t, sin_t, gv, ws_bf, bias_t, wconv, gq, gk, bd)


def _attn_kernel(lam_ref, q_ref, *rest, n_seg, coef, tq):
    kv_refs = rest[:2 * n_seg]
    gsub_ref, o_ref, k_scr, vt_scr = rest[2 * n_seg:]

    @pl.when(pl.program_id(2) == 0)
    def _():
        off = 0
        for s in range(n_seg):
            n = kv_refs[s].shape[0]
            k_scr[off:off + n, :] = kv_refs[s][...]
            vt_scr[0:V_DIM, off:off + n] = kv_refs[n_seg + s][...].astype(F32).T.astype(BF16)
            off += n
        ones_row = lax.broadcasted_iota(jnp.int32, (16, off), 0) == 0
        vt_scr[V_DIM:, :] = jnp.where(ones_row, 1.0, 0.0).astype(BF16)

    lam = lam_ref[0]
    shift = lam_ref[1]
    qc = ATT_CHAIN

    def scores(c):
        rows = slice(c * qc, (c + 1) * qc)
        q = q_ref[rows, :]
        lane = lax.broadcasted_iota(jnp.int32, q.shape, 1)
        zero = jnp.zeros_like(q)
        qs = jnp.concatenate([jnp.where(lane < QK_DIM, q, zero), jnp.where(lane >= QK_DIM, q, zero)], axis=0)
        return lax.dot_general(k_scr[...], qs, _DN_T, preferred_element_type=F32)

    def finish(c, pt):
        ot = jnp.dot(vt_scr[...], pt, preferred_element_type=F32)
        inv = 1.0 / ot[V_DIM:V_DIM + 1, :]
        dt = ot[0:V_DIM, :qc] * inv[:, :qc] - ot[0:V_DIM, qc:] * (lam * inv[:, qc:])
        o = dt.T
        ms = jnp.mean(o * o, axis=-1, keepdims=True)
        o_ref[c * qc:(c + 1) * qc, :] = (o * lax.rsqrt(ms + EPS) * gsub_ref[...] * coef).astype(o_ref.dtype)

    @pl.when(lam_ref[2] > 0.5)
    def _():
        for c in range(tq // qc):
            finish(c, jnp.exp2(scores(c) - shift).astype(BF16))

    @pl.when(lam_ref[2] <= 0.5)
    def _():
        sts = [scores(c) for c in range(tq // qc)]
        for c in range(tq // qc):
            st = sts[c]
            finish(c, jnp.exp2(st - jnp.max(st, axis=0, keepdims=True)).astype(BF16))


def _attention(lam, q, kk, v, gsub, coef, *, ctx_queries):
    if ctx_queries:
        tq = CTX
        nq, lk, n_seg = 1, CTX, 1
        q_map = lambda b, h, qi: (NL // tq + b, h)
        kv_specs = [pl.BlockSpec((CTX, 128), lambda b, h, qi: (NL // CTX + b, h)),
                    pl.BlockSpec((CTX, 128), lambda b, h, qi: (NL // CTX + b, h))]
        kv_args = [kk, v]
        rows = NC
        o_map = lambda b, h, qi: (b, h)
    else:
        tq = TQ
        nq, lk, n_seg = L // tq, CTX + L, 2
        q_map = lambda b, h, qi: (b * (L // tq) + qi, h)
        kv_specs = [pl.BlockSpec((CTX, 128), lambda b, h, qi: (NL // CTX + b, h)),
                    pl.BlockSpec((L, 128), lambda b, h, qi: (b, h)),
                    pl.BlockSpec((CTX, 128), lambda b, h, qi: (NL // CTX + b, h)),
                    pl.BlockSpec((L, 128), lambda b, h, qi: (b, h))]
        kv_args = [kk, kk, v, v]
        rows = NL
        o_map = lambda b, h, qi: (b * (L // tq) + qi, h)
    return pl.pallas_call(
        functools.partial(_attn_kernel, n_seg=n_seg, coef=coef, tq=tq),
        out_shape=jax.ShapeDtypeStruct((rows, C_WIDTH), BF16),
        grid=(B, HEADS, nq),
        in_specs=[pl.BlockSpec(memory_space=pltpu.SMEM),
                  pl.BlockSpec((tq, 128), q_map)]
                 + kv_specs + [pl.BlockSpec((1, V_DIM), lambda b, h, qi: (0, 0))],
        out_specs=pl.BlockSpec((tq, 128), o_map),
        scratch_shapes=[pltpu.VMEM((lk, 128), BF16), pltpu.VMEM((V_DIM + 16, lk), BF16)],
        compiler_params=_cparams(("parallel", "parallel", "arbitrary"), 56),
        name="attn_ctx" if ctx_queries else "attn_lat",
    )(lam, q, *kv_args, gsub)


def _pack_rows(t, out_ref, row0=0):
    half = D // 2
    w = pltpu.pack_elementwise([t[:, :half], t[:, half:]], packed_dtype=BF16)
    w = lax.bitcast_convert_type(w, jnp.uint32)
    rows = t.shape[0]
    for j in range(4):
        out_ref[pl.ds(4 * row0 + j, rows, stride=4), :] = w[:, j * 128:(j + 1) * 128]


def _unpack_rows(ref, rows, lead=None, row0=0):
    los, his = [], []
    for j in range(4):
        sl = pl.ds(4 * row0 + j, rows, stride=4)
        w = ref[sl, :] if lead is None else ref[lead, sl, :]
        los.append(pltpu.unpack_elementwise(w, index=0, packed_dtype=BF16, unpacked_dtype=F32))
        his.append(pltpu.unpack_elementwise(w, index=1, packed_dtype=BF16, unpacked_dtype=F32))
    return jnp.concatenate(los, axis=1), jnp.concatenate(his, axis=1)


def _out_router_kernel(xa_ref, xb_ref, yab_ref, yca_ref, ycb_ref, wo_ref, g1_ref, g2n_ref, sh2_ref, sc2_ref, wrh_ref,
                       wrl_ref, br_ref, tri_ref, xo_ref, h2p_ref, idx_ref, gate_ref, rank_ref, cnt_ref, run_ref, *,
                       n_first, n_first_c):
    tm = TM_OUT

    @pl.when(pl.program_id(0) == 0)
    def _():
        run_ref[...] = jnp.zeros_like(run_ref)

    yc = _two_source_rows(yca_ref, ycb_ref, n_first_c)
    y = jnp.dot(jnp.concatenate([yab_ref[...], yc], axis=1), wo_ref[...], preferred_element_type=F32)
    x = _two_source_rows(xa_ref, xb_ref, n_first) + g1_ref[...] * y
    xo_ref[...] = x
    h2 = _rms_mod(x, g2n_ref[...], sc2_ref[...], sh2_ref[...])
    hi = h2.astype(BF16)
    _pack_rows(h2, h2p_ref)
    lo = (h2 - hi.astype(F32)).astype(BF16)
    wh = wrh_ref[...]
    z = (lax.dot_general(wh, hi, _DN_T, preferred_element_type=F32)
         + lax.dot_general(wh, lo, _DN_T, preferred_element_type=F32)
         + lax.dot_general(wrl_ref[...], hi, _DN_T, preferred_element_type=F32))
    scores = jax.nn.sigmoid(z)
    work = scores + br_ref[...]
    eio = lax.broadcasted_iota(jnp.int32, work.shape, 0)
    idxs, sels, hits = [], [], []
    for _ in range(TOP_K):
        m = jnp.max(work, axis=0, keepdims=True)
        idx = jnp.min(jnp.where(work == m, eio, N_EXPERTS), axis=0, keepdims=True)
        hit = eio == idx
        sels.append(jnp.sum(jnp.where(hit, scores, 0.0), axis=0, keepdims=True))
        idxs.append(idx)
        hits.append(hit)
        work = jnp.where(hit, -jnp.inf, work)
    sel = jnp.concatenate(sels, axis=0)
    idx8 = jnp.concatenate(idxs, axis=0)
    gate8 = sel / jnp.sum(sel, axis=0, keepdims=True) * ROUTED_SCALE

    chosen = functools.reduce(jnp.logical_or, hits)
    prefix = jnp.dot(jnp.where(chosen, 1.0, 0.0).astype(BF16), tri_ref[...], preferred_element_type=F32)
    rank_dense = prefix + run_ref[:, 0:1]
    rank8 = jnp.concatenate([jnp.sum(jnp.where(h, rank_dense, 0.0), axis=0, keepdims=True) for h in hits],
                            axis=0).astype(jnp.int32)
    run = run_ref[...] + jnp.sum(jnp.where(chosen, 1.0, 0.0), axis=1, keepdims=True)
    run_ref[...] = run
    cnt_ref[...] = run
    for c in range(tm // 128):
        idx_ref[c] = idx8[:, c * 128:(c + 1) * 128]
        gate_ref[c] = gate8[:, c * 128:(c + 1) * 128]
        rank_ref[c] = rank8[:, c * 128:(c + 1) * 128]


def _out_router(x_first, x_second, yab, yc_first, yc_second, wo_bf, g2n, mod3, wr_hi, wr_lo, br, n_rows):
    tm = TM_OUT
    n_first = x_first.shape[0] // tm
    n_first_c = yc_first.shape[0] // tm
    const2 = lambda i: (0, 0)
    row = lambda i: (i, 0)
    chunk3 = pl.BlockSpec((tm // 128, TOP_K, 128), lambda i: (i, 0, 0))
    nch = n_rows // 128
    return pl.pallas_call(
        functools.partial(_out_router_kernel, n_first=n_first, n_first_c=n_first_c),
        out_shape=[jax.ShapeDtypeStruct((n_rows, D), F32),
                   jax.ShapeDtypeStruct((n_rows * 4, 128), jnp.uint32),
                   jax.ShapeDtypeStruct((nch, TOP_K, 128), jnp.int32),
                   jax.ShapeDtypeStruct((nch, TOP_K, 128), F32),
                   jax.ShapeDtypeStruct((nch, TOP_K, 128), jnp.int32),
                   jax.ShapeDtypeStruct((N_EXPERTS, 128), F32)],
        grid=(n_rows // tm,),
        in_specs=_two_source_specs(tm, n_first)
                 + [pl.BlockSpec((tm, 512), row)]
                 + _two_source_specs(tm, n_first_c, C_WIDTH)
                 + [pl.BlockSpec((D, D), const2),
                  _mod_spec(tm, 2),
                  pl.BlockSpec((1, D), const2), _mod_spec(tm, 3), _mod_spec(tm, 4),
                  pl.BlockSpec((N_EXPERTS, D), const2), pl.BlockSpec((N_EXPERTS, D), const2),
                  pl.BlockSpec((N_EXPERTS, 1), const2), pl.BlockSpec((tm, tm), const2)],
        out_specs=[pl.BlockSpec((tm, D), row), pl.BlockSpec((tm * 4, 128), row),
                   chunk3, chunk3, chunk3, pl.BlockSpec((N_EXPERTS, 128), const2)],
        scratch_shapes=[pltpu.VMEM((N_EXPERTS, 128), F32)],
        compiler_params=_cparams(("arbitrary",), 48),
        name="out_router",
    )(x_first, x_second, yab, yc_first, yc_second, wo_bf, mod3, g2n.reshape(1, D), mod3, mod3, wr_hi, wr_lo,
      br.reshape(N_EXPERTS, 1), jnp.asarray(np.triu(np.ones((tm, tm), np.float32), k=1), BF16))


def _experts_kernel(te_ref, nx_ref, ws_ref, tr_ref, nu_ref, x_hbm, wg_hbm, wu_hbm, wd_hbm, y_ref,
                    wgu_s, wd_s, x_ring, x_sems, wg_f, wu_f, wd_f, w_sems, *, layer):
    i = pl.program_id(0)
    n_used = nu_ref[0]
    used = i < n_used
    new_expert = jnp.logical_or(i == 0, te_ref[i] != te_ref[jnp.maximum(i - 1, 0)])

    def w_copies(e, slot):
        return [pltpu.make_async_copy(wg_hbm.at[layer, e], wg_f.at[slot], w_sems.at[slot, 0]),
                pltpu.make_async_copy(wu_hbm.at[layer, e], wu_f.at[slot], w_sems.at[slot, 1]),
                pltpu.make_async_copy(wd_hbm.at[layer, e], wd_f.at[slot], w_sems.at[slot, 2])]

    @pl.when(i == 0)
    def _():
        for cp in w_copies(te_ref[0], ws_ref[0]):
            cp.start()

    ahead = X_RING - 1

    def x_copy(t):
        slot = t % X_RING
        return pltpu.make_async_copy(x_hbm.at[pl.ds(t * (4 * TM_X), 4 * TM_X)], x_ring.at[slot], x_sems.at[slot])

    @pl.when(i == 0)
    def _():
        for t in range(ahead):
            @pl.when(t < n_used)
            def _():
                x_copy(t).start()

    @pl.when(i + ahead < n_used)
    def _():
        x_copy(i + ahead).start()

    @pl.when(jnp.logical_and(used, new_expert))
    def _():
        slot = ws_ref[i]
        for cp in w_copies(te_ref[i], slot):
            cp.wait()

        @pl.when(nx_ref[i] >= 0)
        def _():
            for cp in w_copies(nx_ref[i], 1 - slot):
                cp.start()

        wgu_s[:, 0:D_EXPERT] = wg_f[slot].astype(BF16)
        wgu_s[:, D_EXPERT:] = wu_f[slot].astype(BF16)
        wd_s[...] = wd_f[slot].astype(BF16)

    def first_dot(x_ref, row0, rows):
        x_lo, x_hi = _unpack_rows(x_ref, rows, row0=row0)
        x = jnp.concatenate([x_lo.astype(BF16), x_hi.astype(BF16)], axis=1)
        return jnp.dot(x, wgu_s[...], preferred_element_type=F32)

    def second_dot(ab, row0):
        a = ab[:, :D_EXPERT]
        hid = (a * jax.nn.sigmoid(a) * ab[:, D_EXPERT:]).astype(BF16)
        _pack_rows(jnp.dot(hid, wd_s[...], preferred_element_type=F32), y_ref, row0=row0)

    @pl.when(used)
    def _():
        x_copy(i).wait()

    valid = tr_ref[i]

    @pl.when(jnp.logical_and(used, valid > X_TAIL_ROWS))
    def _():
        x_ref = x_ring.at[i % X_RING]
        rc = TM_X // X_CHAINS
        abs_ = [first_dot(x_ref, c * rc, rc) for c in range(X_CHAINS)]
        for c in range(X_CHAINS):
            second_dot(abs_[c], c * rc)

    for c in range(X_TAIL_ROWS // X_TAIL):
        @pl.when(jnp.logical_and(used, jnp.logical_and(valid <= X_TAIL_ROWS, c * X_TAIL < valid)))
        def _():
            second_dot(first_dot(x_ring.at[i % X_RING], c * X_TAIL, X_TAIL), c * X_TAIL)


def _experts(layer, sched, x_sorted, w_gate, w_up, w_down):
    tm = TM_X
    r_pad = x_sorted.shape[0] // 4
    nt = r_pad // tm
    row = lambda i, te, nx, ws, tr, nu: (jnp.minimum(i, nu[0] - 1), 0)
    hbm = pl.BlockSpec(memory_space=pl.ANY)
    grid_spec = pltpu.PrefetchScalarGridSpec(
        num_scalar_prefetch=5,
        grid=(nt,),
        in_specs=[hbm, hbm, hbm, hbm],
        out_specs=pl.BlockSpec((tm * 4, 128), row),
        scratch_shapes=[pltpu.VMEM((D, 2 * D_EXPERT), BF16), pltpu.VMEM((D_EXPERT, D), BF16),
                        pltpu.VMEM((X_RING, tm * 4, 128), jnp.uint32), pltpu.SemaphoreType.DMA((X_RING,)),
                        pltpu.VMEM((2, D, D_EXPERT), F32), pltpu.VMEM((2, D, D_EXPERT), F32),
                        pltpu.VMEM((2, D_EXPERT, D), F32), pltpu.SemaphoreType.DMA((2, 3))],
    )
    return pl.pallas_call(
        functools.partial(_experts_kernel, layer=layer),
        out_shape=jax.ShapeDtypeStruct((r_pad * 4, 128), jnp.uint32),
        grid_spec=grid_spec,
        compiler_params=_cparams(("arbitrary",), 48),
        name="experts",
    )(*sched, x_sorted, w_gate, w_up, w_down)


SC_CORES = 2
SC_SUBCORES = 16
SC_WORKERS = SC_CORES * SC_SUBCORES
SC_CHUNK = 128


def _sc_mesh():
    return plsc.VectorSubcoreMesh(core_axis_name="c", subcore_axis_name="s")


def _sc_params():
    return pltpu.CompilerParams(use_tc_tiling_on_sc=True)


def _sc_dispatch(h2p, pos3, r_pad):
    nch = pos3.shape[0]
    steps = -(-nch // SC_WORKERS)

    def body(h_hbm, pos_hbm, out_hbm, idx_v, rows_v, sem):
        wid = lax.axis_index("s") * SC_CORES + lax.axis_index("c")

        @pl.loop(0, steps)
        def _(s):
            ch = wid + s * SC_WORKERS

            @pl.when(ch < nch)
            def _():
                pltpu.sync_copy(pos_hbm.at[ch], idx_v)
                pltpu.sync_copy(h_hbm.at[pl.ds(ch * SC_CHUNK, SC_CHUNK)], rows_v)
                copies = [pltpu.async_copy(rows_v, out_hbm.at[idx_v.at[k]], sem) for k in range(TOP_K)]
                for cp in copies:
                    cp.wait()

    return pl.kernel(
        body,
        out_type=jax.ShapeDtypeStruct((r_pad, 4, 128), jnp.uint32),
        mesh=_sc_mesh(),
        scratch_types=[pltpu.VMEM((TOP_K, SC_CHUNK), jnp.int32),
                       pltpu.VMEM((SC_CHUNK, 4, 128), jnp.uint32),
                       pltpu.SemaphoreType.DMA],
        compiler_params=_sc_params(),
        name="sc_dispatch",
    )(h2p, pos3)


def _sc_collect(y_sorted, pos3):
    nch = pos3.shape[0]
    steps = -(-nch // SC_WORKERS)
    half = SC_CHUNK // 2
    units = [(k, hh) for k in range(TOP_K) for hh in range(2)]

    def body(y_hbm, pos_hbm, out_hbm, idx_v, rows_a, rows_b, sem_a, sem_b):
        wid = lax.axis_index("s") * SC_CORES + lax.axis_index("c")
        bufs = (rows_a, rows_b)
        sems = (sem_a, sem_b)

        def gather(u):
            k, hh = units[u]
            return pltpu.async_copy(y_hbm.at[idx_v.at[k, pl.ds(hh * half, half)]], bufs[u % 2], sems[u % 2])

        @pl.loop(0, steps)
        def _(s):
            ch = wid + s * SC_WORKERS

            @pl.when(ch < nch)
            def _():
                pltpu.sync_copy(pos_hbm.at[ch], idx_v)
                pending = gather(0)
                for u, (k, hh) in enumerate(units):
                    nxt = gather(u + 1) if u + 1 < len(units) else None
                    pending.wait()
                    pltpu.sync_copy(bufs[u % 2], out_hbm.at[k, pl.ds(ch * SC_CHUNK + hh * half, half)])
                    pending = nxt

    return pl.kernel(
        body,
        out_type=jax.ShapeDtypeStruct((TOP_K, nch * SC_CHUNK, 4, 128), jnp.uint32),
        mesh=_sc_mesh(),
        scratch_types=[pltpu.VMEM((TOP_K, SC_CHUNK), jnp.int32),
                       pltpu.VMEM((half, 4, 128), jnp.uint32),
                       pltpu.VMEM((half, 4, 128), jnp.uint32),
                       pltpu.SemaphoreType.DMA, pltpu.SemaphoreType.DMA],
        compiler_params=_sc_params(),
        name="sc_collect",
    )(y_sorted, pos3)


def _shared_kernel(x_ref, h2p_ref, yg_ref, gate_ref, wg_ref, wu_ref, wd_ref, g2_ref, o_ref):
    h_lo, h_hi = _unpack_rows(h2p_ref, TM_F)
    h = jnp.concatenate([h_lo.astype(BF16), h_hi.astype(BF16)], axis=1)
    a = jnp.dot(h, wg_ref[...], preferred_element_type=F32)
    b = jnp.dot(h, wu_ref[...], preferred_element_type=F32)
    hid = (a * jax.nn.sigmoid(a) * b).astype(BF16)
    f = jnp.dot(hid, wd_ref[...], preferred_element_type=F32)
    gate = gate_ref[...]
    f_lo = f[:, :D // 2]
    f_hi = f[:, D // 2:]
    for k in range(TOP_K):
        y_lo, y_hi = _unpack_rows(yg_ref, TM_F, lead=k)
        f_lo = f_lo + gate[:, k:k + 1] * y_lo
        f_hi = f_hi + gate[:, k:k + 1] * y_hi
    o_ref[...] = x_ref[...] + g2_ref[...] * jnp.concatenate([f_lo, f_hi], axis=1)


def _shared_residual(xa, h2p, yg, gates, wsg_bf, wsu_bf, wsd_bf, mod3, n_rows):
    tm = TM_F
    row = lambda i: (i, 0)
    const2 = lambda i: (0, 0)
    return pl.pallas_call(
        _shared_kernel,
        out_shape=jax.ShapeDtypeStruct((n_rows, D), F32),
        grid=(n_rows // tm,),
        in_specs=[pl.BlockSpec((tm, D), row), pl.BlockSpec((tm * 4, 128), row),
                  pl.BlockSpec((TOP_K, tm * 4, 128), lambda i: (0, i, 0)),
                  pl.BlockSpec((tm, TOP_K), row),
                  pl.BlockSpec((D, D_SHARED), const2), pl.BlockSpec((D, D_SHARED), const2),
                  pl.BlockSpec((D_SHARED, D), const2), _mod_spec(tm, 5)],
        out_specs=pl.BlockSpec((tm, D), row),
        compiler_params=_cparams(("parallel",), 48),
        name="shared_residual",
    )(xa, h2p, yg, gates, wsg_bf, wsu_bf, wsd_bf, mod3)


def _positions_kernel(offs_ref, idx_ref, rank_ref, pos_ref):
    idx = idx_ref[...]
    base = jnp.zeros_like(idx)
    for e in range(N_EXPERTS):
        base = jnp.where(idx == e, offs_ref[e], base)
    pos_ref[...] = rank_ref[...] + base


def _positions(offs, idx3, rank3):
    nch = idx3.shape[0]
    cb = nch // 2
    spec = pl.BlockSpec((cb, TOP_K, 128), lambda i, offs: (i, 0, 0))
    return pl.pallas_call(
        _positions_kernel,
        out_shape=jax.ShapeDtypeStruct((nch, TOP_K, 128), jnp.int32),
        grid_spec=pltpu.PrefetchScalarGridSpec(num_scalar_prefetch=1, grid=(nch // cb,),
                                               in_specs=[spec, spec], out_specs=spec),
        compiler_params=_cparams(("parallel",)),
        name="positions",
    )(offs, idx3, rank3)


def _route_positions(idx3, rank3, counts, n_rows):
    tm = TM_X
    counts = counts.astype(jnp.int32)
    padded = ((counts + tm - 1) // tm) * tm
    ends = jnp.cumsum(padded)
    offs = ends - padded
    pos3 = _positions(offs.astype(jnp.int32), idx3, rank3)
    r_pad = n_rows * TOP_K + N_EXPERTS * tm
    nt = r_pad // tm
    tile_ids = jnp.arange(nt, dtype=jnp.int32)
    tile_expert = jnp.sum((ends // tm)[None, :] <= tile_ids[:, None], axis=1)
    tile_expert = jnp.minimum(tile_expert, N_EXPERTS - 1).astype(jnp.int32)
    n_used = (ends[-1] // tm).astype(jnp.int32).reshape(1)
    e_ids = jnp.arange(N_EXPERTS, dtype=jnp.int32)
    nonempty = counts > 0
    later = jnp.where(nonempty[None, :] & (e_ids[None, :] > e_ids[:, None]), e_ids[None, :], N_EXPERTS)
    next_e = jnp.min(later, axis=1)
    next_e = jnp.where(next_e == N_EXPERTS, -1, next_e).astype(jnp.int32)
    slot_e = ((jnp.cumsum(nonempty.astype(jnp.int32)) - nonempty.astype(jnp.int32)) % 2).astype(jnp.int32)
    tile_rows = jnp.clip(counts[tile_expert] - (tile_ids - (offs // tm)[tile_expert]) * tm, 0, tm)
    tile_rows = jnp.where(tile_ids < n_used[0], tile_rows, 0).astype(jnp.int32)
    sched = (tile_expert, next_e[tile_expert], slot_e[tile_expert], tile_rows, n_used)
    return pos3, sched, r_pad


def _in_weights(w):
    def regroup(cols):
        return cols.reshape(D, 2, HEADS, QK_DIM).transpose(0, 2, 1, 3).reshape(D, 2 * HEADS * QK_DIM)
    return jnp.concatenate([w[:, :OFF_Q], regroup(w[:, OFF_Q:OFF_K]), regroup(w[:, OFF_K:OFF_V]), w[:, OFF_V:]],
                           axis=1).astype(BF16)


def _rope_tables():
    t = jnp.arange(L)
    row = (t // GRID_W).astype(F32)
    col = (t % GRID_W).astype(F32)
    n_freq = QK_DIM // 4
    inv = ROPE_BASE ** (-jnp.arange(n_freq, dtype=F32) / n_freq)
    ar = row[:, None] * inv
    ac = col[:, None] * inv
    cos64 = jnp.concatenate([jnp.cos(ar), jnp.cos(ar), jnp.cos(ac), jnp.cos(ac)], axis=1)
    sin64 = jnp.concatenate([-jnp.sin(ar), jnp.sin(ar), -jnp.sin(ac), jnp.sin(ac)], axis=1)
    cos_t = jnp.concatenate([jnp.tile(cos64, (1, 2)), jnp.ones((TM_IN, 128), F32)], axis=0)
    sin_t = jnp.concatenate([jnp.tile(sin64, (1, 2)), jnp.zeros((TM_IN, 128), F32)], axis=0)
    return cos_t, sin_t


def _split_bf16(w):
    hi = w.astype(BF16)
    return hi, (w - hi.astype(F32)).astype(BF16)


def kernel(x, c, ctx, c_ctx, w_ada, b_ada, g_norm1, g_norm2, w_in, w_out, g_v, w_s, b_s, w_conv, g_q, g_k,
           lam_q1, lam_k1, lam_q2, lam_k2, g_sub, w_router, b_router, w_gate, w_up, w_down,
           ws_gate, ws_up, ws_down):
    src = (x.reshape(NL, D), ctx.reshape(NC, D))
    cc = jnp.concatenate([c, c_ctx[None, :], jnp.zeros((MOD_ROWS - B - 1, D), F32)], axis=0)
    mod = _ada(cc, w_ada, b_ada)
    cos_t, sin_t = _rope_tables()
    bd = jnp.asarray(np.kron(np.eye(RMS_BLOCK // QK_DIM, dtype=np.float32),
                             np.full((QK_DIM, QK_DIM), 1.0 / QK_DIM, np.float32)), BF16)

    for l in range(DEPTH):
        last = l == DEPTH - 1
        lam_init = 0.8 - 0.6 * math.exp(-0.3 * l)
        lam = (jnp.exp(jnp.sum(lam_q1[l] * lam_k1[l])) - jnp.exp(jnp.sum(lam_q2[l] * lam_k2[l])) + lam_init)
        bound = (QK_DIM * jnp.max(jnp.abs(g_q[l])) * jnp.max(jnp.abs(g_k[l]))
                 * (QK_DIM ** -0.5 * LOG2E) * ATT_BOUND_MARGIN)
        use_bound = (2.0 * bound < ATT_MAX_SHIFT_RANGE).astype(F32)
        lam = jnp.stack([lam, bound, use_bound]).astype(F32)
        mod3 = mod[l].reshape(MOD_ROWS, 1, 6 * D)
        w_in_bf = _in_weights(w_in[l])
        bias_t = jnp.repeat(b_s[l].T, A_GD, axis=1)
        wconv = jnp.concatenate([w_conv[l], jnp.zeros((5, B_WIDTH), F32)], axis=0)
        yab, q, kk, v = _in_mix(src[0], src[1], g_norm1[l], mod3, w_in_bf, cos_t, sin_t,
                                     g_v[l].reshape(1, A_WIDTH), w_s[l].astype(BF16), bias_t, wconv,
                                     jnp.tile(g_q[l], 8).reshape(1, 512), jnp.tile(g_k[l], 8).reshape(1, 512), bd)
        gsub = g_sub[l].reshape(1, V_DIM)
        coef = 1.0 - lam_init
        n_rows = NL if last else NR
        yc = _attention(lam, q, kk, v, gsub, coef, ctx_queries=False)
        yc_ctx = yc if last else _attention(lam, q, kk, v, gsub, coef, ctx_queries=True)
        wr_hi, wr_lo = _split_bf16(w_router[l].T)
        wo_bf = w_out[l].astype(BF16)
        ws_bf = (ws_gate[l].astype(BF16), ws_up[l].astype(BF16), ws_down[l].astype(BF16))
        xa, h2p, idx3, gate3, rank3, counts = _out_router(
            src[0], src[1], yab, yc, yc_ctx, wo_bf, g_norm2[l], mod3, wr_hi, wr_lo, b_router[l], n_rows)
        pos3, sched, r_pad = _route_positions(idx3, rank3, counts[:, 0], n_rows)
        x_sorted = _sc_dispatch(h2p.reshape(n_rows, 4, 128), pos3, r_pad)
        y_sorted = _experts(l, sched, x_sorted.reshape(r_pad * 4, 128), w_gate, w_up, w_down)
        yg = _sc_collect(y_sorted.reshape(r_pad, 4, 128), pos3)
        gates = gate3.transpose(0, 2, 1).reshape(n_rows, TOP_K)
        xa = _shared_residual(xa, h2p, yg.reshape(TOP_K, n_rows * 4, 128), gates, *ws_bf, mod3, n_rows)
        src = (xa, xa)
    return xa.reshape(B, L, D)
```

```python
import functools
import math

import numpy as np
import jax
import jax.numpy as jnp
from jax import lax
from jax.experimental import pallas as pl
from jax.experimental.pallas import tpu as pltpu
from jax.experimental.pallas import tpu_sc as plsc

F32 = jnp.float32
BF16 = jnp.bfloat16

D = 1024
B = 8
L = 2048
DEPTH = 2
GRID_W = 64
CTX = 256
A_WIDTH = 256
A_GROUPS = 4
A_GD = 64
CHUNK = 128
B_WIDTH = 256
C_WIDTH = 512
HEADS = 4
V_DIM = 128
QK_DIM = 64
ROPE_BASE = 10000.0
OFF_BB = 512
OFF_BC = 768
OFF_BX = 1024
OFF_Q = 1280
OFF_K = 1792
OFF_V = 2304
D_IN = 2816
N_EXPERTS = 64
TOP_K = 8
D_EXPERT = 256
D_SHARED = 256
ROUTED_SCALE = 2.5
EPS = 1e-6

NL = B * L
NC = B * CTX
NR = NL + NC
MOD_ROWS = 16
LOG2E = 1.4426950408889634

TM_IN = 512
RMS_BLOCK = 256
TQ = 2048
ATT_CHAIN = 256
ATT_GROUP = 4
ATT_BOUND_MARGIN = 1.02
ATT_MAX_SHIFT_RANGE = 100.0
TM_OUT = 512
TM_X = 1024
X_CHAINS = 2
X_TAIL_ROWS = 512
X_TAIL = 256
X_RING = 3
TM_F = 512

_DN_T = (((1,), (1,)), ((), ()))


def _cparams(sem, vmem_mb=None):
    kw = dict(dimension_semantics=sem)
    if vmem_mb is not None:
        kw["vmem_limit_bytes"] = vmem_mb * 1024 * 1024
    return pltpu.CompilerParams(**kw)


def _mod_row(i, tm):
    return jnp.where(i < NL // tm, i // (L // tm), B)


def _mod_spec(tm, chunk):
    return pl.BlockSpec((None, 1, D), lambda i: (_mod_row(i, tm), 0, chunk))


def _ada_kernel(c_ref, w_ref, b_ref, o_ref):
    c = c_ref[...]
    cs = c * jax.nn.sigmoid(c)
    o_ref[...] = jnp.dot(cs, w_ref[...], preferred_element_type=F32,
                         precision=lax.Precision.HIGHEST) + b_ref[...]


def _ada(cc, w_ada, b_ada):
    nb = 6
    return pl.pallas_call(
        _ada_kernel,
        out_shape=jax.ShapeDtypeStruct((DEPTH, MOD_ROWS, 6 * D), F32),
        grid=(DEPTH, nb),
        in_specs=[pl.BlockSpec((MOD_ROWS, D), lambda l, j: (0, 0)),
                  pl.BlockSpec((None, D, D), lambda l, j: (l, 0, j)),
                  pl.BlockSpec((None, 1, D), lambda l, j: (l, 0, j))],
        out_specs=pl.BlockSpec((None, MOD_ROWS, D), lambda l, j: (l, 0, j)),
        compiler_params=_cparams(("arbitrary", "arbitrary"), 40),
        name="ada_mod",
    )(cc, w_ada, b_ada.reshape(DEPTH, 1, 6 * D))


def _rms_mod(x, g, sc, sh):
    ms = jnp.mean(x * x, axis=-1, keepdims=True)
    return x * lax.rsqrt(ms + EPS) * (g * (1.0 + sc)) + sh


def _two_source_specs(tm, n_first, width=D):
    return [pl.BlockSpec((tm, width), lambda i: (jnp.minimum(i, n_first - 1), 0)),
            pl.BlockSpec((tm, width), lambda i: (jnp.maximum(i - n_first, 0), 0))]


def _two_source_rows(a_ref, b_ref, n_first):
    return jnp.where(pl.program_id(0) < n_first, a_ref[...], b_ref[...])


def _group_rms(t, g, bd):
    sq = (t * t).astype(BF16)
    ms = jnp.concatenate([jnp.dot(sq[:, c:c + RMS_BLOCK], bd, preferred_element_type=F32)
                          for c in range(0, t.shape[1], RMS_BLOCK)], axis=1)
    return t * lax.rsqrt(ms + EPS) * g


def _rope(t, cos, sin):
    w = t.shape[1]
    lane = lax.broadcasted_iota(jnp.int32, t.shape, 1)
    first = (lane % 32) < 16
    partner = jnp.where(first, pltpu.roll(t, w - 16, 1), pltpu.roll(t, 16, 1))
    cos4 = jnp.concatenate([cos] * (w // 128), axis=1)
    sin4 = jnp.concatenate([sin] * (w // 128), axis=1)
    return t * cos4 + partner * sin4


def _in_mix_kernel(xa_ref, xb_ref, pa_ref, pb_ref, na_ref, nb_ref, g_ref, sh_ref, sc_ref, w_ref,
                   cos_ref, sin_ref, gv_ref, ws_ref, bias_ref, wconv_ref, gq_ref, gk_ref, bd_ref,
                   yab_ref, q_ref, kk_ref, v_ref, *, n_first):
    tm = TM_IN
    i = pl.program_id(0)
    tiles_per_seq = L // tm
    is_lat = i < NL // tm
    is_start = jnp.logical_or(jnp.logical_not(is_lat), i % tiles_per_seq == 0)
    is_end = jnp.logical_or(jnp.logical_not(is_lat), i % tiles_per_seq == tiles_per_seq - 1)
    first = i < n_first
    g, sc, sh = g_ref[...], sc_ref[...], sh_ref[...]

    h = _rms_mod(jnp.where(first, xa_ref[...], xb_ref[...]), g, sc, sh)
    p = jnp.dot(h.astype(BF16), w_ref[...], preferred_element_type=F32)
    v_ref[...] = p[:, OFF_V:].astype(BF16)
    halo = jnp.concatenate([jnp.where(first, pa_ref[...], pb_ref[...]),
                            jnp.where(first, na_ref[...], nb_ref[...])], axis=0)
    ph = jnp.dot(_rms_mod(halo, g, sc, sh).astype(BF16), w_ref[:, OFF_BC:OFF_Q], preferred_element_type=F32)
    zh = ph[:, :B_WIDTH] * ph[:, B_WIDTH:]
    zp = jnp.where(is_start, 0.0, zh[15:16])
    zn = jnp.where(is_end, 0.0, zh[16:17])

    uv = p[:, 0:2 * A_WIDTH]
    uv = 0.5 * uv * (1.0 + lax.erf(uv * (2.0 ** -0.5)))
    u = uv[:, :A_WIDTH]
    v = uv[:, A_WIDTH:]
    ms = jnp.mean(v * v, axis=-1, keepdims=True)
    vb = (v * lax.rsqrt(ms + EPS) * gv_ref[...]).astype(BF16)
    lane = lax.broadcasted_iota(jnp.int32, (CHUNK, 128), 1)
    mixes = []
    for c in range(tm // CHUNK):
        vc = vb[c * CHUNK:(c + 1) * CHUNK]
        halves = []
        for j in range(2):
            vj = vc[:, j * 128:(j + 1) * 128]
            m0 = jnp.dot(ws_ref[2 * j], vj, preferred_element_type=F32)
            m1 = jnp.dot(ws_ref[2 * j + 1], vj, preferred_element_type=F32)
            halves.append(jnp.where(lane < A_GD, m0, m1))
        mixes.append(jnp.concatenate(halves, axis=1) + bias_ref[...])
    ya = u * jnp.concatenate(mixes, axis=0)

    bg = p[:, OFF_BB:OFF_BC]
    z = p[:, OFF_BC:OFF_BX] * p[:, OFF_BX:OFF_Q]
    row = lax.broadcasted_iota(jnp.int32, z.shape, 0)
    inner = jnp.logical_not(is_lat)
    z_prev = jnp.where(row == 0, zp, pltpu.roll(z, 1, 0))
    z_prev = jnp.where(jnp.logical_and(inner, row % CTX == 0), 0.0, z_prev)
    z_next = jnp.where(row == tm - 1, zn, pltpu.roll(z, tm - 1, 0))
    z_next = jnp.where(jnp.logical_and(inner, row % CTX == CTX - 1), 0.0, z_next)
    yb = bg * (z_prev * wconv_ref[0:1] + z * wconv_ref[1:2] + z_next * wconv_ref[2:3])
    yab_ref[...] = jnp.concatenate([ya, yb], axis=1).astype(BF16)

    cos = cos_ref[...]
    sin = sin_ref[...]
    bd = bd_ref[...]
    q = _rope(_group_rms(p[:, OFF_Q:OFF_K], gq_ref[...], bd), cos, sin)
    q = q * (QK_DIM ** -0.5 * LOG2E)
    q_ref[...] = q.astype(BF16)
    k = _rope(_group_rms(p[:, OFF_K:OFF_V], gk_ref[...], bd), cos, sin)
    kk_ref[...] = k.astype(BF16)


def _in_mix(x_first, x_second, g, mod3, w_bf, cos_t, sin_t, gv, ws_bf, bias_t, wconv, gq, gk, bd):
    tm = TM_IN
    n_first = x_first.shape[0] // tm
    hb = tm // 16
    nhb_first = x_first.shape[0] // 16
    nhb_second = x_second.shape[0] // 16
    pos_blocks = L // tm

    def tab_map(i):
        return (jnp.where(i < NL // tm, i % pos_blocks, pos_blocks), 0)

    def halo_specs(shift):
        blk = lambda i: (i * tm + shift) // 16
        return [pl.BlockSpec((16, D), lambda i: (jnp.clip(blk(i), 0, nhb_first - 1), 0)),
                pl.BlockSpec((16, D), lambda i: (jnp.clip(blk(i) - nhb_first, 0, nhb_second - 1), 0))]

    const2 = lambda i: (0, 0)
    row512 = pl.BlockSpec((tm, 512), lambda i: (i, 0))
    return pl.pallas_call(
        functools.partial(_in_mix_kernel, n_first=n_first),
        out_shape=[jax.ShapeDtypeStruct((NR, 512), BF16)] * 4,
        grid=(NR // tm,),
        in_specs=_two_source_specs(tm, n_first) + halo_specs(-1) + halo_specs(tm)
                 + [pl.BlockSpec((1, D), const2), _mod_spec(tm, 0), _mod_spec(tm, 1),
                    pl.BlockSpec((D, D_IN), const2),
                    pl.BlockSpec((tm, 128), tab_map), pl.BlockSpec((tm, 128), tab_map),
                    pl.BlockSpec((1, A_WIDTH), const2),
                    pl.BlockSpec((A_GROUPS, CHUNK, CHUNK), lambda i: (0, 0, 0)),
                    pl.BlockSpec((CHUNK, A_WIDTH), const2),
                    pl.BlockSpec((8, B_WIDTH), const2),
                    pl.BlockSpec((1, 512), const2), pl.BlockSpec((1, 512), const2),
                    pl.BlockSpec((RMS_BLOCK, RMS_BLOCK), const2)],
        out_specs=[row512] * 4,
        compiler_params=_cparams(("parallel",), 56),
        name="in_mix",
    )(x_first, x_second, x_first, x_second, x_first, x_second, g.reshape(1, D), mod3, mod3, w_bf,
      cos_t, sin_t, gv, ws_bf, bias_t, wconv, gq, gk, bd)


def _attn_kernel(lam_ref, q_ref, *rest, n_seg, coef, tq):
    kv_refs = rest[:2 * n_seg]
    gsub_ref, o_ref, k_scr, vt_scr = rest[2 * n_seg:]

    @pl.when(pl.program_id(2) == 0)
    def _():
        off = 0
        for s in range(n_seg):
            n = kv_refs[s].shape[0]
            k_scr[off:off + n, :] = kv_refs[s][...]
            vt_scr[0:V_DIM, off:off + n] = kv_refs[n_seg + s][...].astype(F32).T.astype(BF16)
            off += n
        ones_row = lax.broadcasted_iota(jnp.int32, (16, off), 0) == 0
        vt_scr[V_DIM:, :] = jnp.where(ones_row, 1.0, 0.0).astype(BF16)

    lam = lam_ref[0]
    shift = lam_ref[1]
    qc = ATT_CHAIN

    def scores(c):
        rows = slice(c * qc, (c + 1) * qc)
        q = q_ref[rows, :]
        lane = lax.broadcasted_iota(jnp.int32, q.shape, 1)
        zero = jnp.zeros_like(q)
        qs = jnp.concatenate([jnp.where(lane < QK_DIM, q, zero), jnp.where(lane >= QK_DIM, q, zero)], axis=0)
        return lax.dot_general(k_scr[...], qs, _DN_T, preferred_element_type=F32)

    def finish(c, pt):
        ot = jnp.dot(vt_scr[...], pt, preferred_element_type=F32)
        inv = 1.0 / ot[V_DIM:V_DIM + 1, :]
        dt = ot[0:V_DIM, :qc] * inv[:, :qc] - ot[0:V_DIM, qc:] * (lam * inv[:, qc:])
        o = dt.T
        ms = jnp.mean(o * o, axis=-1, keepdims=True)
        o_ref[c * qc:(c + 1) * qc, :] = (o * lax.rsqrt(ms + EPS) * gsub_ref[...] * coef).astype(o_ref.dtype)

    @pl.when(lam_ref[2] > 0.5)
    def _():
        for c in range(tq // qc):
            finish(c, jnp.exp2(scores(c) - shift).astype(BF16))

    @pl.when(lam_ref[2] <= 0.5)
    def _():
        for g0 in range(0, tq // qc, ATT_GROUP):
            group = range(g0, min(g0 + ATT_GROUP, tq // qc))
            sts = [scores(c) for c in group]
            for c, st in zip(group, sts):
                finish(c, jnp.exp2(st - jnp.max(st, axis=0, keepdims=True)).astype(BF16))


def _attention(lam, q, kk, v, gsub, coef, *, ctx_queries):
    if ctx_queries:
        tq = CTX
        nq, lk, n_seg = 1, CTX, 1
        q_map = lambda b, h, qi: (NL // tq + b, h)
        kv_specs = [pl.BlockSpec((CTX, 128), lambda b, h, qi: (NL // CTX + b, h)),
                    pl.BlockSpec((CTX, 128), lambda b, h, qi: (NL // CTX + b, h))]
        kv_args = [kk, v]
        rows = NC
        o_map = lambda b, h, qi: (b, h)
    else:
        tq = TQ
        nq, lk, n_seg = L // tq, CTX + L, 2
        q_map = lambda b, h, qi: (b * (L // tq) + qi, h)
        kv_specs = [pl.BlockSpec((CTX, 128), lambda b, h, qi: (NL // CTX + b, h)),
                    pl.BlockSpec((L, 128), lambda b, h, qi: (b, h)),
                    pl.BlockSpec((CTX, 128), lambda b, h, qi: (NL // CTX + b, h)),
                    pl.BlockSpec((L, 128), lambda b, h, qi: (b, h))]
        kv_args = [kk, kk, v, v]
        rows = NL
        o_map = lambda b, h, qi: (b * (L // tq) + qi, h)
    return pl.pallas_call(
        functools.partial(_attn_kernel, n_seg=n_seg, coef=coef, tq=tq),
        out_shape=jax.ShapeDtypeStruct((rows, C_WIDTH), BF16),
        grid=(B, HEADS, nq),
        in_specs=[pl.BlockSpec(memory_space=pltpu.SMEM),
                  pl.BlockSpec((tq, 128), q_map)]
                 + kv_specs + [pl.BlockSpec((1, V_DIM), lambda b, h, qi: (0, 0))],
        out_specs=pl.BlockSpec((tq, 128), o_map),
        scratch_shapes=[pltpu.VMEM((lk, 128), BF16), pltpu.VMEM((V_DIM + 16, lk), BF16)],
        compiler_params=_cparams(("parallel", "parallel", "arbitrary"), 56),
        name="attn_ctx" if ctx_queries else "attn_lat",
    )(lam, q, *kv_args, gsub)


def _pack_rows(t, out_ref, row0=0):
    half = D // 2
    w = pltpu.pack_elementwise([t[:, :half], t[:, half:]], packed_dtype=BF16)
    w = lax.bitcast_convert_type(w, jnp.uint32)
    rows = t.shape[0]
    for j in range(4):
        out_ref[pl.ds(4 * row0 + j, rows, stride=4), :] = w[:, j * 128:(j + 1) * 128]


def _unpack_rows(ref, rows, lead=None, row0=0):
    los, his = [], []
    for j in range(4):
        sl = pl.ds(4 * row0 + j, rows, stride=4)
        w = ref[sl, :] if lead is None else ref[lead, sl, :]
        los.append(pltpu.unpack_elementwise(w, index=0, packed_dtype=BF16, unpacked_dtype=F32))
        his.append(pltpu.unpack_elementwise(w, index=1, packed_dtype=BF16, unpacked_dtype=F32))
    return jnp.concatenate(los, axis=1), jnp.concatenate(his, axis=1)


def _out_router_kernel(xa_ref, xb_ref, yab_ref, yca_ref, ycb_ref, wo_ref, g1_ref, g2n_ref, sh2_ref, sc2_ref, wrh_ref,
                       wrl_ref, br_ref, xo_ref, h2p_ref, idx_ref, gate_ref, rank_ref, cnt_ref, run_ref, *,
                       n_first, n_first_c):
    tm = TM_OUT

    @pl.when(pl.program_id(0) == 0)
    def _():
        run_ref[...] = jnp.zeros_like(run_ref)

    yc = _two_source_rows(yca_ref, ycb_ref, n_first_c)
    y = jnp.dot(jnp.concatenate([yab_ref[...], yc], axis=1), wo_ref[...], preferred_element_type=F32)
    x = _two_source_rows(xa_ref, xb_ref, n_first) + g1_ref[...] * y
    xo_ref[...] = x
    h2 = _rms_mod(x, g2n_ref[...], sc2_ref[...], sh2_ref[...])
    hi = h2.astype(BF16)
    _pack_rows(h2, h2p_ref)
    lo = (h2 - hi.astype(F32)).astype(BF16)
    wh = wrh_ref[...]
    z = (lax.dot_general(wh, hi, _DN_T, preferred_element_type=F32)
         + lax.dot_general(wh, lo, _DN_T, preferred_element_type=F32)
         + lax.dot_general(wrl_ref[...], hi, _DN_T, preferred_element_type=F32))
    scores = jax.nn.sigmoid(z)
    work = scores + br_ref[...]
    eio = lax.broadcasted_iota(jnp.int32, work.shape, 0)
    idxs, sels, hits = [], [], []
    for _ in range(TOP_K):
        m = jnp.max(work, axis=0, keepdims=True)
        idx = jnp.min(jnp.where(work == m, eio, N_EXPERTS), axis=0, keepdims=True)
        hit = eio == idx
        sels.append(jnp.sum(jnp.where(hit, scores, 0.0), axis=0, keepdims=True))
        idxs.append(idx)
        hits.append(hit)
        work = jnp.where(hit, -jnp.inf, work)
    sel = jnp.concatenate(sels, axis=0)
    idx8 = jnp.concatenate(idxs, axis=0)
    gate8 = sel / jnp.sum(sel, axis=0, keepdims=True) * ROUTED_SCALE

    chosen = functools.reduce(jnp.logical_or, hits)
    before = (lax.broadcasted_iota(jnp.int32, (tm, tm), 0) < lax.broadcasted_iota(jnp.int32, (tm, tm), 1))
    prefix = jnp.dot(jnp.where(chosen, 1.0, 0.0).astype(BF16), jnp.where(before, 1.0, 0.0).astype(BF16),
                     preferred_element_type=F32)
    rank_dense = prefix + run_ref[:, 0:1]
    rank8 = jnp.concatenate([jnp.sum(jnp.where(h, rank_dense, 0.0), axis=0, keepdims=True) for h in hits],
                            axis=0).astype(jnp.int32)
    run = run_ref[...] + jnp.sum(jnp.where(chosen, 1.0, 0.0), axis=1, keepdims=True)
    run_ref[...] = run
    cnt_ref[...] = run
    for c in range(tm // 128):
        idx_ref[c] = idx8[:, c * 128:(c + 1) * 128]
        gate_ref[c] = gate8[:, c * 128:(c + 1) * 128]
        rank_ref[c] = rank8[:, c * 128:(c + 1) * 128]


def _out_router(x_first, x_second, yab, yc_first, yc_second, wo_bf, g2n, mod3, wr_hi, wr_lo, br, n_rows):
    tm = TM_OUT
    n_first = x_first.shape[0] // tm
    n_first_c = yc_first.shape[0] // tm
    const2 = lambda i: (0, 0)
    row = lambda i: (i, 0)
    chunk3 = pl.BlockSpec((tm // 128, TOP_K, 128), lambda i: (i, 0, 0))
    nch = n_rows // 128
    return pl.pallas_call(
        functools.partial(_out_router_kernel, n_first=n_first, n_first_c=n_first_c),
        out_shape=[jax.ShapeDtypeStruct((n_rows, D), F32),
                   jax.ShapeDtypeStruct((n_rows * 4, 128), jnp.uint32),
                   jax.ShapeDtypeStruct((nch, TOP_K, 128), jnp.int32),
                   jax.ShapeDtypeStruct((nch, TOP_K, 128), F32),
                   jax.ShapeDtypeStruct((nch, TOP_K, 128), jnp.int32),
                   jax.ShapeDtypeStruct((N_EXPERTS, 128), F32)],
        grid=(n_rows // tm,),
        in_specs=_two_source_specs(tm, n_first)
                 + [pl.BlockSpec((tm, 512), row)]
                 + _two_source_specs(tm, n_first_c, C_WIDTH)
                 + [pl.BlockSpec((D, D), const2),
                  _mod_spec(tm, 2),
                  pl.BlockSpec((1, D), const2), _mod_spec(tm, 3), _mod_spec(tm, 4),
                  pl.BlockSpec((N_EXPERTS, D), const2), pl.BlockSpec((N_EXPERTS, D), const2),
                  pl.BlockSpec((N_EXPERTS, 1), const2)],
        out_specs=[pl.BlockSpec((tm, D), row), pl.BlockSpec((tm * 4, 128), row),
                   chunk3, chunk3, chunk3, pl.BlockSpec((N_EXPERTS, 128), const2)],
        scratch_shapes=[pltpu.VMEM((N_EXPERTS, 128), F32)],
        compiler_params=_cparams(("arbitrary",), 48),
        name="out_router",
    )(x_first, x_second, yab, yc_first, yc_second, wo_bf, mod3, g2n.reshape(1, D), mod3, mod3, wr_hi, wr_lo,
      br.reshape(N_EXPERTS, 1))


def _experts_kernel(te_ref, nx_ref, ws_ref, tr_ref, nu_ref, x_hbm, wg_hbm, wu_hbm, wd_hbm, y_ref,
                    wgu_s, wd_s, x_ring, x_sems, wg_f, wu_f, wd_f, w_sems, *, layer):
    i = pl.program_id(0)
    n_used = nu_ref[0]
    used = i < n_used
    new_expert = jnp.logical_or(i == 0, te_ref[i] != te_ref[jnp.maximum(i - 1, 0)])

    def w_copies(e, slot):
        return [pltpu.make_async_copy(wg_hbm.at[layer, e], wg_f.at[slot], w_sems.at[slot, 0]),
                pltpu.make_async_copy(wu_hbm.at[layer, e], wu_f.at[slot], w_sems.at[slot, 1]),
                pltpu.make_async_copy(wd_hbm.at[layer, e], wd_f.at[slot], w_sems.at[slot, 2])]

    @pl.when(i == 0)
    def _():
        for cp in w_copies(te_ref[0], ws_ref[0]):
            cp.start()

    ahead = X_RING - 1

    def x_copy(t):
        slot = t % X_RING
        return pltpu.make_async_copy(x_hbm.at[pl.ds(t * (4 * TM_X), 4 * TM_X)], x_ring.at[slot], x_sems.at[slot])

    @pl.when(i == 0)
    def _():
        for t in range(ahead):
            @pl.when(t < n_used)
            def _():
                x_copy(t).start()

    @pl.when(i + ahead < n_used)
    def _():
        x_copy(i + ahead).start()

    @pl.when(jnp.logical_and(used, new_expert))
    def _():
        slot = ws_ref[i]
        for cp in w_copies(te_ref[i], slot):
            cp.wait()

        @pl.when(nx_ref[i] >= 0)
        def _():
            for cp in w_copies(nx_ref[i], 1 - slot):
                cp.start()

        wgu_s[:, 0:D_EXPERT] = wg_f[slot].astype(BF16)
        wgu_s[:, D_EXPERT:] = wu_f[slot].astype(BF16)
        wd_s[...] = wd_f[slot].astype(BF16)

    def first_dot(x_ref, row0, rows):
        x_lo, x_hi = _unpack_rows(x_ref, rows, row0=row0)
        x = jnp.concatenate([x_lo.astype(BF16), x_hi.astype(BF16)], axis=1)
        return jnp.dot(x, wgu_s[...], preferred_element_type=F32)

    def second_dot(ab, row0):
        a = ab[:, :D_EXPERT]
        hid = (a * jax.nn.sigmoid(a) * ab[:, D_EXPERT:]).astype(BF16)
        _pack_rows(jnp.dot(hid, wd_s[...], preferred_element_type=F32), y_ref, row0=row0)

    @pl.when(used)
    def _():
        x_copy(i).wait()

    valid = tr_ref[i]

    @pl.when(jnp.logical_and(used, valid > X_TAIL_ROWS))
    def _():
        x_ref = x_ring.at[i % X_RING]
        rc = TM_X // X_CHAINS
        abs_ = [first_dot(x_ref, c * rc, rc) for c in range(X_CHAINS)]
        for c in range(X_CHAINS):
            second_dot(abs_[c], c * rc)

    for c in range(X_TAIL_ROWS // X_TAIL):
        @pl.when(jnp.logical_and(used, jnp.logical_and(valid <= X_TAIL_ROWS, c * X_TAIL < valid)))
        def _():
            second_dot(first_dot(x_ring.at[i % X_RING], c * X_TAIL, X_TAIL), c * X_TAIL)


def _experts(layer, sched, x_sorted, w_gate, w_up, w_down):
    tm = TM_X
    r_pad = x_sorted.shape[0] // 4
    nt = r_pad // tm
    row = lambda i, te, nx, ws, tr, nu: (jnp.minimum(i, nu[0] - 1), 0)
    hbm = pl.BlockSpec(memory_space=pl.ANY)
    grid_spec = pltpu.PrefetchScalarGridSpec(
        num_scalar_prefetch=5,
        grid=(nt,),
        in_specs=[hbm, hbm, hbm, hbm],
        out_specs=pl.BlockSpec((tm * 4, 128), row),
        scratch_shapes=[pltpu.VMEM((D, 2 * D_EXPERT), BF16), pltpu.VMEM((D_EXPERT, D), BF16),
                        pltpu.VMEM((X_RING, tm * 4, 128), jnp.uint32), pltpu.SemaphoreType.DMA((X_RING,)),
                        pltpu.VMEM((2, D, D_EXPERT), F32), pltpu.VMEM((2, D, D_EXPERT), F32),
                        pltpu.VMEM((2, D_EXPERT, D), F32), pltpu.SemaphoreType.DMA((2, 3))],
    )
    return pl.pallas_call(
        functools.partial(_experts_kernel, layer=layer),
        out_shape=jax.ShapeDtypeStruct((r_pad * 4, 128), jnp.uint32),
        grid_spec=grid_spec,
        compiler_params=_cparams(("arbitrary",), 48),
        name="experts",
    )(*sched, x_sorted, w_gate, w_up, w_down)


SC_CORES = 2
SC_SUBCORES = 16
SC_WORKERS = SC_CORES * SC_SUBCORES
SC_CHUNK = 128


def _sc_mesh():
    return plsc.VectorSubcoreMesh(core_axis_name="c", subcore_axis_name="s")


def _sc_params():
    return pltpu.CompilerParams(use_tc_tiling_on_sc=True)


def _sc_dispatch(h2p, pos3, r_pad):
    nch = pos3.shape[0]
    steps = -(-nch // SC_WORKERS)

    def body(h_hbm, pos_hbm, out_hbm, idx_v, rows_v, sem):
        wid = lax.axis_index("s") * SC_CORES + lax.axis_index("c")

        @pl.loop(0, steps)
        def _(s):
            ch = wid + s * SC_WORKERS

            @pl.when(ch < nch)
            def _():
                pltpu.sync_copy(pos_hbm.at[ch], idx_v)
                pltpu.sync_copy(h_hbm.at[pl.ds(ch * SC_CHUNK, SC_CHUNK)], rows_v)
                copies = [pltpu.async_copy(rows_v, out_hbm.at[idx_v.at[k]], sem) for k in range(TOP_K)]
                for cp in copies:
                    cp.wait()

    return pl.kernel(
        body,
        out_type=jax.ShapeDtypeStruct((r_pad, 4, 128), jnp.uint32),
        mesh=_sc_mesh(),
        scratch_types=[pltpu.VMEM((TOP_K, SC_CHUNK), jnp.int32),
                       pltpu.VMEM((SC_CHUNK, 4, 128), jnp.uint32),
                       pltpu.SemaphoreType.DMA],
        compiler_params=_sc_params(),
        name="sc_dispatch",
    )(h2p, pos3)


def _sc_collect(y_sorted, pos3):
    nch = pos3.shape[0]
    steps = -(-nch // SC_WORKERS)
    half = SC_CHUNK // 2
    units = [(k, hh) for k in range(TOP_K) for hh in range(2)]

    def body(y_hbm, pos_hbm, out_hbm, idx_v, rows_a, rows_b, sem_a, sem_b):
        wid = lax.axis_index("s") * SC_CORES + lax.axis_index("c")
        bufs = (rows_a, rows_b)
        sems = (sem_a, sem_b)

        def gather(u):
            k, hh = units[u]
            return pltpu.async_copy(y_hbm.at[idx_v.at[k, pl.ds(hh * half, half)]], bufs[u % 2], sems[u % 2])

        @pl.loop(0, steps)
        def _(s):
            ch = wid + s * SC_WORKERS

            @pl.when(ch < nch)
            def _():
                pltpu.sync_copy(pos_hbm.at[ch], idx_v)
                pending = gather(0)
                for u, (k, hh) in enumerate(units):
                    nxt = gather(u + 1) if u + 1 < len(units) else None
                    pending.wait()
                    pltpu.sync_copy(bufs[u % 2], out_hbm.at[k, pl.ds(ch * SC_CHUNK + hh * half, half)])
                    pending = nxt

    return pl.kernel(
        body,
        out_type=jax.ShapeDtypeStruct((TOP_K, nch * SC_CHUNK, 4, 128), jnp.uint32),
        mesh=_sc_mesh(),
        scratch_types=[pltpu.VMEM((TOP_K, SC_CHUNK), jnp.int32),
                       pltpu.VMEM((half, 4, 128), jnp.uint32),
                       pltpu.VMEM((half, 4, 128), jnp.uint32),
                       pltpu.SemaphoreType.DMA, pltpu.SemaphoreType.DMA],
        compiler_params=_sc_params(),
        name="sc_collect",
    )(y_sorted, pos3)


def _shared_kernel(x_ref, h2p_ref, yg_ref, gate_ref, wg_ref, wu_ref, wd_ref, g2_ref, o_ref):
    h_lo, h_hi = _unpack_rows(h2p_ref, TM_F)
    h = jnp.concatenate([h_lo.astype(BF16), h_hi.astype(BF16)], axis=1)
    a = jnp.dot(h, wg_ref[...], preferred_element_type=F32)
    b = jnp.dot(h, wu_ref[...], preferred_element_type=F32)
    hid = (a * jax.nn.sigmoid(a) * b).astype(BF16)
    f = jnp.dot(hid, wd_ref[...], preferred_element_type=F32)
    gate = gate_ref[...]
    f_lo = f[:, :D // 2]
    f_hi = f[:, D // 2:]
    for k in range(TOP_K):
        y_lo, y_hi = _unpack_rows(yg_ref, TM_F, lead=k)
        f_lo = f_lo + gate[:, k:k + 1] * y_lo
        f_hi = f_hi + gate[:, k:k + 1] * y_hi
    o_ref[...] = x_ref[...] + g2_ref[...] * jnp.concatenate([f_lo, f_hi], axis=1)


def _shared_residual(xa, h2p, yg, gates, wsg_bf, wsu_bf, wsd_bf, mod3, n_rows):
    tm = TM_F
    row = lambda i: (i, 0)
    const2 = lambda i: (0, 0)
    return pl.pallas_call(
        _shared_kernel,
        out_shape=jax.ShapeDtypeStruct((n_rows, D), F32),
        grid=(n_rows // tm,),
        in_specs=[pl.BlockSpec((tm, D), row), pl.BlockSpec((tm * 4, 128), row),
                  pl.BlockSpec((TOP_K, tm * 4, 128), lambda i: (0, i, 0)),
                  pl.BlockSpec((tm, TOP_K), row),
                  pl.BlockSpec((D, D_SHARED), const2), pl.BlockSpec((D, D_SHARED), const2),
                  pl.BlockSpec((D_SHARED, D), const2), _mod_spec(tm, 5)],
        out_specs=pl.BlockSpec((tm, D), row),
        compiler_params=_cparams(("parallel",), 48),
        name="shared_residual",
    )(xa, h2p, yg, gates, wsg_bf, wsu_bf, wsd_bf, mod3)


def _positions_kernel(offs_ref, idx_ref, rank_ref, pos_ref):
    idx = idx_ref[...]
    base = jnp.zeros_like(idx)
    for e in range(N_EXPERTS):
        base = jnp.where(idx == e, offs_ref[e], base)
    pos_ref[...] = rank_ref[...] + base


def _positions(offs, idx3, rank3):
    nch = idx3.shape[0]
    cb = nch // 2
    spec = pl.BlockSpec((cb, TOP_K, 128), lambda i, offs: (i, 0, 0))
    return pl.pallas_call(
        _positions_kernel,
        out_shape=jax.ShapeDtypeStruct((nch, TOP_K, 128), jnp.int32),
        grid_spec=pltpu.PrefetchScalarGridSpec(num_scalar_prefetch=1, grid=(nch // cb,),
                                               in_specs=[spec, spec], out_specs=spec),
        compiler_params=_cparams(("parallel",)),
        name="positions",
    )(offs, idx3, rank3)


def _route_positions(idx3, rank3, counts, n_rows):
    tm = TM_X
    counts = counts.astype(jnp.int32)
    padded = ((counts + tm - 1) // tm) * tm
    ends = jnp.cumsum(padded)
    offs = ends - padded
    pos3 = _positions(offs.astype(jnp.int32), idx3, rank3)
    r_pad = n_rows * TOP_K + N_EXPERTS * tm
    nt = r_pad // tm
    tile_ids = jnp.arange(nt, dtype=jnp.int32)
    tile_expert = jnp.sum((ends // tm)[None, :] <= tile_ids[:, None], axis=1)
    tile_expert = jnp.minimum(tile_expert, N_EXPERTS - 1).astype(jnp.int32)
    n_used = (ends[-1] // tm).astype(jnp.int32).reshape(1)
    e_ids = jnp.arange(N_EXPERTS, dtype=jnp.int32)
    nonempty = counts > 0
    later = jnp.where(nonempty[None, :] & (e_ids[None, :] > e_ids[:, None]), e_ids[None, :], N_EXPERTS)
    next_e = jnp.min(later, axis=1)
    next_e = jnp.where(next_e == N_EXPERTS, -1, next_e).astype(jnp.int32)
    slot_e = ((jnp.cumsum(nonempty.astype(jnp.int32)) - nonempty.astype(jnp.int32)) % 2).astype(jnp.int32)
    tile_rows = jnp.clip(counts[tile_expert] - (tile_ids - (offs // tm)[tile_expert]) * tm, 0, tm)
    tile_rows = jnp.where(tile_ids < n_used[0], tile_rows, 0).astype(jnp.int32)
    sched = (tile_expert, next_e[tile_expert], slot_e[tile_expert], tile_rows, n_used)
    return pos3, sched, r_pad


def _in_weights(w):
    def regroup(cols):
        return cols.reshape(D, 2, HEADS, QK_DIM).transpose(0, 2, 1, 3).reshape(D, 2 * HEADS * QK_DIM)
    return jnp.concatenate([w[:, :OFF_Q], regroup(w[:, OFF_Q:OFF_K]), regroup(w[:, OFF_K:OFF_V]), w[:, OFF_V:]],
                           axis=1).astype(BF16)


def _rope_tables():
    t = jnp.arange(L)
    row = (t // GRID_W).astype(F32)
    col = (t % GRID_W).astype(F32)
    n_freq = QK_DIM // 4
    inv = ROPE_BASE ** (-jnp.arange(n_freq, dtype=F32) / n_freq)
    ar = row[:, None] * inv
    ac = col[:, None] * inv
    cos64 = jnp.concatenate([jnp.cos(ar), jnp.cos(ar), jnp.cos(ac), jnp.cos(ac)], axis=1)
    sin64 = jnp.concatenate([-jnp.sin(ar), jnp.sin(ar), -jnp.sin(ac), jnp.sin(ac)], axis=1)
    cos_t = jnp.concatenate([jnp.tile(cos64, (1, 2)), jnp.ones((TM_IN, 128), F32)], axis=0)
    sin_t = jnp.concatenate([jnp.tile(sin64, (1, 2)), jnp.zeros((TM_IN, 128), F32)], axis=0)
    return cos_t, sin_t


def _split_bf16(w):
    hi = w.astype(BF16)
    return hi, (w - hi.astype(F32)).astype(BF16)


def kernel(x, c, ctx, c_ctx, w_ada, b_ada, g_norm1, g_norm2, w_in, w_out, g_v, w_s, b_s, w_conv, g_q, g_k,
           lam_q1, lam_k1, lam_q2, lam_k2, g_sub, w_router, b_router, w_gate, w_up, w_down,
           ws_gate, ws_up, ws_down):
    src = (x.reshape(NL, D), ctx.reshape(NC, D))
    cc = jnp.concatenate([c, c_ctx[None, :], jnp.zeros((MOD_ROWS - B - 1, D), F32)], axis=0)
    mod = _ada(cc, w_ada, b_ada)
    cos_t, sin_t = _rope_tables()
    bd = jnp.asarray(np.kron(np.eye(RMS_BLOCK // QK_DIM, dtype=np.float32),
                             np.full((QK_DIM, QK_DIM), 1.0 / QK_DIM, np.float32)), BF16)

    for l in range(DEPTH):
        last = l == DEPTH - 1
        lam_init = 0.8 - 0.6 * math.exp(-0.3 * l)
        lam = (jnp.exp(jnp.sum(lam_q1[l] * lam_k1[l])) - jnp.exp(jnp.sum(lam_q2[l] * lam_k2[l])) + lam_init)
        bound = (QK_DIM * jnp.max(jnp.abs(g_q[l])) * jnp.max(jnp.abs(g_k[l]))
                 * (QK_DIM ** -0.5 * LOG2E) * ATT_BOUND_MARGIN)
        use_bound = (2.0 * bound < ATT_MAX_SHIFT_RANGE).astype(F32)
        lam = jnp.stack([lam, bound, use_bound]).astype(F32)
        mod3 = mod[l].reshape(MOD_ROWS, 1, 6 * D)
        w_in_bf = _in_weights(w_in[l])
        bias_t = jnp.repeat(b_s[l].T, A_GD, axis=1)
        wconv = jnp.concatenate([w_conv[l], jnp.zeros((5, B_WIDTH), F32)], axis=0)
        yab, q, kk, v = _in_mix(src[0], src[1], g_norm1[l], mod3, w_in_bf, cos_t, sin_t,
                                     g_v[l].reshape(1, A_WIDTH), w_s[l].astype(BF16), bias_t, wconv,
                                     jnp.tile(g_q[l], 8).reshape(1, 512), jnp.tile(g_k[l], 8).reshape(1, 512), bd)
        gsub = g_sub[l].reshape(1, V_DIM)
        coef = 1.0 - lam_init
        n_rows = NL if last else NR
        yc = _attention(lam, q, kk, v, gsub, coef, ctx_queries=False)
        yc_ctx = yc if last else _attention(lam, q, kk, v, gsub, coef, ctx_queries=True)
        wr_hi, wr_lo = _split_bf16(w_router[l].T)
        wo_bf = w_out[l].astype(BF16)
        ws_bf = (ws_gate[l].astype(BF16), ws_up[l].astype(BF16), ws_down[l].astype(BF16))
        xa, h2p, idx3, gate3, rank3, counts = _out_router(
            src[0], src[1], yab, yc, yc_ctx, wo_bf, g_norm2[l], mod3, wr_hi, wr_lo, b_router[l], n_rows)
        pos3, sched, r_pad = _route_positions(idx3, rank3, counts[:, 0], n_rows)
        x_sorted = _sc_dispatch(h2p.reshape(n_rows, 4, 128), pos3, r_pad)
        y_sorted = _experts(l, sched, x_sorted.reshape(r_pad * 4, 128), w_gate, w_up, w_down)
        yg = _sc_collect(y_sorted.reshape(r_pad, 4, 128), pos3)
        gates = gate3.transpose(0, 2, 1).reshape(n_rows, TOP_K)
        xa = _shared_residual(xa, h2p, yg.reshape(TOP_K, n_rows * 4, 128), gates, *ws_bf, mod3, n_rows)
        src = (xa, xa)
    return xa.reshape(B, L, D)
```

```python
import functools
import math

import numpy as np
import jax
import jax.numpy as jnp
from jax import lax
from jax.experimental import pallas as pl
from jax.experimental.pallas import tpu as pltpu
from jax.experimental.pallas import tpu_sc as plsc

F32 = jnp.float32
BF16 = jnp.bfloat16

D = 1024
B = 8
L = 2048
DEPTH = 2
GRID_W = 64
CTX = 256
A_WIDTH = 256
A_GROUPS = 4
A_GD = 64
CHUNK = 128
B_WIDTH = 256
C_WIDTH = 512
HEADS = 4
V_DIM = 128
QK_DIM = 64
ROPE_BASE = 10000.0
OFF_BB = 512
OFF_BC = 768
OFF_BX = 1024
OFF_Q = 1280
OFF_K = 1792
OFF_V = 2304
D_IN = 2816
N_EXPERTS = 64
TOP_K = 8
D_EXPERT = 256
D_SHARED = 256
ROUTED_SCALE = 2.5
EPS = 1e-6

NL = B * L
NC = B * CTX
NR = NL + NC
MOD_ROWS = 16
LOG2E = 1.4426950408889634

TM_IN = 512
RMS_BLOCK = 256
TQ = 2048
ATT_CHAIN = 256
ATT_GROUP = 4
ATT_BOUND_MARGIN = 1.02
ATT_MAX_SHIFT_RANGE = 100.0
TM_OUT = 512
TM_X = 1024
X_CHAINS = 2
X_TAIL_ROWS = 512
X_TAIL = 256
X_RING = 3
TM_F = 512

_DN_T = (((1,), (1,)), ((), ()))


def _cparams(sem, vmem_mb=None):
    kw = dict(dimension_semantics=sem)
    if vmem_mb is not None:
        kw["vmem_limit_bytes"] = vmem_mb * 1024 * 1024
    return pltpu.CompilerParams(**kw)


def _mod_row(i, tm):
    return jnp.where(i < NL // tm, i // (L // tm), B)


def _mod_spec(tm, chunk):
    return pl.BlockSpec((None, 1, D), lambda i: (_mod_row(i, tm), 0, chunk))


def _ada_kernel(c_ref, w_ref, b_ref, o_ref):
    c = c_ref[...]
    cs = c * jax.nn.sigmoid(c)
    o_ref[...] = jnp.dot(cs, w_ref[...], preferred_element_type=F32,
                         precision=lax.Precision.HIGHEST) + b_ref[...]


def _ada(cc, w_ada, b_ada):
    nb = 6
    return pl.pallas_call(
        _ada_kernel,
        out_shape=jax.ShapeDtypeStruct((DEPTH, MOD_ROWS, 6 * D), F32),
        grid=(DEPTH, nb),
        in_specs=[pl.BlockSpec((MOD_ROWS, D), lambda l, j: (0, 0)),
                  pl.BlockSpec((None, D, D), lambda l, j: (l, 0, j)),
                  pl.BlockSpec((None, 1, D), lambda l, j: (l, 0, j))],
        out_specs=pl.BlockSpec((None, MOD_ROWS, D), lambda l, j: (l, 0, j)),
        compiler_params=_cparams(("arbitrary", "arbitrary"), 40),
        name="ada_mod",
    )(cc, w_ada, b_ada.reshape(DEPTH, 1, 6 * D))


def _rms_mod(x, g, sc, sh):
    ms = jnp.mean(x * x, axis=-1, keepdims=True)
    return x * lax.rsqrt(ms + EPS) * (g * (1.0 + sc)) + sh


def _two_source_specs(tm, n_first, width=D):
    return [pl.BlockSpec((tm, width), lambda i: (jnp.minimum(i, n_first - 1), 0)),
            pl.BlockSpec((tm, width), lambda i: (jnp.maximum(i - n_first, 0), 0))]


def _two_source_rows(a_ref, b_ref, n_first):
    return jnp.where(pl.program_id(0) < n_first, a_ref[...], b_ref[...])


def _group_rms(t, g, bd):
    sq = (t * t).astype(BF16)
    ms = jnp.concatenate([jnp.dot(sq[:, c:c + RMS_BLOCK], bd, preferred_element_type=F32)
                          for c in range(0, t.shape[1], RMS_BLOCK)], axis=1)
    return t * lax.rsqrt(ms + EPS) * g


def _rope(t, cos, sin):
    w = t.shape[1]
    lane = lax.broadcasted_iota(jnp.int32, t.shape, 1)
    first = (lane % 32) < 16
    partner = jnp.where(first, pltpu.roll(t, w - 16, 1), pltpu.roll(t, 16, 1))
    cos4 = jnp.concatenate([cos] * (w // 128), axis=1)
    sin4 = jnp.concatenate([sin] * (w // 128), axis=1)
    return t * cos4 + partner * sin4


def _in_mix_kernel(xa_ref, xb_ref, pa_ref, pb_ref, na_ref, nb_ref, g_ref, sh_ref, sc_ref, w_ref,
                   cos_ref, sin_ref, gv_ref, ws_ref, bias_ref, wconv_ref, gq_ref, gk_ref, bd_ref,
                   yab_ref, q_ref, kk_ref, v_ref, *, n_first):
    tm = TM_IN
    i = pl.program_id(0)
    tiles_per_seq = L // tm
    is_lat = i < NL // tm
    is_start = jnp.logical_or(jnp.logical_not(is_lat), i % tiles_per_seq == 0)
    is_end = jnp.logical_or(jnp.logical_not(is_lat), i % tiles_per_seq == tiles_per_seq - 1)
    first = i < n_first
    g, sc, sh = g_ref[...], sc_ref[...], sh_ref[...]

    h = _rms_mod(jnp.where(first, xa_ref[...], xb_ref[...]), g, sc, sh)
    p = jnp.dot(h.astype(BF16), w_ref[...], preferred_element_type=F32)
    v_ref[...] = p[:, OFF_V:].astype(BF16)
    halo = jnp.concatenate([jnp.where(first, pa_ref[...], pb_ref[...]),
                            jnp.where(first, na_ref[...], nb_ref[...])], axis=0)
    ph = jnp.dot(_rms_mod(halo, g, sc, sh).astype(BF16), w_ref[:, OFF_BC:OFF_Q], preferred_element_type=F32)
    zh = ph[:, :B_WIDTH] * ph[:, B_WIDTH:]
    zp = jnp.where(is_start, 0.0, zh[15:16])
    zn = jnp.where(is_end, 0.0, zh[16:17])

    uv = p[:, 0:2 * A_WIDTH]
    uv = 0.5 * uv * (1.0 + lax.erf(uv * (2.0 ** -0.5)))
    u = uv[:, :A_WIDTH]
    v = uv[:, A_WIDTH:]
    ms = jnp.mean(v * v, axis=-1, keepdims=True)
    vb = (v * lax.rsqrt(ms + EPS) * gv_ref[...]).astype(BF16)
    lane = lax.broadcasted_iota(jnp.int32, (CHUNK, 128), 1)
    mixes = []
    for c in range(tm // CHUNK):
        vc = vb[c * CHUNK:(c + 1) * CHUNK]
        halves = []
        for j in range(2):
            vj = vc[:, j * 128:(j + 1) * 128]
            m0 = jnp.dot(ws_ref[2 * j], vj, preferred_element_type=F32)
            m1 = jnp.dot(ws_ref[2 * j + 1], vj, preferred_element_type=F32)
            halves.append(jnp.where(lane < A_GD, m0, m1))
        mixes.append(jnp.concatenate(halves, axis=1) + bias_ref[...])
    ya = u * jnp.concatenate(mixes, axis=0)

    bg = p[:, OFF_BB:OFF_BC]
    z = p[:, OFF_BC:OFF_BX] * p[:, OFF_BX:OFF_Q]
    row = lax.broadcasted_iota(jnp.int32, z.shape, 0)
    inner = jnp.logical_not(is_lat)
    z_prev = jnp.where(row == 0, zp, pltpu.roll(z, 1, 0))
    z_prev = jnp.where(jnp.logical_and(inner, row % CTX == 0), 0.0, z_prev)
    z_next = jnp.where(row == tm - 1, zn, pltpu.roll(z, tm - 1, 0))
    z_next = jnp.where(jnp.logical_and(inner, row % CTX == CTX - 1), 0.0, z_next)
    yb = bg * (z_prev * wconv_ref[0:1] + z * wconv_ref[1:2] + z_next * wconv_ref[2:3])
    yab_ref[...] = jnp.concatenate([ya, yb], axis=1).astype(BF16)

    cos = cos_ref[...]
    sin = sin_ref[...]
    bd = bd_ref[...]
    q = _rope(_group_rms(p[:, OFF_Q:OFF_K], gq_ref[...], bd), cos, sin)
    q = q * (QK_DIM ** -0.5 * LOG2E)
    q_ref[...] = q.astype(BF16)
    k = _rope(_group_rms(p[:, OFF_K:OFF_V], gk_ref[...], bd), cos, sin)
    kk_ref[...] = k.astype(BF16)


def _in_mix(x_first, x_second, g, mod3, w_bf, cos_t, sin_t, gv, ws_bf, bias_t, wconv, gq, gk, bd):
    tm = TM_IN
    n_first = x_first.shape[0] // tm
    hb = tm // 16
    nhb_first = x_first.shape[0] // 16
    nhb_second = x_second.shape[0] // 16
    pos_blocks = L // tm

    def tab_map(i):
        return (jnp.where(i < NL // tm, i % pos_blocks, pos_blocks), 0)

    def halo_specs(shift):
        blk = lambda i: (i * tm + shift) // 16
        return [pl.BlockSpec((16, D), lambda i: (jnp.clip(blk(i), 0, nhb_first - 1), 0)),
                pl.BlockSpec((16, D), lambda i: (jnp.clip(blk(i) - nhb_first, 0, nhb_second - 1), 0))]

    const2 = lambda i: (0, 0)
    row512 = pl.BlockSpec((tm, 512), lambda i: (i, 0))
    return pl.pallas_call(
        functools.partial(_in_mix_kernel, n_first=n_first),
        out_shape=[jax.ShapeDtypeStruct((NR, 512), BF16)] * 4,
        grid=(NR // tm,),
        in_specs=_two_source_specs(tm, n_first) + halo_specs(-1) + halo_specs(tm)
                 + [pl.BlockSpec((1, D), const2), _mod_spec(tm, 0), _mod_spec(tm, 1),
                    pl.BlockSpec((D, D_IN), const2),
                    pl.BlockSpec((tm, 128), tab_map), pl.BlockSpec((tm, 128), tab_map),
                    pl.BlockSpec((1, A_WIDTH), const2),
                    pl.BlockSpec((A_GROUPS, CHUNK, CHUNK), lambda i: (0, 0, 0)),
                    pl.BlockSpec((CHUNK, A_WIDTH), const2),
                    pl.BlockSpec((8, B_WIDTH), const2),
                    pl.BlockSpec((1, 512), const2), pl.BlockSpec((1, 512), const2),
                    pl.BlockSpec((RMS_BLOCK, RMS_BLOCK), const2)],
        out_specs=[row512] * 4,
        compiler_params=_cparams(("parallel",), 56),
        name="in_mix",
    )(x_first, x_second, x_first, x_second, x_first, x_second, g.reshape(1, D), mod3, mod3, w_bf,
      cos_t, sin_t, gv, ws_bf, bias_t, wconv, gq, gk, bd)


def _attn_kernel(lam_ref, q_ref, *rest, n_seg, coef, tq):
    kv_refs = rest[:2 * n_seg]
    gsub_ref, o_ref, k_scr, vt_scr = rest[2 * n_seg:]

    @pl.when(pl.program_id(2) == 0)
    def _():
        off = 0
        for s in range(n_seg):
            n = kv_refs[s].shape[0]
            k_scr[off:off + n, :] = kv_refs[s][...]
            vt_scr[0:V_DIM, off:off + n] = kv_refs[n_seg + s][...].astype(F32).T.astype(BF16)
            off += n
        ones_row = lax.broadcasted_iota(jnp.int32, (16, off), 0) == 0
        vt_scr[V_DIM:, :] = jnp.where(ones_row, 1.0, 0.0).astype(BF16)

    lam = lam_ref[0]
    shift = lam_ref[1]
    qc = ATT_CHAIN

    def scores(c):
        rows = slice(c * qc, (c + 1) * qc)
        q = q_ref[rows, :]
        lane = lax.broadcasted_iota(jnp.int32, q.shape, 1)
        zero = jnp.zeros_like(q)
        qs = jnp.concatenate([jnp.where(lane < QK_DIM, q, zero), jnp.where(lane >= QK_DIM, q, zero)], axis=0)
        return lax.dot_general(k_scr[...], qs, _DN_T, preferred_element_type=F32)

    def finish(c, pt):
        ot = jnp.dot(vt_scr[...], pt, preferred_element_type=F32)
        inv = 1.0 / ot[V_DIM:V_DIM + 1, :]
        dt = ot[0:V_DIM, :qc] * inv[:, :qc] - ot[0:V_DIM, qc:] * (lam * inv[:, qc:])
        o = dt.T
        ms = jnp.mean(o * o, axis=-1, keepdims=True)
        o_ref[c * qc:(c + 1) * qc, :] = (o * lax.rsqrt(ms + EPS) * gsub_ref[...] * coef).astype(o_ref.dtype)

    @pl.when(lam_ref[2] > 0.5)
    def _():
        for c in range(tq // qc):
            finish(c, jnp.exp2(scores(c) - shift).astype(BF16))

    @pl.when(lam_ref[2] <= 0.5)
    def _():
        for g0 in range(0, tq // qc, ATT_GROUP):
            group = range(g0, min(g0 + ATT_GROUP, tq // qc))
            sts = [scores(c) for c in group]
            for c, st in zip(group, sts):
                finish(c, jnp.exp2(st - jnp.max(st, axis=0, keepdims=True)).astype(BF16))


def _attention(lam, q, kk, v, gsub, coef, *, ctx_queries):
    if ctx_queries:
        tq = CTX
        nq, lk, n_seg = 1, CTX, 1
        q_map = lambda b, h, qi: (NL // tq + b, h)
        kv_specs = [pl.BlockSpec((CTX, 128), lambda b, h, qi: (NL // CTX + b, h)),
                    pl.BlockSpec((CTX, 128), lambda b, h, qi: (NL // CTX + b, h))]
        kv_args = [kk, v]
        rows = NC
        o_map = lambda b, h, qi: (b, h)
    else:
        tq = TQ
        nq, lk, n_seg = L // tq, CTX + L, 2
        q_map = lambda b, h, qi: (b * (L // tq) + qi, h)
        kv_specs = [pl.BlockSpec((CTX, 128), lambda b, h, qi: (NL // CTX + b, h)),
                    pl.BlockSpec((L, 128), lambda b, h, qi: (b, h)),
                    pl.BlockSpec((CTX, 128), lambda b, h, qi: (NL // CTX + b, h)),
                    pl.BlockSpec((L, 128), lambda b, h, qi: (b, h))]
        kv_args = [kk, kk, v, v]
        rows = NL
        o_map = lambda b, h, qi: (b * (L // tq) + qi, h)
    return pl.pallas_call(
        functools.partial(_attn_kernel, n_seg=n_seg, coef=coef, tq=tq),
        out_shape=jax.ShapeDtypeStruct((rows, C_WIDTH), BF16),
        grid=(B, HEADS, nq),
        in_specs=[pl.BlockSpec(memory_space=pltpu.SMEM),
                  pl.BlockSpec((tq, 128), q_map)]
                 + kv_specs + [pl.BlockSpec((1, V_DIM), lambda b, h, qi: (0, 0))],
        out_specs=pl.BlockSpec((tq, 128), o_map),
        scratch_shapes=[pltpu.VMEM((lk, 128), BF16), pltpu.VMEM((V_DIM + 16, lk), BF16)],
        compiler_params=_cparams(("parallel", "parallel", "arbitrary"), 56),
        name="attn_ctx" if ctx_queries else "attn_lat",
    )(lam, q, *kv_args, gsub)


def _pack_rows(t, out_ref, row0=0):
    half = D // 2
    w = pltpu.pack_elementwise([t[:, :half], t[:, half:]], packed_dtype=BF16)
    w = lax.bitcast_convert_type(w, jnp.uint32)
    rows = t.shape[0]
    for j in range(4):
        out_ref[pl.ds(4 * row0 + j, rows, stride=4), :] = w[:, j * 128:(j + 1) * 128]


def _unpack_rows(ref, rows, lead=None, row0=0):
    los, his = [], []
    for j in range(4):
        sl = pl.ds(4 * row0 + j, rows, stride=4)
        w = ref[sl, :] if lead is None else ref[lead, sl, :]
        los.append(pltpu.unpack_elementwise(w, index=0, packed_dtype=BF16, unpacked_dtype=F32))
        his.append(pltpu.unpack_elementwise(w, index=1, packed_dtype=BF16, unpacked_dtype=F32))
    return jnp.concatenate(los, axis=1), jnp.concatenate(his, axis=1)


def _out_router_kernel(xa_ref, xb_ref, yab_ref, yca_ref, ycb_ref, wo_ref, g1_ref, g2n_ref, sh2_ref, sc2_ref, wrh_ref,
                       wrl_ref, br_ref, xo_ref, h2p_ref, idx_ref, gate_ref, rank_ref, cnt_ref, run_ref, *,
                       n_first, n_first_c):
    tm = TM_OUT

    @pl.when(pl.program_id(0) == 0)
    def _():
        run_ref[...] = jnp.zeros_like(run_ref)

    yc = _two_source_rows(yca_ref, ycb_ref, n_first_c)
    y = jnp.dot(jnp.concatenate([yab_ref[...], yc], axis=1), wo_ref[...], preferred_element_type=F32)
    x = _two_source_rows(xa_ref, xb_ref, n_first) + g1_ref[...] * y
    xo_ref[...] = x
    h2 = _rms_mod(x, g2n_ref[...], sc2_ref[...], sh2_ref[...])
    hi = h2.astype(BF16)
    _pack_rows(h2, h2p_ref)
    lo = (h2 - hi.astype(F32)).astype(BF16)
    wh = wrh_ref[...]
    z = (lax.dot_general(wh, hi, _DN_T, preferred_element_type=F32)
         + lax.dot_general(wh, lo, _DN_T, preferred_element_type=F32)
         + lax.dot_general(wrl_ref[...], hi, _DN_T, preferred_element_type=F32))
    scores = jax.nn.sigmoid(z)
    work = scores + br_ref[...]
    eio = lax.broadcasted_iota(jnp.int32, work.shape, 0)
    idxs, sels, hits = [], [], []
    for _ in range(TOP_K):
        m = jnp.max(work, axis=0, keepdims=True)
        idx = jnp.min(jnp.where(work == m, eio, N_EXPERTS), axis=0, keepdims=True)
        hit = eio == idx
        sels.append(jnp.sum(jnp.where(hit, scores, 0.0), axis=0, keepdims=True))
        idxs.append(idx)
        hits.append(hit)
        work = jnp.where(hit, -jnp.inf, work)
    sel = jnp.concatenate(sels, axis=0)
    idx8 = jnp.concatenate(idxs, axis=0)
    gate8 = sel / jnp.sum(sel, axis=0, keepdims=True) * ROUTED_SCALE

    chosen = functools.reduce(jnp.logical_or, hits)
    before = (lax.broadcasted_iota(jnp.int32, (tm, tm), 0) < lax.broadcasted_iota(jnp.int32, (tm, tm), 1))
    prefix = jnp.dot(jnp.where(chosen, 1.0, 0.0).astype(BF16), jnp.where(before, 1.0, 0.0).astype(BF16),
                     preferred_element_type=F32)
    rank_dense = prefix + run_ref[:, 0:1]
    rank8 = jnp.concatenate([jnp.sum(jnp.where(h, rank_dense, 0.0), axis=0, keepdims=True) for h in hits],
                            axis=0).astype(jnp.int32)
    run = run_ref[...] + jnp.sum(jnp.where(chosen, 1.0, 0.0), axis=1, keepdims=True)
    run_ref[...] = run
    cnt_ref[...] = run
    for c in range(tm // 128):
        idx_ref[c] = idx8[:, c * 128:(c + 1) * 128]
        gate_ref[c] = gate8[:, c * 128:(c + 1) * 128]
        rank_ref[c] = rank8[:, c * 128:(c + 1) * 128]


def _out_router(x_first, x_second, yab, yc_first, yc_second, wo_bf, g2n, mod3, wr_hi, wr_lo, br, n_rows):
    tm = TM_OUT
    n_first = x_first.shape[0] // tm
    n_first_c = yc_first.shape[0] // tm
    const2 = lambda i: (0, 0)
    row = lambda i: (i, 0)
    chunk3 = pl.BlockSpec((tm // 128, TOP_K, 128), lambda i: (i, 0, 0))
    nch = n_rows // 128
    return pl.pallas_call(
        functools.partial(_out_router_kernel, n_first=n_first, n_first_c=n_first_c),
        out_shape=[jax.ShapeDtypeStruct((n_rows, D), F32),
                   jax.ShapeDtypeStruct((n_rows * 4, 128), jnp.uint32),
                   jax.ShapeDtypeStruct((nch, TOP_K, 128), jnp.int32),
                   jax.ShapeDtypeStruct((nch, TOP_K, 128), F32),
                   jax.ShapeDtypeStruct((nch, TOP_K, 128), jnp.int32),
                   jax.ShapeDtypeStruct((N_EXPERTS, 128), F32)],
        grid=(n_rows // tm,),
        in_specs=_two_source_specs(tm, n_first)
                 + [pl.BlockSpec((tm, 512), row)]
                 + _two_source_specs(tm, n_first_c, C_WIDTH)
                 + [pl.BlockSpec((D, D), const2),
                  _mod_spec(tm, 2),
                  pl.BlockSpec((1, D), const2), _mod_spec(tm, 3), _mod_spec(tm, 4),
                  pl.BlockSpec((N_EXPERTS, D), const2), pl.BlockSpec((N_EXPERTS, D), const2),
                  pl.BlockSpec((N_EXPERTS, 1), const2)],
        out_specs=[pl.BlockSpec((tm, D), row), pl.BlockSpec((tm * 4, 128), row),
                   chunk3, chunk3, chunk3, pl.BlockSpec((N_EXPERTS, 128), const2)],
        scratch_shapes=[pltpu.VMEM((N_EXPERTS, 128), F32)],
        compiler_params=_cparams(("arbitrary",), 48),
        name="out_router",
    )(x_first, x_second, yab, yc_first, yc_second, wo_bf, mod3, g2n.reshape(1, D), mod3, mod3, wr_hi, wr_lo,
      br.reshape(N_EXPERTS, 1))


def _experts_kernel(te_ref, nx_ref, ws_ref, tr_ref, nu_ref, x_hbm, wg_hbm, wu_hbm, wd_hbm, y_ref,
                    wgu_s, wd_s, x_ring, x_sems, wg_f, wu_f, wd_f, w_sems, *, layer):
    i = pl.program_id(0)
    n_used = nu_ref[0]
    used = i < n_used
    new_expert = jnp.logical_or(i == 0, te_ref[i] != te_ref[jnp.maximum(i - 1, 0)])

    def w_copies(e, slot):
        return [pltpu.make_async_copy(wg_hbm.at[layer, e], wg_f.at[slot], w_sems.at[slot, 0]),
                pltpu.make_async_copy(wu_hbm.at[layer, e], wu_f.at[slot], w_sems.at[slot, 1]),
                pltpu.make_async_copy(wd_hbm.at[layer, e], wd_f.at[slot], w_sems.at[slot, 2])]

    @pl.when(i == 0)
    def _():
        for cp in w_copies(te_ref[0], ws_ref[0]):
            cp.start()

    ahead = X_RING - 1

    def x_copy(t):
        slot = t % X_RING
        return pltpu.make_async_copy(x_hbm.at[pl.ds(t * (4 * TM_X), 4 * TM_X)], x_ring.at[slot], x_sems.at[slot])

    @pl.when(i == 0)
    def _():
        for t in range(ahead):
            @pl.when(t < n_used)
            def _():
                x_copy(t).start()

    @pl.when(i + ahead < n_used)
    def _():
        x_copy(i + ahead).start()

    @pl.when(jnp.logical_and(used, new_expert))
    def _():
        slot = ws_ref[i]
        for cp in w_copies(te_ref[i], slot):
            cp.wait()

        @pl.when(nx_ref[i] >= 0)
        def _():
            for cp in w_copies(nx_ref[i], 1 - slot):
                cp.start()

        wgu_s[:, 0:D_EXPERT] = wg_f[slot].astype(BF16)
        wgu_s[:, D_EXPERT:] = wu_f[slot].astype(BF16)
        wd_s[...] = wd_f[slot].astype(BF16)

    def first_dot(x_ref, row0, rows):
        x_lo, x_hi = _unpack_rows(x_ref, rows, row0=row0)
        x = jnp.concatenate([x_lo.astype(BF16), x_hi.astype(BF16)], axis=1)
        return jnp.dot(x, wgu_s[...], preferred_element_type=F32)

    def second_dot(ab, row0):
        a = ab[:, :D_EXPERT]
        hid = (a * jax.nn.sigmoid(a) * ab[:, D_EXPERT:]).astype(BF16)
        _pack_rows(jnp.dot(hid, wd_s[...], preferred_element_type=F32), y_ref, row0=row0)

    @pl.when(used)
    def _():
        x_copy(i).wait()

    valid = tr_ref[i]

    @pl.when(jnp.logical_and(used, valid > X_TAIL_ROWS))
    def _():
        x_ref = x_ring.at[i % X_RING]
        rc = TM_X // X_CHAINS
        abs_ = [first_dot(x_ref, c * rc, rc) for c in range(X_CHAINS)]
        for c in range(X_CHAINS):
            second_dot(abs_[c], c * rc)

    for c in range(X_TAIL_ROWS // X_TAIL):
        @pl.when(jnp.logical_and(used, jnp.logical_and(valid <= X_TAIL_ROWS, c * X_TAIL < valid)))
        def _():
            second_dot(first_dot(x_ring.at[i % X_RING], c * X_TAIL, X_TAIL), c * X_TAIL)


def _experts(layer, sched, x_sorted, w_gate, w_up, w_down):
    tm = TM_X
    r_pad = x_sorted.shape[0] // 4
    nt = r_pad // tm
    row = lambda i, te, nx, ws, tr, nu: (jnp.minimum(i, nu[0] - 1), 0)
    hbm = pl.BlockSpec(memory_space=pl.ANY)
    grid_spec = pltpu.PrefetchScalarGridSpec(
        num_scalar_prefetch=5,
        grid=(nt,),
        in_specs=[hbm, hbm, hbm, hbm],
        out_specs=pl.BlockSpec((tm * 4, 128), row),
        scratch_shapes=[pltpu.VMEM((D, 2 * D_EXPERT), BF16), pltpu.VMEM((D_EXPERT, D), BF16),
                        pltpu.VMEM((X_RING, tm * 4, 128), jnp.uint32), pltpu.SemaphoreType.DMA((X_RING,)),
                        pltpu.VMEM((2, D, D_EXPERT), F32), pltpu.VMEM((2, D, D_EXPERT), F32),
                        pltpu.VMEM((2, D_EXPERT, D), F32), pltpu.SemaphoreType.DMA((2, 3))],
    )
    return pl.pallas_call(
        functools.partial(_experts_kernel, layer=layer),
        out_shape=jax.ShapeDtypeStruct((r_pad * 4, 128), jnp.uint32),
        grid_spec=grid_spec,
        compiler_params=_cparams(("arbitrary",), 48),
        name="experts",
    )(*sched, x_sorted, w_gate, w_up, w_down)


SC_CORES = 2
SC_SUBCORES = 16
SC_WORKERS = SC_CORES * SC_SUBCORES
SC_CHUNK = 128


def _sc_mesh():
    return plsc.VectorSubcoreMesh(core_axis_name="c", subcore_axis_name="s")


def _sc_params():
    return pltpu.CompilerParams(use_tc_tiling_on_sc=True)


def _sc_dispatch(h2p, pos3, r_pad):
    nch = pos3.shape[0]
    steps = -(-nch // SC_WORKERS)

    def body(h_hbm, pos_hbm, out_hbm, idx_v, rows_v, sem):
        wid = lax.axis_index("s") * SC_CORES + lax.axis_index("c")

        @pl.loop(0, steps)
        def _(s):
            ch = wid + s * SC_WORKERS

            @pl.when(ch < nch)
            def _():
                pltpu.sync_copy(pos_hbm.at[ch], idx_v)
                pltpu.sync_copy(h_hbm.at[pl.ds(ch * SC_CHUNK, SC_CHUNK)], rows_v)
                copies = [pltpu.async_copy(rows_v, out_hbm.at[idx_v.at[k]], sem) for k in range(TOP_K)]
                for cp in copies:
                    cp.wait()

    return pl.kernel(
        body,
        out_type=jax.ShapeDtypeStruct((r_pad, 4, 128), jnp.uint32),
        mesh=_sc_mesh(),
        scratch_types=[pltpu.VMEM((TOP_K, SC_CHUNK), jnp.int32),
                       pltpu.VMEM((SC_CHUNK, 4, 128), jnp.uint32),
                       pltpu.SemaphoreType.DMA],
        compiler_params=_sc_params(),
        name="sc_dispatch",
    )(h2p, pos3)


def _sc_collect(y_sorted, pos3):
    nch = pos3.shape[0]
    steps = -(-nch // SC_WORKERS)
    half = SC_CHUNK // 2
    units = [(k, hh) for k in range(TOP_K) for hh in range(2)]

    def body(y_hbm, pos_hbm, out_hbm, idx_v, rows_a, rows_b, sem_a, sem_b):
        wid = lax.axis_index("s") * SC_CORES + lax.axis_index("c")
        bufs = (rows_a, rows_b)
        sems = (sem_a, sem_b)

        def gather(u):
            k, hh = units[u]
            return pltpu.async_copy(y_hbm.at[idx_v.at[k, pl.ds(hh * half, half)]], bufs[u % 2], sems[u % 2])

        @pl.loop(0, steps)
        def _(s):
            ch = wid + s * SC_WORKERS

            @pl.when(ch < nch)
            def _():
                pltpu.sync_copy(pos_hbm.at[ch], idx_v)
                pending = gather(0)
                for u, (k, hh) in enumerate(units):
                    nxt = gather(u + 1) if u + 1 < len(units) else None
                    pending.wait()
                    pltpu.sync_copy(bufs[u % 2], out_hbm.at[k, pl.ds(ch * SC_CHUNK + hh * half, half)])
                    pending = nxt

    return pl.kernel(
        body,
        out_type=jax.ShapeDtypeStruct((TOP_K, nch * SC_CHUNK, 4, 128), jnp.uint32),
        mesh=_sc_mesh(),
        scratch_types=[pltpu.VMEM((TOP_K, SC_CHUNK), jnp.int32),
                       pltpu.VMEM((half, 4, 128), jnp.uint32),
                       pltpu.VMEM((half, 4, 128), jnp.uint32),
                       pltpu.SemaphoreType.DMA, pltpu.SemaphoreType.DMA],
        compiler_params=_sc_params(),
        name="sc_collect",
    )(y_sorted, pos3)


def _shared_kernel(x_ref, h2p_ref, yg_ref, gate_ref, wg_ref, wu_ref, wd_ref, g2_ref, o_ref):
    h_lo, h_hi = _unpack_rows(h2p_ref, TM_F)
    h = jnp.concatenate([h_lo.astype(BF16), h_hi.astype(BF16)], axis=1)
    a = jnp.dot(h, wg_ref[...].astype(BF16), preferred_element_type=F32)
    b = jnp.dot(h, wu_ref[...].astype(BF16), preferred_element_type=F32)
    hid = (a * jax.nn.sigmoid(a) * b).astype(BF16)
    f = jnp.dot(hid, wd_ref[...].astype(BF16), preferred_element_type=F32)
    eye = jnp.where(lax.broadcasted_iota(jnp.int32, (128, 128), 0) == lax.broadcasted_iota(jnp.int32, (128, 128), 1),
                    1.0, 0.0).astype(BF16)
    cols = []
    for c in range(TM_F // 128):
        g = gate_ref[c]
        g_hi = g.astype(BF16)
        g_lo = (g - g_hi.astype(F32)).astype(BF16)
        cols.append(lax.dot_general(eye, g_hi, _DN_T, preferred_element_type=F32)
                    + lax.dot_general(eye, g_lo, _DN_T, preferred_element_type=F32))
    gate = jnp.concatenate(cols, axis=0)
    f_lo = f[:, :D // 2]
    f_hi = f[:, D // 2:]
    for k in range(TOP_K):
        y_lo, y_hi = _unpack_rows(yg_ref, TM_F, lead=k)
        f_lo = f_lo + gate[:, k:k + 1] * y_lo
        f_hi = f_hi + gate[:, k:k + 1] * y_hi
    o_ref[...] = x_ref[...] + g2_ref[...] * jnp.concatenate([f_lo, f_hi], axis=1)


def _shared_residual(layer, xa, h2p, yg, gate3, ws_gate, ws_up, ws_down, mod3, n_rows):
    tm = TM_F
    row = lambda i: (i, 0)
    wmap = lambda i: (layer, 0, 0)
    return pl.pallas_call(
        _shared_kernel,
        out_shape=jax.ShapeDtypeStruct((n_rows, D), F32),
        grid=(n_rows // tm,),
        in_specs=[pl.BlockSpec((tm, D), row), pl.BlockSpec((tm * 4, 128), row),
                  pl.BlockSpec((TOP_K, tm * 4, 128), lambda i: (0, i, 0)),
                  pl.BlockSpec((tm // 128, TOP_K, 128), lambda i: (i, 0, 0)),
                  pl.BlockSpec((None, D, D_SHARED), wmap), pl.BlockSpec((None, D, D_SHARED), wmap),
                  pl.BlockSpec((None, D_SHARED, D), wmap), _mod_spec(tm, 5)],
        out_specs=pl.BlockSpec((tm, D), row),
        compiler_params=_cparams(("parallel",), 48),
        name="shared_residual",
    )(xa, h2p, yg, gate3, ws_gate, ws_up, ws_down, mod3)


def _positions_kernel(offs_ref, idx_ref, rank_ref, pos_ref):
    idx = idx_ref[...]
    base = jnp.zeros_like(idx)
    for e in range(N_EXPERTS):
        base = jnp.where(idx == e, offs_ref[e], base)
    pos_ref[...] = rank_ref[...] + base


def _positions(offs, idx3, rank3):
    nch = idx3.shape[0]
    cb = nch // 2
    spec = pl.BlockSpec((cb, TOP_K, 128), lambda i, offs: (i, 0, 0))
    return pl.pallas_call(
        _positions_kernel,
        out_shape=jax.ShapeDtypeStruct((nch, TOP_K, 128), jnp.int32),
        grid_spec=pltpu.PrefetchScalarGridSpec(num_scalar_prefetch=1, grid=(nch // cb,),
                                               in_specs=[spec, spec], out_specs=spec),
        compiler_params=_cparams(("parallel",)),
        name="positions",
    )(offs, idx3, rank3)


def _route_positions(idx3, rank3, counts, n_rows):
    tm = TM_X
    counts = counts.astype(jnp.int32)
    padded = ((counts + tm - 1) // tm) * tm
    ends = jnp.cumsum(padded)
    offs = ends - padded
    pos3 = _positions(offs.astype(jnp.int32), idx3, rank3)
    r_pad = n_rows * TOP_K + N_EXPERTS * tm
    nt = r_pad // tm
    tile_ids = jnp.arange(nt, dtype=jnp.int32)
    tile_expert = jnp.sum((ends // tm)[None, :] <= tile_ids[:, None], axis=1)
    tile_expert = jnp.minimum(tile_expert, N_EXPERTS - 1).astype(jnp.int32)
    n_used = (ends[-1] // tm).astype(jnp.int32).reshape(1)
    e_ids = jnp.arange(N_EXPERTS, dtype=jnp.int32)
    is_tile_expert = tile_expert[:, None] == e_ids[None, :]

    def per_tile(table):
        return jnp.sum(jnp.where(is_tile_expert, table[None, :], 0), axis=1).astype(jnp.int32)

    nonempty = counts > 0
    later = jnp.where(nonempty[None, :] & (e_ids[None, :] > e_ids[:, None]), e_ids[None, :], N_EXPERTS)
    next_e = jnp.min(later, axis=1)
    next_e = jnp.where(next_e == N_EXPERTS, -1, next_e).astype(jnp.int32)
    slot_e = ((jnp.cumsum(nonempty.astype(jnp.int32)) - nonempty.astype(jnp.int32)) % 2).astype(jnp.int32)
    tile_rows = jnp.clip(per_tile(counts) - (tile_ids - per_tile(offs // tm)) * tm, 0, tm)
    tile_rows = jnp.where(tile_ids < n_used[0], tile_rows, 0).astype(jnp.int32)
    sched = (tile_expert, per_tile(next_e), per_tile(slot_e), tile_rows, n_used)
    return pos3, sched, r_pad


def _in_weights(w):
    def regroup(cols):
        return cols.reshape(D, 2, HEADS, QK_DIM).transpose(0, 2, 1, 3).reshape(D, 2 * HEADS * QK_DIM)
    return jnp.concatenate([w[:, :OFF_Q], regroup(w[:, OFF_Q:OFF_K]), regroup(w[:, OFF_K:OFF_V]), w[:, OFF_V:]],
                           axis=1).astype(BF16)


def _rope_tables():
    t = jnp.arange(L)
    row = (t // GRID_W).astype(F32)
    col = (t % GRID_W).astype(F32)
    n_freq = QK_DIM // 4
    inv = ROPE_BASE ** (-jnp.arange(n_freq, dtype=F32) / n_freq)
    ar = row[:, None] * inv
    ac = col[:, None] * inv
    cos64 = jnp.concatenate([jnp.cos(ar), jnp.cos(ar), jnp.cos(ac), jnp.cos(ac)], axis=1)
    sin64 = jnp.concatenate([-jnp.sin(ar), jnp.sin(ar), -jnp.sin(ac), jnp.sin(ac)], axis=1)
    cos_t = jnp.concatenate([jnp.tile(cos64, (1, 2)), jnp.ones((TM_IN, 128), F32)], axis=0)
    sin_t = jnp.concatenate([jnp.tile(sin64, (1, 2)), jnp.zeros((TM_IN, 128), F32)], axis=0)
    return cos_t, sin_t


def _split_bf16(w):
    hi = w.astype(BF16)
    return hi, (w - hi.astype(F32)).astype(BF16)


def kernel(x, c, ctx, c_ctx, w_ada, b_ada, g_norm1, g_norm2, w_in, w_out, g_v, w_s, b_s, w_conv, g_q, g_k,
           lam_q1, lam_k1, lam_q2, lam_k2, g_sub, w_router, b_router, w_gate, w_up, w_down,
           ws_gate, ws_up, ws_down):
    src = (x.reshape(NL, D), ctx.reshape(NC, D))
    cc = jnp.concatenate([c, c_ctx[None, :], jnp.zeros((MOD_ROWS - B - 1, D), F32)], axis=0)
    mod = _ada(cc, w_ada, b_ada)
    cos_t, sin_t = _rope_tables()
    bd = jnp.asarray(np.kron(np.eye(RMS_BLOCK // QK_DIM, dtype=np.float32),
                             np.full((QK_DIM, QK_DIM), 1.0 / QK_DIM, np.float32)), BF16)

    for l in range(DEPTH):
        last = l == DEPTH - 1
        lam_init = 0.8 - 0.6 * math.exp(-0.3 * l)
        lam = (jnp.exp(jnp.sum(lam_q1[l] * lam_k1[l])) - jnp.exp(jnp.sum(lam_q2[l] * lam_k2[l])) + lam_init)
        bound = (QK_DIM * jnp.max(jnp.abs(g_q[l])) * jnp.max(jnp.abs(g_k[l]))
                 * (QK_DIM ** -0.5 * LOG2E) * ATT_BOUND_MARGIN)
        use_bound = (2.0 * bound < ATT_MAX_SHIFT_RANGE).astype(F32)
        lam = jnp.stack([lam, bound, use_bound]).astype(F32)
        mod3 = mod[l].reshape(MOD_ROWS, 1, 6 * D)
        w_in_bf = _in_weights(w_in[l])
        bias_t = jnp.repeat(b_s[l].T, A_GD, axis=1)
        wconv = jnp.concatenate([w_conv[l], jnp.zeros((5, B_WIDTH), F32)], axis=0)
        yab, q, kk, v = _in_mix(src[0], src[1], g_norm1[l], mod3, w_in_bf, cos_t, sin_t,
                                     g_v[l].reshape(1, A_WIDTH), w_s[l].astype(BF16), bias_t, wconv,
                                     jnp.tile(g_q[l], 8).reshape(1, 512), jnp.tile(g_k[l], 8).reshape(1, 512), bd)
        gsub = g_sub[l].reshape(1, V_DIM)
        coef = 1.0 - lam_init
        n_rows = NL if last else NR
        yc = _attention(lam, q, kk, v, gsub, coef, ctx_queries=False)
        yc_ctx = yc if last else _attention(lam, q, kk, v, gsub, coef, ctx_queries=True)
        wr_hi, wr_lo = _split_bf16(w_router[l].T)
        wo_bf = w_out[l].astype(BF16)
        xa, h2p, idx3, gate3, rank3, counts = _out_router(
            src[0], src[1], yab, yc, yc_ctx, wo_bf, g_norm2[l], mod3, wr_hi, wr_lo, b_router[l], n_rows)
        pos3, sched, r_pad = _route_positions(idx3, rank3, counts[:, 0], n_rows)
        x_sorted = _sc_dispatch(h2p.reshape(n_rows, 4, 128), pos3, r_pad)
        y_sorted = _experts(l, sched, x_sorted.reshape(r_pad * 4, 128), w_gate, w_up, w_down)
        yg = _sc_collect(y_sorted.reshape(r_pad, 4, 128), pos3)
        xa = _shared_residual(l, xa, h2p, yg.reshape(TOP_K, n_rows * 4, 128), gate3, ws_gate, ws_up, ws_down, mod3,
                              n_rows)
        src = (xa, xa)
    return xa.reshape(B, L, D)
```

```python
import functools
import math

import numpy as np
import jax
import jax.numpy as jnp
from jax import lax
from jax.experimental import pallas as pl
from jax.experimental.pallas import tpu as pltpu
from jax.experimental.pallas import tpu_sc as plsc

F32 = jnp.float32
BF16 = jnp.bfloat16

D = 1024
B = 8
L = 2048
DEPTH = 2
GRID_W = 64
CTX = 256
A_WIDTH = 256
A_GROUPS = 4
A_GD = 64
CHUNK = 128
B_WIDTH = 256
C_WIDTH = 512
HEADS = 4
V_DIM = 128
QK_DIM = 64
ROPE_BASE = 10000.0
OFF_BB = 512
OFF_BC = 768
OFF_BX = 1024
OFF_Q = 1280
OFF_K = 1792
OFF_V = 2304
D_IN = 2816
N_EXPERTS = 64
TOP_K = 8
D_EXPERT = 256
D_SHARED = 256
ROUTED_SCALE = 2.5
EPS = 1e-6

NL = B * L
NC = B * CTX
NR = NL + NC
MOD_ROWS = 16
LOG2E = 1.4426950408889634

TM_IN = 512
RMS_BLOCK = 256
TQ = 2048
ATT_CHAIN = 256
ATT_GROUP = 4
ATT_BOUND_MARGIN = 1.02
ATT_MAX_SHIFT_RANGE = 100.0
TM_OUT = 512
TM_X = 1024
X_CHAINS = 2
X_TAIL_ROWS = 512
X_TAIL = 256
X_RING = 3
TM_F = 512

_DN_T = (((1,), (1,)), ((), ()))


def _cparams(sem, vmem_mb=None):
    kw = dict(dimension_semantics=sem)
    if vmem_mb is not None:
        kw["vmem_limit_bytes"] = vmem_mb * 1024 * 1024
    return pltpu.CompilerParams(**kw)


def _mod_row(i, tm):
    return jnp.where(i < NL // tm, i // (L // tm), B)


def _mod_spec(tm, chunk):
    return pl.BlockSpec((None, 1, D), lambda i: (_mod_row(i, tm), 0, chunk))


def _ada_kernel(c_ref, w_ref, b_ref, o_ref):
    c = c_ref[...]
    cs = c * jax.nn.sigmoid(c)
    o_ref[...] = jnp.dot(cs, w_ref[...], preferred_element_type=F32,
                         precision=lax.Precision.HIGHEST) + b_ref[...]


def _ada(cc, w_ada, b_ada):
    nb = 6
    return pl.pallas_call(
        _ada_kernel,
        out_shape=jax.ShapeDtypeStruct((DEPTH, MOD_ROWS, 6 * D), F32),
        grid=(DEPTH, nb),
        in_specs=[pl.BlockSpec((MOD_ROWS, D), lambda l, j: (0, 0)),
                  pl.BlockSpec((None, D, D), lambda l, j: (l, 0, j)),
                  pl.BlockSpec((None, 1, D), lambda l, j: (l, 0, j))],
        out_specs=pl.BlockSpec((None, MOD_ROWS, D), lambda l, j: (l, 0, j)),
        compiler_params=_cparams(("arbitrary", "arbitrary"), 40),
        name="ada_mod",
    )(cc, w_ada, b_ada.reshape(DEPTH, 1, 6 * D))


def _rms_mod(x, g, sc, sh):
    ms = jnp.mean(x * x, axis=-1, keepdims=True)
    return x * lax.rsqrt(ms + EPS) * (g * (1.0 + sc)) + sh


def _two_source_specs(tm, n_first, width=D):
    return [pl.BlockSpec((tm, width), lambda i: (jnp.minimum(i, n_first - 1), 0)),
            pl.BlockSpec((tm, width), lambda i: (jnp.maximum(i - n_first, 0), 0))]


def _two_source_rows(a_ref, b_ref, n_first):
    return jnp.where(pl.program_id(0) < n_first, a_ref[...], b_ref[...])


def _group_rms(t, g, bd):
    sq = (t * t).astype(BF16)
    ms = jnp.concatenate([jnp.dot(sq[:, c:c + RMS_BLOCK], bd, preferred_element_type=F32)
                          for c in range(0, t.shape[1], RMS_BLOCK)], axis=1)
    return t * lax.rsqrt(ms + EPS) * g


def _rope(t, cos, sin):
    w = t.shape[1]
    lane = lax.broadcasted_iota(jnp.int32, t.shape, 1)
    first = (lane % 32) < 16
    partner = jnp.where(first, pltpu.roll(t, w - 16, 1), pltpu.roll(t, 16, 1))
    cos4 = jnp.concatenate([cos] * (w // 128), axis=1)
    sin4 = jnp.concatenate([sin] * (w // 128), axis=1)
    return t * cos4 + partner * sin4


def _in_mix_kernel(xa_ref, xb_ref, pa_ref, pb_ref, na_ref, nb_ref, g_ref, sh_ref, sc_ref, w_ref,
                   cos_ref, sin_ref, gv_ref, ws_ref, bias_ref, wconv_ref, gq_ref, gk_ref, bd_ref,
                   yab_ref, q_ref, kk_ref, v_ref, *, n_first):
    tm = TM_IN
    i = pl.program_id(0)
    tiles_per_seq = L // tm
    is_lat = i < NL // tm
    is_start = jnp.logical_or(jnp.logical_not(is_lat), i % tiles_per_seq == 0)
    is_end = jnp.logical_or(jnp.logical_not(is_lat), i % tiles_per_seq == tiles_per_seq - 1)
    first = i < n_first
    g, sc, sh = g_ref[...], sc_ref[...], sh_ref[...]

    h = _rms_mod(jnp.where(first, xa_ref[...], xb_ref[...]), g, sc, sh)
    p = jnp.dot(h.astype(BF16), w_ref[...], preferred_element_type=F32)
    v_ref[...] = p[:, OFF_V:].astype(BF16)
    halo = jnp.concatenate([jnp.where(first, pa_ref[...], pb_ref[...]),
                            jnp.where(first, na_ref[...], nb_ref[...])], axis=0)
    ph = jnp.dot(_rms_mod(halo, g, sc, sh).astype(BF16), w_ref[:, OFF_BC:OFF_Q], preferred_element_type=F32)
    zh = ph[:, :B_WIDTH] * ph[:, B_WIDTH:]
    zp = jnp.where(is_start, 0.0, zh[15:16])
    zn = jnp.where(is_end, 0.0, zh[16:17])

    uv = p[:, 0:2 * A_WIDTH]
    uv = 0.5 * uv * (1.0 + lax.erf(uv * (2.0 ** -0.5)))
    u = uv[:, :A_WIDTH]
    v = uv[:, A_WIDTH:]
    ms = jnp.mean(v * v, axis=-1, keepdims=True)
    vb = (v * lax.rsqrt(ms + EPS) * gv_ref[...]).astype(BF16)
    lane = lax.broadcasted_iota(jnp.int32, (CHUNK, 128), 1)
    mixes = []
    for c in range(tm // CHUNK):
        vc = vb[c * CHUNK:(c + 1) * CHUNK]
        halves = []
        for j in range(2):
            vj = vc[:, j * 128:(j + 1) * 128]
            m0 = jnp.dot(ws_ref[2 * j], vj, preferred_element_type=F32)
            m1 = jnp.dot(ws_ref[2 * j + 1], vj, preferred_element_type=F32)
            halves.append(jnp.where(lane < A_GD, m0, m1))
        mixes.append(jnp.concatenate(halves, axis=1) + bias_ref[...])
    ya = u * jnp.concatenate(mixes, axis=0)

    bg = p[:, OFF_BB:OFF_BC]
    z = p[:, OFF_BC:OFF_BX] * p[:, OFF_BX:OFF_Q]
    row = lax.broadcasted_iota(jnp.int32, z.shape, 0)
    inner = jnp.logical_not(is_lat)
    z_prev = jnp.where(row == 0, zp, pltpu.roll(z, 1, 0))
    z_prev = jnp.where(jnp.logical_and(inner, row % CTX == 0), 0.0, z_prev)
    z_next = jnp.where(row == tm - 1, zn, pltpu.roll(z, tm - 1, 0))
    z_next = jnp.where(jnp.logical_and(inner, row % CTX == CTX - 1), 0.0, z_next)
    yb = bg * (z_prev * wconv_ref[0:1] + z * wconv_ref[1:2] + z_next * wconv_ref[2:3])
    yab_ref[...] = jnp.concatenate([ya, yb], axis=1).astype(BF16)

    cos = cos_ref[...]
    sin = sin_ref[...]
    bd = bd_ref[...]
    q = _rope(_group_rms(p[:, OFF_Q:OFF_K], gq_ref[...], bd), cos, sin)
    q = q * (QK_DIM ** -0.5 * LOG2E)
    q_ref[...] = q.astype(BF16)
    k = _rope(_group_rms(p[:, OFF_K:OFF_V], gk_ref[...], bd), cos, sin)
    kk_ref[...] = k.astype(BF16)


def _in_mix(x_first, x_second, g, mod3, w_bf, cos_t, sin_t, gv, ws_bf, bias_t, wconv, gq, gk, bd):
    tm = TM_IN
    n_first = x_first.shape[0] // tm
    hb = tm // 16
    nhb_first = x_first.shape[0] // 16
    nhb_second = x_second.shape[0] // 16
    pos_blocks = L // tm

    def tab_map(i):
        return (jnp.where(i < NL // tm, i % pos_blocks, pos_blocks), 0)

    def halo_specs(shift):
        blk = lambda i: (i * tm + shift) // 16
        return [pl.BlockSpec((16, D), lambda i: (jnp.clip(blk(i), 0, nhb_first - 1), 0)),
                pl.BlockSpec((16, D), lambda i: (jnp.clip(blk(i) - nhb_first, 0, nhb_second - 1), 0))]

    const2 = lambda i: (0, 0)
    row512 = pl.BlockSpec((tm, 512), lambda i: (i, 0))
    return pl.pallas_call(
        functools.partial(_in_mix_kernel, n_first=n_first),
        out_shape=[jax.ShapeDtypeStruct((NR, 512), BF16)] * 4,
        grid=(NR // tm,),
        in_specs=_two_source_specs(tm, n_first) + halo_specs(-1) + halo_specs(tm)
                 + [pl.BlockSpec((1, D), const2), _mod_spec(tm, 0), _mod_spec(tm, 1),
                    pl.BlockSpec((D, D_IN), const2),
                    pl.BlockSpec((tm, 128), tab_map), pl.BlockSpec((tm, 128), tab_map),
                    pl.BlockSpec((1, A_WIDTH), const2),
                    pl.BlockSpec((A_GROUPS, CHUNK, CHUNK), lambda i: (0, 0, 0)),
                    pl.BlockSpec((CHUNK, A_WIDTH), const2),
                    pl.BlockSpec((8, B_WIDTH), const2),
                    pl.BlockSpec((1, 512), const2), pl.BlockSpec((1, 512), const2),
                    pl.BlockSpec((RMS_BLOCK, RMS_BLOCK), const2)],
        out_specs=[row512] * 4,
        compiler_params=_cparams(("parallel",), 56),
        name="in_mix",
    )(x_first, x_second, x_first, x_second, x_first, x_second, g.reshape(1, D), mod3, mod3, w_bf,
      cos_t, sin_t, gv, ws_bf, bias_t, wconv, gq, gk, bd)


def _attn_kernel(lam_ref, q_ref, *rest, n_seg, coef, tq):
    kv_refs = rest[:2 * n_seg]
    gsub_ref, o_ref, k_scr, vt_scr = rest[2 * n_seg:]

    @pl.when(pl.program_id(2) == 0)
    def _():
        off = 0
        for s in range(n_seg):
            n = kv_refs[s].shape[0]
            k_scr[off:off + n, :] = kv_refs[s][...]
            vt_scr[0:V_DIM, off:off + n] = kv_refs[n_seg + s][...].astype(F32).T.astype(BF16)
            off += n
        ones_row = lax.broadcasted_iota(jnp.int32, (16, off), 0) == 0
        vt_scr[V_DIM:, :] = jnp.where(ones_row, 1.0, 0.0).astype(BF16)

    lam = lam_ref[0]
    shift = lam_ref[1]
    qc = ATT_CHAIN

    def scores(c):
        rows = slice(c * qc, (c + 1) * qc)
        q = q_ref[rows, :]
        lane = lax.broadcasted_iota(jnp.int32, q.shape, 1)
        zero = jnp.zeros_like(q)
        qs = jnp.concatenate([jnp.where(lane < QK_DIM, q, zero), jnp.where(lane >= QK_DIM, q, zero)], axis=0)
        return lax.dot_general(k_scr[...], qs, _DN_T, preferred_element_type=F32)

    def finish(c, pt):
        ot = jnp.dot(vt_scr[...], pt, preferred_element_type=F32)
        inv = 1.0 / ot[V_DIM:V_DIM + 1, :]
        dt = ot[0:V_DIM, :qc] * inv[:, :qc] - ot[0:V_DIM, qc:] * (lam * inv[:, qc:])
        o = dt.T
        ms = jnp.mean(o * o, axis=-1, keepdims=True)
        o_ref[c * qc:(c + 1) * qc, :] = (o * lax.rsqrt(ms + EPS) * gsub_ref[...] * coef).astype(o_ref.dtype)

    @pl.when(lam_ref[2] > 0.5)
    def _():
        for c in range(tq // qc):
            finish(c, jnp.exp2(scores(c) - shift).astype(BF16))

    @pl.when(lam_ref[2] <= 0.5)
    def _():
        for g0 in range(0, tq // qc, ATT_GROUP):
            group = range(g0, min(g0 + ATT_GROUP, tq // qc))
            sts = [scores(c) for c in group]
            for c, st in zip(group, sts):
                finish(c, jnp.exp2(st - jnp.max(st, axis=0, keepdims=True)).astype(BF16))


def _attention(lam, q, kk, v, gsub, coef, *, ctx_queries):
    if ctx_queries:
        tq = CTX
        nq, lk, n_seg = 1, CTX, 1
        q_map = lambda b, h, qi: (NL // tq + b, h)
        kv_specs = [pl.BlockSpec((CTX, 128), lambda b, h, qi: (NL // CTX + b, h)),
                    pl.BlockSpec((CTX, 128), lambda b, h, qi: (NL // CTX + b, h))]
        kv_args = [kk, v]
        rows = NC
        o_map = lambda b, h, qi: (b, h)
    else:
        tq = TQ
        nq, lk, n_seg = L // tq, CTX + L, 2
        q_map = lambda b, h, qi: (b * (L // tq) + qi, h)
        kv_specs = [pl.BlockSpec((CTX, 128), lambda b, h, qi: (NL // CTX + b, h)),
                    pl.BlockSpec((L, 128), lambda b, h, qi: (b, h)),
                    pl.BlockSpec((CTX, 128), lambda b, h, qi: (NL // CTX + b, h)),
                    pl.BlockSpec((L, 128), lambda b, h, qi: (b, h))]
        kv_args = [kk, kk, v, v]
        rows = NL
        o_map = lambda b, h, qi: (b * (L // tq) + qi, h)
    return pl.pallas_call(
        functools.partial(_attn_kernel, n_seg=n_seg, coef=coef, tq=tq),
        out_shape=jax.ShapeDtypeStruct((rows, C_WIDTH), BF16),
        grid=(B, HEADS, nq),
        in_specs=[pl.BlockSpec(memory_space=pltpu.SMEM),
                  pl.BlockSpec((tq, 128), q_map)]
                 + kv_specs + [pl.BlockSpec((1, V_DIM), lambda b, h, qi: (0, 0))],
        out_specs=pl.BlockSpec((tq, 128), o_map),
        scratch_shapes=[pltpu.VMEM((lk, 128), BF16), pltpu.VMEM((V_DIM + 16, lk), BF16)],
        compiler_params=_cparams(("parallel", "parallel", "arbitrary"), 56),
        name="attn_ctx" if ctx_queries else "attn_lat",
    )(lam, q, *kv_args, gsub)


def _pack_rows(t, out_ref, row0=0):
    half = D // 2
    w = pltpu.pack_elementwise([t[:, :half], t[:, half:]], packed_dtype=BF16)
    w = lax.bitcast_convert_type(w, jnp.uint32)
    rows = t.shape[0]
    for j in range(4):
        out_ref[pl.ds(4 * row0 + j, rows, stride=4), :] = w[:, j * 128:(j + 1) * 128]


def _unpack_rows(ref, rows, lead=None, row0=0):
    los, his = [], []
    for j in range(4):
        sl = pl.ds(4 * row0 + j, rows, stride=4)
        w = ref[sl, :] if lead is None else ref[lead, sl, :]
        los.append(pltpu.unpack_elementwise(w, index=0, packed_dtype=BF16, unpacked_dtype=F32))
        his.append(pltpu.unpack_elementwise(w, index=1, packed_dtype=BF16, unpacked_dtype=F32))
    return jnp.concatenate(los, axis=1), jnp.concatenate(his, axis=1)


def _out_router_kernel(xa_ref, xb_ref, yab_ref, yca_ref, ycb_ref, wo_ref, g1_ref, g2n_ref, sh2_ref, sc2_ref, wrh_ref,
                       wrl_ref, br_ref, xo_ref, h2p_ref, idx_ref, gate_ref, rank_ref, cnt_ref, run_ref, *,
                       n_first, n_first_c):
    tm = TM_OUT

    @pl.when(pl.program_id(0) == 0)
    def _():
        run_ref[...] = jnp.zeros_like(run_ref)

    yc = _two_source_rows(yca_ref, ycb_ref, n_first_c)
    y = jnp.dot(jnp.concatenate([yab_ref[...], yc], axis=1), wo_ref[...], preferred_element_type=F32)
    x = _two_source_rows(xa_ref, xb_ref, n_first) + g1_ref[...] * y
    xo_ref[...] = x
    h2 = _rms_mod(x, g2n_ref[...], sc2_ref[...], sh2_ref[...])
    hi = h2.astype(BF16)
    _pack_rows(h2, h2p_ref)
    lo = (h2 - hi.astype(F32)).astype(BF16)
    wh = wrh_ref[...]
    z = (lax.dot_general(wh, hi, _DN_T, preferred_element_type=F32)
         + lax.dot_general(wh, lo, _DN_T, preferred_element_type=F32)
         + lax.dot_general(wrl_ref[...], hi, _DN_T, preferred_element_type=F32))
    scores = jax.nn.sigmoid(z)
    work = scores + br_ref[...]
    eio = lax.broadcasted_iota(jnp.int32, work.shape, 0)
    idxs, sels, hits = [], [], []
    for _ in range(TOP_K):
        m = jnp.max(work, axis=0, keepdims=True)
        idx = jnp.min(jnp.where(work == m, eio, N_EXPERTS), axis=0, keepdims=True)
        hit = eio == idx
        sels.append(jnp.sum(jnp.where(hit, scores, 0.0), axis=0, keepdims=True))
        idxs.append(idx)
        hits.append(hit)
        work = jnp.where(hit, -jnp.inf, work)
    sel = jnp.concatenate(sels, axis=0)
    idx8 = jnp.concatenate(idxs, axis=0)
    gate8 = sel / jnp.sum(sel, axis=0, keepdims=True) * ROUTED_SCALE

    chosen = functools.reduce(jnp.logical_or, hits)
    before = (lax.broadcasted_iota(jnp.int32, (tm, tm), 0) < lax.broadcasted_iota(jnp.int32, (tm, tm), 1))
    prefix = jnp.dot(jnp.where(chosen, 1.0, 0.0).astype(BF16), jnp.where(before, 1.0, 0.0).astype(BF16),
                     preferred_element_type=F32)
    rank_dense = prefix + run_ref[:, 0:1]
    rank8 = jnp.concatenate([jnp.sum(jnp.where(h, rank_dense, 0.0), axis=0, keepdims=True) for h in hits],
                            axis=0).astype(jnp.int32)
    run = run_ref[...] + jnp.sum(jnp.where(chosen, 1.0, 0.0), axis=1, keepdims=True)
    run_ref[...] = run
    cnt_ref[...] = run
    for c in range(tm // 128):
        idx_ref[c] = idx8[:, c * 128:(c + 1) * 128]
        gate_ref[c] = gate8[:, c * 128:(c + 1) * 128]
        rank_ref[c] = rank8[:, c * 128:(c + 1) * 128]


def _out_router(x_first, x_second, yab, yc_first, yc_second, wo_bf, g2n, mod3, wr_hi, wr_lo, br, n_rows):
    tm = TM_OUT
    n_first = x_first.shape[0] // tm
    n_first_c = yc_first.shape[0] // tm
    const2 = lambda i: (0, 0)
    row = lambda i: (i, 0)
    chunk3 = pl.BlockSpec((tm // 128, TOP_K, 128), lambda i: (i, 0, 0))
    nch = n_rows // 128
    return pl.pallas_call(
        functools.partial(_out_router_kernel, n_first=n_first, n_first_c=n_first_c),
        out_shape=[jax.ShapeDtypeStruct((n_rows, D), F32),
                   jax.ShapeDtypeStruct((n_rows * 4, 128), jnp.uint32),
                   jax.ShapeDtypeStruct((nch, TOP_K, 128), jnp.int32),
                   jax.ShapeDtypeStruct((nch, TOP_K, 128), F32),
                   jax.ShapeDtypeStruct((nch, TOP_K, 128), jnp.int32),
                   jax.ShapeDtypeStruct((N_EXPERTS, 128), F32)],
        grid=(n_rows // tm,),
        in_specs=_two_source_specs(tm, n_first)
                 + [pl.BlockSpec((tm, 512), row)]
                 + _two_source_specs(tm, n_first_c, C_WIDTH)
                 + [pl.BlockSpec((D, D), const2),
                  _mod_spec(tm, 2),
                  pl.BlockSpec((1, D), const2), _mod_spec(tm, 3), _mod_spec(tm, 4),
                  pl.BlockSpec((N_EXPERTS, D), const2), pl.BlockSpec((N_EXPERTS, D), const2),
                  pl.BlockSpec((N_EXPERTS, 1), const2)],
        out_specs=[pl.BlockSpec((tm, D), row), pl.BlockSpec((tm * 4, 128), row),
                   chunk3, chunk3, chunk3, pl.BlockSpec((N_EXPERTS, 128), const2)],
        scratch_shapes=[pltpu.VMEM((N_EXPERTS, 128), F32)],
        compiler_params=_cparams(("arbitrary",), 48),
        name="out_router",
    )(x_first, x_second, yab, yc_first, yc_second, wo_bf, mod3, g2n.reshape(1, D), mod3, mod3, wr_hi, wr_lo,
      br.reshape(N_EXPERTS, 1))


def _experts_kernel(ts_ref, nt_ref, cnt_ref, nu_ref, x_hbm, wg_ref, wu_ref, wd_ref, y_hbm,
                    wgu_s, wd_s, x_ring, x_sems, y_buf, y_sems):
    e = pl.program_id(0)
    n_used = nu_ref[0]
    t0 = ts_ref[e]
    n_tiles = nt_ref[e]
    count = cnt_ref[e]
    ahead = X_RING - 1
    rows4 = 4 * TM_X

    def x_copy(t):
        slot = t % X_RING
        return pltpu.make_async_copy(x_hbm.at[pl.ds(t * rows4, rows4)], x_ring.at[slot], x_sems.at[slot])

    def y_copy(t):
        slot = t % 2
        return pltpu.make_async_copy(y_buf.at[slot], y_hbm.at[pl.ds(t * rows4, rows4)], y_sems.at[slot])

    @pl.when(e == 0)
    def _():
        for t in range(ahead):
            @pl.when(t < n_used)
            def _():
                x_copy(t).start()

    @pl.when(n_tiles > 0)
    def _():
        wgu_s[:, 0:D_EXPERT] = wg_ref[...].astype(BF16)
        wgu_s[:, D_EXPERT:] = wu_ref[...].astype(BF16)
        wd_s[...] = wd_ref[...].astype(BF16)

    def first_dot(x_ref, row0, rows):
        x_lo, x_hi = _unpack_rows(x_ref, rows, row0=row0)
        x = jnp.concatenate([x_lo.astype(BF16), x_hi.astype(BF16)], axis=1)
        return jnp.dot(x, wgu_s[...], preferred_element_type=F32)

    def second_dot(ab, y_ref, row0):
        a = ab[:, :D_EXPERT]
        hid = (a * jax.nn.sigmoid(a) * ab[:, D_EXPERT:]).astype(BF16)
        _pack_rows(jnp.dot(hid, wd_s[...], preferred_element_type=F32), y_ref, row0=row0)

    def tile(j, carry):
        t = t0 + j

        @pl.when(t + ahead < n_used)
        def _():
            x_copy(t + ahead).start()

        x_copy(t).wait()

        @pl.when(t >= 2)
        def _():
            y_copy(t - 2).wait()

        x_ref = x_ring.at[t % X_RING]
        y_ref = y_buf.at[t % 2]
        valid = count - j * TM_X

        @pl.when(valid > X_TAIL_ROWS)
        def _():
            rc = TM_X // X_CHAINS
            abs_ = [first_dot(x_ref, c * rc, rc) for c in range(X_CHAINS)]
            for c in range(X_CHAINS):
                second_dot(abs_[c], y_ref, c * rc)

        for c in range(X_TAIL_ROWS // X_TAIL):
            @pl.when(jnp.logical_and(valid <= X_TAIL_ROWS, c * X_TAIL < valid))
            def _():
                second_dot(first_dot(x_ref, c * X_TAIL, X_TAIL), y_ref, c * X_TAIL)

        y_copy(t).start()
        return carry

    lax.fori_loop(0, n_tiles, tile, 0)

    @pl.when(e == N_EXPERTS - 1)
    def _():
        for d in (2, 1):
            @pl.when(n_used >= d)
            def _():
                y_copy(n_used - d).wait()


def _experts(layer, sched, x_sorted, w_gate, w_up, w_down):
    tm = TM_X
    r_pad = x_sorted.shape[0] // 4
    hbm = pl.BlockSpec(memory_space=pl.ANY)
    wmap = lambda e, ts, nt, cnt, nu: (layer, e, 0, 0)
    grid_spec = pltpu.PrefetchScalarGridSpec(
        num_scalar_prefetch=4,
        grid=(N_EXPERTS,),
        in_specs=[hbm,
                  pl.BlockSpec((None, None, D, D_EXPERT), wmap),
                  pl.BlockSpec((None, None, D, D_EXPERT), wmap),
                  pl.BlockSpec((None, None, D_EXPERT, D), wmap)],
        out_specs=hbm,
        scratch_shapes=[pltpu.VMEM((D, 2 * D_EXPERT), BF16), pltpu.VMEM((D_EXPERT, D), BF16),
                        pltpu.VMEM((X_RING, tm * 4, 128), jnp.uint32), pltpu.SemaphoreType.DMA((X_RING,)),
                        pltpu.VMEM((2, tm * 4, 128), jnp.uint32), pltpu.SemaphoreType.DMA((2,))],
    )
    return pl.pallas_call(
        _experts_kernel,
        out_shape=jax.ShapeDtypeStruct((r_pad * 4, 128), jnp.uint32),
        grid_spec=grid_spec,
        compiler_params=_cparams(("arbitrary",), 48),
        name="experts",
    )(*sched, x_sorted, w_gate, w_up, w_down)


SC_CORES = 2
SC_SUBCORES = 16
SC_WORKERS = SC_CORES * SC_SUBCORES
SC_CHUNK = 128


def _sc_mesh():
    return plsc.VectorSubcoreMesh(core_axis_name="c", subcore_axis_name="s")


def _sc_params():
    return pltpu.CompilerParams(use_tc_tiling_on_sc=True)


def _sc_dispatch(h2p, pos3, r_pad):
    nch = pos3.shape[0]
    steps = -(-nch // SC_WORKERS)

    def body(h_hbm, pos_hbm, out_hbm, idx_v, rows_v, sem):
        wid = lax.axis_index("s") * SC_CORES + lax.axis_index("c")

        @pl.loop(0, steps)
        def _(s):
            ch = wid + s * SC_WORKERS

            @pl.when(ch < nch)
            def _():
                pltpu.sync_copy(pos_hbm.at[ch], idx_v)
                pltpu.sync_copy(h_hbm.at[pl.ds(ch * SC_CHUNK, SC_CHUNK)], rows_v)
                copies = [pltpu.async_copy(rows_v, out_hbm.at[idx_v.at[k]], sem) for k in range(TOP_K)]
                for cp in copies:
                    cp.wait()

    return pl.kernel(
        body,
        out_type=jax.ShapeDtypeStruct((r_pad, 4, 128), jnp.uint32),
        mesh=_sc_mesh(),
        scratch_types=[pltpu.VMEM((TOP_K, SC_CHUNK), jnp.int32),
                       pltpu.VMEM((SC_CHUNK, 4, 128), jnp.uint32),
                       pltpu.SemaphoreType.DMA],
        compiler_params=_sc_params(),
        name="sc_dispatch",
    )(h2p, pos3)


def _sc_collect(y_sorted, pos3):
    nch = pos3.shape[0]
    steps = -(-nch // SC_WORKERS)
    half = SC_CHUNK // 2
    units = [(k, hh) for k in range(TOP_K) for hh in range(2)]

    def body(y_hbm, pos_hbm, out_hbm, idx_v, rows_a, rows_b, sem_a, sem_b):
        wid = lax.axis_index("s") * SC_CORES + lax.axis_index("c")
        bufs = (rows_a, rows_b)
        sems = (sem_a, sem_b)

        def gather(u):
            k, hh = units[u]
            return pltpu.async_copy(y_hbm.at[idx_v.at[k, pl.ds(hh * half, half)]], bufs[u % 2], sems[u % 2])

        @pl.loop(0, steps)
        def _(s):
            ch = wid + s * SC_WORKERS

            @pl.when(ch < nch)
            def _():
                pltpu.sync_copy(pos_hbm.at[ch], idx_v)
                pending = gather(0)
                for u, (k, hh) in enumerate(units):
                    nxt = gather(u + 1) if u + 1 < len(units) else None
                    pending.wait()
                    pltpu.sync_copy(bufs[u % 2], out_hbm.at[k, pl.ds(ch * SC_CHUNK + hh * half, half)])
                    pending = nxt

    return pl.kernel(
        body,
        out_type=jax.ShapeDtypeStruct((TOP_K, nch * SC_CHUNK, 4, 128), jnp.uint32),
        mesh=_sc_mesh(),
        scratch_types=[pltpu.VMEM((TOP_K, SC_CHUNK), jnp.int32),
                       pltpu.VMEM((half, 4, 128), jnp.uint32),
                       pltpu.VMEM((half, 4, 128), jnp.uint32),
                       pltpu.SemaphoreType.DMA, pltpu.SemaphoreType.DMA],
        compiler_params=_sc_params(),
        name="sc_collect",
    )(y_sorted, pos3)


def _shared_kernel(x_ref, h2p_ref, yg_ref, gate_ref, wg_ref, wu_ref, wd_ref, g2_ref, o_ref):
    h_lo, h_hi = _unpack_rows(h2p_ref, TM_F)
    h = jnp.concatenate([h_lo.astype(BF16), h_hi.astype(BF16)], axis=1)
    a = jnp.dot(h, wg_ref[...], preferred_element_type=F32)
    b = jnp.dot(h, wu_ref[...], preferred_element_type=F32)
    hid = (a * jax.nn.sigmoid(a) * b).astype(BF16)
    f = jnp.dot(hid, wd_ref[...], preferred_element_type=F32)
    gate = gate_ref[...]
    f_lo = f[:, :D // 2]
    f_hi = f[:, D // 2:]
    for k in range(TOP_K):
        y_lo, y_hi = _unpack_rows(yg_ref, TM_F, lead=k)
        f_lo = f_lo + gate[:, k:k + 1] * y_lo
        f_hi = f_hi + gate[:, k:k + 1] * y_hi
    o_ref[...] = x_ref[...] + g2_ref[...] * jnp.concatenate([f_lo, f_hi], axis=1)


def _shared_residual(xa, h2p, yg, gates, wsg_bf, wsu_bf, wsd_bf, mod3, n_rows):
    tm = TM_F
    row = lambda i: (i, 0)
    const2 = lambda i: (0, 0)
    return pl.pallas_call(
        _shared_kernel,
        out_shape=jax.ShapeDtypeStruct((n_rows, D), F32),
        grid=(n_rows // tm,),
        in_specs=[pl.BlockSpec((tm, D), row), pl.BlockSpec((tm * 4, 128), row),
                  pl.BlockSpec((TOP_K, tm * 4, 128), lambda i: (0, i, 0)),
                  pl.BlockSpec((tm, TOP_K), row),
                  pl.BlockSpec((D, D_SHARED), const2), pl.BlockSpec((D, D_SHARED), const2),
                  pl.BlockSpec((D_SHARED, D), const2), _mod_spec(tm, 5)],
        out_specs=pl.BlockSpec((tm, D), row),
        compiler_params=_cparams(("parallel",), 48),
        name="shared_residual",
    )(xa, h2p, yg, gates, wsg_bf, wsu_bf, wsd_bf, mod3)


def _positions_kernel(offs_ref, idx_ref, rank_ref, pos_ref):
    idx = idx_ref[...]
    base = jnp.zeros_like(idx)
    for e in range(N_EXPERTS):
        base = jnp.where(idx == e, offs_ref[e], base)
    pos_ref[...] = rank_ref[...] + base


def _positions(offs, idx3, rank3):
    nch = idx3.shape[0]
    cb = nch // 2
    spec = pl.BlockSpec((cb, TOP_K, 128), lambda i, offs: (i, 0, 0))
    return pl.pallas_call(
        _positions_kernel,
        out_shape=jax.ShapeDtypeStruct((nch, TOP_K, 128), jnp.int32),
        grid_spec=pltpu.PrefetchScalarGridSpec(num_scalar_prefetch=1, grid=(nch // cb,),
                                               in_specs=[spec, spec], out_specs=spec),
        compiler_params=_cparams(("parallel",)),
        name="positions",
    )(offs, idx3, rank3)


def _route_positions(idx3, rank3, counts, n_rows):
    tm = TM_X
    counts = counts.astype(jnp.int32)
    padded = ((counts + tm - 1) // tm) * tm
    ends = jnp.cumsum(padded)
    offs = ends - padded
    pos3 = _positions(offs.astype(jnp.int32), idx3, rank3)
    r_pad = n_rows * TOP_K + N_EXPERTS * tm
    n_used = (ends[-1] // tm).astype(jnp.int32).reshape(1)
    sched = ((offs // tm).astype(jnp.int32), (padded // tm).astype(jnp.int32), counts, n_used)
    return pos3, sched, r_pad


def _in_weights(w):
    def regroup(cols):
        return cols.reshape(D, 2, HEADS, QK_DIM).transpose(0, 2, 1, 3).reshape(D, 2 * HEADS * QK_DIM)
    return jnp.concatenate([w[:, :OFF_Q], regroup(w[:, OFF_Q:OFF_K]), regroup(w[:, OFF_K:OFF_V]), w[:, OFF_V:]],
                           axis=1).astype(BF16)


def _rope_tables():
    t = jnp.arange(L)
    row = (t // GRID_W).astype(F32)
    col = (t % GRID_W).astype(F32)
    n_freq = QK_DIM // 4
    inv = ROPE_BASE ** (-jnp.arange(n_freq, dtype=F32) / n_freq)
    ar = row[:, None] * inv
    ac = col[:, None] * inv
    cos64 = jnp.concatenate([jnp.cos(ar), jnp.cos(ar), jnp.cos(ac), jnp.cos(ac)], axis=1)
    sin64 = jnp.concatenate([-jnp.sin(ar), jnp.sin(ar), -jnp.sin(ac), jnp.sin(ac)], axis=1)
    cos_t = jnp.concatenate([jnp.tile(cos64, (1, 2)), jnp.ones((TM_IN, 128), F32)], axis=0)
    sin_t = jnp.concatenate([jnp.tile(sin64, (1, 2)), jnp.zeros((TM_IN, 128), F32)], axis=0)
    return cos_t, sin_t


def _split_bf16(w):
    hi = w.astype(BF16)
    return hi, (w - hi.astype(F32)).astype(BF16)


def kernel(x, c, ctx, c_ctx, w_ada, b_ada, g_norm1, g_norm2, w_in, w_out, g_v, w_s, b_s, w_conv, g_q, g_k,
           lam_q1, lam_k1, lam_q2, lam_k2, g_sub, w_router, b_router, w_gate, w_up, w_down,
           ws_gate, ws_up, ws_down):
    src = (x.reshape(NL, D), ctx.reshape(NC, D))
    cc = jnp.concatenate([c, c_ctx[None, :], jnp.zeros((MOD_ROWS - B - 1, D), F32)], axis=0)
    mod = _ada(cc, w_ada, b_ada)
    cos_t, sin_t = _rope_tables()
    bd = jnp.asarray(np.kron(np.eye(RMS_BLOCK // QK_DIM, dtype=np.float32),
                             np.full((QK_DIM, QK_DIM), 1.0 / QK_DIM, np.float32)), BF16)

    for l in range(DEPTH):
        last = l == DEPTH - 1
        lam_init = 0.8 - 0.6 * math.exp(-0.3 * l)
        lam = (jnp.exp(jnp.sum(lam_q1[l] * lam_k1[l])) - jnp.exp(jnp.sum(lam_q2[l] * lam_k2[l])) + lam_init)
        bound = (QK_DIM * jnp.max(jnp.abs(g_q[l])) * jnp.max(jnp.abs(g_k[l]))
                 * (QK_DIM ** -0.5 * LOG2E) * ATT_BOUND_MARGIN)
        use_bound = (2.0 * bound < ATT_MAX_SHIFT_RANGE).astype(F32)
        lam = jnp.stack([lam, bound, use_bound]).astype(F32)
        mod3 = mod[l].reshape(MOD_ROWS, 1, 6 * D)
        w_in_bf = _in_weights(w_in[l])
        bias_t = jnp.repeat(b_s[l].T, A_GD, axis=1)
        wconv = jnp.concatenate([w_conv[l], jnp.zeros((5, B_WIDTH), F32)], axis=0)
        yab, q, kk, v = _in_mix(src[0], src[1], g_norm1[l], mod3, w_in_bf, cos_t, sin_t,
                                     g_v[l].reshape(1, A_WIDTH), w_s[l].astype(BF16), bias_t, wconv,
                                     jnp.tile(g_q[l], 8).reshape(1, 512), jnp.tile(g_k[l], 8).reshape(1, 512), bd)
        gsub = g_sub[l].reshape(1, V_DIM)
        coef = 1.0 - lam_init
        n_rows = NL if last else NR
        yc = _attention(lam, q, kk, v, gsub, coef, ctx_queries=False)
        yc_ctx = yc if last else _attention(lam, q, kk, v, gsub, coef, ctx_queries=True)
        wr_hi, wr_lo = _split_bf16(w_router[l].T)
        wo_bf = w_out[l].astype(BF16)
        ws_bf = (ws_gate[l].astype(BF16), ws_up[l].astype(BF16), ws_down[l].astype(BF16))
        xa, h2p, idx3, gate3, rank3, counts = _out_router(
            src[0], src[1], yab, yc, yc_ctx, wo_bf, g_norm2[l], mod3, wr_hi, wr_lo, b_router[l], n_rows)
        pos3, sched, r_pad = _route_positions(idx3, rank3, counts[:, 0], n_rows)
        x_sorted = _sc_dispatch(h2p.reshape(n_rows, 4, 128), pos3, r_pad)
        y_sorted = _experts(l, sched, x_sorted.reshape(r_pad * 4, 128), w_gate, w_up, w_down)
        yg = _sc_collect(y_sorted.reshape(r_pad, 4, 128), pos3)
        gates = gate3.transpose(0, 2, 1).reshape(n_rows, TOP_K)
        xa = _shared_residual(xa, h2p, yg.reshape(TOP_K, n_rows * 4, 128), gates, *ws_bf, mod3, n_rows)
        src = (xa, xa)
    return xa.reshape(B, L, D)
```

```python
import functools
import math

import numpy as np
import jax
import jax.numpy as jnp
from jax import lax
from jax.experimental import pallas as pl
from jax.experimental.pallas import tpu as pltpu
from jax.experimental.pallas import tpu_sc as plsc

F32 = jnp.float32
BF16 = jnp.bfloat16

D = 1024
B = 8
L = 2048
DEPTH = 2
GRID_W = 64
CTX = 256
A_WIDTH = 256
A_GROUPS = 4
A_GD = 64
CHUNK = 128
B_WIDTH = 256
C_WIDTH = 512
HEADS = 4
V_DIM = 128
QK_DIM = 64
ROPE_BASE = 10000.0
OFF_BB = 512
OFF_BC = 768
OFF_BX = 1024
OFF_Q = 1280
OFF_K = 1792
OFF_V = 2304
D_IN = 2816
N_EXPERTS = 64
TOP_K = 8
D_EXPERT = 256
D_SHARED = 256
ROUTED_SCALE = 2.5
EPS = 1e-6

NL = B * L
NC = B * CTX
NR = NL + NC
MOD_ROWS = 16
LOG2E = 1.4426950408889634

TM_IN = 512
RMS_BLOCK = 256
TQ = 2048
ATT_CHAIN = 256
ATT_GROUP = 4
ATT_BOUND_MARGIN = 1.02
ATT_MAX_SHIFT_RANGE = 100.0
TM_OUT = 512
TM_X = 1024
X_CHAINS = 2
X_TAIL_ROWS = 512
X_TAIL = 256
X_RING = 3
TM_F = 512
F_RING = 3

_DN_T = (((1,), (1,)), ((), ()))


def _cparams(sem, vmem_mb=None):
    kw = dict(dimension_semantics=sem)
    if vmem_mb is not None:
        kw["vmem_limit_bytes"] = vmem_mb * 1024 * 1024
    return pltpu.CompilerParams(**kw)


def _mod_row(i, tm):
    return jnp.where(i < NL // tm, i // (L // tm), B)


def _mod_spec(tm, chunk):
    return pl.BlockSpec((None, 1, D), lambda i: (_mod_row(i, tm), 0, chunk))


def _ada_kernel(c_ref, w_ref, b_ref, o_ref):
    c = c_ref[...]
    cs = c * jax.nn.sigmoid(c)
    o_ref[...] = jnp.dot(cs, w_ref[...], preferred_element_type=F32,
                         precision=lax.Precision.HIGHEST) + b_ref[...]


def _ada(cc, w_ada, b_ada):
    nb = 6
    return pl.pallas_call(
        _ada_kernel,
        out_shape=jax.ShapeDtypeStruct((DEPTH, MOD_ROWS, 6 * D), F32),
        grid=(DEPTH, nb),
        in_specs=[pl.BlockSpec((MOD_ROWS, D), lambda l, j: (0, 0)),
                  pl.BlockSpec((None, D, D), lambda l, j: (l, 0, j)),
                  pl.BlockSpec((None, 1, D), lambda l, j: (l, 0, j))],
        out_specs=pl.BlockSpec((None, MOD_ROWS, D), lambda l, j: (l, 0, j)),
        compiler_params=_cparams(("arbitrary", "arbitrary"), 40),
        name="ada_mod",
    )(cc, w_ada, b_ada.reshape(DEPTH, 1, 6 * D))


def _rms_mod(x, g, sc, sh):
    ms = jnp.mean(x * x, axis=-1, keepdims=True)
    return x * lax.rsqrt(ms + EPS) * (g * (1.0 + sc)) + sh


def _two_source_specs(tm, n_first, width=D):
    return [pl.BlockSpec((tm, width), lambda i: (jnp.minimum(i, n_first - 1), 0)),
            pl.BlockSpec((tm, width), lambda i: (jnp.maximum(i - n_first, 0), 0))]


def _two_source_rows(a_ref, b_ref, n_first):
    return jnp.where(pl.program_id(0) < n_first, a_ref[...], b_ref[...])


def _group_rms(t, g, bd):
    sq = (t * t).astype(BF16)
    ms = jnp.concatenate([jnp.dot(sq[:, c:c + RMS_BLOCK], bd, preferred_element_type=F32)
                          for c in range(0, t.shape[1], RMS_BLOCK)], axis=1)
    return t * lax.rsqrt(ms + EPS) * g


def _rope(t, cos, sin):
    w = t.shape[1]
    lane = lax.broadcasted_iota(jnp.int32, t.shape, 1)
    first = (lane % 32) < 16
    partner = jnp.where(first, pltpu.roll(t, w - 16, 1), pltpu.roll(t, 16, 1))
    cos4 = jnp.concatenate([cos] * (w // 128), axis=1)
    sin4 = jnp.concatenate([sin] * (w // 128), axis=1)
    return t * cos4 + partner * sin4


def _in_mix_kernel(xa_ref, xb_ref, pa_ref, pb_ref, na_ref, nb_ref, g_ref, sh_ref, sc_ref, w_ref,
                   cos_ref, sin_ref, gv_ref, ws_ref, bias_ref, wconv_ref, gq_ref, gk_ref, bd_ref,
                   yab_ref, q_ref, kk_ref, v_ref, *, n_first):
    tm = TM_IN
    i = pl.program_id(0)
    tiles_per_seq = L // tm
    is_lat = i < NL // tm
    is_start = jnp.logical_or(jnp.logical_not(is_lat), i % tiles_per_seq == 0)
    is_end = jnp.logical_or(jnp.logical_not(is_lat), i % tiles_per_seq == tiles_per_seq - 1)
    first = i < n_first
    g, sc, sh = g_ref[...], sc_ref[...], sh_ref[...]

    h = _rms_mod(jnp.where(first, xa_ref[...], xb_ref[...]), g, sc, sh)
    p = jnp.dot(h.astype(BF16), w_ref[...], preferred_element_type=F32)
    v_ref[...] = p[:, OFF_V:].astype(BF16)
    halo = jnp.concatenate([jnp.where(first, pa_ref[...], pb_ref[...]),
                            jnp.where(first, na_ref[...], nb_ref[...])], axis=0)
    ph = jnp.dot(_rms_mod(halo, g, sc, sh).astype(BF16), w_ref[:, OFF_BC:OFF_Q], preferred_element_type=F32)
    zh = ph[:, :B_WIDTH] * ph[:, B_WIDTH:]
    zp = jnp.where(is_start, 0.0, zh[15:16])
    zn = jnp.where(is_end, 0.0, zh[16:17])

    uv = p[:, 0:2 * A_WIDTH]
    uv = 0.5 * uv * (1.0 + lax.erf(uv * (2.0 ** -0.5)))
    u = uv[:, :A_WIDTH]
    v = uv[:, A_WIDTH:]
    ms = jnp.mean(v * v, axis=-1, keepdims=True)
    vb = (v * lax.rsqrt(ms + EPS) * gv_ref[...]).astype(BF16)
    lane = lax.broadcasted_iota(jnp.int32, (CHUNK, 128), 1)
    mixes = []
    for c in range(tm // CHUNK):
        vc = vb[c * CHUNK:(c + 1) * CHUNK]
        halves = []
        for j in range(2):
            vj = vc[:, j * 128:(j + 1) * 128]
            m0 = jnp.dot(ws_ref[2 * j], vj, preferred_element_type=F32)
            m1 = jnp.dot(ws_ref[2 * j + 1], vj, preferred_element_type=F32)
            halves.append(jnp.where(lane < A_GD, m0, m1))
        mixes.append(jnp.concatenate(halves, axis=1) + bias_ref[...])
    ya = u * jnp.concatenate(mixes, axis=0)

    bg = p[:, OFF_BB:OFF_BC]
    z = p[:, OFF_BC:OFF_BX] * p[:, OFF_BX:OFF_Q]
    row = lax.broadcasted_iota(jnp.int32, z.shape, 0)
    inner = jnp.logical_not(is_lat)
    z_prev = jnp.where(row == 0, zp, pltpu.roll(z, 1, 0))
    z_prev = jnp.where(jnp.logical_and(inner, row % CTX == 0), 0.0, z_prev)
    z_next = jnp.where(row == tm - 1, zn, pltpu.roll(z, tm - 1, 0))
    z_next = jnp.where(jnp.logical_and(inner, row % CTX == CTX - 1), 0.0, z_next)
    yb = bg * (z_prev * wconv_ref[0:1] + z * wconv_ref[1:2] + z_next * wconv_ref[2:3])
    yab_ref[...] = jnp.concatenate([ya, yb], axis=1).astype(BF16)

    cos = cos_ref[...]
    sin = sin_ref[...]
    bd = bd_ref[...]
    q = _rope(_group_rms(p[:, OFF_Q:OFF_K], gq_ref[...], bd), cos, sin)
    q = q * (QK_DIM ** -0.5 * LOG2E)
    q_ref[...] = q.astype(BF16)
    k = _rope(_group_rms(p[:, OFF_K:OFF_V], gk_ref[...], bd), cos, sin)
    kk_ref[...] = k.astype(BF16)


def _in_mix(x_first, x_second, g, mod3, w_bf, cos_t, sin_t, gv, ws_bf, bias_t, wconv, gq, gk, bd):
    tm = TM_IN
    n_first = x_first.shape[0] // tm
    hb = tm // 16
    nhb_first = x_first.shape[0] // 16
    nhb_second = x_second.shape[0] // 16
    pos_blocks = L // tm

    def tab_map(i):
        return (jnp.where(i < NL // tm, i % pos_blocks, pos_blocks), 0)

    def halo_specs(shift):
        blk = lambda i: (i * tm + shift) // 16
        return [pl.BlockSpec((16, D), lambda i: (jnp.clip(blk(i), 0, nhb_first - 1), 0)),
                pl.BlockSpec((16, D), lambda i: (jnp.clip(blk(i) - nhb_first, 0, nhb_second - 1), 0))]

    const2 = lambda i: (0, 0)
    row512 = pl.BlockSpec((tm, 512), lambda i: (i, 0))
    return pl.pallas_call(
        functools.partial(_in_mix_kernel, n_first=n_first),
        out_shape=[jax.ShapeDtypeStruct((NR, 512), BF16)] * 4,
        grid=(NR // tm,),
        in_specs=_two_source_specs(tm, n_first) + halo_specs(-1) + halo_specs(tm)
                 + [pl.BlockSpec((1, D), const2), _mod_spec(tm, 0), _mod_spec(tm, 1),
                    pl.BlockSpec((D, D_IN), const2),
                    pl.BlockSpec((tm, 128), tab_map), pl.BlockSpec((tm, 128), tab_map),
                    pl.BlockSpec((1, A_WIDTH), const2),
                    pl.BlockSpec((A_GROUPS, CHUNK, CHUNK), lambda i: (0, 0, 0)),
                    pl.BlockSpec((CHUNK, A_WIDTH), const2),
                    pl.BlockSpec((8, B_WIDTH), const2),
                    pl.BlockSpec((1, 512), const2), pl.BlockSpec((1, 512), const2),
                    pl.BlockSpec((RMS_BLOCK, RMS_BLOCK), const2)],
        out_specs=[row512] * 4,
        compiler_params=_cparams(("parallel",), 56),
        name="in_mix",
    )(x_first, x_second, x_first, x_second, x_first, x_second, g.reshape(1, D), mod3, mod3, w_bf,
      cos_t, sin_t, gv, ws_bf, bias_t, wconv, gq, gk, bd)


def _attn_kernel(lam_ref, q_ref, *rest, n_seg, coef, tq):
    kv_refs = rest[:2 * n_seg]
    gsub_ref, o_ref, k_scr, vt_scr = rest[2 * n_seg:]

    @pl.when(pl.program_id(2) == 0)
    def _():
        off = 0
        for s in range(n_seg):
            n = kv_refs[s].shape[0]
            k_scr[off:off + n, :] = kv_refs[s][...]
            vt_scr[0:V_DIM, off:off + n] = kv_refs[n_seg + s][...].astype(F32).T.astype(BF16)
            off += n
        ones_row = lax.broadcasted_iota(jnp.int32, (16, off), 0) == 0
        vt_scr[V_DIM:, :] = jnp.where(ones_row, 1.0, 0.0).astype(BF16)

    lam = lam_ref[0]
    shift = lam_ref[1]
    qc = ATT_CHAIN

    def scores(c):
        rows = slice(c * qc, (c + 1) * qc)
        q = q_ref[rows, :]
        lane = lax.broadcasted_iota(jnp.int32, q.shape, 1)
        zero = jnp.zeros_like(q)
        qs = jnp.concatenate([jnp.where(lane < QK_DIM, q, zero), jnp.where(lane >= QK_DIM, q, zero)], axis=0)
        return lax.dot_general(k_scr[...], qs, _DN_T, preferred_element_type=F32)

    def finish(c, pt):
        ot = jnp.dot(vt_scr[...], pt, preferred_element_type=F32)
        inv = 1.0 / ot[V_DIM:V_DIM + 1, :]
        dt = ot[0:V_DIM, :qc] * inv[:, :qc] - ot[0:V_DIM, qc:] * (lam * inv[:, qc:])
        o = dt.T
        ms = jnp.mean(o * o, axis=-1, keepdims=True)
        o_ref[c * qc:(c + 1) * qc, :] = (o * lax.rsqrt(ms + EPS) * gsub_ref[...] * coef).astype(o_ref.dtype)

    @pl.when(lam_ref[2] > 0.5)
    def _():
        for c in range(tq // qc):
            finish(c, jnp.exp2(scores(c) - shift).astype(BF16))

    @pl.when(lam_ref[2] <= 0.5)
    def _():
        for g0 in range(0, tq // qc, ATT_GROUP):
            group = range(g0, min(g0 + ATT_GROUP, tq // qc))
            sts = [scores(c) for c in group]
            for c, st in zip(group, sts):
                finish(c, jnp.exp2(st - jnp.max(st, axis=0, keepdims=True)).astype(BF16))


def _attention(lam, q, kk, v, gsub, coef, *, ctx_queries):
    if ctx_queries:
        tq = CTX
        nq, lk, n_seg = 1, CTX, 1
        q_map = lambda b, h, qi: (NL // tq + b, h)
        kv_specs = [pl.BlockSpec((CTX, 128), lambda b, h, qi: (NL // CTX + b, h)),
                    pl.BlockSpec((CTX, 128), lambda b, h, qi: (NL // CTX + b, h))]
        kv_args = [kk, v]
        rows = NC
        o_map = lambda b, h, qi: (b, h)
    else:
        tq = TQ
        nq, lk, n_seg = L // tq, CTX + L, 2
        q_map = lambda b, h, qi: (b * (L // tq) + qi, h)
        kv_specs = [pl.BlockSpec((CTX, 128), lambda b, h, qi: (NL // CTX + b, h)),
                    pl.BlockSpec((L, 128), lambda b, h, qi: (b, h)),
                    pl.BlockSpec((CTX, 128), lambda b, h, qi: (NL // CTX + b, h)),
                    pl.BlockSpec((L, 128), lambda b, h, qi: (b, h))]
        kv_args = [kk, kk, v, v]
        rows = NL
        o_map = lambda b, h, qi: (b * (L // tq) + qi, h)
    return pl.pallas_call(
        functools.partial(_attn_kernel, n_seg=n_seg, coef=coef, tq=tq),
        out_shape=jax.ShapeDtypeStruct((rows, C_WIDTH), BF16),
        grid=(B, HEADS, nq),
        in_specs=[pl.BlockSpec(memory_space=pltpu.SMEM),
                  pl.BlockSpec((tq, 128), q_map)]
                 + kv_specs + [pl.BlockSpec((1, V_DIM), lambda b, h, qi: (0, 0))],
        out_specs=pl.BlockSpec((tq, 128), o_map),
        scratch_shapes=[pltpu.VMEM((lk, 128), BF16), pltpu.VMEM((V_DIM + 16, lk), BF16)],
        compiler_params=_cparams(("parallel", "parallel", "arbitrary"), 56),
        name="attn_ctx" if ctx_queries else "attn_lat",
    )(lam, q, *kv_args, gsub)


def _pack_rows(t, out_ref, row0=0):
    half = D // 2
    w = pltpu.pack_elementwise([t[:, :half], t[:, half:]], packed_dtype=BF16)
    w = lax.bitcast_convert_type(w, jnp.uint32)
    rows = t.shape[0]
    for j in range(4):
        out_ref[pl.ds(4 * row0 + j, rows, stride=4), :] = w[:, j * 128:(j + 1) * 128]


def _unpack_rows(ref, rows, lead=None, row0=0):
    los, his = [], []
    for j in range(4):
        sl = pl.ds(4 * row0 + j, rows, stride=4)
        w = ref[sl, :] if lead is None else ref[lead, sl, :]
        los.append(pltpu.unpack_elementwise(w, index=0, packed_dtype=BF16, unpacked_dtype=F32))
        his.append(pltpu.unpack_elementwise(w, index=1, packed_dtype=BF16, unpacked_dtype=F32))
    return jnp.concatenate(los, axis=1), jnp.concatenate(his, axis=1)


def _out_router_kernel(xa_ref, xb_ref, yab_ref, yca_ref, ycb_ref, wo_ref, g1_ref, g2n_ref, sh2_ref, sc2_ref, wrh_ref,
                       wrl_ref, br_ref, xo_ref, h2p_ref, idx_ref, gate_ref, rank_ref, cnt_ref, run_ref, *,
                       n_first, n_first_c):
    tm = TM_OUT

    @pl.when(pl.program_id(0) == 0)
    def _():
        run_ref[...] = jnp.zeros_like(run_ref)

    yc = _two_source_rows(yca_ref, ycb_ref, n_first_c)
    y = jnp.dot(jnp.concatenate([yab_ref[...], yc], axis=1), wo_ref[...], preferred_element_type=F32)
    x = _two_source_rows(xa_ref, xb_ref, n_first) + g1_ref[...] * y
    xo_ref[...] = x
    h2 = _rms_mod(x, g2n_ref[...], sc2_ref[...], sh2_ref[...])
    hi = h2.astype(BF16)
    _pack_rows(h2, h2p_ref)
    lo = (h2 - hi.astype(F32)).astype(BF16)
    wh = wrh_ref[...]
    z = (lax.dot_general(wh, hi, _DN_T, preferred_element_type=F32)
         + lax.dot_general(wh, lo, _DN_T, preferred_element_type=F32)
         + lax.dot_general(wrl_ref[...], hi, _DN_T, preferred_element_type=F32))
    scores = jax.nn.sigmoid(z)
    work = scores + br_ref[...]
    eio = lax.broadcasted_iota(jnp.int32, work.shape, 0)
    idxs, sels, hits = [], [], []
    for _ in range(TOP_K):
        m = jnp.max(work, axis=0, keepdims=True)
        idx = jnp.min(jnp.where(work == m, eio, N_EXPERTS), axis=0, keepdims=True)
        hit = eio == idx
        sels.append(jnp.sum(jnp.where(hit, scores, 0.0), axis=0, keepdims=True))
        idxs.append(idx)
        hits.append(hit)
        work = jnp.where(hit, -jnp.inf, work)
    sel = jnp.concatenate(sels, axis=0)
    idx8 = jnp.concatenate(idxs, axis=0)
    gate8 = sel / jnp.sum(sel, axis=0, keepdims=True) * ROUTED_SCALE

    chosen = functools.reduce(jnp.logical_or, hits)
    before = (lax.broadcasted_iota(jnp.int32, (tm, tm), 0) < lax.broadcasted_iota(jnp.int32, (tm, tm), 1))
    prefix = jnp.dot(jnp.where(chosen, 1.0, 0.0).astype(BF16), jnp.where(before, 1.0, 0.0).astype(BF16),
                     preferred_element_type=F32)
    rank_dense = prefix + run_ref[:, 0:1]
    rank8 = jnp.concatenate([jnp.sum(jnp.where(h, rank_dense, 0.0), axis=0, keepdims=True) for h in hits],
                            axis=0).astype(jnp.int32)
    run = run_ref[...] + jnp.sum(jnp.where(chosen, 1.0, 0.0), axis=1, keepdims=True)
    run_ref[...] = run
    cnt_ref[...] = run
    for c in range(tm // 128):
        idx_ref[c] = idx8[:, c * 128:(c + 1) * 128]
        gate_ref[c] = gate8[:, c * 128:(c + 1) * 128]
        rank_ref[c] = rank8[:, c * 128:(c + 1) * 128]


def _out_router(x_first, x_second, yab, yc_first, yc_second, wo_bf, g2n, mod3, wr_hi, wr_lo, br, n_rows):
    tm = TM_OUT
    n_first = x_first.shape[0] // tm
    n_first_c = yc_first.shape[0] // tm
    const2 = lambda i: (0, 0)
    row = lambda i: (i, 0)
    chunk3 = pl.BlockSpec((tm // 128, TOP_K, 128), lambda i: (i, 0, 0))
    nch = n_rows // 128
    return pl.pallas_call(
        functools.partial(_out_router_kernel, n_first=n_first, n_first_c=n_first_c),
        out_shape=[jax.ShapeDtypeStruct((n_rows, D), F32),
                   jax.ShapeDtypeStruct((n_rows * 4, 128), jnp.uint32),
                   jax.ShapeDtypeStruct((nch, TOP_K, 128), jnp.int32),
                   jax.ShapeDtypeStruct((nch, TOP_K, 128), F32),
                   jax.ShapeDtypeStruct((nch, TOP_K, 128), jnp.int32),
                   jax.ShapeDtypeStruct((N_EXPERTS, 128), F32)],
        grid=(n_rows // tm,),
        in_specs=_two_source_specs(tm, n_first)
                 + [pl.BlockSpec((tm, 512), row)]
                 + _two_source_specs(tm, n_first_c, C_WIDTH)
                 + [pl.BlockSpec((D, D), const2),
                  _mod_spec(tm, 2),
                  pl.BlockSpec((1, D), const2), _mod_spec(tm, 3), _mod_spec(tm, 4),
                  pl.BlockSpec((N_EXPERTS, D), const2), pl.BlockSpec((N_EXPERTS, D), const2),
                  pl.BlockSpec((N_EXPERTS, 1), const2)],
        out_specs=[pl.BlockSpec((tm, D), row), pl.BlockSpec((tm * 4, 128), row),
                   chunk3, chunk3, chunk3, pl.BlockSpec((N_EXPERTS, 128), const2)],
        scratch_shapes=[pltpu.VMEM((N_EXPERTS, 128), F32)],
        compiler_params=_cparams(("arbitrary",), 48),
        name="out_router",
    )(x_first, x_second, yab, yc_first, yc_second, wo_bf, mod3, g2n.reshape(1, D), mod3, mod3, wr_hi, wr_lo,
      br.reshape(N_EXPERTS, 1))


def _experts_kernel(ts_ref, nt_ref, cnt_ref, nu_ref, x_hbm, wg_ref, wu_ref, wd_ref, y_hbm,
                    wgu_s, wd_s, x_ring, x_sems, y_buf, y_sems):
    e = pl.program_id(0)
    n_used = nu_ref[0]
    t0 = ts_ref[e]
    n_tiles = nt_ref[e]
    count = cnt_ref[e]
    ahead = X_RING - 1
    rows4 = 4 * TM_X

    def x_copy(t):
        slot = t % X_RING
        return pltpu.make_async_copy(x_hbm.at[pl.ds(t * rows4, rows4)], x_ring.at[slot], x_sems.at[slot])

    def y_copy(t):
        slot = t % 2
        return pltpu.make_async_copy(y_buf.at[slot], y_hbm.at[pl.ds(t * rows4, rows4)], y_sems.at[slot])

    @pl.when(e == 0)
    def _():
        for t in range(ahead):
            @pl.when(t < n_used)
            def _():
                x_copy(t).start()

    @pl.when(n_tiles > 0)
    def _():
        wgu_s[:, 0:D_EXPERT] = wg_ref[...].astype(BF16)
        wgu_s[:, D_EXPERT:] = wu_ref[...].astype(BF16)
        wd_s[...] = wd_ref[...].astype(BF16)

    def first_dot(x_ref, row0, rows):
        x_lo, x_hi = _unpack_rows(x_ref, rows, row0=row0)
        x = jnp.concatenate([x_lo.astype(BF16), x_hi.astype(BF16)], axis=1)
        return jnp.dot(x, wgu_s[...], preferred_element_type=F32)

    def second_dot(ab, y_ref, row0):
        a = ab[:, :D_EXPERT]
        hid = (a * jax.nn.sigmoid(a) * ab[:, D_EXPERT:]).astype(BF16)
        _pack_rows(jnp.dot(hid, wd_s[...], preferred_element_type=F32), y_ref, row0=row0)

    def tile(j, carry):
        t = t0 + j

        @pl.when(t + ahead < n_used)
        def _():
            x_copy(t + ahead).start()

        x_copy(t).wait()

        @pl.when(t >= 2)
        def _():
            y_copy(t - 2).wait()

        x_ref = x_ring.at[t % X_RING]
        y_ref = y_buf.at[t % 2]
        valid = count - j * TM_X

        @pl.when(valid > X_TAIL_ROWS)
        def _():
            rc = TM_X // X_CHAINS
            abs_ = [first_dot(x_ref, c * rc, rc) for c in range(X_CHAINS)]
            for c in range(X_CHAINS):
                second_dot(abs_[c], y_ref, c * rc)

        for c in range(X_TAIL_ROWS // X_TAIL):
            @pl.when(jnp.logical_and(valid <= X_TAIL_ROWS, c * X_TAIL < valid))
            def _():
                second_dot(first_dot(x_ref, c * X_TAIL, X_TAIL), y_ref, c * X_TAIL)

        y_copy(t).start()
        return carry

    lax.fori_loop(0, n_tiles, tile, 0)

    @pl.when(e == N_EXPERTS - 1)
    def _():
        for d in (2, 1):
            @pl.when(n_used >= d)
            def _():
                y_copy(n_used - d).wait()


def _experts(layer, sched, x_sorted, w_gate, w_up, w_down):
    tm = TM_X
    r_pad = x_sorted.shape[0] // 4
    hbm = pl.BlockSpec(memory_space=pl.ANY)
    wmap = lambda e, ts, nt, cnt, nu: (layer, e, 0, 0)
    grid_spec = pltpu.PrefetchScalarGridSpec(
        num_scalar_prefetch=4,
        grid=(N_EXPERTS,),
        in_specs=[hbm,
                  pl.BlockSpec((None, None, D, D_EXPERT), wmap),
                  pl.BlockSpec((None, None, D, D_EXPERT), wmap),
                  pl.BlockSpec((None, None, D_EXPERT, D), wmap)],
        out_specs=hbm,
        scratch_shapes=[pltpu.VMEM((D, 2 * D_EXPERT), BF16), pltpu.VMEM((D_EXPERT, D), BF16),
                        pltpu.VMEM((X_RING, tm * 4, 128), jnp.uint32), pltpu.SemaphoreType.DMA((X_RING,)),
                        pltpu.VMEM((2, tm * 4, 128), jnp.uint32), pltpu.SemaphoreType.DMA((2,))],
    )
    return pl.pallas_call(
        _experts_kernel,
        out_shape=jax.ShapeDtypeStruct((r_pad * 4, 128), jnp.uint32),
        grid_spec=grid_spec,
        compiler_params=_cparams(("arbitrary",), 48),
        name="experts",
    )(*sched, x_sorted, w_gate, w_up, w_down)


SC_CORES = 2
SC_SUBCORES = 16
SC_WORKERS = SC_CORES * SC_SUBCORES
SC_CHUNK = 128


def _sc_mesh():
    return plsc.VectorSubcoreMesh(core_axis_name="c", subcore_axis_name="s")


def _sc_params():
    return pltpu.CompilerParams(use_tc_tiling_on_sc=True)


def _sc_dispatch(h2p, pos3, r_pad):
    nch = pos3.shape[0]
    steps = -(-nch // SC_WORKERS)

    def body(h_hbm, pos_hbm, out_hbm, idx_v, rows_v, sem):
        wid = lax.axis_index("s") * SC_CORES + lax.axis_index("c")

        @pl.loop(0, steps)
        def _(s):
            ch = wid + s * SC_WORKERS

            @pl.when(ch < nch)
            def _():
                pltpu.sync_copy(pos_hbm.at[ch], idx_v)
                pltpu.sync_copy(h_hbm.at[pl.ds(ch * SC_CHUNK, SC_CHUNK)], rows_v)
                copies = [pltpu.async_copy(rows_v, out_hbm.at[idx_v.at[k]], sem) for k in range(TOP_K)]
                for cp in copies:
                    cp.wait()

    return pl.kernel(
        body,
        out_type=jax.ShapeDtypeStruct((r_pad, 4, 128), jnp.uint32),
        mesh=_sc_mesh(),
        scratch_types=[pltpu.VMEM((TOP_K, SC_CHUNK), jnp.int32),
                       pltpu.VMEM((SC_CHUNK, 4, 128), jnp.uint32),
                       pltpu.SemaphoreType.DMA],
        compiler_params=_sc_params(),
        name="sc_dispatch",
    )(h2p, pos3)


def _sc_collect(y_sorted, pos3):
    nch = pos3.shape[0]
    steps = -(-nch // SC_WORKERS)
    half = SC_CHUNK // 2
    units = [(k, hh) for k in range(TOP_K) for hh in range(2)]

    def body(y_hbm, pos_hbm, out_hbm, idx_v, rows_a, rows_b, sem_a, sem_b):
        wid = lax.axis_index("s") * SC_CORES + lax.axis_index("c")
        bufs = (rows_a, rows_b)
        sems = (sem_a, sem_b)

        def gather(u):
            k, hh = units[u]
            return pltpu.async_copy(y_hbm.at[idx_v.at[k, pl.ds(hh * half, half)]], bufs[u % 2], sems[u % 2])

        @pl.loop(0, steps)
        def _(s):
            ch = wid + s * SC_WORKERS

            @pl.when(ch < nch)
            def _():
                pltpu.sync_copy(pos_hbm.at[ch], idx_v)
                pending = gather(0)
                for u, (k, hh) in enumerate(units):
                    nxt = gather(u + 1) if u + 1 < len(units) else None
                    pending.wait()
                    pltpu.sync_copy(bufs[u % 2], out_hbm.at[k, pl.ds(ch * SC_CHUNK + hh * half, half)])
                    pending = nxt

    return pl.kernel(
        body,
        out_type=jax.ShapeDtypeStruct((TOP_K, nch * SC_CHUNK, 4, 128), jnp.uint32),
        mesh=_sc_mesh(),
        scratch_types=[pltpu.VMEM((TOP_K, SC_CHUNK), jnp.int32),
                       pltpu.VMEM((half, 4, 128), jnp.uint32),
                       pltpu.VMEM((half, 4, 128), jnp.uint32),
                       pltpu.SemaphoreType.DMA, pltpu.SemaphoreType.DMA],
        compiler_params=_sc_params(),
        name="sc_collect",
    )(y_sorted, pos3)


def _shared_kernel(x_ref, h2p_ref, yg_hbm, gate_ref, wg_ref, wu_ref, wd_ref, g2_ref, o_ref, yg_ring, yg_sems, *,
                   n_steps):
    i = pl.program_id(0)
    ahead = F_RING - 1

    def yg_copy(t):
        slot = t % F_RING
        return pltpu.make_async_copy(yg_hbm.at[:, pl.ds(t * (4 * TM_F), 4 * TM_F), :], yg_ring.at[slot],
                                     yg_sems.at[slot])

    @pl.when(i == 0)
    def _():
        for t in range(min(ahead, n_steps)):
            yg_copy(t).start()

    @pl.when(i + ahead < n_steps)
    def _():
        yg_copy(i + ahead).start()

    yg_copy(i).wait()
    yg_ref = yg_ring.at[i % F_RING]
    h_lo, h_hi = _unpack_rows(h2p_ref, TM_F)
    h = jnp.concatenate([h_lo.astype(BF16), h_hi.astype(BF16)], axis=1)
    a = jnp.dot(h, wg_ref[...], preferred_element_type=F32)
    b = jnp.dot(h, wu_ref[...], preferred_element_type=F32)
    hid = (a * jax.nn.sigmoid(a) * b).astype(BF16)
    f = jnp.dot(hid, wd_ref[...], preferred_element_type=F32)
    gate = gate_ref[...]
    f_lo = f[:, :D // 2]
    f_hi = f[:, D // 2:]
    for k in range(TOP_K):
        y_lo, y_hi = _unpack_rows(yg_ref, TM_F, lead=k)
        f_lo = f_lo + gate[:, k:k + 1] * y_lo
        f_hi = f_hi + gate[:, k:k + 1] * y_hi
    o_ref[...] = x_ref[...] + g2_ref[...] * jnp.concatenate([f_lo, f_hi], axis=1)


def _shared_residual(xa, h2p, yg, gates, wsg_bf, wsu_bf, wsd_bf, mod3, n_rows):
    tm = TM_F
    row = lambda i: (i, 0)
    const2 = lambda i: (0, 0)
    return pl.pallas_call(
        functools.partial(_shared_kernel, n_steps=n_rows // tm),
        out_shape=jax.ShapeDtypeStruct((n_rows, D), F32),
        grid=(n_rows // tm,),
        in_specs=[pl.BlockSpec((tm, D), row), pl.BlockSpec((tm * 4, 128), row),
                  pl.BlockSpec(memory_space=pl.ANY),
                  pl.BlockSpec((tm, TOP_K), row),
                  pl.BlockSpec((D, D_SHARED), const2), pl.BlockSpec((D, D_SHARED), const2),
                  pl.BlockSpec((D_SHARED, D), const2), _mod_spec(tm, 5)],
        out_specs=pl.BlockSpec((tm, D), row),
        scratch_shapes=[pltpu.VMEM((F_RING, TOP_K, tm * 4, 128), jnp.uint32), pltpu.SemaphoreType.DMA((F_RING,))],
        compiler_params=_cparams(("arbitrary",), 56),
        name="shared_residual",
    )(xa, h2p, yg, gates, wsg_bf, wsu_bf, wsd_bf, mod3)


def _positions_kernel(offs_ref, idx_ref, rank_ref, pos_ref):
    idx = idx_ref[...]
    base = jnp.zeros_like(idx)
    for e in range(N_EXPERTS):
        base = jnp.where(idx == e, offs_ref[e], base)
    pos_ref[...] = rank_ref[...] + base


def _positions(offs, idx3, rank3):
    nch = idx3.shape[0]
    cb = nch // 2
    spec = pl.BlockSpec((cb, TOP_K, 128), lambda i, offs: (i, 0, 0))
    return pl.pallas_call(
        _positions_kernel,
        out_shape=jax.ShapeDtypeStruct((nch, TOP_K, 128), jnp.int32),
        grid_spec=pltpu.PrefetchScalarGridSpec(num_scalar_prefetch=1, grid=(nch // cb,),
                                               in_specs=[spec, spec], out_specs=spec),
        compiler_params=_cparams(("parallel",)),
        name="positions",
    )(offs, idx3, rank3)


def _route_positions(idx3, rank3, counts, n_rows):
    tm = TM_X
    counts = counts.astype(jnp.int32)
    padded = ((counts + tm - 1) // tm) * tm
    ends = jnp.cumsum(padded)
    offs = ends - padded
    pos3 = _positions(offs.astype(jnp.int32), idx3, rank3)
    r_pad = n_rows * TOP_K + N_EXPERTS * tm
    n_used = (ends[-1] // tm).astype(jnp.int32).reshape(1)
    sched = ((offs // tm).astype(jnp.int32), (padded // tm).astype(jnp.int32), counts, n_used)
    return pos3, sched, r_pad


def _in_weights(w):
    def regroup(cols):
        return cols.reshape(D, 2, HEADS, QK_DIM).transpose(0, 2, 1, 3).reshape(D, 2 * HEADS * QK_DIM)
    return jnp.concatenate([w[:, :OFF_Q], regroup(w[:, OFF_Q:OFF_K]), regroup(w[:, OFF_K:OFF_V]), w[:, OFF_V:]],
                           axis=1).astype(BF16)


def _rope_tables():
    t = jnp.arange(L)
    row = (t // GRID_W).astype(F32)
    col = (t % GRID_W).astype(F32)
    n_freq = QK_DIM // 4
    inv = ROPE_BASE ** (-jnp.arange(n_freq, dtype=F32) / n_freq)
    ar = row[:, None] * inv
    ac = col[:, None] * inv
    cos64 = jnp.concatenate([jnp.cos(ar), jnp.cos(ar), jnp.cos(ac), jnp.cos(ac)], axis=1)
    sin64 = jnp.concatenate([-jnp.sin(ar), jnp.sin(ar), -jnp.sin(ac), jnp.sin(ac)], axis=1)
    cos_t = jnp.concatenate([jnp.tile(cos64, (1, 2)), jnp.ones((TM_IN, 128), F32)], axis=0)
    sin_t = jnp.concatenate([jnp.tile(sin64, (1, 2)), jnp.zeros((TM_IN, 128), F32)], axis=0)
    return cos_t, sin_t


def _split_bf16(w):
    hi = w.astype(BF16)
    return hi, (w - hi.astype(F32)).astype(BF16)


def kernel(x, c, ctx, c_ctx, w_ada, b_ada, g_norm1, g_norm2, w_in, w_out, g_v, w_s, b_s, w_conv, g_q, g_k,
           lam_q1, lam_k1, lam_q2, lam_k2, g_sub, w_router, b_router, w_gate, w_up, w_down,
           ws_gate, ws_up, ws_down):
    src = (x.reshape(NL, D), ctx.reshape(NC, D))
    cc = jnp.concatenate([c, c_ctx[None, :], jnp.zeros((MOD_ROWS - B - 1, D), F32)], axis=0)
    mod = _ada(cc, w_ada, b_ada)
    cos_t, sin_t = _rope_tables()
    bd = jnp.asarray(np.kron(np.eye(RMS_BLOCK // QK_DIM, dtype=np.float32),
                             np.full((QK_DIM, QK_DIM), 1.0 / QK_DIM, np.float32)), BF16)

    for l in range(DEPTH):
        last = l == DEPTH - 1
        lam_init = 0.8 - 0.6 * math.exp(-0.3 * l)
        lam = (jnp.exp(jnp.sum(lam_q1[l] * lam_k1[l])) - jnp.exp(jnp.sum(lam_q2[l] * lam_k2[l])) + lam_init)
        bound = (QK_DIM * jnp.max(jnp.abs(g_q[l])) * jnp.max(jnp.abs(g_k[l]))
                 * (QK_DIM ** -0.5 * LOG2E) * ATT_BOUND_MARGIN)
        use_bound = (2.0 * bound < ATT_MAX_SHIFT_RANGE).astype(F32)
        lam = jnp.stack([lam, bound, use_bound]).astype(F32)
        mod3 = mod[l].reshape(MOD_ROWS, 1, 6 * D)
        w_in_bf = _in_weights(w_in[l])
        bias_t = jnp.repeat(b_s[l].T, A_GD, axis=1)
        wconv = jnp.concatenate([w_conv[l], jnp.zeros((5, B_WIDTH), F32)], axis=0)
        yab, q, kk, v = _in_mix(src[0], src[1], g_norm1[l], mod3, w_in_bf, cos_t, sin_t,
                                     g_v[l].reshape(1, A_WIDTH), w_s[l].astype(BF16), bias_t, wconv,
                                     jnp.tile(g_q[l], 8).reshape(1, 512), jnp.tile(g_k[l], 8).reshape(1, 512), bd)
        gsub = g_sub[l].reshape(1, V_DIM)
        coef = 1.0 - lam_init
        n_rows = NL if last else NR
        yc = _attention(lam, q, kk, v, gsub, coef, ctx_queries=False)
        yc_ctx = yc if last else _attention(lam, q, kk, v, gsub, coef, ctx_queries=True)
        wr_hi, wr_lo = _split_bf16(w_router[l].T)
        wo_bf = w_out[l].astype(BF16)
        ws_bf = (ws_gate[l].astype(BF16), ws_up[l].astype(BF16), ws_down[l].astype(BF16))
        xa, h2p, idx3, gate3, rank3, counts = _out_router(
            src[0], src[1], yab, yc, yc_ctx, wo_bf, g_norm2[l], mod3, wr_hi, wr_lo, b_router[l], n_rows)
        pos3, sched, r_pad = _route_positions(idx3, rank3, counts[:, 0], n_rows)
        x_sorted = _sc_dispatch(h2p.reshape(n_rows, 4, 128), pos3, r_pad)
        y_sorted = _experts(l, sched, x_sorted.reshape(r_pad * 4, 128), w_gate, w_up, w_down)
        yg = _sc_collect(y_sorted.reshape(r_pad, 4, 128), pos3)
        gates = gate3.transpose(0, 2, 1).reshape(n_rows, TOP_K)
        xa = _shared_residual(xa, h2p, yg.reshape(TOP_K, n_rows * 4, 128), gates, *ws_bf, mod3, n_rows)
        src = (xa, xa)
    return xa.reshape(B, L, D)
```

```python
import functools
import math

import numpy as np
import jax
import jax.numpy as jnp
from jax import lax
from jax.experimental import pallas as pl
from jax.experimental.pallas import tpu as pltpu
from jax.experimental.pallas import tpu_sc as plsc

F32 = jnp.float32
BF16 = jnp.bfloat16

D = 1024
B = 8
L = 2048
DEPTH = 2
GRID_W = 64
CTX = 256
A_WIDTH = 256
A_GROUPS = 4
A_GD = 64
CHUNK = 128
B_WIDTH = 256
C_WIDTH = 512
HEADS = 4
V_DIM = 128
QK_DIM = 64
ROPE_BASE = 10000.0
OFF_BB = 512
OFF_BC = 768
OFF_BX = 1024
OFF_Q = 1280
OFF_K = 1792
OFF_V = 2304
D_IN = 2816
N_EXPERTS = 64
TOP_K = 8
D_EXPERT = 256
D_SHARED = 256
ROUTED_SCALE = 2.5
EPS = 1e-6

NL = B * L
NC = B * CTX
NR = NL + NC
MOD_ROWS = 16
LOG2E = 1.4426950408889634

TM_IN = 512
RMS_BLOCK = 256
TQ = 2048
ATT_CHAIN = 256
ATT_GROUP = 4
ATT_BOUND_MARGIN = 1.02
ATT_MAX_SHIFT_RANGE = 100.0
TM_OUT = 512
TM_X = 1024
X_CHAINS = 2
X_TAIL_ROWS = 512
X_TAIL = 256
X_RING = 3
TM_F = 512

_DN_T = (((1,), (1,)), ((), ()))


def _cparams(sem, vmem_mb=None):
    kw = dict(dimension_semantics=sem)
    if vmem_mb is not None:
        kw["vmem_limit_bytes"] = vmem_mb * 1024 * 1024
    return pltpu.CompilerParams(**kw)


def _mod_row(i, tm):
    return jnp.where(i < NL // tm, i // (L // tm), B)


def _mod_spec(tm, chunk):
    return pl.BlockSpec((None, 1, D), lambda i: (_mod_row(i, tm), 0, chunk))


def _ada_kernel(c_ref, w_ref, b_ref, o_ref):
    c = c_ref[...]
    cs = c * jax.nn.sigmoid(c)
    o_ref[...] = jnp.dot(cs, w_ref[...], preferred_element_type=F32,
                         precision=lax.Precision.HIGHEST) + b_ref[...]


def _ada(cc, w_ada, b_ada):
    nb = 6
    return pl.pallas_call(
        _ada_kernel,
        out_shape=jax.ShapeDtypeStruct((DEPTH, MOD_ROWS, 6 * D), F32),
        grid=(DEPTH, nb),
        in_specs=[pl.BlockSpec((MOD_ROWS, D), lambda l, j: (0, 0)),
                  pl.BlockSpec((None, D, D), lambda l, j: (l, 0, j)),
                  pl.BlockSpec((None, 1, D), lambda l, j: (l, 0, j))],
        out_specs=pl.BlockSpec((None, MOD_ROWS, D), lambda l, j: (l, 0, j)),
        compiler_params=_cparams(("arbitrary", "arbitrary"), 40),
        name="ada_mod",
    )(cc, w_ada, b_ada.reshape(DEPTH, 1, 6 * D))


def _rms_mod(x, g, sc, sh):
    ms = jnp.mean(x * x, axis=-1, keepdims=True)
    return x * lax.rsqrt(ms + EPS) * (g * (1.0 + sc)) + sh


def _two_source_specs(tm, n_first, width=D):
    return [pl.BlockSpec((tm, width), lambda i: (jnp.minimum(i, n_first - 1), 0)),
            pl.BlockSpec((tm, width), lambda i: (jnp.maximum(i - n_first, 0), 0))]


def _two_source_rows(a_ref, b_ref, n_first):
    return jnp.where(pl.program_id(0) < n_first, a_ref[...], b_ref[...])


def _group_rms(t, g, bd):
    sq = (t * t).astype(BF16)
    ms = jnp.concatenate([jnp.dot(sq[:, c:c + RMS_BLOCK], bd, preferred_element_type=F32)
                          for c in range(0, t.shape[1], RMS_BLOCK)], axis=1)
    return t * lax.rsqrt(ms + EPS) * g


def _rope(t, cos, sin):
    w = t.shape[1]
    lane = lax.broadcasted_iota(jnp.int32, t.shape, 1)
    first = (lane % 32) < 16
    partner = jnp.where(first, pltpu.roll(t, w - 16, 1), pltpu.roll(t, 16, 1))
    cos4 = jnp.concatenate([cos] * (w // 128), axis=1)
    sin4 = jnp.concatenate([sin] * (w // 128), axis=1)
    return t * cos4 + partner * sin4


def _in_mix_kernel(xa_ref, xb_ref, pa_ref, pb_ref, na_ref, nb_ref, g_ref, sh_ref, sc_ref, w_ref,
                   cos_ref, sin_ref, gv_ref, ws_ref, bias_ref, wconv_ref, gq_ref, gk_ref, bd_ref,
                   yab_ref, q_ref, kk_ref, v_ref, *, n_first):
    tm = TM_IN
    i = pl.program_id(0)
    tiles_per_seq = L // tm
    is_lat = i < NL // tm
    is_start = jnp.logical_or(jnp.logical_not(is_lat), i % tiles_per_seq == 0)
    is_end = jnp.logical_or(jnp.logical_not(is_lat), i % tiles_per_seq == tiles_per_seq - 1)
    first = i < n_first
    g, sc, sh = g_ref[...], sc_ref[...], sh_ref[...]

    h = _rms_mod(jnp.where(first, xa_ref[...], xb_ref[...]), g, sc, sh)
    p = jnp.dot(h.astype(BF16), w_ref[...], preferred_element_type=F32)
    v_ref[...] = p[:, OFF_V:].astype(BF16)
    halo = jnp.concatenate([jnp.where(first, pa_ref[...], pb_ref[...]),
                            jnp.where(first, na_ref[...], nb_ref[...])], axis=0)
    ph = jnp.dot(_rms_mod(halo, g, sc, sh).astype(BF16), w_ref[:, OFF_BC:OFF_Q], preferred_element_type=F32)
    zh = ph[:, :B_WIDTH] * ph[:, B_WIDTH:]
    zp = jnp.where(is_start, 0.0, zh[15:16])
    zn = jnp.where(is_end, 0.0, zh[16:17])

    uv = p[:, 0:2 * A_WIDTH]
    uv = 0.5 * uv * (1.0 + lax.erf(uv * (2.0 ** -0.5)))
    u = uv[:, :A_WIDTH]
    v = uv[:, A_WIDTH:]
    ms = jnp.mean(v * v, axis=-1, keepdims=True)
    vb = (v * lax.rsqrt(ms + EPS) * gv_ref[...]).astype(BF16)
    lane = lax.broadcasted_iota(jnp.int32, (CHUNK, 128), 1)
    mixes = []
    for c in range(tm // CHUNK):
        vc = vb[c * CHUNK:(c + 1) * CHUNK]
        halves = []
        for j in range(2):
            vj = vc[:, j * 128:(j + 1) * 128]
            m0 = jnp.dot(ws_ref[2 * j], vj, preferred_element_type=F32)
            m1 = jnp.dot(ws_ref[2 * j + 1], vj, preferred_element_type=F32)
            halves.append(jnp.where(lane < A_GD, m0, m1))
        mixes.append(jnp.concatenate(halves, axis=1) + bias_ref[...])
    ya = u * jnp.concatenate(mixes, axis=0)

    bg = p[:, OFF_BB:OFF_BC]
    z = p[:, OFF_BC:OFF_BX] * p[:, OFF_BX:OFF_Q]
    row = lax.broadcasted_iota(jnp.int32, z.shape, 0)
    inner = jnp.logical_not(is_lat)
    z_prev = jnp.where(row == 0, zp, pltpu.roll(z, 1, 0))
    z_prev = jnp.where(jnp.logical_and(inner, row % CTX == 0), 0.0, z_prev)
    z_next = jnp.where(row == tm - 1, zn, pltpu.roll(z, tm - 1, 0))
    z_next = jnp.where(jnp.logical_and(inner, row % CTX == CTX - 1), 0.0, z_next)
    yb = bg * (z_prev * wconv_ref[0:1] + z * wconv_ref[1:2] + z_next * wconv_ref[2:3])
    yab_ref[...] = jnp.concatenate([ya, yb], axis=1).astype(BF16)

    cos = cos_ref[...]
    sin = sin_ref[...]
    bd = bd_ref[...]
    q = _rope(_group_rms(p[:, OFF_Q:OFF_K], gq_ref[...], bd), cos, sin)
    q = q * (QK_DIM ** -0.5 * LOG2E)
    q_ref[...] = q.astype(BF16)
    k = _rope(_group_rms(p[:, OFF_K:OFF_V], gk_ref[...], bd), cos, sin)
    kk_ref[...] = k.astype(BF16)


def _in_mix(x_first, x_second, g, mod3, w_bf, cos_t, sin_t, gv, ws_bf, bias_t, wconv, gq, gk, bd):
    tm = TM_IN
    n_first = x_first.shape[0] // tm
    hb = tm // 16
    nhb_first = x_first.shape[0] // 16
    nhb_second = x_second.shape[0] // 16
    pos_blocks = L // tm

    def tab_map(i):
        return (jnp.where(i < NL // tm, i % pos_blocks, pos_blocks), 0)

    def halo_specs(shift):
        blk = lambda i: (i * tm + shift) // 16
        return [pl.BlockSpec((16, D), lambda i: (jnp.clip(blk(i), 0, nhb_first - 1), 0)),
                pl.BlockSpec((16, D), lambda i: (jnp.clip(blk(i) - nhb_first, 0, nhb_second - 1), 0))]

    const2 = lambda i: (0, 0)
    row512 = pl.BlockSpec((tm, 512), lambda i: (i, 0))
    return pl.pallas_call(
        functools.partial(_in_mix_kernel, n_first=n_first),
        out_shape=[jax.ShapeDtypeStruct((NR, 512), BF16)] * 4,
        grid=(NR // tm,),
        in_specs=_two_source_specs(tm, n_first) + halo_specs(-1) + halo_specs(tm)
                 + [pl.BlockSpec((1, D), const2), _mod_spec(tm, 0), _mod_spec(tm, 1),
                    pl.BlockSpec((D, D_IN), const2),
                    pl.BlockSpec((tm, 128), tab_map), pl.BlockSpec((tm, 128), tab_map),
                    pl.BlockSpec((1, A_WIDTH), const2),
                    pl.BlockSpec((A_GROUPS, CHUNK, CHUNK), lambda i: (0, 0, 0)),
                    pl.BlockSpec((CHUNK, A_WIDTH), const2),
                    pl.BlockSpec((8, B_WIDTH), const2),
                    pl.BlockSpec((1, 512), const2), pl.BlockSpec((1, 512), const2),
                    pl.BlockSpec((RMS_BLOCK, RMS_BLOCK), const2)],
        out_specs=[row512] * 4,
        compiler_params=_cparams(("parallel",), 56),
        name="in_mix",
    )(x_first, x_second, x_first, x_second, x_first, x_second, g.reshape(1, D), mod3, mod3, w_bf,
      cos_t, sin_t, gv, ws_bf, bias_t, wconv, gq, gk, bd)


def _attn_kernel(lam_ref, q_ref, *rest, n_seg, coef, tq):
    kv_refs = rest[:2 * n_seg]
    gsub_ref, o_ref, k_scr, vt_scr = rest[2 * n_seg:]

    @pl.when(pl.program_id(2) == 0)
    def _():
        off = 0
        for s in range(n_seg):
            n = kv_refs[s].shape[0]
            k_scr[off:off + n, :] = kv_refs[s][...]
            vt_scr[0:V_DIM, off:off + n] = kv_refs[n_seg + s][...].astype(F32).T.astype(BF16)
            off += n
        ones_row = lax.broadcasted_iota(jnp.int32, (16, off), 0) == 0
        vt_scr[V_DIM:, :] = jnp.where(ones_row, 1.0, 0.0).astype(BF16)

    lam = lam_ref[0]
    shift = lam_ref[1]
    qc = ATT_CHAIN

    def scores(c):
        rows = slice(c * qc, (c + 1) * qc)
        q = q_ref[rows, :]
        lane = lax.broadcasted_iota(jnp.int32, q.shape, 1)
        zero = jnp.zeros_like(q)
        qs = jnp.concatenate([jnp.where(lane < QK_DIM, q, zero), jnp.where(lane >= QK_DIM, q, zero)], axis=0)
        return lax.dot_general(k_scr[...], qs, _DN_T, preferred_element_type=F32)

    def finish(c, pt):
        ot = jnp.dot(vt_scr[...], pt, preferred_element_type=F32)
        inv = 1.0 / ot[V_DIM:V_DIM + 1, :]
        dt = ot[0:V_DIM, :qc] * inv[:, :qc] - ot[0:V_DIM, qc:] * (lam * inv[:, qc:])
        o = dt.T
        ms = jnp.mean(o * o, axis=-1, keepdims=True)
        o_ref[c * qc:(c + 1) * qc, :] = (o * lax.rsqrt(ms + EPS) * gsub_ref[...] * coef).astype(o_ref.dtype)

    @pl.when(lam_ref[2] > 0.5)
    def _():
        for c in range(tq // qc):
            finish(c, jnp.exp2(scores(c) - shift).astype(BF16))

    @pl.when(lam_ref[2] <= 0.5)
    def _():
        for g0 in range(0, tq // qc, ATT_GROUP):
            group = range(g0, min(g0 + ATT_GROUP, tq // qc))
            sts = [scores(c) for c in group]
            for c, st in zip(group, sts):
                finish(c, jnp.exp2(st - jnp.max(st, axis=0, keepdims=True)).astype(BF16))


def _attention(lam, q, kk, v, gsub, coef, *, ctx_queries):
    if ctx_queries:
        tq = CTX
        nq, lk, n_seg = 1, CTX, 1
        q_map = lambda b, h, qi: (NL // tq + b, h)
        kv_specs = [pl.BlockSpec((CTX, 128), lambda b, h, qi: (NL // CTX + b, h)),
                    pl.BlockSpec((CTX, 128), lambda b, h, qi: (NL // CTX + b, h))]
        kv_args = [kk, v]
        rows = NC
        o_map = lambda b, h, qi: (b, h)
    else:
        tq = TQ
        nq, lk, n_seg = L // tq, CTX + L, 2
        q_map = lambda b, h, qi: (b * (L // tq) + qi, h)
        kv_specs = [pl.BlockSpec((CTX, 128), lambda b, h, qi: (NL // CTX + b, h)),
                    pl.BlockSpec((L, 128), lambda b, h, qi: (b, h)),
                    pl.BlockSpec((CTX, 128), lambda b, h, qi: (NL // CTX + b, h)),
                    pl.BlockSpec((L, 128), lambda b, h, qi: (b, h))]
        kv_args = [kk, kk, v, v]
        rows = NL
        o_map = lambda b, h, qi: (b * (L // tq) + qi, h)
    return pl.pallas_call(
        functools.partial(_attn_kernel, n_seg=n_seg, coef=coef, tq=tq),
        out_shape=jax.ShapeDtypeStruct((rows, C_WIDTH), BF16),
        grid=(B, HEADS, nq),
        in_specs=[pl.BlockSpec(memory_space=pltpu.SMEM),
                  pl.BlockSpec((tq, 128), q_map)]
                 + kv_specs + [pl.BlockSpec((1, V_DIM), lambda b, h, qi: (0, 0))],
        out_specs=pl.BlockSpec((tq, 128), o_map),
        scratch_shapes=[pltpu.VMEM((lk, 128), BF16), pltpu.VMEM((V_DIM + 16, lk), BF16)],
        compiler_params=_cparams(("parallel", "parallel", "arbitrary"), 56),
        name="attn_ctx" if ctx_queries else "attn_lat",
    )(lam, q, *kv_args, gsub)


def _pack_rows(t, out_ref, row0=0):
    half = D // 2
    w = pltpu.pack_elementwise([t[:, :half], t[:, half:]], packed_dtype=BF16)
    w = lax.bitcast_convert_type(w, jnp.uint32)
    rows = t.shape[0]
    for j in range(4):
        out_ref[pl.ds(4 * row0 + j, rows, stride=4), :] = w[:, j * 128:(j + 1) * 128]


def _unpack_rows(ref, rows, lead=None, row0=0):
    los, his = [], []
    for j in range(4):
        sl = pl.ds(4 * row0 + j, rows, stride=4)
        w = ref[sl, :] if lead is None else ref[lead, sl, :]
        los.append(pltpu.unpack_elementwise(w, index=0, packed_dtype=BF16, unpacked_dtype=F32))
        his.append(pltpu.unpack_elementwise(w, index=1, packed_dtype=BF16, unpacked_dtype=F32))
    return jnp.concatenate(los, axis=1), jnp.concatenate(his, axis=1)


def _out_router_kernel(xa_ref, xb_ref, yab_ref, yca_ref, ycb_ref, wo_ref, g1_ref, g2n_ref, sh2_ref, sc2_ref, wrh_ref,
                       wrl_ref, br_ref, xo_ref, h2p_ref, idx_ref, gate_ref, rank_ref, cnt_ref, run_ref, *,
                       n_first, n_first_c):
    tm = TM_OUT

    @pl.when(pl.program_id(0) == 0)
    def _():
        run_ref[...] = jnp.zeros_like(run_ref)

    yc = _two_source_rows(yca_ref, ycb_ref, n_first_c)
    y = jnp.dot(jnp.concatenate([yab_ref[...], yc], axis=1), wo_ref[...], preferred_element_type=F32)
    x = _two_source_rows(xa_ref, xb_ref, n_first) + g1_ref[...] * y
    xo_ref[...] = x
    h2 = _rms_mod(x, g2n_ref[...], sc2_ref[...], sh2_ref[...])
    hi = h2.astype(BF16)
    _pack_rows(h2, h2p_ref)
    lo = (h2 - hi.astype(F32)).astype(BF16)
    wh = wrh_ref[...]
    z = (lax.dot_general(wh, hi, _DN_T, preferred_element_type=F32)
         + lax.dot_general(wh, lo, _DN_T, preferred_element_type=F32)
         + lax.dot_general(wrl_ref[...], hi, _DN_T, preferred_element_type=F32))
    scores = jax.nn.sigmoid(z)
    work = scores + br_ref[...]
    eio = lax.broadcasted_iota(jnp.int32, work.shape, 0)
    idxs, sels, hits = [], [], []
    for _ in range(TOP_K):
        m = jnp.max(work, axis=0, keepdims=True)
        idx = jnp.min(jnp.where(work == m, eio, N_EXPERTS), axis=0, keepdims=True)
        hit = eio == idx
        sels.append(jnp.sum(jnp.where(hit, scores, 0.0), axis=0, keepdims=True))
        idxs.append(idx)
        hits.append(hit)
        work = jnp.where(hit, -jnp.inf, work)
    sel = jnp.concatenate(sels, axis=0)
    idx8 = jnp.concatenate(idxs, axis=0)
    gate8 = sel / jnp.sum(sel, axis=0, keepdims=True) * ROUTED_SCALE

    chosen = functools.reduce(jnp.logical_or, hits)
    before = (lax.broadcasted_iota(jnp.int32, (tm, tm), 0) < lax.broadcasted_iota(jnp.int32, (tm, tm), 1))
    prefix = jnp.dot(jnp.where(chosen, 1.0, 0.0).astype(BF16), jnp.where(before, 1.0, 0.0).astype(BF16),
                     preferred_element_type=F32)
    rank_dense = prefix + run_ref[:, 0:1]
    rank8 = jnp.concatenate([jnp.sum(jnp.where(h, rank_dense, 0.0), axis=0, keepdims=True) for h in hits],
                            axis=0).astype(jnp.int32)
    run = run_ref[...] + jnp.sum(jnp.where(chosen, 1.0, 0.0), axis=1, keepdims=True)
    run_ref[...] = run
    cnt_ref[...] = run
    for c in range(tm // 128):
        idx_ref[c] = idx8[:, c * 128:(c + 1) * 128]
        gate_ref[c] = gate8[:, c * 128:(c + 1) * 128]
        rank_ref[c] = rank8[:, c * 128:(c + 1) * 128]


def _out_router(x_first, x_second, yab, yc_first, yc_second, wo_bf, g2n, mod3, wr_hi, wr_lo, br, n_rows):
    tm = TM_OUT
    n_first = x_first.shape[0] // tm
    n_first_c = yc_first.shape[0] // tm
    const2 = lambda i: (0, 0)
    row = lambda i: (i, 0)
    chunk3 = pl.BlockSpec((tm // 128, TOP_K, 128), lambda i: (i, 0, 0))
    nch = n_rows // 128
    return pl.pallas_call(
        functools.partial(_out_router_kernel, n_first=n_first, n_first_c=n_first_c),
        out_shape=[jax.ShapeDtypeStruct((n_rows, D), F32),
                   jax.ShapeDtypeStruct((n_rows * 4, 128), jnp.uint32),
                   jax.ShapeDtypeStruct((nch, TOP_K, 128), jnp.int32),
                   jax.ShapeDtypeStruct((nch, TOP_K, 128), F32),
                   jax.ShapeDtypeStruct((nch, TOP_K, 128), jnp.int32),
                   jax.ShapeDtypeStruct((N_EXPERTS, 128), F32)],
        grid=(n_rows // tm,),
        in_specs=_two_source_specs(tm, n_first)
                 + [pl.BlockSpec((tm, 512), row)]
                 + _two_source_specs(tm, n_first_c, C_WIDTH)
                 + [pl.BlockSpec((D, D), const2),
                  _mod_spec(tm, 2),
                  pl.BlockSpec((1, D), const2), _mod_spec(tm, 3), _mod_spec(tm, 4),
                  pl.BlockSpec((N_EXPERTS, D), const2), pl.BlockSpec((N_EXPERTS, D), const2),
                  pl.BlockSpec((N_EXPERTS, 1), const2)],
        out_specs=[pl.BlockSpec((tm, D), row), pl.BlockSpec((tm * 4, 128), row),
                   chunk3, chunk3, chunk3, pl.BlockSpec((N_EXPERTS, 128), const2)],
        scratch_shapes=[pltpu.VMEM((N_EXPERTS, 128), F32)],
        compiler_params=_cparams(("arbitrary",), 48),
        name="out_router",
    )(x_first, x_second, yab, yc_first, yc_second, wo_bf, mod3, g2n.reshape(1, D), mod3, mod3, wr_hi, wr_lo,
      br.reshape(N_EXPERTS, 1))


def _experts_kernel(ts_ref, nt_ref, cnt_ref, nu_ref, x_hbm, wg_ref, wu_ref, wd_ref, y_hbm,
                    wgu_s, wd_s, x_ring, x_sems, y_buf, y_sems, nb_s):
    e = pl.program_id(0)
    n_used = nu_ref[0]
    t0 = ts_ref[e]
    n_tiles = nt_ref[e]
    count = cnt_ref[e]
    ahead = X_RING - 1
    rows4 = 4 * TM_X
    blocks = TM_X // X_TAIL
    brows4 = 4 * X_TAIL

    def x_copy(t, b):
        slot = t % X_RING
        return pltpu.make_async_copy(x_hbm.at[pl.ds(t * rows4 + b * brows4, brows4)],
                                     x_ring.at[slot, pl.ds(b * brows4, brows4)], x_sems.at[slot, b])

    def y_copy(t, b):
        slot = t % 2
        return pltpu.make_async_copy(y_buf.at[slot, pl.ds(b * brows4, brows4)],
                                     y_hbm.at[pl.ds(t * rows4 + b * brows4, brows4)], y_sems.at[slot, b])

    def for_blocks(t, fn):
        n = nb_s[t]
        for b in range(blocks):
            @pl.when(b < n)
            def _():
                fn(b)

    @pl.when(e == 0)
    def _():
        def fill(e2, carry):
            def one(j, c):
                left = cnt_ref[e2] - j * TM_X
                nb_s[ts_ref[e2] + j] = jnp.minimum(lax.div(left + (X_TAIL - 1), jnp.int32(X_TAIL)), blocks)
                return c
            return lax.fori_loop(0, nt_ref[e2], one, carry)
        lax.fori_loop(0, N_EXPERTS, fill, 0)

        for t in range(ahead):
            @pl.when(t < n_used)
            def _():
                for_blocks(t, lambda b: x_copy(t, b).start())

    @pl.when(n_tiles > 0)
    def _():
        wgu_s[:, 0:D_EXPERT] = wg_ref[...].astype(BF16)
        wgu_s[:, D_EXPERT:] = wu_ref[...].astype(BF16)
        wd_s[...] = wd_ref[...].astype(BF16)

    def first_dot(x_ref, row0, rows):
        x_lo, x_hi = _unpack_rows(x_ref, rows, row0=row0)
        x = jnp.concatenate([x_lo.astype(BF16), x_hi.astype(BF16)], axis=1)
        return jnp.dot(x, wgu_s[...], preferred_element_type=F32)

    def second_dot(ab, y_ref, row0):
        a = ab[:, :D_EXPERT]
        hid = (a * jax.nn.sigmoid(a) * ab[:, D_EXPERT:]).astype(BF16)
        _pack_rows(jnp.dot(hid, wd_s[...], preferred_element_type=F32), y_ref, row0=row0)

    def tile(j, carry):
        t = t0 + j

        @pl.when(t + ahead < n_used)
        def _():
            for_blocks(t + ahead, lambda b: x_copy(t + ahead, b).start())

        for_blocks(t, lambda b: x_copy(t, b).wait())

        @pl.when(t >= 2)
        def _():
            for_blocks(t - 2, lambda b: y_copy(t - 2, b).wait())

        x_ref = x_ring.at[t % X_RING]
        y_ref = y_buf.at[t % 2]
        valid = count - j * TM_X

        @pl.when(valid > X_TAIL_ROWS)
        def _():
            rc = TM_X // X_CHAINS
            abs_ = [first_dot(x_ref, c * rc, rc) for c in range(X_CHAINS)]
            for c in range(X_CHAINS):
                second_dot(abs_[c], y_ref, c * rc)

        for c in range(X_TAIL_ROWS // X_TAIL):
            @pl.when(jnp.logical_and(valid <= X_TAIL_ROWS, c * X_TAIL < valid))
            def _():
                second_dot(first_dot(x_ref, c * X_TAIL, X_TAIL), y_ref, c * X_TAIL)

        for_blocks(t, lambda b: y_copy(t, b).start())
        return carry

    lax.fori_loop(0, n_tiles, tile, 0)

    @pl.when(e == N_EXPERTS - 1)
    def _():
        for d in (2, 1):
            @pl.when(n_used >= d)
            def _():
                for_blocks(n_used - d, lambda b: y_copy(n_used - d, b).wait())


def _experts(layer, sched, x_sorted, w_gate, w_up, w_down):
    tm = TM_X
    r_pad = x_sorted.shape[0] // 4
    hbm = pl.BlockSpec(memory_space=pl.ANY)
    wmap = lambda e, ts, nt, cnt, nu: (layer, e, 0, 0)
    grid_spec = pltpu.PrefetchScalarGridSpec(
        num_scalar_prefetch=4,
        grid=(N_EXPERTS,),
        in_specs=[hbm,
                  pl.BlockSpec((None, None, D, D_EXPERT), wmap),
                  pl.BlockSpec((None, None, D, D_EXPERT), wmap),
                  pl.BlockSpec((None, None, D_EXPERT, D), wmap)],
        out_specs=hbm,
        scratch_shapes=[pltpu.VMEM((D, 2 * D_EXPERT), BF16), pltpu.VMEM((D_EXPERT, D), BF16),
                        pltpu.VMEM((X_RING, tm * 4, 128), jnp.uint32),
                        pltpu.SemaphoreType.DMA((X_RING, tm // X_TAIL)),
                        pltpu.VMEM((2, tm * 4, 128), jnp.uint32), pltpu.SemaphoreType.DMA((2, tm // X_TAIL)),
                        pltpu.SMEM((r_pad // tm,), jnp.int32)],
    )
    return pl.pallas_call(
        _experts_kernel,
        out_shape=jax.ShapeDtypeStruct((r_pad * 4, 128), jnp.uint32),
        grid_spec=grid_spec,
        compiler_params=_cparams(("arbitrary",), 48),
        name="experts",
    )(*sched, x_sorted, w_gate, w_up, w_down)


SC_CORES = 2
SC_SUBCORES = 16
SC_WORKERS = SC_CORES * SC_SUBCORES
SC_CHUNK = 128


def _sc_mesh():
    return plsc.VectorSubcoreMesh(core_axis_name="c", subcore_axis_name="s")


def _sc_params():
    return pltpu.CompilerParams(use_tc_tiling_on_sc=True)


def _sc_dispatch(h2p, pos3, r_pad):
    nch = pos3.shape[0]
    steps = -(-nch // SC_WORKERS)

    def body(h_hbm, pos_hbm, out_hbm, idx_v, rows_v, sem):
        wid = lax.axis_index("s") * SC_CORES + lax.axis_index("c")

        @pl.loop(0, steps)
        def _(s):
            ch = wid + s * SC_WORKERS

            @pl.when(ch < nch)
            def _():
                pltpu.sync_copy(pos_hbm.at[ch], idx_v)
                pltpu.sync_copy(h_hbm.at[pl.ds(ch * SC_CHUNK, SC_CHUNK)], rows_v)
                copies = [pltpu.async_copy(rows_v, out_hbm.at[idx_v.at[k]], sem) for k in range(TOP_K)]
                for cp in copies:
                    cp.wait()

    return pl.kernel(
        body,
        out_type=jax.ShapeDtypeStruct((r_pad, 4, 128), jnp.uint32),
        mesh=_sc_mesh(),
        scratch_types=[pltpu.VMEM((TOP_K, SC_CHUNK), jnp.int32),
                       pltpu.VMEM((SC_CHUNK, 4, 128), jnp.uint32),
                       pltpu.SemaphoreType.DMA],
        compiler_params=_sc_params(),
        name="sc_dispatch",
    )(h2p, pos3)


def _sc_collect(y_sorted, pos3):
    nch = pos3.shape[0]
    steps = -(-nch // SC_WORKERS)
    half = SC_CHUNK // 2
    units = [(k, hh) for k in range(TOP_K) for hh in range(2)]

    def body(y_hbm, pos_hbm, out_hbm, idx_v, rows_a, rows_b, sem_a, sem_b):
        wid = lax.axis_index("s") * SC_CORES + lax.axis_index("c")
        bufs = (rows_a, rows_b)
        sems = (sem_a, sem_b)

        def gather(u):
            k, hh = units[u]
            return pltpu.async_copy(y_hbm.at[idx_v.at[k, pl.ds(hh * half, half)]], bufs[u % 2], sems[u % 2])

        @pl.loop(0, steps)
        def _(s):
            ch = wid + s * SC_WORKERS

            @pl.when(ch < nch)
            def _():
                pltpu.sync_copy(pos_hbm.at[ch], idx_v)
                pending = gather(0)
                for u, (k, hh) in enumerate(units):
                    nxt = gather(u + 1) if u + 1 < len(units) else None
                    pending.wait()
                    pltpu.sync_copy(bufs[u % 2], out_hbm.at[k, pl.ds(ch * SC_CHUNK + hh * half, half)])
                    pending = nxt

    return pl.kernel(
        body,
        out_type=jax.ShapeDtypeStruct((TOP_K, nch * SC_CHUNK, 4, 128), jnp.uint32),
        mesh=_sc_mesh(),
        scratch_types=[pltpu.VMEM((TOP_K, SC_CHUNK), jnp.int32),
                       pltpu.VMEM((half, 4, 128), jnp.uint32),
                       pltpu.VMEM((half, 4, 128), jnp.uint32),
                       pltpu.SemaphoreType.DMA, pltpu.SemaphoreType.DMA],
        compiler_params=_sc_params(),
        name="sc_collect",
    )(y_sorted, pos3)


def _shared_kernel(x_ref, h2p_ref, yg_ref, gate_ref, wg_ref, wu_ref, wd_ref, g2_ref, o_ref):
    h_lo, h_hi = _unpack_rows(h2p_ref, TM_F)
    h = jnp.concatenate([h_lo.astype(BF16), h_hi.astype(BF16)], axis=1)
    a = jnp.dot(h, wg_ref[...], preferred_element_type=F32)
    b = jnp.dot(h, wu_ref[...], preferred_element_type=F32)
    hid = (a * jax.nn.sigmoid(a) * b).astype(BF16)
    f = jnp.dot(hid, wd_ref[...], preferred_element_type=F32)
    gate = gate_ref[...]
    f_lo = f[:, :D // 2]
    f_hi = f[:, D // 2:]
    for k in range(TOP_K):
        y_lo, y_hi = _unpack_rows(yg_ref, TM_F, lead=k)
        f_lo = f_lo + gate[:, k:k + 1] * y_lo
        f_hi = f_hi + gate[:, k:k + 1] * y_hi
    o_ref[...] = x_ref[...] + g2_ref[...] * jnp.concatenate([f_lo, f_hi], axis=1)


def _shared_residual(xa, h2p, yg, gates, wsg_bf, wsu_bf, wsd_bf, mod3, n_rows):
    tm = TM_F
    row = lambda i: (i, 0)
    const2 = lambda i: (0, 0)
    return pl.pallas_call(
        _shared_kernel,
        out_shape=jax.ShapeDtypeStruct((n_rows, D), F32),
        grid=(n_rows // tm,),
        in_specs=[pl.BlockSpec((tm, D), row), pl.BlockSpec((tm * 4, 128), row),
                  pl.BlockSpec((TOP_K, tm * 4, 128), lambda i: (0, i, 0)),
                  pl.BlockSpec((tm, TOP_K), row),
                  pl.BlockSpec((D, D_SHARED), const2), pl.BlockSpec((D, D_SHARED), const2),
                  pl.BlockSpec((D_SHARED, D), const2), _mod_spec(tm, 5)],
        out_specs=pl.BlockSpec((tm, D), row),
        compiler_params=_cparams(("parallel",), 48),
        name="shared_residual",
    )(xa, h2p, yg, gates, wsg_bf, wsu_bf, wsd_bf, mod3)


def _positions_kernel(offs_ref, idx_ref, rank_ref, pos_ref):
    idx = idx_ref[...]
    base = jnp.zeros_like(idx)
    for e in range(N_EXPERTS):
        base = jnp.where(idx == e, offs_ref[e], base)
    pos_ref[...] = rank_ref[...] + base


def _positions(offs, idx3, rank3):
    nch = idx3.shape[0]
    cb = nch // 2
    spec = pl.BlockSpec((cb, TOP_K, 128), lambda i, offs: (i, 0, 0))
    return pl.pallas_call(
        _positions_kernel,
        out_shape=jax.ShapeDtypeStruct((nch, TOP_K, 128), jnp.int32),
        grid_spec=pltpu.PrefetchScalarGridSpec(num_scalar_prefetch=1, grid=(nch // cb,),
                                               in_specs=[spec, spec], out_specs=spec),
        compiler_params=_cparams(("parallel",)),
        name="positions",
    )(offs, idx3, rank3)


def _route_positions(idx3, rank3, counts, n_rows):
    tm = TM_X
    counts = counts.astype(jnp.int32)
    padded = ((counts + tm - 1) // tm) * tm
    ends = jnp.cumsum(padded)
    offs = ends - padded
    pos3 = _positions(offs.astype(jnp.int32), idx3, rank3)
    r_pad = n_rows * TOP_K + N_EXPERTS * tm
    n_used = (ends[-1] // tm).astype(jnp.int32).reshape(1)
    sched = ((offs // tm).astype(jnp.int32), (padded // tm).astype(jnp.int32), counts, n_used)
    return pos3, sched, r_pad


def _in_weights(w):
    def regroup(cols):
        return cols.reshape(D, 2, HEADS, QK_DIM).transpose(0, 2, 1, 3).reshape(D, 2 * HEADS * QK_DIM)
    return jnp.concatenate([w[:, :OFF_Q], regroup(w[:, OFF_Q:OFF_K]), regroup(w[:, OFF_K:OFF_V]), w[:, OFF_V:]],
                           axis=1).astype(BF16)


def _rope_tables():
    t = jnp.arange(L)
    row = (t // GRID_W).astype(F32)
    col = (t % GRID_W).astype(F32)
    n_freq = QK_DIM // 4
    inv = ROPE_BASE ** (-jnp.arange(n_freq, dtype=F32) / n_freq)
    ar = row[:, None] * inv
    ac = col[:, None] * inv
    cos64 = jnp.concatenate([jnp.cos(ar), jnp.cos(ar), jnp.cos(ac), jnp.cos(ac)], axis=1)
    sin64 = jnp.concatenate([-jnp.sin(ar), jnp.sin(ar), -jnp.sin(ac), jnp.sin(ac)], axis=1)
    cos_t = jnp.concatenate([jnp.tile(cos64, (1, 2)), jnp.ones((TM_IN, 128), F32)], axis=0)
    sin_t = jnp.concatenate([jnp.tile(sin64, (1, 2)), jnp.zeros((TM_IN, 128), F32)], axis=0)
    return cos_t, sin_t


def _split_bf16(w):
    hi = w.astype(BF16)
    return hi, (w - hi.astype(F32)).astype(BF16)


def kernel(x, c, ctx, c_ctx, w_ada, b_ada, g_norm1, g_norm2, w_in, w_out, g_v, w_s, b_s, w_conv, g_q, g_k,
           lam_q1, lam_k1, lam_q2, lam_k2, g_sub, w_router, b_router, w_gate, w_up, w_down,
           ws_gate, ws_up, ws_down):
    src = (x.reshape(NL, D), ctx.reshape(NC, D))
    cc = jnp.concatenate([c, c_ctx[None, :], jnp.zeros((MOD_ROWS - B - 1, D), F32)], axis=0)
    mod = _ada(cc, w_ada, b_ada)
    cos_t, sin_t = _rope_tables()
    bd = jnp.asarray(np.kron(np.eye(RMS_BLOCK // QK_DIM, dtype=np.float32),
                             np.full((QK_DIM, QK_DIM), 1.0 / QK_DIM, np.float32)), BF16)

    for l in range(DEPTH):
        last = l == DEPTH - 1
        lam_init = 0.8 - 0.6 * math.exp(-0.3 * l)
        lam = (jnp.exp(jnp.sum(lam_q1[l] * lam_k1[l])) - jnp.exp(jnp.sum(lam_q2[l] * lam_k2[l])) + lam_init)
        bound = (QK_DIM * jnp.max(jnp.abs(g_q[l])) * jnp.max(jnp.abs(g_k[l]))
                 * (QK_DIM ** -0.5 * LOG2E) * ATT_BOUND_MARGIN)
        use_bound = (2.0 * bound < ATT_MAX_SHIFT_RANGE).astype(F32)
        lam = jnp.stack([lam, bound, use_bound]).astype(F32)
        mod3 = mod[l].reshape(MOD_ROWS, 1, 6 * D)
        w_in_bf = _in_weights(w_in[l])
        bias_t = jnp.repeat(b_s[l].T, A_GD, axis=1)
        wconv = jnp.concatenate([w_conv[l], jnp.zeros((5, B_WIDTH), F32)], axis=0)
        yab, q, kk, v = _in_mix(src[0], src[1], g_norm1[l], mod3, w_in_bf, cos_t, sin_t,
                                     g_v[l].reshape(1, A_WIDTH), w_s[l].astype(BF16), bias_t, wconv,
                                     jnp.tile(g_q[l], 8).reshape(1, 512), jnp.tile(g_k[l], 8).reshape(1, 512), bd)
        gsub = g_sub[l].reshape(1, V_DIM)
        coef = 1.0 - lam_init
        n_rows = NL if last else NR
        yc = _attention(lam, q, kk, v, gsub, coef, ctx_queries=False)
        yc_ctx = yc if last else _attention(lam, q, kk, v, gsub, coef, ctx_queries=True)
        wr_hi, wr_lo = _split_bf16(w_router[l].T)
        wo_bf = w_out[l].astype(BF16)
        ws_bf = (ws_gate[l].astype(BF16), ws_up[l].astype(BF16), ws_down[l].astype(BF16))
        xa, h2p, idx3, gate3, rank3, counts = _out_router(
            src[0], src[1], yab, yc, yc_ctx, wo_bf, g_norm2[l], mod3, wr_hi, wr_lo, b_router[l], n_rows)
        pos3, sched, r_pad = _route_positions(idx3, rank3, counts[:, 0], n_rows)
        x_sorted = _sc_dispatch(h2p.reshape(n_rows, 4, 128), pos3, r_pad)
        y_sorted = _experts(l, sched, x_sorted.reshape(r_pad * 4, 128), w_gate, w_up, w_down)
        yg = _sc_collect(y_sorted.reshape(r_pad, 4, 128), pos3)
        gates = gate3.transpose(0, 2, 1).reshape(n_rows, TOP_K)
        xa = _shared_residual(xa, h2p, yg.reshape(TOP_K, n_rows * 4, 128), gates, *ws_bf, mod3, n_rows)
        src = (xa, xa)
    return xa.reshape(B, L, D)
```

```python
import functools
import math

import numpy as np
import jax
import jax.numpy as jnp
from jax import lax
from jax.experimental import pallas as pl
from jax.experimental.pallas import tpu as pltpu
from jax.experimental.pallas import tpu_sc as plsc

F32 = jnp.float32
BF16 = jnp.bfloat16

D = 1024
B = 8
L = 2048
DEPTH = 2
GRID_W = 64
CTX = 256
A_WIDTH = 256
A_GROUPS = 4
A_GD = 64
CHUNK = 128
B_WIDTH = 256
C_WIDTH = 512
HEADS = 4
V_DIM = 128
QK_DIM = 64
ROPE_BASE = 10000.0
OFF_BB = 512
OFF_BC = 768
OFF_BX = 1024
OFF_Q = 1280
OFF_K = 1792
OFF_V = 2304
D_IN = 2816
N_EXPERTS = 64
TOP_K = 8
D_EXPERT = 256
D_SHARED = 256
ROUTED_SCALE = 2.5
EPS = 1e-6

NL = B * L
NC = B * CTX
NR = NL + NC
MOD_ROWS = 16
LOG2E = 1.4426950408889634

TM_IN = 512
RMS_BLOCK = 256
TQ = 2048
ATT_CHAIN = 256
ATT_GROUP = 4
ATT_BOUND_MARGIN = 1.02
ATT_MAX_SHIFT_RANGE = 100.0
TM_OUT = 512
TM_X = 1024
X_CHAINS = 2
X_TAIL_ROWS = 512
X_TAIL = 256
X_RING = 3
TM_F = 512

_DN_T = (((1,), (1,)), ((), ()))


def _cparams(sem, vmem_mb=None):
    kw = dict(dimension_semantics=sem)
    if vmem_mb is not None:
        kw["vmem_limit_bytes"] = vmem_mb * 1024 * 1024
    return pltpu.CompilerParams(**kw)


def _mod_row(i, tm):
    return jnp.where(i < NL // tm, i // (L // tm), B)


def _mod_spec(tm, chunk):
    return pl.BlockSpec((None, 1, D), lambda i: (_mod_row(i, tm), 0, chunk))


def _ada_kernel(c_ref, w_ref, b_ref, o_ref):
    c = c_ref[...]
    cs = c * jax.nn.sigmoid(c)
    o_ref[...] = jnp.dot(cs, w_ref[...], preferred_element_type=F32,
                         precision=lax.Precision.HIGHEST) + b_ref[...]


def _ada(cc, w_ada, b_ada):
    nb = 6
    return pl.pallas_call(
        _ada_kernel,
        out_shape=jax.ShapeDtypeStruct((DEPTH, MOD_ROWS, 6 * D), F32),
        grid=(DEPTH, nb),
        in_specs=[pl.BlockSpec((MOD_ROWS, D), lambda l, j: (0, 0)),
                  pl.BlockSpec((None, D, D), lambda l, j: (l, 0, j)),
                  pl.BlockSpec((None, 1, D), lambda l, j: (l, 0, j))],
        out_specs=pl.BlockSpec((None, MOD_ROWS, D), lambda l, j: (l, 0, j)),
        compiler_params=_cparams(("arbitrary", "arbitrary"), 40),
        name="ada_mod",
    )(cc, w_ada, b_ada.reshape(DEPTH, 1, 6 * D))


def _rms_mod(x, g, sc, sh):
    ms = jnp.mean(x * x, axis=-1, keepdims=True)
    return x * lax.rsqrt(ms + EPS) * (g * (1.0 + sc)) + sh


def _two_source_specs(tm, n_first, width=D):
    return [pl.BlockSpec((tm, width), lambda i: (jnp.minimum(i, n_first - 1), 0)),
            pl.BlockSpec((tm, width), lambda i: (jnp.maximum(i - n_first, 0), 0))]


def _two_source_rows(a_ref, b_ref, n_first):
    return jnp.where(pl.program_id(0) < n_first, a_ref[...], b_ref[...])


def _group_rms(t, g, bd):
    sq = (t * t).astype(BF16)
    ms = jnp.concatenate([jnp.dot(sq[:, c:c + RMS_BLOCK], bd, preferred_element_type=F32)
                          for c in range(0, t.shape[1], RMS_BLOCK)], axis=1)
    return t * lax.rsqrt(ms + EPS) * g


def _rope(t, cos, sin):
    w = t.shape[1]
    lane = lax.broadcasted_iota(jnp.int32, t.shape, 1)
    first = (lane % 32) < 16
    partner = jnp.where(first, pltpu.roll(t, w - 16, 1), pltpu.roll(t, 16, 1))
    cos4 = jnp.concatenate([cos] * (w // 128), axis=1)
    sin4 = jnp.concatenate([sin] * (w // 128), axis=1)
    return t * cos4 + partner * sin4


def _in_mix_kernel(xa_ref, xb_ref, pa_ref, pb_ref, na_ref, nb_ref, g_ref, sh_ref, sc_ref, w_ref,
                   cos_ref, sin_ref, gv_ref, ws_ref, bias_ref, wconv_ref, gq_ref, gk_ref, bd_ref,
                   yab_ref, q_ref, kk_ref, v_ref, *, n_first):
    tm = TM_IN
    i = pl.program_id(0)
    tiles_per_seq = L // tm
    is_lat = i < NL // tm
    is_start = jnp.logical_or(jnp.logical_not(is_lat), i % tiles_per_seq == 0)
    is_end = jnp.logical_or(jnp.logical_not(is_lat), i % tiles_per_seq == tiles_per_seq - 1)
    first = i < n_first
    g, sc, sh = g_ref[...], sc_ref[...], sh_ref[...]

    h = _rms_mod(jnp.where(first, xa_ref[...], xb_ref[...]), g, sc, sh)
    p = jnp.dot(h.astype(BF16), w_ref[...], preferred_element_type=F32)
    v_ref[...] = p[:, OFF_V:].astype(BF16)
    halo = jnp.concatenate([jnp.where(first, pa_ref[...], pb_ref[...]),
                            jnp.where(first, na_ref[...], nb_ref[...])], axis=0)
    ph = jnp.dot(_rms_mod(halo, g, sc, sh).astype(BF16), w_ref[:, OFF_BC:OFF_Q], preferred_element_type=F32)
    zh = ph[:, :B_WIDTH] * ph[:, B_WIDTH:]
    zp = jnp.where(is_start, 0.0, zh[15:16])
    zn = jnp.where(is_end, 0.0, zh[16:17])

    uv = p[:, 0:2 * A_WIDTH]
    uv = 0.5 * uv * (1.0 + lax.erf(uv * (2.0 ** -0.5)))
    u = uv[:, :A_WIDTH]
    v = uv[:, A_WIDTH:]
    ms = jnp.mean(v * v, axis=-1, keepdims=True)
    vb = (v * lax.rsqrt(ms + EPS) * gv_ref[...]).astype(BF16)
    lane = lax.broadcasted_iota(jnp.int32, (CHUNK, 128), 1)
    mixes = []
    for c in range(tm // CHUNK):
        vc = vb[c * CHUNK:(c + 1) * CHUNK]
        halves = []
        for j in range(2):
            vj = vc[:, j * 128:(j + 1) * 128]
            m0 = jnp.dot(ws_ref[2 * j], vj, preferred_element_type=F32)
            m1 = jnp.dot(ws_ref[2 * j + 1], vj, preferred_element_type=F32)
            halves.append(jnp.where(lane < A_GD, m0, m1))
        mixes.append(jnp.concatenate(halves, axis=1) + bias_ref[...])
    ya = u * jnp.concatenate(mixes, axis=0)

    bg = p[:, OFF_BB:OFF_BC]
    z = p[:, OFF_BC:OFF_BX] * p[:, OFF_BX:OFF_Q]
    row = lax.broadcasted_iota(jnp.int32, z.shape, 0)
    inner = jnp.logical_not(is_lat)
    z_prev = jnp.where(row == 0, zp, pltpu.roll(z, 1, 0))
    z_prev = jnp.where(jnp.logical_and(inner, row % CTX == 0), 0.0, z_prev)
    z_next = jnp.where(row == tm - 1, zn, pltpu.roll(z, tm - 1, 0))
    z_next = jnp.where(jnp.logical_and(inner, row % CTX == CTX - 1), 0.0, z_next)
    yb = bg * (z_prev * wconv_ref[0:1] + z * wconv_ref[1:2] + z_next * wconv_ref[2:3])
    yab_ref[...] = jnp.concatenate([ya, yb], axis=1).astype(BF16)

    cos = cos_ref[...]
    sin = sin_ref[...]
    bd = bd_ref[...]
    q = _rope(_group_rms(p[:, OFF_Q:OFF_K], gq_ref[...], bd), cos, sin)
    q = q * (QK_DIM ** -0.5 * LOG2E)
    q_ref[...] = q.astype(BF16)
    k = _rope(_group_rms(p[:, OFF_K:OFF_V], gk_ref[...], bd), cos, sin)
    kk_ref[...] = k.astype(BF16)


def _in_mix(x_first, x_second, g, mod3, w_bf, cos_t, sin_t, gv, ws_bf, bias_t, wconv, gq, gk, bd):
    tm = TM_IN
    n_first = x_first.shape[0] // tm
    hb = tm // 16
    nhb_first = x_first.shape[0] // 16
    nhb_second = x_second.shape[0] // 16
    pos_blocks = L // tm

    def tab_map(i):
        return (jnp.where(i < NL // tm, i % pos_blocks, pos_blocks), 0)

    def halo_specs(shift):
        blk = lambda i: (i * tm + shift) // 16
        return [pl.BlockSpec((16, D), lambda i: (jnp.clip(blk(i), 0, nhb_first - 1), 0)),
                pl.BlockSpec((16, D), lambda i: (jnp.clip(blk(i) - nhb_first, 0, nhb_second - 1), 0))]

    const2 = lambda i: (0, 0)
    row512 = pl.BlockSpec((tm, 512), lambda i: (i, 0))
    return pl.pallas_call(
        functools.partial(_in_mix_kernel, n_first=n_first),
        out_shape=[jax.ShapeDtypeStruct((NR, 512), BF16)] * 4,
        grid=(NR // tm,),
        in_specs=_two_source_specs(tm, n_first) + halo_specs(-1) + halo_specs(tm)
                 + [pl.BlockSpec((1, D), const2), _mod_spec(tm, 0), _mod_spec(tm, 1),
                    pl.BlockSpec((D, D_IN), const2),
                    pl.BlockSpec((tm, 128), tab_map), pl.BlockSpec((tm, 128), tab_map),
                    pl.BlockSpec((1, A_WIDTH), const2),
                    pl.BlockSpec((A_GROUPS, CHUNK, CHUNK), lambda i: (0, 0, 0)),
                    pl.BlockSpec((CHUNK, A_WIDTH), const2),
                    pl.BlockSpec((8, B_WIDTH), const2),
                    pl.BlockSpec((1, 512), const2), pl.BlockSpec((1, 512), const2),
                    pl.BlockSpec((RMS_BLOCK, RMS_BLOCK), const2)],
        out_specs=[row512] * 4,
        compiler_params=_cparams(("parallel",), 56),
        name="in_mix",
    )(x_first, x_second, x_first, x_second, x_first, x_second, g.reshape(1, D), mod3, mod3, w_bf,
      cos_t, sin_t, gv, ws_bf, bias_t, wconv, gq, gk, bd)


def _attn_kernel(lam_ref, q_ref, *rest, n_seg, coef, tq):
    kv_refs = rest[:2 * n_seg]
    gsub_ref, o_ref, k_scr, vt_scr = rest[2 * n_seg:]

    @pl.when(pl.program_id(2) == 0)
    def _():
        off = 0
        for s in range(n_seg):
            n = kv_refs[s].shape[0]
            k_scr[off:off + n, :] = kv_refs[s][...]
            vt_scr[0:V_DIM, off:off + n] = kv_refs[n_seg + s][...].astype(F32).T.astype(BF16)
            off += n
        ones_row = lax.broadcasted_iota(jnp.int32, (16, off), 0) == 0
        vt_scr[V_DIM:, :] = jnp.where(ones_row, 1.0, 0.0).astype(BF16)

    lam = lam_ref[0]
    shift = lam_ref[1]
    qc = ATT_CHAIN

    def scores(c):
        rows = slice(c * qc, (c + 1) * qc)
        q = q_ref[rows, :]
        lane = lax.broadcasted_iota(jnp.int32, q.shape, 1)
        zero = jnp.zeros_like(q)
        qs = jnp.concatenate([jnp.where(lane < QK_DIM, q, zero), jnp.where(lane >= QK_DIM, q, zero)], axis=0)
        return lax.dot_general(k_scr[...], qs, _DN_T, preferred_element_type=F32)

    def finish(c, pt):
        ot = jnp.dot(vt_scr[...], pt, preferred_element_type=F32)
        inv = 1.0 / ot[V_DIM:V_DIM + 1, :]
        dt = ot[0:V_DIM, :qc] * inv[:, :qc] - ot[0:V_DIM, qc:] * (lam * inv[:, qc:])
        o = dt.T
        ms = jnp.mean(o * o, axis=-1, keepdims=True)
        o_ref[c * qc:(c + 1) * qc, :] = (o * lax.rsqrt(ms + EPS) * gsub_ref[...] * coef).astype(o_ref.dtype)

    @pl.when(lam_ref[2] > 0.5)
    def _():
        for c in range(tq // qc):
            finish(c, jnp.exp2(scores(c) - shift).astype(BF16))

    @pl.when(lam_ref[2] <= 0.5)
    def _():
        for g0 in range(0, tq // qc, ATT_GROUP):
            group = range(g0, min(g0 + ATT_GROUP, tq // qc))
            sts = [scores(c) for c in group]
            for c, st in zip(group, sts):
                finish(c, jnp.exp2(st - jnp.max(st, axis=0, keepdims=True)).astype(BF16))


def _attention(lam, q, kk, v, gsub, coef, *, ctx_queries):
    if ctx_queries:
        tq = CTX
        nq, lk, n_seg = 1, CTX, 1
        q_map = lambda b, h, qi: (NL // tq + b, h)
        kv_specs = [pl.BlockSpec((CTX, 128), lambda b, h, qi: (NL // CTX + b, h)),
                    pl.BlockSpec((CTX, 128), lambda b, h, qi: (NL // CTX + b, h))]
        kv_args = [kk, v]
        rows = NC
        o_map = lambda b, h, qi: (b, h)
    else:
        tq = TQ
        nq, lk, n_seg = L // tq, CTX + L, 2
        q_map = lambda b, h, qi: (b * (L // tq) + qi, h)
        kv_specs = [pl.BlockSpec((CTX, 128), lambda b, h, qi: (NL // CTX + b, h)),
                    pl.BlockSpec((L, 128), lambda b, h, qi: (b, h)),
                    pl.BlockSpec((CTX, 128), lambda b, h, qi: (NL // CTX + b, h)),
                    pl.BlockSpec((L, 128), lambda b, h, qi: (b, h))]
        kv_args = [kk, kk, v, v]
        rows = NL
        o_map = lambda b, h, qi: (b * (L // tq) + qi, h)
    return pl.pallas_call(
        functools.partial(_attn_kernel, n_seg=n_seg, coef=coef, tq=tq),
        out_shape=jax.ShapeDtypeStruct((rows, C_WIDTH), BF16),
        grid=(B, HEADS, nq),
        in_specs=[pl.BlockSpec(memory_space=pltpu.SMEM),
                  pl.BlockSpec((tq, 128), q_map)]
                 + kv_specs + [pl.BlockSpec((1, V_DIM), lambda b, h, qi: (0, 0))],
        out_specs=pl.BlockSpec((tq, 128), o_map),
        scratch_shapes=[pltpu.VMEM((lk, 128), BF16), pltpu.VMEM((V_DIM + 16, lk), BF16)],
        compiler_params=_cparams(("parallel", "parallel", "arbitrary"), 56),
        name="attn_ctx" if ctx_queries else "attn_lat",
    )(lam, q, *kv_args, gsub)


def _pack_rows(t, out_ref, row0=0):
    half = D // 2
    w = pltpu.pack_elementwise([t[:, :half], t[:, half:]], packed_dtype=BF16)
    w = lax.bitcast_convert_type(w, jnp.uint32)
    rows = t.shape[0]
    for j in range(4):
        out_ref[pl.ds(4 * row0 + j, rows, stride=4), :] = w[:, j * 128:(j + 1) * 128]


def _unpack_rows(ref, rows, lead=None, row0=0):
    los, his = [], []
    for j in range(4):
        sl = pl.ds(4 * row0 + j, rows, stride=4)
        w = ref[sl, :] if lead is None else ref[lead, sl, :]
        los.append(pltpu.unpack_elementwise(w, index=0, packed_dtype=BF16, unpacked_dtype=F32))
        his.append(pltpu.unpack_elementwise(w, index=1, packed_dtype=BF16, unpacked_dtype=F32))
    return jnp.concatenate(los, axis=1), jnp.concatenate(his, axis=1)


def _out_router_kernel(xa_ref, xb_ref, yab_ref, yca_ref, ycb_ref, wo_ref, g1_ref, g2n_ref, sh2_ref, sc2_ref, wrh_ref,
                       wrl_ref, br_ref, xo_ref, h2p_ref, idx_ref, gate_ref, rank_ref, cnt_ref, run_ref, *,
                       n_first, n_first_c):
    tm = TM_OUT

    @pl.when(pl.program_id(0) == 0)
    def _():
        run_ref[...] = jnp.zeros_like(run_ref)

    yc = _two_source_rows(yca_ref, ycb_ref, n_first_c)
    y = jnp.dot(jnp.concatenate([yab_ref[...], yc], axis=1), wo_ref[...], preferred_element_type=F32)
    x = _two_source_rows(xa_ref, xb_ref, n_first) + g1_ref[...] * y
    xo_ref[...] = x
    h2 = _rms_mod(x, g2n_ref[...], sc2_ref[...], sh2_ref[...])
    hi = h2.astype(BF16)
    _pack_rows(h2, h2p_ref)
    lo = (h2 - hi.astype(F32)).astype(BF16)
    wh = wrh_ref[...]
    z = (lax.dot_general(wh, hi, _DN_T, preferred_element_type=F32)
         + lax.dot_general(wh, lo, _DN_T, preferred_element_type=F32)
         + lax.dot_general(wrl_ref[...], hi, _DN_T, preferred_element_type=F32))
    scores = jax.nn.sigmoid(z)
    work = scores + br_ref[...]
    eio = lax.broadcasted_iota(jnp.int32, work.shape, 0)
    idxs, sels, hits = [], [], []
    for _ in range(TOP_K):
        m = jnp.max(work, axis=0, keepdims=True)
        idx = jnp.min(jnp.where(work == m, eio, N_EXPERTS), axis=0, keepdims=True)
        hit = eio == idx
        sels.append(jnp.sum(jnp.where(hit, scores, 0.0), axis=0, keepdims=True))
        idxs.append(idx)
        hits.append(hit)
        work = jnp.where(hit, -jnp.inf, work)
    sel = jnp.concatenate(sels, axis=0)
    idx8 = jnp.concatenate(idxs, axis=0)
    gate8 = sel / jnp.sum(sel, axis=0, keepdims=True) * ROUTED_SCALE

    chosen = functools.reduce(jnp.logical_or, hits)
    before = (lax.broadcasted_iota(jnp.int32, (tm, tm), 0) < lax.broadcasted_iota(jnp.int32, (tm, tm), 1))
    prefix = jnp.dot(jnp.where(chosen, 1.0, 0.0).astype(BF16), jnp.where(before, 1.0, 0.0).astype(BF16),
                     preferred_element_type=F32)
    rank_dense = prefix + run_ref[:, 0:1]
    rank8 = jnp.concatenate([jnp.sum(jnp.where(h, rank_dense, 0.0), axis=0, keepdims=True) for h in hits],
                            axis=0).astype(jnp.int32)
    run = run_ref[...] + jnp.sum(jnp.where(chosen, 1.0, 0.0), axis=1, keepdims=True)
    run_ref[...] = run
    cnt_ref[...] = run
    for c in range(tm // 128):
        idx_ref[c] = idx8[:, c * 128:(c + 1) * 128]
        gate_ref[c] = gate8[:, c * 128:(c + 1) * 128]
        rank_ref[c] = rank8[:, c * 128:(c + 1) * 128]


def _out_router(x_first, x_second, yab, yc_first, yc_second, wo_bf, g2n, mod3, wr_hi, wr_lo, br, n_rows):
    tm = TM_OUT
    n_first = x_first.shape[0] // tm
    n_first_c = yc_first.shape[0] // tm
    const2 = lambda i: (0, 0)
    row = lambda i: (i, 0)
    chunk3 = pl.BlockSpec((tm // 128, TOP_K, 128), lambda i: (i, 0, 0))
    nch = n_rows // 128
    return pl.pallas_call(
        functools.partial(_out_router_kernel, n_first=n_first, n_first_c=n_first_c),
        out_shape=[jax.ShapeDtypeStruct((n_rows, D), F32),
                   jax.ShapeDtypeStruct((n_rows * 4, 128), jnp.uint32),
                   jax.ShapeDtypeStruct((nch, TOP_K, 128), jnp.int32),
                   jax.ShapeDtypeStruct((nch, TOP_K, 128), F32),
                   jax.ShapeDtypeStruct((nch, TOP_K, 128), jnp.int32),
                   jax.ShapeDtypeStruct((N_EXPERTS, 128), F32)],
        grid=(n_rows // tm,),
        in_specs=_two_source_specs(tm, n_first)
                 + [pl.BlockSpec((tm, 512), row)]
                 + _two_source_specs(tm, n_first_c, C_WIDTH)
                 + [pl.BlockSpec((D, D), const2),
                  _mod_spec(tm, 2),
                  pl.BlockSpec((1, D), const2), _mod_spec(tm, 3), _mod_spec(tm, 4),
                  pl.BlockSpec((N_EXPERTS, D), const2), pl.BlockSpec((N_EXPERTS, D), const2),
                  pl.BlockSpec((N_EXPERTS, 1), const2)],
        out_specs=[pl.BlockSpec((tm, D), row), pl.BlockSpec((tm * 4, 128), row),
                   chunk3, chunk3, chunk3, pl.BlockSpec((N_EXPERTS, 128), const2)],
        scratch_shapes=[pltpu.VMEM((N_EXPERTS, 128), F32)],
        compiler_params=_cparams(("arbitrary",), 48),
        name="out_router",
    )(x_first, x_second, yab, yc_first, yc_second, wo_bf, mod3, g2n.reshape(1, D), mod3, mod3, wr_hi, wr_lo,
      br.reshape(N_EXPERTS, 1))


def _experts_kernel(ts_ref, nt_ref, cnt_ref, nu_ref, x_hbm, wg_ref, wu_ref, wd_ref, y_hbm,
                    wgu_s, wd_s, x_ring, x_sems, y_buf, y_sems, nb_s):
    e = pl.program_id(0)
    n_used = nu_ref[0]
    t0 = ts_ref[e]
    n_tiles = nt_ref[e]
    count = cnt_ref[e]
    ahead = X_RING - 1
    rows4 = 4 * TM_X
    tail4 = 4 * X_TAIL_ROWS

    def x_copy(t, whole):
        slot = t % X_RING
        n4 = rows4 if whole else tail4
        return pltpu.make_async_copy(x_hbm.at[pl.ds(t * rows4, n4)], x_ring.at[slot, pl.ds(0, n4)], x_sems.at[slot])

    def y_copy(t, whole):
        slot = t % 2
        n4 = rows4 if whole else tail4
        return pltpu.make_async_copy(y_buf.at[slot, pl.ds(0, n4)], y_hbm.at[pl.ds(t * rows4, n4)], y_sems.at[slot])

    def for_blocks(t, fn):
        w = nb_s[t]

        @pl.when(w == 0)
        def _():
            fn(False)

        @pl.when(w != 0)
        def _():
            fn(True)

    @pl.when(e == 0)
    def _():
        def fill(e2, carry):
            def one(j, c):
                left = cnt_ref[e2] - j * TM_X
                nb_s[ts_ref[e2] + j] = (left > X_TAIL_ROWS).astype(jnp.int32)
                return c
            return lax.fori_loop(0, nt_ref[e2], one, carry)
        lax.fori_loop(0, N_EXPERTS, fill, 0)

        for t in range(ahead):
            @pl.when(t < n_used)
            def _():
                for_blocks(t, lambda b: x_copy(t, b).start())

    @pl.when(n_tiles > 0)
    def _():
        wgu_s[:, 0:D_EXPERT] = wg_ref[...].astype(BF16)
        wgu_s[:, D_EXPERT:] = wu_ref[...].astype(BF16)
        wd_s[...] = wd_ref[...].astype(BF16)

    def first_dot(x_ref, row0, rows):
        x_lo, x_hi = _unpack_rows(x_ref, rows, row0=row0)
        x = jnp.concatenate([x_lo.astype(BF16), x_hi.astype(BF16)], axis=1)
        return jnp.dot(x, wgu_s[...], preferred_element_type=F32)

    def second_dot(ab, y_ref, row0):
        a = ab[:, :D_EXPERT]
        hid = (a * jax.nn.sigmoid(a) * ab[:, D_EXPERT:]).astype(BF16)
        _pack_rows(jnp.dot(hid, wd_s[...], preferred_element_type=F32), y_ref, row0=row0)

    def tile(j, carry):
        t = t0 + j

        @pl.when(t + ahead < n_used)
        def _():
            for_blocks(t + ahead, lambda b: x_copy(t + ahead, b).start())

        for_blocks(t, lambda b: x_copy(t, b).wait())

        @pl.when(t >= 2)
        def _():
            for_blocks(t - 2, lambda b: y_copy(t - 2, b).wait())

        x_ref = x_ring.at[t % X_RING]
        y_ref = y_buf.at[t % 2]
        valid = count - j * TM_X

        @pl.when(valid > X_TAIL_ROWS)
        def _():
            rc = TM_X // X_CHAINS
            abs_ = [first_dot(x_ref, c * rc, rc) for c in range(X_CHAINS)]
            for c in range(X_CHAINS):
                second_dot(abs_[c], y_ref, c * rc)

        for c in range(X_TAIL_ROWS // X_TAIL):
            @pl.when(jnp.logical_and(valid <= X_TAIL_ROWS, c * X_TAIL < valid))
            def _():
                second_dot(first_dot(x_ref, c * X_TAIL, X_TAIL), y_ref, c * X_TAIL)

        for_blocks(t, lambda b: y_copy(t, b).start())
        return carry

    lax.fori_loop(0, n_tiles, tile, 0)

    @pl.when(e == N_EXPERTS - 1)
    def _():
        for d in (2, 1):
            @pl.when(n_used >= d)
            def _():
                for_blocks(n_used - d, lambda b: y_copy(n_used - d, b).wait())


def _experts(layer, sched, x_sorted, w_gate, w_up, w_down):
    tm = TM_X
    r_pad = x_sorted.shape[0] // 4
    hbm = pl.BlockSpec(memory_space=pl.ANY)
    wmap = lambda e, ts, nt, cnt, nu: (layer, e, 0, 0)
    grid_spec = pltpu.PrefetchScalarGridSpec(
        num_scalar_prefetch=4,
        grid=(N_EXPERTS,),
        in_specs=[hbm,
                  pl.BlockSpec((None, None, D, D_EXPERT), wmap),
                  pl.BlockSpec((None, None, D, D_EXPERT), wmap),
                  pl.BlockSpec((None, None, D_EXPERT, D), wmap)],
        out_specs=hbm,
        scratch_shapes=[pltpu.VMEM((D, 2 * D_EXPERT), BF16), pltpu.VMEM((D_EXPERT, D), BF16),
                        pltpu.VMEM((X_RING, tm * 4, 128), jnp.uint32), pltpu.SemaphoreType.DMA((X_RING,)),
                        pltpu.VMEM((2, tm * 4, 128), jnp.uint32), pltpu.SemaphoreType.DMA((2,)),
                        pltpu.SMEM((r_pad // tm,), jnp.int32)],
    )
    return pl.pallas_call(
        _experts_kernel,
        out_shape=jax.ShapeDtypeStruct((r_pad * 4, 128), jnp.uint32),
        grid_spec=grid_spec,
        compiler_params=_cparams(("arbitrary",), 48),
        name="experts",
    )(*sched, x_sorted, w_gate, w_up, w_down)


SC_CORES = 2
SC_SUBCORES = 16
SC_WORKERS = SC_CORES * SC_SUBCORES
SC_CHUNK = 128


def _sc_mesh():
    return plsc.VectorSubcoreMesh(core_axis_name="c", subcore_axis_name="s")


def _sc_params():
    return pltpu.CompilerParams(use_tc_tiling_on_sc=True)


def _sc_dispatch(h2p, pos3, r_pad):
    nch = pos3.shape[0]
    steps = -(-nch // SC_WORKERS)

    def body(h_hbm, pos_hbm, out_hbm, idx_v, rows_v, sem):
        wid = lax.axis_index("s") * SC_CORES + lax.axis_index("c")

        @pl.loop(0, steps)
        def _(s):
            ch = wid + s * SC_WORKERS

            @pl.when(ch < nch)
            def _():
                pltpu.sync_copy(pos_hbm.at[ch], idx_v)
                pltpu.sync_copy(h_hbm.at[pl.ds(ch * SC_CHUNK, SC_CHUNK)], rows_v)
                copies = [pltpu.async_copy(rows_v, out_hbm.at[idx_v.at[k]], sem) for k in range(TOP_K)]
                for cp in copies:
                    cp.wait()

    return pl.kernel(
        body,
        out_type=jax.ShapeDtypeStruct((r_pad, 4, 128), jnp.uint32),
        mesh=_sc_mesh(),
        scratch_types=[pltpu.VMEM((TOP_K, SC_CHUNK), jnp.int32),
                       pltpu.VMEM((SC_CHUNK, 4, 128), jnp.uint32),
                       pltpu.SemaphoreType.DMA],
        compiler_params=_sc_params(),
        name="sc_dispatch",
    )(h2p, pos3)


def _sc_collect(y_sorted, pos3):
    nch = pos3.shape[0]
    steps = -(-nch // SC_WORKERS)
    half = SC_CHUNK // 2
    units = [(k, hh) for k in range(TOP_K) for hh in range(2)]

    def body(y_hbm, pos_hbm, out_hbm, idx_v, rows_a, rows_b, sem_a, sem_b):
        wid = lax.axis_index("s") * SC_CORES + lax.axis_index("c")
        bufs = (rows_a, rows_b)
        sems = (sem_a, sem_b)

        def gather(u):
            k, hh = units[u]
            return pltpu.async_copy(y_hbm.at[idx_v.at[k, pl.ds(hh * half, half)]], bufs[u % 2], sems[u % 2])

        @pl.loop(0, steps)
        def _(s):
            ch = wid + s * SC_WORKERS

            @pl.when(ch < nch)
            def _():
                pltpu.sync_copy(pos_hbm.at[ch], idx_v)
                pending = gather(0)
                for u, (k, hh) in enumerate(units):
                    nxt = gather(u + 1) if u + 1 < len(units) else None
                    pending.wait()
                    pltpu.sync_copy(bufs[u % 2], out_hbm.at[k, pl.ds(ch * SC_CHUNK + hh * half, half)])
                    pending = nxt

    return pl.kernel(
        body,
        out_type=jax.ShapeDtypeStruct((TOP_K, nch * SC_CHUNK, 4, 128), jnp.uint32),
        mesh=_sc_mesh(),
        scratch_types=[pltpu.VMEM((TOP_K, SC_CHUNK), jnp.int32),
                       pltpu.VMEM((half, 4, 128), jnp.uint32),
                       pltpu.VMEM((half, 4, 128), jnp.uint32),
                       pltpu.SemaphoreType.DMA, pltpu.SemaphoreType.DMA],
        compiler_params=_sc_params(),
        name="sc_collect",
    )(y_sorted, pos3)


def _shared_kernel(x_ref, h2p_ref, yg_ref, gate_ref, wg_ref, wu_ref, wd_ref, g2_ref, o_ref):
    h_lo, h_hi = _unpack_rows(h2p_ref, TM_F)
    h = jnp.concatenate([h_lo.astype(BF16), h_hi.astype(BF16)], axis=1)
    a = jnp.dot(h, wg_ref[...], preferred_element_type=F32)
    b = jnp.dot(h, wu_ref[...], preferred_element_type=F32)
    hid = (a * jax.nn.sigmoid(a) * b).astype(BF16)
    f = jnp.dot(hid, wd_ref[...], preferred_element_type=F32)
    gate = gate_ref[...]
    f_lo = f[:, :D // 2]
    f_hi = f[:, D // 2:]
    for k in range(TOP_K):
        y_lo, y_hi = _unpack_rows(yg_ref, TM_F, lead=k)
        f_lo = f_lo + gate[:, k:k + 1] * y_lo
        f_hi = f_hi + gate[:, k:k + 1] * y_hi
    o_ref[...] = x_ref[...] + g2_ref[...] * jnp.concatenate([f_lo, f_hi], axis=1)


def _shared_residual(xa, h2p, yg, gates, wsg_bf, wsu_bf, wsd_bf, mod3, n_rows):
    tm = TM_F
    row = lambda i: (i, 0)
    const2 = lambda i: (0, 0)
    return pl.pallas_call(
        _shared_kernel,
        out_shape=jax.ShapeDtypeStruct((n_rows, D), F32),
        grid=(n_rows // tm,),
        in_specs=[pl.BlockSpec((tm, D), row), pl.BlockSpec((tm * 4, 128), row),
                  pl.BlockSpec((TOP_K, tm * 4, 128), lambda i: (0, i, 0)),
                  pl.BlockSpec((tm, TOP_K), row),
                  pl.BlockSpec((D, D_SHARED), const2), pl.BlockSpec((D, D_SHARED), const2),
                  pl.BlockSpec((D_SHARED, D), const2), _mod_spec(tm, 5)],
        out_specs=pl.BlockSpec((tm, D), row),
        compiler_params=_cparams(("parallel",), 48),
        name="shared_residual",
    )(xa, h2p, yg, gates, wsg_bf, wsu_bf, wsd_bf, mod3)


def _positions_kernel(offs_ref, idx_ref, rank_ref, pos_ref):
    idx = idx_ref[...]
    base = jnp.zeros_like(idx)
    for e in range(N_EXPERTS):
        base = jnp.where(idx == e, offs_ref[e], base)
    pos_ref[...] = rank_ref[...] + base


def _positions(offs, idx3, rank3):
    nch = idx3.shape[0]
    cb = nch // 2
    spec = pl.BlockSpec((cb, TOP_K, 128), lambda i, offs: (i, 0, 0))
    return pl.pallas_call(
        _positions_kernel,
        out_shape=jax.ShapeDtypeStruct((nch, TOP_K, 128), jnp.int32),
        grid_spec=pltpu.PrefetchScalarGridSpec(num_scalar_prefetch=1, grid=(nch // cb,),
                                               in_specs=[spec, spec], out_specs=spec),
        compiler_params=_cparams(("parallel",)),
        name="positions",
    )(offs, idx3, rank3)


def _route_positions(idx3, rank3, counts, n_rows):
    tm = TM_X
    counts = counts.astype(jnp.int32)
    padded = ((counts + tm - 1) // tm) * tm
    ends = jnp.cumsum(padded)
    offs = ends - padded
    pos3 = _positions(offs.astype(jnp.int32), idx3, rank3)
    r_pad = n_rows * TOP_K + N_EXPERTS * tm
    n_used = (ends[-1] // tm).astype(jnp.int32).reshape(1)
    sched = ((offs // tm).astype(jnp.int32), (padded // tm).astype(jnp.int32), counts, n_used)
    return pos3, sched, r_pad


def _in_weights(w):
    def regroup(cols):
        return cols.reshape(D, 2, HEADS, QK_DIM).transpose(0, 2, 1, 3).reshape(D, 2 * HEADS * QK_DIM)
    return jnp.concatenate([w[:, :OFF_Q], regroup(w[:, OFF_Q:OFF_K]), regroup(w[:, OFF_K:OFF_V]), w[:, OFF_V:]],
                           axis=1).astype(BF16)


def _rope_tables():
    t = jnp.arange(L)
    row = (t // GRID_W).astype(F32)
    col = (t % GRID_W).astype(F32)
    n_freq = QK_DIM // 4
    inv = ROPE_BASE ** (-jnp.arange(n_freq, dtype=F32) / n_freq)
    ar = row[:, None] * inv
    ac = col[:, None] * inv
    cos64 = jnp.concatenate([jnp.cos(ar), jnp.cos(ar), jnp.cos(ac), jnp.cos(ac)], axis=1)
    sin64 = jnp.concatenate([-jnp.sin(ar), jnp.sin(ar), -jnp.sin(ac), jnp.sin(ac)], axis=1)
    cos_t = jnp.concatenate([jnp.tile(cos64, (1, 2)), jnp.ones((TM_IN, 128), F32)], axis=0)
    sin_t = jnp.concatenate([jnp.tile(sin64, (1, 2)), jnp.zeros((TM_IN, 128), F32)], axis=0)
    return cos_t, sin_t


def _split_bf16(w):
    hi = w.astype(BF16)
    return hi, (w - hi.astype(F32)).astype(BF16)


def kernel(x, c, ctx, c_ctx, w_ada, b_ada, g_norm1, g_norm2, w_in, w_out, g_v, w_s, b_s, w_conv, g_q, g_k,
           lam_q1, lam_k1, lam_q2, lam_k2, g_sub, w_router, b_router, w_gate, w_up, w_down,
           ws_gate, ws_up, ws_down):
    src = (x.reshape(NL, D), ctx.reshape(NC, D))
    cc = jnp.concatenate([c, c_ctx[None, :], jnp.zeros((MOD_ROWS - B - 1, D), F32)], axis=0)
    mod = _ada(cc, w_ada, b_ada)
    cos_t, sin_t = _rope_tables()
    bd = jnp.asarray(np.kron(np.eye(RMS_BLOCK // QK_DIM, dtype=np.float32),
                             np.full((QK_DIM, QK_DIM), 1.0 / QK_DIM, np.float32)), BF16)

    for l in range(DEPTH):
        last = l == DEPTH - 1
        lam_init = 0.8 - 0.6 * math.exp(-0.3 * l)
        lam = (jnp.exp(jnp.sum(lam_q1[l] * lam_k1[l])) - jnp.exp(jnp.sum(lam_q2[l] * lam_k2[l])) + lam_init)
        bound = (QK_DIM * jnp.max(jnp.abs(g_q[l])) * jnp.max(jnp.abs(g_k[l]))
                 * (QK_DIM ** -0.5 * LOG2E) * ATT_BOUND_MARGIN)
        use_bound = (2.0 * bound < ATT_MAX_SHIFT_RANGE).astype(F32)
        lam = jnp.stack([lam, bound, use_bound]).astype(F32)
        mod3 = mod[l].reshape(MOD_ROWS, 1, 6 * D)
        w_in_bf = _in_weights(w_in[l])
        bias_t = jnp.repeat(b_s[l].T, A_GD, axis=1)
        wconv = jnp.concatenate([w_conv[l], jnp.zeros((5, B_WIDTH), F32)], axis=0)
        yab, q, kk, v = _in_mix(src[0], src[1], g_norm1[l], mod3, w_in_bf, cos_t, sin_t,
                                     g_v[l].reshape(1, A_WIDTH), w_s[l].astype(BF16), bias_t, wconv,
                                     jnp.tile(g_q[l], 8).reshape(1, 512), jnp.tile(g_k[l], 8).reshape(1, 512), bd)
        gsub = g_sub[l].reshape(1, V_DIM)
        coef = 1.0 - lam_init
        n_rows = NL if last else NR
        yc = _attention(lam, q, kk, v, gsub, coef, ctx_queries=False)
        yc_ctx = yc if last else _attention(lam, q, kk, v, gsub, coef, ctx_queries=True)
        wr_hi, wr_lo = _split_bf16(w_router[l].T)
        wo_bf = w_out[l].astype(BF16)
        ws_bf = (ws_gate[l].astype(BF16), ws_up[l].astype(BF16), ws_down[l].astype(BF16))
        xa, h2p, idx3, gate3, rank3, counts = _out_router(
            src[0], src[1], yab, yc, yc_ctx, wo_bf, g_norm2[l], mod3, wr_hi, wr_lo, b_router[l], n_rows)
        pos3, sched, r_pad = _route_positions(idx3, rank3, counts[:, 0], n_rows)
        x_sorted = _sc_dispatch(h2p.reshape(n_rows, 4, 128), pos3, r_pad)
        y_sorted = _experts(l, sched, x_sorted.reshape(r_pad * 4, 128), w_gate, w_up, w_down)
        yg = _sc_collect(y_sorted.reshape(r_pad, 4, 128), pos3)
        gates = gate3.transpose(0, 2, 1).reshape(n_rows, TOP_K)
        xa = _shared_residual(xa, h2p, yg.reshape(TOP_K, n_rows * 4, 128), gates, *ws_bf, mod3, n_rows)
        src = (xa, xa)
    return xa.reshape(B, L, D)
```

```python
import functools
import math

import numpy as np
import jax
import jax.numpy as jnp
from jax import lax
from jax.experimental import pallas as pl
from jax.experimental.pallas import tpu as pltpu
from jax.experimental.pallas import tpu_sc as plsc

F32 = jnp.float32
BF16 = jnp.bfloat16

D = 1024
B = 8
L = 2048
DEPTH = 2
GRID_W = 64
CTX = 256
A_WIDTH = 256
A_GROUPS = 4
A_GD = 64
CHUNK = 128
B_WIDTH = 256
C_WIDTH = 512
HEADS = 4
V_DIM = 128
QK_DIM = 64
ROPE_BASE = 10000.0
OFF_BB = 512
OFF_BC = 768
OFF_BX = 1024
OFF_Q = 1280
OFF_K = 1792
OFF_V = 2304
D_IN = 2816
N_EXPERTS = 64
TOP_K = 8
D_EXPERT = 256
D_SHARED = 256
ROUTED_SCALE = 2.5
EPS = 1e-6

NL = B * L
NC = B * CTX
NR = NL + NC
MOD_ROWS = 16
LOG2E = 1.4426950408889634

TM_IN = 1024
RMS_BLOCK = 256
TQ = 2048
ATT_CHAIN = 256
ATT_GROUP = 4
ATT_BOUND_MARGIN = 1.02
ATT_MAX_SHIFT_RANGE = 100.0
TM_OUT = 512
TM_X = 1024
X_CHAINS = 2
X_TAIL_ROWS = 512
X_TAIL = 256
X_RING = 3
TM_F = 512

_DN_T = (((1,), (1,)), ((), ()))


def _cparams(sem, vmem_mb=None):
    kw = dict(dimension_semantics=sem)
    if vmem_mb is not None:
        kw["vmem_limit_bytes"] = vmem_mb * 1024 * 1024
    return pltpu.CompilerParams(**kw)


def _mod_row(i, tm):
    return jnp.where(i < NL // tm, i // (L // tm), B)


def _mod_spec(tm, chunk):
    return pl.BlockSpec((None, 1, D), lambda i: (_mod_row(i, tm), 0, chunk))


def _ada_kernel(c_ref, w_ref, b_ref, o_ref):
    c = c_ref[...]
    cs = c * jax.nn.sigmoid(c)
    o_ref[...] = jnp.dot(cs, w_ref[...], preferred_element_type=F32,
                         precision=lax.Precision.HIGHEST) + b_ref[...]


def _ada(cc, w_ada, b_ada):
    nb = 6
    return pl.pallas_call(
        _ada_kernel,
        out_shape=jax.ShapeDtypeStruct((DEPTH, MOD_ROWS, 6 * D), F32),
        grid=(DEPTH, nb),
        in_specs=[pl.BlockSpec((MOD_ROWS, D), lambda l, j: (0, 0)),
                  pl.BlockSpec((None, D, D), lambda l, j: (l, 0, j)),
                  pl.BlockSpec((None, 1, D), lambda l, j: (l, 0, j))],
        out_specs=pl.BlockSpec((None, MOD_ROWS, D), lambda l, j: (l, 0, j)),
        compiler_params=_cparams(("arbitrary", "arbitrary"), 40),
        name="ada_mod",
    )(cc, w_ada, b_ada.reshape(DEPTH, 1, 6 * D))


def _rms_mod(x, g, sc, sh):
    ms = jnp.mean(x * x, axis=-1, keepdims=True)
    return x * lax.rsqrt(ms + EPS) * (g * (1.0 + sc)) + sh


def _two_source_specs(tm, n_first, width=D):
    return [pl.BlockSpec((tm, width), lambda i: (jnp.minimum(i, n_first - 1), 0)),
            pl.BlockSpec((tm, width), lambda i: (jnp.maximum(i - n_first, 0), 0))]


def _two_source_rows(a_ref, b_ref, n_first):
    return jnp.where(pl.program_id(0) < n_first, a_ref[...], b_ref[...])


def _group_rms(t, g, bd):
    sq = (t * t).astype(BF16)
    ms = jnp.concatenate([jnp.dot(sq[:, c:c + RMS_BLOCK], bd, preferred_element_type=F32)
                          for c in range(0, t.shape[1], RMS_BLOCK)], axis=1)
    return t * lax.rsqrt(ms + EPS) * g


def _rope(t, cos, sin):
    w = t.shape[1]
    lane = lax.broadcasted_iota(jnp.int32, t.shape, 1)
    first = (lane % 32) < 16
    partner = jnp.where(first, pltpu.roll(t, w - 16, 1), pltpu.roll(t, 16, 1))
    cos4 = jnp.concatenate([cos] * (w // 128), axis=1)
    sin4 = jnp.concatenate([sin] * (w // 128), axis=1)
    return t * cos4 + partner * sin4


def _in_mix_kernel(xa_ref, xb_ref, pa_ref, pb_ref, na_ref, nb_ref, g_ref, sh_ref, sc_ref, w_ref,
                   cos_ref, sin_ref, gv_ref, ws_ref, bias_ref, wconv_ref, gq_ref, gk_ref, bd_ref,
                   yab_ref, q_ref, kk_ref, v_ref, *, n_first):
    tm = TM_IN
    i = pl.program_id(0)
    tiles_per_seq = L // tm
    is_lat = i < NL // tm
    is_start = jnp.logical_or(jnp.logical_not(is_lat), i % tiles_per_seq == 0)
    is_end = jnp.logical_or(jnp.logical_not(is_lat), i % tiles_per_seq == tiles_per_seq - 1)
    first = i < n_first
    g, sc, sh = g_ref[...], sc_ref[...], sh_ref[...]

    h = _rms_mod(jnp.where(first, xa_ref[...], xb_ref[...]), g, sc, sh)
    p = jnp.dot(h.astype(BF16), w_ref[...], preferred_element_type=F32)
    v_ref[...] = p[:, OFF_V:].astype(BF16)
    halo = jnp.concatenate([jnp.where(first, pa_ref[...], pb_ref[...]),
                            jnp.where(first, na_ref[...], nb_ref[...])], axis=0)
    ph = jnp.dot(_rms_mod(halo, g, sc, sh).astype(BF16), w_ref[:, OFF_BC:OFF_Q], preferred_element_type=F32)
    zh = ph[:, :B_WIDTH] * ph[:, B_WIDTH:]
    zp = jnp.where(is_start, 0.0, zh[15:16])
    zn = jnp.where(is_end, 0.0, zh[16:17])

    uv = p[:, 0:2 * A_WIDTH]
    uv = 0.5 * uv * (1.0 + lax.erf(uv * (2.0 ** -0.5)))
    u = uv[:, :A_WIDTH]
    v = uv[:, A_WIDTH:]
    ms = jnp.mean(v * v, axis=-1, keepdims=True)
    vb = (v * lax.rsqrt(ms + EPS) * gv_ref[...]).astype(BF16)
    lane = lax.broadcasted_iota(jnp.int32, (CHUNK, 128), 1)
    mixes = []
    for c in range(tm // CHUNK):
        vc = vb[c * CHUNK:(c + 1) * CHUNK]
        halves = []
        for j in range(2):
            vj = vc[:, j * 128:(j + 1) * 128]
            m0 = jnp.dot(ws_ref[2 * j], vj, preferred_element_type=F32)
            m1 = jnp.dot(ws_ref[2 * j + 1], vj, preferred_element_type=F32)
            halves.append(jnp.where(lane < A_GD, m0, m1))
        mixes.append(jnp.concatenate(halves, axis=1) + bias_ref[...])
    ya = u * jnp.concatenate(mixes, axis=0)

    bg = p[:, OFF_BB:OFF_BC]
    z = p[:, OFF_BC:OFF_BX] * p[:, OFF_BX:OFF_Q]
    row = lax.broadcasted_iota(jnp.int32, z.shape, 0)
    inner = jnp.logical_not(is_lat)
    z_prev = jnp.where(row == 0, zp, pltpu.roll(z, 1, 0))
    z_prev = jnp.where(jnp.logical_and(inner, row % CTX == 0), 0.0, z_prev)
    z_next = jnp.where(row == tm - 1, zn, pltpu.roll(z, tm - 1, 0))
    z_next = jnp.where(jnp.logical_and(inner, row % CTX == CTX - 1), 0.0, z_next)
    yb = bg * (z_prev * wconv_ref[0:1] + z * wconv_ref[1:2] + z_next * wconv_ref[2:3])
    yab_ref[...] = jnp.concatenate([ya, yb], axis=1).astype(BF16)

    cos = cos_ref[...]
    sin = sin_ref[...]
    bd = bd_ref[...]
    q = _rope(_group_rms(p[:, OFF_Q:OFF_K], gq_ref[...], bd), cos, sin)
    q = q * (QK_DIM ** -0.5 * LOG2E)
    q_ref[...] = q.astype(BF16)
    k = _rope(_group_rms(p[:, OFF_K:OFF_V], gk_ref[...], bd), cos, sin)
    kk_ref[...] = k.astype(BF16)


def _in_mix(x_first, x_second, g, mod3, w_bf, cos_t, sin_t, gv, ws_bf, bias_t, wconv, gq, gk, bd):
    tm = TM_IN
    n_first = x_first.shape[0] // tm
    hb = tm // 16
    nhb_first = x_first.shape[0] // 16
    nhb_second = x_second.shape[0] // 16
    pos_blocks = L // tm

    def tab_map(i):
        return (jnp.where(i < NL // tm, i % pos_blocks, pos_blocks), 0)

    def halo_specs(shift):
        blk = lambda i: (i * tm + shift) // 16
        return [pl.BlockSpec((16, D), lambda i: (jnp.clip(blk(i), 0, nhb_first - 1), 0)),
                pl.BlockSpec((16, D), lambda i: (jnp.clip(blk(i) - nhb_first, 0, nhb_second - 1), 0))]

    const2 = lambda i: (0, 0)
    row512 = pl.BlockSpec((tm, 512), lambda i: (i, 0))
    return pl.pallas_call(
        functools.partial(_in_mix_kernel, n_first=n_first),
        out_shape=[jax.ShapeDtypeStruct((NR, 512), BF16)] * 4,
        grid=(NR // tm,),
        in_specs=_two_source_specs(tm, n_first) + halo_specs(-1) + halo_specs(tm)
                 + [pl.BlockSpec((1, D), const2), _mod_spec(tm, 0), _mod_spec(tm, 1),
                    pl.BlockSpec((D, D_IN), const2),
                    pl.BlockSpec((tm, 128), tab_map), pl.BlockSpec((tm, 128), tab_map),
                    pl.BlockSpec((1, A_WIDTH), const2),
                    pl.BlockSpec((A_GROUPS, CHUNK, CHUNK), lambda i: (0, 0, 0)),
                    pl.BlockSpec((CHUNK, A_WIDTH), const2),
                    pl.BlockSpec((8, B_WIDTH), const2),
                    pl.BlockSpec((1, 512), const2), pl.BlockSpec((1, 512), const2),
                    pl.BlockSpec((RMS_BLOCK, RMS_BLOCK), const2)],
        out_specs=[row512] * 4,
        compiler_params=_cparams(("parallel",), 56),
        name="in_mix",
    )(x_first, x_second, x_first, x_second, x_first, x_second, g.reshape(1, D), mod3, mod3, w_bf,
      cos_t, sin_t, gv, ws_bf, bias_t, wconv, gq, gk, bd)


def _attn_kernel(lam_ref, q_ref, *rest, n_seg, coef, tq):
    kv_refs = rest[:2 * n_seg]
    gsub_ref, o_ref, k_scr, vt_scr = rest[2 * n_seg:]

    @pl.when(pl.program_id(2) == 0)
    def _():
        off = 0
        for s in range(n_seg):
            n = kv_refs[s].shape[0]
            k_scr[off:off + n, :] = kv_refs[s][...]
            vt_scr[0:V_DIM, off:off + n] = kv_refs[n_seg + s][...].astype(F32).T.astype(BF16)
            off += n
        ones_row = lax.broadcasted_iota(jnp.int32, (16, off), 0) == 0
        vt_scr[V_DIM:, :] = jnp.where(ones_row, 1.0, 0.0).astype(BF16)

    lam = lam_ref[0]
    shift = lam_ref[1]
    qc = ATT_CHAIN

    def scores(c):
        rows = slice(c * qc, (c + 1) * qc)
        q = q_ref[rows, :]
        lane = lax.broadcasted_iota(jnp.int32, q.shape, 1)
        zero = jnp.zeros_like(q)
        qs = jnp.concatenate([jnp.where(lane < QK_DIM, q, zero), jnp.where(lane >= QK_DIM, q, zero)], axis=0)
        return lax.dot_general(k_scr[...], qs, _DN_T, preferred_element_type=F32)

    def finish(c, pt):
        ot = jnp.dot(vt_scr[...], pt, preferred_element_type=F32)
        inv = 1.0 / ot[V_DIM:V_DIM + 1, :]
        dt = ot[0:V_DIM, :qc] * inv[:, :qc] - ot[0:V_DIM, qc:] * (lam * inv[:, qc:])
        o = dt.T
        ms = jnp.mean(o * o, axis=-1, keepdims=True)
        o_ref[c * qc:(c + 1) * qc, :] = (o * lax.rsqrt(ms + EPS) * gsub_ref[...] * coef).astype(o_ref.dtype)

    @pl.when(lam_ref[2] > 0.5)
    def _():
        for c in range(tq // qc):
            finish(c, jnp.exp2(scores(c) - shift).astype(BF16))

    @pl.when(lam_ref[2] <= 0.5)
    def _():
        for g0 in range(0, tq // qc, ATT_GROUP):
            group = range(g0, min(g0 + ATT_GROUP, tq // qc))
            sts = [scores(c) for c in group]
            for c, st in zip(group, sts):
                finish(c, jnp.exp2(st - jnp.max(st, axis=0, keepdims=True)).astype(BF16))


def _attention(lam, q, kk, v, gsub, coef, *, ctx_queries):
    if ctx_queries:
        tq = CTX
        nq, lk, n_seg = 1, CTX, 1
        q_map = lambda b, h, qi: (NL // tq + b, h)
        kv_specs = [pl.BlockSpec((CTX, 128), lambda b, h, qi: (NL // CTX + b, h)),
                    pl.BlockSpec((CTX, 128), lambda b, h, qi: (NL // CTX + b, h))]
        kv_args = [kk, v]
        rows = NC
        o_map = lambda b, h, qi: (b, h)
    else:
        tq = TQ
        nq, lk, n_seg = L // tq, CTX + L, 2
        q_map = lambda b, h, qi: (b * (L // tq) + qi, h)
        kv_specs = [pl.BlockSpec((CTX, 128), lambda b, h, qi: (NL // CTX + b, h)),
                    pl.BlockSpec((L, 128), lambda b, h, qi: (b, h)),
                    pl.BlockSpec((CTX, 128), lambda b, h, qi: (NL // CTX + b, h)),
                    pl.BlockSpec((L, 128), lambda b, h, qi: (b, h))]
        kv_args = [kk, kk, v, v]
        rows = NL
        o_map = lambda b, h, qi: (b * (L // tq) + qi, h)
    return pl.pallas_call(
        functools.partial(_attn_kernel, n_seg=n_seg, coef=coef, tq=tq),
        out_shape=jax.ShapeDtypeStruct((rows, C_WIDTH), BF16),
        grid=(B, HEADS, nq),
        in_specs=[pl.BlockSpec(memory_space=pltpu.SMEM),
                  pl.BlockSpec((tq, 128), q_map)]
                 + kv_specs + [pl.BlockSpec((1, V_DIM), lambda b, h, qi: (0, 0))],
        out_specs=pl.BlockSpec((tq, 128), o_map),
        scratch_shapes=[pltpu.VMEM((lk, 128), BF16), pltpu.VMEM((V_DIM + 16, lk), BF16)],
        compiler_params=_cparams(("parallel", "parallel", "arbitrary"), 56),
        name="attn_ctx" if ctx_queries else "attn_lat",
    )(lam, q, *kv_args, gsub)


def _pack_rows(t, out_ref, row0=0):
    half = D // 2
    w = pltpu.pack_elementwise([t[:, :half], t[:, half:]], packed_dtype=BF16)
    w = lax.bitcast_convert_type(w, jnp.uint32)
    rows = t.shape[0]
    for j in range(4):
        out_ref[pl.ds(4 * row0 + j, rows, stride=4), :] = w[:, j * 128:(j + 1) * 128]


def _unpack_rows(ref, rows, lead=None, row0=0):
    los, his = [], []
    for j in range(4):
        sl = pl.ds(4 * row0 + j, rows, stride=4)
        w = ref[sl, :] if lead is None else ref[lead, sl, :]
        los.append(pltpu.unpack_elementwise(w, index=0, packed_dtype=BF16, unpacked_dtype=F32))
        his.append(pltpu.unpack_elementwise(w, index=1, packed_dtype=BF16, unpacked_dtype=F32))
    return jnp.concatenate(los, axis=1), jnp.concatenate(his, axis=1)


def _out_router_kernel(xa_ref, xb_ref, yab_ref, yca_ref, ycb_ref, wo_ref, g1_ref, g2n_ref, sh2_ref, sc2_ref, wrh_ref,
                       wrl_ref, br_ref, xo_ref, h2p_ref, idx_ref, gate_ref, rank_ref, cnt_ref, run_ref, *,
                       n_first, n_first_c):
    tm = TM_OUT

    @pl.when(pl.program_id(0) == 0)
    def _():
        run_ref[...] = jnp.zeros_like(run_ref)

    yc = _two_source_rows(yca_ref, ycb_ref, n_first_c)
    y = jnp.dot(jnp.concatenate([yab_ref[...], yc], axis=1), wo_ref[...], preferred_element_type=F32)
    x = _two_source_rows(xa_ref, xb_ref, n_first) + g1_ref[...] * y
    xo_ref[...] = x
    h2 = _rms_mod(x, g2n_ref[...], sc2_ref[...], sh2_ref[...])
    hi = h2.astype(BF16)
    _pack_rows(h2, h2p_ref)
    lo = (h2 - hi.astype(F32)).astype(BF16)
    wh = wrh_ref[...]
    z = (lax.dot_general(wh, hi, _DN_T, preferred_element_type=F32)
         + lax.dot_general(wh, lo, _DN_T, preferred_element_type=F32)
         + lax.dot_general(wrl_ref[...], hi, _DN_T, preferred_element_type=F32))
    scores = jax.nn.sigmoid(z)
    work = scores + br_ref[...]
    eio = lax.broadcasted_iota(jnp.int32, work.shape, 0)
    idxs, sels, hits = [], [], []
    for _ in range(TOP_K):
        m = jnp.max(work, axis=0, keepdims=True)
        idx = jnp.min(jnp.where(work == m, eio, N_EXPERTS), axis=0, keepdims=True)
        hit = eio == idx
        sels.append(jnp.sum(jnp.where(hit, scores, 0.0), axis=0, keepdims=True))
        idxs.append(idx)
        hits.append(hit)
        work = jnp.where(hit, -jnp.inf, work)
    sel = jnp.concatenate(sels, axis=0)
    idx8 = jnp.concatenate(idxs, axis=0)
    gate8 = sel / jnp.sum(sel, axis=0, keepdims=True) * ROUTED_SCALE

    chosen = functools.reduce(jnp.logical_or, hits)
    before = (lax.broadcasted_iota(jnp.int32, (tm, tm), 0) < lax.broadcasted_iota(jnp.int32, (tm, tm), 1))
    prefix = jnp.dot(jnp.where(chosen, 1.0, 0.0).astype(BF16), jnp.where(before, 1.0, 0.0).astype(BF16),
                     preferred_element_type=F32)
    rank_dense = prefix + run_ref[:, 0:1]
    rank8 = jnp.concatenate([jnp.sum(jnp.where(h, rank_dense, 0.0), axis=0, keepdims=True) for h in hits],
                            axis=0).astype(jnp.int32)
    run = run_ref[...] + jnp.sum(jnp.where(chosen, 1.0, 0.0), axis=1, keepdims=True)
    run_ref[...] = run
    cnt_ref[...] = run
    for c in range(tm // 128):
        idx_ref[c] = idx8[:, c * 128:(c + 1) * 128]
        gate_ref[c] = gate8[:, c * 128:(c + 1) * 128]
        rank_ref[c] = rank8[:, c * 128:(c + 1) * 128]


def _out_router(x_first, x_second, yab, yc_first, yc_second, wo_bf, g2n, mod3, wr_hi, wr_lo, br, n_rows):
    tm = TM_OUT
    n_first = x_first.shape[0] // tm
    n_first_c = yc_first.shape[0] // tm
    const2 = lambda i: (0, 0)
    row = lambda i: (i, 0)
    chunk3 = pl.BlockSpec((tm // 128, TOP_K, 128), lambda i: (i, 0, 0))
    nch = n_rows // 128
    return pl.pallas_call(
        functools.partial(_out_router_kernel, n_first=n_first, n_first_c=n_first_c),
        out_shape=[jax.ShapeDtypeStruct((n_rows, D), F32),
                   jax.ShapeDtypeStruct((n_rows * 4, 128), jnp.uint32),
                   jax.ShapeDtypeStruct((nch, TOP_K, 128), jnp.int32),
                   jax.ShapeDtypeStruct((nch, TOP_K, 128), F32),
                   jax.ShapeDtypeStruct((nch, TOP_K, 128), jnp.int32),
                   jax.ShapeDtypeStruct((N_EXPERTS, 128), F32)],
        grid=(n_rows // tm,),
        in_specs=_two_source_specs(tm, n_first)
                 + [pl.BlockSpec((tm, 512), row)]
                 + _two_source_specs(tm, n_first_c, C_WIDTH)
                 + [pl.BlockSpec((D, D), const2),
                  _mod_spec(tm, 2),
                  pl.BlockSpec((1, D), const2), _mod_spec(tm, 3), _mod_spec(tm, 4),
                  pl.BlockSpec((N_EXPERTS, D), const2), pl.BlockSpec((N_EXPERTS, D), const2),
                  pl.BlockSpec((N_EXPERTS, 1), const2)],
        out_specs=[pl.BlockSpec((tm, D), row), pl.BlockSpec((tm * 4, 128), row),
                   chunk3, chunk3, chunk3, pl.BlockSpec((N_EXPERTS, 128), const2)],
        scratch_shapes=[pltpu.VMEM((N_EXPERTS, 128), F32)],
        compiler_params=_cparams(("arbitrary",), 48),
        name="out_router",
    )(x_first, x_second, yab, yc_first, yc_second, wo_bf, mod3, g2n.reshape(1, D), mod3, mod3, wr_hi, wr_lo,
      br.reshape(N_EXPERTS, 1))


def _experts_kernel(ts_ref, nt_ref, cnt_ref, nu_ref, x_hbm, wg_ref, wu_ref, wd_ref, y_hbm,
                    wgu_s, wd_s, x_ring, x_sems, y_buf, y_sems):
    e = pl.program_id(0)
    n_used = nu_ref[0]
    t0 = ts_ref[e]
    n_tiles = nt_ref[e]
    count = cnt_ref[e]
    ahead = X_RING - 1
    rows4 = 4 * TM_X

    def x_copy(t):
        slot = t % X_RING
        return pltpu.make_async_copy(x_hbm.at[pl.ds(t * rows4, rows4)], x_ring.at[slot], x_sems.at[slot])

    def y_copy(t):
        slot = t % 2
        return pltpu.make_async_copy(y_buf.at[slot], y_hbm.at[pl.ds(t * rows4, rows4)], y_sems.at[slot])

    @pl.when(e == 0)
    def _():
        for t in range(ahead):
            @pl.when(t < n_used)
            def _():
                x_copy(t).start()

    @pl.when(n_tiles > 0)
    def _():
        wgu_s[:, 0:D_EXPERT] = wg_ref[...].astype(BF16)
        wgu_s[:, D_EXPERT:] = wu_ref[...].astype(BF16)
        wd_s[...] = wd_ref[...].astype(BF16)

    def first_dot(x_ref, row0, rows):
        x_lo, x_hi = _unpack_rows(x_ref, rows, row0=row0)
        x = jnp.concatenate([x_lo.astype(BF16), x_hi.astype(BF16)], axis=1)
        return jnp.dot(x, wgu_s[...], preferred_element_type=F32)

    def second_dot(ab, y_ref, row0):
        a = ab[:, :D_EXPERT]
        hid = (a * jax.nn.sigmoid(a) * ab[:, D_EXPERT:]).astype(BF16)
        _pack_rows(jnp.dot(hid, wd_s[...], preferred_element_type=F32), y_ref, row0=row0)

    def tile(j, carry):
        t = t0 + j

        @pl.when(t + ahead < n_used)
        def _():
            x_copy(t + ahead).start()

        x_copy(t).wait()

        @pl.when(t >= 2)
        def _():
            y_copy(t - 2).wait()

        x_ref = x_ring.at[t % X_RING]
        y_ref = y_buf.at[t % 2]
        valid = count - j * TM_X

        @pl.when(valid > X_TAIL_ROWS)
        def _():
            rc = TM_X // X_CHAINS
            abs_ = [first_dot(x_ref, c * rc, rc) for c in range(X_CHAINS)]
            for c in range(X_CHAINS):
                second_dot(abs_[c], y_ref, c * rc)

        for c in range(X_TAIL_ROWS // X_TAIL):
            @pl.when(jnp.logical_and(valid <= X_TAIL_ROWS, c * X_TAIL < valid))
            def _():
                second_dot(first_dot(x_ref, c * X_TAIL, X_TAIL), y_ref, c * X_TAIL)

        y_copy(t).start()
        return carry

    lax.fori_loop(0, n_tiles, tile, 0)

    @pl.when(e == N_EXPERTS - 1)
    def _():
        for d in (2, 1):
            @pl.when(n_used >= d)
            def _():
                y_copy(n_used - d).wait()


def _experts(layer, sched, x_sorted, w_gate, w_up, w_down):
    tm = TM_X
    r_pad = x_sorted.shape[0] // 4
    hbm = pl.BlockSpec(memory_space=pl.ANY)
    wmap = lambda e, ts, nt, cnt, nu: (layer, e, 0, 0)
    grid_spec = pltpu.PrefetchScalarGridSpec(
        num_scalar_prefetch=4,
        grid=(N_EXPERTS,),
        in_specs=[hbm,
                  pl.BlockSpec((None, None, D, D_EXPERT), wmap),
                  pl.BlockSpec((None, None, D, D_EXPERT), wmap),
                  pl.BlockSpec((None, None, D_EXPERT, D), wmap)],
        out_specs=hbm,
        scratch_shapes=[pltpu.VMEM((D, 2 * D_EXPERT), BF16), pltpu.VMEM((D_EXPERT, D), BF16),
                        pltpu.VMEM((X_RING, tm * 4, 128), jnp.uint32), pltpu.SemaphoreType.DMA((X_RING,)),
                        pltpu.VMEM((2, tm * 4, 128), jnp.uint32), pltpu.SemaphoreType.DMA((2,))],
    )
    return pl.pallas_call(
        _experts_kernel,
        out_shape=jax.ShapeDtypeStruct((r_pad * 4, 128), jnp.uint32),
        grid_spec=grid_spec,
        compiler_params=_cparams(("arbitrary",), 48),
        name="experts",
    )(*sched, x_sorted, w_gate, w_up, w_down)


SC_CORES = 2
SC_SUBCORES = 16
SC_WORKERS = SC_CORES * SC_SUBCORES
SC_CHUNK = 128


def _sc_mesh():
    return plsc.VectorSubcoreMesh(core_axis_name="c", subcore_axis_name="s")


def _sc_params():
    return pltpu.CompilerParams(use_tc_tiling_on_sc=True)


def _sc_dispatch(h2p, pos3, r_pad):
    nch = pos3.shape[0]
    steps = -(-nch // SC_WORKERS)

    def body(h_hbm, pos_hbm, out_hbm, idx_v, rows_v, sem):
        wid = lax.axis_index("s") * SC_CORES + lax.axis_index("c")

        @pl.loop(0, steps)
        def _(s):
            ch = wid + s * SC_WORKERS

            @pl.when(ch < nch)
            def _():
                pltpu.sync_copy(pos_hbm.at[ch], idx_v)
                pltpu.sync_copy(h_hbm.at[pl.ds(ch * SC_CHUNK, SC_CHUNK)], rows_v)
                copies = [pltpu.async_copy(rows_v, out_hbm.at[idx_v.at[k]], sem) for k in range(TOP_K)]
                for cp in copies:
                    cp.wait()

    return pl.kernel(
        body,
        out_type=jax.ShapeDtypeStruct((r_pad, 4, 128), jnp.uint32),
        mesh=_sc_mesh(),
        scratch_types=[pltpu.VMEM((TOP_K, SC_CHUNK), jnp.int32),
                       pltpu.VMEM((SC_CHUNK, 4, 128), jnp.uint32),
                       pltpu.SemaphoreType.DMA],
        compiler_params=_sc_params(),
        name="sc_dispatch",
    )(h2p, pos3)


def _sc_collect(y_sorted, pos3):
    nch = pos3.shape[0]
    steps = -(-nch // SC_WORKERS)
    half = SC_CHUNK // 2
    units = [(k, hh) for k in range(TOP_K) for hh in range(2)]

    def body(y_hbm, pos_hbm, out_hbm, idx_v, rows_a, rows_b, sem_a, sem_b):
        wid = lax.axis_index("s") * SC_CORES + lax.axis_index("c")
        bufs = (rows_a, rows_b)
        sems = (sem_a, sem_b)

        def gather(u):
            k, hh = units[u]
            return pltpu.async_copy(y_hbm.at[idx_v.at[k, pl.ds(hh * half, half)]], bufs[u % 2], sems[u % 2])

        @pl.loop(0, steps)
        def _(s):
            ch = wid + s * SC_WORKERS

            @pl.when(ch < nch)
            def _():
                pltpu.sync_copy(pos_hbm.at[ch], idx_v)
                pending = gather(0)
                for u, (k, hh) in enumerate(units):
                    nxt = gather(u + 1) if u + 1 < len(units) else None
                    pending.wait()
                    pltpu.sync_copy(bufs[u % 2], out_hbm.at[k, pl.ds(ch * SC_CHUNK + hh * half, half)])
                    pending = nxt

    return pl.kernel(
        body,
        out_type=jax.ShapeDtypeStruct((TOP_K, nch * SC_CHUNK, 4, 128), jnp.uint32),
        mesh=_sc_mesh(),
        scratch_types=[pltpu.VMEM((TOP_K, SC_CHUNK), jnp.int32),
                       pltpu.VMEM((half, 4, 128), jnp.uint32),
                       pltpu.VMEM((half, 4, 128), jnp.uint32),
                       pltpu.SemaphoreType.DMA, pltpu.SemaphoreType.DMA],
        compiler_params=_sc_params(),
        name="sc_collect",
    )(y_sorted, pos3)


def _shared_kernel(x_ref, h2p_ref, yg_ref, gate_ref, wg_ref, wu_ref, wd_ref, g2_ref, o_ref):
    h_lo, h_hi = _unpack_rows(h2p_ref, TM_F)
    h = jnp.concatenate([h_lo.astype(BF16), h_hi.astype(BF16)], axis=1)
    a = jnp.dot(h, wg_ref[...], preferred_element_type=F32)
    b = jnp.dot(h, wu_ref[...], preferred_element_type=F32)
    hid = (a * jax.nn.sigmoid(a) * b).astype(BF16)
    f = jnp.dot(hid, wd_ref[...], preferred_element_type=F32)
    gate = gate_ref[...]
    f_lo = f[:, :D // 2]
    f_hi = f[:, D // 2:]
    for k in range(TOP_K):
        y_lo, y_hi = _unpack_rows(yg_ref, TM_F, lead=k)
        f_lo = f_lo + gate[:, k:k + 1] * y_lo
        f_hi = f_hi + gate[:, k:k + 1] * y_hi
    o_ref[...] = x_ref[...] + g2_ref[...] * jnp.concatenate([f_lo, f_hi], axis=1)


def _shared_residual(xa, h2p, yg, gates, wsg_bf, wsu_bf, wsd_bf, mod3, n_rows):
    tm = TM_F
    row = lambda i: (i, 0)
    const2 = lambda i: (0, 0)
    return pl.pallas_call(
        _shared_kernel,
        out_shape=jax.ShapeDtypeStruct((n_rows, D), F32),
        grid=(n_rows // tm,),
        in_specs=[pl.BlockSpec((tm, D), row), pl.BlockSpec((tm * 4, 128), row),
                  pl.BlockSpec((TOP_K, tm * 4, 128), lambda i: (0, i, 0)),
                  pl.BlockSpec((tm, TOP_K), row),
                  pl.BlockSpec((D, D_SHARED), const2), pl.BlockSpec((D, D_SHARED), const2),
                  pl.BlockSpec((D_SHARED, D), const2), _mod_spec(tm, 5)],
        out_specs=pl.BlockSpec((tm, D), row),
        compiler_params=_cparams(("parallel",), 48),
        name="shared_residual",
    )(xa, h2p, yg, gates, wsg_bf, wsu_bf, wsd_bf, mod3)


def _positions_kernel(offs_ref, idx_ref, rank_ref, pos_ref):
    idx = idx_ref[...]
    base = jnp.zeros_like(idx)
    for e in range(N_EXPERTS):
        base = jnp.where(idx == e, offs_ref[e], base)
    pos_ref[...] = rank_ref[...] + base


def _positions(offs, idx3, rank3):
    nch = idx3.shape[0]
    cb = nch // 2
    spec = pl.BlockSpec((cb, TOP_K, 128), lambda i, offs: (i, 0, 0))
    return pl.pallas_call(
        _positions_kernel,
        out_shape=jax.ShapeDtypeStruct((nch, TOP_K, 128), jnp.int32),
        grid_spec=pltpu.PrefetchScalarGridSpec(num_scalar_prefetch=1, grid=(nch // cb,),
                                               in_specs=[spec, spec], out_specs=spec),
        compiler_params=_cparams(("parallel",)),
        name="positions",
    )(offs, idx3, rank3)


def _route_positions(idx3, rank3, counts, n_rows):
    tm = TM_X
    counts = counts.astype(jnp.int32)
    padded = ((counts + tm - 1) // tm) * tm
    ends = jnp.cumsum(padded)
    offs = ends - padded
    pos3 = _positions(offs.astype(jnp.int32), idx3, rank3)
    r_pad = n_rows * TOP_K + N_EXPERTS * tm
    n_used = (ends[-1] // tm).astype(jnp.int32).reshape(1)
    sched = ((offs // tm).astype(jnp.int32), (padded // tm).astype(jnp.int32), counts, n_used)
    return pos3, sched, r_pad


def _in_weights(w):
    def regroup(cols):
        return cols.reshape(D, 2, HEADS, QK_DIM).transpose(0, 2, 1, 3).reshape(D, 2 * HEADS * QK_DIM)
    return jnp.concatenate([w[:, :OFF_Q], regroup(w[:, OFF_Q:OFF_K]), regroup(w[:, OFF_K:OFF_V]), w[:, OFF_V:]],
                           axis=1).astype(BF16)


def _rope_tables():
    t = jnp.arange(L)
    row = (t // GRID_W).astype(F32)
    col = (t % GRID_W).astype(F32)
    n_freq = QK_DIM // 4
    inv = ROPE_BASE ** (-jnp.arange(n_freq, dtype=F32) / n_freq)
    ar = row[:, None] * inv
    ac = col[:, None] * inv
    cos64 = jnp.concatenate([jnp.cos(ar), jnp.cos(ar), jnp.cos(ac), jnp.cos(ac)], axis=1)
    sin64 = jnp.concatenate([-jnp.sin(ar), jnp.sin(ar), -jnp.sin(ac), jnp.sin(ac)], axis=1)
    cos_t = jnp.concatenate([jnp.tile(cos64, (1, 2)), jnp.ones((TM_IN, 128), F32)], axis=0)
    sin_t = jnp.concatenate([jnp.tile(sin64, (1, 2)), jnp.zeros((TM_IN, 128), F32)], axis=0)
    return cos_t, sin_t


def _split_bf16(w):
    hi = w.astype(BF16)
    return hi, (w - hi.astype(F32)).astype(BF16)


def kernel(x, c, ctx, c_ctx, w_ada, b_ada, g_norm1, g_norm2, w_in, w_out, g_v, w_s, b_s, w_conv, g_q, g_k,
           lam_q1, lam_k1, lam_q2, lam_k2, g_sub, w_router, b_router, w_gate, w_up, w_down,
           ws_gate, ws_up, ws_down):
    src = (x.reshape(NL, D), ctx.reshape(NC, D))
    cc = jnp.concatenate([c, c_ctx[None, :], jnp.zeros((MOD_ROWS - B - 1, D), F32)], axis=0)
    mod = _ada(cc, w_ada, b_ada)
    cos_t, sin_t = _rope_tables()
    bd = jnp.asarray(np.kron(np.eye(RMS_BLOCK // QK_DIM, dtype=np.float32),
                             np.full((QK_DIM, QK_DIM), 1.0 / QK_DIM, np.float32)), BF16)

    for l in range(DEPTH):
        last = l == DEPTH - 1
        lam_init = 0.8 - 0.6 * math.exp(-0.3 * l)
        lam = (jnp.exp(jnp.sum(lam_q1[l] * lam_k1[l])) - jnp.exp(jnp.sum(lam_q2[l] * lam_k2[l])) + lam_init)
        bound = (QK_DIM * jnp.max(jnp.abs(g_q[l])) * jnp.max(jnp.abs(g_k[l]))
                 * (QK_DIM ** -0.5 * LOG2E) * ATT_BOUND_MARGIN)
        use_bound = (2.0 * bound < ATT_MAX_SHIFT_RANGE).astype(F32)
        lam = jnp.stack([lam, bound, use_bound]).astype(F32)
        mod3 = mod[l].reshape(MOD_ROWS, 1, 6 * D)
        w_in_bf = _in_weights(w_in[l])
        bias_t = jnp.repeat(b_s[l].T, A_GD, axis=1)
        wconv = jnp.concatenate([w_conv[l], jnp.zeros((5, B_WIDTH), F32)], axis=0)
        yab, q, kk, v = _in_mix(src[0], src[1], g_norm1[l], mod3, w_in_bf, cos_t, sin_t,
                                     g_v[l].reshape(1, A_WIDTH), w_s[l].astype(BF16), bias_t, wconv,
                                     jnp.tile(g_q[l], 8).reshape(1, 512), jnp.tile(g_k[l], 8).reshape(1, 512), bd)
        gsub = g_sub[l].reshape(1, V_DIM)
        coef = 1.0 - lam_init
        n_rows = NL if last else NR
        yc = _attention(lam, q, kk, v, gsub, coef, ctx_queries=False)
        yc_ctx = yc if last else _attention(lam, q, kk, v, gsub, coef, ctx_queries=True)
        wr_hi, wr_lo = _split_bf16(w_router[l].T)
        wo_bf = w_out[l].astype(BF16)
        ws_bf = (ws_gate[l].astype(BF16), ws_up[l].astype(BF16), ws_down[l].astype(BF16))
        xa, h2p, idx3, gate3, rank3, counts = _out_router(
            src[0], src[1], yab, yc, yc_ctx, wo_bf, g_norm2[l], mod3, wr_hi, wr_lo, b_router[l], n_rows)
        pos3, sched, r_pad = _route_positions(idx3, rank3, counts[:, 0], n_rows)
        x_sorted = _sc_dispatch(h2p.reshape(n_rows, 4, 128), pos3, r_pad)
        y_sorted = _experts(l, sched, x_sorted.reshape(r_pad * 4, 128), w_gate, w_up, w_down)
        yg = _sc_collect(y_sorted.reshape(r_pad, 4, 128), pos3)
        gates = gate3.transpose(0, 2, 1).reshape(n_rows, TOP_K)
        xa = _shared_residual(xa, h2p, yg.reshape(TOP_K, n_rows * 4, 128), gates, *ws_bf, mod3, n_rows)
        src = (xa, xa)
    return xa.reshape(B, L, D)
```

```python
import functools
import math

import numpy as np
import jax
import jax.numpy as jnp
from jax import lax
from jax.experimental import pallas as pl
from jax.experimental.pallas import tpu as pltpu
from jax.experimental.pallas import tpu_sc as plsc

F32 = jnp.float32
BF16 = jnp.bfloat16

D = 1024
B = 8
L = 2048
DEPTH = 2
GRID_W = 64
CTX = 256
A_WIDTH = 256
A_GROUPS = 4
A_GD = 64
CHUNK = 128
B_WIDTH = 256
C_WIDTH = 512
HEADS = 4
V_DIM = 128
QK_DIM = 64
ROPE_BASE = 10000.0
OFF_BB = 512
OFF_BC = 768
OFF_BX = 1024
OFF_Q = 1280
OFF_K = 1792
OFF_V = 2304
D_IN = 2816
N_EXPERTS = 64
TOP_K = 8
D_EXPERT = 256
D_SHARED = 256
ROUTED_SCALE = 2.5
EPS = 1e-6

NL = B * L
NC = B * CTX
NR = NL + NC
MOD_ROWS = 16
LOG2E = 1.4426950408889634

TM_IN = 1024
RMS_BLOCK = 256
TQ = 2048
ATT_CHAIN = 256
ATT_GROUP = 4
ATT_BOUND_MARGIN = 1.02
ATT_MAX_SHIFT_RANGE = 100.0
TM_OUT = 1024
TM_X = 1024
X_CHAINS = 2
X_TAIL_ROWS = 512
X_TAIL = 256
X_RING = 3
TM_F = 512

_DN_T = (((1,), (1,)), ((), ()))


def _cparams(sem, vmem_mb=None):
    kw = dict(dimension_semantics=sem)
    if vmem_mb is not None:
        kw["vmem_limit_bytes"] = vmem_mb * 1024 * 1024
    return pltpu.CompilerParams(**kw)


def _mod_row(i, tm):
    return jnp.where(i < NL // tm, i // (L // tm), B)


def _mod_spec(tm, chunk):
    return pl.BlockSpec((None, 1, D), lambda i: (_mod_row(i, tm), 0, chunk))


def _ada_kernel(c_ref, w_ref, b_ref, o_ref):
    c = c_ref[...]
    cs = c * jax.nn.sigmoid(c)
    o_ref[...] = jnp.dot(cs, w_ref[...], preferred_element_type=F32,
                         precision=lax.Precision.HIGHEST) + b_ref[...]


def _ada(cc, w_ada, b_ada):
    nb = 6
    return pl.pallas_call(
        _ada_kernel,
        out_shape=jax.ShapeDtypeStruct((DEPTH, MOD_ROWS, 6 * D), F32),
        grid=(DEPTH, nb),
        in_specs=[pl.BlockSpec((MOD_ROWS, D), lambda l, j: (0, 0)),
                  pl.BlockSpec((None, D, D), lambda l, j: (l, 0, j)),
                  pl.BlockSpec((None, 1, D), lambda l, j: (l, 0, j))],
        out_specs=pl.BlockSpec((None, MOD_ROWS, D), lambda l, j: (l, 0, j)),
        compiler_params=_cparams(("arbitrary", "arbitrary"), 40),
        name="ada_mod",
    )(cc, w_ada, b_ada.reshape(DEPTH, 1, 6 * D))


def _rms_mod(x, g, sc, sh):
    ms = jnp.mean(x * x, axis=-1, keepdims=True)
    return x * lax.rsqrt(ms + EPS) * (g * (1.0 + sc)) + sh


def _two_source_specs(tm, n_first, width=D):
    return [pl.BlockSpec((tm, width), lambda i: (jnp.minimum(i, n_first - 1), 0)),
            pl.BlockSpec((tm, width), lambda i: (jnp.maximum(i - n_first, 0), 0))]


def _two_source_rows(a_ref, b_ref, n_first):
    return jnp.where(pl.program_id(0) < n_first, a_ref[...], b_ref[...])


def _group_rms(t, g, bd):
    sq = (t * t).astype(BF16)
    ms = jnp.concatenate([jnp.dot(sq[:, c:c + RMS_BLOCK], bd, preferred_element_type=F32)
                          for c in range(0, t.shape[1], RMS_BLOCK)], axis=1)
    return t * lax.rsqrt(ms + EPS) * g


def _rope(t, cos, sin):
    w = t.shape[1]
    lane = lax.broadcasted_iota(jnp.int32, t.shape, 1)
    first = (lane % 32) < 16
    partner = jnp.where(first, pltpu.roll(t, w - 16, 1), pltpu.roll(t, 16, 1))
    cos4 = jnp.concatenate([cos] * (w // 128), axis=1)
    sin4 = jnp.concatenate([sin] * (w // 128), axis=1)
    return t * cos4 + partner * sin4


def _in_mix_kernel(xa_ref, xb_ref, pa_ref, pb_ref, na_ref, nb_ref, g_ref, sh_ref, sc_ref, w_ref,
                   cos_ref, sin_ref, gv_ref, ws_ref, bias_ref, wconv_ref, gq_ref, gk_ref, bd_ref,
                   yab_ref, q_ref, kk_ref, v_ref, *, n_first):
    tm = TM_IN
    i = pl.program_id(0)
    tiles_per_seq = L // tm
    is_lat = i < NL // tm
    is_start = jnp.logical_or(jnp.logical_not(is_lat), i % tiles_per_seq == 0)
    is_end = jnp.logical_or(jnp.logical_not(is_lat), i % tiles_per_seq == tiles_per_seq - 1)
    first = i < n_first
    g, sc, sh = g_ref[...], sc_ref[...], sh_ref[...]

    h = _rms_mod(jnp.where(first, xa_ref[...], xb_ref[...]), g, sc, sh)
    p = jnp.dot(h.astype(BF16), w_ref[...], preferred_element_type=F32)
    v_ref[...] = p[:, OFF_V:].astype(BF16)
    halo = jnp.concatenate([jnp.where(first, pa_ref[...], pb_ref[...]),
                            jnp.where(first, na_ref[...], nb_ref[...])], axis=0)
    ph = jnp.dot(_rms_mod(halo, g, sc, sh).astype(BF16), w_ref[:, OFF_BC:OFF_Q], preferred_element_type=F32)
    zh = ph[:, :B_WIDTH] * ph[:, B_WIDTH:]
    zp = jnp.where(is_start, 0.0, zh[15:16])
    zn = jnp.where(is_end, 0.0, zh[16:17])

    uv = p[:, 0:2 * A_WIDTH]
    uv = 0.5 * uv * (1.0 + lax.erf(uv * (2.0 ** -0.5)))
    u = uv[:, :A_WIDTH]
    v = uv[:, A_WIDTH:]
    ms = jnp.mean(v * v, axis=-1, keepdims=True)
    vb = (v * lax.rsqrt(ms + EPS) * gv_ref[...]).astype(BF16)
    lane = lax.broadcasted_iota(jnp.int32, (CHUNK, 128), 1)
    mixes = []
    for c in range(tm // CHUNK):
        vc = vb[c * CHUNK:(c + 1) * CHUNK]
        halves = []
        for j in range(2):
            vj = vc[:, j * 128:(j + 1) * 128]
            m0 = jnp.dot(ws_ref[2 * j], vj, preferred_element_type=F32)
            m1 = jnp.dot(ws_ref[2 * j + 1], vj, preferred_element_type=F32)
            halves.append(jnp.where(lane < A_GD, m0, m1))
        mixes.append(jnp.concatenate(halves, axis=1) + bias_ref[...])
    ya = u * jnp.concatenate(mixes, axis=0)

    bg = p[:, OFF_BB:OFF_BC]
    z = p[:, OFF_BC:OFF_BX] * p[:, OFF_BX:OFF_Q]
    row = lax.broadcasted_iota(jnp.int32, z.shape, 0)
    inner = jnp.logical_not(is_lat)
    z_prev = jnp.where(row == 0, zp, pltpu.roll(z, 1, 0))
    z_prev = jnp.where(jnp.logical_and(inner, row % CTX == 0), 0.0, z_prev)
    z_next = jnp.where(row == tm - 1, zn, pltpu.roll(z, tm - 1, 0))
    z_next = jnp.where(jnp.logical_and(inner, row % CTX == CTX - 1), 0.0, z_next)
    yb = bg * (z_prev * wconv_ref[0:1] + z * wconv_ref[1:2] + z_next * wconv_ref[2:3])
    yab_ref[...] = jnp.concatenate([ya, yb], axis=1).astype(BF16)

    cos = cos_ref[...]
    sin = sin_ref[...]
    bd = bd_ref[...]
    q = _rope(_group_rms(p[:, OFF_Q:OFF_K], gq_ref[...], bd), cos, sin)
    q = q * (QK_DIM ** -0.5 * LOG2E)
    q_ref[...] = q.astype(BF16)
    k = _rope(_group_rms(p[:, OFF_K:OFF_V], gk_ref[...], bd), cos, sin)
    kk_ref[...] = k.astype(BF16)


def _in_mix(x_first, x_second, g, mod3, w_bf, cos_t, sin_t, gv, ws_bf, bias_t, wconv, gq, gk, bd):
    tm = TM_IN
    n_first = x_first.shape[0] // tm
    hb = tm // 16
    nhb_first = x_first.shape[0] // 16
    nhb_second = x_second.shape[0] // 16
    pos_blocks = L // tm

    def tab_map(i):
        return (jnp.where(i < NL // tm, i % pos_blocks, pos_blocks), 0)

    def halo_specs(shift):
        blk = lambda i: (i * tm + shift) // 16
        return [pl.BlockSpec((16, D), lambda i: (jnp.clip(blk(i), 0, nhb_first - 1), 0)),
                pl.BlockSpec((16, D), lambda i: (jnp.clip(blk(i) - nhb_first, 0, nhb_second - 1), 0))]

    const2 = lambda i: (0, 0)
    row512 = pl.BlockSpec((tm, 512), lambda i: (i, 0))
    return pl.pallas_call(
        functools.partial(_in_mix_kernel, n_first=n_first),
        out_shape=[jax.ShapeDtypeStruct((NR, 512), BF16)] * 4,
        grid=(NR // tm,),
        in_specs=_two_source_specs(tm, n_first) + halo_specs(-1) + halo_specs(tm)
                 + [pl.BlockSpec((1, D), const2), _mod_spec(tm, 0), _mod_spec(tm, 1),
                    pl.BlockSpec((D, D_IN), const2),
                    pl.BlockSpec((tm, 128), tab_map), pl.BlockSpec((tm, 128), tab_map),
                    pl.BlockSpec((1, A_WIDTH), const2),
                    pl.BlockSpec((A_GROUPS, CHUNK, CHUNK), lambda i: (0, 0, 0)),
                    pl.BlockSpec((CHUNK, A_WIDTH), const2),
                    pl.BlockSpec((8, B_WIDTH), const2),
                    pl.BlockSpec((1, 512), const2), pl.BlockSpec((1, 512), const2),
                    pl.BlockSpec((RMS_BLOCK, RMS_BLOCK), const2)],
        out_specs=[row512] * 4,
        compiler_params=_cparams(("parallel",), 56),
        name="in_mix",
    )(x_first, x_second, x_first, x_second, x_first, x_second, g.reshape(1, D), mod3, mod3, w_bf,
      cos_t, sin_t, gv, ws_bf, bias_t, wconv, gq, gk, bd)


def _attn_kernel(lam_ref, q_ref, *rest, n_seg, coef, tq):
    kv_refs = rest[:2 * n_seg]
    gsub_ref, o_ref, k_scr, vt_scr = rest[2 * n_seg:]

    @pl.when(pl.program_id(2) == 0)
    def _():
        off = 0
        for s in range(n_seg):
            n = kv_refs[s].shape[0]
            k_scr[off:off + n, :] = kv_refs[s][...]
            vt_scr[0:V_DIM, off:off + n] = kv_refs[n_seg + s][...].astype(F32).T.astype(BF16)
            off += n
        ones_row = lax.broadcasted_iota(jnp.int32, (16, off), 0) == 0
        vt_scr[V_DIM:, :] = jnp.where(ones_row, 1.0, 0.0).astype(BF16)

    lam = lam_ref[0]
    shift = lam_ref[1]
    qc = ATT_CHAIN

    def scores(c):
        rows = slice(c * qc, (c + 1) * qc)
        q = q_ref[rows, :]
        lane = lax.broadcasted_iota(jnp.int32, q.shape, 1)
        zero = jnp.zeros_like(q)
        qs = jnp.concatenate([jnp.where(lane < QK_DIM, q, zero), jnp.where(lane >= QK_DIM, q, zero)], axis=0)
        return lax.dot_general(k_scr[...], qs, _DN_T, preferred_element_type=F32)

    def finish(c, pt):
        ot = jnp.dot(vt_scr[...], pt, preferred_element_type=F32)
        inv = 1.0 / ot[V_DIM:V_DIM + 1, :]
        dt = ot[0:V_DIM, :qc] * inv[:, :qc] - ot[0:V_DIM, qc:] * (lam * inv[:, qc:])
        o = dt.T
        ms = jnp.mean(o * o, axis=-1, keepdims=True)
        o_ref[c * qc:(c + 1) * qc, :] = (o * lax.rsqrt(ms + EPS) * gsub_ref[...] * coef).astype(o_ref.dtype)

    @pl.when(lam_ref[2] > 0.5)
    def _():
        for c in range(tq // qc):
            finish(c, jnp.exp2(scores(c) - shift).astype(BF16))

    @pl.when(lam_ref[2] <= 0.5)
    def _():
        for g0 in range(0, tq // qc, ATT_GROUP):
            group = range(g0, min(g0 + ATT_GROUP, tq // qc))
            sts = [scores(c) for c in group]
            for c, st in zip(group, sts):
                finish(c, jnp.exp2(st - jnp.max(st, axis=0, keepdims=True)).astype(BF16))


def _attention(lam, q, kk, v, gsub, coef, *, ctx_queries):
    if ctx_queries:
        tq = CTX
        nq, lk, n_seg = 1, CTX, 1
        q_map = lambda b, h, qi: (NL // tq + b, h)
        kv_specs = [pl.BlockSpec((CTX, 128), lambda b, h, qi: (NL // CTX + b, h)),
                    pl.BlockSpec((CTX, 128), lambda b, h, qi: (NL // CTX + b, h))]
        kv_args = [kk, v]
        rows = NC
        o_map = lambda b, h, qi: (b, h)
    else:
        tq = TQ
        nq, lk, n_seg = L // tq, CTX + L, 2
        q_map = lambda b, h, qi: (b * (L // tq) + qi, h)
        kv_specs = [pl.BlockSpec((CTX, 128), lambda b, h, qi: (NL // CTX + b, h)),
                    pl.BlockSpec((L, 128), lambda b, h, qi: (b, h)),
                    pl.BlockSpec((CTX, 128), lambda b, h, qi: (NL // CTX + b, h)),
                    pl.BlockSpec((L, 128), lambda b, h, qi: (b, h))]
        kv_args = [kk, kk, v, v]
        rows = NL
        o_map = lambda b, h, qi: (b * (L // tq) + qi, h)
    return pl.pallas_call(
        functools.partial(_attn_kernel, n_seg=n_seg, coef=coef, tq=tq),
        out_shape=jax.ShapeDtypeStruct((rows, C_WIDTH), BF16),
        grid=(B, HEADS, nq),
        in_specs=[pl.BlockSpec(memory_space=pltpu.SMEM),
                  pl.BlockSpec((tq, 128), q_map)]
                 + kv_specs + [pl.BlockSpec((1, V_DIM), lambda b, h, qi: (0, 0))],
        out_specs=pl.BlockSpec((tq, 128), o_map),
        scratch_shapes=[pltpu.VMEM((lk, 128), BF16), pltpu.VMEM((V_DIM + 16, lk), BF16)],
        compiler_params=_cparams(("parallel", "parallel", "arbitrary"), 56),
        name="attn_ctx" if ctx_queries else "attn_lat",
    )(lam, q, *kv_args, gsub)


def _pack_rows(t, out_ref, row0=0):
    half = D // 2
    w = pltpu.pack_elementwise([t[:, :half], t[:, half:]], packed_dtype=BF16)
    w = lax.bitcast_convert_type(w, jnp.uint32)
    rows = t.shape[0]
    for j in range(4):
        out_ref[pl.ds(4 * row0 + j, rows, stride=4), :] = w[:, j * 128:(j + 1) * 128]


def _unpack_rows(ref, rows, lead=None, row0=0):
    los, his = [], []
    for j in range(4):
        sl = pl.ds(4 * row0 + j, rows, stride=4)
        w = ref[sl, :] if lead is None else ref[lead, sl, :]
        los.append(pltpu.unpack_elementwise(w, index=0, packed_dtype=BF16, unpacked_dtype=F32))
        his.append(pltpu.unpack_elementwise(w, index=1, packed_dtype=BF16, unpacked_dtype=F32))
    return jnp.concatenate(los, axis=1), jnp.concatenate(his, axis=1)


def _out_router_kernel(xa_ref, xb_ref, yab_ref, yca_ref, ycb_ref, wo_ref, g1_ref, g2n_ref, sh2_ref, sc2_ref, wrh_ref,
                       wrl_ref, br_ref, xo_ref, h2p_ref, idx_ref, gate_ref, rank_ref, cnt_ref, run_ref, *,
                       n_first, n_first_c):
    tm = TM_OUT

    @pl.when(pl.program_id(0) == 0)
    def _():
        run_ref[...] = jnp.zeros_like(run_ref)

    yc = _two_source_rows(yca_ref, ycb_ref, n_first_c)
    y = jnp.dot(jnp.concatenate([yab_ref[...], yc], axis=1), wo_ref[...], preferred_element_type=F32)
    x = _two_source_rows(xa_ref, xb_ref, n_first) + g1_ref[...] * y
    xo_ref[...] = x
    h2 = _rms_mod(x, g2n_ref[...], sc2_ref[...], sh2_ref[...])
    hi = h2.astype(BF16)
    _pack_rows(h2, h2p_ref)
    lo = (h2 - hi.astype(F32)).astype(BF16)
    wh = wrh_ref[...]
    z = (lax.dot_general(wh, hi, _DN_T, preferred_element_type=F32)
         + lax.dot_general(wh, lo, _DN_T, preferred_element_type=F32)
         + lax.dot_general(wrl_ref[...], hi, _DN_T, preferred_element_type=F32))
    scores = jax.nn.sigmoid(z)
    work = scores + br_ref[...]
    eio = lax.broadcasted_iota(jnp.int32, work.shape, 0)
    idxs, sels, hits = [], [], []
    for _ in range(TOP_K):
        m = jnp.max(work, axis=0, keepdims=True)
        idx = jnp.min(jnp.where(work == m, eio, N_EXPERTS), axis=0, keepdims=True)
        hit = eio == idx
        sels.append(jnp.sum(jnp.where(hit, scores, 0.0), axis=0, keepdims=True))
        idxs.append(idx)
        hits.append(hit)
        work = jnp.where(hit, -jnp.inf, work)
    sel = jnp.concatenate(sels, axis=0)
    idx8 = jnp.concatenate(idxs, axis=0)
    gate8 = sel / jnp.sum(sel, axis=0, keepdims=True) * ROUTED_SCALE

    chosen = functools.reduce(jnp.logical_or, hits)
    before = (lax.broadcasted_iota(jnp.int32, (tm, tm), 0) < lax.broadcasted_iota(jnp.int32, (tm, tm), 1))
    prefix = jnp.dot(jnp.where(chosen, 1.0, 0.0).astype(BF16), jnp.where(before, 1.0, 0.0).astype(BF16),
                     preferred_element_type=F32)
    rank_dense = prefix + run_ref[:, 0:1]
    rank8 = jnp.concatenate([jnp.sum(jnp.where(h, rank_dense, 0.0), axis=0, keepdims=True) for h in hits],
                            axis=0).astype(jnp.int32)
    run = run_ref[...] + jnp.sum(jnp.where(chosen, 1.0, 0.0), axis=1, keepdims=True)
    run_ref[...] = run
    cnt_ref[...] = run
    for c in range(tm // 128):
        idx_ref[c] = idx8[:, c * 128:(c + 1) * 128]
        gate_ref[c] = gate8[:, c * 128:(c + 1) * 128]
        rank_ref[c] = rank8[:, c * 128:(c + 1) * 128]


def _out_router(x_first, x_second, yab, yc_first, yc_second, wo_bf, g2n, mod3, wr_hi, wr_lo, br, n_rows):
    tm = TM_OUT
    n_first = x_first.shape[0] // tm
    n_first_c = yc_first.shape[0] // tm
    const2 = lambda i: (0, 0)
    row = lambda i: (i, 0)
    chunk3 = pl.BlockSpec((tm // 128, TOP_K, 128), lambda i: (i, 0, 0))
    nch = n_rows // 128
    return pl.pallas_call(
        functools.partial(_out_router_kernel, n_first=n_first, n_first_c=n_first_c),
        out_shape=[jax.ShapeDtypeStruct((n_rows, D), F32),
                   jax.ShapeDtypeStruct((n_rows * 4, 128), jnp.uint32),
                   jax.ShapeDtypeStruct((nch, TOP_K, 128), jnp.int32),
                   jax.ShapeDtypeStruct((nch, TOP_K, 128), F32),
                   jax.ShapeDtypeStruct((nch, TOP_K, 128), jnp.int32),
                   jax.ShapeDtypeStruct((N_EXPERTS, 128), F32)],
        grid=(n_rows // tm,),
        in_specs=_two_source_specs(tm, n_first)
                 + [pl.BlockSpec((tm, 512), row)]
                 + _two_source_specs(tm, n_first_c, C_WIDTH)
                 + [pl.BlockSpec((D, D), const2),
                  _mod_spec(tm, 2),
                  pl.BlockSpec((1, D), const2), _mod_spec(tm, 3), _mod_spec(tm, 4),
                  pl.BlockSpec((N_EXPERTS, D), const2), pl.BlockSpec((N_EXPERTS, D), const2),
                  pl.BlockSpec((N_EXPERTS, 1), const2)],
        out_specs=[pl.BlockSpec((tm, D), row), pl.BlockSpec((tm * 4, 128), row),
                   chunk3, chunk3, chunk3, pl.BlockSpec((N_EXPERTS, 128), const2)],
        scratch_shapes=[pltpu.VMEM((N_EXPERTS, 128), F32)],
        compiler_params=_cparams(("arbitrary",), 48),
        name="out_router",
    )(x_first, x_second, yab, yc_first, yc_second, wo_bf, mod3, g2n.reshape(1, D), mod3, mod3, wr_hi, wr_lo,
      br.reshape(N_EXPERTS, 1))


def _experts_kernel(ts_ref, nt_ref, cnt_ref, nu_ref, x_hbm, wg_ref, wu_ref, wd_ref, y_hbm,
                    wgu_s, wd_s, x_ring, x_sems, y_buf, y_sems):
    e = pl.program_id(0)
    n_used = nu_ref[0]
    t0 = ts_ref[e]
    n_tiles = nt_ref[e]
    count = cnt_ref[e]
    ahead = X_RING - 1
    rows4 = 4 * TM_X

    def x_copy(t):
        slot = t % X_RING
        return pltpu.make_async_copy(x_hbm.at[pl.ds(t * rows4, rows4)], x_ring.at[slot], x_sems.at[slot])

    def y_copy(t):
        slot = t % 2
        return pltpu.make_async_copy(y_buf.at[slot], y_hbm.at[pl.ds(t * rows4, rows4)], y_sems.at[slot])

    @pl.when(e == 0)
    def _():
        for t in range(ahead):
            @pl.when(t < n_used)
            def _():
                x_copy(t).start()

    @pl.when(n_tiles > 0)
    def _():
        wgu_s[:, 0:D_EXPERT] = wg_ref[...].astype(BF16)
        wgu_s[:, D_EXPERT:] = wu_ref[...].astype(BF16)
        wd_s[...] = wd_ref[...].astype(BF16)

    def first_dot(x_ref, row0, rows):
        x_lo, x_hi = _unpack_rows(x_ref, rows, row0=row0)
        x = jnp.concatenate([x_lo.astype(BF16), x_hi.astype(BF16)], axis=1)
        return jnp.dot(x, wgu_s[...], preferred_element_type=F32)

    def second_dot(ab, y_ref, row0):
        a = ab[:, :D_EXPERT]
        hid = (a * jax.nn.sigmoid(a) * ab[:, D_EXPERT:]).astype(BF16)
        _pack_rows(jnp.dot(hid, wd_s[...], preferred_element_type=F32), y_ref, row0=row0)

    def tile(j, carry):
        t = t0 + j

        @pl.when(t + ahead < n_used)
        def _():
            x_copy(t + ahead).start()

        x_copy(t).wait()

        @pl.when(t >= 2)
        def _():
            y_copy(t - 2).wait()

        x_ref = x_ring.at[t % X_RING]
        y_ref = y_buf.at[t % 2]
        valid = count - j * TM_X

        @pl.when(valid > X_TAIL_ROWS)
        def _():
            rc = TM_X // X_CHAINS
            abs_ = [first_dot(x_ref, c * rc, rc) for c in range(X_CHAINS)]
            for c in range(X_CHAINS):
                second_dot(abs_[c], y_ref, c * rc)

        for c in range(X_TAIL_ROWS // X_TAIL):
            @pl.when(jnp.logical_and(valid <= X_TAIL_ROWS, c * X_TAIL < valid))
            def _():
                second_dot(first_dot(x_ref, c * X_TAIL, X_TAIL), y_ref, c * X_TAIL)

        y_copy(t).start()
        return carry

    lax.fori_loop(0, n_tiles, tile, 0)

    @pl.when(e == N_EXPERTS - 1)
    def _():
        for d in (2, 1):
            @pl.when(n_used >= d)
            def _():
                y_copy(n_used - d).wait()


def _experts(layer, sched, x_sorted, w_gate, w_up, w_down):
    tm = TM_X
    r_pad = x_sorted.shape[0] // 4
    hbm = pl.BlockSpec(memory_space=pl.ANY)
    wmap = lambda e, ts, nt, cnt, nu: (layer, e, 0, 0)
    grid_spec = pltpu.PrefetchScalarGridSpec(
        num_scalar_prefetch=4,
        grid=(N_EXPERTS,),
        in_specs=[hbm,
                  pl.BlockSpec((None, None, D, D_EXPERT), wmap),
                  pl.BlockSpec((None, None, D, D_EXPERT), wmap),
                  pl.BlockSpec((None, None, D_EXPERT, D), wmap)],
        out_specs=hbm,
        scratch_shapes=[pltpu.VMEM((D, 2 * D_EXPERT), BF16), pltpu.VMEM((D_EXPERT, D), BF16),
                        pltpu.VMEM((X_RING, tm * 4, 128), jnp.uint32), pltpu.SemaphoreType.DMA((X_RING,)),
                        pltpu.VMEM((2, tm * 4, 128), jnp.uint32), pltpu.SemaphoreType.DMA((2,))],
    )
    return pl.pallas_call(
        _experts_kernel,
        out_shape=jax.ShapeDtypeStruct((r_pad * 4, 128), jnp.uint32),
        grid_spec=grid_spec,
        compiler_params=_cparams(("arbitrary",), 48),
        name="experts",
    )(*sched, x_sorted, w_gate, w_up, w_down)


SC_CORES = 2
SC_SUBCORES = 16
SC_WORKERS = SC_CORES * SC_SUBCORES
SC_CHUNK = 128


def _sc_mesh():
    return plsc.VectorSubcoreMesh(core_axis_name="c", subcore_axis_name="s")


def _sc_params():
    return pltpu.CompilerParams(use_tc_tiling_on_sc=True)


def _sc_dispatch(h2p, pos3, r_pad):
    nch = pos3.shape[0]
    steps = -(-nch // SC_WORKERS)

    def body(h_hbm, pos_hbm, out_hbm, idx_v, rows_v, sem):
        wid = lax.axis_index("s") * SC_CORES + lax.axis_index("c")

        @pl.loop(0, steps)
        def _(s):
            ch = wid + s * SC_WORKERS

            @pl.when(ch < nch)
            def _():
                pltpu.sync_copy(pos_hbm.at[ch], idx_v)
                pltpu.sync_copy(h_hbm.at[pl.ds(ch * SC_CHUNK, SC_CHUNK)], rows_v)
                copies = [pltpu.async_copy(rows_v, out_hbm.at[idx_v.at[k]], sem) for k in range(TOP_K)]
                for cp in copies:
                    cp.wait()

    return pl.kernel(
        body,
        out_type=jax.ShapeDtypeStruct((r_pad, 4, 128), jnp.uint32),
        mesh=_sc_mesh(),
        scratch_types=[pltpu.VMEM((TOP_K, SC_CHUNK), jnp.int32),
                       pltpu.VMEM((SC_CHUNK, 4, 128), jnp.uint32),
                       pltpu.SemaphoreType.DMA],
        compiler_params=_sc_params(),
        name="sc_dispatch",
    )(h2p, pos3)


def _sc_collect(y_sorted, pos3):
    nch = pos3.shape[0]
    steps = -(-nch // SC_WORKERS)
    half = SC_CHUNK // 2
    units = [(k, hh) for k in range(TOP_K) for hh in range(2)]

    def body(y_hbm, pos_hbm, out_hbm, idx_v, rows_a, rows_b, sem_a, sem_b):
        wid = lax.axis_index("s") * SC_CORES + lax.axis_index("c")
        bufs = (rows_a, rows_b)
        sems = (sem_a, sem_b)

        def gather(u):
            k, hh = units[u]
            return pltpu.async_copy(y_hbm.at[idx_v.at[k, pl.ds(hh * half, half)]], bufs[u % 2], sems[u % 2])

        @pl.loop(0, steps)
        def _(s):
            ch = wid + s * SC_WORKERS

            @pl.when(ch < nch)
            def _():
                pltpu.sync_copy(pos_hbm.at[ch], idx_v)
                pending = gather(0)
                for u, (k, hh) in enumerate(units):
                    nxt = gather(u + 1) if u + 1 < len(units) else None
                    pending.wait()
                    pltpu.sync_copy(bufs[u % 2], out_hbm.at[k, pl.ds(ch * SC_CHUNK + hh * half, half)])
                    pending = nxt

    return pl.kernel(
        body,
        out_type=jax.ShapeDtypeStruct((TOP_K, nch * SC_CHUNK, 4, 128), jnp.uint32),
        mesh=_sc_mesh(),
        scratch_types=[pltpu.VMEM((TOP_K, SC_CHUNK), jnp.int32),
                       pltpu.VMEM((half, 4, 128), jnp.uint32),
                       pltpu.VMEM((half, 4, 128), jnp.uint32),
                       pltpu.SemaphoreType.DMA, pltpu.SemaphoreType.DMA],
        compiler_params=_sc_params(),
        name="sc_collect",
    )(y_sorted, pos3)


def _shared_kernel(x_ref, h2p_ref, yg_ref, gate_ref, wg_ref, wu_ref, wd_ref, g2_ref, o_ref):
    h_lo, h_hi = _unpack_rows(h2p_ref, TM_F)
    h = jnp.concatenate([h_lo.astype(BF16), h_hi.astype(BF16)], axis=1)
    a = jnp.dot(h, wg_ref[...], preferred_element_type=F32)
    b = jnp.dot(h, wu_ref[...], preferred_element_type=F32)
    hid = (a * jax.nn.sigmoid(a) * b).astype(BF16)
    f = jnp.dot(hid, wd_ref[...], preferred_element_type=F32)
    gate = gate_ref[...]
    f_lo = f[:, :D // 2]
    f_hi = f[:, D // 2:]
    for k in range(TOP_K):
        y_lo, y_hi = _unpack_rows(yg_ref, TM_F, lead=k)
        f_lo = f_lo + gate[:, k:k + 1] * y_lo
        f_hi = f_hi + gate[:, k:k + 1] * y_hi
    o_ref[...] = x_ref[...] + g2_ref[...] * jnp.concatenate([f_lo, f_hi], axis=1)


def _shared_residual(xa, h2p, yg, gates, wsg_bf, wsu_bf, wsd_bf, mod3, n_rows):
    tm = TM_F
    row = lambda i: (i, 0)
    const2 = lambda i: (0, 0)
    return pl.pallas_call(
        _shared_kernel,
        out_shape=jax.ShapeDtypeStruct((n_rows, D), F32),
        grid=(n_rows // tm,),
        in_specs=[pl.BlockSpec((tm, D), row), pl.BlockSpec((tm * 4, 128), row),
                  pl.BlockSpec((TOP_K, tm * 4, 128), lambda i: (0, i, 0)),
                  pl.BlockSpec((tm, TOP_K), row),
                  pl.BlockSpec((D, D_SHARED), const2), pl.BlockSpec((D, D_SHARED), const2),
                  pl.BlockSpec((D_SHARED, D), const2), _mod_spec(tm, 5)],
        out_specs=pl.BlockSpec((tm, D), row),
        compiler_params=_cparams(("parallel",), 48),
        name="shared_residual",
    )(xa, h2p, yg, gates, wsg_bf, wsu_bf, wsd_bf, mod3)


def _positions_kernel(offs_ref, idx_ref, rank_ref, pos_ref):
    idx = idx_ref[...]
    base = jnp.zeros_like(idx)
    for e in range(N_EXPERTS):
        base = jnp.where(idx == e, offs_ref[e], base)
    pos_ref[...] = rank_ref[...] + base


def _positions(offs, idx3, rank3):
    nch = idx3.shape[0]
    cb = nch // 2
    spec = pl.BlockSpec((cb, TOP_K, 128), lambda i, offs: (i, 0, 0))
    return pl.pallas_call(
        _positions_kernel,
        out_shape=jax.ShapeDtypeStruct((nch, TOP_K, 128), jnp.int32),
        grid_spec=pltpu.PrefetchScalarGridSpec(num_scalar_prefetch=1, grid=(nch // cb,),
                                               in_specs=[spec, spec], out_specs=spec),
        compiler_params=_cparams(("parallel",)),
        name="positions",
    )(offs, idx3, rank3)


def _route_positions(idx3, rank3, counts, n_rows):
    tm = TM_X
    counts = counts.astype(jnp.int32)
    padded = ((counts + tm - 1) // tm) * tm
    ends = jnp.cumsum(padded)
    offs = ends - padded
    pos3 = _positions(offs.astype(jnp.int32), idx3, rank3)
    r_pad = n_rows * TOP_K + N_EXPERTS * tm
    n_used = (ends[-1] // tm).astype(jnp.int32).reshape(1)
    sched = ((offs // tm).astype(jnp.int32), (padded // tm).astype(jnp.int32), counts, n_used)
    return pos3, sched, r_pad


def _in_weights(w):
    def regroup(cols):
        return cols.reshape(D, 2, HEADS, QK_DIM).transpose(0, 2, 1, 3).reshape(D, 2 * HEADS * QK_DIM)
    return jnp.concatenate([w[:, :OFF_Q], regroup(w[:, OFF_Q:OFF_K]), regroup(w[:, OFF_K:OFF_V]), w[:, OFF_V:]],
                           axis=1).astype(BF16)


def _rope_tables():
    t = jnp.arange(L)
    row = (t // GRID_W).astype(F32)
    col = (t % GRID_W).astype(F32)
    n_freq = QK_DIM // 4
    inv = ROPE_BASE ** (-jnp.arange(n_freq, dtype=F32) / n_freq)
    ar = row[:, None] * inv
    ac = col[:, None] * inv
    cos64 = jnp.concatenate([jnp.cos(ar), jnp.cos(ar), jnp.cos(ac), jnp.cos(ac)], axis=1)
    sin64 = jnp.concatenate([-jnp.sin(ar), jnp.sin(ar), -jnp.sin(ac), jnp.sin(ac)], axis=1)
    cos_t = jnp.concatenate([jnp.tile(cos64, (1, 2)), jnp.ones((TM_IN, 128), F32)], axis=0)
    sin_t = jnp.concatenate([jnp.tile(sin64, (1, 2)), jnp.zeros((TM_IN, 128), F32)], axis=0)
    return cos_t, sin_t


def _split_bf16(w):
    hi = w.astype(BF16)
    return hi, (w - hi.astype(F32)).astype(BF16)


def kernel(x, c, ctx, c_ctx, w_ada, b_ada, g_norm1, g_norm2, w_in, w_out, g_v, w_s, b_s, w_conv, g_q, g_k,
           lam_q1, lam_k1, lam_q2, lam_k2, g_sub, w_router, b_router, w_gate, w_up, w_down,
           ws_gate, ws_up, ws_down):
    src = (x.reshape(NL, D), ctx.reshape(NC, D))
    cc = jnp.concatenate([c, c_ctx[None, :], jnp.zeros((MOD_ROWS - B - 1, D), F32)], axis=0)
    mod = _ada(cc, w_ada, b_ada)
    cos_t, sin_t = _rope_tables()
    bd = jnp.asarray(np.kron(np.eye(RMS_BLOCK // QK_DIM, dtype=np.float32),
                             np.full((QK_DIM, QK_DIM), 1.0 / QK_DIM, np.float32)), BF16)

    for l in range(DEPTH):
        last = l == DEPTH - 1
        lam_init = 0.8 - 0.6 * math.exp(-0.3 * l)
        lam = (jnp.exp(jnp.sum(lam_q1[l] * lam_k1[l])) - jnp.exp(jnp.sum(lam_q2[l] * lam_k2[l])) + lam_init)
        bound = (QK_DIM * jnp.max(jnp.abs(g_q[l])) * jnp.max(jnp.abs(g_k[l]))
                 * (QK_DIM ** -0.5 * LOG2E) * ATT_BOUND_MARGIN)
        use_bound = (2.0 * bound < ATT_MAX_SHIFT_RANGE).astype(F32)
        lam = jnp.stack([lam, bound, use_bound]).astype(F32)
        mod3 = mod[l].reshape(MOD_ROWS, 1, 6 * D)
        w_in_bf = _in_weights(w_in[l])
        bias_t = jnp.repeat(b_s[l].T, A_GD, axis=1)
        wconv = jnp.concatenate([w_conv[l], jnp.zeros((5, B_WIDTH), F32)], axis=0)
        yab, q, kk, v = _in_mix(src[0], src[1], g_norm1[l], mod3, w_in_bf, cos_t, sin_t,
                                     g_v[l].reshape(1, A_WIDTH), w_s[l].astype(BF16), bias_t, wconv,
                                     jnp.tile(g_q[l], 8).reshape(1, 512), jnp.tile(g_k[l], 8).reshape(1, 512), bd)
        gsub = g_sub[l].reshape(1, V_DIM)
        coef = 1.0 - lam_init
        n_rows = NL if last else NR
        yc = _attention(lam, q, kk, v, gsub, coef, ctx_queries=False)
        yc_ctx = yc if last else _attention(lam, q, kk, v, gsub, coef, ctx_queries=True)
        wr_hi, wr_lo = _split_bf16(w_router[l].T)
        wo_bf = w_out[l].astype(BF16)
        ws_bf = (ws_gate[l].astype(BF16), ws_up[l].astype(BF16), ws_down[l].astype(BF16))
        xa, h2p, idx3, gate3, rank3, counts = _out_router(
            src[0], src[1], yab, yc, yc_ctx, wo_bf, g_norm2[l], mod3, wr_hi, wr_lo, b_router[l], n_rows)
        pos3, sched, r_pad = _route_positions(idx3, rank3, counts[:, 0], n_rows)
        x_sorted = _sc_dispatch(h2p.reshape(n_rows, 4, 128), pos3, r_pad)
        y_sorted = _experts(l, sched, x_sorted.reshape(r_pad * 4, 128), w_gate, w_up, w_down)
        yg = _sc_collect(y_sorted.reshape(r_pad, 4, 128), pos3)
        gates = gate3.transpose(0, 2, 1).reshape(n_rows, TOP_K)
        xa = _shared_residual(xa, h2p, yg.reshape(TOP_K, n_rows * 4, 128), gates, *ws_bf, mod3, n_rows)
        src = (xa, xa)
    return xa.reshape(B, L, D)
```
